```python
import math
import jax, jax.numpy as jnp
from jax import lax
import numpy as np

D_MODEL = 1024
BATCH = 32
SEQ = 2048
DEPTH = 2

N_MIXERS = 2
CONV_WIDTH = 3
HEAD_DIM = 64
N_Q_HEADS = D_MODEL // HEAD_DIM
N_KV_HEADS = 4
GROUP = N_Q_HEADS // N_KV_HEADS
WINDOW = 128
BLOCK = 128
D_FF = ((8 * D_MODEL // 3 + 255) // 256) * 256
QKV_WIDTH = (N_Q_HEADS + 2 * N_KV_HEADS) * HEAD_DIM
N_CONV_LAYERS = (DEPTH + 1) // 2
N_ATTN_LAYERS = DEPTH // 2
EPS = 1e-6

kernel_name = "hybrid_shortconv_swa_sink_alibi_swiglu"


def rmsnorm(x, gain):
    xf = x.astype(jnp.float32)
    r = lax.rsqrt(jnp.mean(xf * xf, axis=-1, keepdims=True) + EPS)
    return (xf * r).astype(x.dtype) * gain


def alibi_slopes():
    h = jnp.arange(1, N_Q_HEADS + 1, dtype=jnp.float32)
    return jnp.exp2(-8.0 * h / N_Q_HEADS)


def short_conv_mixer(h, w_in, conv_w, w_out):
    d = h.shape[-1]
    bcx = h @ w_in
    b_gate, c_gate, xv = jnp.split(bcx, 3, axis=-1)
    u = b_gate * xv
    y = lax.conv_general_dilated(
        u, conv_w[:, None, :].astype(u.dtype),
        window_strides=(1,), padding=[(CONV_WIDTH - 1, 0)],
        dimension_numbers=('NWC', 'WIO', 'NWC'), feature_group_count=d)
    return (c_gate * y) @ w_out


def swa_sink_attention(h, w_qkv, q_gain, k_gain, sinks, w_o):
    bsz, s, _ = h.shape
    nb = s // BLOCK
    qkv = h @ w_qkv
    q_end = N_Q_HEADS * HEAD_DIM
    k_end = q_end + N_KV_HEADS * HEAD_DIM
    q = qkv[..., :q_end].reshape(bsz, s, N_KV_HEADS, GROUP, HEAD_DIM)
    k = qkv[..., q_end:k_end].reshape(bsz, s, N_KV_HEADS, HEAD_DIM)
    v = qkv[..., k_end:].reshape(bsz, s, N_KV_HEADS, HEAD_DIM)
    q = rmsnorm(q, q_gain)
    k = rmsnorm(k, k_gain)

    qb = q.reshape(bsz, nb, BLOCK, N_KV_HEADS, GROUP, HEAD_DIM)

    def band(t):
        tp = jnp.pad(t, ((0, 0), (BLOCK, 0), (0, 0), (0, 0)))
        tb = tp.reshape(bsz, nb + 1, BLOCK, N_KV_HEADS, HEAD_DIM)
        return jnp.concatenate([tb[:, :-1], tb[:, 1:]], axis=2)

    kw = band(k)
    vw = band(v)

    scale = 1.0 / math.sqrt(HEAD_DIM)
    scores = jnp.einsum('bnqkgd,bnskd->bnkgqs', qb, kw).astype(jnp.float32) * scale

    qi = jnp.arange(BLOCK)[:, None]
    kj = jnp.arange(2 * BLOCK)[None, :]
    dist = qi + BLOCK - kj
    key_pos = (jnp.arange(nb) * BLOCK - BLOCK)[:, None, None] + kj[None]
    mask = (dist >= 0)[None] & (dist < WINDOW)[None] & (key_pos >= 0)

    slopes = alibi_slopes().reshape(N_KV_HEADS, GROUP)
    alibi = -slopes[:, :, None, None] * dist.astype(jnp.float32)[None, None]
    scores = scores + alibi[None, None]
    scores = jnp.where(mask[None, :, None, None], scores, -jnp.inf)

    sink = sinks.astype(jnp.float32).reshape(N_KV_HEADS, GROUP)
    sink_col = jnp.broadcast_to(sink[None, None, :, :, None, None],
                                scores.shape[:-1] + (1,))
    logits = jnp.concatenate([scores, sink_col], axis=-1)
    p = jax.nn.softmax(logits, axis=-1)[..., :-1]

    out = jnp.einsum('bnkgqs,bnskd->bnqkgd', p.astype(vw.dtype), vw)
    out = out.reshape(bsz, s, N_Q_HEADS * HEAD_DIM)
    return out @ w_o


def swiglu_ffn(h, w_gate_up, w_down):
    gu = h @ w_gate_up
    g, u = jnp.split(gu, 2, axis=-1)
    return (jax.nn.silu(g) * u) @ w_down


def _fwd_setup_inputs(seed: int = 0) -> dict:
    key = jax.random.key(seed)
    ks = jax.random.split(key, 14)
    f32 = jnp.float32
    d = D_MODEL

    def w(k, shape, fan_in):
        return jax.random.normal(k, shape, f32) * (fan_in ** -0.5)

    return {
        "x": jax.random.normal(ks[0], (BATCH, SEQ, d), f32),
        "conv_w_in": w(ks[1], (N_CONV_LAYERS, d, 3 * d), d),
        "conv_w": w(ks[2], (N_CONV_LAYERS, CONV_WIDTH, d), CONV_WIDTH),
        "conv_w_out": w(ks[3], (N_CONV_LAYERS, d, d), d),
        "attn_w_qkv": w(ks[4], (N_ATTN_LAYERS, d, QKV_WIDTH), d),
        "attn_q_gain": 1.0 + 0.05 * jax.random.normal(ks[5], (N_ATTN_LAYERS, HEAD_DIM), f32),
        "attn_k_gain": 1.0 + 0.05 * jax.random.normal(ks[6], (N_ATTN_LAYERS, HEAD_DIM), f32),
        "attn_sinks": 0.5 * jax.random.normal(ks[7], (N_ATTN_LAYERS, N_Q_HEADS), f32),
        "attn_w_o": w(ks[8], (N_ATTN_LAYERS, N_Q_HEADS * HEAD_DIM, d), N_Q_HEADS * HEAD_DIM),
        "norm_mixer": 1.0 + 0.05 * jax.random.normal(ks[9], (DEPTH, d), f32),
        "norm_ffn": 1.0 + 0.05 * jax.random.normal(ks[10], (DEPTH, d), f32),
        "ffn_w_gate_up": w(ks[11], (DEPTH, d, 2 * D_FF), d),
        "ffn_w_down": w(ks[12], (DEPTH, D_FF, d), D_FF),
    }


def _fwd_reference(x, conv_w_in, conv_w, conv_w_out, attn_w_qkv, attn_q_gain, attn_k_gain,
              attn_sinks, attn_w_o, norm_mixer, norm_ffn, ffn_w_gate_up, ffn_w_down):
    for i in range(DEPTH):
        h = rmsnorm(x, norm_mixer[i])
        j = i // N_MIXERS
        if i % N_MIXERS == 0:
            mix = short_conv_mixer(h, conv_w_in[j], conv_w[j], conv_w_out[j])
        else:
            mix = swa_sink_attention(h, attn_w_qkv[j], attn_q_gain[j], attn_k_gain[j],
                                     attn_sinks[j], attn_w_o[j])
        x = x + mix
        h = rmsnorm(x, norm_ffn[i])
        x = x + swiglu_ffn(h, ffn_w_gate_up[i], ffn_w_down[i])
    return x


import jax as _jax
import jax.numpy as _jnp

TWIN_FORMAT = 'train_step'
FWD_PARAMS = ['x', 'conv_w_in', 'conv_w', 'conv_w_out', 'attn_w_qkv', 'attn_q_gain', 'attn_k_gain', 'attn_sinks', 'attn_w_o', 'norm_mixer', 'norm_ffn', 'ffn_w_gate_up', 'ffn_w_down']
TWIN_WEIGHTS = ['conv_w_in', 'conv_w', 'conv_w_out', 'attn_w_qkv', 'attn_q_gain', 'attn_k_gain', 'attn_sinks', 'attn_w_o', 'norm_mixer', 'norm_ffn', 'ffn_w_gate_up', 'ffn_w_down']
TWIN_DIFF_INPUT = 'x'
TWIN_INPUTS = ['x', 'conv_w_in', 'conv_w', 'conv_w_out', 'attn_w_qkv', 'attn_q_gain', 'attn_k_gain', 'attn_sinks', 'attn_w_o', 'norm_mixer', 'norm_ffn', 'ffn_w_gate_up', 'ffn_w_down', 'loss_target', 'm_conv_w_in', 'm_conv_w', 'm_conv_w_out', 'm_attn_w_qkv', 'm_attn_q_gain', 'm_attn_k_gain', 'm_attn_sinks', 'm_attn_w_o', 'm_norm_mixer', 'm_norm_ffn', 'm_ffn_w_gate_up', 'm_ffn_w_down', 'v_conv_w_in', 'v_conv_w', 'v_conv_w_out', 'v_attn_w_qkv', 'v_attn_q_gain', 'v_attn_k_gain', 'v_attn_sinks', 'v_attn_w_o', 'v_norm_mixer', 'v_norm_ffn', 'v_ffn_w_gate_up', 'v_ffn_w_down']
TWIN_OUTPUTS = ['loss', 'grad_x', 'grad_conv_w_in', 'grad_conv_w', 'grad_conv_w_out', 'grad_attn_w_qkv', 'grad_attn_q_gain', 'grad_attn_k_gain', 'grad_attn_sinks', 'grad_attn_w_o', 'grad_norm_mixer', 'grad_norm_ffn', 'grad_ffn_w_gate_up', 'grad_ffn_w_down', 'delta_conv_w_in', 'delta_conv_w', 'delta_conv_w_out', 'delta_attn_w_qkv', 'delta_attn_q_gain', 'delta_attn_k_gain', 'delta_attn_sinks', 'delta_attn_w_o', 'delta_norm_mixer', 'delta_norm_ffn', 'delta_ffn_w_gate_up', 'delta_ffn_w_down', 'new_m_conv_w_in', 'new_m_conv_w', 'new_m_conv_w_out', 'new_m_attn_w_qkv', 'new_m_attn_q_gain', 'new_m_attn_k_gain', 'new_m_attn_sinks', 'new_m_attn_w_o', 'new_m_norm_mixer', 'new_m_norm_ffn', 'new_m_ffn_w_gate_up', 'new_m_ffn_w_down', 'new_v_conv_w_in', 'new_v_conv_w', 'new_v_conv_w_out', 'new_v_attn_w_qkv', 'new_v_attn_q_gain', 'new_v_attn_k_gain', 'new_v_attn_sinks', 'new_v_attn_w_o', 'new_v_norm_mixer', 'new_v_norm_ffn', 'new_v_ffn_w_gate_up', 'new_v_ffn_w_down']
TWIN_LEAF_KINDS = {'loss': 'loss', 'grad_x': 'grad_x', 'grad_conv_w_in': 'grad_w', 'grad_conv_w': 'grad_w', 'grad_conv_w_out': 'grad_w', 'grad_attn_w_qkv': 'grad_w', 'grad_attn_q_gain': 'grad_w', 'grad_attn_k_gain': 'grad_w', 'grad_attn_sinks': 'grad_w', 'grad_attn_w_o': 'grad_w', 'grad_norm_mixer': 'grad_w', 'grad_norm_ffn': 'grad_w', 'grad_ffn_w_gate_up': 'grad_w', 'grad_ffn_w_down': 'grad_w', 'delta_conv_w_in': 'delta_w', 'delta_conv_w': 'delta_w', 'delta_conv_w_out': 'delta_w', 'delta_attn_w_qkv': 'delta_w', 'delta_attn_q_gain': 'delta_w', 'delta_attn_k_gain': 'delta_w', 'delta_attn_sinks': 'delta_w', 'delta_attn_w_o': 'delta_w', 'delta_norm_mixer': 'delta_w', 'delta_norm_ffn': 'delta_w', 'delta_ffn_w_gate_up': 'delta_w', 'delta_ffn_w_down': 'delta_w', 'new_m_conv_w_in': 'new_m', 'new_m_conv_w': 'new_m', 'new_m_conv_w_out': 'new_m', 'new_m_attn_w_qkv': 'new_m', 'new_m_attn_q_gain': 'new_m', 'new_m_attn_k_gain': 'new_m', 'new_m_attn_sinks': 'new_m', 'new_m_attn_w_o': 'new_m', 'new_m_norm_mixer': 'new_m', 'new_m_norm_ffn': 'new_m', 'new_m_ffn_w_gate_up': 'new_m', 'new_m_ffn_w_down': 'new_m', 'new_v_conv_w_in': 'new_v', 'new_v_conv_w': 'new_v', 'new_v_conv_w_out': 'new_v', 'new_v_attn_w_qkv': 'new_v', 'new_v_attn_q_gain': 'new_v', 'new_v_attn_k_gain': 'new_v', 'new_v_attn_sinks': 'new_v', 'new_v_attn_w_o': 'new_v', 'new_v_norm_mixer': 'new_v', 'new_v_norm_ffn': 'new_v', 'new_v_ffn_w_gate_up': 'new_v', 'new_v_ffn_w_down': 'new_v'}


def _forward(args):
    return _fwd_reference(*[args[k] for k in FWD_PARAMS])


def _output_shape():
    out = _jax.eval_shape(lambda: _forward(_fwd_setup_inputs(0)))
    return out.shape, out.dtype

N_MICROBATCH = 1
ADAM_LR = 0.001
ADAM_B1 = 0.9
ADAM_B2 = 0.999
ADAM_EPS = 1e-08
ADAM_WD = 0.01
ADAM_STEP = 10
PER_EXAMPLE_BATCH_AXIS = {'x': 0, 'loss_target': 0}
SHARED_INPUTS = []
_WEIGHT_DTYPES = {'conv_w_in': _jnp.float32, 'conv_w': _jnp.float32, 'conv_w_out': _jnp.float32, 'attn_w_qkv': _jnp.float32, 'attn_q_gain': _jnp.float32, 'attn_k_gain': _jnp.float32, 'attn_sinks': _jnp.float32, 'attn_w_o': _jnp.float32, 'norm_mixer': _jnp.float32, 'norm_ffn': _jnp.float32, 'ffn_w_gate_up': _jnp.float32, 'ffn_w_down': _jnp.float32}
MOMENT_SCALE = {'conv_w_in': 1.625727e+00, 'conv_w': 3.636198e+01, 'conv_w_out': 1.584922e+00, 'attn_w_qkv': 4.011487e-01, 'attn_q_gain': 3.839669e+01, 'attn_k_gain': 3.884174e+01, 'attn_sinks': 6.582845e+01, 'attn_w_o': 3.325360e-01, 'norm_mixer': 1.369217e+02, 'norm_ffn': 4.918009e+01, 'ffn_w_gate_up': 3.000672e-01, 'ffn_w_down': 5.134453e-01}


def _to_microbatches(a, axis):
    t = _jnp.moveaxis(a, axis, 0)
    t = t.reshape((N_MICROBATCH, t.shape[0] // N_MICROBATCH) + t.shape[1:])
    return _jnp.moveaxis(t, 1, axis + 1)


def setup_inputs(seed: int = 0) -> dict:
    inp = _fwd_setup_inputs(seed)
    key = _jax.random.fold_in(_jax.random.key(seed), 7919)
    shape, _ = _output_shape()
    out = dict(inp)
    out["loss_target"] = _jax.random.normal(_jax.random.fold_in(key, 0), shape, _jnp.float32)
    for i, name in enumerate(TWIN_WEIGHTS):
        w = inp[name].astype(_jnp.float32)
        if MOMENT_SCALE is None:
            s = _jnp.sqrt(_jnp.mean(_jnp.square(w)) + 1e-30)
        else:
            s = MOMENT_SCALE[name]
        km, kv = _jax.random.split(_jax.random.fold_in(key, i + 1))
        out[name] = w
        out["m_" + name] = s * _jax.random.normal(km, w.shape, _jnp.float32)
        out["v_" + name] = (s * s) * _jax.random.uniform(kv, w.shape, _jnp.float32, 0.5, 1.5)
    if N_MICROBATCH > 1:
        for name, axis in PER_EXAMPLE_BATCH_AXIS.items():
            out[name] = _to_microbatches(out[name], axis)
    return {'x': out['x'], 'conv_w_in': out['conv_w_in'], 'conv_w': out['conv_w'], 'conv_w_out': out['conv_w_out'], 'attn_w_qkv': out['attn_w_qkv'], 'attn_q_gain': out['attn_q_gain'], 'attn_k_gain': out['attn_k_gain'], 'attn_sinks': out['attn_sinks'], 'attn_w_o': out['attn_w_o'], 'norm_mixer': out['norm_mixer'], 'norm_ffn': out['norm_ffn'], 'ffn_w_gate_up': out['ffn_w_gate_up'], 'ffn_w_down': out['ffn_w_down'], 'loss_target': out['loss_target'], 'm_conv_w_in': out['m_conv_w_in'], 'm_conv_w': out['m_conv_w'], 'm_conv_w_out': out['m_conv_w_out'], 'm_attn_w_qkv': out['m_attn_w_qkv'], 'm_attn_q_gain': out['m_attn_q_gain'], 'm_attn_k_gain': out['m_attn_k_gain'], 'm_attn_sinks': out['m_attn_sinks'], 'm_attn_w_o': out['m_attn_w_o'], 'm_norm_mixer': out['m_norm_mixer'], 'm_norm_ffn': out['m_norm_ffn'], 'm_ffn_w_gate_up': out['m_ffn_w_gate_up'], 'm_ffn_w_down': out['m_ffn_w_down'], 'v_conv_w_in': out['v_conv_w_in'], 'v_conv_w': out['v_conv_w'], 'v_conv_w_out': out['v_conv_w_out'], 'v_attn_w_qkv': out['v_attn_w_qkv'], 'v_attn_q_gain': out['v_attn_q_gain'], 'v_attn_k_gain': out['v_attn_k_gain'], 'v_attn_sinks': out['v_attn_sinks'], 'v_attn_w_o': out['v_attn_w_o'], 'v_norm_mixer': out['v_norm_mixer'], 'v_norm_ffn': out['v_norm_ffn'], 'v_ffn_w_gate_up': out['v_ffn_w_gate_up'], 'v_ffn_w_down': out['v_ffn_w_down']}


def _loss(weights, diff, rest, loss_target):
    with _jax.named_scope("forward"):
        args = {**rest, TWIN_DIFF_INPUT: diff, **{k: w.astype(_WEIGHT_DTYPES[k]) for k, w in weights.items()}}
        y = _forward(args)
    with _jax.named_scope("loss_head"):
        err = _jnp.square(y.astype(_jnp.float32) - loss_target)
        return 0.5 * _jnp.sum(_jnp.mean(err, axis=-1)) if err.ndim else 0.5 * err


def _adamw(w, g, m, v):
    m = ADAM_B1 * m + (1.0 - ADAM_B1) * g
    v = ADAM_B2 * v + (1.0 - ADAM_B2) * _jnp.square(g)
    m_hat = m / (1.0 - ADAM_B1 ** ADAM_STEP)
    v_hat = v / (1.0 - ADAM_B2 ** ADAM_STEP)
    delta = -ADAM_LR * (m_hat / (_jnp.sqrt(v_hat) + ADAM_EPS) + ADAM_WD * w)
    return delta, m, v


def reference(x, conv_w_in, conv_w, conv_w_out, attn_w_qkv, attn_q_gain, attn_k_gain, attn_sinks, attn_w_o, norm_mixer, norm_ffn, ffn_w_gate_up, ffn_w_down, loss_target, m_conv_w_in, m_conv_w, m_conv_w_out, m_attn_w_qkv, m_attn_q_gain, m_attn_k_gain, m_attn_sinks, m_attn_w_o, m_norm_mixer, m_norm_ffn, m_ffn_w_gate_up, m_ffn_w_down, v_conv_w_in, v_conv_w, v_conv_w_out, v_attn_w_qkv, v_attn_q_gain, v_attn_k_gain, v_attn_sinks, v_attn_w_o, v_norm_mixer, v_norm_ffn, v_ffn_w_gate_up, v_ffn_w_down):
    given = dict(x=x, conv_w_in=conv_w_in, conv_w=conv_w, conv_w_out=conv_w_out, attn_w_qkv=attn_w_qkv, attn_q_gain=attn_q_gain, attn_k_gain=attn_k_gain, attn_sinks=attn_sinks, attn_w_o=attn_w_o, norm_mixer=norm_mixer, norm_ffn=norm_ffn, ffn_w_gate_up=ffn_w_gate_up, ffn_w_down=ffn_w_down, loss_target=loss_target, m_conv_w_in=m_conv_w_in, m_conv_w=m_conv_w, m_conv_w_out=m_conv_w_out, m_attn_w_qkv=m_attn_w_qkv, m_attn_q_gain=m_attn_q_gain, m_attn_k_gain=m_attn_k_gain, m_attn_sinks=m_attn_sinks, m_attn_w_o=m_attn_w_o, m_norm_mixer=m_norm_mixer, m_norm_ffn=m_norm_ffn, m_ffn_w_gate_up=m_ffn_w_gate_up, m_ffn_w_down=m_ffn_w_down, v_conv_w_in=v_conv_w_in, v_conv_w=v_conv_w, v_conv_w_out=v_conv_w_out, v_attn_w_qkv=v_attn_w_qkv, v_attn_q_gain=v_attn_q_gain, v_attn_k_gain=v_attn_k_gain, v_attn_sinks=v_attn_sinks, v_attn_w_o=v_attn_w_o, v_norm_mixer=v_norm_mixer, v_norm_ffn=v_norm_ffn, v_ffn_w_gate_up=v_ffn_w_gate_up, v_ffn_w_down=v_ffn_w_down)
    weights = {n: given[n] for n in TWIN_WEIGHTS}
    shared = {n: given[n] for n in SHARED_INPUTS}
    per_example = {n: given[n] for n in ['x']}
    grad_fn = _jax.value_and_grad(_loss, argnums=(0, 1))

    def one_microbatch(ex, loss_target):
        ex = dict(ex)
        diff = ex.pop(TWIN_DIFF_INPUT)
        return grad_fn(weights, diff, {**shared, **ex}, loss_target)

    if N_MICROBATCH == 1:
        loss, (grad_w, grad_x) = one_microbatch(per_example, given["loss_target"])
    else:
        def body(carry, xs):
            loss_sum, grad_sum = carry
            l_k, (gw_k, gx_k) = one_microbatch(xs[0], xs[1])
            with _jax.named_scope("update"):
                return (loss_sum + l_k, _jax.tree.map(_jnp.add, grad_sum, gw_k)), gx_k

        init = (_jnp.zeros((), _jnp.float32), _jax.tree.map(_jnp.zeros_like, weights))
        (loss, grad_w), grad_x = _jax.lax.scan(body, init, (per_example, given["loss_target"]))
    with _jax.named_scope("update"):
        delta_w, new_m, new_v = {}, {}, {}
        for n in TWIN_WEIGHTS:
            delta_w[n], new_m[n], new_v[n] = _adamw(weights[n], grad_w[n], given["m_" + n], given["v_" + n])
    return (loss, grad_x, *[grad_w[n] for n in TWIN_WEIGHTS], *[delta_w[n] for n in TWIN_WEIGHTS],
            *[new_m[n] for n in TWIN_WEIGHTS], *[new_v[n] for n in TWIN_WEIGHTS])
```

```python
import functools
import math

import jax
import jax.numpy as jnp
from jax import lax
from jax.experimental import pallas as pl
from jax.experimental.pallas import tpu as pltpu

F32 = jnp.float32
BF16 = jnp.bfloat16

EPS = 1e-6
HEAD_DIM = 64
N_KV_HEADS = 4
BLOCK = 128
LANES = 128
N_DEV = 8
N_CHIP = 4
NEG = -1e30

ADAM_LR = 0.001
ADAM_B1 = 0.9
ADAM_B2 = 0.999
ADAM_EPS = 1e-08
ADAM_WD = 0.01
ADAM_STEP = 10

V7X_VMEM_BYTES = 64 * 1024 * 1024
VMEM_LIMIT = V7X_VMEM_BYTES - 8 * 1024 * 1024
MESH = pl.DeviceIdType.MESH

_NT = (((1,), (1,)), ((), ()))
_TN = (((0,), (0,)), ((), ()))


def _params(n_grid):
    return pltpu.CompilerParams(dimension_semantics=("arbitrary",) * n_grid, vmem_limit_bytes=VMEM_LIMIT)


def _resident(shape):
    nd = len(shape)
    return pl.BlockSpec(shape, lambda *_: (0,) * nd, pipeline_mode=pl.Buffered(1))


def _rms(x):
    return lax.rsqrt(jnp.mean(x * x, axis=-1, keepdims=True) + EPS)


def _rms_bwd(x, r, gain, dh):
    xn = x * r
    dxn = dh * gain
    dx = r * (dxn - xn * jnp.mean(dxn * xn, axis=-1, keepdims=True))
    return dx, jnp.sum(dh * xn, axis=0, keepdims=True)


def _dot(a, b):
    return jnp.dot(a, b, preferred_element_type=F32)


def _dot_nt(a, b):
    return lax.dot_general(a, b, _NT, preferred_element_type=F32)


def _dot_tn(a, b):
    return lax.dot_general(a, b, _TN, preferred_element_type=F32)


def _shift_down(u, prev8, row, n):
    out = pltpu.roll(u, n, 0)
    for k in range(n):
        out = jnp.where(row == k, prev8[8 - n + k : 8 - n + k + 1, :], out)
    return out


def _shift_up(u, next8, row, n, tm):
    out = pltpu.roll(u, tm - n, 0)
    for k in range(n):
        out = jnp.where(row == tm - n + k, next8[k : k + 1, :], out)
    return out


def conv_fwd(x, gain, w_in, cw, w_out, *, seq, tm):
    T, D = x.shape
    tps = seq // tm

    def body(x_ref, g_ref, win_ref, cw_ref, wout_ref, x1_ref, bcx_ref, carry_ref):
        i = pl.program_id(0)

        @pl.when(i % tps == 0)
        def _():
            carry_ref[...] = jnp.zeros_like(carry_ref)

        xt = x_ref[...]
        h = ((xt * _rms(xt)) * g_ref[...]).astype(BF16)
        bcx = _dot(h, win_ref[...])
        bcx_ref[...] = bcx
        b, c, xv = bcx[:, :D], bcx[:, D : 2 * D], bcx[:, 2 * D :]
        u = b * xv
        row = lax.broadcasted_iota(jnp.int32, u.shape, 0)
        prev = carry_ref[...]
        u1 = _shift_down(u, prev, row, 1)
        u2 = _shift_down(u, prev, row, 2)
        carry_ref[...] = u[tm - 8 :, :]
        cwv = cw_ref[...]
        y = cwv[0:1, :] * u2 + cwv[1:2, :] * u1 + cwv[2:3, :] * u
        z = (c * y).astype(BF16)
        x1_ref[...] = xt + _dot(z, wout_ref[...])

    return pl.pallas_call(
        body,
        name="conv_fwd",
        grid=(T // tm,),
        in_specs=[
            pl.BlockSpec((tm, D), lambda i: (i, 0)),
            _resident((1, D)),
            _resident((D, 3 * D)),
            _resident((3, D)),
            _resident((D, D)),
        ],
        out_specs=[pl.BlockSpec((tm, D), lambda i: (i, 0)), pl.BlockSpec((tm, 3 * D), lambda i: (i, 0))],
        out_shape=[jax.ShapeDtypeStruct((T, D), F32), jax.ShapeDtypeStruct((T, 3 * D), F32)],
        scratch_shapes=[pltpu.VMEM((8, D), F32)],
        compiler_params=_params(1),
    )(x, gain, w_in, cw, w_out)


def conv_bwd(dx1, x, gain, bcx, cw, w_in, w_out, *, seq, tm):
    T, D = x.shape
    n = T // tm
    tps = seq // tm
    r8 = tm // 8

    def body(d_ref, x_ref, g_ref, bcx_ref, halo_ref, cw_ref, win_ref, wout_ref,
             gx_ref, dbcx_ref, h_ref, z_ref, d16_ref, dcw_ref, dg_ref, carry_ref):
        i = pl.program_id(0)
        t = n - 1 - i

        @pl.when(i == 0)
        def _():
            dcw_ref[...] = jnp.zeros_like(dcw_ref)
            dg_ref[...] = jnp.zeros_like(dg_ref)

        @pl.when(t % tps == tps - 1)
        def _():
            carry_ref[...] = jnp.zeros_like(carry_ref)

        d = d_ref[...]
        d16 = d.astype(BF16)
        d16_ref[...] = d16
        dz = _dot_nt(d16, wout_ref[...])
        bcx = bcx_ref[...]
        b, c, xv = bcx[:, :D], bcx[:, D : 2 * D], bcx[:, 2 * D :]
        u = b * xv
        halo = halo_ref[...]
        hu = jnp.where(t % tps == 0, 0.0, halo[:, :D] * halo[:, 2 * D :])
        row = lax.broadcasted_iota(jnp.int32, u.shape, 0)
        u1 = _shift_down(u, hu, row, 1)
        u2 = _shift_down(u, hu, row, 2)
        cwv = cw_ref[...]
        y = cwv[0:1, :] * u2 + cwv[1:2, :] * u1 + cwv[2:3, :] * u
        z_ref[...] = (c * y).astype(BF16)
        dc = dz * y
        dy = dz * c
        dcw_ref[0:1, :] += jnp.sum(dy * u2, axis=0, keepdims=True)
        dcw_ref[1:2, :] += jnp.sum(dy * u1, axis=0, keepdims=True)
        dcw_ref[2:3, :] += jnp.sum(dy * u, axis=0, keepdims=True)
        nxt = carry_ref[...]
        dy1 = _shift_up(dy, nxt, row, 1, tm)
        dy2 = _shift_up(dy, nxt, row, 2, tm)
        carry_ref[...] = dy[0:8, :]
        du = cwv[2:3, :] * dy + cwv[1:2, :] * dy1 + cwv[0:1, :] * dy2
        dbcx_ref[:, :D] = (du * xv).astype(BF16)
        dbcx_ref[:, D : 2 * D] = dc.astype(BF16)
        dbcx_ref[:, 2 * D :] = (du * b).astype(BF16)
        dh = _dot_nt(dbcx_ref[...], win_ref[...])
        xt = x_ref[...]
        r = _rms(xt)
        gn = g_ref[...]
        h_ref[...] = ((xt * r) * gn).astype(BF16)
        dx, dgn = _rms_bwd(xt, r, gn, dh)
        dg_ref[0:1, :] += dgn
        gx_ref[...] = d + dx

    rev = lambda i: (n - 1 - i, 0)
    return pl.pallas_call(
        body,
        name="conv_bwd",
        grid=(n,),
        in_specs=[
            pl.BlockSpec((tm, D), rev),
            pl.BlockSpec((tm, D), rev),
            _resident((1, D)),
            pl.BlockSpec((tm, 3 * D), rev),
            pl.BlockSpec((8, 3 * D), lambda i: (jnp.maximum((n - 1 - i) * r8 - 1, 0), 0)),
            _resident((3, D)),
            _resident((D, 3 * D)),
            _resident((D, D)),
        ],
        out_specs=[
            pl.BlockSpec((tm, D), rev),
            pl.BlockSpec((tm, 3 * D), rev),
            pl.BlockSpec((tm, D), rev),
            pl.BlockSpec((tm, D), rev),
            pl.BlockSpec((tm, D), rev),
            pl.BlockSpec((8, D), lambda i: (0, 0)),
            pl.BlockSpec((8, D), lambda i: (0, 0)),
        ],
        out_shape=[
            jax.ShapeDtypeStruct((T, D), F32),
            jax.ShapeDtypeStruct((T, 3 * D), BF16),
            jax.ShapeDtypeStruct((T, D), BF16),
            jax.ShapeDtypeStruct((T, D), BF16),
            jax.ShapeDtypeStruct((T, D), BF16),
            jax.ShapeDtypeStruct((8, D), F32),
            jax.ShapeDtypeStruct((8, D), F32),
        ],
        scratch_shapes=[pltpu.VMEM((8, D), F32)],
        compiler_params=_params(1),
    )(dx1, x, gain, bcx, bcx, cw, w_in, w_out)


def _sigmoid(g):
    return 1.0 / (1.0 + jnp.exp(-g))


def ffn_fwd(x, gain, w_gu, w_d, *, tm):
    T, D = x.shape
    _, _, C = w_gu.shape
    half = N_DEV // 2

    def body(x_ref, g_ref, wgu_ref, wd_ref, xo_ref, gu_ref):
        xt = x_ref[...]
        h = ((xt * _rms(xt)) * g_ref[...]).astype(BF16)
        acc = xt
        for j in range(half):
            g = _dot(h, wgu_ref[j])
            u = _dot(h, wgu_ref[j + half])
            gu_ref[j] = g
            gu_ref[j + half] = u
            a = ((g * _sigmoid(g)) * u).astype(BF16)
            acc = acc + _dot(a, wd_ref[j * C : (j + 1) * C, :])
        xo_ref[...] = acc

    return pl.pallas_call(
        body,
        name="ffn_fwd",
        grid=(T // tm,),
        in_specs=[
            pl.BlockSpec((tm, D), lambda i: (i, 0)),
            _resident((1, D)),
            _resident((N_DEV, D, C)),
            _resident((half * C, D)),
        ],
        out_specs=[pl.BlockSpec((tm, D), lambda i: (i, 0)), pl.BlockSpec((N_DEV, tm, C), lambda i: (0, i, 0))],
        out_shape=[jax.ShapeDtypeStruct((T, D), F32), jax.ShapeDtypeStruct((N_DEV, T, C), F32)],
        compiler_params=_params(1),
    )(x, gain, w_gu, w_d)


def ffn_bwd(dxo, x, gain, gu, w_gu, w_d, *, tm):
    T, D = x.shape
    _, _, C = w_gu.shape
    half = N_DEV // 2

    def body(d_ref, x_ref, g_ref, gu_ref, wgu_ref, wd_ref, dx_ref, a_ref, dgu_ref, h_ref, d16_ref, dg_ref):
        @pl.when(pl.program_id(0) == 0)
        def _():
            dg_ref[...] = jnp.zeros_like(dg_ref)

        d = d_ref[...]
        d16 = d.astype(BF16)
        d16_ref[...] = d16
        dh = jnp.zeros((tm, D), F32)
        for j in range(half):
            g = gu_ref[j]
            u = gu_ref[j + half]
            da = _dot_nt(d16, wd_ref[j * C : (j + 1) * C, :])
            s = _sigmoid(g)
            sg = g * s
            a_ref[j] = (sg * u).astype(BF16)
            dg16 = (da * u * (s + sg * (1.0 - s))).astype(BF16)
            du16 = (da * sg).astype(BF16)
            dgu_ref[j] = dg16
            dgu_ref[j + half] = du16
            dh = dh + _dot_nt(dg16, wgu_ref[j]) + _dot_nt(du16, wgu_ref[j + half])
        xt = x_ref[...]
        r = _rms(xt)
        gn = g_ref[...]
        h_ref[...] = ((xt * r) * gn).astype(BF16)
        dx, dgn = _rms_bwd(xt, r, gn, dh)
        dg_ref[0:1, :] += dgn
        dx_ref[...] = d + dx

    return pl.pallas_call(
        body,
        name="ffn_bwd",
        grid=(T // tm,),
        in_specs=[
            pl.BlockSpec((tm, D), lambda i: (i, 0)),
            pl.BlockSpec((tm, D), lambda i: (i, 0)),
            _resident((1, D)),
            pl.BlockSpec((N_DEV, tm, C), lambda i: (0, i, 0)),
            _resident((N_DEV, D, C)),
            _resident((half * C, D)),
        ],
        out_specs=[
            pl.BlockSpec((tm, D), lambda i: (i, 0)),
            pl.BlockSpec((half, tm, C), lambda i: (0, i, 0)),
            pl.BlockSpec((N_DEV, tm, C), lambda i: (0, i, 0)),
            pl.BlockSpec((tm, D), lambda i: (i, 0)),
            pl.BlockSpec((tm, D), lambda i: (i, 0)),
            pl.BlockSpec((8, D), lambda i: (0, 0)),
        ],
        out_shape=[
            jax.ShapeDtypeStruct((T, D), F32),
            jax.ShapeDtypeStruct((half, T, C), BF16),
            jax.ShapeDtypeStruct((N_DEV, T, C), BF16),
            jax.ShapeDtypeStruct((T, D), BF16),
            jax.ShapeDtypeStruct((T, D), BF16),
            jax.ShapeDtypeStruct((8, D), F32),
        ],
        compiler_params=_params(1),
    )(dxo, x, gain, gu, w_gu, w_d)


def norm_matmul(x, gain, w, *, tm):
    T, D = x.shape
    N = w.shape[1]

    def body(x_ref, g_ref, w_ref, o_ref):
        xt = x_ref[...]
        h = ((xt * _rms(xt)) * g_ref[...]).astype(BF16)
        o_ref[...] = _dot(h, w_ref[...])

    return pl.pallas_call(
        body,
        name="norm_matmul",
        grid=(T // tm,),
        in_specs=[pl.BlockSpec((tm, D), lambda i: (i, 0)), _resident((1, D)), _resident((D, N))],
        out_specs=pl.BlockSpec((tm, N), lambda i: (i, 0)),
        out_shape=jax.ShapeDtypeStruct((T, N), F32),
        compiler_params=_params(1),
    )(x, gain, w)


def matmul_residual(a, w, res, *, tm):
    T, K = a.shape
    N = w.shape[1]

    def body(a_ref, w_ref, r_ref, o_ref):
        o_ref[...] = r_ref[...] + _dot(a_ref[...], w_ref[...])

    return pl.pallas_call(
        body,
        name="matmul_residual",
        grid=(T // tm,),
        in_specs=[pl.BlockSpec((tm, K), lambda i: (i, 0)), _resident((K, N)), pl.BlockSpec((tm, N), lambda i: (i, 0))],
        out_specs=pl.BlockSpec((tm, N), lambda i: (i, 0)),
        out_shape=jax.ShapeDtypeStruct((T, N), F32),
        compiler_params=_params(1),
    )(a, w, res)


def matmul_nt_cast(d, w, *, tm):
    T, N = d.shape
    K = w.shape[0]

    def body(d_ref, w_ref, o_ref, d16_ref):
        d16 = d_ref[...].astype(BF16)
        d16_ref[...] = d16
        o_ref[...] = _dot_nt(d16, w_ref[...]).astype(BF16)

    return pl.pallas_call(
        body,
        name="matmul_nt_cast",
        grid=(T // tm,),
        in_specs=[pl.BlockSpec((tm, N), lambda i: (i, 0)), _resident((K, N))],
        out_specs=[pl.BlockSpec((tm, K), lambda i: (i, 0)), pl.BlockSpec((tm, N), lambda i: (i, 0))],
        out_shape=[jax.ShapeDtypeStruct((T, K), BF16), jax.ShapeDtypeStruct((T, N), BF16)],
        compiler_params=_params(1),
    )(d, w)


def wgrad(a, b, *, name, a_chunks, b_chunks, b_cols, tk, out_dtype):
    T, K = a.shape[-2:]
    J = 1
    if a_chunks:
        J = a.shape[0]
        a_spec = pl.BlockSpec((None, tk, K), lambda j, k: (j, k, 0))
    else:
        a_spec = pl.BlockSpec((tk, K), lambda j, k: (k, 0))
    if b_chunks:
        J, _, N = b.shape
        b_spec = pl.BlockSpec((None, tk, N), lambda j, k: (j, k, 0))
    elif b_cols:
        N = b_cols
        J = b.shape[1] // N
        b_spec = pl.BlockSpec((tk, N), lambda j, k: (k, j))
    else:
        N = b.shape[1]
        b_spec = pl.BlockSpec((tk, N), lambda j, k: (k, 0))
    nk = T // tk

    def body(a_ref, b_ref, o_ref, acc_ref):
        k = pl.program_id(1)

        @pl.when(k == 0)
        def _():
            acc_ref[...] = jnp.zeros_like(acc_ref)

        acc_ref[...] += _dot_tn(a_ref[...], b_ref[...])

        @pl.when(k == nk - 1)
        def _():
            o_ref[...] = acc_ref[...].astype(out_dtype)

    return pl.pallas_call(
        body,
        name=name,
        grid=(J, nk),
        in_specs=[a_spec, b_spec],
        out_specs=pl.BlockSpec((None, K, N), lambda j, k: (j, 0, 0)),
        out_shape=jax.ShapeDtypeStruct((J, K, N), out_dtype),
        scratch_shapes=[pltpu.VMEM((K, N), F32)],
        compiler_params=_params(2),
    )(a, b)


def _seg(x, lo):
    s_lo = jnp.sum(jnp.where(lo, x, 0.0), axis=-1, keepdims=True)
    s_hi = jnp.sum(jnp.where(lo, 0.0, x), axis=-1, keepdims=True)
    return jnp.where(lo, s_lo, s_hi)


def _head_norm(x, gain, lo):
    r = lax.rsqrt(_seg(x * x, lo) * (1.0 / HEAD_DIM) + EPS)
    return (x * r) * gain, r


def _head_norm_bwd(x, r, gain, dy, lo):
    xn = x * r
    dxn = dy * gain
    dx = r * (dxn - xn * (_seg(dxn * xn, lo) * (1.0 / HEAD_DIM)))
    return dx, jnp.sum(dy * xn, axis=0, keepdims=True)


def _attn_tables(sinks, n_q_heads):
    G = n_q_heads // N_KV_HEADS
    h = jnp.arange(1, n_q_heads + 1, dtype=F32)
    slopes = jnp.exp2(-8.0 * h / n_q_heads).reshape(N_KV_HEADS, G)
    qi = jnp.arange(BLOCK)[:, None]
    kj = jnp.arange(2 * BLOCK)[None, :]
    dist = qi + BLOCK - kj
    ok = (dist >= 0) & (dist < BLOCK)
    bias = jnp.where(ok[None, None], -slopes[:, :, None, None] * dist.astype(F32)[None, None], NEG)
    sink = jnp.broadcast_to(sinks.astype(F32).reshape(N_KV_HEADS, G, 1, 1), (N_KV_HEADS, G, BLOCK, 1))
    return bias.reshape(N_KV_HEADS, G * BLOCK, 2 * BLOCK), sink.reshape(N_KV_HEADS, G * BLOCK, 1)


def _attn_specs(D, nb):
    kvw = N_KV_HEADS * HEAD_DIM
    cur = lambda b, i: (b * nb + i, 0)
    kcol = D // kvw
    return [
        pl.BlockSpec((BLOCK, D), cur),
        pl.BlockSpec((BLOCK, kvw), lambda b, i: (b * nb + i, kcol)),
        pl.BlockSpec((BLOCK, kvw), lambda b, i: (jnp.maximum(b * nb + i - 1, 0), kcol)),
        pl.BlockSpec((BLOCK, kvw), lambda b, i: (b * nb + i, kcol + 1)),
        pl.BlockSpec((BLOCK, kvw), lambda b, i: (jnp.maximum(b * nb + i - 1, 0), kcol + 1)),
    ]


def _attn_scores(qn_tiles, kn16, kh, G, half, bias_ref, sink_ref, first):
    qs = []
    for g in range(G):
        hq = kh * G + g
        q = qn_tiles[hq // 2] * (1.0 / math.sqrt(HEAD_DIM))
        if hq % 2 != half:
            q = pltpu.roll(q, HEAD_DIM, 1)
        qs.append(q)
    qs16 = jnp.concatenate(qs, axis=0).astype(BF16)
    s = _dot_nt(qs16, kn16) + bias_ref[kh]
    col = lax.broadcasted_iota(jnp.int32, s.shape, 1)
    s = jnp.where(jnp.logical_and(first, col < BLOCK), NEG, s)
    sink = sink_ref[kh]
    m = jnp.maximum(jnp.max(s, axis=-1, keepdims=True), sink)
    e = jnp.exp(s - m)
    es = jnp.exp(sink - m)
    inv = 1.0 / (jnp.sum(e, axis=-1, keepdims=True) + es)
    return qs16, e * inv, es * inv


def attn_fwd(qkv, qg, kg, bias, sinkcol, *, seq, n_seq):
    T = qkv.shape[0]
    D = qkv.shape[1] - 2 * N_KV_HEADS * HEAD_DIM
    nb = seq // BLOCK
    G = D // HEAD_DIM // N_KV_HEADS
    R = G * BLOCK

    def body(q_ref, kc_ref, kp_ref, vc_ref, vp_ref, qg_ref, kg_ref, bias_ref, sink_ref, o_ref):
        first = pl.program_id(1) == 0
        lane = lax.broadcasted_iota(jnp.int32, (1, LANES), 1)
        lo = lane < HEAD_DIM
        qn_tiles = [_head_norm(q_ref[:, t * LANES : (t + 1) * LANES], qg_ref[...], lo)[0] for t in range(D // LANES)]
        out_tiles = [jnp.zeros((BLOCK, LANES), F32) for _ in range(D // LANES)]
        for kh in range(N_KV_HEADS):
            kt, half = kh // 2, kh % 2
            sl = slice(kt * LANES, (kt + 1) * LANES)
            mine = lo if half == 0 else jnp.logical_not(lo)
            k = jnp.concatenate([kp_ref[:, sl], kc_ref[:, sl]], axis=0)
            kn16 = jnp.where(mine, _head_norm(k, kg_ref[...], lo)[0], 0.0).astype(BF16)
            v = jnp.concatenate([vp_ref[:, sl], vc_ref[:, sl]], axis=0)
            v16 = jnp.where(mine, v, 0.0).astype(BF16)
            _, p, _ = _attn_scores(qn_tiles, kn16, kh, G, half, bias_ref, sink_ref, first)
            o = _dot(p.astype(BF16), v16)
            for g in range(G):
                hq = kh * G + g
                og = o[g * BLOCK : (g + 1) * BLOCK, :]
                if hq % 2 != half:
                    og = pltpu.roll(og, HEAD_DIM, 1)
                out_tiles[hq // 2] = out_tiles[hq // 2] + og
        for t in range(D // LANES):
            o_ref[:, t * LANES : (t + 1) * LANES] = out_tiles[t].astype(BF16)

    return pl.pallas_call(
        body,
        name="attn_fwd",
        grid=(n_seq, nb),
        in_specs=_attn_specs(D, nb)
        + [
            _resident((1, LANES)),
            _resident((1, LANES)),
            _resident((N_KV_HEADS, R, 2 * BLOCK)),
            _resident((N_KV_HEADS, R, 1)),
        ],
        out_specs=pl.BlockSpec((BLOCK, D), lambda b, i: (b * nb + i, 0)),
        out_shape=jax.ShapeDtypeStruct((T, D), BF16),
        compiler_params=_params(2),
    )(qkv, qkv, qkv, qkv, qkv, qg, kg, bias, sinkcol)


def attn_bwd(qkv, do, qg, kg, bias, sinkcol, *, seq, n_seq):
    T = qkv.shape[0]
    kvw = N_KV_HEADS * HEAD_DIM
    D = qkv.shape[1] - 2 * kvw
    nb = seq // BLOCK
    G = D // HEAD_DIM // N_KV_HEADS
    R = G * BLOCK
    nqt, nkt = D // LANES, kvw // LANES

    def body(q_ref, kc_ref, kp_ref, vc_ref, vp_ref, do_ref, qg_ref, kg_ref, bias_ref, sink_ref,
             dq_ref, dc_ref, dp_ref, gain_ref, dsink_ref):
        first = pl.program_id(1) == 0

        @pl.when(jnp.logical_and(pl.program_id(0) == 0, first))
        def _():
            gain_ref[...] = jnp.zeros_like(gain_ref)
            dsink_ref[...] = jnp.zeros_like(dsink_ref)

        lane = lax.broadcasted_iota(jnp.int32, (1, LANES), 1)
        lo = lane < HEAD_DIM
        q_raw = [q_ref[:, t * LANES : (t + 1) * LANES] for t in range(nqt)]
        qn = [_head_norm(q, qg_ref[...], lo) for q in q_raw]
        qn_tiles = [a for a, _ in qn]
        dqn_tiles = [jnp.zeros((BLOCK, LANES), F32) for _ in range(nqt)]
        dkn_tiles = [jnp.zeros((2 * BLOCK, LANES), F32) for _ in range(nkt)]
        dv_tiles = [jnp.zeros((2 * BLOCK, LANES), F32) for _ in range(nkt)]
        k_raw, k_r = [None] * nkt, [None] * nkt
        dsink = jnp.zeros((1, LANES), F32)
        for kh in range(N_KV_HEADS):
            kt, half = kh // 2, kh % 2
            sl = slice(kt * LANES, (kt + 1) * LANES)
            mine = lo if half == 0 else jnp.logical_not(lo)
            if half == 0:
                k_raw[kt] = jnp.concatenate([kp_ref[:, sl], kc_ref[:, sl]], axis=0)
            kn, k_r[kt] = _head_norm(k_raw[kt], kg_ref[...], lo)
            kn16 = jnp.where(mine, kn, 0.0).astype(BF16)
            v = jnp.concatenate([vp_ref[:, sl], vc_ref[:, sl]], axis=0)
            v16 = jnp.where(mine, v, 0.0).astype(BF16)
            qs16, p, ps = _attn_scores(qn_tiles, kn16, kh, G, half, bias_ref, sink_ref, first)
            dos = []
            for g in range(G):
                hq = kh * G + g
                dog = do_ref[:, (hq // 2) * LANES : (hq // 2 + 1) * LANES].astype(F32)
                if hq % 2 != half:
                    dog = pltpu.roll(dog, HEAD_DIM, 1)
                dos.append(dog)
            do16 = jnp.concatenate(dos, axis=0).astype(BF16)
            dv_tiles[kt] = dv_tiles[kt] + jnp.where(mine, _dot_tn(p.astype(BF16), do16), 0.0)
            dp = _dot_nt(do16, v16)
            delta = jnp.sum(p * dp, axis=-1, keepdims=True)
            ds16 = (p * (dp - delta)).astype(BF16)
            dsr = -(ps * delta)
            dqs = _dot(ds16, kn16) * (1.0 / math.sqrt(HEAD_DIM))
            dkn_tiles[kt] = dkn_tiles[kt] + jnp.where(mine, _dot_tn(ds16, qs16), 0.0)
            for g in range(G):
                hq = kh * G + g
                rows = slice(g * BLOCK, (g + 1) * BLOCK)
                dsink = dsink + jnp.where(lane == hq, jnp.sum(dsr[rows, :], axis=0, keepdims=True), 0.0)
                dqg = dqs[rows, :]
                if hq % 2 != half:
                    dqg = pltpu.roll(dqg, HEAD_DIM, 1)
                dqn_tiles[hq // 2] = dqn_tiles[hq // 2] + dqg
        dsink_ref[0:1, :] += dsink
        gq = jnp.zeros((1, LANES), F32)
        for t in range(nqt):
            dx, dgn = _head_norm_bwd(q_raw[t], qn[t][1], qg_ref[...], dqn_tiles[t], lo)
            dq_ref[:, t * LANES : (t + 1) * LANES] = dx
            gq = gq + dgn
        gain_ref[0:1, :] += gq
        gk = jnp.zeros((1, LANES), F32)
        for t in range(nkt):
            dx, dgn = _head_norm_bwd(k_raw[t], k_r[t], kg_ref[...], dkn_tiles[t], lo)
            dp_ref[:, t * LANES : (t + 1) * LANES] = dx[:BLOCK, :]
            dc_ref[:, t * LANES : (t + 1) * LANES] = dx[BLOCK:, :]
            dp_ref[:, kvw + t * LANES : kvw + (t + 1) * LANES] = dv_tiles[t][:BLOCK, :]
            dc_ref[:, kvw + t * LANES : kvw + (t + 1) * LANES] = dv_tiles[t][BLOCK:, :]
            gk = gk + dgn
        gain_ref[1:2, :] += gk

    cur = lambda b, i: (b * nb + i, 0)
    return pl.pallas_call(
        body,
        name="attn_bwd",
        grid=(n_seq, nb),
        in_specs=_attn_specs(D, nb)
        + [
            pl.BlockSpec((BLOCK, D), cur),
            _resident((1, LANES)),
            _resident((1, LANES)),
            _resident((N_KV_HEADS, R, 2 * BLOCK)),
            _resident((N_KV_HEADS, R, 1)),
        ],
        out_specs=[
            pl.BlockSpec((BLOCK, D), cur),
            pl.BlockSpec((BLOCK, 2 * kvw), cur),
            pl.BlockSpec((BLOCK, 2 * kvw), cur),
            pl.BlockSpec((8, LANES), lambda b, i: (0, 0)),
            pl.BlockSpec((8, LANES), lambda b, i: (0, 0)),
        ],
        out_shape=[
            jax.ShapeDtypeStruct((T, D), F32),
            jax.ShapeDtypeStruct((T, 2 * kvw), F32),
            jax.ShapeDtypeStruct((T, 2 * kvw), F32),
            jax.ShapeDtypeStruct((8, LANES), F32),
            jax.ShapeDtypeStruct((8, LANES), F32),
        ],
        compiler_params=_params(2),
    )(qkv, qkv, qkv, qkv, qkv, do, qg, kg, bias, sinkcol)


def qkv_bwd(dq, dkv_cur, dkv_prev, dres, x, gain, w_qkv, *, seq):
    T, D = x.shape
    kvw2 = dkv_cur.shape[1]
    nb = seq // BLOCK
    n = T // BLOCK

    def body(dq_ref, dc_ref, dp_ref, dres_ref, x_ref, g_ref, w_ref, dx_ref, dqkv_ref, h_ref, dg_ref):
        i = pl.program_id(0)

        @pl.when(i == 0)
        def _():
            dg_ref[...] = jnp.zeros_like(dg_ref)

        dqkv_ref[:, :D] = dq_ref[...].astype(BF16)
        dkv = dc_ref[...] + jnp.where(i % nb == nb - 1, 0.0, dp_ref[...])
        dqkv_ref[:, D:] = dkv.astype(BF16)
        dh = _dot_nt(dqkv_ref[...], w_ref[...])
        xt = x_ref[...]
        r = _rms(xt)
        gn = g_ref[...]
        h_ref[...] = ((xt * r) * gn).astype(BF16)
        dx, dgn = _rms_bwd(xt, r, gn, dh)
        dg_ref[0:1, :] += dgn
        dx_ref[...] = dres_ref[...] + dx

    row = lambda i: (i, 0)
    return pl.pallas_call(
        body,
        name="qkv_bwd",
        grid=(n,),
        in_specs=[
            pl.BlockSpec((BLOCK, D), row),
            pl.BlockSpec((BLOCK, kvw2), row),
            pl.BlockSpec((BLOCK, kvw2), lambda i: (jnp.minimum(i + 1, n - 1), 0)),
            pl.BlockSpec((BLOCK, D), row),
            pl.BlockSpec((BLOCK, D), row),
            _resident((1, D)),
            _resident((D, D + kvw2)),
        ],
        out_specs=[
            pl.BlockSpec((BLOCK, D), row),
            pl.BlockSpec((BLOCK, D + kvw2), row),
            pl.BlockSpec((BLOCK, D), row),
            pl.BlockSpec((8, D), lambda i: (0, 0)),
        ],
        out_shape=[
            jax.ShapeDtypeStruct((T, D), F32),
            jax.ShapeDtypeStruct((T, D + kvw2), BF16),
            jax.ShapeDtypeStruct((T, D), BF16),
            jax.ShapeDtypeStruct((8, D), F32),
        ],
        compiler_params=_params(1),
    )(dq, dkv_cur, dkv_prev, dres, x, gain, w_qkv)


def loss_head(y, target, *, tm):
    T, D = y.shape

    def body(y_ref, t_ref, dy_ref, s_ref):
        @pl.when(pl.program_id(0) == 0)
        def _():
            s_ref[...] = jnp.zeros_like(s_ref)

        e = y_ref[...] - t_ref[...]
        dy_ref[...] = e * (1.0 / D)
        s_ref[...] += jnp.sum(jnp.sum(e * e, axis=-1, keepdims=True), axis=0, keepdims=True)

    return pl.pallas_call(
        body,
        name="loss_head",
        grid=(T // tm,),
        in_specs=[pl.BlockSpec((tm, D), lambda i: (i, 0)), pl.BlockSpec((tm, D), lambda i: (i, 0))],
        out_specs=[pl.BlockSpec((tm, D), lambda i: (i, 0)), pl.BlockSpec((8, LANES), lambda i: (0, 0))],
        out_shape=[jax.ShapeDtypeStruct((T, D), F32), jax.ShapeDtypeStruct((8, LANES), F32)],
        compiler_params=_params(1),
    )(y, target)


def local_step(x, target, gains, w, *, seq, tm=256, tk=2048):
    T, D = x.shape
    n_seq = T // seq
    nm, nf, qgain, kgain, sinks = gains
    H = D // HEAD_DIM
    tk = min(tk, T)
    qg2, kg2 = jnp.tile(qgain, (1, 2)), jnp.tile(kgain, (1, 2))
    bias, sinkcol = _attn_tables(sinks, H)

    x1, bcx = conv_fwd(x, nm[0:1], w["w_in"], w["cw"], w["w_out"], seq=seq, tm=tm)
    x2, gu0 = ffn_fwd(x1, nf[0:1], w["w_gu"][0], w["w_d"][0], tm=tm)
    qkv = norm_matmul(x2, nm[1:2], w["w_qkv"], tm=tm)
    ao = attn_fwd(qkv, qg2, kg2, bias, sinkcol, seq=seq, n_seq=n_seq)
    x3 = matmul_residual(ao, w["w_o"], x2, tm=tm)
    x4, gu1 = ffn_fwd(x3, nf[1:2], w["w_gu"][1], w["w_d"][1], tm=tm)
    dx4, sse = loss_head(x4, target, tm=tm)

    g = {}
    C = w["w_gu"][0].shape[2]

    def ffn_grads(dxo, xin, gain, gu, layer):
        dxi, a16, dgu, h16, d16, dgain = ffn_bwd(dxo, xin, gain, gu, w["w_gu"][layer], w["w_d"][layer], tm=tm)
        g["w_gu%d" % layer] = wgrad(h16, dgu, name="wgrad_gu", a_chunks=False, b_chunks=True, b_cols=0, tk=tk, out_dtype=BF16)
        g["w_d%d" % layer] = wgrad(a16, d16, name="wgrad_d", a_chunks=True, b_chunks=False, b_cols=0, tk=tk, out_dtype=BF16).reshape(N_DEV, C // 2, D)
        return dxi, dgain

    dx3, dnf1 = ffn_grads(dx4, x3, nf[1:2], gu1, 1)
    dao, dx3_16 = matmul_nt_cast(dx3, w["w_o"], tm=tm)
    g["w_o"] = wgrad(ao, dx3_16, name="wgrad_o", a_chunks=False, b_chunks=False, b_cols=0, tk=tk, out_dtype=BF16).reshape(N_DEV, D // N_DEV, D)
    dq, dkv_cur, dkv_prev, dgains, dsinks = attn_bwd(qkv, dao, qg2, kg2, bias, sinkcol, seq=seq, n_seq=n_seq)
    dx2, dqkv16, h16, dnm1 = qkv_bwd(dq, dkv_cur, dkv_prev, dx3, x2, nm[1:2], w["w_qkv"], seq=seq)
    nq = dqkv16.shape[1]
    gq = wgrad(h16, dqkv16, name="wgrad_qkv", a_chunks=False, b_chunks=False, b_cols=0, tk=tk, out_dtype=BF16)
    g["w_qkv"] = gq.reshape(D, N_DEV, nq // N_DEV).transpose(1, 0, 2)
    dx1, dnf0 = ffn_grads(dx2, x1, nf[0:1], gu0, 0)
    gx, dbcx, h16, z16, d16, dcw, dnm0 = conv_bwd(dx1, x, nm[0:1], bcx, w["cw"], w["w_in"], w["w_out"], seq=seq, tm=tm)
    g["w_in"] = wgrad(h16, dbcx, name="wgrad_in", a_chunks=False, b_chunks=False, b_cols=3 * D // N_DEV, tk=tk, out_dtype=BF16)
    g["w_out"] = wgrad(z16, d16, name="wgrad_out", a_chunks=False, b_chunks=False, b_cols=0, tk=tk, out_dtype=BF16).reshape(N_DEV, D // N_DEV, D)
    g["cw"] = dcw[0:3].reshape(3, N_DEV, D // N_DEV).transpose(1, 0, 2)
    small = dict(nm0=dnm0, nm1=dnm1, nf0=dnf0, nf1=dnf1, gains=dgains, sinks=dsinks)
    return sse, gx, g, small


def _place():
    x, y, c = lax.axis_index("x"), lax.axis_index("y"), lax.axis_index("c")
    return x, y, c


def _flip(v, bit):
    return 1 - v if bit else v


def _any_specs(n):
    return [pl.BlockSpec(memory_space=pl.ANY)] * n


def all_gather_weights(shards):
    nt = len(shards)

    def body(*refs):
        ins, outs = refs[:nt], refs[nt : 2 * nt]
        send_sems, recv_sems, loc_sems = refs[2 * nt :]
        x, y, c = _place()
        me = 4 * x + 2 * y + c
        sib = (x, y, 1 - c)
        chips = [(_flip(x, k >> 1), _flip(y, k & 1)) for k in (1, 2, 3)]

        def slot(px, py, pc):
            return 4 * px + 2 * py + pc

        def copy(t, k, src, dst_slot, to):
            return pltpu.make_async_remote_copy(
                src_ref=src, dst_ref=outs[t].at[dst_slot], send_sem=send_sems.at[t, k], recv_sem=recv_sems.at[t, k],
                device_id=to, device_id_type=MESH)

        started = []
        for t in range(nt):
            mine = pltpu.make_async_copy(ins[t], outs[t].at[me], loc_sems.at[t])
            mine.start()
            started.append(mine)
        sends = []
        for t in range(nt):
            sends.append(copy(t, 0, ins[t], me, sib))
            for j, (px, py) in enumerate(chips):
                sends.append(copy(t, 1 + j, ins[t], me, (px, py, c)))
        for cp in sends:
            cp.start()
        for t in range(nt):
            for j, (px, py) in enumerate(chips):
                s = slot(px, py, c)
                copy(t, 1 + j, outs[t].at[s], s, sib).wait_recv()
                fwd = copy(t, 4 + j, outs[t].at[s], s, sib)
                fwd.start()
                sends.append(fwd)
        for t in range(nt):
            s = slot(x, y, 1 - c)
            copy(t, 0, outs[t].at[s], s, sib).wait_recv()
            for j, (px, py) in enumerate(chips):
                s = slot(px, py, 1 - c)
                copy(t, 4 + j, outs[t].at[s], s, sib).wait_recv()
        for cp in sends:
            cp.wait_send()
        for cp in started:
            cp.wait()

    return pl.pallas_call(
        body,
        name="all_gather_weights",
        in_specs=_any_specs(nt),
        out_specs=_any_specs(nt),
        out_shape=[jax.ShapeDtypeStruct((N_DEV,) + s.shape, s.dtype) for s in shards],
        scratch_shapes=[pltpu.SemaphoreType.DMA((nt, 7)), pltpu.SemaphoreType.DMA((nt, 7)), pltpu.SemaphoreType.DMA((nt,))],
    )(*shards)


def exchange_cores(grads):
    nt = len(grads)

    def body(*refs):
        ins, outs = refs[:nt], refs[nt : 2 * nt]
        send_sems, recv_sems = refs[2 * nt :]
        x, y, c = _place()
        sib = (x, y, 1 - c)
        copies = []
        for t in range(nt):
            for p in range(N_CHIP):
                cp = pltpu.make_async_remote_copy(
                    src_ref=ins[t].at[2 * p + (1 - c)], dst_ref=outs[t].at[p], send_sem=send_sems.at[t, p],
                    recv_sem=recv_sems.at[t, p], device_id=sib, device_id_type=MESH)
                cp.start()
                copies.append(cp)
        for cp in copies:
            cp.wait_recv()
        for cp in copies:
            cp.wait_send()

    return pl.pallas_call(
        body,
        name="exchange_cores",
        in_specs=_any_specs(nt),
        out_specs=_any_specs(nt),
        out_shape=[jax.ShapeDtypeStruct((N_CHIP,) + g.shape[1:], g.dtype) for g in grads],
        scratch_shapes=[pltpu.SemaphoreType.DMA((nt, N_CHIP)), pltpu.SemaphoreType.DMA((nt, N_CHIP))],
    )(*grads)


def add_cores(mine, theirs):
    _, R, C = theirs.shape
    tr = R if R <= 512 else 512
    assert R % tr == 0

    def body(c_ref, a_ref, b_ref, o_ref):
        o_ref[...] = (a_ref[...].astype(F32) + b_ref[...].astype(F32)).astype(o_ref.dtype)

    core = lax.axis_index("c").astype(jnp.int32).reshape(1)
    return pl.pallas_call(
        body,
        name="add_cores",
        grid_spec=pltpu.PrefetchScalarGridSpec(
            num_scalar_prefetch=1,
            grid=(N_CHIP, R // tr),
            in_specs=[
                pl.BlockSpec((None, None, tr, C), lambda p, i, c_ref: (p, c_ref[0], i, 0)),
                pl.BlockSpec((None, tr, C), lambda p, i, c_ref: (p, i, 0)),
            ],
            out_specs=pl.BlockSpec((None, tr, C), lambda p, i, c_ref: (p, i, 0)),
        ),
        out_shape=jax.ShapeDtypeStruct(theirs.shape, mine.dtype),
        compiler_params=_params(2),
    )(core, mine.reshape(N_CHIP, 2, R, C), theirs)


def exchange_chips(sums, layout, small):
    nt = len(sums)
    bufs = {}
    for t, (b, layer) in enumerate(layout):
        shp = sums[t].shape[1:]
        if layer is None:
            bufs[b] = jax.ShapeDtypeStruct((N_CHIP,) + shp, sums[t].dtype)
        else:
            n_layers = 1 + max(l for bb, l in layout if bb == b)
            bufs[b] = jax.ShapeDtypeStruct((N_CHIP, n_layers) + shp, sums[t].dtype)
    nb = len(bufs)

    def body(*refs):
        ins, small_ref = refs[:nt], refs[nt]
        outs, small_out = refs[nt + 1 : nt + 1 + nb], refs[nt + 1 + nb]
        send_sems, recv_sems, loc_sems, ssend, srecv = refs[nt + 2 + nb :]
        x, y, c = _place()
        me = 4 * x + 2 * y + c
        p = 2 * x + y

        def land(t, src_chip):
            b, layer = layout[t]
            return outs[b].at[src_chip] if layer is None else outs[b].at[src_chip, layer]

        local = []
        for t in range(nt):
            cp = pltpu.make_async_copy(ins[t].at[p], land(t, p), loc_sems.at[t])
            cp.start()
            local.append(cp)
        cp = pltpu.make_async_copy(small_ref, small_out.at[me], loc_sems.at[nt])
        cp.start()
        local.append(cp)
        remote = []
        for k in (1, 2, 3):
            px, py = _flip(x, k >> 1), _flip(y, k & 1)
            q = 2 * px + py
            for t in range(nt):
                cp = pltpu.make_async_remote_copy(
                    src_ref=ins[t].at[q], dst_ref=land(t, p), send_sem=send_sems.at[t, k - 1],
                    recv_sem=recv_sems.at[t, k - 1], device_id=(px, py, c), device_id_type=MESH)
                cp.start()
                remote.append((cp, pltpu.make_async_remote_copy(
                    src_ref=ins[t].at[q], dst_ref=land(t, q), send_sem=send_sems.at[t, k - 1],
                    recv_sem=recv_sems.at[t, k - 1], device_id=(px, py, c), device_id_type=MESH)))
        for k in range(1, N_DEV):
            px, py, pc = _flip(x, (k >> 2) & 1), _flip(y, (k >> 1) & 1), _flip(c, k & 1)
            cp = pltpu.make_async_remote_copy(
                src_ref=small_ref, dst_ref=small_out.at[me], send_sem=ssend.at[k - 1], recv_sem=srecv.at[k - 1],
                device_id=(px, py, pc), device_id_type=MESH)
            cp.start()
            remote.append((cp, pltpu.make_async_remote_copy(
                src_ref=small_ref, dst_ref=small_out.at[4 * px + 2 * py + pc], send_sem=ssend.at[k - 1],
                recv_sem=srecv.at[k - 1], device_id=(px, py, pc), device_id_type=MESH)))
        for _, arrival in remote:
            arrival.wait_recv()
        for cp, _ in remote:
            cp.wait_send()
        for cp in local:
            cp.wait()

    res = pl.pallas_call(
        body,
        name="exchange_chips",
        in_specs=_any_specs(nt + 1),
        out_specs=_any_specs(nb + 1),
        out_shape=[bufs[b] for b in range(nb)] + [jax.ShapeDtypeStruct((N_DEV,) + small.shape, small.dtype)],
        scratch_shapes=[
            pltpu.SemaphoreType.DMA((nt, 3)),
            pltpu.SemaphoreType.DMA((nt, 3)),
            pltpu.SemaphoreType.DMA((nt + 1,)),
            pltpu.SemaphoreType.DMA((7,)),
            pltpu.SemaphoreType.DMA((7,)),
        ],
    )(*sums, small)
    return res[:nb], res[nb]


def _adamw_math(g, w, m, v):
    m = ADAM_B1 * m + (1.0 - ADAM_B1) * g
    v = ADAM_B2 * v + (1.0 - ADAM_B2) * (g * g)
    m_hat = m / (1.0 - ADAM_B1 ** ADAM_STEP)
    v_hat = v / (1.0 - ADAM_B2 ** ADAM_STEP)
    delta = -ADAM_LR * (m_hat / (jnp.sqrt(v_hat) + ADAM_EPS) + ADAM_WD * w)
    return delta, m, v


def adamw(parts, w, m, v, *, name):
    n, R, C = parts.shape
    tr = R
    for cand in (256, 128, 88, 64, 32, 16, 8):
        if R > cand and R % cand == 0:
            tr = cand
            break

    def body(p_ref, w_ref, m_ref, v_ref, g_ref, d_ref, mo_ref, vo_ref):
        g = p_ref[0].astype(F32)
        for s in range(1, n):
            g = g + p_ref[s].astype(F32)
        g_ref[...] = g
        d_ref[...], mo_ref[...], vo_ref[...] = _adamw_math(g, w_ref[...], m_ref[...], v_ref[...])

    blk = pl.BlockSpec((tr, C), lambda i: (i, 0))
    return pl.pallas_call(
        body,
        name=name,
        grid=(R // tr,),
        in_specs=[pl.BlockSpec((n, tr, C), lambda i: (0, i, 0)), blk, blk, blk],
        out_specs=[blk] * 4,
        out_shape=[jax.ShapeDtypeStruct((R, C), F32)] * 4,
        compiler_params=_params(1),
    )(parts, w, m, v)


def pack_small(small, D):
    W = max(D, 2 * LANES)

    def body(nm0, nm1, nf0, nf1, gains, sinks, o_ref):
        o_ref[...] = jnp.zeros_like(o_ref)
        o_ref[0:1, :D] = nm0[0:1, :]
        o_ref[1:2, :D] = nm1[0:1, :]
        o_ref[2:3, :D] = nf0[0:1, :]
        o_ref[3:4, :D] = nf1[0:1, :]
        gq = gains[0:1, :] + pltpu.roll(gains[0:1, :], HEAD_DIM, 1)
        gk = gains[1:2, :] + pltpu.roll(gains[1:2, :], HEAD_DIM, 1)
        lane = lax.broadcasted_iota(jnp.int32, (1, LANES), 1)
        o_ref[4:5, :LANES] = jnp.where(lane < HEAD_DIM, gq, gk)
        o_ref[4:5, LANES : 2 * LANES] = sinks[0:1, :]

    return pl.pallas_call(
        body,
        name="pack_small",
        out_shape=jax.ShapeDtypeStruct((8, W), F32),
    )(small["nm0"], small["nm1"], small["nf0"], small["nf1"], small["gains"], small["sinks"])


def _pack_small_params(nm, nf, qg, kg, sk, D):
    W = max(D, 2 * LANES)
    row4 = jnp.concatenate([qg.reshape(-1), kg.reshape(-1), jnp.zeros((LANES - 2 * HEAD_DIM,), F32), sk.reshape(-1)])
    row4 = jnp.pad(row4, (0, W - row4.shape[0]))
    rows = [jnp.pad(r, (0, W - D)) for r in (nm[0], nm[1], nf[0], nf[1])] + [row4]
    return jnp.concatenate([jnp.stack(rows), jnp.zeros((3, W), F32)], axis=0)


def _unpack_small(a, D, H):
    nm = a[0:2, :D]
    nf = a[2:4, :D]
    qg = a[4:5, 0:HEAD_DIM]
    kg = a[4:5, HEAD_DIM : 2 * HEAD_DIM]
    sk = a[4:5, LANES : LANES + H]
    return qg, kg, sk, nm, nf


def kernel(x, conv_w_in, conv_w, conv_w_out, attn_w_qkv, attn_q_gain, attn_k_gain, attn_sinks, attn_w_o, norm_mixer, norm_ffn, ffn_w_gate_up, ffn_w_down, loss_target, m_conv_w_in, m_conv_w, m_conv_w_out, m_attn_w_qkv, m_attn_q_gain, m_attn_k_gain, m_attn_sinks, m_attn_w_o, m_norm_mixer, m_norm_ffn, m_ffn_w_gate_up, m_ffn_w_down, v_conv_w_in, v_conv_w, v_conv_w_out, v_attn_w_qkv, v_attn_q_gain, v_attn_k_gain, v_attn_sinks, v_attn_w_o, v_norm_mixer, v_norm_ffn, v_ffn_w_gate_up, v_ffn_w_down):
    n_seq, seq, D = x.shape
    T = n_seq * seq
    H = D // HEAD_DIM
    L = ffn_w_gate_up.shape[0]
    C = ffn_w_gate_up.shape[2]

    shards = [conv_w_in[0].astype(BF16), conv_w[0], conv_w_out[0].astype(BF16), attn_w_qkv[0].astype(BF16),
              attn_w_o[0].astype(BF16)]
    shards += [ffn_w_gate_up[l].astype(BF16) for l in range(L)] + [ffn_w_down[l].astype(BF16) for l in range(L)]
    full = all_gather_weights(shards)
    nq = attn_w_qkv.shape[2] * N_DEV
    w = dict(
        w_in=full[0].transpose(1, 0, 2).reshape(D, 3 * D),
        cw=full[1].transpose(1, 0, 2).reshape(3, D),
        w_out=full[2].reshape(D, D),
        w_qkv=full[3].transpose(1, 0, 2).reshape(D, nq),
        w_o=full[4].reshape(D, D),
        w_gu=[full[5 + l] for l in range(L)],
        w_d=[full[5 + L + l].reshape(-1, D) for l in range(L)],
    )
    gains = (norm_mixer, norm_ffn, attn_q_gain, attn_k_gain, attn_sinks)
    sse, gx, g, small = local_step(x.reshape(T, D), loss_target.reshape(T, D), gains, w, seq=seq)
    loss = lax.psum(sse[0, 0] * (0.5 / D), ("x", "y", "c"))

    names = ["w_in", "cw", "w_out", "w_qkv", "w_o"] + ["w_gu%d" % l for l in range(L)] + ["w_d%d" % l for l in range(L)]
    grads = [g[k] for k in names]
    theirs = exchange_cores(grads)
    sums = [add_cores(a, b) for a, b in zip(grads, theirs)]
    layout = [(0, None), (1, None), (2, None), (3, None), (4, None)] + [(5, l) for l in range(L)] + [(6, l) for l in range(L)]
    bufs, small_all = exchange_chips(sums, layout, pack_small(small, D))

    def flat(a):
        return a.reshape(-1, a.shape[-1])

    big = [conv_w_in, conv_w, conv_w_out, attn_w_qkv, attn_w_o, ffn_w_gate_up, ffn_w_down]
    big_m = [m_conv_w_in, m_conv_w, m_conv_w_out, m_attn_w_qkv, m_attn_w_o, m_ffn_w_gate_up, m_ffn_w_down]
    big_v = [v_conv_w_in, v_conv_w, v_conv_w_out, v_attn_w_qkv, v_attn_w_o, v_ffn_w_gate_up, v_ffn_w_down]
    tags = ["in", "cw", "out", "qkv", "o", "gu", "d"]
    res = []
    for b in range(7):
        parts = bufs[b].reshape(N_CHIP, -1, bufs[b].shape[-1])
        outs = adamw(parts, flat(big[b]), flat(big_m[b]), flat(big_v[b]), name="adamw_" + tags[b])
        res.append([o.reshape(big[b].shape) for o in outs])
    sw = _pack_small_params(norm_mixer, norm_ffn, attn_q_gain, attn_k_gain, attn_sinks, D)
    sm = _pack_small_params(m_norm_mixer, m_norm_ffn, m_attn_q_gain, m_attn_k_gain, m_attn_sinks, D)
    sv = _pack_small_params(v_norm_mixer, v_norm_ffn, v_attn_q_gain, v_attn_k_gain, v_attn_sinks, D)
    souts = adamw(small_all, sw, sm, sv, name="adamw_small")
    sres = [_unpack_small(o, D, H) for o in souts]

    def ordered(i):
        r, s = [r[i] for r in res], sres[i]
        return [r[0], r[1], r[2], r[3], s[0], s[1], s[2], r[4], s[3], s[4], r[5], r[6]]

    return (loss, gx.reshape(n_seq, seq, D), *ordered(0), *ordered(1), *ordered(2), *ordered(3))
```

```python
import functools
import math

import jax
import jax.numpy as jnp
from jax import lax
from jax.experimental import pallas as pl
from jax.experimental.pallas import tpu as pltpu

F32 = jnp.float32
BF16 = jnp.bfloat16

EPS = 1e-6
HEAD_DIM = 64
N_KV_HEADS = 4
BLOCK = 128
LANES = 128
N_DEV = 8
N_CHIP = 4
NEG = -1e30
SCALE = 1.0 / math.sqrt(HEAD_DIM)

ADAM_LR = 0.001
ADAM_B1 = 0.9
ADAM_B2 = 0.999
ADAM_EPS = 1e-08
ADAM_WD = 0.01
ADAM_STEP = 10

V7X_VMEM_BYTES = 64 * 1024 * 1024
VMEM_LIMIT = V7X_VMEM_BYTES - 8 * 1024 * 1024
MESH = pl.DeviceIdType.MESH

_NT = (((1,), (1,)), ((), ()))
_TN = (((0,), (0,)), ((), ()))


def _params(n_grid):
    return pltpu.CompilerParams(dimension_semantics=("arbitrary",) * n_grid, vmem_limit_bytes=VMEM_LIMIT)


def _resident(shape):
    nd = len(shape)
    return pl.BlockSpec(shape, lambda *_: (0,) * nd, pipeline_mode=pl.Buffered(1))


def _rms(x):
    return lax.rsqrt(jnp.mean(x * x, axis=-1, keepdims=True) + EPS)


def _rms_bwd(x, r, gain, dh):
    xn = x * r
    dxn = dh * gain
    dx = r * (dxn - xn * jnp.mean(dxn * xn, axis=-1, keepdims=True))
    return dx, jnp.sum(dh * xn, axis=0, keepdims=True)


def _dot(a, b):
    return jnp.dot(a, b, preferred_element_type=F32)


def _dot_nt(a, b):
    return lax.dot_general(a, b, _NT, preferred_element_type=F32)


def _dot_tn(a, b):
    return lax.dot_general(a, b, _TN, preferred_element_type=F32)


def _shift_down(u, prev8, row, n):
    out = pltpu.roll(u, n, 0)
    for k in range(n):
        out = jnp.where(row == k, prev8[8 - n + k : 8 - n + k + 1, :], out)
    return out


def _shift_up(u, next8, row, n, tm):
    out = pltpu.roll(u, tm - n, 0)
    for k in range(n):
        out = jnp.where(row == tm - n + k, next8[k : k + 1, :], out)
    return out


def conv_fwd(x, gain, w_in, cw, w_out, *, seq, tm):
    T, D = x.shape
    tps = seq // tm

    def body(x_ref, g_ref, win_ref, cw_ref, wout_ref, x1_ref, bcx_ref, carry_ref):
        i = pl.program_id(0)

        @pl.when(i % tps == 0)
        def _():
            carry_ref[...] = jnp.zeros_like(carry_ref)

        xt = x_ref[...]
        h = ((xt * _rms(xt)) * g_ref[...]).astype(BF16)
        bcx = _dot(h, win_ref[...])
        bcx_ref[...] = bcx
        b, c, xv = bcx[:, :D], bcx[:, D : 2 * D], bcx[:, 2 * D :]
        u = b * xv
        row = lax.broadcasted_iota(jnp.int32, u.shape, 0)
        prev = carry_ref[...]
        u1 = _shift_down(u, prev, row, 1)
        u2 = _shift_down(u, prev, row, 2)
        carry_ref[...] = u[tm - 8 :, :]
        cwv = cw_ref[...]
        y = cwv[0:1, :] * u2 + cwv[1:2, :] * u1 + cwv[2:3, :] * u
        z = (c * y).astype(BF16)
        x1_ref[...] = xt + _dot(z, wout_ref[...])

    return pl.pallas_call(
        body,
        name="conv_fwd",
        grid=(T // tm,),
        in_specs=[
            pl.BlockSpec((tm, D), lambda i: (i, 0)),
            _resident((1, D)),
            _resident((D, 3 * D)),
            _resident((3, D)),
            _resident((D, D)),
        ],
        out_specs=[pl.BlockSpec((tm, D), lambda i: (i, 0)), pl.BlockSpec((tm, 3 * D), lambda i: (i, 0))],
        out_shape=[jax.ShapeDtypeStruct((T, D), F32), jax.ShapeDtypeStruct((T, 3 * D), F32)],
        scratch_shapes=[pltpu.VMEM((8, D), F32)],
        compiler_params=_params(1),
    )(x, gain, w_in, cw, w_out)


def conv_bwd(dx1, x, gain, bcx, cw, w_in, w_out, *, seq, tm):
    T, D = x.shape
    n = T // tm
    tps = seq // tm
    r8 = tm // 8

    def body(d_ref, x_ref, g_ref, bcx_ref, halo_ref, cw_ref, win_ref, wout_ref,
             gx_ref, dbcx_ref, h_ref, z_ref, d16_ref, dcw_ref, dg_ref, carry_ref):
        i = pl.program_id(0)
        t = n - 1 - i

        @pl.when(i == 0)
        def _():
            dcw_ref[...] = jnp.zeros_like(dcw_ref)
            dg_ref[...] = jnp.zeros_like(dg_ref)

        @pl.when(t % tps == tps - 1)
        def _():
            carry_ref[...] = jnp.zeros_like(carry_ref)

        d = d_ref[...]
        d16 = d.astype(BF16)
        d16_ref[...] = d16
        dz = _dot_nt(d16, wout_ref[...])
        bcx = bcx_ref[...]
        b, c, xv = bcx[:, :D], bcx[:, D : 2 * D], bcx[:, 2 * D :]
        u = b * xv
        halo = halo_ref[...]
        hu = jnp.where(t % tps == 0, 0.0, halo[:, :D] * halo[:, 2 * D :])
        row = lax.broadcasted_iota(jnp.int32, u.shape, 0)
        u1 = _shift_down(u, hu, row, 1)
        u2 = _shift_down(u, hu, row, 2)
        cwv = cw_ref[...]
        y = cwv[0:1, :] * u2 + cwv[1:2, :] * u1 + cwv[2:3, :] * u
        z_ref[...] = (c * y).astype(BF16)
        dc = dz * y
        dy = dz * c
        dcw_ref[0:1, :] += jnp.sum(dy * u2, axis=0, keepdims=True)
        dcw_ref[1:2, :] += jnp.sum(dy * u1, axis=0, keepdims=True)
        dcw_ref[2:3, :] += jnp.sum(dy * u, axis=0, keepdims=True)
        nxt = carry_ref[...]
        dy1 = _shift_up(dy, nxt, row, 1, tm)
        dy2 = _shift_up(dy, nxt, row, 2, tm)
        carry_ref[...] = dy[0:8, :]
        du = cwv[2:3, :] * dy + cwv[1:2, :] * dy1 + cwv[0:1, :] * dy2
        dbcx_ref[:, :D] = (du * xv).astype(BF16)
        dbcx_ref[:, D : 2 * D] = dc.astype(BF16)
        dbcx_ref[:, 2 * D :] = (du * b).astype(BF16)
        dh = _dot_nt(dbcx_ref[...], win_ref[...])
        xt = x_ref[...]
        r = _rms(xt)
        gn = g_ref[...]
        h_ref[...] = ((xt * r) * gn).astype(BF16)
        dx, dgn = _rms_bwd(xt, r, gn, dh)
        dg_ref[0:1, :] += dgn
        gx_ref[...] = d + dx

    rev = lambda i: (n - 1 - i, 0)
    return pl.pallas_call(
        body,
        name="conv_bwd",
        grid=(n,),
        in_specs=[
            pl.BlockSpec((tm, D), rev),
            pl.BlockSpec((tm, D), rev),
            _resident((1, D)),
            pl.BlockSpec((tm, 3 * D), rev),
            pl.BlockSpec((8, 3 * D), lambda i: (jnp.maximum((n - 1 - i) * r8 - 1, 0), 0)),
            _resident((3, D)),
            _resident((D, 3 * D)),
            _resident((D, D)),
        ],
        out_specs=[
            pl.BlockSpec((tm, D), rev),
            pl.BlockSpec((tm, 3 * D), rev),
            pl.BlockSpec((tm, D), rev),
            pl.BlockSpec((tm, D), rev),
            pl.BlockSpec((tm, D), rev),
            pl.BlockSpec((8, D), lambda i: (0, 0)),
            pl.BlockSpec((8, D), lambda i: (0, 0)),
        ],
        out_shape=[
            jax.ShapeDtypeStruct((T, D), F32),
            jax.ShapeDtypeStruct((T, 3 * D), BF16),
            jax.ShapeDtypeStruct((T, D), BF16),
            jax.ShapeDtypeStruct((T, D), BF16),
            jax.ShapeDtypeStruct((T, D), BF16),
            jax.ShapeDtypeStruct((8, D), F32),
            jax.ShapeDtypeStruct((8, D), F32),
        ],
        scratch_shapes=[pltpu.VMEM((8, D), F32)],
        compiler_params=_params(1),
    )(dx1, x, gain, bcx, bcx, cw, w_in, w_out)


def _sigmoid(g):
    return 1.0 / (1.0 + jnp.exp(-g))


def ffn_fwd(x, gain, w_gu, w_d, *, tm):
    T, D = x.shape
    _, _, C = w_gu.shape
    half = N_DEV // 2

    def body(x_ref, g_ref, wgu_ref, wd_ref, xo_ref, gu_ref):
        xt = x_ref[...]
        h = ((xt * _rms(xt)) * g_ref[...]).astype(BF16)
        acc = xt
        for j in range(half):
            g = _dot(h, wgu_ref[j])
            u = _dot(h, wgu_ref[j + half])
            gu_ref[j] = g
            gu_ref[j + half] = u
            a = ((g * _sigmoid(g)) * u).astype(BF16)
            acc = acc + _dot(a, wd_ref[j * C : (j + 1) * C, :])
        xo_ref[...] = acc

    return pl.pallas_call(
        body,
        name="ffn_fwd",
        grid=(T // tm,),
        in_specs=[
            pl.BlockSpec((tm, D), lambda i: (i, 0)),
            _resident((1, D)),
            _resident((N_DEV, D, C)),
            _resident((half * C, D)),
        ],
        out_specs=[pl.BlockSpec((tm, D), lambda i: (i, 0)), pl.BlockSpec((N_DEV, tm, C), lambda i: (0, i, 0))],
        out_shape=[jax.ShapeDtypeStruct((T, D), F32), jax.ShapeDtypeStruct((N_DEV, T, C), F32)],
        compiler_params=_params(1),
    )(x, gain, w_gu, w_d)


def ffn_bwd(dxo, x, gain, gu, w_gu, w_d, *, tm):
    T, D = x.shape
    _, _, C = w_gu.shape
    half = N_DEV // 2

    def body(d_ref, x_ref, g_ref, gu_ref, wgu_ref, wd_ref, dx_ref, a_ref, dgu_ref, h_ref, d16_ref, dg_ref):
        @pl.when(pl.program_id(0) == 0)
        def _():
            dg_ref[...] = jnp.zeros_like(dg_ref)

        d = d_ref[...]
        d16 = d.astype(BF16)
        d16_ref[...] = d16
        dh = jnp.zeros((tm, D), F32)
        for j in range(half):
            g = gu_ref[j]
            u = gu_ref[j + half]
            da = _dot_nt(d16, wd_ref[j * C : (j + 1) * C, :])
            s = _sigmoid(g)
            sg = g * s
            a_ref[j] = (sg * u).astype(BF16)
            dg16 = (da * u * (s + sg * (1.0 - s))).astype(BF16)
            du16 = (da * sg).astype(BF16)
            dgu_ref[j] = dg16
            dgu_ref[j + half] = du16
            dh = dh + _dot_nt(dg16, wgu_ref[j]) + _dot_nt(du16, wgu_ref[j + half])
        xt = x_ref[...]
        r = _rms(xt)
        gn = g_ref[...]
        h_ref[...] = ((xt * r) * gn).astype(BF16)
        dx, dgn = _rms_bwd(xt, r, gn, dh)
        dg_ref[0:1, :] += dgn
        dx_ref[...] = d + dx

    return pl.pallas_call(
        body,
        name="ffn_bwd",
        grid=(T // tm,),
        in_specs=[
            pl.BlockSpec((tm, D), lambda i: (i, 0)),
            pl.BlockSpec((tm, D), lambda i: (i, 0)),
            _resident((1, D)),
            pl.BlockSpec((N_DEV, tm, C), lambda i: (0, i, 0)),
            _resident((N_DEV, D, C)),
            _resident((half * C, D)),
        ],
        out_specs=[
            pl.BlockSpec((tm, D), lambda i: (i, 0)),
            pl.BlockSpec((half, tm, C), lambda i: (0, i, 0)),
            pl.BlockSpec((N_DEV, tm, C), lambda i: (0, i, 0)),
            pl.BlockSpec((tm, D), lambda i: (i, 0)),
            pl.BlockSpec((tm, D), lambda i: (i, 0)),
            pl.BlockSpec((8, D), lambda i: (0, 0)),
        ],
        out_shape=[
            jax.ShapeDtypeStruct((T, D), F32),
            jax.ShapeDtypeStruct((half, T, C), BF16),
            jax.ShapeDtypeStruct((N_DEV, T, C), BF16),
            jax.ShapeDtypeStruct((T, D), BF16),
            jax.ShapeDtypeStruct((T, D), BF16),
            jax.ShapeDtypeStruct((8, D), F32),
        ],
        compiler_params=_params(1),
    )(dxo, x, gain, gu, w_gu, w_d)


def matmul_residual(a, w, res, *, tm):
    T, K = a.shape
    N = w.shape[1]

    def body(a_ref, w_ref, r_ref, o_ref):
        o_ref[...] = r_ref[...] + _dot(a_ref[...], w_ref[...])

    return pl.pallas_call(
        body,
        name="matmul_residual",
        grid=(T // tm,),
        in_specs=[pl.BlockSpec((tm, K), lambda i: (i, 0)), _resident((K, N)), pl.BlockSpec((tm, N), lambda i: (i, 0))],
        out_specs=pl.BlockSpec((tm, N), lambda i: (i, 0)),
        out_shape=jax.ShapeDtypeStruct((T, N), F32),
        compiler_params=_params(1),
    )(a, w, res)


def matmul_nt_cast(d, w, *, tm):
    T, N = d.shape
    K = w.shape[0]

    def body(d_ref, w_ref, o_ref, d16_ref):
        d16 = d_ref[...].astype(BF16)
        d16_ref[...] = d16
        o_ref[...] = _dot_nt(d16, w_ref[...]).astype(BF16)

    return pl.pallas_call(
        body,
        name="matmul_nt_cast",
        grid=(T // tm,),
        in_specs=[pl.BlockSpec((tm, N), lambda i: (i, 0)), _resident((K, N))],
        out_specs=[pl.BlockSpec((tm, K), lambda i: (i, 0)), pl.BlockSpec((tm, N), lambda i: (i, 0))],
        out_shape=[jax.ShapeDtypeStruct((T, K), BF16), jax.ShapeDtypeStruct((T, N), BF16)],
        compiler_params=_params(1),
    )(d, w)


def wgrad(a, b, *, name, a_chunks, b_chunks, b_cols, tk, out_dtype):
    T, K = a.shape[-2:]
    J = 1
    if a_chunks:
        J = a.shape[0]
        a_spec = pl.BlockSpec((None, tk, K), lambda j, k: (j, k, 0))
    else:
        a_spec = pl.BlockSpec((tk, K), lambda j, k: (k, 0))
    if b_chunks:
        J, _, N = b.shape
        b_spec = pl.BlockSpec((None, tk, N), lambda j, k: (j, k, 0))
    elif b_cols:
        N = b_cols
        J = b.shape[1] // N
        b_spec = pl.BlockSpec((tk, N), lambda j, k: (k, j))
    else:
        N = b.shape[1]
        b_spec = pl.BlockSpec((tk, N), lambda j, k: (k, 0))
    nk = T // tk

    def body(a_ref, b_ref, o_ref, acc_ref):
        k = pl.program_id(1)

        @pl.when(k == 0)
        def _():
            acc_ref[...] = jnp.zeros_like(acc_ref)

        acc_ref[...] += _dot_tn(a_ref[...], b_ref[...])

        @pl.when(k == nk - 1)
        def _():
            o_ref[...] = acc_ref[...].astype(out_dtype)

    return pl.pallas_call(
        body,
        name=name,
        grid=(J, nk),
        in_specs=[a_spec, b_spec],
        out_specs=pl.BlockSpec((None, K, N), lambda j, k: (j, 0, 0)),
        out_shape=jax.ShapeDtypeStruct((J, K, N), out_dtype),
        scratch_shapes=[pltpu.VMEM((K, N), F32)],
        compiler_params=_params(2),
    )(a, b)


def _seg(x, lo):
    s_lo = jnp.sum(jnp.where(lo, x, 0.0), axis=-1, keepdims=True)
    s_hi = jnp.sum(jnp.where(lo, 0.0, x), axis=-1, keepdims=True)
    return jnp.where(lo, s_lo, s_hi)


def _head_norm(x, gain, lo):
    r = lax.rsqrt(_seg(x * x, lo) * (1.0 / HEAD_DIM) + EPS)
    return (x * r) * gain, r


def _head_norm_bwd(x, r, gain, dy, lo):
    xn = x * r
    dxn = dy * gain
    dx = r * (dxn - xn * (_seg(dxn * xn, lo) * (1.0 / HEAD_DIM)))
    return dx, jnp.sum(dy * xn, axis=0, keepdims=True)


def _swap_halves(x):
    return pltpu.roll(x, HEAD_DIM, 1)


def qkv_proj(x, gain, w, qg, kg, *, tm):
    T, D = x.shape
    N = w.shape[1]
    kvw = N_KV_HEADS * HEAD_DIM
    nqt, nkt = D // LANES, kvw // LANES

    def body(x_ref, g_ref, w_ref, qg_ref, kg_ref, qkv_ref, q_ref, kd_ref, vd_ref):
        xt = x_ref[...]
        h = ((xt * _rms(xt)) * g_ref[...]).astype(BF16)
        qkv = _dot(h, w_ref[...])
        qkv_ref[...] = qkv
        lo = lax.broadcasted_iota(jnp.int32, (1, LANES), 1) < HEAD_DIM
        for t in range(nqt):
            qn, _ = _head_norm(qkv[:, t * LANES : (t + 1) * LANES], qg_ref[...], lo)
            q_ref[:, t * LANES : (t + 1) * LANES] = (qn * SCALE).astype(BF16)
        for t in range(nkt):
            kn, _ = _head_norm(qkv[:, D + t * LANES : D + (t + 1) * LANES], kg_ref[...], lo)
            v = qkv[:, D + kvw + t * LANES : D + kvw + (t + 1) * LANES]
            for src, dst in ((kn, kd_ref), (v, vd_ref)):
                sw = _swap_halves(src)
                dst[:, 2 * t * LANES : (2 * t + 1) * LANES] = jnp.where(lo, src, sw).astype(BF16)
                dst[:, (2 * t + 1) * LANES : (2 * t + 2) * LANES] = jnp.where(lo, sw, src).astype(BF16)

    row = lambda i: (i, 0)
    return pl.pallas_call(
        body,
        name="qkv_proj",
        grid=(T // tm,),
        in_specs=[pl.BlockSpec((tm, D), row), _resident((1, D)), _resident((D, N)), _resident((1, LANES)), _resident((1, LANES))],
        out_specs=[pl.BlockSpec((tm, N), row), pl.BlockSpec((tm, D), row), pl.BlockSpec((tm, 2 * kvw), row), pl.BlockSpec((tm, 2 * kvw), row)],
        out_shape=[
            jax.ShapeDtypeStruct((T, N), F32),
            jax.ShapeDtypeStruct((T, D), BF16),
            jax.ShapeDtypeStruct((T, 2 * kvw), BF16),
            jax.ShapeDtypeStruct((T, 2 * kvw), BF16),
        ],
        compiler_params=_params(1),
    )(x, gain, w, qg, kg)


def _attn_tables(sinks, n_q_heads):
    P = n_q_heads // N_KV_HEADS // 2
    h = jnp.arange(1, n_q_heads + 1, dtype=F32)
    slopes = jnp.exp2(-8.0 * h / n_q_heads).reshape(N_KV_HEADS, P, 1, 2, 1)
    qi = jnp.arange(BLOCK)[:, None]
    kj = jnp.arange(BLOCK)[None, :]
    dist = jnp.where(kj <= qi, qi - kj, qi + BLOCK - kj).astype(F32)
    shape = (N_KV_HEADS, P, BLOCK, 2, BLOCK)
    bias = jnp.broadcast_to(-slopes * dist[None, None, :, None, :], shape)
    sink = jnp.broadcast_to(sinks.astype(F32).reshape(N_KV_HEADS, P, 1, 2, 1), shape)
    return bias.reshape(N_KV_HEADS, P * BLOCK, 2 * BLOCK), sink.reshape(N_KV_HEADS, P * BLOCK, 2 * BLOCK)


def _attn_specs(D, nb):
    kvw2 = 2 * N_KV_HEADS * HEAD_DIM
    cur = lambda b, i: (b * nb + i, 0)
    prev = lambda b, i: (jnp.maximum(b * nb + i - 1, 0), 0)
    return [
        pl.BlockSpec((BLOCK, D), cur),
        pl.BlockSpec((BLOCK, kvw2), cur),
        pl.BlockSpec((BLOCK, kvw2), prev),
        pl.BlockSpec((BLOCK, kvw2), cur),
        pl.BlockSpec((BLOCK, kvw2), prev),
    ]


def _attn_operands(kh, P, lo, q_ref, kc_ref, kp_ref, vc_ref, vp_ref):
    sl = slice(kh * LANES, (kh + 1) * LANES)

    def cat(prev_ref, cur_ref):
        d = jnp.concatenate([prev_ref[:, sl], cur_ref[:, sl]], axis=0)
        z = jnp.zeros_like(d)
        return jnp.concatenate([jnp.where(lo, d, z), jnp.where(lo, z, d)], axis=0)

    qt = jnp.concatenate([q_ref[:, (kh * P + pr) * LANES : (kh * P + pr + 1) * LANES] for pr in range(P)], axis=0)
    return qt, cat(kp_ref, kc_ref), cat(vp_ref, vc_ref)


def _attn_exp(s_all, bias, sink, tri, first):
    out = []
    for par in range(2):
        c0 = 2 * par * BLOCK
        s = jnp.where(tri, s_all[:, c0 + BLOCK : c0 + 2 * BLOCK], jnp.where(first, NEG, s_all[:, c0 : c0 + BLOCK]))
        s = s + bias[:, par * BLOCK : (par + 1) * BLOCK]
        snk = sink[:, par * BLOCK : (par + 1) * BLOCK]
        m = jnp.maximum(jnp.max(s, axis=-1, keepdims=True), snk)
        out.append((jnp.exp(s - m), jnp.exp(snk - m)))
    return out


def _unfold(x, tri):
    z = jnp.zeros_like(x)
    return jnp.concatenate([jnp.where(tri, z, x), jnp.where(tri, x, z)], axis=1)


def _attn_masks(R):
    lane = lax.broadcasted_iota(jnp.int32, (1, LANES), 1)
    row = lax.broadcasted_iota(jnp.int32, (R, BLOCK), 0) & (BLOCK - 1)
    col = lax.broadcasted_iota(jnp.int32, (R, BLOCK), 1)
    return lane, lane < HEAD_DIM, col <= row


def attn_fwd(q16, kd, vd, bias, sink, *, seq, n_seq):
    T, D = q16.shape
    nb = seq // BLOCK
    P = D // HEAD_DIM // N_KV_HEADS // 2
    R = P * BLOCK
    KV = range(N_KV_HEADS)

    def body(q_ref, kc_ref, kp_ref, vc_ref, vp_ref, bias_ref, sink_ref, o_ref):
        first = pl.program_id(1) == 0
        _, lo, tri = _attn_masks(R)
        r4 = lax.broadcasted_iota(jnp.int32, (4 * BLOCK, LANES), 0)
        l4 = lax.broadcasted_iota(jnp.int32, (4 * BLOCK, LANES), 1)
        ones = ((r4 < 2 * BLOCK) == (l4 < HEAD_DIM)).astype(BF16)
        ops = [_attn_operands(kh, P, lo, q_ref, kc_ref, kp_ref, vc_ref, vp_ref) for kh in KV]
        s_all = [_dot_nt(ops[kh][0], ops[kh][1]) for kh in KV]
        ex = [_attn_exp(s_all[kh], bias_ref[kh], sink_ref[kh], tri, first) for kh in KV]
        lhs = [jnp.concatenate([_unfold(e, tri) for e, _ in ex[kh]], axis=1).astype(BF16) for kh in KV]
        o = [_dot(lhs[kh], ops[kh][2]) for kh in KV]
        den = [_dot(lhs[kh], ones) for kh in KV]
        for kh in KV:
            out = o[kh] / (den[kh] + jnp.where(lo, ex[kh][0][1], ex[kh][1][1]))
            for pr in range(P):
                t = kh * P + pr
                o_ref[:, t * LANES : (t + 1) * LANES] = out[pr * BLOCK : (pr + 1) * BLOCK, :].astype(BF16)

    return pl.pallas_call(
        body,
        name="attn_fwd",
        grid=(n_seq, nb),
        in_specs=_attn_specs(D, nb) + [_resident((N_KV_HEADS, R, 2 * BLOCK)), _resident((N_KV_HEADS, R, 2 * BLOCK))],
        out_specs=pl.BlockSpec((BLOCK, D), lambda b, i: (b * nb + i, 0)),
        out_shape=jax.ShapeDtypeStruct((T, D), BF16),
        compiler_params=_params(2),
    )(q16, kd, kd, vd, vd, bias, sink)


def attn_bwd(q16, kd, vd, do, bias, sink, *, seq, n_seq):
    T, D = q16.shape
    kvw2 = 2 * N_KV_HEADS * HEAD_DIM
    nb = seq // BLOCK
    G = D // HEAD_DIM // N_KV_HEADS
    P = G // 2
    R = P * BLOCK
    KV = range(N_KV_HEADS)

    def body(q_ref, kc_ref, kp_ref, vc_ref, vp_ref, do_ref, bias_ref, sink_ref,
             dq_ref, dkc_ref, dkp_ref, dvc_ref, dvp_ref, dsink_ref):
        first = pl.program_id(1) == 0

        @pl.when(jnp.logical_and(pl.program_id(0) == 0, first))
        def _():
            dsink_ref[...] = jnp.zeros_like(dsink_ref)

        lane, lo, tri = _attn_masks(R)
        r4 = lax.broadcasted_iota(jnp.int32, (4 * BLOCK, 2 * BLOCK), 0)
        c4 = lax.broadcasted_iota(jnp.int32, (4 * BLOCK, 2 * BLOCK), 1)
        ones = ((r4 < 2 * BLOCK) == (c4 < BLOCK)).astype(BF16)
        ops = [_attn_operands(kh, P, lo, q_ref, kc_ref, kp_ref, vc_ref, vp_ref) for kh in KV]
        do16 = [jnp.concatenate([do_ref[:, (kh * P + pr) * LANES : (kh * P + pr + 1) * LANES] for pr in range(P)], axis=0)
                for kh in KV]
        s_all = [_dot_nt(ops[kh][0], ops[kh][1]) for kh in KV]
        dp_all = [_dot_nt(do16[kh], ops[kh][2]) for kh in KV]
        ex = [_attn_exp(s_all[kh], bias_ref[kh], sink_ref[kh], tri, first) for kh in KV]
        den = [_dot(jnp.concatenate([_unfold(e, tri) for e, _ in ex[kh]], axis=1).astype(BF16), ones) for kh in KV]
        dsink = jnp.zeros((1, LANES), F32)
        pf, dsf = [], []
        for kh in KV:
            ps_, ds_ = [], []
            for par in range(2):
                e, es = ex[kh][par]
                inv = 1.0 / (den[kh][:, par * BLOCK : (par + 1) * BLOCK] + es)
                p = e * inv
                c0 = 2 * par * BLOCK
                dp = jnp.where(tri, dp_all[kh][:, c0 + BLOCK : c0 + 2 * BLOCK], dp_all[kh][:, c0 : c0 + BLOCK])
                delta = jnp.sum(p * dp, axis=-1, keepdims=True)
                ds_.append(_unfold(p * (dp - delta), tri))
                ps_.append(_unfold(p, tri))
                dsr = -((es * inv) * delta)
                for pr in range(P):
                    hq = kh * G + 2 * pr + par
                    tot = jnp.sum(dsr[pr * BLOCK : (pr + 1) * BLOCK, :], axis=0, keepdims=True)
                    dsink = dsink + jnp.where(lane == hq, tot, 0.0)
            pf.append(jnp.concatenate(ps_, axis=1).astype(BF16))
            dsf.append(jnp.concatenate(ds_, axis=1).astype(BF16))
        dq = [_dot(dsf[kh], ops[kh][1]) for kh in KV]
        dk = [_dot_tn(dsf[kh], ops[kh][0]) for kh in KV]
        dv = [_dot_tn(pf[kh], do16[kh]) for kh in KV]
        dsink_ref[0:1, :] += dsink
        for kh in KV:
            sl = slice(kh * LANES, (kh + 1) * LANES)
            for pr in range(P):
                t = kh * P + pr
                dq_ref[:, t * LANES : (t + 1) * LANES] = dq[kh][pr * BLOCK : (pr + 1) * BLOCK, :]
            for full, prev_ref, cur_ref in ((dk[kh], dkp_ref, dkc_ref), (dv[kh], dvp_ref, dvc_ref)):
                dup = jnp.where(lo, full[: 2 * BLOCK, :], full[2 * BLOCK :, :])
                prev_ref[:, sl] = dup[:BLOCK, :]
                cur_ref[:, sl] = dup[BLOCK:, :]

    cur = lambda b, i: (b * nb + i, 0)
    kv_spec = pl.BlockSpec((BLOCK, kvw2), cur)
    kv_shape = jax.ShapeDtypeStruct((T, kvw2), F32)
    return pl.pallas_call(
        body,
        name="attn_bwd",
        grid=(n_seq, nb),
        in_specs=_attn_specs(D, nb)
        + [pl.BlockSpec((BLOCK, D), cur), _resident((N_KV_HEADS, R, 2 * BLOCK)), _resident((N_KV_HEADS, R, 2 * BLOCK))],
        out_specs=[pl.BlockSpec((BLOCK, D), cur), kv_spec, kv_spec, kv_spec, kv_spec, pl.BlockSpec((8, LANES), lambda b, i: (0, 0))],
        out_shape=[jax.ShapeDtypeStruct((T, D), F32), kv_shape, kv_shape, kv_shape, kv_shape, jax.ShapeDtypeStruct((8, LANES), F32)],
        compiler_params=_params(2),
    )(q16, kd, kd, vd, vd, do, bias, sink)


def qkv_bwd(dq, dkc, dkp, dvc, dvp, qkv, dres, x, gain, w_qkv, qg, kg, *, seq):
    T, D = x.shape
    kvw2 = dkc.shape[1]
    kvw = kvw2 // 2
    nqt, nkt = D // LANES, kvw // LANES
    nb = seq // BLOCK
    n = T // BLOCK

    def body(dq_ref, dkc_ref, dkp_ref, dvc_ref, dvp_ref, qkv_ref, dres_ref, x_ref, g_ref, w_ref, qg_ref, kg_ref,
             dx_ref, dqkv_ref, h_ref, dg_ref, hg_ref):
        i = pl.program_id(0)

        @pl.when(i == 0)
        def _():
            dg_ref[...] = jnp.zeros_like(dg_ref)
            hg_ref[...] = jnp.zeros_like(hg_ref)

        lo = lax.broadcasted_iota(jnp.int32, (1, LANES), 1) < HEAD_DIM
        last = i % nb == nb - 1
        dkd = dkc_ref[...] + jnp.where(last, 0.0, dkp_ref[...])
        dvd = dvc_ref[...] + jnp.where(last, 0.0, dvp_ref[...])

        def undup(d, t):
            a, b = d[:, 2 * t * LANES : (2 * t + 1) * LANES], d[:, (2 * t + 1) * LANES : (2 * t + 2) * LANES]
            return jnp.where(lo, a + _swap_halves(a), b + _swap_halves(b))

        gq = jnp.zeros((1, LANES), F32)
        for t in range(nqt):
            sl = slice(t * LANES, (t + 1) * LANES)
            q = qkv_ref[:, sl]
            _, r = _head_norm(q, qg_ref[...], lo)
            dxq, dgn = _head_norm_bwd(q, r, qg_ref[...], dq_ref[:, sl] * SCALE, lo)
            dqkv_ref[:, sl] = dxq.astype(BF16)
            gq = gq + dgn
        gk = jnp.zeros((1, LANES), F32)
        for t in range(nkt):
            sl = slice(D + t * LANES, D + (t + 1) * LANES)
            k = qkv_ref[:, sl]
            _, r = _head_norm(k, kg_ref[...], lo)
            dxk, dgn = _head_norm_bwd(k, r, kg_ref[...], undup(dkd, t), lo)
            dqkv_ref[:, sl] = dxk.astype(BF16)
            gk = gk + dgn
            dqkv_ref[:, D + kvw + t * LANES : D + kvw + (t + 1) * LANES] = undup(dvd, t).astype(BF16)
        hg_ref[0:1, :] += gq
        hg_ref[1:2, :] += gk
        dh = _dot_nt(dqkv_ref[...], w_ref[...])
        xt = x_ref[...]
        r = _rms(xt)
        gn = g_ref[...]
        h_ref[...] = ((xt * r) * gn).astype(BF16)
        dx, dgn = _rms_bwd(xt, r, gn, dh)
        dg_ref[0:1, :] += dgn
        dx_ref[...] = dres_ref[...] + dx

    row = lambda i: (i, 0)
    nxt = lambda i: (jnp.minimum(i + 1, n - 1), 0)
    return pl.pallas_call(
        body,
        name="qkv_bwd",
        grid=(n,),
        in_specs=[
            pl.BlockSpec((BLOCK, D), row),
            pl.BlockSpec((BLOCK, kvw2), row),
            pl.BlockSpec((BLOCK, kvw2), nxt),
            pl.BlockSpec((BLOCK, kvw2), row),
            pl.BlockSpec((BLOCK, kvw2), nxt),
            pl.BlockSpec((BLOCK, D + kvw2), row),
            pl.BlockSpec((BLOCK, D), row),
            pl.BlockSpec((BLOCK, D), row),
            _resident((1, D)),
            _resident((D, D + kvw2)),
            _resident((1, LANES)),
            _resident((1, LANES)),
        ],
        out_specs=[
            pl.BlockSpec((BLOCK, D), row),
            pl.BlockSpec((BLOCK, D + kvw2), row),
            pl.BlockSpec((BLOCK, D), row),
            pl.BlockSpec((8, D), lambda i: (0, 0)),
            pl.BlockSpec((8, LANES), lambda i: (0, 0)),
        ],
        out_shape=[
            jax.ShapeDtypeStruct((T, D), F32),
            jax.ShapeDtypeStruct((T, D + kvw2), BF16),
            jax.ShapeDtypeStruct((T, D), BF16),
            jax.ShapeDtypeStruct((8, D), F32),
            jax.ShapeDtypeStruct((8, LANES), F32),
        ],
        compiler_params=_params(1),
    )(dq, dkc, dkp, dvc, dvp, qkv, dres, x, gain, w_qkv, qg, kg)


def loss_head(y, target, *, tm):
    T, D = y.shape

    def body(y_ref, t_ref, dy_ref, s_ref):
        @pl.when(pl.program_id(0) == 0)
        def _():
            s_ref[...] = jnp.zeros_like(s_ref)

        e = y_ref[...] - t_ref[...]
        dy_ref[...] = e * (1.0 / D)
        s_ref[...] += jnp.sum(jnp.sum(e * e, axis=-1, keepdims=True), axis=0, keepdims=True)

    return pl.pallas_call(
        body,
        name="loss_head",
        grid=(T // tm,),
        in_specs=[pl.BlockSpec((tm, D), lambda i: (i, 0)), pl.BlockSpec((tm, D), lambda i: (i, 0))],
        out_specs=[pl.BlockSpec((tm, D), lambda i: (i, 0)), pl.BlockSpec((8, LANES), lambda i: (0, 0))],
        out_shape=[jax.ShapeDtypeStruct((T, D), F32), jax.ShapeDtypeStruct((8, LANES), F32)],
        compiler_params=_params(1),
    )(y, target)


def local_step(x, target, gains, w, *, seq, tm=256, tk=2048):
    T, D = x.shape
    n_seq = T // seq
    nm, nf, qgain, kgain, sinks = gains
    H = D // HEAD_DIM
    tk = min(tk, T)
    qg2, kg2 = jnp.tile(qgain, (1, 2)), jnp.tile(kgain, (1, 2))
    bias, sinkcol = _attn_tables(sinks, H)

    x1, bcx = conv_fwd(x, nm[0:1], w["w_in"], w["cw"], w["w_out"], seq=seq, tm=tm)
    x2, gu0 = ffn_fwd(x1, nf[0:1], w["w_gu"][0], w["w_d"][0], tm=tm)
    qkv, q16, kd, vd = qkv_proj(x2, nm[1:2], w["w_qkv"], qg2, kg2, tm=tm)
    ao = attn_fwd(q16, kd, vd, bias, sinkcol, seq=seq, n_seq=n_seq)
    x3 = matmul_residual(ao, w["w_o"], x2, tm=tm)
    x4, gu1 = ffn_fwd(x3, nf[1:2], w["w_gu"][1], w["w_d"][1], tm=tm)
    dx4, sse = loss_head(x4, target, tm=tm)

    g = {}
    C = w["w_gu"][0].shape[2]

    def ffn_grads(dxo, xin, gain, gu, layer):
        dxi, a16, dgu, h16, d16, dgain = ffn_bwd(dxo, xin, gain, gu, w["w_gu"][layer], w["w_d"][layer], tm=tm)
        g["w_gu%d" % layer] = wgrad(h16, dgu, name="wgrad_gu", a_chunks=False, b_chunks=True, b_cols=0, tk=tk, out_dtype=BF16)
        g["w_d%d" % layer] = wgrad(a16, d16, name="wgrad_d", a_chunks=True, b_chunks=False, b_cols=0, tk=tk, out_dtype=BF16).reshape(N_DEV, C // 2, D)
        return dxi, dgain

    dx3, dnf1 = ffn_grads(dx4, x3, nf[1:2], gu1, 1)
    dao, dx3_16 = matmul_nt_cast(dx3, w["w_o"], tm=tm)
    g["w_o"] = wgrad(ao, dx3_16, name="wgrad_o", a_chunks=False, b_chunks=False, b_cols=0, tk=tk, out_dtype=BF16).reshape(N_DEV, D // N_DEV, D)
    dq, dkc, dkp, dvc, dvp, dsinks = attn_bwd(q16, kd, vd, dao, bias, sinkcol, seq=seq, n_seq=n_seq)
    dx2, dqkv16, h16, dnm1, dgains = qkv_bwd(dq, dkc, dkp, dvc, dvp, qkv, dx3, x2, nm[1:2], w["w_qkv"], qg2, kg2, seq=seq)
    nq = dqkv16.shape[1]
    gq = wgrad(h16, dqkv16, name="wgrad_qkv", a_chunks=False, b_chunks=False, b_cols=0, tk=tk, out_dtype=BF16)
    g["w_qkv"] = gq.reshape(D, N_DEV, nq // N_DEV).transpose(1, 0, 2)
    dx1, dnf0 = ffn_grads(dx2, x1, nf[0:1], gu0, 0)
    gx, dbcx, h16, z16, d16, dcw, dnm0 = conv_bwd(dx1, x, nm[0:1], bcx, w["cw"], w["w_in"], w["w_out"], seq=seq, tm=tm)
    g["w_in"] = wgrad(h16, dbcx, name="wgrad_in", a_chunks=False, b_chunks=False, b_cols=3 * D // N_DEV, tk=tk, out_dtype=BF16)
    g["w_out"] = wgrad(z16, d16, name="wgrad_out", a_chunks=False, b_chunks=False, b_cols=0, tk=tk, out_dtype=BF16).reshape(N_DEV, D // N_DEV, D)
    g["cw"] = dcw[0:3].reshape(3, N_DEV, D // N_DEV).transpose(1, 0, 2)
    small = dict(nm0=dnm0, nm1=dnm1, nf0=dnf0, nf1=dnf1, gains=dgains, sinks=dsinks)
    return sse, gx, g, small


def _place():
    x, y, c = lax.axis_index("x"), lax.axis_index("y"), lax.axis_index("c")
    return x, y, c


def _flip(v, bit):
    return 1 - v if bit else v


def _any_specs(n):
    return [pl.BlockSpec(memory_space=pl.ANY)] * n


def all_gather_weights(shards):
    nt = len(shards)

    def body(*refs):
        ins, outs = refs[:nt], refs[nt : 2 * nt]
        send_sems, recv_sems, loc_sems = refs[2 * nt :]
        x, y, c = _place()
        me = 4 * x + 2 * y + c
        sib = (x, y, 1 - c)
        chips = [(_flip(x, k >> 1), _flip(y, k & 1)) for k in (1, 2, 3)]

        def slot(px, py, pc):
            return 4 * px + 2 * py + pc

        def copy(t, k, src, dst_slot, to):
            return pltpu.make_async_remote_copy(
                src_ref=src, dst_ref=outs[t].at[dst_slot], send_sem=send_sems.at[t, k], recv_sem=recv_sems.at[t, k],
                device_id=to, device_id_type=MESH)

        started = []
        for t in range(nt):
            mine = pltpu.make_async_copy(ins[t], outs[t].at[me], loc_sems.at[t])
            mine.start()
            started.append(mine)
        sends = []
        for t in range(nt):
            sends.append(copy(t, 0, ins[t], me, sib))
            for j, (px, py) in enumerate(chips):
                sends.append(copy(t, 1 + j, ins[t], me, (px, py, c)))
        for cp in sends:
            cp.start()
        for t in range(nt):
            for j, (px, py) in enumerate(chips):
                s = slot(px, py, c)
                copy(t, 1 + j, outs[t].at[s], s, sib).wait_recv()
                fwd = copy(t, 4 + j, outs[t].at[s], s, sib)
                fwd.start()
                sends.append(fwd)
        for t in range(nt):
            s = slot(x, y, 1 - c)
            copy(t, 0, outs[t].at[s], s, sib).wait_recv()
            for j, (px, py) in enumerate(chips):
                s = slot(px, py, 1 - c)
                copy(t, 4 + j, outs[t].at[s], s, sib).wait_recv()
        for cp in sends:
            cp.wait_send()
        for cp in started:
            cp.wait()

    return pl.pallas_call(
        body,
        name="all_gather_weights",
        in_specs=_any_specs(nt),
        out_specs=_any_specs(nt),
        out_shape=[jax.ShapeDtypeStruct((N_DEV,) + s.shape, s.dtype) for s in shards],
        scratch_shapes=[pltpu.SemaphoreType.DMA((nt, 7)), pltpu.SemaphoreType.DMA((nt, 7)), pltpu.SemaphoreType.DMA((nt,))],
    )(*shards)


def exchange_cores(grads):
    nt = len(grads)

    def body(*refs):
        ins, outs = refs[:nt], refs[nt : 2 * nt]
        send_sems, recv_sems = refs[2 * nt :]
        x, y, c = _place()
        sib = (x, y, 1 - c)
        copies = []
        for t in range(nt):
            for p in range(N_CHIP):
                cp = pltpu.make_async_remote_copy(
                    src_ref=ins[t].at[2 * p + (1 - c)], dst_ref=outs[t].at[p], send_sem=send_sems.at[t, p],
                    recv_sem=recv_sems.at[t, p], device_id=sib, device_id_type=MESH)
                cp.start()
                copies.append(cp)
        for cp in copies:
            cp.wait_recv()
        for cp in copies:
            cp.wait_send()

    return pl.pallas_call(
        body,
        name="exchange_cores",
        in_specs=_any_specs(nt),
        out_specs=_any_specs(nt),
        out_shape=[jax.ShapeDtypeStruct((N_CHIP,) + g.shape[1:], g.dtype) for g in grads],
        scratch_shapes=[pltpu.SemaphoreType.DMA((nt, N_CHIP)), pltpu.SemaphoreType.DMA((nt, N_CHIP))],
    )(*grads)


def add_cores(mine, theirs):
    _, R, C = theirs.shape
    tr = R if R <= 512 else 512
    assert R % tr == 0

    def body(c_ref, a_ref, b_ref, o_ref):
        o_ref[...] = (a_ref[...].astype(F32) + b_ref[...].astype(F32)).astype(o_ref.dtype)

    core = lax.axis_index("c").astype(jnp.int32).reshape(1)
    return pl.pallas_call(
        body,
        name="add_cores",
        grid_spec=pltpu.PrefetchScalarGridSpec(
            num_scalar_prefetch=1,
            grid=(N_CHIP, R // tr),
            in_specs=[
                pl.BlockSpec((None, None, tr, C), lambda p, i, c_ref: (p, c_ref[0], i, 0)),
                pl.BlockSpec((None, tr, C), lambda p, i, c_ref: (p, i, 0)),
            ],
            out_specs=pl.BlockSpec((None, tr, C), lambda p, i, c_ref: (p, i, 0)),
        ),
        out_shape=jax.ShapeDtypeStruct(theirs.shape, mine.dtype),
        compiler_params=_params(2),
    )(core, mine.reshape(N_CHIP, 2, R, C), theirs)


def exchange_chips(sums, layout, small):
    nt = len(sums)
    bufs = {}
    for t, (b, layer) in enumerate(layout):
        shp = sums[t].shape[1:]
        if layer is None:
            bufs[b] = jax.ShapeDtypeStruct((N_CHIP,) + shp, sums[t].dtype)
        else:
            n_layers = 1 + max(l for bb, l in layout if bb == b)
            bufs[b] = jax.ShapeDtypeStruct((N_CHIP, n_layers) + shp, sums[t].dtype)
    nb = len(bufs)

    def body(*refs):
        ins, small_ref = refs[:nt], refs[nt]
        outs, small_out = refs[nt + 1 : nt + 1 + nb], refs[nt + 1 + nb]
        send_sems, recv_sems, loc_sems, ssend, srecv = refs[nt + 2 + nb :]
        x, y, c = _place()
        me = 4 * x + 2 * y + c
        p = 2 * x + y

        def land(t, src_chip):
            b, layer = layout[t]
            return outs[b].at[src_chip] if layer is None else outs[b].at[src_chip, layer]

        local = []
        for t in range(nt):
            cp = pltpu.make_async_copy(ins[t].at[p], land(t, p), loc_sems.at[t])
            cp.start()
            local.append(cp)
        cp = pltpu.make_async_copy(small_ref, small_out.at[me], loc_sems.at[nt])
        cp.start()
        local.append(cp)
        remote = []
        for k in (1, 2, 3):
            px, py = _flip(x, k >> 1), _flip(y, k & 1)
            q = 2 * px + py
            for t in range(nt):
                cp = pltpu.make_async_remote_copy(
                    src_ref=ins[t].at[q], dst_ref=land(t, p), send_sem=send_sems.at[t, k - 1],
                    recv_sem=recv_sems.at[t, k - 1], device_id=(px, py, c), device_id_type=MESH)
                cp.start()
                remote.append((cp, pltpu.make_async_remote_copy(
                    src_ref=ins[t].at[q], dst_ref=land(t, q), send_sem=send_sems.at[t, k - 1],
                    recv_sem=recv_sems.at[t, k - 1], device_id=(px, py, c), device_id_type=MESH)))
        for k in range(1, N_DEV):
            px, py, pc = _flip(x, (k >> 2) & 1), _flip(y, (k >> 1) & 1), _flip(c, k & 1)
            cp = pltpu.make_async_remote_copy(
                src_ref=small_ref, dst_ref=small_out.at[me], send_sem=ssend.at[k - 1], recv_sem=srecv.at[k - 1],
                device_id=(px, py, pc), device_id_type=MESH)
            cp.start()
            remote.append((cp, pltpu.make_async_remote_copy(
                src_ref=small_ref, dst_ref=small_out.at[4 * px + 2 * py + pc], send_sem=ssend.at[k - 1],
                recv_sem=srecv.at[k - 1], device_id=(px, py, pc), device_id_type=MESH)))
        for _, arrival in remote:
            arrival.wait_recv()
        for cp, _ in remote:
            cp.wait_send()
        for cp in local:
            cp.wait()

    res = pl.pallas_call(
        body,
        name="exchange_chips",
        in_specs=_any_specs(nt + 1),
        out_specs=_any_specs(nb + 1),
        out_shape=[bufs[b] for b in range(nb)] + [jax.ShapeDtypeStruct((N_DEV,) + small.shape, small.dtype)],
        scratch_shapes=[
            pltpu.SemaphoreType.DMA((nt, 3)),
            pltpu.SemaphoreType.DMA((nt, 3)),
            pltpu.SemaphoreType.DMA((nt + 1,)),
            pltpu.SemaphoreType.DMA((7,)),
            pltpu.SemaphoreType.DMA((7,)),
        ],
    )(*sums, small)
    return res[:nb], res[nb]


def _adamw_math(g, w, m, v):
    m = ADAM_B1 * m + (1.0 - ADAM_B1) * g
    v = ADAM_B2 * v + (1.0 - ADAM_B2) * (g * g)
    m_hat = m / (1.0 - ADAM_B1 ** ADAM_STEP)
    v_hat = v / (1.0 - ADAM_B2 ** ADAM_STEP)
    delta = -ADAM_LR * (m_hat / (jnp.sqrt(v_hat) + ADAM_EPS) + ADAM_WD * w)
    return delta, m, v


def adamw(parts, w, m, v, *, name):
    n, R, C = parts.shape
    tr = R
    for cand in (256, 128, 88, 64, 32, 16, 8):
        if R > cand and R % cand == 0:
            tr = cand
            break

    def body(p_ref, w_ref, m_ref, v_ref, g_ref, d_ref, mo_ref, vo_ref):
        g = p_ref[0].astype(F32)
        for s in range(1, n):
            g = g + p_ref[s].astype(F32)
        g_ref[...] = g
        d_ref[...], mo_ref[...], vo_ref[...] = _adamw_math(g, w_ref[...], m_ref[...], v_ref[...])

    blk = pl.BlockSpec((tr, C), lambda i: (i, 0))
    return pl.pallas_call(
        body,
        name=name,
        grid=(R // tr,),
        in_specs=[pl.BlockSpec((n, tr, C), lambda i: (0, i, 0)), blk, blk, blk],
        out_specs=[blk] * 4,
        out_shape=[jax.ShapeDtypeStruct((R, C), F32)] * 4,
        compiler_params=_params(1),
    )(parts, w, m, v)


def pack_small(small, D):
    W = max(D, 2 * LANES)

    def body(nm0, nm1, nf0, nf1, gains, sinks, o_ref):
        o_ref[...] = jnp.zeros_like(o_ref)
        o_ref[0:1, :D] = nm0[0:1, :]
        o_ref[1:2, :D] = nm1[0:1, :]
        o_ref[2:3, :D] = nf0[0:1, :]
        o_ref[3:4, :D] = nf1[0:1, :]
        gq = gains[0:1, :] + pltpu.roll(gains[0:1, :], HEAD_DIM, 1)
        gk = gains[1:2, :] + pltpu.roll(gains[1:2, :], HEAD_DIM, 1)
        lane = lax.broadcasted_iota(jnp.int32, (1, LANES), 1)
        o_ref[4:5, :LANES] = jnp.where(lane < HEAD_DIM, gq, gk)
        o_ref[4:5, LANES : 2 * LANES] = sinks[0:1, :]

    return pl.pallas_call(
        body,
        name="pack_small",
        out_shape=jax.ShapeDtypeStruct((8, W), F32),
    )(small["nm0"], small["nm1"], small["nf0"], small["nf1"], small["gains"], small["sinks"])


def _pack_small_params(nm, nf, qg, kg, sk, D):
    W = max(D, 2 * LANES)
    row4 = jnp.concatenate([qg.reshape(-1), kg.reshape(-1), jnp.zeros((LANES - 2 * HEAD_DIM,), F32), sk.reshape(-1)])
    row4 = jnp.pad(row4, (0, W - row4.shape[0]))
    rows = [jnp.pad(r, (0, W - D)) for r in (nm[0], nm[1], nf[0], nf[1])] + [row4]
    return jnp.concatenate([jnp.stack(rows), jnp.zeros((3, W), F32)], axis=0)


def _unpack_small(a, D, H):
    nm = a[0:2, :D]
    nf = a[2:4, :D]
    qg = a[4:5, 0:HEAD_DIM]
    kg = a[4:5, HEAD_DIM : 2 * HEAD_DIM]
    sk = a[4:5, LANES : LANES + H]
    return qg, kg, sk, nm, nf


def kernel(x, conv_w_in, conv_w, conv_w_out, attn_w_qkv, attn_q_gain, attn_k_gain, attn_sinks, attn_w_o, norm_mixer, norm_ffn, ffn_w_gate_up, ffn_w_down, loss_target, m_conv_w_in, m_conv_w, m_conv_w_out, m_attn_w_qkv, m_attn_q_gain, m_attn_k_gain, m_attn_sinks, m_attn_w_o, m_norm_mixer, m_norm_ffn, m_ffn_w_gate_up, m_ffn_w_down, v_conv_w_in, v_conv_w, v_conv_w_out, v_attn_w_qkv, v_attn_q_gain, v_attn_k_gain, v_attn_sinks, v_attn_w_o, v_norm_mixer, v_norm_ffn, v_ffn_w_gate_up, v_ffn_w_down):
    n_seq, seq, D = x.shape
    T = n_seq * seq
    H = D // HEAD_DIM
    L = ffn_w_gate_up.shape[0]
    C = ffn_w_gate_up.shape[2]

    shards = [conv_w_in[0].astype(BF16), conv_w[0], conv_w_out[0].astype(BF16), attn_w_qkv[0].astype(BF16),
              attn_w_o[0].astype(BF16)]
    shards += [ffn_w_gate_up[l].astype(BF16) for l in range(L)] + [ffn_w_down[l].astype(BF16) for l in range(L)]
    full = all_gather_weights(shards)
    nq = attn_w_qkv.shape[2] * N_DEV
    w = dict(
        w_in=full[0].transpose(1, 0, 2).reshape(D, 3 * D),
        cw=full[1].transpose(1, 0, 2).reshape(3, D),
        w_out=full[2].reshape(D, D),
        w_qkv=full[3].transpose(1, 0, 2).reshape(D, nq),
        w_o=full[4].reshape(D, D),
        w_gu=[full[5 + l] for l in range(L)],
        w_d=[full[5 + L + l].reshape(-1, D) for l in range(L)],
    )
    gains = (norm_mixer, norm_ffn, attn_q_gain, attn_k_gain, attn_sinks)
    sse, gx, g, small = local_step(x.reshape(T, D), loss_target.reshape(T, D), gains, w, seq=seq)
    loss = lax.psum(sse[0, 0] * (0.5 / D), ("x", "y", "c"))

    names = ["w_in", "cw", "w_out", "w_qkv", "w_o"] + ["w_gu%d" % l for l in range(L)] + ["w_d%d" % l for l in range(L)]
    grads = [g[k] for k in names]
    theirs = exchange_cores(grads)
    sums = [add_cores(a, b) for a, b in zip(grads, theirs)]
    layout = [(0, None), (1, None), (2, None), (3, None), (4, None)] + [(5, l) for l in range(L)] + [(6, l) for l in range(L)]
    bufs, small_all = exchange_chips(sums, layout, pack_small(small, D))

    def flat(a):
        return a.reshape(-1, a.shape[-1])

    big = [conv_w_in, conv_w, conv_w_out, attn_w_qkv, attn_w_o, ffn_w_gate_up, ffn_w_down]
    big_m = [m_conv_w_in, m_conv_w, m_conv_w_out, m_attn_w_qkv, m_attn_w_o, m_ffn_w_gate_up, m_ffn_w_down]
    big_v = [v_conv_w_in, v_conv_w, v_conv_w_out, v_attn_w_qkv, v_attn_w_o, v_ffn_w_gate_up, v_ffn_w_down]
    tags = ["in", "cw", "out", "qkv", "o", "gu", "d"]
    res = []
    for b in range(7):
        parts = bufs[b].reshape(N_CHIP, -1, bufs[b].shape[-1])
        outs = adamw(parts, flat(big[b]), flat(big_m[b]), flat(big_v[b]), name="adamw_" + tags[b])
        res.append([o.reshape(big[b].shape) for o in outs])
    sw = _pack_small_params(norm_mixer, norm_ffn, attn_q_gain, attn_k_gain, attn_sinks, D)
    sm = _pack_small_params(m_norm_mixer, m_norm_ffn, m_attn_q_gain, m_attn_k_gain, m_attn_sinks, D)
    sv = _pack_small_params(v_norm_mixer, v_norm_ffn, v_attn_q_gain, v_attn_k_gain, v_attn_sinks, D)
    souts = adamw(small_all, sw, sm, sv, name="adamw_small")
    sres = [_unpack_small(o, D, H) for o in souts]

    def ordered(i):
        r, s = [r[i] for r in res], sres[i]
        return [r[0], r[1], r[2], r[3], s[0], s[1], s[2], r[4], s[3], s[4], r[5], r[6]]

    return (loss, gx.reshape(n_seq, seq, D), *ordered(0), *ordered(1), *ordered(2), *ordered(3))
```

```python
import functools
import math

import jax
import jax.numpy as jnp
from jax import lax
from jax.experimental import pallas as pl
from jax.experimental.pallas import tpu as pltpu

F32 = jnp.float32
BF16 = jnp.bfloat16

EPS = 1e-6
HEAD_DIM = 64
N_KV_HEADS = 4
BLOCK = 128
LANES = 128
N_DEV = 8
NEG = -1e30
SCALE = 1.0 / math.sqrt(HEAD_DIM)

ADAM_LR = 0.001
ADAM_B1 = 0.9
ADAM_B2 = 0.999
ADAM_EPS = 1e-08
ADAM_WD = 0.01
ADAM_STEP = 10

V7X_VMEM_BYTES = 64 * 1024 * 1024
VMEM_LIMIT = V7X_VMEM_BYTES - 8 * 1024 * 1024
MESH = pl.DeviceIdType.MESH

_NT = (((1,), (1,)), ((), ()))
_TN = (((0,), (0,)), ((), ()))


def _params(n_grid):
    return pltpu.CompilerParams(dimension_semantics=("arbitrary",) * n_grid, vmem_limit_bytes=VMEM_LIMIT)


def _resident(shape):
    nd = len(shape)
    return pl.BlockSpec(shape, lambda *_: (0,) * nd, pipeline_mode=pl.Buffered(1))


def _rms(x):
    return lax.rsqrt(jnp.mean(x * x, axis=-1, keepdims=True) + EPS)


def _rms_bwd(x, r, gain, dh):
    xn = x * r
    dxn = dh * gain
    dx = r * (dxn - xn * jnp.mean(dxn * xn, axis=-1, keepdims=True))
    return dx, jnp.sum(dh * xn, axis=0, keepdims=True)


def _dot(a, b):
    return jnp.dot(a, b, preferred_element_type=F32)


def _dot_nt(a, b):
    return lax.dot_general(a, b, _NT, preferred_element_type=F32)


def _dot_tn(a, b):
    return lax.dot_general(a, b, _TN, preferred_element_type=F32)


def _place():
    return lax.axis_index("x"), lax.axis_index("y"), lax.axis_index("c")


def _flip(v, bit):
    return 1 - v if bit else v


def _slot(px, py, pc):
    return 4 * px + 2 * py + pc


class _Gather:
    def __init__(self, shards):
        nt = len(shards)
        self.nt = nt
        self.inputs = list(shards)
        self.out_shapes = [jax.ShapeDtypeStruct((N_DEV,) + s.shape, s.dtype) for s in shards]
        self.scratch = [pltpu.SemaphoreType.DMA((nt, 7)), pltpu.SemaphoreType.DMA((nt, 7)), pltpu.SemaphoreType.DMA((nt,))]
        self.aliases = {}

    def phases(self, total):
        assert total >= 3
        return [(0, self.start), (total - 2, self.forward), (total - 1, self.finish)]

    def _copies(self, ins, outs, sems):
        send_sems, recv_sems, loc_sems = sems
        x, y, c = _place()
        me = _slot(x, y, c)
        sib = (x, y, 1 - c)
        chips = [(_flip(x, k >> 1), _flip(y, k & 1)) for k in (1, 2, 3)]

        def copy(t, k, src, dst_slot, to):
            return pltpu.make_async_remote_copy(
                src_ref=src, dst_ref=outs[t].at[dst_slot], send_sem=send_sems.at[t, k], recv_sem=recv_sems.at[t, k],
                device_id=to, device_id_type=MESH)

        local = [pltpu.make_async_copy(ins[t], outs[t].at[me], loc_sems.at[t]) for t in range(self.nt)]
        first, passed, arrive_ici, arrive_sib = [], [], [], []
        for t in range(self.nt):
            first.append(copy(t, 0, ins[t], me, sib))
            s = _slot(x, y, 1 - c)
            arrive_sib.append(copy(t, 0, outs[t].at[s], s, sib))
            for j, (px, py) in enumerate(chips):
                first.append(copy(t, 1 + j, ins[t], me, (px, py, c)))
                s = _slot(px, py, c)
                arrive_ici.append(copy(t, 1 + j, outs[t].at[s], s, sib))
                passed.append(copy(t, 4 + j, outs[t].at[s], s, sib))
                s = _slot(px, py, 1 - c)
                arrive_sib.append(copy(t, 4 + j, outs[t].at[s], s, sib))
        return local, first, passed, arrive_ici, arrive_sib

    def start(self, ins, outs, sems):
        local, first, _, _, _ = self._copies(ins, outs, sems)
        for cp in local + first:
            cp.start()

    def forward(self, ins, outs, sems):
        _, _, passed, arrive_ici, _ = self._copies(ins, outs, sems)
        for arrival, fwd in zip(arrive_ici, passed):
            arrival.wait_recv()
            fwd.start()

    def finish(self, ins, outs, sems):
        local, first, passed, _, arrive_sib = self._copies(ins, outs, sems)
        for cp in arrive_sib:
            cp.wait_recv()
        for cp in first + passed:
            cp.wait_send()
        for cp in local:
            cp.wait()


class _Scatter:
    def __init__(self, items):
        self.items = items
        nt = len(items)
        self.nt = nt
        reused = [(t, it[1]) for t, it in enumerate(items) if not isinstance(it[1], jax.ShapeDtypeStruct)]
        self.inputs = [it[0] for it in items] + [land for _, land in reused]
        self.out_shapes = [jax.ShapeDtypeStruct(it[1].shape, it[1].dtype) for it in items]
        self.aliases = {nt + i: t for i, (t, _) in enumerate(reused)}
        self.scratch = [pltpu.SemaphoreType.DMA((nt, 7)), pltpu.SemaphoreType.DMA((nt, 7)), pltpu.SemaphoreType.DMA((nt,))]

    def phases(self, total):
        assert total >= 2
        return [(0, self.start), (total - 1, self.finish)]

    def _copies(self, ins, outs, sems):
        send_sems, recv_sems, loc_sems = sems
        x, y, c = _place()
        me = _slot(x, y, c)

        def land(t, s):
            layer = self.items[t][2]
            return outs[t].at[s] if layer is None else outs[t].at[s, layer]

        def src(t, s):
            return ins[t] if self.items[t][3] else ins[t].at[s]

        local = [pltpu.make_async_copy(src(t, me), land(t, me), loc_sems.at[t]) for t in range(self.nt)]
        sends, arrivals = [], []
        for k in range(1, N_DEV):
            px, py, pc = _flip(x, (k >> 2) & 1), _flip(y, (k >> 1) & 1), _flip(c, k & 1)
            peer = _slot(px, py, pc)
            for t in range(self.nt):
                for dst, into in ((land(t, me), sends), (land(t, peer), arrivals)):
                    into.append(pltpu.make_async_remote_copy(
                        src_ref=src(t, peer), dst_ref=dst, send_sem=send_sems.at[t, k - 1], recv_sem=recv_sems.at[t, k - 1],
                        device_id=(px, py, pc), device_id_type=MESH))
        return local, sends, arrivals

    def start(self, ins, outs, sems):
        local, sends, _ = self._copies(ins, outs, sems)
        for cp in local + sends:
            cp.start()

    def finish(self, ins, outs, sems):
        local, sends, arrivals = self._copies(ins, outs, sems)
        for cp in arrivals:
            cp.wait_recv()
        for cp in sends:
            cp.wait_send()
        for cp in local:
            cp.wait()


def _any_specs(n):
    return [pl.BlockSpec(memory_space=pl.ANY)] * n


def run_plan(plan, *, name):
    def body(*refs):
        n_in, n_out = len(plan.inputs), len(plan.out_shapes)
        ins, outs, sems = refs[:n_in], refs[n_in : n_in + n_out], refs[n_in + n_out :]
        for _, phase in plan.phases(3):
            phase(ins, outs, sems)

    return pl.pallas_call(
        body,
        name=name,
        in_specs=_any_specs(len(plan.inputs)),
        out_specs=_any_specs(len(plan.out_shapes)),
        out_shape=plan.out_shapes,
        scratch_shapes=plan.scratch,
        input_output_aliases=plan.aliases,
    )(*plan.inputs)


def _call(body, *, name, grid, in_specs, out_specs, out_shape, args, scratch=(), plan=None):
    n_in, n_out, n_scr = len(in_specs), len(out_specs), len(scratch)
    if plan is None:
        outs = pl.pallas_call(
            body, name=name, grid=grid, in_specs=in_specs, out_specs=out_specs, out_shape=out_shape,
            scratch_shapes=list(scratch), compiler_params=_params(len(grid)))(*args)
        return outs, None
    c_in, c_out = len(plan.inputs), len(plan.out_shapes)
    phases = plan.phases(math.prod(grid))

    def full(*refs):
        a, refs = refs[:n_in], refs[n_in:]
        ci, refs = refs[:c_in], refs[c_in:]
        o, refs = refs[:n_out], refs[n_out:]
        co, refs = refs[:c_out], refs[c_out:]
        s, cs = refs[:n_scr], refs[n_scr:]
        step = pl.program_id(0)
        for d in range(1, len(grid)):
            step = step * grid[d] + pl.program_id(d)
        for at, phase in phases:
            if at == 0:
                pl.when(step == 0)(functools.partial(phase, ci, co, cs))
        body(*a, *o, *s)
        for at, phase in phases:
            if at > 0:
                pl.when(step == at)(functools.partial(phase, ci, co, cs))

    outs = pl.pallas_call(
        full,
        name=name,
        grid=grid,
        in_specs=list(in_specs) + _any_specs(c_in),
        out_specs=list(out_specs) + _any_specs(c_out),
        out_shape=list(out_shape) + plan.out_shapes,
        scratch_shapes=list(scratch) + plan.scratch,
        input_output_aliases={n_in + i: n_out + t for i, t in plan.aliases.items()},
        compiler_params=_params(len(grid)),
    )(*args, *plan.inputs)
    return outs[:n_out], outs[n_out:]


def _shift_down(u, prev8, row, n):
    out = pltpu.roll(u, n, 0)
    for k in range(n):
        out = jnp.where(row == k, prev8[8 - n + k : 8 - n + k + 1, :], out)
    return out


def _shift_up(u, next8, row, n, tm):
    out = pltpu.roll(u, tm - n, 0)
    for k in range(n):
        out = jnp.where(row == tm - n + k, next8[k : k + 1, :], out)
    return out


def conv_fwd(x, gain, w_in, cw, w_out, *, seq, tm, plan=None):
    T, D = x.shape
    tps = seq // tm

    def body(x_ref, g_ref, win_ref, cw_ref, wout_ref, x1_ref, bcx_ref, carry_ref):
        i = pl.program_id(0)

        @pl.when(i % tps == 0)
        def _():
            carry_ref[...] = jnp.zeros_like(carry_ref)

        xt = x_ref[...]
        h = ((xt * _rms(xt)) * g_ref[...]).astype(BF16)
        bcx = _dot(h, win_ref[...])
        bcx_ref[...] = bcx
        b, c, xv = bcx[:, :D], bcx[:, D : 2 * D], bcx[:, 2 * D :]
        u = b * xv
        row = lax.broadcasted_iota(jnp.int32, u.shape, 0)
        prev = carry_ref[...]
        u1 = _shift_down(u, prev, row, 1)
        u2 = _shift_down(u, prev, row, 2)
        carry_ref[...] = u[tm - 8 :, :]
        cwv = cw_ref[...]
        y = cwv[0:1, :] * u2 + cwv[1:2, :] * u1 + cwv[2:3, :] * u
        z = (c * y).astype(BF16)
        x1_ref[...] = xt + _dot(z, wout_ref[...])

    return _call(
        body,
        plan=plan,
        args=(x, gain, w_in, cw, w_out),
        name="conv_fwd",
        grid=(T // tm,),
        in_specs=[
            pl.BlockSpec((tm, D), lambda i: (i, 0)),
            _resident((1, D)),
            _resident((D, 3 * D)),
            _resident((3, D)),
            _resident((D, D)),
        ],
        out_specs=[pl.BlockSpec((tm, D), lambda i: (i, 0)), pl.BlockSpec((tm, 3 * D), lambda i: (i, 0))],
        out_shape=[jax.ShapeDtypeStruct((T, D), F32), jax.ShapeDtypeStruct((T, 3 * D), F32)],
        scratch=[pltpu.VMEM((8, D), F32)],
    )


def conv_bwd(dx1, x, gain, bcx, cw, w_in, w_out, *, seq, tm, plan=None):
    T, D = x.shape
    n = T // tm
    tps = seq // tm
    r8 = tm // 8

    def body(d_ref, x_ref, g_ref, bcx_ref, halo_ref, cw_ref, win_ref, wout_ref,
             gx_ref, dbcx_ref, h_ref, z_ref, d16_ref, dcw_ref, dg_ref, carry_ref):
        i = pl.program_id(0)
        t = n - 1 - i

        @pl.when(i == 0)
        def _():
            dcw_ref[...] = jnp.zeros_like(dcw_ref)
            dg_ref[...] = jnp.zeros_like(dg_ref)

        @pl.when(t % tps == tps - 1)
        def _():
            carry_ref[...] = jnp.zeros_like(carry_ref)

        d = d_ref[...]
        d16 = d.astype(BF16)
        d16_ref[...] = d16
        dz = _dot_nt(d16, wout_ref[...])
        bcx = bcx_ref[...]
        b, c, xv = bcx[:, :D], bcx[:, D : 2 * D], bcx[:, 2 * D :]
        u = b * xv
        halo = halo_ref[...]
        hu = jnp.where(t % tps == 0, 0.0, halo[:, :D] * halo[:, 2 * D :])
        row = lax.broadcasted_iota(jnp.int32, u.shape, 0)
        u1 = _shift_down(u, hu, row, 1)
        u2 = _shift_down(u, hu, row, 2)
        cwv = cw_ref[...]
        y = cwv[0:1, :] * u2 + cwv[1:2, :] * u1 + cwv[2:3, :] * u
        z_ref[...] = (c * y).astype(BF16)
        dc = dz * y
        dy = dz * c
        dcw_ref[0:1, :] += jnp.sum(dy * u2, axis=0, keepdims=True)
        dcw_ref[1:2, :] += jnp.sum(dy * u1, axis=0, keepdims=True)
        dcw_ref[2:3, :] += jnp.sum(dy * u, axis=0, keepdims=True)
        nxt = carry_ref[...]
        dy1 = _shift_up(dy, nxt, row, 1, tm)
        dy2 = _shift_up(dy, nxt, row, 2, tm)
        carry_ref[...] = dy[0:8, :]
        du = cwv[2:3, :] * dy + cwv[1:2, :] * dy1 + cwv[0:1, :] * dy2
        dbcx_ref[:, :D] = (du * xv).astype(BF16)
        dbcx_ref[:, D : 2 * D] = dc.astype(BF16)
        dbcx_ref[:, 2 * D :] = (du * b).astype(BF16)
        dh = _dot_nt(dbcx_ref[...], win_ref[...])
        xt = x_ref[...]
        r = _rms(xt)
        gn = g_ref[...]
        h_ref[...] = ((xt * r) * gn).astype(BF16)
        dx, dgn = _rms_bwd(xt, r, gn, dh)
        dg_ref[0:1, :] += dgn
        gx_ref[...] = d + dx

    rev = lambda i: (n - 1 - i, 0)
    return _call(
        body,
        plan=plan,
        args=(dx1, x, gain, bcx, bcx, cw, w_in, w_out),
        name="conv_bwd",
        grid=(n,),
        in_specs=[
            pl.BlockSpec((tm, D), rev),
            pl.BlockSpec((tm, D), rev),
            _resident((1, D)),
            pl.BlockSpec((tm, 3 * D), rev),
            pl.BlockSpec((8, 3 * D), lambda i: (jnp.maximum((n - 1 - i) * r8 - 1, 0), 0)),
            _resident((3, D)),
            _resident((D, 3 * D)),
            _resident((D, D)),
        ],
        out_specs=[
            pl.BlockSpec((tm, D), rev),
            pl.BlockSpec((tm, 3 * D), rev),
            pl.BlockSpec((tm, D), rev),
            pl.BlockSpec((tm, D), rev),
            pl.BlockSpec((tm, D), rev),
            pl.BlockSpec((8, D), lambda i: (0, 0)),
            pl.BlockSpec((8, D), lambda i: (0, 0)),
        ],
        out_shape=[
            jax.ShapeDtypeStruct((T, D), F32),
            jax.ShapeDtypeStruct((T, 3 * D), BF16),
            jax.ShapeDtypeStruct((T, D), BF16),
            jax.ShapeDtypeStruct((T, D), BF16),
            jax.ShapeDtypeStruct((T, D), BF16),
            jax.ShapeDtypeStruct((8, D), F32),
            jax.ShapeDtypeStruct((8, D), F32),
        ],
        scratch=[pltpu.VMEM((8, D), F32)],
    )


def _sigmoid(g):
    return 1.0 / (1.0 + jnp.exp(-g))


def ffn_fwd(x, gain, w_gu, w_d, *, tm, plan=None):
    T, D = x.shape
    _, _, C = w_gu.shape
    half = N_DEV // 2

    def body(x_ref, g_ref, wgu_ref, wd_ref, xo_ref, gu_ref):
        xt = x_ref[...]
        h = ((xt * _rms(xt)) * g_ref[...]).astype(BF16)
        acc = xt
        for j in range(half):
            g = _dot(h, wgu_ref[j])
            u = _dot(h, wgu_ref[j + half])
            gu_ref[j] = g
            gu_ref[j + half] = u
            a = ((g * _sigmoid(g)) * u).astype(BF16)
            acc = acc + _dot(a, wd_ref[j * C : (j + 1) * C, :])
        xo_ref[...] = acc

    return _call(
        body,
        plan=plan,
        args=(x, gain, w_gu, w_d),
        name="ffn_fwd",
        grid=(T // tm,),
        in_specs=[
            pl.BlockSpec((tm, D), lambda i: (i, 0)),
            _resident((1, D)),
            _resident((N_DEV, D, C)),
            _resident((half * C, D)),
        ],
        out_specs=[pl.BlockSpec((tm, D), lambda i: (i, 0)), pl.BlockSpec((N_DEV, tm, C), lambda i: (0, i, 0))],
        out_shape=[jax.ShapeDtypeStruct((T, D), F32), jax.ShapeDtypeStruct((N_DEV, T, C), F32)],
    )


def ffn_bwd(dxo, x, gain, gu, w_gu, w_d, *, tm, plan=None):
    T, D = x.shape
    _, _, C = w_gu.shape
    half = N_DEV // 2

    def body(d_ref, x_ref, g_ref, gu_ref, wgu_ref, wd_ref, dx_ref, a_ref, dgu_ref, h_ref, d16_ref, dg_ref):
        @pl.when(pl.program_id(0) == 0)
        def _():
            dg_ref[...] = jnp.zeros_like(dg_ref)

        d = d_ref[...]
        d16 = d.astype(BF16)
        d16_ref[...] = d16
        dh = jnp.zeros((tm, D), F32)
        for j in range(half):
            g = gu_ref[j]
            u = gu_ref[j + half]
            da = _dot_nt(d16, wd_ref[j * C : (j + 1) * C, :])
            s = _sigmoid(g)
            sg = g * s
            a_ref[j] = (sg * u).astype(BF16)
            dg16 = (da * u * (s + sg * (1.0 - s))).astype(BF16)
            du16 = (da * sg).astype(BF16)
            dgu_ref[j] = dg16
            dgu_ref[j + half] = du16
            dh = dh + _dot_nt(dg16, wgu_ref[j]) + _dot_nt(du16, wgu_ref[j + half])
        xt = x_ref[...]
        r = _rms(xt)
        gn = g_ref[...]
        h_ref[...] = ((xt * r) * gn).astype(BF16)
        dx, dgn = _rms_bwd(xt, r, gn, dh)
        dg_ref[0:1, :] += dgn
        dx_ref[...] = d + dx

    return _call(
        body,
        plan=plan,
        args=(dxo, x, gain, gu, w_gu, w_d),
        name="ffn_bwd",
        grid=(T // tm,),
        in_specs=[
            pl.BlockSpec((tm, D), lambda i: (i, 0)),
            pl.BlockSpec((tm, D), lambda i: (i, 0)),
            _resident((1, D)),
            pl.BlockSpec((N_DEV, tm, C), lambda i: (0, i, 0)),
            _resident((N_DEV, D, C)),
            _resident((half * C, D)),
        ],
        out_specs=[
            pl.BlockSpec((tm, D), lambda i: (i, 0)),
            pl.BlockSpec((half, tm, C), lambda i: (0, i, 0)),
            pl.BlockSpec((N_DEV, tm, C), lambda i: (0, i, 0)),
            pl.BlockSpec((tm, D), lambda i: (i, 0)),
            pl.BlockSpec((tm, D), lambda i: (i, 0)),
            pl.BlockSpec((8, D), lambda i: (0, 0)),
        ],
        out_shape=[
            jax.ShapeDtypeStruct((T, D), F32),
            jax.ShapeDtypeStruct((half, T, C), BF16),
            jax.ShapeDtypeStruct((N_DEV, T, C), BF16),
            jax.ShapeDtypeStruct((T, D), BF16),
            jax.ShapeDtypeStruct((T, D), BF16),
            jax.ShapeDtypeStruct((8, D), F32),
        ],
    )


def matmul_residual(a, w, res, *, tm):
    T, K = a.shape
    N = w.shape[1]

    def body(a_ref, w_ref, r_ref, o_ref):
        o_ref[...] = r_ref[...] + _dot(a_ref[...], w_ref[...])

    return pl.pallas_call(
        body,
        name="matmul_residual",
        grid=(T // tm,),
        in_specs=[pl.BlockSpec((tm, K), lambda i: (i, 0)), _resident((K, N)), pl.BlockSpec((tm, N), lambda i: (i, 0))],
        out_specs=pl.BlockSpec((tm, N), lambda i: (i, 0)),
        out_shape=jax.ShapeDtypeStruct((T, N), F32),
        compiler_params=_params(1),
    )(a, w, res)


def matmul_nt_cast(d, w, *, tm):
    T, N = d.shape
    K = w.shape[0]

    def body(d_ref, w_ref, o_ref, d16_ref):
        d16 = d_ref[...].astype(BF16)
        d16_ref[...] = d16
        o_ref[...] = _dot_nt(d16, w_ref[...]).astype(BF16)

    return pl.pallas_call(
        body,
        name="matmul_nt_cast",
        grid=(T // tm,),
        in_specs=[pl.BlockSpec((tm, N), lambda i: (i, 0)), _resident((K, N))],
        out_specs=[pl.BlockSpec((tm, K), lambda i: (i, 0)), pl.BlockSpec((tm, N), lambda i: (i, 0))],
        out_shape=[jax.ShapeDtypeStruct((T, K), BF16), jax.ShapeDtypeStruct((T, N), BF16)],
        compiler_params=_params(1),
    )(d, w)


def wgrad(a, b, *, name, a_chunks=False, b_chunks=False, b_cols=0, tk, out_dtype=BF16, plan=None):
    T, K = a.shape[-2:]
    J = 1
    if a_chunks:
        J = a.shape[0]
        a_spec = pl.BlockSpec((None, tk, K), lambda j, k: (j, k, 0))
    else:
        a_spec = pl.BlockSpec((tk, K), lambda j, k: (k, 0))
    if b_chunks:
        J, _, N = b.shape
        b_spec = pl.BlockSpec((None, tk, N), lambda j, k: (j, k, 0))
    elif b_cols:
        N = b_cols
        J = b.shape[1] // N
        b_spec = pl.BlockSpec((tk, N), lambda j, k: (k, j))
    else:
        N = b.shape[1]
        b_spec = pl.BlockSpec((tk, N), lambda j, k: (k, 0))
    nk = T // tk

    def body(a_ref, b_ref, o_ref, acc_ref):
        k = pl.program_id(1)

        @pl.when(k == 0)
        def _():
            acc_ref[...] = jnp.zeros_like(acc_ref)

        acc_ref[...] += _dot_tn(a_ref[...], b_ref[...])

        @pl.when(k == nk - 1)
        def _():
            o_ref[...] = acc_ref[...].astype(out_dtype)

    outs, sent = _call(
        body,
        plan=plan,
        args=(a, b),
        name=name,
        grid=(J, nk),
        in_specs=[a_spec, b_spec],
        out_specs=[pl.BlockSpec((None, K, N), lambda j, k: (j, 0, 0))],
        out_shape=[jax.ShapeDtypeStruct((J, K, N), out_dtype)],
        scratch=[pltpu.VMEM((K, N), F32)],
    )
    return outs[0], sent


def _seg(x, lo):
    s_lo = jnp.sum(jnp.where(lo, x, 0.0), axis=-1, keepdims=True)
    s_hi = jnp.sum(jnp.where(lo, 0.0, x), axis=-1, keepdims=True)
    return jnp.where(lo, s_lo, s_hi)


def _head_norm(x, gain, lo):
    r = lax.rsqrt(_seg(x * x, lo) * (1.0 / HEAD_DIM) + EPS)
    return (x * r) * gain, r


def _head_norm_bwd(x, r, gain, dy, lo):
    xn = x * r
    dxn = dy * gain
    dx = r * (dxn - xn * (_seg(dxn * xn, lo) * (1.0 / HEAD_DIM)))
    return dx, jnp.sum(dy * xn, axis=0, keepdims=True)


def _swap_halves(x):
    return pltpu.roll(x, HEAD_DIM, 1)


def qkv_proj(x, gain, w, qg, kg, *, tm):
    T, D = x.shape
    N = w.shape[1]
    kvw = N_KV_HEADS * HEAD_DIM
    nqt, nkt = D // LANES, kvw // LANES

    def body(x_ref, g_ref, w_ref, qg_ref, kg_ref, qkv_ref, q_ref, kd_ref, vd_ref):
        xt = x_ref[...]
        h = ((xt * _rms(xt)) * g_ref[...]).astype(BF16)
        qkv = _dot(h, w_ref[...])
        qkv_ref[...] = qkv
        lo = lax.broadcasted_iota(jnp.int32, (1, LANES), 1) < HEAD_DIM
        for t in range(nqt):
            qn, _ = _head_norm(qkv[:, t * LANES : (t + 1) * LANES], qg_ref[...], lo)
            q_ref[:, t * LANES : (t + 1) * LANES] = (qn * SCALE).astype(BF16)
        for t in range(nkt):
            kn, _ = _head_norm(qkv[:, D + t * LANES : D + (t + 1) * LANES], kg_ref[...], lo)
            v = qkv[:, D + kvw + t * LANES : D + kvw + (t + 1) * LANES]
            for src, dst in ((kn, kd_ref), (v, vd_ref)):
                sw = _swap_halves(src)
                dst[:, 2 * t * LANES : (2 * t + 1) * LANES] = jnp.where(lo, src, sw).astype(BF16)
                dst[:, (2 * t + 1) * LANES : (2 * t + 2) * LANES] = jnp.where(lo, sw, src).astype(BF16)

    row = lambda i: (i, 0)
    return pl.pallas_call(
        body,
        name="qkv_proj",
        grid=(T // tm,),
        in_specs=[pl.BlockSpec((tm, D), row), _resident((1, D)), _resident((D, N)), _resident((1, LANES)), _resident((1, LANES))],
        out_specs=[pl.BlockSpec((tm, N), row), pl.BlockSpec((tm, D), row), pl.BlockSpec((tm, 2 * kvw), row), pl.BlockSpec((tm, 2 * kvw), row)],
        out_shape=[
            jax.ShapeDtypeStruct((T, N), F32),
            jax.ShapeDtypeStruct((T, D), BF16),
            jax.ShapeDtypeStruct((T, 2 * kvw), BF16),
            jax.ShapeDtypeStruct((T, 2 * kvw), BF16),
        ],
        compiler_params=_params(1),
    )(x, gain, w, qg, kg)


def _attn_tables(sinks, n_q_heads):
    P = n_q_heads // N_KV_HEADS // 2
    h = jnp.arange(1, n_q_heads + 1, dtype=F32)
    slopes = jnp.exp2(-8.0 * h / n_q_heads).reshape(N_KV_HEADS, P, 1, 2, 1)
    qi = jnp.arange(BLOCK)[:, None]
    kj = jnp.arange(BLOCK)[None, :]
    dist = jnp.where(kj <= qi, qi - kj, qi + BLOCK - kj).astype(F32)
    shape = (N_KV_HEADS, P, BLOCK, 2, BLOCK)
    bias = jnp.broadcast_to(-slopes * dist[None, None, :, None, :], shape)
    sink = jnp.broadcast_to(sinks.astype(F32).reshape(N_KV_HEADS, P, 1, 2, 1), shape)
    return bias.reshape(N_KV_HEADS, P * BLOCK, 2 * BLOCK), sink.reshape(N_KV_HEADS, P * BLOCK, 2 * BLOCK)


def _attn_specs(D, nb):
    kvw2 = 2 * N_KV_HEADS * HEAD_DIM
    cur = lambda b, i: (b * nb + i, 0)
    prev = lambda b, i: (jnp.maximum(b * nb + i - 1, 0), 0)
    return [
        pl.BlockSpec((BLOCK, D), cur),
        pl.BlockSpec((BLOCK, kvw2), cur),
        pl.BlockSpec((BLOCK, kvw2), prev),
        pl.BlockSpec((BLOCK, kvw2), cur),
        pl.BlockSpec((BLOCK, kvw2), prev),
    ]


def _attn_operands(kh, P, lo, q_ref, kc_ref, kp_ref, vc_ref, vp_ref):
    sl = slice(kh * LANES, (kh + 1) * LANES)

    def cat(prev_ref, cur_ref):
        d = jnp.concatenate([prev_ref[:, sl], cur_ref[:, sl]], axis=0)
        z = jnp.zeros_like(d)
        return jnp.concatenate([jnp.where(lo, d, z), jnp.where(lo, z, d)], axis=0)

    qt = jnp.concatenate([q_ref[:, (kh * P + pr) * LANES : (kh * P + pr + 1) * LANES] for pr in range(P)], axis=0)
    return qt, cat(kp_ref, kc_ref), cat(vp_ref, vc_ref)


def _attn_exp(s_all, bias, sink, tri, first):
    out = []
    for par in range(2):
        c0 = 2 * par * BLOCK
        s = jnp.where(tri, s_all[:, c0 + BLOCK : c0 + 2 * BLOCK], jnp.where(first, NEG, s_all[:, c0 : c0 + BLOCK]))
        s = s + bias[:, par * BLOCK : (par + 1) * BLOCK]
        snk = sink[:, par * BLOCK : (par + 1) * BLOCK]
        m = jnp.maximum(jnp.max(s, axis=-1, keepdims=True), snk)
        out.append((jnp.exp(s - m), jnp.exp(snk - m)))
    return out


def _unfold(x, tri):
    z = jnp.zeros_like(x)
    return jnp.concatenate([jnp.where(tri, z, x), jnp.where(tri, x, z)], axis=1)


def _attn_masks(R):
    lane = lax.broadcasted_iota(jnp.int32, (1, LANES), 1)
    row = lax.broadcasted_iota(jnp.int32, (R, BLOCK), 0) & (BLOCK - 1)
    col = lax.broadcasted_iota(jnp.int32, (R, BLOCK), 1)
    return lane, lane < HEAD_DIM, col <= row


def attn_fwd(q16, kd, vd, bias, sink, *, seq, n_seq):
    T, D = q16.shape
    nb = seq // BLOCK
    P = D // HEAD_DIM // N_KV_HEADS // 2
    R = P * BLOCK
    KV = range(N_KV_HEADS)

    def body(q_ref, kc_ref, kp_ref, vc_ref, vp_ref, bias_ref, sink_ref, o_ref):
        first = pl.program_id(1) == 0
        _, lo, tri = _attn_masks(R)
        r4 = lax.broadcasted_iota(jnp.int32, (4 * BLOCK, LANES), 0)
        l4 = lax.broadcasted_iota(jnp.int32, (4 * BLOCK, LANES), 1)
        ones = ((r4 < 2 * BLOCK) == (l4 < HEAD_DIM)).astype(BF16)
        ops = [_attn_operands(kh, P, lo, q_ref, kc_ref, kp_ref, vc_ref, vp_ref) for kh in KV]
        s_all = [_dot_nt(ops[kh][0], ops[kh][1]) for kh in KV]
        ex = [_attn_exp(s_all[kh], bias_ref[kh], sink_ref[kh], tri, first) for kh in KV]
        lhs = [jnp.concatenate([_unfold(e, tri) for e, _ in ex[kh]], axis=1).astype(BF16) for kh in KV]
        o = [_dot(lhs[kh], ops[kh][2]) for kh in KV]
        den = [_dot(lhs[kh], ones) for kh in KV]
        for kh in KV:
            out = o[kh] / (den[kh] + jnp.where(lo, ex[kh][0][1], ex[kh][1][1]))
            for pr in range(P):
                t = kh * P + pr
                o_ref[:, t * LANES : (t + 1) * LANES] = out[pr * BLOCK : (pr + 1) * BLOCK, :].astype(BF16)

    return pl.pallas_call(
        body,
        name="attn_fwd",
        grid=(n_seq, nb),
        in_specs=_attn_specs(D, nb) + [_resident((N_KV_HEADS, R, 2 * BLOCK)), _resident((N_KV_HEADS, R, 2 * BLOCK))],
        out_specs=pl.BlockSpec((BLOCK, D), lambda b, i: (b * nb + i, 0)),
        out_shape=jax.ShapeDtypeStruct((T, D), BF16),
        compiler_params=_params(2),
    )(q16, kd, kd, vd, vd, bias, sink)


def attn_bwd(q16, kd, vd, do, bias, sink, *, seq, n_seq):
    T, D = q16.shape
    kvw2 = 2 * N_KV_HEADS * HEAD_DIM
    nb = seq // BLOCK
    G = D // HEAD_DIM // N_KV_HEADS
    P = G // 2
    R = P * BLOCK
    KV = range(N_KV_HEADS)

    def body(q_ref, kc_ref, kp_ref, vc_ref, vp_ref, do_ref, bias_ref, sink_ref,
             dq_ref, dkc_ref, dkp_ref, dvc_ref, dvp_ref, dsink_ref):
        first = pl.program_id(1) == 0

        @pl.when(jnp.logical_and(pl.program_id(0) == 0, first))
        def _():
            dsink_ref[...] = jnp.zeros_like(dsink_ref)

        lane, lo, tri = _attn_masks(R)
        r4 = lax.broadcasted_iota(jnp.int32, (4 * BLOCK, 2 * BLOCK), 0)
        c4 = lax.broadcasted_iota(jnp.int32, (4 * BLOCK, 2 * BLOCK), 1)
        ones = ((r4 < 2 * BLOCK) == (c4 < BLOCK)).astype(BF16)
        ops = [_attn_operands(kh, P, lo, q_ref, kc_ref, kp_ref, vc_ref, vp_ref) for kh in KV]
        do16 = [jnp.concatenate([do_ref[:, (kh * P + pr) * LANES : (kh * P + pr + 1) * LANES] for pr in range(P)], axis=0)
                for kh in KV]
        s_all = [_dot_nt(ops[kh][0], ops[kh][1]) for kh in KV]
        dp_all = [_dot_nt(do16[kh], ops[kh][2]) for kh in KV]
        ex = [_attn_exp(s_all[kh], bias_ref[kh], sink_ref[kh], tri, first) for kh in KV]
        den = [_dot(jnp.concatenate([_unfold(e, tri) for e, _ in ex[kh]], axis=1).astype(BF16), ones) for kh in KV]
        dsink = jnp.zeros((1, LANES), F32)
        pf, dsf = [], []
        for kh in KV:
            ps_, ds_ = [], []
            for par in range(2):
                e, es = ex[kh][par]
                inv = 1.0 / (den[kh][:, par * BLOCK : (par + 1) * BLOCK] + es)
                p = e * inv
                c0 = 2 * par * BLOCK
                dp = jnp.where(tri, dp_all[kh][:, c0 + BLOCK : c0 + 2 * BLOCK], dp_all[kh][:, c0 : c0 + BLOCK])
                delta = jnp.sum(p * dp, axis=-1, keepdims=True)
                ds_.append(_unfold(p * (dp - delta), tri))
                ps_.append(_unfold(p, tri))
                dsr = -((es * inv) * delta)
                for pr in range(P):
                    hq = kh * G + 2 * pr + par
                    tot = jnp.sum(dsr[pr * BLOCK : (pr + 1) * BLOCK, :], axis=0, keepdims=True)
                    dsink = dsink + jnp.where(lane == hq, tot, 0.0)
            pf.append(jnp.concatenate(ps_, axis=1).astype(BF16))
            dsf.append(jnp.concatenate(ds_, axis=1).astype(BF16))
        dq = [_dot(dsf[kh], ops[kh][1]) for kh in KV]
        dk = [_dot_tn(dsf[kh], ops[kh][0]) for kh in KV]
        dv = [_dot_tn(pf[kh], do16[kh]) for kh in KV]
        dsink_ref[0:1, :] += dsink
        for kh in KV:
            sl = slice(kh * LANES, (kh + 1) * LANES)
            for pr in range(P):
                t = kh * P + pr
                dq_ref[:, t * LANES : (t + 1) * LANES] = dq[kh][pr * BLOCK : (pr + 1) * BLOCK, :]
            for full, prev_ref, cur_ref in ((dk[kh], dkp_ref, dkc_ref), (dv[kh], dvp_ref, dvc_ref)):
                dup = jnp.where(lo, full[: 2 * BLOCK, :], full[2 * BLOCK :, :])
                prev_ref[:, sl] = dup[:BLOCK, :]
                cur_ref[:, sl] = dup[BLOCK:, :]

    cur = lambda b, i: (b * nb + i, 0)
    kv_spec = pl.BlockSpec((BLOCK, kvw2), cur)
    kv_shape = jax.ShapeDtypeStruct((T, kvw2), F32)
    return pl.pallas_call(
        body,
        name="attn_bwd",
        grid=(n_seq, nb),
        in_specs=_attn_specs(D, nb)
        + [pl.BlockSpec((BLOCK, D), cur), _resident((N_KV_HEADS, R, 2 * BLOCK)), _resident((N_KV_HEADS, R, 2 * BLOCK))],
        out_specs=[pl.BlockSpec((BLOCK, D), cur), kv_spec, kv_spec, kv_spec, kv_spec, pl.BlockSpec((8, LANES), lambda b, i: (0, 0))],
        out_shape=[jax.ShapeDtypeStruct((T, D), F32), kv_shape, kv_shape, kv_shape, kv_shape, jax.ShapeDtypeStruct((8, LANES), F32)],
        compiler_params=_params(2),
    )(q16, kd, kd, vd, vd, do, bias, sink)


def qkv_bwd(dq, dkc, dkp, dvc, dvp, qkv, dres, x, gain, w_qkv, qg, kg, *, seq):
    T, D = x.shape
    kvw2 = dkc.shape[1]
    kvw = kvw2 // 2
    nqt, nkt = D // LANES, kvw // LANES
    nb = seq // BLOCK
    n = T // BLOCK

    def body(dq_ref, dkc_ref, dkp_ref, dvc_ref, dvp_ref, qkv_ref, dres_ref, x_ref, g_ref, w_ref, qg_ref, kg_ref,
             dx_ref, dqkv_ref, h_ref, dg_ref, hg_ref):
        i = pl.program_id(0)

        @pl.when(i == 0)
        def _():
            dg_ref[...] = jnp.zeros_like(dg_ref)
            hg_ref[...] = jnp.zeros_like(hg_ref)

        lo = lax.broadcasted_iota(jnp.int32, (1, LANES), 1) < HEAD_DIM
        last = i % nb == nb - 1
        dkd = dkc_ref[...] + jnp.where(last, 0.0, dkp_ref[...])
        dvd = dvc_ref[...] + jnp.where(last, 0.0, dvp_ref[...])

        def undup(d, t):
            a, b = d[:, 2 * t * LANES : (2 * t + 1) * LANES], d[:, (2 * t + 1) * LANES : (2 * t + 2) * LANES]
            return jnp.where(lo, a + _swap_halves(a), b + _swap_halves(b))

        gq = jnp.zeros((1, LANES), F32)
        for t in range(nqt):
            sl = slice(t * LANES, (t + 1) * LANES)
            q = qkv_ref[:, sl]
            _, r = _head_norm(q, qg_ref[...], lo)
            dxq, dgn = _head_norm_bwd(q, r, qg_ref[...], dq_ref[:, sl] * SCALE, lo)
            dqkv_ref[:, sl] = dxq.astype(BF16)
            gq = gq + dgn
        gk = jnp.zeros((1, LANES), F32)
        for t in range(nkt):
            sl = slice(D + t * LANES, D + (t + 1) * LANES)
            k = qkv_ref[:, sl]
            _, r = _head_norm(k, kg_ref[...], lo)
            dxk, dgn = _head_norm_bwd(k, r, kg_ref[...], undup(dkd, t), lo)
            dqkv_ref[:, sl] = dxk.astype(BF16)
            gk = gk + dgn
            dqkv_ref[:, D + kvw + t * LANES : D + kvw + (t + 1) * LANES] = undup(dvd, t).astype(BF16)
        hg_ref[0:1, :] += gq
        hg_ref[1:2, :] += gk
        dh = _dot_nt(dqkv_ref[...], w_ref[...])
        xt = x_ref[...]
        r = _rms(xt)
        gn = g_ref[...]
        h_ref[...] = ((xt * r) * gn).astype(BF16)
        dx, dgn = _rms_bwd(xt, r, gn, dh)
        dg_ref[0:1, :] += dgn
        dx_ref[...] = dres_ref[...] + dx

    row = lambda i: (i, 0)
    nxt = lambda i: (jnp.minimum(i + 1, n - 1), 0)
    return pl.pallas_call(
        body,
        name="qkv_bwd",
        grid=(n,),
        in_specs=[
            pl.BlockSpec((BLOCK, D), row),
            pl.BlockSpec((BLOCK, kvw2), row),
            pl.BlockSpec((BLOCK, kvw2), nxt),
            pl.BlockSpec((BLOCK, kvw2), row),
            pl.BlockSpec((BLOCK, kvw2), nxt),
            pl.BlockSpec((BLOCK, D + kvw2), row),
            pl.BlockSpec((BLOCK, D), row),
            pl.BlockSpec((BLOCK, D), row),
            _resident((1, D)),
            _resident((D, D + kvw2)),
            _resident((1, LANES)),
            _resident((1, LANES)),
        ],
        out_specs=[
            pl.BlockSpec((BLOCK, D), row),
            pl.BlockSpec((BLOCK, D + kvw2), row),
            pl.BlockSpec((BLOCK, D), row),
            pl.BlockSpec((8, D), lambda i: (0, 0)),
            pl.BlockSpec((8, LANES), lambda i: (0, 0)),
        ],
        out_shape=[
            jax.ShapeDtypeStruct((T, D), F32),
            jax.ShapeDtypeStruct((T, D + kvw2), BF16),
            jax.ShapeDtypeStruct((T, D), BF16),
            jax.ShapeDtypeStruct((8, D), F32),
            jax.ShapeDtypeStruct((8, LANES), F32),
        ],
        compiler_params=_params(1),
    )(dq, dkc, dkp, dvc, dvp, qkv, dres, x, gain, w_qkv, qg, kg)


def loss_head(y, target, *, tm):
    T, D = y.shape

    def body(y_ref, t_ref, dy_ref, s_ref):
        @pl.when(pl.program_id(0) == 0)
        def _():
            s_ref[...] = jnp.zeros_like(s_ref)

        e = y_ref[...] - t_ref[...]
        dy_ref[...] = e * (1.0 / D)
        s_ref[...] += jnp.sum(jnp.sum(e * e, axis=-1, keepdims=True), axis=0, keepdims=True)

    return pl.pallas_call(
        body,
        name="loss_head",
        grid=(T // tm,),
        in_specs=[pl.BlockSpec((tm, D), lambda i: (i, 0)), pl.BlockSpec((tm, D), lambda i: (i, 0))],
        out_specs=[pl.BlockSpec((tm, D), lambda i: (i, 0)), pl.BlockSpec((8, LANES), lambda i: (0, 0))],
        out_shape=[jax.ShapeDtypeStruct((T, D), F32), jax.ShapeDtypeStruct((8, LANES), F32)],
        compiler_params=_params(1),
    )(y, target)


def local_step(x, target, gains, w, *, seq, tm=256, tk=2048, shards=None):
    T, D = x.shape
    n_seq = T // seq
    nm, nf, qgain, kgain, sinks = gains
    H = D // HEAD_DIM
    tk = min(tk, T)
    qg2, kg2 = jnp.tile(qgain, (1, 2)), jnp.tile(kgain, (1, 2))
    bias, sinkcol = _attn_tables(sinks, H)

    dist = shards is not None
    w = dict(w)

    plan = _Gather([shards["w_gu"][0], shards["w_d"][0]]) if dist else None
    (x1, bcx), got = conv_fwd(x, nm[0:1], w["w_in"], w["cw"], w["w_out"], seq=seq, tm=tm, plan=plan)
    if dist:
        w["w_gu"], w["w_d"] = [got[0], None], [got[1].reshape(-1, D), None]
    plan = _Gather([shards["w_qkv"], shards["w_o"], shards["w_gu"][1], shards["w_d"][1]]) if dist else None
    (x2, gu0), got = ffn_fwd(x1, nf[0:1], w["w_gu"][0], w["w_d"][0], tm=tm, plan=plan)
    if dist:
        w["w_qkv"], w["w_o"] = got[0].transpose(1, 0, 2).reshape(D, -1), got[1].reshape(D, D)
        w["w_gu"][1], w["w_d"][1] = got[2], got[3].reshape(-1, D)
    qkv, q16, kd, vd = qkv_proj(x2, nm[1:2], w["w_qkv"], qg2, kg2, tm=tm)
    ao = attn_fwd(q16, kd, vd, bias, sinkcol, seq=seq, n_seq=n_seq)
    x3 = matmul_residual(ao, w["w_o"], x2, tm=tm)
    (x4, gu1), _ = ffn_fwd(x3, nf[1:2], w["w_gu"][1], w["w_d"][1], tm=tm)
    dx4, sse = loss_head(x4, target, tm=tm)

    C = w["w_gu"][0].shape[2]
    by_dest = lambda a: a.reshape(N_DEV, -1, a.shape[-1])
    fresh = lambda a, *lead: jax.ShapeDtypeStruct((N_DEV,) + lead + a.shape[1:], a.dtype)

    (dx3, a16, dgu, h16, d16, dnf1), _ = ffn_bwd(dx4, x3, nf[1:2], gu1, w["w_gu"][1], w["w_d"][1], tm=tm)
    g_gu1, _ = wgrad(h16, dgu, name="wgrad_gu1", b_chunks=True, tk=tk)
    g_d1 = by_dest(wgrad(a16, d16, name="wgrad_d1", a_chunks=True, tk=tk)[0])
    dao, dx3_16 = matmul_nt_cast(dx3, w["w_o"], tm=tm)
    g_o = by_dest(wgrad(ao, dx3_16, name="wgrad_o", tk=tk)[0])
    dq, dkc, dkp, dvc, dvp, dsinks = attn_bwd(q16, kd, vd, dao, bias, sinkcol, seq=seq, n_seq=n_seq)
    dx2, dqkv16, h16, dnm1, dgains = qkv_bwd(dq, dkc, dkp, dvc, dvp, qkv, dx3, x2, nm[1:2], w["w_qkv"], qg2, kg2, seq=seq)
    nq = dqkv16.shape[1]
    g_qkv = wgrad(h16, dqkv16, name="wgrad_qkv", tk=tk)[0].reshape(D, N_DEV, nq // N_DEV).transpose(1, 0, 2)

    plan = _Scatter([(g_gu1, fresh(g_gu1, 2), 1, False), (g_d1, fresh(g_d1, 2), 1, False)]) if dist else None
    (dx1, a16, dgu, h16, d16, dnf0), land_ffn = ffn_bwd(dx2, x1, nf[0:1], gu0, w["w_gu"][0], w["w_d"][0], tm=tm, plan=plan)
    plan = _Scatter([(g_o, fresh(g_o), None, False), (g_qkv, fresh(g_qkv), None, False)]) if dist else None
    g_gu0, land_attn = wgrad(h16, dgu, name="wgrad_gu0", b_chunks=True, tk=tk, plan=plan)
    g_d0 = by_dest(wgrad(a16, d16, name="wgrad_d0", a_chunks=True, tk=tk)[0])
    plan = _Scatter([(g_gu0, land_ffn[0], 0, False)]) if dist else None
    (gx, dbcx, h16, z16, d16, dcw, dnm0), land_gu = conv_bwd(
        dx1, x, nm[0:1], bcx, w["cw"], w["w_in"], w["w_out"], seq=seq, tm=tm, plan=plan)
    plan = _Scatter([(g_d0, land_ffn[1], 0, False)]) if dist else None
    g_in, land_d = wgrad(h16, dbcx, name="wgrad_in", b_cols=3 * D // N_DEV, tk=tk, plan=plan)
    g = dict(w_in=g_in, cw=dcw[0:3].reshape(3, N_DEV, D // N_DEV).transpose(1, 0, 2),
             w_out=by_dest(wgrad(z16, d16, name="wgrad_out", tk=tk)[0]))
    if dist:
        g.update(w_gu=land_gu[0], w_d=land_d[0], w_o=land_attn[0], w_qkv=land_attn[1])
    else:
        g.update(w_gu0=g_gu0, w_gu1=g_gu1, w_d0=g_d0, w_d1=g_d1, w_o=g_o, w_qkv=g_qkv)
    small = dict(nm0=dnm0, nm1=dnm1, nf0=dnf0, nf1=dnf1, gains=dgains, sinks=dsinks)
    return sse, gx, g, small


def _adamw_math(g, w, m, v):
    m = ADAM_B1 * m + (1.0 - ADAM_B1) * g
    v = ADAM_B2 * v + (1.0 - ADAM_B2) * (g * g)
    m_hat = m / (1.0 - ADAM_B1 ** ADAM_STEP)
    v_hat = v / (1.0 - ADAM_B2 ** ADAM_STEP)
    delta = -ADAM_LR * (m_hat / (jnp.sqrt(v_hat) + ADAM_EPS) + ADAM_WD * w)
    return delta, m, v


def adamw(parts, w, m, v, *, name):
    n, R, C = parts.shape
    tr = R
    for cand in (256, 128, 88, 64, 32, 16, 8):
        if R > cand and R % cand == 0:
            tr = cand
            break

    def body(p_ref, w_ref, m_ref, v_ref, g_ref, d_ref, mo_ref, vo_ref):
        g = p_ref[0].astype(F32)
        for s in range(1, n):
            g = g + p_ref[s].astype(F32)
        g_ref[...] = g
        d_ref[...], mo_ref[...], vo_ref[...] = _adamw_math(g, w_ref[...], m_ref[...], v_ref[...])

    blk = pl.BlockSpec((tr, C), lambda i: (i, 0))
    return pl.pallas_call(
        body,
        name=name,
        grid=(R // tr,),
        in_specs=[pl.BlockSpec((n, tr, C), lambda i: (0, i, 0)), blk, blk, blk],
        out_specs=[blk] * 4,
        out_shape=[jax.ShapeDtypeStruct((R, C), F32)] * 4,
        compiler_params=_params(1),
    )(parts, w, m, v)


def pack_small(small, D):
    W = max(D, 2 * LANES)

    def body(nm0, nm1, nf0, nf1, gains, sinks, o_ref):
        o_ref[...] = jnp.zeros_like(o_ref)
        o_ref[0:1, :D] = nm0[0:1, :]
        o_ref[1:2, :D] = nm1[0:1, :]
        o_ref[2:3, :D] = nf0[0:1, :]
        o_ref[3:4, :D] = nf1[0:1, :]
        gq = gains[0:1, :] + pltpu.roll(gains[0:1, :], HEAD_DIM, 1)
        gk = gains[1:2, :] + pltpu.roll(gains[1:2, :], HEAD_DIM, 1)
        lane = lax.broadcasted_iota(jnp.int32, (1, LANES), 1)
        o_ref[4:5, :LANES] = jnp.where(lane < HEAD_DIM, gq, gk)
        o_ref[4:5, LANES : 2 * LANES] = sinks[0:1, :]

    return pl.pallas_call(
        body,
        name="pack_small",
        out_shape=jax.ShapeDtypeStruct((8, W), F32),
    )(small["nm0"], small["nm1"], small["nf0"], small["nf1"], small["gains"], small["sinks"])


def _pack_small_params(nm, nf, qg, kg, sk, D):
    W = max(D, 2 * LANES)
    row4 = jnp.concatenate([qg.reshape(-1), kg.reshape(-1), jnp.zeros((LANES - 2 * HEAD_DIM,), F32), sk.reshape(-1)])
    row4 = jnp.pad(row4, (0, W - row4.shape[0]))
    rows = [jnp.pad(r, (0, W - D)) for r in (nm[0], nm[1], nf[0], nf[1])] + [row4]
    return jnp.concatenate([jnp.stack(rows), jnp.zeros((3, W), F32)], axis=0)


def _unpack_small(a, D, H):
    nm = a[0:2, :D]
    nf = a[2:4, :D]
    qg = a[4:5, 0:HEAD_DIM]
    kg = a[4:5, HEAD_DIM : 2 * HEAD_DIM]
    sk = a[4:5, LANES : LANES + H]
    return qg, kg, sk, nm, nf


def kernel(x, conv_w_in, conv_w, conv_w_out, attn_w_qkv, attn_q_gain, attn_k_gain, attn_sinks, attn_w_o, norm_mixer, norm_ffn, ffn_w_gate_up, ffn_w_down, loss_target, m_conv_w_in, m_conv_w, m_conv_w_out, m_attn_w_qkv, m_attn_q_gain, m_attn_k_gain, m_attn_sinks, m_attn_w_o, m_norm_mixer, m_norm_ffn, m_ffn_w_gate_up, m_ffn_w_down, v_conv_w_in, v_conv_w, v_conv_w_out, v_attn_w_qkv, v_attn_q_gain, v_attn_k_gain, v_attn_sinks, v_attn_w_o, v_norm_mixer, v_norm_ffn, v_ffn_w_gate_up, v_ffn_w_down):
    n_seq, seq, D = x.shape
    T = n_seq * seq
    H = D // HEAD_DIM
    L = ffn_w_gate_up.shape[0]

    full = run_plan(_Gather([conv_w_in[0].astype(BF16), conv_w[0], conv_w_out[0].astype(BF16)]), name="gather_conv_weights")
    w = dict(w_in=full[0].transpose(1, 0, 2).reshape(D, 3 * D), cw=full[1].transpose(1, 0, 2).reshape(3, D),
             w_out=full[2].reshape(D, D))
    shards = dict(w_gu=[ffn_w_gate_up[l].astype(BF16) for l in range(L)], w_d=[ffn_w_down[l].astype(BF16) for l in range(L)],
                  w_qkv=attn_w_qkv[0].astype(BF16), w_o=attn_w_o[0].astype(BF16))
    gains = (norm_mixer, norm_ffn, attn_q_gain, attn_k_gain, attn_sinks)
    sse, gx, g, small = local_step(x.reshape(T, D), loss_target.reshape(T, D), gains, w, seq=seq, shards=shards)
    loss = lax.psum(sse[0, 0] * (0.5 / D), ("x", "y", "c"))

    packed = pack_small(small, D)
    last = [g["w_in"], g["cw"], g["w_out"]]
    items = [(a, jax.ShapeDtypeStruct(a.shape, a.dtype), None, False) for a in last]
    items.append((packed, jax.ShapeDtypeStruct((N_DEV,) + packed.shape, packed.dtype), None, True))
    land_in, land_cw, land_out, small_all = run_plan(_Scatter(items), name="exchange_conv_grads")
    bufs = [land_in, land_cw, land_out, g["w_qkv"], g["w_o"], g["w_gu"], g["w_d"]]

    def flat(a):
        return a.reshape(-1, a.shape[-1])

    big = [conv_w_in, conv_w, conv_w_out, attn_w_qkv, attn_w_o, ffn_w_gate_up, ffn_w_down]
    big_m = [m_conv_w_in, m_conv_w, m_conv_w_out, m_attn_w_qkv, m_attn_w_o, m_ffn_w_gate_up, m_ffn_w_down]
    big_v = [v_conv_w_in, v_conv_w, v_conv_w_out, v_attn_w_qkv, v_attn_w_o, v_ffn_w_gate_up, v_ffn_w_down]
    tags = ["in", "cw", "out", "qkv", "o", "gu", "d"]
    res = []
    for b in range(7):
        parts = bufs[b].reshape(N_DEV, -1, bufs[b].shape[-1])
        outs = adamw(parts, flat(big[b]), flat(big_m[b]), flat(big_v[b]), name="adamw_" + tags[b])
        res.append([o.reshape(big[b].shape) for o in outs])
    sw = _pack_small_params(norm_mixer, norm_ffn, attn_q_gain, attn_k_gain, attn_sinks, D)
    sm = _pack_small_params(m_norm_mixer, m_norm_ffn, m_attn_q_gain, m_attn_k_gain, m_attn_sinks, D)
    sv = _pack_small_params(v_norm_mixer, v_norm_ffn, v_attn_q_gain, v_attn_k_gain, v_attn_sinks, D)
    souts = adamw(small_all, sw, sm, sv, name="adamw_small")
    sres = [_unpack_small(o, D, H) for o in souts]

    def ordered(i):
        r, s = [r[i] for r in res], sres[i]
        return [r[0], r[1], r[2], r[3], s[0], s[1], s[2], r[4], s[3], s[4], r[5], r[6]]

    return (loss, gx.reshape(n_seq, seq, D), *ordered(0), *ordered(1), *ordered(2), *ordered(3))
```

```python
import functools
import math

import jax
import jax.numpy as jnp
from jax import lax
from jax.experimental import pallas as pl
from jax.experimental.pallas import tpu as pltpu

F32 = jnp.float32
BF16 = jnp.bfloat16

EPS = 1e-6
HEAD_DIM = 64
N_KV_HEADS = 4
BLOCK = 128
LANES = 128
N_DEV = 8
NEG = -1e30
SCALE = 1.0 / math.sqrt(HEAD_DIM)

ADAM_LR = 0.001
ADAM_B1 = 0.9
ADAM_B2 = 0.999
ADAM_EPS = 1e-08
ADAM_WD = 0.01
ADAM_STEP = 10

V7X_VMEM_BYTES = 64 * 1024 * 1024
VMEM_LIMIT = V7X_VMEM_BYTES - 8 * 1024 * 1024
MESH = pl.DeviceIdType.MESH

_NT = (((1,), (1,)), ((), ()))
_TN = (((0,), (0,)), ((), ()))


def _params(n_grid):
    return pltpu.CompilerParams(dimension_semantics=("arbitrary",) * n_grid, vmem_limit_bytes=VMEM_LIMIT)


def _resident(shape):
    nd = len(shape)
    return pl.BlockSpec(shape, lambda *_: (0,) * nd, pipeline_mode=pl.Buffered(1))


def _rms(x):
    return lax.rsqrt(jnp.mean(x * x, axis=-1, keepdims=True) + EPS)


def _rms_bwd(x, r, gain, dh):
    xn = x * r
    dxn = dh * gain
    dx = r * (dxn - xn * jnp.mean(dxn * xn, axis=-1, keepdims=True))
    return dx, jnp.sum(dh * xn, axis=0, keepdims=True)


def _dot(a, b):
    return jnp.dot(a, b, preferred_element_type=F32)


def _dot_nt(a, b):
    return lax.dot_general(a, b, _NT, preferred_element_type=F32)


def _dot_tn(a, b):
    return lax.dot_general(a, b, _TN, preferred_element_type=F32)


def _place():
    return lax.axis_index("x"), lax.axis_index("y"), lax.axis_index("c")


def _flip(v, bit):
    return 1 - v if bit else v


def _slot(px, py, pc):
    return 4 * px + 2 * py + pc


class _Gather:
    def __init__(self, shards):
        nt = len(shards)
        self.nt = nt
        self.inputs = list(shards)
        self.out_shapes = [jax.ShapeDtypeStruct((N_DEV,) + s.shape, s.dtype) for s in shards]
        self.scratch = [pltpu.SemaphoreType.DMA((nt, 7)), pltpu.SemaphoreType.DMA((nt, 7)), pltpu.SemaphoreType.DMA((nt,))]
        self.aliases = {}

    def phases(self, total):
        assert total >= 3
        return [(0, self.start), (total - 2, self.forward), (total - 1, self.finish)]

    def _copies(self, ins, outs, sems):
        send_sems, recv_sems, loc_sems = sems
        x, y, c = _place()
        me = _slot(x, y, c)
        sib = (x, y, 1 - c)
        chips = [(_flip(x, k >> 1), _flip(y, k & 1)) for k in (1, 2, 3)]

        def copy(t, k, src, dst_slot, to):
            return pltpu.make_async_remote_copy(
                src_ref=src, dst_ref=outs[t].at[dst_slot], send_sem=send_sems.at[t, k], recv_sem=recv_sems.at[t, k],
                device_id=to, device_id_type=MESH)

        local = [pltpu.make_async_copy(ins[t], outs[t].at[me], loc_sems.at[t]) for t in range(self.nt)]
        first, passed, arrive_ici, arrive_sib = [], [], [], []
        for t in range(self.nt):
            first.append(copy(t, 0, ins[t], me, sib))
            s = _slot(x, y, 1 - c)
            arrive_sib.append(copy(t, 0, outs[t].at[s], s, sib))
            for j, (px, py) in enumerate(chips):
                first.append(copy(t, 1 + j, ins[t], me, (px, py, c)))
                s = _slot(px, py, c)
                arrive_ici.append(copy(t, 1 + j, outs[t].at[s], s, sib))
                passed.append(copy(t, 4 + j, outs[t].at[s], s, sib))
                s = _slot(px, py, 1 - c)
                arrive_sib.append(copy(t, 4 + j, outs[t].at[s], s, sib))
        return local, first, passed, arrive_ici, arrive_sib

    def start(self, ins, outs, sems):
        local, first, _, _, _ = self._copies(ins, outs, sems)
        for cp in local + first:
            cp.start()

    def forward(self, ins, outs, sems):
        _, _, passed, arrive_ici, _ = self._copies(ins, outs, sems)
        for arrival, fwd in zip(arrive_ici, passed):
            arrival.wait_recv()
            fwd.start()

    def finish(self, ins, outs, sems):
        local, first, passed, _, arrive_sib = self._copies(ins, outs, sems)
        for cp in arrive_sib:
            cp.wait_recv()
        for cp in first + passed:
            cp.wait_send()
        for cp in local:
            cp.wait()


class _Scatter:
    def __init__(self, items):
        self.items = items
        nt = len(items)
        self.nt = nt
        reused = [(t, it[1]) for t, it in enumerate(items) if not isinstance(it[1], jax.ShapeDtypeStruct)]
        self.inputs = [it[0] for it in items] + [land for _, land in reused]
        self.out_shapes = [jax.ShapeDtypeStruct(it[1].shape, it[1].dtype) for it in items]
        self.aliases = {nt + i: t for i, (t, _) in enumerate(reused)}
        self.scratch = [pltpu.SemaphoreType.DMA((nt, 7)), pltpu.SemaphoreType.DMA((nt, 7)), pltpu.SemaphoreType.DMA((nt,))]

    def phases(self, total):
        assert total >= 2
        return [(0, self.start), (total - 1, self.finish)]

    def _copies(self, ins, outs, sems):
        send_sems, recv_sems, loc_sems = sems
        x, y, c = _place()
        me = _slot(x, y, c)

        def land(t, s):
            layer = self.items[t][2]
            return outs[t].at[s] if layer is None else outs[t].at[s, layer]

        def src(t, s):
            return ins[t] if self.items[t][3] else ins[t].at[s]

        local = [pltpu.make_async_copy(src(t, me), land(t, me), loc_sems.at[t]) for t in range(self.nt)]
        sends, arrivals = [], []
        for k in range(1, N_DEV):
            px, py, pc = _flip(x, (k >> 2) & 1), _flip(y, (k >> 1) & 1), _flip(c, k & 1)
            peer = _slot(px, py, pc)
            for t in range(self.nt):
                for dst, into in ((land(t, me), sends), (land(t, peer), arrivals)):
                    into.append(pltpu.make_async_remote_copy(
                        src_ref=src(t, peer), dst_ref=dst, send_sem=send_sems.at[t, k - 1], recv_sem=recv_sems.at[t, k - 1],
                        device_id=(px, py, pc), device_id_type=MESH))
        return local, sends, arrivals

    def start(self, ins, outs, sems):
        local, sends, _ = self._copies(ins, outs, sems)
        for cp in local + sends:
            cp.start()

    def finish(self, ins, outs, sems):
        local, sends, arrivals = self._copies(ins, outs, sems)
        for cp in arrivals:
            cp.wait_recv()
        for cp in sends:
            cp.wait_send()
        for cp in local:
            cp.wait()


def _any_specs(n):
    return [pl.BlockSpec(memory_space=pl.ANY)] * n


def run_plan(plan, *, name):
    def body(*refs):
        n_in, n_out = len(plan.inputs), len(plan.out_shapes)
        ins, outs, sems = refs[:n_in], refs[n_in : n_in + n_out], refs[n_in + n_out :]
        for _, phase in plan.phases(3):
            phase(ins, outs, sems)

    return pl.pallas_call(
        body,
        name=name,
        in_specs=_any_specs(len(plan.inputs)),
        out_specs=_any_specs(len(plan.out_shapes)),
        out_shape=plan.out_shapes,
        scratch_shapes=plan.scratch,
        input_output_aliases=plan.aliases,
    )(*plan.inputs)


def _call(body, *, name, grid, in_specs, out_specs, out_shape, args, scratch=(), plan=None):
    n_in, n_out, n_scr = len(in_specs), len(out_specs), len(scratch)
    if plan is None:
        outs = pl.pallas_call(
            body, name=name, grid=grid, in_specs=in_specs, out_specs=out_specs, out_shape=out_shape,
            scratch_shapes=list(scratch), compiler_params=_params(len(grid)))(*args)
        return outs, None
    c_in, c_out = len(plan.inputs), len(plan.out_shapes)
    phases = plan.phases(math.prod(grid))

    def full(*refs):
        a, refs = refs[:n_in], refs[n_in:]
        ci, refs = refs[:c_in], refs[c_in:]
        o, refs = refs[:n_out], refs[n_out:]
        co, refs = refs[:c_out], refs[c_out:]
        s, cs = refs[:n_scr], refs[n_scr:]
        step = pl.program_id(0)
        for d in range(1, len(grid)):
            step = step * grid[d] + pl.program_id(d)
        for at, phase in phases:
            if at == 0:
                pl.when(step == 0)(functools.partial(phase, ci, co, cs))
        body(*a, *o, *s)
        for at, phase in phases:
            if at > 0:
                pl.when(step == at)(functools.partial(phase, ci, co, cs))

    outs = pl.pallas_call(
        full,
        name=name,
        grid=grid,
        in_specs=list(in_specs) + _any_specs(c_in),
        out_specs=list(out_specs) + _any_specs(c_out),
        out_shape=list(out_shape) + plan.out_shapes,
        scratch_shapes=list(scratch) + plan.scratch,
        input_output_aliases={n_in + i: n_out + t for i, t in plan.aliases.items()},
        compiler_params=_params(len(grid)),
    )(*args, *plan.inputs)
    return outs[:n_out], outs[n_out:]


def _shift_down(u, prev8, row, n):
    out = pltpu.roll(u, n, 0)
    for k in range(n):
        out = jnp.where(row == k, prev8[8 - n + k : 8 - n + k + 1, :], out)
    return out


def _shift_up(u, next8, row, n, tm):
    out = pltpu.roll(u, tm - n, 0)
    for k in range(n):
        out = jnp.where(row == tm - n + k, next8[k : k + 1, :], out)
    return out


def conv_fwd(x, gain, w_in, cw, w_out, *, seq, tm, plan=None):
    T, D = x.shape
    tps = seq // tm

    def body(x_ref, g_ref, win_ref, cw_ref, wout_ref, x1_ref, bcx_ref, carry_ref):
        i = pl.program_id(0)

        @pl.when(i % tps == 0)
        def _():
            carry_ref[...] = jnp.zeros_like(carry_ref)

        xt = x_ref[...]
        h = ((xt * _rms(xt)) * g_ref[...]).astype(BF16)
        bcx = _dot(h, win_ref[...])
        bcx_ref[...] = bcx
        b, c, xv = bcx[:, :D], bcx[:, D : 2 * D], bcx[:, 2 * D :]
        u = b * xv
        row = lax.broadcasted_iota(jnp.int32, u.shape, 0)
        prev = carry_ref[...]
        u1 = _shift_down(u, prev, row, 1)
        u2 = _shift_down(u, prev, row, 2)
        carry_ref[...] = u[tm - 8 :, :]
        cwv = cw_ref[...]
        y = cwv[0:1, :] * u2 + cwv[1:2, :] * u1 + cwv[2:3, :] * u
        z = (c * y).astype(BF16)
        x1_ref[...] = xt + _dot(z, wout_ref[...])

    return _call(
        body,
        plan=plan,
        args=(x, gain, w_in, cw, w_out),
        name="conv_fwd",
        grid=(T // tm,),
        in_specs=[
            pl.BlockSpec((tm, D), lambda i: (i, 0)),
            _resident((1, D)),
            _resident((D, 3 * D)),
            _resident((3, D)),
            _resident((D, D)),
        ],
        out_specs=[pl.BlockSpec((tm, D), lambda i: (i, 0)), pl.BlockSpec((tm, 3 * D), lambda i: (i, 0))],
        out_shape=[jax.ShapeDtypeStruct((T, D), F32), jax.ShapeDtypeStruct((T, 3 * D), F32)],
        scratch=[pltpu.VMEM((8, D), F32)],
    )


def conv_bwd(dx1, x, gain, bcx, cw, w_in, w_out, *, seq, tm, plan=None):
    T, D = x.shape
    n = T // tm
    tps = seq // tm
    r8 = tm // 8

    def body(d_ref, x_ref, g_ref, bcx_ref, halo_ref, cw_ref, win_ref, wout_ref,
             gx_ref, dbcx_ref, h_ref, z_ref, d16_ref, dcw_ref, dg_ref, carry_ref):
        i = pl.program_id(0)
        t = n - 1 - i

        @pl.when(i == 0)
        def _():
            dcw_ref[...] = jnp.zeros_like(dcw_ref)
            dg_ref[...] = jnp.zeros_like(dg_ref)

        @pl.when(t % tps == tps - 1)
        def _():
            carry_ref[...] = jnp.zeros_like(carry_ref)

        d = d_ref[...]
        d16 = d.astype(BF16)
        d16_ref[...] = d16
        dz = _dot_nt(d16, wout_ref[...])
        bcx = bcx_ref[...]
        b, c, xv = bcx[:, :D], bcx[:, D : 2 * D], bcx[:, 2 * D :]
        u = b * xv
        halo = halo_ref[...]
        hu = jnp.where(t % tps == 0, 0.0, halo[:, :D] * halo[:, 2 * D :])
        row = lax.broadcasted_iota(jnp.int32, u.shape, 0)
        u1 = _shift_down(u, hu, row, 1)
        u2 = _shift_down(u, hu, row, 2)
        cwv = cw_ref[...]
        y = cwv[0:1, :] * u2 + cwv[1:2, :] * u1 + cwv[2:3, :] * u
        z_ref[...] = (c * y).astype(BF16)
        dc = dz * y
        dy = dz * c
        dcw_ref[0:1, :] += jnp.sum(dy * u2, axis=0, keepdims=True)
        dcw_ref[1:2, :] += jnp.sum(dy * u1, axis=0, keepdims=True)
        dcw_ref[2:3, :] += jnp.sum(dy * u, axis=0, keepdims=True)
        nxt = carry_ref[...]
        dy1 = _shift_up(dy, nxt, row, 1, tm)
        dy2 = _shift_up(dy, nxt, row, 2, tm)
        carry_ref[...] = dy[0:8, :]
        du = cwv[2:3, :] * dy + cwv[1:2, :] * dy1 + cwv[0:1, :] * dy2
        dbcx_ref[:, :D] = (du * xv).astype(BF16)
        dbcx_ref[:, D : 2 * D] = dc.astype(BF16)
        dbcx_ref[:, 2 * D :] = (du * b).astype(BF16)
        dh = _dot_nt(dbcx_ref[...], win_ref[...])
        xt = x_ref[...]
        r = _rms(xt)
        gn = g_ref[...]
        h_ref[...] = ((xt * r) * gn).astype(BF16)
        dx, dgn = _rms_bwd(xt, r, gn, dh)
        dg_ref[0:1, :] += dgn
        gx_ref[...] = d + dx

    rev = lambda i: (n - 1 - i, 0)
    return _call(
        body,
        plan=plan,
        args=(dx1, x, gain, bcx, bcx, cw, w_in, w_out),
        name="conv_bwd",
        grid=(n,),
        in_specs=[
            pl.BlockSpec((tm, D), rev),
            pl.BlockSpec((tm, D), rev),
            _resident((1, D)),
            pl.BlockSpec((tm, 3 * D), rev),
            pl.BlockSpec((8, 3 * D), lambda i: (jnp.maximum((n - 1 - i) * r8 - 1, 0), 0)),
            _resident((3, D)),
            _resident((D, 3 * D)),
            _resident((D, D)),
        ],
        out_specs=[
            pl.BlockSpec((tm, D), rev),
            pl.BlockSpec((tm, 3 * D), rev),
            pl.BlockSpec((tm, D), rev),
            pl.BlockSpec((tm, D), rev),
            pl.BlockSpec((tm, D), rev),
            pl.BlockSpec((8, D), lambda i: (0, 0)),
            pl.BlockSpec((8, D), lambda i: (0, 0)),
        ],
        out_shape=[
            jax.ShapeDtypeStruct((T, D), F32),
            jax.ShapeDtypeStruct((T, 3 * D), BF16),
            jax.ShapeDtypeStruct((T, D), BF16),
            jax.ShapeDtypeStruct((T, D), BF16),
            jax.ShapeDtypeStruct((T, D), BF16),
            jax.ShapeDtypeStruct((8, D), F32),
            jax.ShapeDtypeStruct((8, D), F32),
        ],
        scratch=[pltpu.VMEM((8, D), F32)],
    )


def _sigmoid(g):
    return 1.0 / (1.0 + jnp.exp(-g))


def ffn_fwd(x, gain, w_gu, w_d, *, tm, plan=None, attn=None, target=None):
    T, D = x.shape
    _, _, C = w_gu.shape
    half = N_DEV // 2
    row = lambda i: (i, 0)
    tile = pl.BlockSpec((tm, D), row)

    def body(*refs):
        refs = list(refs)
        x_ref, g_ref, wgu_ref, wd_ref = refs[:4]
        del refs[:4]
        if attn is not None:
            ao_ref, wo_ref = refs[:2]
            del refs[:2]
        if target is not None:
            t_ref = refs.pop(0)
        if attn is not None:
            xin_ref = refs.pop(0)
        xo_ref, gu_ref = refs[:2]
        xt = x_ref[...]
        if attn is not None:
            xt = xt + _dot(ao_ref[...], wo_ref[...])
            xin_ref[...] = xt
        h = ((xt * _rms(xt)) * g_ref[...]).astype(BF16)
        acc = xt
        for j in range(half):
            g = _dot(h, wgu_ref[j])
            u = _dot(h, wgu_ref[j + half])
            gu_ref[j] = g
            gu_ref[j + half] = u
            a = ((g * _sigmoid(g)) * u).astype(BF16)
            acc = acc + _dot(a, wd_ref[j * C : (j + 1) * C, :])
        if target is None:
            xo_ref[...] = acc
        else:
            s_ref = refs[2]

            @pl.when(pl.program_id(0) == 0)
            def _():
                s_ref[...] = jnp.zeros_like(s_ref)

            e = acc - t_ref[...]
            xo_ref[...] = e * (1.0 / D)
            s_ref[...] += jnp.sum(jnp.sum(e * e, axis=-1, keepdims=True), axis=0, keepdims=True)

    args = [x, gain, w_gu, w_d]
    in_specs = [tile, _resident((1, D)), _resident((N_DEV, D, C)), _resident((half * C, D))]
    out_specs = [tile, pl.BlockSpec((N_DEV, tm, C), lambda i: (0, i, 0))]
    out_shape = [jax.ShapeDtypeStruct((T, D), F32), jax.ShapeDtypeStruct((N_DEV, T, C), F32)]
    if attn is not None:
        args += list(attn)
        in_specs += [pl.BlockSpec((tm, attn[0].shape[1]), row), _resident(attn[1].shape)]
        out_specs.insert(0, tile)
        out_shape.insert(0, jax.ShapeDtypeStruct((T, D), F32))
    if target is not None:
        args.append(target)
        in_specs.append(tile)
        out_specs.append(pl.BlockSpec((8, LANES), lambda i: (0, 0)))
        out_shape.append(jax.ShapeDtypeStruct((8, LANES), F32))
    return _call(body, plan=plan, args=args, name="ffn_fwd", grid=(T // tm,), in_specs=in_specs, out_specs=out_specs,
                 out_shape=out_shape)


def ffn_bwd(dxo, x, gain, gu, w_gu, w_d, *, tm, plan=None, w_o=None):
    T, D = x.shape
    _, _, C = w_gu.shape
    half = N_DEV // 2

    def body(d_ref, x_ref, g_ref, gu_ref, wgu_ref, wd_ref, *rest):
        if w_o is not None:
            wo_ref, rest = rest[0], rest[1:]
        dx_ref, a_ref, dgu_ref, h_ref, d16_ref, dg_ref = rest[:6]

        @pl.when(pl.program_id(0) == 0)
        def _():
            dg_ref[...] = jnp.zeros_like(dg_ref)

        d = d_ref[...]
        d16 = d.astype(BF16)
        d16_ref[...] = d16
        dh = jnp.zeros((tm, D), F32)
        for j in range(half):
            g = gu_ref[j]
            u = gu_ref[j + half]
            da = _dot_nt(d16, wd_ref[j * C : (j + 1) * C, :])
            s = _sigmoid(g)
            sg = g * s
            a_ref[j] = (sg * u).astype(BF16)
            dg16 = (da * u * (s + sg * (1.0 - s))).astype(BF16)
            du16 = (da * sg).astype(BF16)
            dgu_ref[j] = dg16
            dgu_ref[j + half] = du16
            dh = dh + _dot_nt(dg16, wgu_ref[j]) + _dot_nt(du16, wgu_ref[j + half])
        xt = x_ref[...]
        r = _rms(xt)
        gn = g_ref[...]
        h_ref[...] = ((xt * r) * gn).astype(BF16)
        dx, dgn = _rms_bwd(xt, r, gn, dh)
        dg_ref[0:1, :] += dgn
        dxi = d + dx
        dx_ref[...] = dxi
        if w_o is not None:
            dxi16_ref, dao_ref = rest[6:8]
            dxi16 = dxi.astype(BF16)
            dxi16_ref[...] = dxi16
            dao_ref[...] = _dot_nt(dxi16, wo_ref[...]).astype(BF16)

    tile = pl.BlockSpec((tm, D), lambda i: (i, 0))
    args = [dxo, x, gain, gu, w_gu, w_d]
    in_specs = [tile, tile, _resident((1, D)), pl.BlockSpec((N_DEV, tm, C), lambda i: (0, i, 0)), _resident((N_DEV, D, C)),
                _resident((half * C, D))]
    out_specs = [tile, pl.BlockSpec((half, tm, C), lambda i: (0, i, 0)), pl.BlockSpec((N_DEV, tm, C), lambda i: (0, i, 0)),
                 tile, tile, pl.BlockSpec((8, D), lambda i: (0, 0))]
    out_shape = [
        jax.ShapeDtypeStruct((T, D), F32),
        jax.ShapeDtypeStruct((half, T, C), BF16),
        jax.ShapeDtypeStruct((N_DEV, T, C), BF16),
        jax.ShapeDtypeStruct((T, D), BF16),
        jax.ShapeDtypeStruct((T, D), BF16),
        jax.ShapeDtypeStruct((8, D), F32),
    ]
    if w_o is not None:
        args.append(w_o)
        in_specs.append(_resident(w_o.shape))
        out_specs += [tile, pl.BlockSpec((tm, w_o.shape[0]), lambda i: (i, 0))]
        out_shape += [jax.ShapeDtypeStruct((T, D), BF16), jax.ShapeDtypeStruct((T, w_o.shape[0]), BF16)]
    return _call(body, plan=plan, args=args, name="ffn_bwd", grid=(T // tm,), in_specs=in_specs, out_specs=out_specs,
                 out_shape=out_shape)


def wgrad(a, b, *, name, a_chunks=False, b_chunks=False, b_cols=0, tk, out_dtype=BF16, plan=None):
    T, K = a.shape[-2:]
    J = 1
    if a_chunks:
        J = a.shape[0]
        a_spec = pl.BlockSpec((None, tk, K), lambda j, k: (j, k, 0))
    else:
        a_spec = pl.BlockSpec((tk, K), lambda j, k: (k, 0))
    if b_chunks:
        J, _, N = b.shape
        b_spec = pl.BlockSpec((None, tk, N), lambda j, k: (j, k, 0))
    elif b_cols:
        N = b_cols
        J = b.shape[1] // N
        b_spec = pl.BlockSpec((tk, N), lambda j, k: (k, j))
    else:
        N = b.shape[1]
        b_spec = pl.BlockSpec((tk, N), lambda j, k: (k, 0))
    nk = T // tk

    def body(a_ref, b_ref, o_ref, acc_ref):
        k = pl.program_id(1)

        @pl.when(k == 0)
        def _():
            acc_ref[...] = jnp.zeros_like(acc_ref)

        acc_ref[...] += _dot_tn(a_ref[...], b_ref[...])

        @pl.when(k == nk - 1)
        def _():
            o_ref[...] = acc_ref[...].astype(out_dtype)

    outs, sent = _call(
        body,
        plan=plan,
        args=(a, b),
        name=name,
        grid=(J, nk),
        in_specs=[a_spec, b_spec],
        out_specs=[pl.BlockSpec((None, K, N), lambda j, k: (j, 0, 0))],
        out_shape=[jax.ShapeDtypeStruct((J, K, N), out_dtype)],
        scratch=[pltpu.VMEM((K, N), F32)],
    )
    return outs[0], sent


def _seg(x, lo):
    s_lo = jnp.sum(jnp.where(lo, x, 0.0), axis=-1, keepdims=True)
    s_hi = jnp.sum(jnp.where(lo, 0.0, x), axis=-1, keepdims=True)
    return jnp.where(lo, s_lo, s_hi)


def _head_norm(x, gain, lo):
    r = lax.rsqrt(_seg(x * x, lo) * (1.0 / HEAD_DIM) + EPS)
    return (x * r) * gain, r


def _head_norm_bwd(x, r, gain, dy, lo):
    xn = x * r
    dxn = dy * gain
    dx = r * (dxn - xn * (_seg(dxn * xn, lo) * (1.0 / HEAD_DIM)))
    return dx, jnp.sum(dy * xn, axis=0, keepdims=True)


def _swap_halves(x):
    return pltpu.roll(x, HEAD_DIM, 1)


def qkv_proj(x, gain, w, qg, kg, *, tm):
    T, D = x.shape
    N = w.shape[1]
    kvw = N_KV_HEADS * HEAD_DIM
    nqt, nkt = D // LANES, kvw // LANES

    def body(x_ref, g_ref, w_ref, qg_ref, kg_ref, qkv_ref, q_ref, kd_ref, vd_ref):
        xt = x_ref[...]
        h = ((xt * _rms(xt)) * g_ref[...]).astype(BF16)
        qkv = _dot(h, w_ref[...])
        qkv_ref[...] = qkv
        lo = lax.broadcasted_iota(jnp.int32, (1, LANES), 1) < HEAD_DIM
        for t in range(nqt):
            qn, _ = _head_norm(qkv[:, t * LANES : (t + 1) * LANES], qg_ref[...], lo)
            q_ref[:, t * LANES : (t + 1) * LANES] = (qn * SCALE).astype(BF16)
        for t in range(nkt):
            kn, _ = _head_norm(qkv[:, D + t * LANES : D + (t + 1) * LANES], kg_ref[...], lo)
            v = qkv[:, D + kvw + t * LANES : D + kvw + (t + 1) * LANES]
            for src, dst in ((kn, kd_ref), (v, vd_ref)):
                sw = _swap_halves(src)
                dst[:, 2 * t * LANES : (2 * t + 1) * LANES] = jnp.where(lo, src, sw).astype(BF16)
                dst[:, (2 * t + 1) * LANES : (2 * t + 2) * LANES] = jnp.where(lo, sw, src).astype(BF16)

    row = lambda i: (i, 0)
    return pl.pallas_call(
        body,
        name="qkv_proj",
        grid=(T // tm,),
        in_specs=[pl.BlockSpec((tm, D), row), _resident((1, D)), _resident((D, N)), _resident((1, LANES)), _resident((1, LANES))],
        out_specs=[pl.BlockSpec((tm, N), row), pl.BlockSpec((tm, D), row), pl.BlockSpec((tm, 2 * kvw), row), pl.BlockSpec((tm, 2 * kvw), row)],
        out_shape=[
            jax.ShapeDtypeStruct((T, N), F32),
            jax.ShapeDtypeStruct((T, D), BF16),
            jax.ShapeDtypeStruct((T, 2 * kvw), BF16),
            jax.ShapeDtypeStruct((T, 2 * kvw), BF16),
        ],
        compiler_params=_params(1),
    )(x, gain, w, qg, kg)


def _attn_tables(sinks, n_q_heads):
    P = n_q_heads // N_KV_HEADS // 2
    h = jnp.arange(1, n_q_heads + 1, dtype=F32)
    slopes = jnp.exp2(-8.0 * h / n_q_heads).reshape(N_KV_HEADS, P, 1, 2, 1)
    qi = jnp.arange(BLOCK)[:, None]
    kj = jnp.arange(BLOCK)[None, :]
    dist = jnp.where(kj <= qi, qi - kj, qi + BLOCK - kj).astype(F32)
    shape = (N_KV_HEADS, P, BLOCK, 2, BLOCK)
    bias = jnp.broadcast_to(-slopes * dist[None, None, :, None, :], shape)
    sink = jnp.broadcast_to(sinks.astype(F32).reshape(N_KV_HEADS, P, 1, 2, 1), shape)
    return bias.reshape(N_KV_HEADS, P * BLOCK, 2 * BLOCK), sink.reshape(N_KV_HEADS, P * BLOCK, 2 * BLOCK)


def _attn_specs(D, nb):
    kvw2 = 2 * N_KV_HEADS * HEAD_DIM
    cur = lambda b, i: (b * nb + i, 0)
    prev = lambda b, i: (jnp.maximum(b * nb + i - 1, 0), 0)
    return [
        pl.BlockSpec((BLOCK, D), cur),
        pl.BlockSpec((BLOCK, kvw2), cur),
        pl.BlockSpec((BLOCK, kvw2), prev),
        pl.BlockSpec((BLOCK, kvw2), cur),
        pl.BlockSpec((BLOCK, kvw2), prev),
    ]


def _attn_operands(kh, P, lo, q_ref, kc_ref, kp_ref, vc_ref, vp_ref):
    sl = slice(kh * LANES, (kh + 1) * LANES)

    def cat(prev_ref, cur_ref):
        d = jnp.concatenate([prev_ref[:, sl], cur_ref[:, sl]], axis=0)
        z = jnp.zeros_like(d)
        return jnp.concatenate([jnp.where(lo, d, z), jnp.where(lo, z, d)], axis=0)

    qt = jnp.concatenate([q_ref[:, (kh * P + pr) * LANES : (kh * P + pr + 1) * LANES] for pr in range(P)], axis=0)
    return qt, cat(kp_ref, kc_ref), cat(vp_ref, vc_ref)


def _attn_exp(s_all, bias, sink, tri, first):
    out = []
    for par in range(2):
        c0 = 2 * par * BLOCK
        s = jnp.where(tri, s_all[:, c0 + BLOCK : c0 + 2 * BLOCK], jnp.where(first, NEG, s_all[:, c0 : c0 + BLOCK]))
        s = s + bias[:, par * BLOCK : (par + 1) * BLOCK]
        snk = sink[:, par * BLOCK : (par + 1) * BLOCK]
        m = jnp.maximum(jnp.max(s, axis=-1, keepdims=True), snk)
        out.append((jnp.exp(s - m), jnp.exp(snk - m)))
    return out


def _unfold(x, tri):
    z = jnp.zeros_like(x)
    return jnp.concatenate([jnp.where(tri, z, x), jnp.where(tri, x, z)], axis=1)


def _attn_masks(R):
    lane = lax.broadcasted_iota(jnp.int32, (1, LANES), 1)
    row = lax.broadcasted_iota(jnp.int32, (R, BLOCK), 0) & (BLOCK - 1)
    col = lax.broadcasted_iota(jnp.int32, (R, BLOCK), 1)
    return lane, lane < HEAD_DIM, col <= row


def attn_fwd(q16, kd, vd, bias, sink, *, seq, n_seq):
    T, D = q16.shape
    nb = seq // BLOCK
    P = D // HEAD_DIM // N_KV_HEADS // 2
    R = P * BLOCK
    KV = range(N_KV_HEADS)

    def body(q_ref, kc_ref, kp_ref, vc_ref, vp_ref, bias_ref, sink_ref, o_ref):
        first = pl.program_id(1) == 0
        _, lo, tri = _attn_masks(R)
        r4 = lax.broadcasted_iota(jnp.int32, (4 * BLOCK, LANES), 0)
        l4 = lax.broadcasted_iota(jnp.int32, (4 * BLOCK, LANES), 1)
        ones = ((r4 < 2 * BLOCK) == (l4 < HEAD_DIM)).astype(BF16)
        ops = [_attn_operands(kh, P, lo, q_ref, kc_ref, kp_ref, vc_ref, vp_ref) for kh in KV]
        s_all = [_dot_nt(ops[kh][0], ops[kh][1]) for kh in KV]
        ex = [_attn_exp(s_all[kh], bias_ref[kh], sink_ref[kh], tri, first) for kh in KV]
        lhs = [jnp.concatenate([_unfold(e, tri) for e, _ in ex[kh]], axis=1).astype(BF16) for kh in KV]
        o = [_dot(lhs[kh], ops[kh][2]) for kh in KV]
        den = [_dot(lhs[kh], ones) for kh in KV]
        for kh in KV:
            out = o[kh] / (den[kh] + jnp.where(lo, ex[kh][0][1], ex[kh][1][1]))
            for pr in range(P):
                t = kh * P + pr
                o_ref[:, t * LANES : (t + 1) * LANES] = out[pr * BLOCK : (pr + 1) * BLOCK, :].astype(BF16)

    return pl.pallas_call(
        body,
        name="attn_fwd",
        grid=(n_seq, nb),
        in_specs=_attn_specs(D, nb) + [_resident((N_KV_HEADS, R, 2 * BLOCK)), _resident((N_KV_HEADS, R, 2 * BLOCK))],
        out_specs=pl.BlockSpec((BLOCK, D), lambda b, i: (b * nb + i, 0)),
        out_shape=jax.ShapeDtypeStruct((T, D), BF16),
        compiler_params=_params(2),
    )(q16, kd, kd, vd, vd, bias, sink)


def attn_bwd(q16, kd, vd, do, bias, sink, *, seq, n_seq):
    T, D = q16.shape
    kvw2 = 2 * N_KV_HEADS * HEAD_DIM
    nb = seq // BLOCK
    G = D // HEAD_DIM // N_KV_HEADS
    P = G // 2
    R = P * BLOCK
    KV = range(N_KV_HEADS)

    def body(q_ref, kc_ref, kp_ref, vc_ref, vp_ref, do_ref, bias_ref, sink_ref,
             dq_ref, dkc_ref, dkp_ref, dvc_ref, dvp_ref, dsink_ref):
        first = pl.program_id(1) == 0

        @pl.when(jnp.logical_and(pl.program_id(0) == 0, first))
        def _():
            dsink_ref[...] = jnp.zeros_like(dsink_ref)

        lane, lo, tri = _attn_masks(R)
        r4 = lax.broadcasted_iota(jnp.int32, (4 * BLOCK, 2 * BLOCK), 0)
        c4 = lax.broadcasted_iota(jnp.int32, (4 * BLOCK, 2 * BLOCK), 1)
        ones = ((r4 < 2 * BLOCK) == (c4 < BLOCK)).astype(BF16)
        ops = [_attn_operands(kh, P, lo, q_ref, kc_ref, kp_ref, vc_ref, vp_ref) for kh in KV]
        do16 = [jnp.concatenate([do_ref[:, (kh * P + pr) * LANES : (kh * P + pr + 1) * LANES] for pr in range(P)], axis=0)
                for kh in KV]
        s_all = [_dot_nt(ops[kh][0], ops[kh][1]) for kh in KV]
        dp_all = [_dot_nt(do16[kh], ops[kh][2]) for kh in KV]
        ex = [_attn_exp(s_all[kh], bias_ref[kh], sink_ref[kh], tri, first) for kh in KV]
        den = [_dot(jnp.concatenate([_unfold(e, tri) for e, _ in ex[kh]], axis=1).astype(BF16), ones) for kh in KV]
        dsink = jnp.zeros((1, LANES), F32)
        pf, dsf = [], []
        for kh in KV:
            ps_, ds_ = [], []
            for par in range(2):
                e, es = ex[kh][par]
                inv = 1.0 / (den[kh][:, par * BLOCK : (par + 1) * BLOCK] + es)
                p = e * inv
                c0 = 2 * par * BLOCK
                dp = jnp.where(tri, dp_all[kh][:, c0 + BLOCK : c0 + 2 * BLOCK], dp_all[kh][:, c0 : c0 + BLOCK])
                delta = jnp.sum(p * dp, axis=-1, keepdims=True)
                ds_.append(_unfold(p * (dp - delta), tri))
                ps_.append(_unfold(p, tri))
                dsr = -((es * inv) * delta)
                for pr in range(P):
                    hq = kh * G + 2 * pr + par
                    tot = jnp.sum(dsr[pr * BLOCK : (pr + 1) * BLOCK, :], axis=0, keepdims=True)
                    dsink = dsink + jnp.where(lane == hq, tot, 0.0)
            pf.append(jnp.concatenate(ps_, axis=1).astype(BF16))
            dsf.append(jnp.concatenate(ds_, axis=1).astype(BF16))
        dq = [_dot(dsf[kh], ops[kh][1]) for kh in KV]
        dk = [_dot_tn(dsf[kh], ops[kh][0]) for kh in KV]
        dv = [_dot_tn(pf[kh], do16[kh]) for kh in KV]
        dsink_ref[0:1, :] += dsink
        for kh in KV:
            sl = slice(kh * LANES, (kh + 1) * LANES)
            for pr in range(P):
                t = kh * P + pr
                dq_ref[:, t * LANES : (t + 1) * LANES] = dq[kh][pr * BLOCK : (pr + 1) * BLOCK, :]
            for full, prev_ref, cur_ref in ((dk[kh], dkp_ref, dkc_ref), (dv[kh], dvp_ref, dvc_ref)):
                dup = jnp.where(lo, full[: 2 * BLOCK, :], full[2 * BLOCK :, :])
                prev_ref[:, sl] = dup[:BLOCK, :]
                cur_ref[:, sl] = dup[BLOCK:, :]

    cur = lambda b, i: (b * nb + i, 0)
    kv_spec = pl.BlockSpec((BLOCK, kvw2), cur)
    kv_shape = jax.ShapeDtypeStruct((T, kvw2), F32)
    return pl.pallas_call(
        body,
        name="attn_bwd",
        grid=(n_seq, nb),
        in_specs=_attn_specs(D, nb)
        + [pl.BlockSpec((BLOCK, D), cur), _resident((N_KV_HEADS, R, 2 * BLOCK)), _resident((N_KV_HEADS, R, 2 * BLOCK))],
        out_specs=[pl.BlockSpec((BLOCK, D), cur), kv_spec, kv_spec, kv_spec, kv_spec, pl.BlockSpec((8, LANES), lambda b, i: (0, 0))],
        out_shape=[jax.ShapeDtypeStruct((T, D), F32), kv_shape, kv_shape, kv_shape, kv_shape, jax.ShapeDtypeStruct((8, LANES), F32)],
        compiler_params=_params(2),
    )(q16, kd, kd, vd, vd, do, bias, sink)


def qkv_bwd(dq, dkc, dkp, dvc, dvp, qkv, dres, x, gain, w_qkv, qg, kg, *, seq):
    T, D = x.shape
    kvw2 = dkc.shape[1]
    kvw = kvw2 // 2
    nqt, nkt = D // LANES, kvw // LANES
    nb = seq // BLOCK
    tm = 2 * BLOCK
    n = T // tm

    def body(dq_ref, dkc_ref, dkpa_ref, dkpb_ref, dvc_ref, dvpa_ref, dvpb_ref, qkv_ref, dres_ref, x_ref, g_ref, w_ref,
             qg_ref, kg_ref, dx_ref, dqkv_ref, h_ref, dg_ref, hg_ref):
        i = pl.program_id(0)

        @pl.when(i == 0)
        def _():
            dg_ref[...] = jnp.zeros_like(dg_ref)
            hg_ref[...] = jnp.zeros_like(hg_ref)

        lo = lax.broadcasted_iota(jnp.int32, (1, LANES), 1) < HEAD_DIM
        last = (2 * i + 1) % nb == nb - 1
        dkd = dkc_ref[...] + jnp.concatenate([dkpa_ref[...], jnp.where(last, 0.0, dkpb_ref[...])], axis=0)
        dvd = dvc_ref[...] + jnp.concatenate([dvpa_ref[...], jnp.where(last, 0.0, dvpb_ref[...])], axis=0)

        def undup(d, t):
            a, b = d[:, 2 * t * LANES : (2 * t + 1) * LANES], d[:, (2 * t + 1) * LANES : (2 * t + 2) * LANES]
            return jnp.where(lo, a + _swap_halves(a), b + _swap_halves(b))

        gq = jnp.zeros((1, LANES), F32)
        for t in range(nqt):
            sl = slice(t * LANES, (t + 1) * LANES)
            q = qkv_ref[:, sl]
            _, r = _head_norm(q, qg_ref[...], lo)
            dxq, dgn = _head_norm_bwd(q, r, qg_ref[...], dq_ref[:, sl] * SCALE, lo)
            dqkv_ref[:, sl] = dxq.astype(BF16)
            gq = gq + dgn
        gk = jnp.zeros((1, LANES), F32)
        for t in range(nkt):
            sl = slice(D + t * LANES, D + (t + 1) * LANES)
            k = qkv_ref[:, sl]
            _, r = _head_norm(k, kg_ref[...], lo)
            dxk, dgn = _head_norm_bwd(k, r, kg_ref[...], undup(dkd, t), lo)
            dqkv_ref[:, sl] = dxk.astype(BF16)
            gk = gk + dgn
            dqkv_ref[:, D + kvw + t * LANES : D + kvw + (t + 1) * LANES] = undup(dvd, t).astype(BF16)
        hg_ref[0:1, :] += gq
        hg_ref[1:2, :] += gk
        dh = _dot_nt(dqkv_ref[...], w_ref[...])
        xt = x_ref[...]
        r = _rms(xt)
        gn = g_ref[...]
        h_ref[...] = ((xt * r) * gn).astype(BF16)
        dx, dgn = _rms_bwd(xt, r, gn, dh)
        dg_ref[0:1, :] += dgn
        dx_ref[...] = dres_ref[...] + dx

    row = lambda i: (i, 0)
    nxt_a = pl.BlockSpec((BLOCK, kvw2), lambda i: (2 * i + 1, 0))
    nxt_b = pl.BlockSpec((BLOCK, kvw2), lambda i: (jnp.minimum(2 * i + 2, 2 * n - 1), 0))
    return pl.pallas_call(
        body,
        name="qkv_bwd",
        grid=(n,),
        in_specs=[
            pl.BlockSpec((tm, D), row),
            pl.BlockSpec((tm, kvw2), row),
            nxt_a,
            nxt_b,
            pl.BlockSpec((tm, kvw2), row),
            nxt_a,
            nxt_b,
            pl.BlockSpec((tm, D + kvw2), row),
            pl.BlockSpec((tm, D), row),
            pl.BlockSpec((tm, D), row),
            _resident((1, D)),
            _resident((D, D + kvw2)),
            _resident((1, LANES)),
            _resident((1, LANES)),
        ],
        out_specs=[
            pl.BlockSpec((tm, D), row),
            pl.BlockSpec((tm, D + kvw2), row),
            pl.BlockSpec((tm, D), row),
            pl.BlockSpec((8, D), lambda i: (0, 0)),
            pl.BlockSpec((8, LANES), lambda i: (0, 0)),
        ],
        out_shape=[
            jax.ShapeDtypeStruct((T, D), F32),
            jax.ShapeDtypeStruct((T, D + kvw2), BF16),
            jax.ShapeDtypeStruct((T, D), BF16),
            jax.ShapeDtypeStruct((8, D), F32),
            jax.ShapeDtypeStruct((8, LANES), F32),
        ],
        compiler_params=_params(1),
    )(dq, dkc, dkp, dkp, dvc, dvp, dvp, qkv, dres, x, gain, w_qkv, qg, kg)


def local_step(x, target, gains, w, *, seq, tm=256, tk=2048, shards=None):
    T, D = x.shape
    n_seq = T // seq
    nm, nf, qgain, kgain, sinks = gains
    H = D // HEAD_DIM
    tk = min(tk, T)
    qg2, kg2 = jnp.tile(qgain, (1, 2)), jnp.tile(kgain, (1, 2))
    bias, sinkcol = _attn_tables(sinks, H)

    dist = shards is not None
    w = dict(w)

    plan = _Gather([shards["w_gu"][0], shards["w_d"][0]]) if dist else None
    (x1, bcx), got = conv_fwd(x, nm[0:1], w["w_in"], w["cw"], w["w_out"], seq=seq, tm=tm, plan=plan)
    if dist:
        w["w_gu"], w["w_d"] = [got[0], None], [got[1].reshape(-1, D), None]
    plan = _Gather([shards["w_qkv"], shards["w_o"], shards["w_gu"][1], shards["w_d"][1]]) if dist else None
    (x2, gu0), got = ffn_fwd(x1, nf[0:1], w["w_gu"][0], w["w_d"][0], tm=tm, plan=plan)
    if dist:
        w["w_qkv"], w["w_o"] = got[0].transpose(1, 0, 2).reshape(D, -1), got[1].reshape(D, D)
        w["w_gu"][1], w["w_d"][1] = got[2], got[3].reshape(-1, D)
    qkv, q16, kd, vd = qkv_proj(x2, nm[1:2], w["w_qkv"], qg2, kg2, tm=tm)
    ao = attn_fwd(q16, kd, vd, bias, sinkcol, seq=seq, n_seq=n_seq)
    (x3, dx4, gu1, sse), _ = ffn_fwd(x2, nf[1:2], w["w_gu"][1], w["w_d"][1], tm=tm, attn=(ao, w["w_o"]), target=target)

    C = w["w_gu"][0].shape[2]
    by_dest = lambda a: a.reshape(N_DEV, -1, a.shape[-1])
    fresh = lambda a, *lead: jax.ShapeDtypeStruct((N_DEV,) + lead + a.shape[1:], a.dtype)

    (dx3, a16, dgu, h16, d16, dnf1, dx3_16, dao), _ = ffn_bwd(
        dx4, x3, nf[1:2], gu1, w["w_gu"][1], w["w_d"][1], tm=tm, w_o=w["w_o"])
    g_gu1, _ = wgrad(h16, dgu, name="wgrad_gu1", b_chunks=True, tk=tk)
    g_d1 = by_dest(wgrad(a16, d16, name="wgrad_d1", a_chunks=True, tk=tk)[0])
    g_o = by_dest(wgrad(ao, dx3_16, name="wgrad_o", tk=tk)[0])
    dq, dkc, dkp, dvc, dvp, dsinks = attn_bwd(q16, kd, vd, dao, bias, sinkcol, seq=seq, n_seq=n_seq)
    dx2, dqkv16, h16, dnm1, dgains = qkv_bwd(dq, dkc, dkp, dvc, dvp, qkv, dx3, x2, nm[1:2], w["w_qkv"], qg2, kg2, seq=seq)
    nq = dqkv16.shape[1]
    g_qkv = wgrad(h16, dqkv16, name="wgrad_qkv", tk=tk)[0].reshape(D, N_DEV, nq // N_DEV).transpose(1, 0, 2)

    plan = _Scatter([(g_gu1, fresh(g_gu1, 2), 1, False), (g_d1, fresh(g_d1, 2), 1, False)]) if dist else None
    (dx1, a16, dgu, h16, d16, dnf0), land_ffn = ffn_bwd(dx2, x1, nf[0:1], gu0, w["w_gu"][0], w["w_d"][0], tm=tm, plan=plan)
    plan = _Scatter([(g_o, fresh(g_o), None, False), (g_qkv, fresh(g_qkv), None, False)]) if dist else None
    g_gu0, land_attn = wgrad(h16, dgu, name="wgrad_gu0", b_chunks=True, tk=tk, plan=plan)
    g_d0 = by_dest(wgrad(a16, d16, name="wgrad_d0", a_chunks=True, tk=tk)[0])
    plan = _Scatter([(g_gu0, land_ffn[0], 0, False)]) if dist else None
    (gx, dbcx, h16, z16, d16, dcw, dnm0), land_gu = conv_bwd(
        dx1, x, nm[0:1], bcx, w["cw"], w["w_in"], w["w_out"], seq=seq, tm=tm, plan=plan)
    g_out = by_dest(wgrad(z16, d16, name="wgrad_out", tk=tk)[0])
    g_cw = dcw[0:3].reshape(3, N_DEV, D // N_DEV).transpose(1, 0, 2)
    plan = _Scatter([(g_d0, land_ffn[1], 0, False), (g_out, fresh(g_out), None, False), (g_cw, fresh(g_cw), None, False)]) if dist else None
    g_in, land_last = wgrad(h16, dbcx, name="wgrad_in", b_cols=3 * D // N_DEV, tk=tk, plan=plan)
    g = dict(w_in=g_in, cw=g_cw, w_out=g_out)
    if dist:
        g.update(w_gu=land_gu[0], w_d=land_last[0], w_out=land_last[1], cw=land_last[2], w_o=land_attn[0], w_qkv=land_attn[1])
    else:
        g.update(w_gu0=g_gu0, w_gu1=g_gu1, w_d0=g_d0, w_d1=g_d1, w_o=g_o, w_qkv=g_qkv)
    small = dict(nm0=dnm0, nm1=dnm1, nf0=dnf0, nf1=dnf1, gains=dgains, sinks=dsinks)
    return sse, gx, g, small


def _adamw_math(g, w, m, v):
    m = ADAM_B1 * m + (1.0 - ADAM_B1) * g
    v = ADAM_B2 * v + (1.0 - ADAM_B2) * (g * g)
    m_hat = m / (1.0 - ADAM_B1 ** ADAM_STEP)
    v_hat = v / (1.0 - ADAM_B2 ** ADAM_STEP)
    delta = -ADAM_LR * (m_hat / (jnp.sqrt(v_hat) + ADAM_EPS) + ADAM_WD * w)
    return delta, m, v


def adamw(parts, w, m, v, *, name):
    n, R, C = parts.shape
    tr = R
    for cand in (256, 128, 88, 64, 32, 16, 8):
        if R > cand and R % cand == 0:
            tr = cand
            break

    def body(p_ref, w_ref, m_ref, v_ref, g_ref, d_ref, mo_ref, vo_ref):
        g = p_ref[0].astype(F32)
        for s in range(1, n):
            g = g + p_ref[s].astype(F32)
        g_ref[...] = g
        d_ref[...], mo_ref[...], vo_ref[...] = _adamw_math(g, w_ref[...], m_ref[...], v_ref[...])

    blk = pl.BlockSpec((tr, C), lambda i: (i, 0))
    return pl.pallas_call(
        body,
        name=name,
        grid=(R // tr,),
        in_specs=[pl.BlockSpec((n, tr, C), lambda i: (0, i, 0)), blk, blk, blk],
        out_specs=[blk] * 4,
        out_shape=[jax.ShapeDtypeStruct((R, C), F32)] * 4,
        compiler_params=_params(1),
    )(parts, w, m, v)


def pack_small(small, D):
    W = max(D, 2 * LANES)

    def body(nm0, nm1, nf0, nf1, gains, sinks, o_ref):
        o_ref[...] = jnp.zeros_like(o_ref)
        o_ref[0:1, :D] = nm0[0:1, :]
        o_ref[1:2, :D] = nm1[0:1, :]
        o_ref[2:3, :D] = nf0[0:1, :]
        o_ref[3:4, :D] = nf1[0:1, :]
        gq = gains[0:1, :] + pltpu.roll(gains[0:1, :], HEAD_DIM, 1)
        gk = gains[1:2, :] + pltpu.roll(gains[1:2, :], HEAD_DIM, 1)
        lane = lax.broadcasted_iota(jnp.int32, (1, LANES), 1)
        o_ref[4:5, :LANES] = jnp.where(lane < HEAD_DIM, gq, gk)
        o_ref[4:5, LANES : 2 * LANES] = sinks[0:1, :]

    return pl.pallas_call(
        body,
        name="pack_small",
        out_shape=jax.ShapeDtypeStruct((8, W), F32),
    )(small["nm0"], small["nm1"], small["nf0"], small["nf1"], small["gains"], small["sinks"])


def _pack_small_params(nm, nf, qg, kg, sk, D):
    W = max(D, 2 * LANES)
    row4 = jnp.concatenate([qg.reshape(-1), kg.reshape(-1), jnp.zeros((LANES - 2 * HEAD_DIM,), F32), sk.reshape(-1)])
    row4 = jnp.pad(row4, (0, W - row4.shape[0]))
    rows = [jnp.pad(r, (0, W - D)) for r in (nm[0], nm[1], nf[0], nf[1])] + [row4]
    return jnp.concatenate([jnp.stack(rows), jnp.zeros((3, W), F32)], axis=0)


def _unpack_small(a, D, H):
    nm = a[0:2, :D]
    nf = a[2:4, :D]
    qg = a[4:5, 0:HEAD_DIM]
    kg = a[4:5, HEAD_DIM : 2 * HEAD_DIM]
    sk = a[4:5, LANES : LANES + H]
    return qg, kg, sk, nm, nf


def kernel(x, conv_w_in, conv_w, conv_w_out, attn_w_qkv, attn_q_gain, attn_k_gain, attn_sinks, attn_w_o, norm_mixer, norm_ffn, ffn_w_gate_up, ffn_w_down, loss_target, m_conv_w_in, m_conv_w, m_conv_w_out, m_attn_w_qkv, m_attn_q_gain, m_attn_k_gain, m_attn_sinks, m_attn_w_o, m_norm_mixer, m_norm_ffn, m_ffn_w_gate_up, m_ffn_w_down, v_conv_w_in, v_conv_w, v_conv_w_out, v_attn_w_qkv, v_attn_q_gain, v_attn_k_gain, v_attn_sinks, v_attn_w_o, v_norm_mixer, v_norm_ffn, v_ffn_w_gate_up, v_ffn_w_down):
    n_seq, seq, D = x.shape
    T = n_seq * seq
    H = D // HEAD_DIM
    L = ffn_w_gate_up.shape[0]

    full = run_plan(_Gather([conv_w_in[0].astype(BF16), conv_w[0], conv_w_out[0].astype(BF16)]), name="gather_conv_weights")
    w = dict(w_in=full[0].transpose(1, 0, 2).reshape(D, 3 * D), cw=full[1].transpose(1, 0, 2).reshape(3, D),
             w_out=full[2].reshape(D, D))
    shards = dict(w_gu=[ffn_w_gate_up[l].astype(BF16) for l in range(L)], w_d=[ffn_w_down[l].astype(BF16) for l in range(L)],
                  w_qkv=attn_w_qkv[0].astype(BF16), w_o=attn_w_o[0].astype(BF16))
    gains = (norm_mixer, norm_ffn, attn_q_gain, attn_k_gain, attn_sinks)
    sse, gx, g, small = local_step(x.reshape(T, D), loss_target.reshape(T, D), gains, w, seq=seq, shards=shards)
    loss = lax.psum(sse[0, 0] * (0.5 / D), ("x", "y", "c"))

    packed = pack_small(small, D)
    items = [(g["w_in"], jax.ShapeDtypeStruct(g["w_in"].shape, BF16), None, False),
             (packed, jax.ShapeDtypeStruct((N_DEV,) + packed.shape, packed.dtype), None, True)]
    land_in, small_all = run_plan(_Scatter(items), name="exchange_last_grads")
    bufs = [land_in, g["cw"], g["w_out"], g["w_qkv"], g["w_o"], g["w_gu"], g["w_d"]]

    def flat(a):
        return a.reshape(-1, a.shape[-1])

    big = [conv_w_in, conv_w, conv_w_out, attn_w_qkv, attn_w_o, ffn_w_gate_up, ffn_w_down]
    big_m = [m_conv_w_in, m_conv_w, m_conv_w_out, m_attn_w_qkv, m_attn_w_o, m_ffn_w_gate_up, m_ffn_w_down]
    big_v = [v_conv_w_in, v_conv_w, v_conv_w_out, v_attn_w_qkv, v_attn_w_o, v_ffn_w_gate_up, v_ffn_w_down]
    tags = ["in", "cw", "out", "qkv", "o", "gu", "d"]
    res = []
    for b in range(7):
        parts = bufs[b].reshape(N_DEV, -1, bufs[b].shape[-1])
        outs = adamw(parts, flat(big[b]), flat(big_m[b]), flat(big_v[b]), name="adamw_" + tags[b])
        res.append([o.reshape(big[b].shape) for o in outs])
    sw = _pack_small_params(norm_mixer, norm_ffn, attn_q_gain, attn_k_gain, attn_sinks, D)
    sm = _pack_small_params(m_norm_mixer, m_norm_ffn, m_attn_q_gain, m_attn_k_gain, m_attn_sinks, D)
    sv = _pack_small_params(v_norm_mixer, v_norm_ffn, v_attn_q_gain, v_attn_k_gain, v_attn_sinks, D)
    souts = adamw(small_all, sw, sm, sv, name="adamw_small")
    sres = [_unpack_small(o, D, H) for o in souts]

    def ordered(i):
        r, s = [r[i] for r in res], sres[i]
        return [r[0], r[1], r[2], r[3], s[0], s[1], s[2], r[4], s[3], s[4], r[5], r[6]]

    return (loss, gx.reshape(n_seq, seq, D), *ordered(0), *ordered(1), *ordered(2), *ordered(3))
```

```python
import functools
import math

import jax
import jax.numpy as jnp
from jax import lax
from jax.experimental import pallas as pl
from jax.experimental.pallas import tpu as pltpu

F32 = jnp.float32
BF16 = jnp.bfloat16

EPS = 1e-6
HEAD_DIM = 64
N_KV_HEADS = 4
BLOCK = 128
LANES = 128
N_DEV = 8
NEG = -1e30
SCALE = 1.0 / math.sqrt(HEAD_DIM)

ADAM_LR = 0.001
ADAM_B1 = 0.9
ADAM_B2 = 0.999
ADAM_EPS = 1e-08
ADAM_WD = 0.01
ADAM_STEP = 10

V7X_VMEM_BYTES = 64 * 1024 * 1024
VMEM_LIMIT = V7X_VMEM_BYTES - 8 * 1024 * 1024
MESH = pl.DeviceIdType.MESH

_NT = (((1,), (1,)), ((), ()))
_TN = (((0,), (0,)), ((), ()))


def _params(n_grid):
    return pltpu.CompilerParams(dimension_semantics=("arbitrary",) * n_grid, vmem_limit_bytes=VMEM_LIMIT)


def _resident(shape):
    nd = len(shape)
    return pl.BlockSpec(shape, lambda *_: (0,) * nd, pipeline_mode=pl.Buffered(1))


def _rms(x):
    return lax.rsqrt(jnp.mean(x * x, axis=-1, keepdims=True) + EPS)


def _rms_bwd(x, r, gain, dh):
    xn = x * r
    dxn = dh * gain
    dx = r * (dxn - xn * jnp.mean(dxn * xn, axis=-1, keepdims=True))
    return dx, jnp.sum(dh * xn, axis=0, keepdims=True)


def _dot(a, b):
    return jnp.dot(a, b, preferred_element_type=F32)


def _dot_nt(a, b):
    return lax.dot_general(a, b, _NT, preferred_element_type=F32)


def _dot_tn(a, b):
    return lax.dot_general(a, b, _TN, preferred_element_type=F32)


def _place():
    return lax.axis_index("x"), lax.axis_index("y"), lax.axis_index("c")


def _flip(v, bit):
    return 1 - v if bit else v


def _slot(px, py, pc):
    return 4 * px + 2 * py + pc


class _Gather:
    def __init__(self, shards):
        nt = len(shards)
        self.nt = nt
        self.inputs = list(shards)
        self.out_shapes = [jax.ShapeDtypeStruct((N_DEV,) + s.shape, s.dtype) for s in shards]
        self.scratch = [pltpu.SemaphoreType.DMA((nt, 7)), pltpu.SemaphoreType.DMA((nt, 7)), pltpu.SemaphoreType.DMA((nt,))]
        self.aliases = {}

    def phases(self, total):
        assert total >= 3
        return [(0, self.start), (total - 2, self.forward), (total - 1, self.finish)]

    def _copies(self, ins, outs, sems):
        send_sems, recv_sems, loc_sems = sems
        x, y, c = _place()
        me = _slot(x, y, c)
        sib = (x, y, 1 - c)
        chips = [(_flip(x, k >> 1), _flip(y, k & 1)) for k in (1, 2, 3)]

        def copy(t, k, src, dst_slot, to):
            return pltpu.make_async_remote_copy(
                src_ref=src, dst_ref=outs[t].at[dst_slot], send_sem=send_sems.at[t, k], recv_sem=recv_sems.at[t, k],
                device_id=to, device_id_type=MESH)

        local = [pltpu.make_async_copy(ins[t], outs[t].at[me], loc_sems.at[t]) for t in range(self.nt)]
        first, passed, arrive_ici, arrive_sib = [], [], [], []
        for t in range(self.nt):
            first.append(copy(t, 0, ins[t], me, sib))
            s = _slot(x, y, 1 - c)
            arrive_sib.append(copy(t, 0, outs[t].at[s], s, sib))
            for j, (px, py) in enumerate(chips):
                first.append(copy(t, 1 + j, ins[t], me, (px, py, c)))
                s = _slot(px, py, c)
                arrive_ici.append(copy(t, 1 + j, outs[t].at[s], s, sib))
                passed.append(copy(t, 4 + j, outs[t].at[s], s, sib))
                s = _slot(px, py, 1 - c)
                arrive_sib.append(copy(t, 4 + j, outs[t].at[s], s, sib))
        return local, first, passed, arrive_ici, arrive_sib

    def start(self, ins, outs, sems):
        local, first, _, _, _ = self._copies(ins, outs, sems)
        for cp in local + first:
            cp.start()

    def forward(self, ins, outs, sems):
        _, _, passed, arrive_ici, _ = self._copies(ins, outs, sems)
        for arrival, fwd in zip(arrive_ici, passed):
            arrival.wait_recv()
            fwd.start()

    def finish(self, ins, outs, sems):
        local, first, passed, _, arrive_sib = self._copies(ins, outs, sems)
        for cp in arrive_sib:
            cp.wait_recv()
        for cp in first + passed:
            cp.wait_send()
        for cp in local:
            cp.wait()


class _Scatter:
    def __init__(self, items):
        self.items = items
        nt = len(items)
        self.nt = nt
        reused = [(t, it[1]) for t, it in enumerate(items) if not isinstance(it[1], jax.ShapeDtypeStruct)]
        self.inputs = [it[0] for it in items] + [land for _, land in reused]
        self.out_shapes = [jax.ShapeDtypeStruct(it[1].shape, it[1].dtype) for it in items]
        self.aliases = {nt + i: t for i, (t, _) in enumerate(reused)}
        self.scratch = [pltpu.SemaphoreType.DMA((nt, 7)), pltpu.SemaphoreType.DMA((nt, 7)), pltpu.SemaphoreType.DMA((nt,))]

    def phases(self, total):
        assert total >= 2
        return [(0, self.start), (total - 1, self.finish)]

    def _copies(self, ins, outs, sems):
        send_sems, recv_sems, loc_sems = sems
        x, y, c = _place()
        me = _slot(x, y, c)

        def land(t, s):
            layer = self.items[t][2]
            return outs[t].at[s] if layer is None else outs[t].at[s, layer]

        def src(t, s):
            return ins[t] if self.items[t][3] else ins[t].at[s]

        local = [pltpu.make_async_copy(src(t, me), land(t, me), loc_sems.at[t]) for t in range(self.nt)]
        sends, arrivals = [], []
        for k in range(1, N_DEV):
            px, py, pc = _flip(x, (k >> 2) & 1), _flip(y, (k >> 1) & 1), _flip(c, k & 1)
            peer = _slot(px, py, pc)
            for t in range(self.nt):
                for dst, into in ((land(t, me), sends), (land(t, peer), arrivals)):
                    into.append(pltpu.make_async_remote_copy(
                        src_ref=src(t, peer), dst_ref=dst, send_sem=send_sems.at[t, k - 1], recv_sem=recv_sems.at[t, k - 1],
                        device_id=(px, py, pc), device_id_type=MESH))
        return local, sends, arrivals

    def start(self, ins, outs, sems):
        local, sends, _ = self._copies(ins, outs, sems)
        for cp in local + sends:
            cp.start()

    def finish(self, ins, outs, sems):
        local, sends, arrivals = self._copies(ins, outs, sems)
        for cp in arrivals:
            cp.wait_recv()
        for cp in sends:
            cp.wait_send()
        for cp in local:
            cp.wait()


def _any_specs(n):
    return [pl.BlockSpec(memory_space=pl.ANY)] * n


def run_plan(plan, *, name):
    def body(*refs):
        n_in, n_out = len(plan.inputs), len(plan.out_shapes)
        ins, outs, sems = refs[:n_in], refs[n_in : n_in + n_out], refs[n_in + n_out :]
        for _, phase in plan.phases(3):
            phase(ins, outs, sems)

    return pl.pallas_call(
        body,
        name=name,
        in_specs=_any_specs(len(plan.inputs)),
        out_specs=_any_specs(len(plan.out_shapes)),
        out_shape=plan.out_shapes,
        scratch_shapes=plan.scratch,
        input_output_aliases=plan.aliases,
    )(*plan.inputs)


def _call(body, *, name, grid, in_specs, out_specs, out_shape, args, scratch=(), plan=None):
    n_in, n_out, n_scr = len(in_specs), len(out_specs), len(scratch)
    if plan is None:
        outs = pl.pallas_call(
            body, name=name, grid=grid, in_specs=in_specs, out_specs=out_specs, out_shape=out_shape,
            scratch_shapes=list(scratch), compiler_params=_params(len(grid)))(*args)
        return outs, None
    c_in, c_out = len(plan.inputs), len(plan.out_shapes)
    phases = plan.phases(math.prod(grid))

    def full(*refs):
        a, refs = refs[:n_in], refs[n_in:]
        ci, refs = refs[:c_in], refs[c_in:]
        o, refs = refs[:n_out], refs[n_out:]
        co, refs = refs[:c_out], refs[c_out:]
        s, cs = refs[:n_scr], refs[n_scr:]
        step = pl.program_id(0)
        for d in range(1, len(grid)):
            step = step * grid[d] + pl.program_id(d)
        for at, phase in phases:
            if at == 0:
                pl.when(step == 0)(functools.partial(phase, ci, co, cs))
        body(*a, *o, *s)
        for at, phase in phases:
            if at > 0:
                pl.when(step == at)(functools.partial(phase, ci, co, cs))

    outs = pl.pallas_call(
        full,
        name=name,
        grid=grid,
        in_specs=list(in_specs) + _any_specs(c_in),
        out_specs=list(out_specs) + _any_specs(c_out),
        out_shape=list(out_shape) + plan.out_shapes,
        scratch_shapes=list(scratch) + plan.scratch,
        input_output_aliases={n_in + i: n_out + t for i, t in plan.aliases.items()},
        compiler_params=_params(len(grid)),
    )(*args, *plan.inputs)
    return outs[:n_out], outs[n_out:]


def _shift_down(u, prev8, row, n):
    out = pltpu.roll(u, n, 0)
    for k in range(n):
        out = jnp.where(row == k, prev8[8 - n + k : 8 - n + k + 1, :], out)
    return out


def _shift_up(u, next8, row, n, tm):
    out = pltpu.roll(u, tm - n, 0)
    for k in range(n):
        out = jnp.where(row == tm - n + k, next8[k : k + 1, :], out)
    return out


def conv_fwd(x, gain, w_in, cw, w_out, *, seq, tm, plan=None):
    T, D = x.shape
    tps = seq // tm

    def body(x_ref, g_ref, win_ref, cw_ref, wout_ref, x1_ref, bcx_ref, carry_ref):
        i = pl.program_id(0)

        @pl.when(i % tps == 0)
        def _():
            carry_ref[...] = jnp.zeros_like(carry_ref)

        xt = x_ref[...]
        h = ((xt * _rms(xt)) * g_ref[...]).astype(BF16)
        bcx = _dot(h, win_ref[...])
        bcx_ref[...] = bcx
        b, c, xv = bcx[:, :D], bcx[:, D : 2 * D], bcx[:, 2 * D :]
        u = b * xv
        row = lax.broadcasted_iota(jnp.int32, u.shape, 0)
        prev = carry_ref[...]
        u1 = _shift_down(u, prev, row, 1)
        u2 = _shift_down(u, prev, row, 2)
        carry_ref[...] = u[tm - 8 :, :]
        cwv = cw_ref[...]
        y = cwv[0:1, :] * u2 + cwv[1:2, :] * u1 + cwv[2:3, :] * u
        z = (c * y).astype(BF16)
        x1_ref[...] = xt + _dot(z, wout_ref[...])

    return _call(
        body,
        plan=plan,
        args=(x, gain, w_in, cw, w_out),
        name="conv_fwd",
        grid=(T // tm,),
        in_specs=[
            pl.BlockSpec((tm, D), lambda i: (i, 0)),
            _resident((1, D)),
            _resident((D, 3 * D)),
            _resident((3, D)),
            _resident((D, D)),
        ],
        out_specs=[pl.BlockSpec((tm, D), lambda i: (i, 0)), pl.BlockSpec((tm, 3 * D), lambda i: (i, 0))],
        out_shape=[jax.ShapeDtypeStruct((T, D), F32), jax.ShapeDtypeStruct((T, 3 * D), F32)],
        scratch=[pltpu.VMEM((8, D), F32)],
    )


def conv_bwd(dx1, x, gain, bcx, cw, w_in, w_out, *, seq, tm, plan=None):
    T, D = x.shape
    n = T // tm
    tps = seq // tm
    r8 = tm // 8

    def body(d_ref, x_ref, g_ref, bcx_ref, halo_ref, cw_ref, win_ref, wout_ref,
             gx_ref, dbcx_ref, h_ref, z_ref, d16_ref, dcw_ref, dg_ref, carry_ref):
        i = pl.program_id(0)
        t = n - 1 - i

        @pl.when(i == 0)
        def _():
            dcw_ref[...] = jnp.zeros_like(dcw_ref)
            dg_ref[...] = jnp.zeros_like(dg_ref)

        @pl.when(t % tps == tps - 1)
        def _():
            carry_ref[...] = jnp.zeros_like(carry_ref)

        d = d_ref[...]
        d16 = d.astype(BF16)
        d16_ref[...] = d16
        dz = _dot_nt(d16, wout_ref[...])
        bcx = bcx_ref[...]
        b, c, xv = bcx[:, :D], bcx[:, D : 2 * D], bcx[:, 2 * D :]
        u = b * xv
        halo = halo_ref[...]
        hu = jnp.where(t % tps == 0, 0.0, halo[:, :D] * halo[:, 2 * D :])
        row = lax.broadcasted_iota(jnp.int32, u.shape, 0)
        u1 = _shift_down(u, hu, row, 1)
        u2 = _shift_down(u, hu, row, 2)
        cwv = cw_ref[...]
        y = cwv[0:1, :] * u2 + cwv[1:2, :] * u1 + cwv[2:3, :] * u
        z_ref[...] = (c * y).astype(BF16)
        dc = dz * y
        dy = dz * c
        dcw_ref[0:1, :] += jnp.sum(dy * u2, axis=0, keepdims=True)
        dcw_ref[1:2, :] += jnp.sum(dy * u1, axis=0, keepdims=True)
        dcw_ref[2:3, :] += jnp.sum(dy * u, axis=0, keepdims=True)
        nxt = carry_ref[...]
        dy1 = _shift_up(dy, nxt, row, 1, tm)
        dy2 = _shift_up(dy, nxt, row, 2, tm)
        carry_ref[...] = dy[0:8, :]
        du = cwv[2:3, :] * dy + cwv[1:2, :] * dy1 + cwv[0:1, :] * dy2
        dbcx_ref[:, :D] = (du * xv).astype(BF16)
        dbcx_ref[:, D : 2 * D] = dc.astype(BF16)
        dbcx_ref[:, 2 * D :] = (du * b).astype(BF16)
        dh = _dot_nt(dbcx_ref[...], win_ref[...])
        xt = x_ref[...]
        r = _rms(xt)
        gn = g_ref[...]
        h_ref[...] = ((xt * r) * gn).astype(BF16)
        dx, dgn = _rms_bwd(xt, r, gn, dh)
        dg_ref[0:1, :] += dgn
        gx_ref[...] = d + dx

    rev = lambda i: (n - 1 - i, 0)
    return _call(
        body,
        plan=plan,
        args=(dx1, x, gain, bcx, bcx, cw, w_in, w_out),
        name="conv_bwd",
        grid=(n,),
        in_specs=[
            pl.BlockSpec((tm, D), rev),
            pl.BlockSpec((tm, D), rev),
            _resident((1, D)),
            pl.BlockSpec((tm, 3 * D), rev),
            pl.BlockSpec((8, 3 * D), lambda i: (jnp.maximum((n - 1 - i) * r8 - 1, 0), 0)),
            _resident((3, D)),
            _resident((D, 3 * D)),
            _resident((D, D)),
        ],
        out_specs=[
            pl.BlockSpec((tm, D), rev),
            pl.BlockSpec((tm, 3 * D), rev),
            pl.BlockSpec((tm, D), rev),
            pl.BlockSpec((tm, D), rev),
            pl.BlockSpec((tm, D), rev),
            pl.BlockSpec((8, D), lambda i: (0, 0)),
            pl.BlockSpec((8, D), lambda i: (0, 0)),
        ],
        out_shape=[
            jax.ShapeDtypeStruct((T, D), F32),
            jax.ShapeDtypeStruct((T, 3 * D), BF16),
            jax.ShapeDtypeStruct((T, D), BF16),
            jax.ShapeDtypeStruct((T, D), BF16),
            jax.ShapeDtypeStruct((T, D), BF16),
            jax.ShapeDtypeStruct((8, D), F32),
            jax.ShapeDtypeStruct((8, D), F32),
        ],
        scratch=[pltpu.VMEM((8, D), F32)],
    )


MXU_TILE = 256
FFN_CHUNK = 4 * MXU_TILE


def _sigmoid(g):
    return 1.0 / (1.0 + jnp.exp(-g))


def _ffn_chunks(F):
    assert F % MXU_TILE == 0
    return [(s, min(FFN_CHUNK, F - s)) for s in range(0, F, FFN_CHUNK)]


def ffn_fwd(x, gain, w_gu, w_d, *, tm, plan=None, attn=None, target=None):
    T, D = x.shape
    F = w_d.shape[0]
    row = lambda i: (i, 0)
    tile = pl.BlockSpec((tm, D), row)

    def body(*refs):
        refs = list(refs)
        x_ref, g_ref, wgu_ref, wd_ref = refs[:4]
        del refs[:4]
        if attn is not None:
            ao_ref, wo_ref = refs[:2]
            del refs[:2]
        if target is not None:
            t_ref = refs.pop(0)
        if attn is not None:
            xin_ref = refs.pop(0)
        xo_ref, gu_ref = refs[:2]
        xt = x_ref[...]
        if attn is not None:
            xt = xt + _dot(ao_ref[...], wo_ref[...])
            xin_ref[...] = xt
        h = ((xt * _rms(xt)) * g_ref[...]).astype(BF16)
        acc = xt
        for s, n in _ffn_chunks(F):
            g = _dot(h, wgu_ref[:, s : s + n])
            u = _dot(h, wgu_ref[:, F + s : F + s + n])
            gu_ref[:, s : s + n] = g
            gu_ref[:, F + s : F + s + n] = u
            a = ((g * _sigmoid(g)) * u).astype(BF16)
            acc = acc + _dot(a, wd_ref[s : s + n, :])
        if target is None:
            xo_ref[...] = acc
        else:
            s_ref = refs[2]

            @pl.when(pl.program_id(0) == 0)
            def _():
                s_ref[...] = jnp.zeros_like(s_ref)

            e = acc - t_ref[...]
            xo_ref[...] = e * (1.0 / D)
            s_ref[...] += jnp.sum(jnp.sum(e * e, axis=-1, keepdims=True), axis=0, keepdims=True)

    args = [x, gain, w_gu, w_d]
    in_specs = [tile, _resident((1, D)), _resident((D, 2 * F)), _resident((F, D))]
    out_specs = [tile, pl.BlockSpec((tm, 2 * F), row)]
    out_shape = [jax.ShapeDtypeStruct((T, D), F32), jax.ShapeDtypeStruct((T, 2 * F), F32)]
    if attn is not None:
        args += list(attn)
        in_specs += [pl.BlockSpec((tm, attn[0].shape[1]), row), _resident(attn[1].shape)]
        out_specs.insert(0, tile)
        out_shape.insert(0, jax.ShapeDtypeStruct((T, D), F32))
    if target is not None:
        args.append(target)
        in_specs.append(tile)
        out_specs.append(pl.BlockSpec((8, LANES), lambda i: (0, 0)))
        out_shape.append(jax.ShapeDtypeStruct((8, LANES), F32))
    return _call(body, plan=plan, args=args, name="ffn_fwd", grid=(T // tm,), in_specs=in_specs, out_specs=out_specs,
                 out_shape=out_shape)


def ffn_bwd(dxo, x, gain, gu, w_gu, w_d, *, tm, plan=None, w_o=None):
    T, D = x.shape
    F = w_d.shape[0]

    def body(d_ref, x_ref, g_ref, gu_ref, wgu_ref, wd_ref, *rest):
        if w_o is not None:
            wo_ref, rest = rest[0], rest[1:]
        dx_ref, a_ref, dgu_ref, h_ref, d16_ref, dg_ref = rest[:6]

        @pl.when(pl.program_id(0) == 0)
        def _():
            dg_ref[...] = jnp.zeros_like(dg_ref)

        d = d_ref[...]
        d16 = d.astype(BF16)
        d16_ref[...] = d16
        dh = jnp.zeros((tm, D), F32)
        for c0, n in _ffn_chunks(F):
            g = gu_ref[:, c0 : c0 + n]
            u = gu_ref[:, F + c0 : F + c0 + n]
            da = _dot_nt(d16, wd_ref[c0 : c0 + n, :])
            s = _sigmoid(g)
            sg = g * s
            a_ref[:, c0 : c0 + n] = (sg * u).astype(BF16)
            dg16 = (da * u * (s + sg * (1.0 - s))).astype(BF16)
            du16 = (da * sg).astype(BF16)
            dgu_ref[:, c0 : c0 + n] = dg16
            dgu_ref[:, F + c0 : F + c0 + n] = du16
            dh = dh + _dot_nt(dg16, wgu_ref[:, c0 : c0 + n]) + _dot_nt(du16, wgu_ref[:, F + c0 : F + c0 + n])
        xt = x_ref[...]
        r = _rms(xt)
        gn = g_ref[...]
        h_ref[...] = ((xt * r) * gn).astype(BF16)
        dx, dgn = _rms_bwd(xt, r, gn, dh)
        dg_ref[0:1, :] += dgn
        dxi = d + dx
        dx_ref[...] = dxi
        if w_o is not None:
            dxi16_ref, dao_ref = rest[6:8]
            dxi16 = dxi.astype(BF16)
            dxi16_ref[...] = dxi16
            dao_ref[...] = _dot_nt(dxi16, wo_ref[...]).astype(BF16)

    tile = pl.BlockSpec((tm, D), lambda i: (i, 0))
    args = [dxo, x, gain, gu, w_gu, w_d]
    wide = lambda n: pl.BlockSpec((tm, n), lambda i: (i, 0))
    in_specs = [tile, tile, _resident((1, D)), wide(2 * F), _resident((D, 2 * F)), _resident((F, D))]
    out_specs = [tile, wide(F), wide(2 * F), tile, tile, pl.BlockSpec((8, D), lambda i: (0, 0))]
    out_shape = [
        jax.ShapeDtypeStruct((T, D), F32),
        jax.ShapeDtypeStruct((T, F), BF16),
        jax.ShapeDtypeStruct((T, 2 * F), BF16),
        jax.ShapeDtypeStruct((T, D), BF16),
        jax.ShapeDtypeStruct((T, D), BF16),
        jax.ShapeDtypeStruct((8, D), F32),
    ]
    if w_o is not None:
        args.append(w_o)
        in_specs.append(_resident(w_o.shape))
        out_specs += [tile, pl.BlockSpec((tm, w_o.shape[0]), lambda i: (i, 0))]
        out_shape += [jax.ShapeDtypeStruct((T, D), BF16), jax.ShapeDtypeStruct((T, w_o.shape[0]), BF16)]
    return _call(body, plan=plan, args=args, name="ffn_bwd", grid=(T // tm,), in_specs=in_specs, out_specs=out_specs,
                 out_shape=out_shape)


def wgrad(a, b, *, name, a_cols=0, b_cols=0, tk, out_dtype=BF16, plan=None):
    T, K = a.shape
    J = 1
    if a_cols:
        K = a_cols
        J = a.shape[1] // K
        a_spec = pl.BlockSpec((tk, K), lambda j, k: (k, j))
    else:
        a_spec = pl.BlockSpec((tk, K), lambda j, k: (k, 0))
    if b_cols:
        N = b_cols
        J = b.shape[1] // N
        b_spec = pl.BlockSpec((tk, N), lambda j, k: (k, j))
    else:
        N = b.shape[1]
        b_spec = pl.BlockSpec((tk, N), lambda j, k: (k, 0))
    nk = T // tk

    def body(a_ref, b_ref, o_ref, acc_ref):
        k = pl.program_id(1)

        @pl.when(k == 0)
        def _():
            acc_ref[...] = jnp.zeros_like(acc_ref)

        acc_ref[...] += _dot_tn(a_ref[...], b_ref[...])

        @pl.when(k == nk - 1)
        def _():
            o_ref[...] = acc_ref[...].astype(out_dtype)

    outs, sent = _call(
        body,
        plan=plan,
        args=(a, b),
        name=name,
        grid=(J, nk),
        in_specs=[a_spec, b_spec],
        out_specs=[pl.BlockSpec((None, K, N), lambda j, k: (j, 0, 0))],
        out_shape=[jax.ShapeDtypeStruct((J, K, N), out_dtype)],
        scratch=[pltpu.VMEM((K, N), F32)],
    )
    return outs[0], sent


def _seg(x, lo):
    s_lo = jnp.sum(jnp.where(lo, x, 0.0), axis=-1, keepdims=True)
    s_hi = jnp.sum(jnp.where(lo, 0.0, x), axis=-1, keepdims=True)
    return jnp.where(lo, s_lo, s_hi)


def _head_norm(x, gain, lo):
    r = lax.rsqrt(_seg(x * x, lo) * (1.0 / HEAD_DIM) + EPS)
    return (x * r) * gain, r


def _head_norm_bwd(x, r, gain, dy, lo):
    xn = x * r
    dxn = dy * gain
    dx = r * (dxn - xn * (_seg(dxn * xn, lo) * (1.0 / HEAD_DIM)))
    return dx, jnp.sum(dy * xn, axis=0, keepdims=True)


def _swap_halves(x):
    return pltpu.roll(x, HEAD_DIM, 1)


def qkv_proj(x, gain, w, qg, kg, *, tm):
    T, D = x.shape
    N = w.shape[1]
    kvw = N_KV_HEADS * HEAD_DIM
    nqt, nkt = D // LANES, kvw // LANES

    def body(x_ref, g_ref, w_ref, qg_ref, kg_ref, qkv_ref, q_ref, kd_ref, vd_ref):
        xt = x_ref[...]
        h = ((xt * _rms(xt)) * g_ref[...]).astype(BF16)
        qkv = _dot(h, w_ref[...])
        qkv_ref[...] = qkv
        lo = lax.broadcasted_iota(jnp.int32, (1, LANES), 1) < HEAD_DIM
        for t in range(nqt):
            qn, _ = _head_norm(qkv[:, t * LANES : (t + 1) * LANES], qg_ref[...], lo)
            q_ref[:, t * LANES : (t + 1) * LANES] = (qn * SCALE).astype(BF16)
        for t in range(nkt):
            kn, _ = _head_norm(qkv[:, D + t * LANES : D + (t + 1) * LANES], kg_ref[...], lo)
            v = qkv[:, D + kvw + t * LANES : D + kvw + (t + 1) * LANES]
            for src, dst in ((kn, kd_ref), (v, vd_ref)):
                sw = _swap_halves(src)
                dst[:, 2 * t * LANES : (2 * t + 1) * LANES] = jnp.where(lo, src, sw).astype(BF16)
                dst[:, (2 * t + 1) * LANES : (2 * t + 2) * LANES] = jnp.where(lo, sw, src).astype(BF16)

    row = lambda i: (i, 0)
    return pl.pallas_call(
        body,
        name="qkv_proj",
        grid=(T // tm,),
        in_specs=[pl.BlockSpec((tm, D), row), _resident((1, D)), _resident((D, N)), _resident((1, LANES)), _resident((1, LANES))],
        out_specs=[pl.BlockSpec((tm, N), row), pl.BlockSpec((tm, D), row), pl.BlockSpec((tm, 2 * kvw), row), pl.BlockSpec((tm, 2 * kvw), row)],
        out_shape=[
            jax.ShapeDtypeStruct((T, N), F32),
            jax.ShapeDtypeStruct((T, D), BF16),
            jax.ShapeDtypeStruct((T, 2 * kvw), BF16),
            jax.ShapeDtypeStruct((T, 2 * kvw), BF16),
        ],
        compiler_params=_params(1),
    )(x, gain, w, qg, kg)


def _attn_tables(sinks, n_q_heads):
    P = n_q_heads // N_KV_HEADS // 2
    h = jnp.arange(1, n_q_heads + 1, dtype=F32)
    slopes = jnp.exp2(-8.0 * h / n_q_heads).reshape(N_KV_HEADS, P, 1, 2, 1)
    qi = jnp.arange(BLOCK)[:, None]
    kj = jnp.arange(BLOCK)[None, :]
    dist = jnp.where(kj <= qi, qi - kj, qi + BLOCK - kj).astype(F32)
    shape = (N_KV_HEADS, P, BLOCK, 2, BLOCK)
    bias = jnp.broadcast_to(-slopes * dist[None, None, :, None, :], shape)
    sink = jnp.broadcast_to(sinks.astype(F32).reshape(N_KV_HEADS, P, 1, 2, 1), shape)
    return bias.reshape(N_KV_HEADS, P * BLOCK, 2 * BLOCK), sink.reshape(N_KV_HEADS, P * BLOCK, 2 * BLOCK)


def _attn_specs(D, nb):
    kvw2 = 2 * N_KV_HEADS * HEAD_DIM
    cur = lambda b, i: (b * nb + i, 0)
    prev = lambda b, i: (jnp.maximum(b * nb + i - 1, 0), 0)
    return [
        pl.BlockSpec((BLOCK, D), cur),
        pl.BlockSpec((BLOCK, kvw2), cur),
        pl.BlockSpec((BLOCK, kvw2), prev),
        pl.BlockSpec((BLOCK, kvw2), cur),
        pl.BlockSpec((BLOCK, kvw2), prev),
    ]


def _attn_operands(kh, P, lo, q_ref, kc_ref, kp_ref, vc_ref, vp_ref):
    sl = slice(kh * LANES, (kh + 1) * LANES)

    def cat(prev_ref, cur_ref):
        d = jnp.concatenate([prev_ref[:, sl], cur_ref[:, sl]], axis=0)
        z = jnp.zeros_like(d)
        return jnp.concatenate([jnp.where(lo, d, z), jnp.where(lo, z, d)], axis=0)

    qt = jnp.concatenate([q_ref[:, (kh * P + pr) * LANES : (kh * P + pr + 1) * LANES] for pr in range(P)], axis=0)
    return qt, cat(kp_ref, kc_ref), cat(vp_ref, vc_ref)


def _attn_exp(s_all, bias, sink, tri, first):
    out = []
    for par in range(2):
        c0 = 2 * par * BLOCK
        s = jnp.where(tri, s_all[:, c0 + BLOCK : c0 + 2 * BLOCK], jnp.where(first, NEG, s_all[:, c0 : c0 + BLOCK]))
        s = s + bias[:, par * BLOCK : (par + 1) * BLOCK]
        snk = sink[:, par * BLOCK : (par + 1) * BLOCK]
        m = jnp.maximum(jnp.max(s, axis=-1, keepdims=True), snk)
        out.append((jnp.exp(s - m), jnp.exp(snk - m)))
    return out


def _unfold(x, tri):
    z = jnp.zeros_like(x)
    return jnp.concatenate([jnp.where(tri, z, x), jnp.where(tri, x, z)], axis=1)


def _attn_masks(R):
    lane = lax.broadcasted_iota(jnp.int32, (1, LANES), 1)
    row = lax.broadcasted_iota(jnp.int32, (R, BLOCK), 0) & (BLOCK - 1)
    col = lax.broadcasted_iota(jnp.int32, (R, BLOCK), 1)
    return lane, lane < HEAD_DIM, col <= row


def attn_fwd(q16, kd, vd, bias, sink, *, seq, n_seq):
    T, D = q16.shape
    nb = seq // BLOCK
    P = D // HEAD_DIM // N_KV_HEADS // 2
    R = P * BLOCK
    KV = range(N_KV_HEADS)

    def body(q_ref, kc_ref, kp_ref, vc_ref, vp_ref, bias_ref, sink_ref, o_ref):
        first = pl.program_id(1) == 0
        _, lo, tri = _attn_masks(R)
        r4 = lax.broadcasted_iota(jnp.int32, (4 * BLOCK, LANES), 0)
        l4 = lax.broadcasted_iota(jnp.int32, (4 * BLOCK, LANES), 1)
        ones = ((r4 < 2 * BLOCK) == (l4 < HEAD_DIM)).astype(BF16)
        ops = [_attn_operands(kh, P, lo, q_ref, kc_ref, kp_ref, vc_ref, vp_ref) for kh in KV]
        s_all = [_dot_nt(ops[kh][0], ops[kh][1]) for kh in KV]
        ex = [_attn_exp(s_all[kh], bias_ref[kh], sink_ref[kh], tri, first) for kh in KV]
        lhs = [jnp.concatenate([_unfold(e, tri) for e, _ in ex[kh]], axis=1).astype(BF16) for kh in KV]
        o = [_dot(lhs[kh], ops[kh][2]) for kh in KV]
        den = [_dot(lhs[kh], ones) for kh in KV]
        for kh in KV:
            out = o[kh] / (den[kh] + jnp.where(lo, ex[kh][0][1], ex[kh][1][1]))
            for pr in range(P):
                t = kh * P + pr
                o_ref[:, t * LANES : (t + 1) * LANES] = out[pr * BLOCK : (pr + 1) * BLOCK, :].astype(BF16)

    return pl.pallas_call(
        body,
        name="attn_fwd",
        grid=(n_seq, nb),
        in_specs=_attn_specs(D, nb) + [_resident((N_KV_HEADS, R, 2 * BLOCK)), _resident((N_KV_HEADS, R, 2 * BLOCK))],
        out_specs=pl.BlockSpec((BLOCK, D), lambda b, i: (b * nb + i, 0)),
        out_shape=jax.ShapeDtypeStruct((T, D), BF16),
        compiler_params=_params(2),
    )(q16, kd, kd, vd, vd, bias, sink)


def attn_bwd(q16, kd, vd, do, bias, sink, *, seq, n_seq):
    T, D = q16.shape
    kvw2 = 2 * N_KV_HEADS * HEAD_DIM
    nb = seq // BLOCK
    G = D // HEAD_DIM // N_KV_HEADS
    P = G // 2
    R = P * BLOCK
    KV = range(N_KV_HEADS)

    def body(q_ref, kc_ref, kp_ref, vc_ref, vp_ref, do_ref, bias_ref, sink_ref,
             dq_ref, dkc_ref, dkp_ref, dvc_ref, dvp_ref, dsink_ref):
        first = pl.program_id(1) == 0

        @pl.when(jnp.logical_and(pl.program_id(0) == 0, first))
        def _():
            dsink_ref[...] = jnp.zeros_like(dsink_ref)

        lane, lo, tri = _attn_masks(R)
        r4 = lax.broadcasted_iota(jnp.int32, (4 * BLOCK, 2 * BLOCK), 0)
        c4 = lax.broadcasted_iota(jnp.int32, (4 * BLOCK, 2 * BLOCK), 1)
        ones = ((r4 < 2 * BLOCK) == (c4 < BLOCK)).astype(BF16)
        ops = [_attn_operands(kh, P, lo, q_ref, kc_ref, kp_ref, vc_ref, vp_ref) for kh in KV]
        do16 = [jnp.concatenate([do_ref[:, (kh * P + pr) * LANES : (kh * P + pr + 1) * LANES] for pr in range(P)], axis=0)
                for kh in KV]
        s_all = [_dot_nt(ops[kh][0], ops[kh][1]) for kh in KV]
        dp_all = [_dot_nt(do16[kh], ops[kh][2]) for kh in KV]
        ex = [_attn_exp(s_all[kh], bias_ref[kh], sink_ref[kh], tri, first) for kh in KV]
        den = [_dot(jnp.concatenate([_unfold(e, tri) for e, _ in ex[kh]], axis=1).astype(BF16), ones) for kh in KV]
        dsink = jnp.zeros((1, LANES), F32)
        pf, dsf = [], []
        for kh in KV:
            ps_, ds_ = [], []
            for par in range(2):
                e, es = ex[kh][par]
                inv = 1.0 / (den[kh][:, par * BLOCK : (par + 1) * BLOCK] + es)
                p = e * inv
                c0 = 2 * par * BLOCK
                dp = jnp.where(tri, dp_all[kh][:, c0 + BLOCK : c0 + 2 * BLOCK], dp_all[kh][:, c0 : c0 + BLOCK])
                delta = jnp.sum(p * dp, axis=-1, keepdims=True)
                ds_.append(_unfold(p * (dp - delta), tri))
                ps_.append(_unfold(p, tri))
                dsr = -((es * inv) * delta)
                for pr in range(P):
                    hq = kh * G + 2 * pr + par
                    tot = jnp.sum(dsr[pr * BLOCK : (pr + 1) * BLOCK, :], axis=0, keepdims=True)
                    dsink = dsink + jnp.where(lane == hq, tot, 0.0)
            pf.append(jnp.concatenate(ps_, axis=1).astype(BF16))
            dsf.append(jnp.concatenate(ds_, axis=1).astype(BF16))
        dq = [_dot(dsf[kh], ops[kh][1]) for kh in KV]
        dk = [_dot_tn(dsf[kh], ops[kh][0]) for kh in KV]
        dv = [_dot_tn(pf[kh], do16[kh]) for kh in KV]
        dsink_ref[0:1, :] += dsink
        for kh in KV:
            sl = slice(kh * LANES, (kh + 1) * LANES)
            for pr in range(P):
                t = kh * P + pr
                dq_ref[:, t * LANES : (t + 1) * LANES] = dq[kh][pr * BLOCK : (pr + 1) * BLOCK, :]
            for full, prev_ref, cur_ref in ((dk[kh], dkp_ref, dkc_ref), (dv[kh], dvp_ref, dvc_ref)):
                dup = jnp.where(lo, full[: 2 * BLOCK, :], full[2 * BLOCK :, :])
                prev_ref[:, sl] = dup[:BLOCK, :]
                cur_ref[:, sl] = dup[BLOCK:, :]

    cur = lambda b, i: (b * nb + i, 0)
    kv_spec = pl.BlockSpec((BLOCK, kvw2), cur)
    kv_shape = jax.ShapeDtypeStruct((T, kvw2), F32)
    return pl.pallas_call(
        body,
        name="attn_bwd",
        grid=(n_seq, nb),
        in_specs=_attn_specs(D, nb)
        + [pl.BlockSpec((BLOCK, D), cur), _resident((N_KV_HEADS, R, 2 * BLOCK)), _resident((N_KV_HEADS, R, 2 * BLOCK))],
        out_specs=[pl.BlockSpec((BLOCK, D), cur), kv_spec, kv_spec, kv_spec, kv_spec, pl.BlockSpec((8, LANES), lambda b, i: (0, 0))],
        out_shape=[jax.ShapeDtypeStruct((T, D), F32), kv_shape, kv_shape, kv_shape, kv_shape, jax.ShapeDtypeStruct((8, LANES), F32)],
        compiler_params=_params(2),
    )(q16, kd, kd, vd, vd, do, bias, sink)


def qkv_bwd(dq, dkc, dkp, dvc, dvp, qkv, dres, x, gain, w_qkv, qg, kg, *, seq):
    T, D = x.shape
    kvw2 = dkc.shape[1]
    kvw = kvw2 // 2
    nqt, nkt = D // LANES, kvw // LANES
    nb = seq // BLOCK
    tm = 2 * BLOCK
    n = T // tm

    def body(dq_ref, dkc_ref, dkpa_ref, dkpb_ref, dvc_ref, dvpa_ref, dvpb_ref, qkv_ref, dres_ref, x_ref, g_ref, w_ref,
             qg_ref, kg_ref, dx_ref, dqkv_ref, h_ref, dg_ref, hg_ref):
        i = pl.program_id(0)

        @pl.when(i == 0)
        def _():
            dg_ref[...] = jnp.zeros_like(dg_ref)
            hg_ref[...] = jnp.zeros_like(hg_ref)

        lo = lax.broadcasted_iota(jnp.int32, (1, LANES), 1) < HEAD_DIM
        last = (2 * i + 1) % nb == nb - 1
        dkd = dkc_ref[...] + jnp.concatenate([dkpa_ref[...], jnp.where(last, 0.0, dkpb_ref[...])], axis=0)
        dvd = dvc_ref[...] + jnp.concatenate([dvpa_ref[...], jnp.where(last, 0.0, dvpb_ref[...])], axis=0)

        def undup(d, t):
            a, b = d[:, 2 * t * LANES : (2 * t + 1) * LANES], d[:, (2 * t + 1) * LANES : (2 * t + 2) * LANES]
            return jnp.where(lo, a + _swap_halves(a), b + _swap_halves(b))

        gq = jnp.zeros((1, LANES), F32)
        for t in range(nqt):
            sl = slice(t * LANES, (t + 1) * LANES)
            q = qkv_ref[:, sl]
            _, r = _head_norm(q, qg_ref[...], lo)
            dxq, dgn = _head_norm_bwd(q, r, qg_ref[...], dq_ref[:, sl] * SCALE, lo)
            dqkv_ref[:, sl] = dxq.astype(BF16)
            gq = gq + dgn
        gk = jnp.zeros((1, LANES), F32)
        for t in range(nkt):
            sl = slice(D + t * LANES, D + (t + 1) * LANES)
            k = qkv_ref[:, sl]
            _, r = _head_norm(k, kg_ref[...], lo)
            dxk, dgn = _head_norm_bwd(k, r, kg_ref[...], undup(dkd, t), lo)
            dqkv_ref[:, sl] = dxk.astype(BF16)
            gk = gk + dgn
            dqkv_ref[:, D + kvw + t * LANES : D + kvw + (t + 1) * LANES] = undup(dvd, t).astype(BF16)
        hg_ref[0:1, :] += gq
        hg_ref[1:2, :] += gk
        dh = _dot_nt(dqkv_ref[...], w_ref[...])
        xt = x_ref[...]
        r = _rms(xt)
        gn = g_ref[...]
        h_ref[...] = ((xt * r) * gn).astype(BF16)
        dx, dgn = _rms_bwd(xt, r, gn, dh)
        dg_ref[0:1, :] += dgn
        dx_ref[...] = dres_ref[...] + dx

    row = lambda i: (i, 0)
    nxt_a = pl.BlockSpec((BLOCK, kvw2), lambda i: (2 * i + 1, 0))
    nxt_b = pl.BlockSpec((BLOCK, kvw2), lambda i: (jnp.minimum(2 * i + 2, 2 * n - 1), 0))
    return pl.pallas_call(
        body,
        name="qkv_bwd",
        grid=(n,),
        in_specs=[
            pl.BlockSpec((tm, D), row),
            pl.BlockSpec((tm, kvw2), row),
            nxt_a,
            nxt_b,
            pl.BlockSpec((tm, kvw2), row),
            nxt_a,
            nxt_b,
            pl.BlockSpec((tm, D + kvw2), row),
            pl.BlockSpec((tm, D), row),
            pl.BlockSpec((tm, D), row),
            _resident((1, D)),
            _resident((D, D + kvw2)),
            _resident((1, LANES)),
            _resident((1, LANES)),
        ],
        out_specs=[
            pl.BlockSpec((tm, D), row),
            pl.BlockSpec((tm, D + kvw2), row),
            pl.BlockSpec((tm, D), row),
            pl.BlockSpec((8, D), lambda i: (0, 0)),
            pl.BlockSpec((8, LANES), lambda i: (0, 0)),
        ],
        out_shape=[
            jax.ShapeDtypeStruct((T, D), F32),
            jax.ShapeDtypeStruct((T, D + kvw2), BF16),
            jax.ShapeDtypeStruct((T, D), BF16),
            jax.ShapeDtypeStruct((8, D), F32),
            jax.ShapeDtypeStruct((8, LANES), F32),
        ],
        compiler_params=_params(1),
    )(dq, dkc, dkp, dkp, dvc, dvp, dvp, qkv, dres, x, gain, w_qkv, qg, kg)


def local_step(x, target, gains, w, *, seq, tm=256, tm_ffn=256, tk=2048, shards=None):
    T, D = x.shape
    n_seq = T // seq
    nm, nf, qgain, kgain, sinks = gains
    H = D // HEAD_DIM
    tk, tk_long = min(tk, T), min(2 * tk, T)
    qg2, kg2 = jnp.tile(qgain, (1, 2)), jnp.tile(kgain, (1, 2))
    bias, sinkcol = _attn_tables(sinks, H)

    dist = shards is not None
    w = dict(w)
    natural = lambda a: a.transpose(1, 0, 2).reshape(a.shape[1], -1)
    by_cols = lambda a: a.reshape(a.shape[0], N_DEV, -1).transpose(1, 0, 2)

    plan = _Gather([shards["w_gu"][0], shards["w_d"][0]]) if dist else None
    (x1, bcx), got = conv_fwd(x, nm[0:1], w["w_in"], w["cw"], w["w_out"], seq=seq, tm=tm, plan=plan)
    if dist:
        w["w_gu"], w["w_d"] = [natural(got[0]), None], [got[1].reshape(-1, D), None]
    plan = _Gather([shards["w_qkv"], shards["w_o"], shards["w_gu"][1], shards["w_d"][1]]) if dist else None
    (x2, gu0), got = ffn_fwd(x1, nf[0:1], w["w_gu"][0], w["w_d"][0], tm=tm_ffn, plan=plan)
    if dist:
        w["w_qkv"], w["w_o"] = natural(got[0]), got[1].reshape(D, D)
        w["w_gu"][1], w["w_d"][1] = natural(got[2]), got[3].reshape(-1, D)
    qkv, q16, kd, vd = qkv_proj(x2, nm[1:2], w["w_qkv"], qg2, kg2, tm=tm)
    ao = attn_fwd(q16, kd, vd, bias, sinkcol, seq=seq, n_seq=n_seq)
    (x3, dx4, gu1, sse), _ = ffn_fwd(x2, nf[1:2], w["w_gu"][1], w["w_d"][1], tm=tm_ffn, attn=(ao, w["w_o"]), target=target)

    by_dest = lambda a: a.reshape(N_DEV, -1, a.shape[-1])
    fresh = lambda a, *lead: jax.ShapeDtypeStruct((N_DEV,) + lead + a.shape[1:], a.dtype)
    gu_cols = 2 * MXU_TILE

    (dx3, a16, dgu, h16, d16, dnf1, dx3_16, dao), _ = ffn_bwd(
        dx4, x3, nf[1:2], gu1, w["w_gu"][1], w["w_d"][1], tm=tm, w_o=w["w_o"])
    g_gu1 = by_cols(natural(wgrad(h16, dgu, name="wgrad_gu1", b_cols=gu_cols, tk=tk_long)[0]))
    g_d1 = by_dest(wgrad(a16, d16, name="wgrad_d1", a_cols=a16.shape[1] // 2, tk=tk)[0])
    g_o = by_dest(wgrad(ao, dx3_16, name="wgrad_o", tk=tk)[0])
    dq, dkc, dkp, dvc, dvp, dsinks = attn_bwd(q16, kd, vd, dao, bias, sinkcol, seq=seq, n_seq=n_seq)
    dx2, dqkv16, h16, dnm1, dgains = qkv_bwd(dq, dkc, dkp, dvc, dvp, qkv, dx3, x2, nm[1:2], w["w_qkv"], qg2, kg2, seq=seq)
    nq = dqkv16.shape[1]
    g_qkv = wgrad(h16, dqkv16, name="wgrad_qkv", tk=tk)[0].reshape(D, N_DEV, nq // N_DEV).transpose(1, 0, 2)

    plan = _Scatter([(g_gu1, fresh(g_gu1, 2), 1, False), (g_d1, fresh(g_d1, 2), 1, False)]) if dist else None
    (dx1, a16, dgu, h16, d16, dnf0), land_ffn = ffn_bwd(dx2, x1, nf[0:1], gu0, w["w_gu"][0], w["w_d"][0], tm=tm, plan=plan)
    plan = _Scatter([(g_o, fresh(g_o), None, False), (g_qkv, fresh(g_qkv), None, False)]) if dist else None
    g_gu0, land_attn = wgrad(h16, dgu, name="wgrad_gu0", b_cols=gu_cols, tk=tk_long, plan=plan)
    g_gu0 = by_cols(natural(g_gu0))
    g_d0 = by_dest(wgrad(a16, d16, name="wgrad_d0", a_cols=a16.shape[1] // 2, tk=tk)[0])
    plan = _Scatter([(g_gu0, land_ffn[0], 0, False)]) if dist else None
    (gx, dbcx, h16, z16, d16, dcw, dnm0), land_gu = conv_bwd(
        dx1, x, nm[0:1], bcx, w["cw"], w["w_in"], w["w_out"], seq=seq, tm=tm, plan=plan)
    g_out = by_dest(wgrad(z16, d16, name="wgrad_out", tk=tk)[0])
    g_cw = dcw[0:3].reshape(3, N_DEV, D // N_DEV).transpose(1, 0, 2)
    plan = _Scatter([(g_d0, land_ffn[1], 0, False), (g_out, fresh(g_out), None, False), (g_cw, fresh(g_cw), None, False)]) if dist else None
    g_in, land_last = wgrad(h16, dbcx, name="wgrad_in", b_cols=3 * D // N_DEV, tk=tk_long, plan=plan)
    g = dict(w_in=g_in, cw=g_cw, w_out=g_out)
    if dist:
        g.update(w_gu=land_gu[0], w_d=land_last[0], w_out=land_last[1], cw=land_last[2], w_o=land_attn[0], w_qkv=land_attn[1])
    else:
        g.update(w_gu0=g_gu0, w_gu1=g_gu1, w_d0=g_d0, w_d1=g_d1, w_o=g_o, w_qkv=g_qkv)
    small = dict(nm0=dnm0, nm1=dnm1, nf0=dnf0, nf1=dnf1, gains=dgains, sinks=dsinks)
    return sse, gx, g, small


def _adamw_math(g, w, m, v):
    m = ADAM_B1 * m + (1.0 - ADAM_B1) * g
    v = ADAM_B2 * v + (1.0 - ADAM_B2) * (g * g)
    m_hat = m / (1.0 - ADAM_B1 ** ADAM_STEP)
    v_hat = v / (1.0 - ADAM_B2 ** ADAM_STEP)
    delta = -ADAM_LR * (m_hat / (jnp.sqrt(v_hat) + ADAM_EPS) + ADAM_WD * w)
    return delta, m, v


def adamw(parts, w, m, v, *, name):
    n, R, C = parts.shape
    tr = R
    for cand in (256, 128, 88, 64, 32, 16, 8):
        if R > cand and R % cand == 0:
            tr = cand
            break

    def body(p_ref, w_ref, m_ref, v_ref, g_ref, d_ref, mo_ref, vo_ref):
        g = p_ref[0].astype(F32)
        for s in range(1, n):
            g = g + p_ref[s].astype(F32)
        g_ref[...] = g
        d_ref[...], mo_ref[...], vo_ref[...] = _adamw_math(g, w_ref[...], m_ref[...], v_ref[...])

    blk = pl.BlockSpec((tr, C), lambda i: (i, 0))
    return pl.pallas_call(
        body,
        name=name,
        grid=(R // tr,),
        in_specs=[pl.BlockSpec((n, tr, C), lambda i: (0, i, 0)), blk, blk, blk],
        out_specs=[blk] * 4,
        out_shape=[jax.ShapeDtypeStruct((R, C), F32)] * 4,
        compiler_params=_params(1),
    )(parts, w, m, v)


def pack_small(small, sse, D):
    W = max(D, 2 * LANES)

    def body(nm0, nm1, nf0, nf1, gains, sinks, sse_ref, o_ref):
        o_ref[...] = jnp.zeros_like(o_ref)
        o_ref[0:1, :D] = nm0[0:1, :]
        o_ref[1:2, :D] = nm1[0:1, :]
        o_ref[2:3, :D] = nf0[0:1, :]
        o_ref[3:4, :D] = nf1[0:1, :]
        gq = gains[0:1, :] + pltpu.roll(gains[0:1, :], HEAD_DIM, 1)
        gk = gains[1:2, :] + pltpu.roll(gains[1:2, :], HEAD_DIM, 1)
        lane = lax.broadcasted_iota(jnp.int32, (1, LANES), 1)
        o_ref[4:5, :LANES] = jnp.where(lane < HEAD_DIM, gq, gk)
        o_ref[4:5, LANES : 2 * LANES] = sinks[0:1, :]
        o_ref[5:6, :LANES] = sse_ref[0:1, :] * (0.5 / D)

    return pl.pallas_call(
        body,
        name="pack_small",
        out_shape=jax.ShapeDtypeStruct((8, W), F32),
    )(small["nm0"], small["nm1"], small["nf0"], small["nf1"], small["gains"], small["sinks"], sse)


def _pack_small_params(nm, nf, qg, kg, sk, D):
    W = max(D, 2 * LANES)
    row4 = jnp.concatenate([qg.reshape(-1), kg.reshape(-1), jnp.zeros((LANES - 2 * HEAD_DIM,), F32), sk.reshape(-1)])
    row4 = jnp.pad(row4, (0, W - row4.shape[0]))
    rows = [jnp.pad(r, (0, W - D)) for r in (nm[0], nm[1], nf[0], nf[1])] + [row4]
    return jnp.concatenate([jnp.stack(rows), jnp.zeros((3, W), F32)], axis=0)


def _unpack_small(a, D, H):
    nm = a[0:2, :D]
    nf = a[2:4, :D]
    qg = a[4:5, 0:HEAD_DIM]
    kg = a[4:5, HEAD_DIM : 2 * HEAD_DIM]
    sk = a[4:5, LANES : LANES + H]
    return qg, kg, sk, nm, nf


def kernel(x, conv_w_in, conv_w, conv_w_out, attn_w_qkv, attn_q_gain, attn_k_gain, attn_sinks, attn_w_o, norm_mixer, norm_ffn, ffn_w_gate_up, ffn_w_down, loss_target, m_conv_w_in, m_conv_w, m_conv_w_out, m_attn_w_qkv, m_attn_q_gain, m_attn_k_gain, m_attn_sinks, m_attn_w_o, m_norm_mixer, m_norm_ffn, m_ffn_w_gate_up, m_ffn_w_down, v_conv_w_in, v_conv_w, v_conv_w_out, v_attn_w_qkv, v_attn_q_gain, v_attn_k_gain, v_attn_sinks, v_attn_w_o, v_norm_mixer, v_norm_ffn, v_ffn_w_gate_up, v_ffn_w_down):
    n_seq, seq, D = x.shape
    T = n_seq * seq
    H = D // HEAD_DIM
    L = ffn_w_gate_up.shape[0]

    full = run_plan(_Gather([conv_w_in[0].astype(BF16), conv_w[0], conv_w_out[0].astype(BF16)]), name="gather_conv_weights")
    w = dict(w_in=full[0].transpose(1, 0, 2).reshape(D, 3 * D), cw=full[1].transpose(1, 0, 2).reshape(3, D),
             w_out=full[2].reshape(D, D))
    shards = dict(w_gu=[ffn_w_gate_up[l].astype(BF16) for l in range(L)], w_d=[ffn_w_down[l].astype(BF16) for l in range(L)],
                  w_qkv=attn_w_qkv[0].astype(BF16), w_o=attn_w_o[0].astype(BF16))
    gains = (norm_mixer, norm_ffn, attn_q_gain, attn_k_gain, attn_sinks)
    sse, gx, g, small = local_step(x.reshape(T, D), loss_target.reshape(T, D), gains, w, seq=seq, shards=shards)

    packed = pack_small(small, sse, D)
    items = [(g["w_in"], jax.ShapeDtypeStruct(g["w_in"].shape, BF16), None, False),
             (packed, jax.ShapeDtypeStruct((N_DEV,) + packed.shape, packed.dtype), None, True)]
    land_in, small_all = run_plan(_Scatter(items), name="exchange_last_grads")
    bufs = [land_in, g["cw"], g["w_out"], g["w_qkv"], g["w_o"], g["w_gu"], g["w_d"]]

    def flat(a):
        return a.reshape(-1, a.shape[-1])

    big = [conv_w_in, conv_w, conv_w_out, attn_w_qkv, attn_w_o, ffn_w_gate_up, ffn_w_down]
    big_m = [m_conv_w_in, m_conv_w, m_conv_w_out, m_attn_w_qkv, m_attn_w_o, m_ffn_w_gate_up, m_ffn_w_down]
    big_v = [v_conv_w_in, v_conv_w, v_conv_w_out, v_attn_w_qkv, v_attn_w_o, v_ffn_w_gate_up, v_ffn_w_down]
    tags = ["in", "cw", "out", "qkv", "o", "gu", "d"]
    res = []
    for b in range(7):
        parts = bufs[b].reshape(N_DEV, -1, bufs[b].shape[-1])
        outs = adamw(parts, flat(big[b]), flat(big_m[b]), flat(big_v[b]), name="adamw_" + tags[b])
        res.append([o.reshape(big[b].shape) for o in outs])
    sw = _pack_small_params(norm_mixer, norm_ffn, attn_q_gain, attn_k_gain, attn_sinks, D)
    sm = _pack_small_params(m_norm_mixer, m_norm_ffn, m_attn_q_gain, m_attn_k_gain, m_attn_sinks, D)
    sv = _pack_small_params(v_norm_mixer, v_norm_ffn, v_attn_q_gain, v_attn_k_gain, v_attn_sinks, D)
    souts = adamw(small_all, sw, sm, sv, name="adamw_small")
    sres = [_unpack_small(o, D, H) for o in souts]
    loss = souts[0][5, 0]

    def ordered(i):
        r, s = [r[i] for r in res], sres[i]
        return [r[0], r[1], r[2], r[3], s[0], s[1], s[2], r[4], s[3], s[4], r[5], r[6]]

    return (loss, gx.reshape(n_seq, seq, D), *ordered(0), *ordered(1), *ordered(2), *ordered(3))
```

```python
import functools
import math

import jax
import jax.numpy as jnp
from jax import lax
from jax.experimental import pallas as pl
from jax.experimental.pallas import tpu as pltpu

F32 = jnp.float32
BF16 = jnp.bfloat16

EPS = 1e-6
HEAD_DIM = 64
N_KV_HEADS = 4
BLOCK = 128
LANES = 128
N_DEV = 8
NEG = -1e30
SCALE = 1.0 / math.sqrt(HEAD_DIM)

ADAM_LR = 0.001
ADAM_B1 = 0.9
ADAM_B2 = 0.999
ADAM_EPS = 1e-08
ADAM_WD = 0.01
ADAM_STEP = 10

V7X_VMEM_BYTES = 64 * 1024 * 1024
VMEM_LIMIT = V7X_VMEM_BYTES - 8 * 1024 * 1024
MESH = pl.DeviceIdType.MESH

_NT = (((1,), (1,)), ((), ()))
_TN = (((0,), (0,)), ((), ()))


def _params(n_grid):
    return pltpu.CompilerParams(dimension_semantics=("arbitrary",) * n_grid, vmem_limit_bytes=VMEM_LIMIT)


def _resident(shape):
    nd = len(shape)
    return pl.BlockSpec(shape, lambda *_: (0,) * nd, pipeline_mode=pl.Buffered(1))


def _rms(x):
    return lax.rsqrt(jnp.mean(x * x, axis=-1, keepdims=True) + EPS)


def _rms_bwd(x, r, gain, dh):
    xn = x * r
    dxn = dh * gain
    dx = r * (dxn - xn * jnp.mean(dxn * xn, axis=-1, keepdims=True))
    return dx, jnp.sum(dh * xn, axis=0, keepdims=True)


def _dot(a, b):
    return jnp.dot(a, b, preferred_element_type=F32)


def _dot_nt(a, b):
    return lax.dot_general(a, b, _NT, preferred_element_type=F32)


def _dot_tn(a, b):
    return lax.dot_general(a, b, _TN, preferred_element_type=F32)


def _place():
    return lax.axis_index("x"), lax.axis_index("y"), lax.axis_index("c")


def _flip(v, bit):
    return 1 - v if bit else v


def _slot(px, py, pc):
    return 4 * px + 2 * py + pc


class _Gather:
    def __init__(self, shards):
        nt = len(shards)
        self.nt = nt
        self.inputs = list(shards)
        self.out_shapes = [jax.ShapeDtypeStruct((N_DEV,) + s.shape, s.dtype) for s in shards]
        self.scratch = [pltpu.SemaphoreType.DMA((nt, 7)), pltpu.SemaphoreType.DMA((nt, 7)), pltpu.SemaphoreType.DMA((nt,))]
        self.aliases = {}

    def phases(self, total):
        assert total >= 3
        return [(0, self.start), (total - 2, self.forward), (total - 1, self.finish)]

    def _copies(self, ins, outs, sems):
        send_sems, recv_sems, loc_sems = sems
        x, y, c = _place()
        me = _slot(x, y, c)
        sib = (x, y, 1 - c)
        chips = [(_flip(x, k >> 1), _flip(y, k & 1)) for k in (1, 2, 3)]

        def copy(t, k, src, dst_slot, to):
            return pltpu.make_async_remote_copy(
                src_ref=src, dst_ref=outs[t].at[dst_slot], send_sem=send_sems.at[t, k], recv_sem=recv_sems.at[t, k],
                device_id=to, device_id_type=MESH)

        local = [pltpu.make_async_copy(ins[t], outs[t].at[me], loc_sems.at[t]) for t in range(self.nt)]
        first, passed, arrive_ici, arrive_sib = [], [], [], []
        for t in range(self.nt):
            first.append(copy(t, 0, ins[t], me, sib))
            s = _slot(x, y, 1 - c)
            arrive_sib.append(copy(t, 0, outs[t].at[s], s, sib))
            for j, (px, py) in enumerate(chips):
                first.append(copy(t, 1 + j, ins[t], me, (px, py, c)))
                s = _slot(px, py, c)
                arrive_ici.append(copy(t, 1 + j, outs[t].at[s], s, sib))
                passed.append(copy(t, 4 + j, outs[t].at[s], s, sib))
                s = _slot(px, py, 1 - c)
                arrive_sib.append(copy(t, 4 + j, outs[t].at[s], s, sib))
        return local, first, passed, arrive_ici, arrive_sib

    def start(self, ins, outs, sems):
        local, first, _, _, _ = self._copies(ins, outs, sems)
        for cp in local + first:
            cp.start()

    def forward(self, ins, outs, sems):
        _, _, passed, arrive_ici, _ = self._copies(ins, outs, sems)
        for arrival, fwd in zip(arrive_ici, passed):
            arrival.wait_recv()
            fwd.start()

    def finish(self, ins, outs, sems):
        local, first, passed, _, arrive_sib = self._copies(ins, outs, sems)
        for cp in arrive_sib:
            cp.wait_recv()
        for cp in first + passed:
            cp.wait_send()
        for cp in local:
            cp.wait()


class _Scatter:
    def __init__(self, items):
        self.items = items
        nt = len(items)
        self.nt = nt
        reused = [(t, it[1]) for t, it in enumerate(items) if not isinstance(it[1], jax.ShapeDtypeStruct)]
        self.inputs = [it[0] for it in items] + [land for _, land in reused]
        self.out_shapes = [jax.ShapeDtypeStruct(it[1].shape, it[1].dtype) for it in items]
        self.aliases = {nt + i: t for i, (t, _) in enumerate(reused)}
        self.scratch = [pltpu.SemaphoreType.DMA((nt, 7)), pltpu.SemaphoreType.DMA((nt, 7)), pltpu.SemaphoreType.DMA((nt,))]

    def phases(self, total):
        assert total >= 2
        return [(0, self.start), (total - 1, self.finish)]

    def _copies(self, ins, outs, sems):
        send_sems, recv_sems, loc_sems = sems
        x, y, c = _place()
        me = _slot(x, y, c)

        def land(t, s):
            layer = self.items[t][2]
            return outs[t].at[s] if layer is None else outs[t].at[s, layer]

        def src(t, s):
            return ins[t] if self.items[t][3] else ins[t].at[s]

        local = [pltpu.make_async_copy(src(t, me), land(t, me), loc_sems.at[t]) for t in range(self.nt)]
        sends, arrivals = [], []
        for k in range(1, N_DEV):
            px, py, pc = _flip(x, (k >> 2) & 1), _flip(y, (k >> 1) & 1), _flip(c, k & 1)
            peer = _slot(px, py, pc)
            for t in range(self.nt):
                for dst, into in ((land(t, me), sends), (land(t, peer), arrivals)):
                    into.append(pltpu.make_async_remote_copy(
                        src_ref=src(t, peer), dst_ref=dst, send_sem=send_sems.at[t, k - 1], recv_sem=recv_sems.at[t, k - 1],
                        device_id=(px, py, pc), device_id_type=MESH))
        return local, sends, arrivals

    def start(self, ins, outs, sems):
        local, sends, _ = self._copies(ins, outs, sems)
        for cp in local + sends:
            cp.start()

    def finish(self, ins, outs, sems):
        local, sends, arrivals = self._copies(ins, outs, sems)
        for cp in arrivals:
            cp.wait_recv()
        for cp in sends:
            cp.wait_send()
        for cp in local:
            cp.wait()


def _any_specs(n):
    return [pl.BlockSpec(memory_space=pl.ANY)] * n


def run_plan(plan, *, name):
    def body(*refs):
        n_in, n_out = len(plan.inputs), len(plan.out_shapes)
        ins, outs, sems = refs[:n_in], refs[n_in : n_in + n_out], refs[n_in + n_out :]
        for _, phase in plan.phases(3):
            phase(ins, outs, sems)

    return pl.pallas_call(
        body,
        name=name,
        in_specs=_any_specs(len(plan.inputs)),
        out_specs=_any_specs(len(plan.out_shapes)),
        out_shape=plan.out_shapes,
        scratch_shapes=plan.scratch,
        input_output_aliases=plan.aliases,
    )(*plan.inputs)


def _call(body, *, name, grid, in_specs, out_specs, out_shape, args, scratch=(), plan=None):
    n_in, n_out, n_scr = len(in_specs), len(out_specs), len(scratch)
    if plan is None:
        outs = pl.pallas_call(
            body, name=name, grid=grid, in_specs=in_specs, out_specs=out_specs, out_shape=out_shape,
            scratch_shapes=list(scratch), compiler_params=_params(len(grid)))(*args)
        return outs, None
    c_in, c_out = len(plan.inputs), len(plan.out_shapes)
    phases = plan.phases(math.prod(grid))

    def full(*refs):
        a, refs = refs[:n_in], refs[n_in:]
        ci, refs = refs[:c_in], refs[c_in:]
        o, refs = refs[:n_out], refs[n_out:]
        co, refs = refs[:c_out], refs[c_out:]
        s, cs = refs[:n_scr], refs[n_scr:]
        step = pl.program_id(0)
        for d in range(1, len(grid)):
            step = step * grid[d] + pl.program_id(d)
        for at, phase in phases:
            if at == 0:
                pl.when(step == 0)(functools.partial(phase, ci, co, cs))
        body(*a, *o, *s)
        for at, phase in phases:
            if at > 0:
                pl.when(step == at)(functools.partial(phase, ci, co, cs))

    outs = pl.pallas_call(
        full,
        name=name,
        grid=grid,
        in_specs=list(in_specs) + _any_specs(c_in),
        out_specs=list(out_specs) + _any_specs(c_out),
        out_shape=list(out_shape) + plan.out_shapes,
        scratch_shapes=list(scratch) + plan.scratch,
        input_output_aliases={n_in + i: n_out + t for i, t in plan.aliases.items()},
        compiler_params=_params(len(grid)),
    )(*args, *plan.inputs)
    return outs[:n_out], outs[n_out:]


def _row_tile(R):
    return 256 if R % 256 == 0 else R


def cols_from_shards(a):
    n, R, C = a.shape
    tr = _row_tile(R)

    def body(i_ref, o_ref):
        for s in range(n):
            o_ref[:, s * C : (s + 1) * C] = i_ref[s]

    return pl.pallas_call(
        body,
        name="cols_from_shards",
        grid=(R // tr,),
        in_specs=[pl.BlockSpec((n, tr, C), lambda i: (0, i, 0))],
        out_specs=pl.BlockSpec((tr, n * C), lambda i: (i, 0)),
        out_shape=jax.ShapeDtypeStruct((R, n * C), a.dtype),
        compiler_params=_params(1),
    )(a)


def shards_from_cols(a):
    R, W = a.shape
    C = W // N_DEV
    tr = _row_tile(R)

    def body(i_ref, o_ref):
        for s in range(N_DEV):
            o_ref[s] = i_ref[:, s * C : (s + 1) * C]

    return pl.pallas_call(
        body,
        name="shards_from_cols",
        grid=(R // tr,),
        in_specs=[pl.BlockSpec((tr, W), lambda i: (i, 0))],
        out_specs=pl.BlockSpec((N_DEV, tr, C), lambda i: (0, i, 0)),
        out_shape=jax.ShapeDtypeStruct((N_DEV, R, C), a.dtype),
        compiler_params=_params(1),
    )(a)


def _shift_down(u, prev8, row, n):
    out = pltpu.roll(u, n, 0)
    for k in range(n):
        out = jnp.where(row == k, prev8[8 - n + k : 8 - n + k + 1, :], out)
    return out


def _shift_up(u, next8, row, n, tm):
    out = pltpu.roll(u, tm - n, 0)
    for k in range(n):
        out = jnp.where(row == tm - n + k, next8[k : k + 1, :], out)
    return out


def conv_fwd(x, gain, w_in, cw, w_out, *, seq, tm, plan=None):
    T, D = x.shape
    tps = seq // tm

    def body(x_ref, g_ref, win_ref, cw_ref, wout_ref, x1_ref, bcx_ref, carry_ref):
        i = pl.program_id(0)

        @pl.when(i % tps == 0)
        def _():
            carry_ref[...] = jnp.zeros_like(carry_ref)

        xt = x_ref[...]
        h = ((xt * _rms(xt)) * g_ref[...]).astype(BF16)
        bcx = _dot(h, win_ref[...])
        bcx_ref[...] = bcx
        b, c, xv = bcx[:, :D], bcx[:, D : 2 * D], bcx[:, 2 * D :]
        u = b * xv
        row = lax.broadcasted_iota(jnp.int32, u.shape, 0)
        prev = carry_ref[...]
        u1 = _shift_down(u, prev, row, 1)
        u2 = _shift_down(u, prev, row, 2)
        carry_ref[...] = u[tm - 8 :, :]
        cwv = cw_ref[...]
        y = cwv[0:1, :] * u2 + cwv[1:2, :] * u1 + cwv[2:3, :] * u
        z = (c * y).astype(BF16)
        x1_ref[...] = xt + _dot(z, wout_ref[...])

    return _call(
        body,
        plan=plan,
        args=(x, gain, w_in, cw, w_out),
        name="conv_fwd",
        grid=(T // tm,),
        in_specs=[
            pl.BlockSpec((tm, D), lambda i: (i, 0)),
            _resident((1, D)),
            _resident((D, 3 * D)),
            _resident((3, D)),
            _resident((D, D)),
        ],
        out_specs=[pl.BlockSpec((tm, D), lambda i: (i, 0)), pl.BlockSpec((tm, 3 * D), lambda i: (i, 0))],
        out_shape=[jax.ShapeDtypeStruct((T, D), F32), jax.ShapeDtypeStruct((T, 3 * D), F32)],
        scratch=[pltpu.VMEM((8, D), F32)],
    )


def conv_bwd(dx1, x, gain, bcx, cw, w_in, w_out, *, seq, tm, plan=None):
    T, D = x.shape
    n = T // tm
    tps = seq // tm
    r8 = tm // 8

    def body(d_ref, x_ref, g_ref, bcx_ref, halo_ref, cw_ref, win_ref, wout_ref,
             gx_ref, dbcx_ref, h_ref, z_ref, d16_ref, dcw_ref, dg_ref, carry_ref):
        i = pl.program_id(0)
        t = n - 1 - i

        @pl.when(i == 0)
        def _():
            dcw_ref[...] = jnp.zeros_like(dcw_ref)
            dg_ref[...] = jnp.zeros_like(dg_ref)

        @pl.when(t % tps == tps - 1)
        def _():
            carry_ref[...] = jnp.zeros_like(carry_ref)

        d = d_ref[...]
        d16 = d.astype(BF16)
        d16_ref[...] = d16
        dz = _dot_nt(d16, wout_ref[...])
        bcx = bcx_ref[...]
        b, c, xv = bcx[:, :D], bcx[:, D : 2 * D], bcx[:, 2 * D :]
        u = b * xv
        halo = halo_ref[...]
        hu = jnp.where(t % tps == 0, 0.0, halo[:, :D] * halo[:, 2 * D :])
        row = lax.broadcasted_iota(jnp.int32, u.shape, 0)
        u1 = _shift_down(u, hu, row, 1)
        u2 = _shift_down(u, hu, row, 2)
        cwv = cw_ref[...]
        y = cwv[0:1, :] * u2 + cwv[1:2, :] * u1 + cwv[2:3, :] * u
        z_ref[...] = (c * y).astype(BF16)
        dc = dz * y
        dy = dz * c
        dcw_ref[0:1, :] += jnp.sum(dy * u2, axis=0, keepdims=True)
        dcw_ref[1:2, :] += jnp.sum(dy * u1, axis=0, keepdims=True)
        dcw_ref[2:3, :] += jnp.sum(dy * u, axis=0, keepdims=True)
        nxt = carry_ref[...]
        dy1 = _shift_up(dy, nxt, row, 1, tm)
        dy2 = _shift_up(dy, nxt, row, 2, tm)
        carry_ref[...] = dy[0:8, :]
        du = cwv[2:3, :] * dy + cwv[1:2, :] * dy1 + cwv[0:1, :] * dy2
        dbcx_ref[:, :D] = (du * xv).astype(BF16)
        dbcx_ref[:, D : 2 * D] = dc.astype(BF16)
        dbcx_ref[:, 2 * D :] = (du * b).astype(BF16)
        dh = _dot_nt(dbcx_ref[...], win_ref[...])
        xt = x_ref[...]
        r = _rms(xt)
        gn = g_ref[...]
        h_ref[...] = ((xt * r) * gn).astype(BF16)
        dx, dgn = _rms_bwd(xt, r, gn, dh)
        dg_ref[0:1, :] += dgn
        gx_ref[...] = d + dx

    rev = lambda i: (n - 1 - i, 0)
    return _call(
        body,
        plan=plan,
        args=(dx1, x, gain, bcx, bcx, cw, w_in, w_out),
        name="conv_bwd",
        grid=(n,),
        in_specs=[
            pl.BlockSpec((tm, D), rev),
            pl.BlockSpec((tm, D), rev),
            _resident((1, D)),
            pl.BlockSpec((tm, 3 * D), rev),
            pl.BlockSpec((8, 3 * D), lambda i: (jnp.maximum((n - 1 - i) * r8 - 1, 0), 0)),
            _resident((3, D)),
            _resident((D, 3 * D)),
            _resident((D, D)),
        ],
        out_specs=[
            pl.BlockSpec((tm, D), rev),
            pl.BlockSpec((tm, 3 * D), rev),
            pl.BlockSpec((tm, D), rev),
            pl.BlockSpec((tm, D), rev),
            pl.BlockSpec((tm, D), rev),
            pl.BlockSpec((8, D), lambda i: (0, 0)),
            pl.BlockSpec((8, D), lambda i: (0, 0)),
        ],
        out_shape=[
            jax.ShapeDtypeStruct((T, D), F32),
            jax.ShapeDtypeStruct((T, 3 * D), BF16),
            jax.ShapeDtypeStruct((T, D), BF16),
            jax.ShapeDtypeStruct((T, D), BF16),
            jax.ShapeDtypeStruct((T, D), BF16),
            jax.ShapeDtypeStruct((8, D), F32),
            jax.ShapeDtypeStruct((8, D), F32),
        ],
        scratch=[pltpu.VMEM((8, D), F32)],
    )


MXU_TILE = 256
FFN_CHUNK = 4 * MXU_TILE


def _sigmoid(g):
    return 1.0 / (1.0 + jnp.exp(-g))


def _ffn_chunks(F):
    assert F % MXU_TILE == 0
    return [(s, min(FFN_CHUNK, F - s)) for s in range(0, F, FFN_CHUNK)]


def ffn_fwd(x, gain, w_gu, w_d, *, tm, plan=None, attn=None, target=None):
    T, D = x.shape
    F = w_d.shape[0]
    row = lambda i: (i, 0)
    tile = pl.BlockSpec((tm, D), row)

    def body(*refs):
        refs = list(refs)
        x_ref, g_ref, wgu_ref, wd_ref = refs[:4]
        del refs[:4]
        if attn is not None:
            ao_ref, wo_ref = refs[:2]
            del refs[:2]
        if target is not None:
            t_ref = refs.pop(0)
        if attn is not None:
            xin_ref = refs.pop(0)
        xo_ref, gu_ref = refs[:2]
        xt = x_ref[...]
        if attn is not None:
            xt = xt + _dot(ao_ref[...], wo_ref[...])
            xin_ref[...] = xt
        h = ((xt * _rms(xt)) * g_ref[...]).astype(BF16)
        acc = xt
        for s, n in _ffn_chunks(F):
            g = _dot(h, wgu_ref[:, s : s + n])
            u = _dot(h, wgu_ref[:, F + s : F + s + n])
            gu_ref[:, s : s + n] = g
            gu_ref[:, F + s : F + s + n] = u
            a = ((g * _sigmoid(g)) * u).astype(BF16)
            acc = acc + _dot(a, wd_ref[s : s + n, :])
        if target is None:
            xo_ref[...] = acc
        else:
            s_ref = refs[2]

            @pl.when(pl.program_id(0) == 0)
            def _():
                s_ref[...] = jnp.zeros_like(s_ref)

            e = acc - t_ref[...]
            xo_ref[...] = e * (1.0 / D)
            s_ref[...] += jnp.sum(jnp.sum(e * e, axis=-1, keepdims=True), axis=0, keepdims=True)

    args = [x, gain, w_gu, w_d]
    in_specs = [tile, _resident((1, D)), _resident((D, 2 * F)), _resident((F, D))]
    out_specs = [tile, pl.BlockSpec((tm, 2 * F), row)]
    out_shape = [jax.ShapeDtypeStruct((T, D), F32), jax.ShapeDtypeStruct((T, 2 * F), F32)]
    if attn is not None:
        args += list(attn)
        in_specs += [pl.BlockSpec((tm, attn[0].shape[1]), row), _resident(attn[1].shape)]
        out_specs.insert(0, tile)
        out_shape.insert(0, jax.ShapeDtypeStruct((T, D), F32))
    if target is not None:
        args.append(target)
        in_specs.append(tile)
        out_specs.append(pl.BlockSpec((8, LANES), lambda i: (0, 0)))
        out_shape.append(jax.ShapeDtypeStruct((8, LANES), F32))
    return _call(body, plan=plan, args=args, name="ffn_fwd", grid=(T // tm,), in_specs=in_specs, out_specs=out_specs,
                 out_shape=out_shape)


def ffn_bwd(dxo, x, gain, gu, w_gu, w_d, *, tm, plan=None, w_o=None):
    T, D = x.shape
    F = w_d.shape[0]

    def body(d_ref, x_ref, g_ref, gu_ref, wgu_ref, wd_ref, *rest):
        if w_o is not None:
            wo_ref, rest = rest[0], rest[1:]
        dx_ref, a_ref, dgu_ref, h_ref, d16_ref, dg_ref = rest[:6]

        @pl.when(pl.program_id(0) == 0)
        def _():
            dg_ref[...] = jnp.zeros_like(dg_ref)

        d = d_ref[...]
        d16 = d.astype(BF16)
        d16_ref[...] = d16
        dh = jnp.zeros((tm, D), F32)
        for c0, n in _ffn_chunks(F):
            g = gu_ref[:, c0 : c0 + n]
            u = gu_ref[:, F + c0 : F + c0 + n]
            da = _dot_nt(d16, wd_ref[c0 : c0 + n, :])
            s = _sigmoid(g)
            sg = g * s
            a_ref[:, c0 : c0 + n] = (sg * u).astype(BF16)
            dg16 = (da * u * (s + sg * (1.0 - s))).astype(BF16)
            du16 = (da * sg).astype(BF16)
            dgu_ref[:, c0 : c0 + n] = dg16
            dgu_ref[:, F + c0 : F + c0 + n] = du16
            dh = dh + _dot_nt(dg16, wgu_ref[:, c0 : c0 + n]) + _dot_nt(du16, wgu_ref[:, F + c0 : F + c0 + n])
        xt = x_ref[...]
        r = _rms(xt)
        gn = g_ref[...]
        h_ref[...] = ((xt * r) * gn).astype(BF16)
        dx, dgn = _rms_bwd(xt, r, gn, dh)
        dg_ref[0:1, :] += dgn
        dxi = d + dx
        dx_ref[...] = dxi
        if w_o is not None:
            dxi16_ref, dao_ref = rest[6:8]
            dxi16 = dxi.astype(BF16)
            dxi16_ref[...] = dxi16
            dao_ref[...] = _dot_nt(dxi16, wo_ref[...]).astype(BF16)

    tile = pl.BlockSpec((tm, D), lambda i: (i, 0))
    args = [dxo, x, gain, gu, w_gu, w_d]
    wide = lambda n: pl.BlockSpec((tm, n), lambda i: (i, 0))
    in_specs = [tile, tile, _resident((1, D)), wide(2 * F), _resident((D, 2 * F)), _resident((F, D))]
    out_specs = [tile, wide(F), wide(2 * F), tile, tile, pl.BlockSpec((8, D), lambda i: (0, 0))]
    out_shape = [
        jax.ShapeDtypeStruct((T, D), F32),
        jax.ShapeDtypeStruct((T, F), BF16),
        jax.ShapeDtypeStruct((T, 2 * F), BF16),
        jax.ShapeDtypeStruct((T, D), BF16),
        jax.ShapeDtypeStruct((T, D), BF16),
        jax.ShapeDtypeStruct((8, D), F32),
    ]
    if w_o is not None:
        args.append(w_o)
        in_specs.append(_resident(w_o.shape))
        out_specs += [tile, pl.BlockSpec((tm, w_o.shape[0]), lambda i: (i, 0))]
        out_shape += [jax.ShapeDtypeStruct((T, D), BF16), jax.ShapeDtypeStruct((T, w_o.shape[0]), BF16)]
    return _call(body, plan=plan, args=args, name="ffn_bwd", grid=(T // tm,), in_specs=in_specs, out_specs=out_specs,
                 out_shape=out_shape)


def wgrad(a, b, *, name, a_cols=0, b_cols=0, flat=False, tk, out_dtype=BF16, plan=None):
    T, K = a.shape
    J = 1
    if a_cols:
        K = a_cols
        J = a.shape[1] // K
        a_spec = pl.BlockSpec((tk, K), lambda j, k: (k, j))
    else:
        a_spec = pl.BlockSpec((tk, K), lambda j, k: (k, 0))
    if b_cols:
        N = b_cols
        J = b.shape[1] // N
        b_spec = pl.BlockSpec((tk, N), lambda j, k: (k, j))
    else:
        N = b.shape[1]
        b_spec = pl.BlockSpec((tk, N), lambda j, k: (k, 0))
    nk = T // tk

    def body(a_ref, b_ref, o_ref, acc_ref):
        k = pl.program_id(1)

        @pl.when(k == 0)
        def _():
            acc_ref[...] = jnp.zeros_like(acc_ref)

        acc_ref[...] += _dot_tn(a_ref[...], b_ref[...])

        @pl.when(k == nk - 1)
        def _():
            o_ref[...] = acc_ref[...].astype(out_dtype)

    outs, sent = _call(
        body,
        plan=plan,
        args=(a, b),
        name=name,
        grid=(J, nk),
        in_specs=[a_spec, b_spec],
        out_specs=[pl.BlockSpec((K, N), lambda j, k: (0, j)) if flat else pl.BlockSpec((None, K, N), lambda j, k: (j, 0, 0))],
        out_shape=[jax.ShapeDtypeStruct((K, J * N) if flat else (J, K, N), out_dtype)],
        scratch=[pltpu.VMEM((K, N), F32)],
    )
    return outs[0], sent


def _seg(x, lo):
    s_lo = jnp.sum(jnp.where(lo, x, 0.0), axis=-1, keepdims=True)
    s_hi = jnp.sum(jnp.where(lo, 0.0, x), axis=-1, keepdims=True)
    return jnp.where(lo, s_lo, s_hi)


def _head_norm(x, gain, lo):
    r = lax.rsqrt(_seg(x * x, lo) * (1.0 / HEAD_DIM) + EPS)
    return (x * r) * gain, r


def _head_norm_bwd(x, r, gain, dy, lo):
    xn = x * r
    dxn = dy * gain
    dx = r * (dxn - xn * (_seg(dxn * xn, lo) * (1.0 / HEAD_DIM)))
    return dx, jnp.sum(dy * xn, axis=0, keepdims=True)


def _swap_halves(x):
    return pltpu.roll(x, HEAD_DIM, 1)


def qkv_proj(x, gain, w, qg, kg, *, tm):
    T, D = x.shape
    N = w.shape[1]
    kvw = N_KV_HEADS * HEAD_DIM
    nqt, nkt = D // LANES, kvw // LANES

    def body(x_ref, g_ref, w_ref, qg_ref, kg_ref, qkv_ref, q_ref, kd_ref, vd_ref):
        xt = x_ref[...]
        h = ((xt * _rms(xt)) * g_ref[...]).astype(BF16)
        qkv = _dot(h, w_ref[...])
        qkv_ref[...] = qkv
        lo = lax.broadcasted_iota(jnp.int32, (1, LANES), 1) < HEAD_DIM
        for t in range(nqt):
            qn, _ = _head_norm(qkv[:, t * LANES : (t + 1) * LANES], qg_ref[...], lo)
            q_ref[:, t * LANES : (t + 1) * LANES] = (qn * SCALE).astype(BF16)
        for t in range(nkt):
            kn, _ = _head_norm(qkv[:, D + t * LANES : D + (t + 1) * LANES], kg_ref[...], lo)
            v = qkv[:, D + kvw + t * LANES : D + kvw + (t + 1) * LANES]
            for src, dst in ((kn, kd_ref), (v, vd_ref)):
                sw = _swap_halves(src)
                dst[:, 2 * t * LANES : (2 * t + 1) * LANES] = jnp.where(lo, src, sw).astype(BF16)
                dst[:, (2 * t + 1) * LANES : (2 * t + 2) * LANES] = jnp.where(lo, sw, src).astype(BF16)

    row = lambda i: (i, 0)
    return pl.pallas_call(
        body,
        name="qkv_proj",
        grid=(T // tm,),
        in_specs=[pl.BlockSpec((tm, D), row), _resident((1, D)), _resident((D, N)), _resident((1, LANES)), _resident((1, LANES))],
        out_specs=[pl.BlockSpec((tm, N), row), pl.BlockSpec((tm, D), row), pl.BlockSpec((tm, 2 * kvw), row), pl.BlockSpec((tm, 2 * kvw), row)],
        out_shape=[
            jax.ShapeDtypeStruct((T, N), F32),
            jax.ShapeDtypeStruct((T, D), BF16),
            jax.ShapeDtypeStruct((T, 2 * kvw), BF16),
            jax.ShapeDtypeStruct((T, 2 * kvw), BF16),
        ],
        compiler_params=_params(1),
    )(x, gain, w, qg, kg)


def _attn_tables(sinks, n_q_heads):
    P = n_q_heads // N_KV_HEADS // 2
    h = jnp.arange(1, n_q_heads + 1, dtype=F32)
    slopes = jnp.exp2(-8.0 * h / n_q_heads).reshape(N_KV_HEADS, P, 1, 2, 1)
    qi = jnp.arange(BLOCK)[:, None]
    kj = jnp.arange(BLOCK)[None, :]
    dist = jnp.where(kj <= qi, qi - kj, qi + BLOCK - kj).astype(F32)
    shape = (N_KV_HEADS, P, BLOCK, 2, BLOCK)
    bias = jnp.broadcast_to(-slopes * dist[None, None, :, None, :], shape)
    sink = jnp.broadcast_to(sinks.astype(F32).reshape(N_KV_HEADS, P, 1, 2, 1), shape)
    return bias.reshape(N_KV_HEADS, P * BLOCK, 2 * BLOCK), sink.reshape(N_KV_HEADS, P * BLOCK, 2 * BLOCK)


def _attn_specs(D, nb):
    kvw2 = 2 * N_KV_HEADS * HEAD_DIM
    cur = lambda b, i: (b * nb + i, 0)
    prev = lambda b, i: (jnp.maximum(b * nb + i - 1, 0), 0)
    return [
        pl.BlockSpec((BLOCK, D), cur),
        pl.BlockSpec((BLOCK, kvw2), cur),
        pl.BlockSpec((BLOCK, kvw2), prev),
        pl.BlockSpec((BLOCK, kvw2), cur),
        pl.BlockSpec((BLOCK, kvw2), prev),
    ]


def _attn_operands(kh, P, lo, q_ref, kc_ref, kp_ref, vc_ref, vp_ref):
    sl = slice(kh * LANES, (kh + 1) * LANES)

    def cat(prev_ref, cur_ref):
        d = jnp.concatenate([prev_ref[:, sl], cur_ref[:, sl]], axis=0)
        z = jnp.zeros_like(d)
        return jnp.concatenate([jnp.where(lo, d, z), jnp.where(lo, z, d)], axis=0)

    qt = jnp.concatenate([q_ref[:, (kh * P + pr) * LANES : (kh * P + pr + 1) * LANES] for pr in range(P)], axis=0)
    return qt, cat(kp_ref, kc_ref), cat(vp_ref, vc_ref)


def _attn_exp(s_all, bias, sink, tri, first):
    out = []
    for par in range(2):
        c0 = 2 * par * BLOCK
        s = jnp.where(tri, s_all[:, c0 + BLOCK : c0 + 2 * BLOCK], jnp.where(first, NEG, s_all[:, c0 : c0 + BLOCK]))
        s = s + bias[:, par * BLOCK : (par + 1) * BLOCK]
        snk = sink[:, par * BLOCK : (par + 1) * BLOCK]
        m = jnp.maximum(jnp.max(s, axis=-1, keepdims=True), snk)
        out.append((jnp.exp(s - m), jnp.exp(snk - m)))
    return out


def _unfold(x, tri):
    z = jnp.zeros_like(x)
    return jnp.concatenate([jnp.where(tri, z, x), jnp.where(tri, x, z)], axis=1)


def _attn_masks(R):
    lane = lax.broadcasted_iota(jnp.int32, (1, LANES), 1)
    row = lax.broadcasted_iota(jnp.int32, (R, BLOCK), 0) & (BLOCK - 1)
    col = lax.broadcasted_iota(jnp.int32, (R, BLOCK), 1)
    return lane, lane < HEAD_DIM, col <= row


def attn_fwd(q16, kd, vd, bias, sink, *, seq, n_seq):
    T, D = q16.shape
    nb = seq // BLOCK
    P = D // HEAD_DIM // N_KV_HEADS // 2
    R = P * BLOCK
    KV = range(N_KV_HEADS)

    def body(q_ref, kc_ref, kp_ref, vc_ref, vp_ref, bias_ref, sink_ref, o_ref):
        first = pl.program_id(1) == 0
        _, lo, tri = _attn_masks(R)
        r4 = lax.broadcasted_iota(jnp.int32, (4 * BLOCK, LANES), 0)
        l4 = lax.broadcasted_iota(jnp.int32, (4 * BLOCK, LANES), 1)
        ones = ((r4 < 2 * BLOCK) == (l4 < HEAD_DIM)).astype(BF16)
        ops = [_attn_operands(kh, P, lo, q_ref, kc_ref, kp_ref, vc_ref, vp_ref) for kh in KV]
        s_all = [_dot_nt(ops[kh][0], ops[kh][1]) for kh in KV]
        ex = [_attn_exp(s_all[kh], bias_ref[kh], sink_ref[kh], tri, first) for kh in KV]
        lhs = [jnp.concatenate([_unfold(e, tri) for e, _ in ex[kh]], axis=1).astype(BF16) for kh in KV]
        o = [_dot(lhs[kh], ops[kh][2]) for kh in KV]
        den = [_dot(lhs[kh], ones) for kh in KV]
        for kh in KV:
            out = o[kh] / (den[kh] + jnp.where(lo, ex[kh][0][1], ex[kh][1][1]))
            for pr in range(P):
                t = kh * P + pr
                o_ref[:, t * LANES : (t + 1) * LANES] = out[pr * BLOCK : (pr + 1) * BLOCK, :].astype(BF16)

    return pl.pallas_call(
        body,
        name="attn_fwd",
        grid=(n_seq, nb),
        in_specs=_attn_specs(D, nb) + [_resident((N_KV_HEADS, R, 2 * BLOCK)), _resident((N_KV_HEADS, R, 2 * BLOCK))],
        out_specs=pl.BlockSpec((BLOCK, D), lambda b, i: (b * nb + i, 0)),
        out_shape=jax.ShapeDtypeStruct((T, D), BF16),
        compiler_params=_params(2),
    )(q16, kd, kd, vd, vd, bias, sink)


def attn_bwd(q16, kd, vd, do, bias, sink, *, seq, n_seq):
    T, D = q16.shape
    kvw2 = 2 * N_KV_HEADS * HEAD_DIM
    nb = seq // BLOCK
    G = D // HEAD_DIM // N_KV_HEADS
    P = G // 2
    R = P * BLOCK
    KV = range(N_KV_HEADS)

    def body(q_ref, kc_ref, kp_ref, vc_ref, vp_ref, do_ref, bias_ref, sink_ref,
             dq_ref, dkc_ref, dkp_ref, dvc_ref, dvp_ref, dsink_ref):
        first = pl.program_id(1) == 0

        @pl.when(jnp.logical_and(pl.program_id(0) == 0, first))
        def _():
            dsink_ref[...] = jnp.zeros_like(dsink_ref)

        lane, lo, tri = _attn_masks(R)
        r4 = lax.broadcasted_iota(jnp.int32, (4 * BLOCK, 2 * BLOCK), 0)
        c4 = lax.broadcasted_iota(jnp.int32, (4 * BLOCK, 2 * BLOCK), 1)
        ones = ((r4 < 2 * BLOCK) == (c4 < BLOCK)).astype(BF16)
        ops = [_attn_operands(kh, P, lo, q_ref, kc_ref, kp_ref, vc_ref, vp_ref) for kh in KV]
        do16 = [jnp.concatenate([do_ref[:, (kh * P + pr) * LANES : (kh * P + pr + 1) * LANES] for pr in range(P)], axis=0)
                for kh in KV]
        s_all = [_dot_nt(ops[kh][0], ops[kh][1]) for kh in KV]
        dp_all = [_dot_nt(do16[kh], ops[kh][2]) for kh in KV]
        ex = [_attn_exp(s_all[kh], bias_ref[kh], sink_ref[kh], tri, first) for kh in KV]
        den = [_dot(jnp.concatenate([_unfold(e, tri) for e, _ in ex[kh]], axis=1).astype(BF16), ones) for kh in KV]
        dsink = jnp.zeros((1, LANES), F32)
        pf, dsf = [], []
        for kh in KV:
            ps_, ds_ = [], []
            for par in range(2):
                e, es = ex[kh][par]
                inv = 1.0 / (den[kh][:, par * BLOCK : (par + 1) * BLOCK] + es)
                p = e * inv
                c0 = 2 * par * BLOCK
                dp = jnp.where(tri, dp_all[kh][:, c0 + BLOCK : c0 + 2 * BLOCK], dp_all[kh][:, c0 : c0 + BLOCK])
                delta = jnp.sum(p * dp, axis=-1, keepdims=True)
                ds_.append(_unfold(p * (dp - delta), tri))
                ps_.append(_unfold(p, tri))
                dsr = -((es * inv) * delta)
                for pr in range(P):
                    hq = kh * G + 2 * pr + par
                    tot = jnp.sum(dsr[pr * BLOCK : (pr + 1) * BLOCK, :], axis=0, keepdims=True)
                    dsink = dsink + jnp.where(lane == hq, tot, 0.0)
            pf.append(jnp.concatenate(ps_, axis=1).astype(BF16))
            dsf.append(jnp.concatenate(ds_, axis=1).astype(BF16))
        dq = [_dot(dsf[kh], ops[kh][1]) for kh in KV]
        dk = [_dot_tn(dsf[kh], ops[kh][0]) for kh in KV]
        dv = [_dot_tn(pf[kh], do16[kh]) for kh in KV]
        dsink_ref[0:1, :] += dsink
        for kh in KV:
            sl = slice(kh * LANES, (kh + 1) * LANES)
            for pr in range(P):
                t = kh * P + pr
                dq_ref[:, t * LANES : (t + 1) * LANES] = dq[kh][pr * BLOCK : (pr + 1) * BLOCK, :]
            for full, prev_ref, cur_ref in ((dk[kh], dkp_ref, dkc_ref), (dv[kh], dvp_ref, dvc_ref)):
                dup = jnp.where(lo, full[: 2 * BLOCK, :], full[2 * BLOCK :, :])
                prev_ref[:, sl] = dup[:BLOCK, :]
                cur_ref[:, sl] = dup[BLOCK:, :]

    cur = lambda b, i: (b * nb + i, 0)
    kv_spec = pl.BlockSpec((BLOCK, kvw2), cur)
    kv_shape = jax.ShapeDtypeStruct((T, kvw2), F32)
    return pl.pallas_call(
        body,
        name="attn_bwd",
        grid=(n_seq, nb),
        in_specs=_attn_specs(D, nb)
        + [pl.BlockSpec((BLOCK, D), cur), _resident((N_KV_HEADS, R, 2 * BLOCK)), _resident((N_KV_HEADS, R, 2 * BLOCK))],
        out_specs=[pl.BlockSpec((BLOCK, D), cur), kv_spec, kv_spec, kv_spec, kv_spec, pl.BlockSpec((8, LANES), lambda b, i: (0, 0))],
        out_shape=[jax.ShapeDtypeStruct((T, D), F32), kv_shape, kv_shape, kv_shape, kv_shape, jax.ShapeDtypeStruct((8, LANES), F32)],
        compiler_params=_params(2),
    )(q16, kd, kd, vd, vd, do, bias, sink)


def qkv_bwd(dq, dkc, dkp, dvc, dvp, qkv, dres, x, gain, w_qkv, qg, kg, *, seq):
    T, D = x.shape
    kvw2 = dkc.shape[1]
    kvw = kvw2 // 2
    nqt, nkt = D // LANES, kvw // LANES
    nb = seq // BLOCK
    tm = 2 * BLOCK
    n = T // tm

    def body(dq_ref, dkc_ref, dkpa_ref, dkpb_ref, dvc_ref, dvpa_ref, dvpb_ref, qkv_ref, dres_ref, x_ref, g_ref, w_ref,
             qg_ref, kg_ref, dx_ref, dqkv_ref, h_ref, dg_ref, hg_ref):
        i = pl.program_id(0)

        @pl.when(i == 0)
        def _():
            dg_ref[...] = jnp.zeros_like(dg_ref)
            hg_ref[...] = jnp.zeros_like(hg_ref)

        lo = lax.broadcasted_iota(jnp.int32, (1, LANES), 1) < HEAD_DIM
        last = (2 * i + 1) % nb == nb - 1
        dkd = dkc_ref[...] + jnp.concatenate([dkpa_ref[...], jnp.where(last, 0.0, dkpb_ref[...])], axis=0)
        dvd = dvc_ref[...] + jnp.concatenate([dvpa_ref[...], jnp.where(last, 0.0, dvpb_ref[...])], axis=0)

        def undup(d, t):
            a, b = d[:, 2 * t * LANES : (2 * t + 1) * LANES], d[:, (2 * t + 1) * LANES : (2 * t + 2) * LANES]
            return jnp.where(lo, a + _swap_halves(a), b + _swap_halves(b))

        gq = jnp.zeros((1, LANES), F32)
        for t in range(nqt):
            sl = slice(t * LANES, (t + 1) * LANES)
            q = qkv_ref[:, sl]
            _, r = _head_norm(q, qg_ref[...], lo)
            dxq, dgn = _head_norm_bwd(q, r, qg_ref[...], dq_ref[:, sl] * SCALE, lo)
            dqkv_ref[:, sl] = dxq.astype(BF16)
            gq = gq + dgn
        gk = jnp.zeros((1, LANES), F32)
        for t in range(nkt):
            sl = slice(D + t * LANES, D + (t + 1) * LANES)
            k = qkv_ref[:, sl]
            _, r = _head_norm(k, kg_ref[...], lo)
            dxk, dgn = _head_norm_bwd(k, r, kg_ref[...], undup(dkd, t), lo)
            dqkv_ref[:, sl] = dxk.astype(BF16)
            gk = gk + dgn
            dqkv_ref[:, D + kvw + t * LANES : D + kvw + (t + 1) * LANES] = undup(dvd, t).astype(BF16)
        hg_ref[0:1, :] += gq
        hg_ref[1:2, :] += gk
        dh = _dot_nt(dqkv_ref[...], w_ref[...])
        xt = x_ref[...]
        r = _rms(xt)
        gn = g_ref[...]
        h_ref[...] = ((xt * r) * gn).astype(BF16)
        dx, dgn = _rms_bwd(xt, r, gn, dh)
        dg_ref[0:1, :] += dgn
        dx_ref[...] = dres_ref[...] + dx

    row = lambda i: (i, 0)
    nxt_a = pl.BlockSpec((BLOCK, kvw2), lambda i: (2 * i + 1, 0))
    nxt_b = pl.BlockSpec((BLOCK, kvw2), lambda i: (jnp.minimum(2 * i + 2, 2 * n - 1), 0))
    return pl.pallas_call(
        body,
        name="qkv_bwd",
        grid=(n,),
        in_specs=[
            pl.BlockSpec((tm, D), row),
            pl.BlockSpec((tm, kvw2), row),
            nxt_a,
            nxt_b,
            pl.BlockSpec((tm, kvw2), row),
            nxt_a,
            nxt_b,
            pl.BlockSpec((tm, D + kvw2), row),
            pl.BlockSpec((tm, D), row),
            pl.BlockSpec((tm, D), row),
            _resident((1, D)),
            _resident((D, D + kvw2)),
            _resident((1, LANES)),
            _resident((1, LANES)),
        ],
        out_specs=[
            pl.BlockSpec((tm, D), row),
            pl.BlockSpec((tm, D + kvw2), row),
            pl.BlockSpec((tm, D), row),
            pl.BlockSpec((8, D), lambda i: (0, 0)),
            pl.BlockSpec((8, LANES), lambda i: (0, 0)),
        ],
        out_shape=[
            jax.ShapeDtypeStruct((T, D), F32),
            jax.ShapeDtypeStruct((T, D + kvw2), BF16),
            jax.ShapeDtypeStruct((T, D), BF16),
            jax.ShapeDtypeStruct((8, D), F32),
            jax.ShapeDtypeStruct((8, LANES), F32),
        ],
        compiler_params=_params(1),
    )(dq, dkc, dkp, dkp, dvc, dvp, dvp, qkv, dres, x, gain, w_qkv, qg, kg)


def local_step(x, target, gains, w, *, seq, tm=256, tm_ffn=256, tk=2048, shards=None):
    T, D = x.shape
    n_seq = T // seq
    nm, nf, qgain, kgain, sinks = gains
    H = D // HEAD_DIM
    tk, tk_long = min(tk, T), min(2 * tk, T)
    qg2, kg2 = jnp.tile(qgain, (1, 2)), jnp.tile(kgain, (1, 2))
    bias, sinkcol = _attn_tables(sinks, H)

    dist = shards is not None
    w = dict(w)

    plan = _Gather([shards["w_gu"][0], shards["w_d"][0]]) if dist else None
    (x1, bcx), got = conv_fwd(x, nm[0:1], w["w_in"], w["cw"], w["w_out"], seq=seq, tm=tm, plan=plan)
    if dist:
        w["w_gu"], w["w_d"] = [cols_from_shards(got[0]), None], [got[1].reshape(-1, D), None]
    plan = _Gather([shards["w_qkv"], shards["w_o"], shards["w_gu"][1], shards["w_d"][1]]) if dist else None
    (x2, gu0), got = ffn_fwd(x1, nf[0:1], w["w_gu"][0], w["w_d"][0], tm=tm_ffn, plan=plan)
    if dist:
        w["w_qkv"], w["w_o"] = cols_from_shards(got[0]), got[1].reshape(D, D)
        w["w_gu"][1], w["w_d"][1] = cols_from_shards(got[2]), got[3].reshape(-1, D)
    qkv, q16, kd, vd = qkv_proj(x2, nm[1:2], w["w_qkv"], qg2, kg2, tm=tm)
    ao = attn_fwd(q16, kd, vd, bias, sinkcol, seq=seq, n_seq=n_seq)
    (x3, dx4, gu1, sse), _ = ffn_fwd(x2, nf[1:2], w["w_gu"][1], w["w_d"][1], tm=tm_ffn, attn=(ao, w["w_o"]), target=target)

    by_dest = lambda a: a.reshape(N_DEV, -1, a.shape[-1])
    fresh = lambda a, *lead: jax.ShapeDtypeStruct((N_DEV,) + lead + a.shape[1:], a.dtype)
    gu_cols = 2 * MXU_TILE

    (dx3, a16, dgu, h16, d16, dnf1, dx3_16, dao), _ = ffn_bwd(
        dx4, x3, nf[1:2], gu1, w["w_gu"][1], w["w_d"][1], tm=tm, w_o=w["w_o"])
    g_gu1 = shards_from_cols(wgrad(h16, dgu, name="wgrad_gu1", b_cols=gu_cols, flat=True, tk=tk_long)[0])
    g_d1 = by_dest(wgrad(a16, d16, name="wgrad_d1", a_cols=a16.shape[1] // 2, tk=tk)[0])
    g_o = by_dest(wgrad(ao, dx3_16, name="wgrad_o", tk=tk)[0])
    dq, dkc, dkp, dvc, dvp, dsinks = attn_bwd(q16, kd, vd, dao, bias, sinkcol, seq=seq, n_seq=n_seq)
    dx2, dqkv16, h16, dnm1, dgains = qkv_bwd(dq, dkc, dkp, dvc, dvp, qkv, dx3, x2, nm[1:2], w["w_qkv"], qg2, kg2, seq=seq)
    g_qkv = shards_from_cols(wgrad(h16, dqkv16, name="wgrad_qkv", tk=tk)[0][0])

    plan = _Scatter([(g_gu1, fresh(g_gu1, 2), 1, False), (g_d1, fresh(g_d1, 2), 1, False)]) if dist else None
    (dx1, a16, dgu, h16, d16, dnf0), land_ffn = ffn_bwd(dx2, x1, nf[0:1], gu0, w["w_gu"][0], w["w_d"][0], tm=tm, plan=plan)
    plan = _Scatter([(g_o, fresh(g_o), None, False), (g_qkv, fresh(g_qkv), None, False)]) if dist else None
    g_gu0, land_attn = wgrad(h16, dgu, name="wgrad_gu0", b_cols=gu_cols, flat=True, tk=tk_long, plan=plan)
    g_gu0 = shards_from_cols(g_gu0)
    g_d0 = by_dest(wgrad(a16, d16, name="wgrad_d0", a_cols=a16.shape[1] // 2, tk=tk)[0])
    plan = _Scatter([(g_gu0, land_ffn[0], 0, False)]) if dist else None
    (gx, dbcx, h16, z16, d16, dcw, dnm0), land_gu = conv_bwd(
        dx1, x, nm[0:1], bcx, w["cw"], w["w_in"], w["w_out"], seq=seq, tm=tm, plan=plan)
    g_out = by_dest(wgrad(z16, d16, name="wgrad_out", tk=tk)[0])
    g_cw = dcw[0:3].reshape(3, N_DEV, D // N_DEV).transpose(1, 0, 2)
    plan = _Scatter([(g_d0, land_ffn[1], 0, False), (g_out, fresh(g_out), None, False), (g_cw, fresh(g_cw), None, False)]) if dist else None
    g_in, land_last = wgrad(h16, dbcx, name="wgrad_in", b_cols=3 * D // N_DEV, tk=tk_long, plan=plan)
    g = dict(w_in=g_in, cw=g_cw, w_out=g_out)
    if dist:
        g.update(w_gu=land_gu[0], w_d=land_last[0], w_out=land_last[1], cw=land_last[2], w_o=land_attn[0], w_qkv=land_attn[1])
    else:
        g.update(w_gu0=g_gu0, w_gu1=g_gu1, w_d0=g_d0, w_d1=g_d1, w_o=g_o, w_qkv=g_qkv)
    small = dict(nm0=dnm0, nm1=dnm1, nf0=dnf0, nf1=dnf1, gains=dgains, sinks=dsinks)
    return sse, gx, g, small


def _adamw_math(g, w, m, v):
    m = ADAM_B1 * m + (1.0 - ADAM_B1) * g
    v = ADAM_B2 * v + (1.0 - ADAM_B2) * (g * g)
    m_hat = m / (1.0 - ADAM_B1 ** ADAM_STEP)
    v_hat = v / (1.0 - ADAM_B2 ** ADAM_STEP)
    delta = -ADAM_LR * (m_hat / (jnp.sqrt(v_hat) + ADAM_EPS) + ADAM_WD * w)
    return delta, m, v


def adamw(parts, w, m, v, *, name):
    n, R, C = parts.shape
    tr = R
    for cand in (256, 128, 88, 64, 32, 16, 8):
        if R > cand and R % cand == 0:
            tr = cand
            break

    def body(p_ref, w_ref, m_ref, v_ref, g_ref, d_ref, mo_ref, vo_ref):
        g = p_ref[0].astype(F32)
        for s in range(1, n):
            g = g + p_ref[s].astype(F32)
        g_ref[...] = g
        d_ref[...], mo_ref[...], vo_ref[...] = _adamw_math(g, w_ref[...], m_ref[...], v_ref[...])

    blk = pl.BlockSpec((tr, C), lambda i: (i, 0))
    return pl.pallas_call(
        body,
        name=name,
        grid=(R // tr,),
        in_specs=[pl.BlockSpec((n, tr, C), lambda i: (0, i, 0)), blk, blk, blk],
        out_specs=[blk] * 4,
        out_shape=[jax.ShapeDtypeStruct((R, C), F32)] * 4,
        compiler_params=_params(1),
    )(parts, w, m, v)


def pack_small(small, sse, D):
    W = max(D, 2 * LANES)

    def body(nm0, nm1, nf0, nf1, gains, sinks, sse_ref, o_ref):
        o_ref[...] = jnp.zeros_like(o_ref)
        o_ref[0:1, :D] = nm0[0:1, :]
        o_ref[1:2, :D] = nm1[0:1, :]
        o_ref[2:3, :D] = nf0[0:1, :]
        o_ref[3:4, :D] = nf1[0:1, :]
        gq = gains[0:1, :] + pltpu.roll(gains[0:1, :], HEAD_DIM, 1)
        gk = gains[1:2, :] + pltpu.roll(gains[1:2, :], HEAD_DIM, 1)
        lane = lax.broadcasted_iota(jnp.int32, (1, LANES), 1)
        o_ref[4:5, :LANES] = jnp.where(lane < HEAD_DIM, gq, gk)
        o_ref[4:5, LANES : 2 * LANES] = sinks[0:1, :]
        o_ref[5:6, :LANES] = sse_ref[0:1, :] * (0.5 / D)

    return pl.pallas_call(
        body,
        name="pack_small",
        out_shape=jax.ShapeDtypeStruct((8, W), F32),
    )(small["nm0"], small["nm1"], small["nf0"], small["nf1"], small["gains"], small["sinks"], sse)


def _pack_small_params(nm, nf, qg, kg, sk, D):
    W = max(D, 2 * LANES)
    row4 = jnp.concatenate([qg.reshape(-1), kg.reshape(-1), jnp.zeros((LANES - 2 * HEAD_DIM,), F32), sk.reshape(-1)])
    row4 = jnp.pad(row4, (0, W - row4.shape[0]))
    rows = [jnp.pad(r, (0, W - D)) for r in (nm[0], nm[1], nf[0], nf[1])] + [row4]
    return jnp.concatenate([jnp.stack(rows), jnp.zeros((3, W), F32)], axis=0)


def _unpack_small(a, D, H):
    nm = a[0:2, :D]
    nf = a[2:4, :D]
    qg = a[4:5, 0:HEAD_DIM]
    kg = a[4:5, HEAD_DIM : 2 * HEAD_DIM]
    sk = a[4:5, LANES : LANES + H]
    return qg, kg, sk, nm, nf


def kernel(x, conv_w_in, conv_w, conv_w_out, attn_w_qkv, attn_q_gain, attn_k_gain, attn_sinks, attn_w_o, norm_mixer, norm_ffn, ffn_w_gate_up, ffn_w_down, loss_target, m_conv_w_in, m_conv_w, m_conv_w_out, m_attn_w_qkv, m_attn_q_gain, m_attn_k_gain, m_attn_sinks, m_attn_w_o, m_norm_mixer, m_norm_ffn, m_ffn_w_gate_up, m_ffn_w_down, v_conv_w_in, v_conv_w, v_conv_w_out, v_attn_w_qkv, v_attn_q_gain, v_attn_k_gain, v_attn_sinks, v_attn_w_o, v_norm_mixer, v_norm_ffn, v_ffn_w_gate_up, v_ffn_w_down):
    n_seq, seq, D = x.shape
    T = n_seq * seq
    H = D // HEAD_DIM
    L = ffn_w_gate_up.shape[0]

    full = run_plan(_Gather([conv_w_in[0].astype(BF16), conv_w[0], conv_w_out[0].astype(BF16)]), name="gather_conv_weights")
    w = dict(w_in=cols_from_shards(full[0]), cw=full[1].transpose(1, 0, 2).reshape(3, D),
             w_out=full[2].reshape(D, D))
    shards = dict(w_gu=[ffn_w_gate_up[l].astype(BF16) for l in range(L)], w_d=[ffn_w_down[l].astype(BF16) for l in range(L)],
                  w_qkv=attn_w_qkv[0].astype(BF16), w_o=attn_w_o[0].astype(BF16))
    gains = (norm_mixer, norm_ffn, attn_q_gain, attn_k_gain, attn_sinks)
    sse, gx, g, small = local_step(x.reshape(T, D), loss_target.reshape(T, D), gains, w, seq=seq, shards=shards)

    packed = pack_small(small, sse, D)
    items = [(g["w_in"], jax.ShapeDtypeStruct(g["w_in"].shape, BF16), None, False),
             (packed, jax.ShapeDtypeStruct((N_DEV,) + packed.shape, packed.dtype), None, True)]
    land_in, small_all = run_plan(_Scatter(items), name="exchange_last_grads")
    bufs = [land_in, g["cw"], g["w_out"], g["w_qkv"], g["w_o"], g["w_gu"], g["w_d"]]

    def flat(a):
        return a.reshape(-1, a.shape[-1])

    big = [conv_w_in, conv_w, conv_w_out, attn_w_qkv, attn_w_o, ffn_w_gate_up, ffn_w_down]
    big_m = [m_conv_w_in, m_conv_w, m_conv_w_out, m_attn_w_qkv, m_attn_w_o, m_ffn_w_gate_up, m_ffn_w_down]
    big_v = [v_conv_w_in, v_conv_w, v_conv_w_out, v_attn_w_qkv, v_attn_w_o, v_ffn_w_gate_up, v_ffn_w_down]
    tags = ["in", "cw", "out", "qkv", "o", "gu", "d"]
    res = []
    for b in range(7):
        parts = bufs[b].reshape(N_DEV, -1, bufs[b].shape[-1])
        outs = adamw(parts, flat(big[b]), flat(big_m[b]), flat(big_v[b]), name="adamw_" + tags[b])
        res.append([o.reshape(big[b].shape) for o in outs])
    sw = _pack_small_params(norm_mixer, norm_ffn, attn_q_gain, attn_k_gain, attn_sinks, D)
    sm = _pack_small_params(m_norm_mixer, m_norm_ffn, m_attn_q_gain, m_attn_k_gain, m_attn_sinks, D)
    sv = _pack_small_params(v_norm_mixer, v_norm_ffn, v_attn_q_gain, v_attn_k_gain, v_attn_sinks, D)
    souts = adamw(small_all, sw, sm, sv, name="adamw_small")
    sres = [_unpack_small(o, D, H) for o in souts]
    loss = souts[0][5, 0]

    def ordered(i):
        r, s = [r[i] for r in res], sres[i]
        return [r[0], r[1], r[2], r[3], s[0], s[1], s[2], r[4], s[3], s[4], r[5], r[6]]

    return (loss, gx.reshape(n_seq, seq, D), *ordered(0), *ordered(1), *ordered(2), *ordered(3))
```

```python
import functools
import math

import jax
import jax.numpy as jnp
from jax import lax
from jax.experimental import pallas as pl
from jax.experimental.pallas import tpu as pltpu

F32 = jnp.float32
BF16 = jnp.bfloat16

EPS = 1e-6
HEAD_DIM = 64
N_KV_HEADS = 4
BLOCK = 128
LANES = 128
N_DEV = 8
NEG = -1e30
SCALE = 1.0 / math.sqrt(HEAD_DIM)

ADAM_LR = 0.001
ADAM_B1 = 0.9
ADAM_B2 = 0.999
ADAM_EPS = 1e-08
ADAM_WD = 0.01
ADAM_STEP = 10

V7X_VMEM_BYTES = 64 * 1024 * 1024
VMEM_LIMIT = V7X_VMEM_BYTES - 8 * 1024 * 1024
MESH = pl.DeviceIdType.MESH

_NT = (((1,), (1,)), ((), ()))
_TN = (((0,), (0,)), ((), ()))


def _params(n_grid):
    return pltpu.CompilerParams(dimension_semantics=("arbitrary",) * n_grid, vmem_limit_bytes=VMEM_LIMIT)


def _resident(shape):
    nd = len(shape)
    return pl.BlockSpec(shape, lambda *_: (0,) * nd, pipeline_mode=pl.Buffered(1))


def _rms(x):
    return lax.rsqrt(jnp.mean(x * x, axis=-1, keepdims=True) + EPS)


def _rms_bwd(x, r, gain, dh):
    xn = x * r
    dxn = dh * gain
    dx = r * (dxn - xn * jnp.mean(dxn * xn, axis=-1, keepdims=True))
    return dx, jnp.sum(dh * xn, axis=0, keepdims=True)


def _dot(a, b):
    return jnp.dot(a, b, preferred_element_type=F32)


def _dot_nt(a, b):
    return lax.dot_general(a, b, _NT, preferred_element_type=F32)


def _dot_tn(a, b):
    return lax.dot_general(a, b, _TN, preferred_element_type=F32)


def _place():
    return lax.axis_index("x"), lax.axis_index("y"), lax.axis_index("c")


def _flip(v, bit):
    return 1 - v if bit else v


def _slot(px, py, pc):
    return 4 * px + 2 * py + pc


class _Gather:
    def __init__(self, shards):
        nt = len(shards)
        self.nt = nt
        self.inputs = list(shards)
        self.out_shapes = [jax.ShapeDtypeStruct((N_DEV,) + s.shape, s.dtype) for s in shards]
        self.scratch = [pltpu.SemaphoreType.DMA((nt, 7)), pltpu.SemaphoreType.DMA((nt, 7)), pltpu.SemaphoreType.DMA((nt,))]
        self.aliases = {}

    def phases(self, total):
        assert total >= 3
        return [(0, self.start), (total - 2, self.forward), (total - 1, self.finish)]

    def _copies(self, ins, outs, sems):
        send_sems, recv_sems, loc_sems = sems
        x, y, c = _place()
        me = _slot(x, y, c)
        sib = (x, y, 1 - c)
        chips = [(_flip(x, k >> 1), _flip(y, k & 1)) for k in (1, 2, 3)]

        def copy(t, k, src, dst_slot, to):
            return pltpu.make_async_remote_copy(
                src_ref=src, dst_ref=outs[t].at[dst_slot], send_sem=send_sems.at[t, k], recv_sem=recv_sems.at[t, k],
                device_id=to, device_id_type=MESH)

        local = [pltpu.make_async_copy(ins[t], outs[t].at[me], loc_sems.at[t]) for t in range(self.nt)]
        first, passed, arrive_ici, arrive_sib = [], [], [], []
        for t in range(self.nt):
            first.append(copy(t, 0, ins[t], me, sib))
            s = _slot(x, y, 1 - c)
            arrive_sib.append(copy(t, 0, outs[t].at[s], s, sib))
            for j, (px, py) in enumerate(chips):
                first.append(copy(t, 1 + j, ins[t], me, (px, py, c)))
                s = _slot(px, py, c)
                arrive_ici.append(copy(t, 1 + j, outs[t].at[s], s, sib))
                passed.append(copy(t, 4 + j, outs[t].at[s], s, sib))
                s = _slot(px, py, 1 - c)
                arrive_sib.append(copy(t, 4 + j, outs[t].at[s], s, sib))
        return local, first, passed, arrive_ici, arrive_sib

    def start(self, ins, outs, sems):
        local, first, _, _, _ = self._copies(ins, outs, sems)
        for cp in local + first:
            cp.start()

    def forward(self, ins, outs, sems):
        _, _, passed, arrive_ici, _ = self._copies(ins, outs, sems)
        for arrival, fwd in zip(arrive_ici, passed):
            arrival.wait_recv()
            fwd.start()

    def finish(self, ins, outs, sems):
        local, first, passed, _, arrive_sib = self._copies(ins, outs, sems)
        for cp in arrive_sib:
            cp.wait_recv()
        for cp in first + passed:
            cp.wait_send()
        for cp in local:
            cp.wait()


class _Scatter:
    def __init__(self, items):
        self.items = items
        nt = len(items)
        self.nt = nt
        reused = [(t, it[1]) for t, it in enumerate(items) if not isinstance(it[1], jax.ShapeDtypeStruct)]
        self.inputs = [it[0] for it in items] + [land for _, land in reused]
        self.out_shapes = [jax.ShapeDtypeStruct(it[1].shape, it[1].dtype) for it in items]
        self.aliases = {nt + i: t for i, (t, _) in enumerate(reused)}
        self.scratch = [pltpu.SemaphoreType.DMA((nt, 7)), pltpu.SemaphoreType.DMA((nt, 7)), pltpu.SemaphoreType.DMA((nt,))]

    def phases(self, total):
        assert total >= 2
        return [(0, self.start), (total - 1, self.finish)]

    def _copies(self, ins, outs, sems):
        send_sems, recv_sems, loc_sems = sems
        x, y, c = _place()
        me = _slot(x, y, c)

        def land(t, s):
            layer = self.items[t][2]
            return outs[t].at[s] if layer is None else outs[t].at[s, layer]

        def src(t, s):
            return ins[t] if self.items[t][3] else ins[t].at[s]

        local = [pltpu.make_async_copy(src(t, me), land(t, me), loc_sems.at[t]) for t in range(self.nt)]
        sends, arrivals = [], []
        for k in range(1, N_DEV):
            px, py, pc = _flip(x, (k >> 2) & 1), _flip(y, (k >> 1) & 1), _flip(c, k & 1)
            peer = _slot(px, py, pc)
            for t in range(self.nt):
                for dst, into in ((land(t, me), sends), (land(t, peer), arrivals)):
                    into.append(pltpu.make_async_remote_copy(
                        src_ref=src(t, peer), dst_ref=dst, send_sem=send_sems.at[t, k - 1], recv_sem=recv_sems.at[t, k - 1],
                        device_id=(px, py, pc), device_id_type=MESH))
        return local, sends, arrivals

    def start(self, ins, outs, sems):
        local, sends, _ = self._copies(ins, outs, sems)
        for cp in local + sends:
            cp.start()

    def finish(self, ins, outs, sems):
        local, sends, arrivals = self._copies(ins, outs, sems)
        for cp in arrivals:
            cp.wait_recv()
        for cp in sends:
            cp.wait_send()
        for cp in local:
            cp.wait()


def _any_specs(n):
    return [pl.BlockSpec(memory_space=pl.ANY)] * n


def run_plan(plan, *, name):
    def body(*refs):
        n_in, n_out = len(plan.inputs), len(plan.out_shapes)
        ins, outs, sems = refs[:n_in], refs[n_in : n_in + n_out], refs[n_in + n_out :]
        for _, phase in plan.phases(3):
            phase(ins, outs, sems)

    return pl.pallas_call(
        body,
        name=name,
        in_specs=_any_specs(len(plan.inputs)),
        out_specs=_any_specs(len(plan.out_shapes)),
        out_shape=plan.out_shapes,
        scratch_shapes=plan.scratch,
        input_output_aliases=plan.aliases,
    )(*plan.inputs)


def _call(body, *, name, grid, in_specs, out_specs, out_shape, args, scratch=(), plan=None):
    n_in, n_out, n_scr = len(in_specs), len(out_specs), len(scratch)
    if plan is None:
        outs = pl.pallas_call(
            body, name=name, grid=grid, in_specs=in_specs, out_specs=out_specs, out_shape=out_shape,
            scratch_shapes=list(scratch), compiler_params=_params(len(grid)))(*args)
        return outs, None
    c_in, c_out = len(plan.inputs), len(plan.out_shapes)
    phases = plan.phases(math.prod(grid))

    def full(*refs):
        a, refs = refs[:n_in], refs[n_in:]
        ci, refs = refs[:c_in], refs[c_in:]
        o, refs = refs[:n_out], refs[n_out:]
        co, refs = refs[:c_out], refs[c_out:]
        s, cs = refs[:n_scr], refs[n_scr:]
        step = pl.program_id(0)
        for d in range(1, len(grid)):
            step = step * grid[d] + pl.program_id(d)
        for at, phase in phases:
            if at == 0:
                pl.when(step == 0)(functools.partial(phase, ci, co, cs))
        body(*a, *o, *s)
        for at, phase in phases:
            if at > 0:
                pl.when(step == at)(functools.partial(phase, ci, co, cs))

    outs = pl.pallas_call(
        full,
        name=name,
        grid=grid,
        in_specs=list(in_specs) + _any_specs(c_in),
        out_specs=list(out_specs) + _any_specs(c_out),
        out_shape=list(out_shape) + plan.out_shapes,
        scratch_shapes=list(scratch) + plan.scratch,
        input_output_aliases={n_in + i: n_out + t for i, t in plan.aliases.items()},
        compiler_params=_params(len(grid)),
    )(*args, *plan.inputs)
    return outs[:n_out], outs[n_out:]


def _row_tile(R):
    return 256 if R % 256 == 0 else R


def cols_from_shards(a):
    n, R, C = a.shape
    tr = _row_tile(R)

    def body(i_ref, o_ref):
        for s in range(n):
            o_ref[:, s * C : (s + 1) * C] = i_ref[s]

    return pl.pallas_call(
        body,
        name="cols_from_shards",
        grid=(R // tr,),
        in_specs=[pl.BlockSpec((n, tr, C), lambda i: (0, i, 0))],
        out_specs=pl.BlockSpec((tr, n * C), lambda i: (i, 0)),
        out_shape=jax.ShapeDtypeStruct((R, n * C), a.dtype),
        compiler_params=_params(1),
    )(a)


def shards_from_cols(a):
    R, W = a.shape
    C = W // N_DEV
    tr = _row_tile(R)

    def body(i_ref, o_ref):
        for s in range(N_DEV):
            o_ref[s] = i_ref[:, s * C : (s + 1) * C]

    return pl.pallas_call(
        body,
        name="shards_from_cols",
        grid=(R // tr,),
        in_specs=[pl.BlockSpec((tr, W), lambda i: (i, 0))],
        out_specs=pl.BlockSpec((N_DEV, tr, C), lambda i: (0, i, 0)),
        out_shape=jax.ShapeDtypeStruct((N_DEV, R, C), a.dtype),
        compiler_params=_params(1),
    )(a)


def _shift_down(u, prev8, row, n):
    out = pltpu.roll(u, n, 0)
    for k in range(n):
        out = jnp.where(row == k, prev8[8 - n + k : 8 - n + k + 1, :], out)
    return out


def _shift_up(u, next8, row, n, tm):
    out = pltpu.roll(u, tm - n, 0)
    for k in range(n):
        out = jnp.where(row == tm - n + k, next8[k : k + 1, :], out)
    return out


def conv_fwd(x, gain, w_in, cw, w_out, *, seq, tm, plan=None):
    T, D = x.shape
    tps = seq // tm

    def body(x_ref, g_ref, win_ref, cw_ref, wout_ref, x1_ref, bcx_ref, y_ref, z_ref, carry_ref):
        i = pl.program_id(0)

        @pl.when(i % tps == 0)
        def _():
            carry_ref[...] = jnp.zeros_like(carry_ref)

        xt = x_ref[...]
        h = ((xt * _rms(xt)) * g_ref[...]).astype(BF16)
        bcx = _dot(h, win_ref[...])
        bcx_ref[...] = bcx
        b, c, xv = bcx[:, :D], bcx[:, D : 2 * D], bcx[:, 2 * D :]
        u = b * xv
        row = lax.broadcasted_iota(jnp.int32, u.shape, 0)
        prev = carry_ref[...]
        u1 = _shift_down(u, prev, row, 1)
        u2 = _shift_down(u, prev, row, 2)
        carry_ref[...] = u[tm - 8 :, :]
        cwv = cw_ref[...]
        y = cwv[0:1, :] * u2 + cwv[1:2, :] * u1 + cwv[2:3, :] * u
        y_ref[...] = y
        z = (c * y).astype(BF16)
        z_ref[...] = z
        x1_ref[...] = xt + _dot(z, wout_ref[...])

    tile = pl.BlockSpec((tm, D), lambda i: (i, 0))
    return _call(
        body,
        plan=plan,
        args=(x, gain, w_in, cw, w_out),
        name="conv_fwd",
        grid=(T // tm,),
        in_specs=[
            pl.BlockSpec((tm, D), lambda i: (i, 0)),
            _resident((1, D)),
            _resident((D, 3 * D)),
            _resident((3, D)),
            _resident((D, D)),
        ],
        out_specs=[tile, pl.BlockSpec((tm, 3 * D), lambda i: (i, 0)), tile, tile],
        out_shape=[jax.ShapeDtypeStruct((T, D), F32), jax.ShapeDtypeStruct((T, 3 * D), F32),
                   jax.ShapeDtypeStruct((T, D), F32), jax.ShapeDtypeStruct((T, D), BF16)],
        scratch=[pltpu.VMEM((8, D), F32)],
    )


def conv_bwd(dx1, x, gain, bcx, y, cw, w_in, w_out, *, seq, tm, plan=None):
    T, D = x.shape
    n = T // tm
    tps = seq // tm

    def body(d_ref, x_ref, g_ref, bcx_ref, y_ref, cw_ref, win_ref, wout_ref,
             gx_ref, dbcx_ref, h_ref, d16_ref, dcw_ref, dg_ref, carry_ref):
        i = pl.program_id(0)
        t = n - 1 - i

        @pl.when(i == 0)
        def _():
            dcw_ref[...] = jnp.zeros_like(dcw_ref)
            dg_ref[...] = jnp.zeros_like(dg_ref)

        @pl.when(t % tps == tps - 1)
        def _():
            carry_ref[...] = jnp.zeros_like(carry_ref)

        d = d_ref[...]
        d16 = d.astype(BF16)
        d16_ref[...] = d16
        dz = _dot_nt(d16, wout_ref[...])
        bcx = bcx_ref[...]
        b, c, xv = bcx[:, :D], bcx[:, D : 2 * D], bcx[:, 2 * D :]
        u = b * xv
        row = lax.broadcasted_iota(jnp.int32, u.shape, 0)
        cwv = cw_ref[...]
        dc = dz * y_ref[...]
        dy = dz * c
        nxt = carry_ref[...]
        dy1 = _shift_up(dy, nxt, row, 1, tm)
        dy2 = _shift_up(dy, nxt, row, 2, tm)
        carry_ref[...] = dy[0:8, :]
        dcw_ref[0:1, :] += jnp.sum(dy2 * u, axis=0, keepdims=True)
        dcw_ref[1:2, :] += jnp.sum(dy1 * u, axis=0, keepdims=True)
        dcw_ref[2:3, :] += jnp.sum(dy * u, axis=0, keepdims=True)
        du = cwv[2:3, :] * dy + cwv[1:2, :] * dy1 + cwv[0:1, :] * dy2
        dbcx_ref[:, :D] = (du * xv).astype(BF16)
        dbcx_ref[:, D : 2 * D] = dc.astype(BF16)
        dbcx_ref[:, 2 * D :] = (du * b).astype(BF16)
        dh = _dot_nt(dbcx_ref[...], win_ref[...])
        xt = x_ref[...]
        r = _rms(xt)
        gn = g_ref[...]
        h_ref[...] = ((xt * r) * gn).astype(BF16)
        dx, dgn = _rms_bwd(xt, r, gn, dh)
        dg_ref[0:1, :] += dgn
        gx_ref[...] = d + dx

    rev = lambda i: (n - 1 - i, 0)
    return _call(
        body,
        plan=plan,
        args=(dx1, x, gain, bcx, y, cw, w_in, w_out),
        name="conv_bwd",
        grid=(n,),
        in_specs=[
            pl.BlockSpec((tm, D), rev),
            pl.BlockSpec((tm, D), rev),
            _resident((1, D)),
            pl.BlockSpec((tm, 3 * D), rev),
            pl.BlockSpec((tm, D), rev),
            _resident((3, D)),
            _resident((D, 3 * D)),
            _resident((D, D)),
        ],
        out_specs=[
            pl.BlockSpec((tm, D), rev),
            pl.BlockSpec((tm, 3 * D), rev),
            pl.BlockSpec((tm, D), rev),
            pl.BlockSpec((tm, D), rev),
            pl.BlockSpec((8, D), lambda i: (0, 0)),
            pl.BlockSpec((8, D), lambda i: (0, 0)),
        ],
        out_shape=[
            jax.ShapeDtypeStruct((T, D), F32),
            jax.ShapeDtypeStruct((T, 3 * D), BF16),
            jax.ShapeDtypeStruct((T, D), BF16),
            jax.ShapeDtypeStruct((T, D), BF16),
            jax.ShapeDtypeStruct((8, D), F32),
            jax.ShapeDtypeStruct((8, D), F32),
        ],
        scratch=[pltpu.VMEM((8, D), F32)],
    )


MXU_TILE = 256
FFN_CHUNK = 4 * MXU_TILE


def _sigmoid(g):
    return 1.0 / (1.0 + jnp.exp(-g))


def _ffn_chunks(F):
    assert F % MXU_TILE == 0
    return [(s, min(FFN_CHUNK, F - s)) for s in range(0, F, FFN_CHUNK)]


def ffn_fwd(x, gain, w_gu, w_d, *, tm, plan=None, attn=None, target=None):
    T, D = x.shape
    F = w_d.shape[0]
    row = lambda i: (i, 0)
    tile = pl.BlockSpec((tm, D), row)

    def body(*refs):
        refs = list(refs)
        x_ref, g_ref, wgu_ref, wd_ref = refs[:4]
        del refs[:4]
        if attn is not None:
            ao_ref, wo_ref = refs[:2]
            del refs[:2]
        if target is not None:
            t_ref = refs.pop(0)
        if attn is not None:
            xin_ref = refs.pop(0)
        xo_ref, gu_ref = refs[:2]
        xt = x_ref[...]
        if attn is not None:
            xt = xt + _dot(ao_ref[...], wo_ref[...])
            xin_ref[...] = xt
        h = ((xt * _rms(xt)) * g_ref[...]).astype(BF16)
        acc = xt
        for s, n in _ffn_chunks(F):
            g = _dot(h, wgu_ref[:, s : s + n])
            u = _dot(h, wgu_ref[:, F + s : F + s + n])
            gu_ref[:, s : s + n] = g
            gu_ref[:, F + s : F + s + n] = u
            a = ((g * _sigmoid(g)) * u).astype(BF16)
            acc = acc + _dot(a, wd_ref[s : s + n, :])
        if target is None:
            xo_ref[...] = acc
        else:
            s_ref = refs[2]

            @pl.when(pl.program_id(0) == 0)
            def _():
                s_ref[...] = jnp.zeros_like(s_ref)

            e = acc - t_ref[...]
            xo_ref[...] = e * (1.0 / D)
            s_ref[...] += jnp.sum(jnp.sum(e * e, axis=-1, keepdims=True), axis=0, keepdims=True)

    args = [x, gain, w_gu, w_d]
    in_specs = [tile, _resident((1, D)), _resident((D, 2 * F)), _resident((F, D))]
    out_specs = [tile, pl.BlockSpec((tm, 2 * F), row)]
    out_shape = [jax.ShapeDtypeStruct((T, D), F32), jax.ShapeDtypeStruct((T, 2 * F), F32)]
    if attn is not None:
        args += list(attn)
        in_specs += [pl.BlockSpec((tm, attn[0].shape[1]), row), _resident(attn[1].shape)]
        out_specs.insert(0, tile)
        out_shape.insert(0, jax.ShapeDtypeStruct((T, D), F32))
    if target is not None:
        args.append(target)
        in_specs.append(tile)
        out_specs.append(pl.BlockSpec((8, LANES), lambda i: (0, 0)))
        out_shape.append(jax.ShapeDtypeStruct((8, LANES), F32))
    return _call(body, plan=plan, args=args, name="ffn_fwd", grid=(T // tm,), in_specs=in_specs, out_specs=out_specs,
                 out_shape=out_shape)


def ffn_bwd(dxo, x, gain, gu, w_gu, w_d, *, tm, plan=None, w_o=None):
    T, D = x.shape
    F = w_d.shape[0]

    def body(d_ref, x_ref, g_ref, gu_ref, wgu_ref, wd_ref, *rest):
        if w_o is not None:
            wo_ref, rest = rest[0], rest[1:]
        dx_ref, a_ref, dgu_ref, h_ref, d16_ref, dg_ref = rest[:6]

        @pl.when(pl.program_id(0) == 0)
        def _():
            dg_ref[...] = jnp.zeros_like(dg_ref)

        d = d_ref[...]
        d16 = d.astype(BF16)
        d16_ref[...] = d16
        dh = jnp.zeros((tm, D), F32)
        for c0, n in _ffn_chunks(F):
            g = gu_ref[:, c0 : c0 + n]
            u = gu_ref[:, F + c0 : F + c0 + n]
            da = _dot_nt(d16, wd_ref[c0 : c0 + n, :])
            s = _sigmoid(g)
            sg = g * s
            a_ref[:, c0 : c0 + n] = (sg * u).astype(BF16)
            dg16 = (da * u * (s + sg * (1.0 - s))).astype(BF16)
            du16 = (da * sg).astype(BF16)
            dgu_ref[:, c0 : c0 + n] = dg16
            dgu_ref[:, F + c0 : F + c0 + n] = du16
            dh = dh + _dot_nt(dg16, wgu_ref[:, c0 : c0 + n]) + _dot_nt(du16, wgu_ref[:, F + c0 : F + c0 + n])
        xt = x_ref[...]
        r = _rms(xt)
        gn = g_ref[...]
        h_ref[...] = ((xt * r) * gn).astype(BF16)
        dx, dgn = _rms_bwd(xt, r, gn, dh)
        dg_ref[0:1, :] += dgn
        dxi = d + dx
        dx_ref[...] = dxi
        if w_o is not None:
            dxi16_ref, dao_ref = rest[6:8]
            dxi16 = dxi.astype(BF16)
            dxi16_ref[...] = dxi16
            dao_ref[...] = _dot_nt(dxi16, wo_ref[...]).astype(BF16)

    tile = pl.BlockSpec((tm, D), lambda i: (i, 0))
    args = [dxo, x, gain, gu, w_gu, w_d]
    wide = lambda n: pl.BlockSpec((tm, n), lambda i: (i, 0))
    in_specs = [tile, tile, _resident((1, D)), wide(2 * F), _resident((D, 2 * F)), _resident((F, D))]
    out_specs = [tile, wide(F), wide(2 * F), tile, tile, pl.BlockSpec((8, D), lambda i: (0, 0))]
    out_shape = [
        jax.ShapeDtypeStruct((T, D), F32),
        jax.ShapeDtypeStruct((T, F), BF16),
        jax.ShapeDtypeStruct((T, 2 * F), BF16),
        jax.ShapeDtypeStruct((T, D), BF16),
        jax.ShapeDtypeStruct((T, D), BF16),
        jax.ShapeDtypeStruct((8, D), F32),
    ]
    if w_o is not None:
        args.append(w_o)
        in_specs.append(_resident(w_o.shape))
        out_specs += [tile, pl.BlockSpec((tm, w_o.shape[0]), lambda i: (i, 0))]
        out_shape += [jax.ShapeDtypeStruct((T, D), BF16), jax.ShapeDtypeStruct((T, w_o.shape[0]), BF16)]
    return _call(body, plan=plan, args=args, name="ffn_bwd", grid=(T // tm,), in_specs=in_specs, out_specs=out_specs,
                 out_shape=out_shape)


def wgrad(a, b, *, name, a_cols=0, b_cols=0, group=1, flat=False, tk, out_dtype=BF16, plan=None):
    T, K = a.shape
    J = 1
    if a_cols:
        K = a_cols
        J = a.shape[1] // K
        a_spec = pl.BlockSpec((tk, K), lambda j, k: (k, j))
    else:
        a_spec = pl.BlockSpec((tk, K), lambda j, k: (k, 0))
    if b_cols:
        N = b_cols * group
        J = b.shape[1] // N
        b_spec = pl.BlockSpec((tk, N), lambda j, k: (k, j))
    else:
        N = b.shape[1]
        b_spec = pl.BlockSpec((tk, N), lambda j, k: (k, 0))
    nk = T // tk
    if flat:
        o_spec, o_shape = pl.BlockSpec((K, N), lambda j, k: (0, j)), (K, J * N)
    elif group > 1:
        o_spec, o_shape = pl.BlockSpec((group, K, b_cols), lambda j, k: (j, 0, 0)), (J * group, K, b_cols)
    else:
        o_spec, o_shape = pl.BlockSpec((None, K, N), lambda j, k: (j, 0, 0)), (J, K, N)

    def body(a_ref, b_ref, o_ref, acc_ref):
        k = pl.program_id(1)

        @pl.when(k == 0)
        def _():
            acc_ref[...] = jnp.zeros_like(acc_ref)

        acc_ref[...] += _dot_tn(a_ref[...], b_ref[...])

        @pl.when(k == nk - 1)
        def _():
            if group > 1 and not flat:
                for i in range(group):
                    o_ref[i] = acc_ref[:, i * b_cols : (i + 1) * b_cols].astype(out_dtype)
            else:
                o_ref[...] = acc_ref[...].astype(out_dtype)

    outs, sent = _call(
        body,
        plan=plan,
        args=(a, b),
        name=name,
        grid=(J, nk),
        in_specs=[a_spec, b_spec],
        out_specs=[o_spec],
        out_shape=[jax.ShapeDtypeStruct(o_shape, out_dtype)],
        scratch=[pltpu.VMEM((K, N), F32)],
    )
    return outs[0], sent


def _seg(xs, lo):
    s_lo = [jnp.sum(jnp.where(lo, x, 0.0), axis=-1, keepdims=True) for x in xs]
    s_hi = [jnp.sum(jnp.where(lo, 0.0, x), axis=-1, keepdims=True) for x in xs]
    return [jnp.where(lo, a, b) for a, b in zip(s_lo, s_hi)]


def _head_norm(xs, gains, lo):
    rs = [lax.rsqrt(s * (1.0 / HEAD_DIM) + EPS) for s in _seg([x * x for x in xs], lo)]
    return [(x * r) * g for x, r, g in zip(xs, rs, gains)], rs


def _head_norm_bwd(xs, rs, gains, dys, lo):
    xns = [x * r for x, r in zip(xs, rs)]
    dxns = [dy * g for dy, g in zip(dys, gains)]
    means = [s * (1.0 / HEAD_DIM) for s in _seg([a * b for a, b in zip(dxns, xns)], lo)]
    dxs = [r * (dxn - xn * m) for r, dxn, xn, m in zip(rs, dxns, xns, means)]
    return dxs, [jnp.sum(dy * xn, axis=0, keepdims=True) for dy, xn in zip(dys, xns)]


def _swap_halves(x):
    return pltpu.roll(x, HEAD_DIM, 1)


def qkv_proj(x, gain, w, qg, kg, *, tm):
    T, D = x.shape
    N = w.shape[1]
    kvw = N_KV_HEADS * HEAD_DIM
    nqt, nkt = D // LANES, kvw // LANES

    def body(x_ref, g_ref, w_ref, qg_ref, kg_ref, qkv_ref, q_ref, kd_ref, vd_ref):
        xt = x_ref[...]
        h = ((xt * _rms(xt)) * g_ref[...]).astype(BF16)
        qkv = _dot(h, w_ref[...])
        qkv_ref[...] = qkv
        lo = lax.broadcasted_iota(jnp.int32, (1, LANES), 1) < HEAD_DIM
        tiles = [qkv[:, t * LANES : (t + 1) * LANES] for t in range(nqt + nkt)]
        normed, _ = _head_norm(tiles, [qg_ref[...]] * nqt + [kg_ref[...]] * nkt, lo)
        for t in range(nqt):
            q_ref[:, t * LANES : (t + 1) * LANES] = (normed[t] * SCALE).astype(BF16)
        for t in range(nkt):
            kn = normed[nqt + t]
            v = qkv[:, D + kvw + t * LANES : D + kvw + (t + 1) * LANES]
            for src, dst in ((kn, kd_ref), (v, vd_ref)):
                sw = _swap_halves(src)
                dst[:, 2 * t * LANES : (2 * t + 1) * LANES] = jnp.where(lo, src, sw).astype(BF16)
                dst[:, (2 * t + 1) * LANES : (2 * t + 2) * LANES] = jnp.where(lo, sw, src).astype(BF16)

    row = lambda i: (i, 0)
    return pl.pallas_call(
        body,
        name="qkv_proj",
        grid=(T // tm,),
        in_specs=[pl.BlockSpec((tm, D), row), _resident((1, D)), _resident((D, N)), _resident((1, LANES)), _resident((1, LANES))],
        out_specs=[pl.BlockSpec((tm, N), row), pl.BlockSpec((tm, D), row), pl.BlockSpec((tm, 2 * kvw), row), pl.BlockSpec((tm, 2 * kvw), row)],
        out_shape=[
            jax.ShapeDtypeStruct((T, N), F32),
            jax.ShapeDtypeStruct((T, D), BF16),
            jax.ShapeDtypeStruct((T, 2 * kvw), BF16),
            jax.ShapeDtypeStruct((T, 2 * kvw), BF16),
        ],
        compiler_params=_params(1),
    )(x, gain, w, qg, kg)


def _attn_tables(sinks, n_q_heads):
    P = n_q_heads // N_KV_HEADS // 2
    h = jnp.arange(1, n_q_heads + 1, dtype=F32)
    slopes = jnp.exp2(-8.0 * h / n_q_heads).reshape(N_KV_HEADS, P, 1, 2, 1)
    qi = jnp.arange(BLOCK)[:, None]
    kj = jnp.arange(BLOCK)[None, :]
    dist = jnp.where(kj <= qi, qi - kj, qi + BLOCK - kj).astype(F32)
    shape = (N_KV_HEADS, P, BLOCK, 2, BLOCK)
    bias = jnp.broadcast_to(-slopes * dist[None, None, :, None, :], shape)
    sink = jnp.broadcast_to(sinks.astype(F32).reshape(N_KV_HEADS, P, 1, 2, 1), shape)
    return bias.reshape(N_KV_HEADS, P * BLOCK, 2 * BLOCK), sink.reshape(N_KV_HEADS, P * BLOCK, 2 * BLOCK)


def _attn_specs(D, nb):
    kvw2 = 2 * N_KV_HEADS * HEAD_DIM
    cur = lambda b, i: (b * nb + i, 0)
    prev = lambda b, i: (jnp.maximum(b * nb + i - 1, 0), 0)
    return [
        pl.BlockSpec((BLOCK, D), cur),
        pl.BlockSpec((BLOCK, kvw2), cur),
        pl.BlockSpec((BLOCK, kvw2), prev),
        pl.BlockSpec((BLOCK, kvw2), cur),
        pl.BlockSpec((BLOCK, kvw2), prev),
    ]


def _attn_operands(kh, P, lo, q_ref, kc_ref, kp_ref, vc_ref, vp_ref):
    sl = slice(kh * LANES, (kh + 1) * LANES)

    def cat(prev_ref, cur_ref):
        d = jnp.concatenate([prev_ref[:, sl], cur_ref[:, sl]], axis=0)
        z = jnp.zeros_like(d)
        return jnp.concatenate([jnp.where(lo, d, z), jnp.where(lo, z, d)], axis=0)

    qt = jnp.concatenate([q_ref[:, (kh * P + pr) * LANES : (kh * P + pr + 1) * LANES] for pr in range(P)], axis=0)
    return qt, cat(kp_ref, kc_ref), cat(vp_ref, vc_ref)


def _attn_exp(s_all, bias, sink, tri, first):
    out = []
    for par in range(2):
        c0 = 2 * par * BLOCK
        s = jnp.where(tri, s_all[:, c0 + BLOCK : c0 + 2 * BLOCK], jnp.where(first, NEG, s_all[:, c0 : c0 + BLOCK]))
        s = s + bias[:, par * BLOCK : (par + 1) * BLOCK]
        snk = sink[:, par * BLOCK : (par + 1) * BLOCK]
        m = jnp.maximum(jnp.max(s, axis=-1, keepdims=True), snk)
        out.append((jnp.exp(s - m), jnp.exp(snk - m)))
    return out


def _unfold(x, tri):
    z = jnp.zeros_like(x)
    return jnp.concatenate([jnp.where(tri, z, x), jnp.where(tri, x, z)], axis=1)


def _attn_masks(R):
    lane = lax.broadcasted_iota(jnp.int32, (1, LANES), 1)
    row = lax.broadcasted_iota(jnp.int32, (R, BLOCK), 0) & (BLOCK - 1)
    col = lax.broadcasted_iota(jnp.int32, (R, BLOCK), 1)
    return lane, lane < HEAD_DIM, col <= row


def attn_fwd(q16, kd, vd, bias, sink, *, seq, n_seq):
    T, D = q16.shape
    nb = seq // BLOCK
    P = D // HEAD_DIM // N_KV_HEADS // 2
    R = P * BLOCK
    KV = range(N_KV_HEADS)

    def body(q_ref, kc_ref, kp_ref, vc_ref, vp_ref, bias_ref, sink_ref, o_ref):
        first = pl.program_id(1) == 0
        _, lo, tri = _attn_masks(R)
        r4 = lax.broadcasted_iota(jnp.int32, (4 * BLOCK, LANES), 0)
        l4 = lax.broadcasted_iota(jnp.int32, (4 * BLOCK, LANES), 1)
        ones = ((r4 < 2 * BLOCK) == (l4 < HEAD_DIM)).astype(BF16)
        ops = [_attn_operands(kh, P, lo, q_ref, kc_ref, kp_ref, vc_ref, vp_ref) for kh in KV]
        s_all = [_dot_nt(ops[kh][0], ops[kh][1]) for kh in KV]
        ex = [_attn_exp(s_all[kh], bias_ref[kh], sink_ref[kh], tri, first) for kh in KV]
        lhs = [jnp.concatenate([_unfold(e, tri) for e, _ in ex[kh]], axis=1).astype(BF16) for kh in KV]
        o = [_dot(lhs[kh], ops[kh][2]) for kh in KV]
        den = [_dot(lhs[kh], ones) for kh in KV]
        for kh in KV:
            out = o[kh] / (den[kh] + jnp.where(lo, ex[kh][0][1], ex[kh][1][1]))
            for pr in range(P):
                t = kh * P + pr
                o_ref[:, t * LANES : (t + 1) * LANES] = out[pr * BLOCK : (pr + 1) * BLOCK, :].astype(BF16)

    return pl.pallas_call(
        body,
        name="attn_fwd",
        grid=(n_seq, nb),
        in_specs=_attn_specs(D, nb) + [_resident((N_KV_HEADS, R, 2 * BLOCK)), _resident((N_KV_HEADS, R, 2 * BLOCK))],
        out_specs=pl.BlockSpec((BLOCK, D), lambda b, i: (b * nb + i, 0)),
        out_shape=jax.ShapeDtypeStruct((T, D), BF16),
        compiler_params=_params(2),
    )(q16, kd, kd, vd, vd, bias, sink)


def attn_bwd(q16, kd, vd, do, bias, sink, *, seq, n_seq):
    T, D = q16.shape
    kvw2 = 2 * N_KV_HEADS * HEAD_DIM
    nb = seq // BLOCK
    G = D // HEAD_DIM // N_KV_HEADS
    P = G // 2
    R = P * BLOCK
    KV = range(N_KV_HEADS)

    def body(q_ref, kc_ref, kp_ref, vc_ref, vp_ref, do_ref, bias_ref, sink_ref,
             dq_ref, dkc_ref, dkp_ref, dvc_ref, dvp_ref, dsink_ref):
        first = pl.program_id(1) == 0

        @pl.when(jnp.logical_and(pl.program_id(0) == 0, first))
        def _():
            dsink_ref[...] = jnp.zeros_like(dsink_ref)

        lane, lo, tri = _attn_masks(R)
        r4 = lax.broadcasted_iota(jnp.int32, (4 * BLOCK, 2 * BLOCK), 0)
        c4 = lax.broadcasted_iota(jnp.int32, (4 * BLOCK, 2 * BLOCK), 1)
        ones = ((r4 < 2 * BLOCK) == (c4 < BLOCK)).astype(BF16)
        ops = [_attn_operands(kh, P, lo, q_ref, kc_ref, kp_ref, vc_ref, vp_ref) for kh in KV]
        do16 = [jnp.concatenate([do_ref[:, (kh * P + pr) * LANES : (kh * P + pr + 1) * LANES] for pr in range(P)], axis=0)
                for kh in KV]
        s_all = [_dot_nt(ops[kh][0], ops[kh][1]) for kh in KV]
        dp_all = [_dot_nt(do16[kh], ops[kh][2]) for kh in KV]
        ex = [_attn_exp(s_all[kh], bias_ref[kh], sink_ref[kh], tri, first) for kh in KV]
        den = [_dot(jnp.concatenate([_unfold(e, tri) for e, _ in ex[kh]], axis=1).astype(BF16), ones) for kh in KV]
        dsink = jnp.zeros((1, LANES), F32)
        pf, dsf = [], []
        for kh in KV:
            ps_, ds_ = [], []
            for par in range(2):
                e, es = ex[kh][par]
                inv = 1.0 / (den[kh][:, par * BLOCK : (par + 1) * BLOCK] + es)
                p = e * inv
                c0 = 2 * par * BLOCK
                dp = jnp.where(tri, dp_all[kh][:, c0 + BLOCK : c0 + 2 * BLOCK], dp_all[kh][:, c0 : c0 + BLOCK])
                delta = jnp.sum(p * dp, axis=-1, keepdims=True)
                ds_.append(_unfold(p * (dp - delta), tri))
                ps_.append(_unfold(p, tri))
                dsr = -((es * inv) * delta)
                for pr in range(P):
                    hq = kh * G + 2 * pr + par
                    tot = jnp.sum(dsr[pr * BLOCK : (pr + 1) * BLOCK, :], axis=0, keepdims=True)
                    dsink = dsink + jnp.where(lane == hq, tot, 0.0)
            pf.append(jnp.concatenate(ps_, axis=1).astype(BF16))
            dsf.append(jnp.concatenate(ds_, axis=1).astype(BF16))
        dq = [_dot(dsf[kh], ops[kh][1]) for kh in KV]
        dk = [_dot_tn(dsf[kh], ops[kh][0]) for kh in KV]
        dv = [_dot_tn(pf[kh], do16[kh]) for kh in KV]
        dsink_ref[0:1, :] += dsink
        for kh in KV:
            sl = slice(kh * LANES, (kh + 1) * LANES)
            for pr in range(P):
                t = kh * P + pr
                dq_ref[:, t * LANES : (t + 1) * LANES] = dq[kh][pr * BLOCK : (pr + 1) * BLOCK, :]
            for full, prev_ref, cur_ref in ((dk[kh], dkp_ref, dkc_ref), (dv[kh], dvp_ref, dvc_ref)):
                dup = jnp.where(lo, full[: 2 * BLOCK, :], full[2 * BLOCK :, :])
                prev_ref[:, sl] = dup[:BLOCK, :]
                cur_ref[:, sl] = dup[BLOCK:, :]

    cur = lambda b, i: (b * nb + i, 0)
    kv_spec = pl.BlockSpec((BLOCK, kvw2), cur)
    kv_shape = jax.ShapeDtypeStruct((T, kvw2), F32)
    return pl.pallas_call(
        body,
        name="attn_bwd",
        grid=(n_seq, nb),
        in_specs=_attn_specs(D, nb)
        + [pl.BlockSpec((BLOCK, D), cur), _resident((N_KV_HEADS, R, 2 * BLOCK)), _resident((N_KV_HEADS, R, 2 * BLOCK))],
        out_specs=[pl.BlockSpec((BLOCK, D), cur), kv_spec, kv_spec, kv_spec, kv_spec, pl.BlockSpec((8, LANES), lambda b, i: (0, 0))],
        out_shape=[jax.ShapeDtypeStruct((T, D), F32), kv_shape, kv_shape, kv_shape, kv_shape, jax.ShapeDtypeStruct((8, LANES), F32)],
        compiler_params=_params(2),
    )(q16, kd, kd, vd, vd, do, bias, sink)


def qkv_bwd(dq, dkc, dkp, dvc, dvp, qkv, dres, x, gain, w_qkv, qg, kg, *, seq):
    T, D = x.shape
    kvw2 = dkc.shape[1]
    kvw = kvw2 // 2
    nqt, nkt = D // LANES, kvw // LANES
    nb = seq // BLOCK
    tm = 2 * BLOCK
    n = T // tm

    def body(dq_ref, dkc_ref, dkpa_ref, dkpb_ref, dvc_ref, dvpa_ref, dvpb_ref, qkv_ref, dres_ref, x_ref, g_ref, w_ref,
             qg_ref, kg_ref, dx_ref, dqkv_ref, h_ref, dg_ref, hg_ref):
        i = pl.program_id(0)

        @pl.when(i == 0)
        def _():
            dg_ref[...] = jnp.zeros_like(dg_ref)
            hg_ref[...] = jnp.zeros_like(hg_ref)

        lo = lax.broadcasted_iota(jnp.int32, (1, LANES), 1) < HEAD_DIM
        last = (2 * i + 1) % nb == nb - 1
        dkd = dkc_ref[...] + jnp.concatenate([dkpa_ref[...], jnp.where(last, 0.0, dkpb_ref[...])], axis=0)
        dvd = dvc_ref[...] + jnp.concatenate([dvpa_ref[...], jnp.where(last, 0.0, dvpb_ref[...])], axis=0)

        def undup(d, t):
            a, b = d[:, 2 * t * LANES : (2 * t + 1) * LANES], d[:, (2 * t + 1) * LANES : (2 * t + 2) * LANES]
            return jnp.where(lo, a + _swap_halves(a), b + _swap_halves(b))

        tiles = [qkv_ref[:, t * LANES : (t + 1) * LANES] for t in range(nqt + nkt)]
        gains = [qg_ref[...]] * nqt + [kg_ref[...]] * nkt
        dys = [dq_ref[:, t * LANES : (t + 1) * LANES] * SCALE for t in range(nqt)] + [undup(dkd, t) for t in range(nkt)]
        _, rs = _head_norm(tiles, gains, lo)
        dxs, dgs = _head_norm_bwd(tiles, rs, gains, dys, lo)
        for t in range(nqt + nkt):
            dqkv_ref[:, t * LANES : (t + 1) * LANES] = dxs[t].astype(BF16)
        for t in range(nkt):
            dqkv_ref[:, D + kvw + t * LANES : D + kvw + (t + 1) * LANES] = undup(dvd, t).astype(BF16)
        hg_ref[0:1, :] += functools.reduce(lambda a, b: a + b, dgs[:nqt])
        hg_ref[1:2, :] += functools.reduce(lambda a, b: a + b, dgs[nqt:])
        dh = _dot_nt(dqkv_ref[...], w_ref[...])
        xt = x_ref[...]
        r = _rms(xt)
        gn = g_ref[...]
        h_ref[...] = ((xt * r) * gn).astype(BF16)
        dx, dgn = _rms_bwd(xt, r, gn, dh)
        dg_ref[0:1, :] += dgn
        dx_ref[...] = dres_ref[...] + dx

    row = lambda i: (i, 0)
    nxt_a = pl.BlockSpec((BLOCK, kvw2), lambda i: (2 * i + 1, 0))
    nxt_b = pl.BlockSpec((BLOCK, kvw2), lambda i: (jnp.minimum(2 * i + 2, 2 * n - 1), 0))
    return pl.pallas_call(
        body,
        name="qkv_bwd",
        grid=(n,),
        in_specs=[
            pl.BlockSpec((tm, D), row),
            pl.BlockSpec((tm, kvw2), row),
            nxt_a,
            nxt_b,
            pl.BlockSpec((tm, kvw2), row),
            nxt_a,
            nxt_b,
            pl.BlockSpec((tm, D + kvw2), row),
            pl.BlockSpec((tm, D), row),
            pl.BlockSpec((tm, D), row),
            _resident((1, D)),
            _resident((D, D + kvw2)),
            _resident((1, LANES)),
            _resident((1, LANES)),
        ],
        out_specs=[
            pl.BlockSpec((tm, D), row),
            pl.BlockSpec((tm, D + kvw2), row),
            pl.BlockSpec((tm, D), row),
            pl.BlockSpec((8, D), lambda i: (0, 0)),
            pl.BlockSpec((8, LANES), lambda i: (0, 0)),
        ],
        out_shape=[
            jax.ShapeDtypeStruct((T, D), F32),
            jax.ShapeDtypeStruct((T, D + kvw2), BF16),
            jax.ShapeDtypeStruct((T, D), BF16),
            jax.ShapeDtypeStruct((8, D), F32),
            jax.ShapeDtypeStruct((8, LANES), F32),
        ],
        compiler_params=_params(1),
    )(dq, dkc, dkp, dkp, dvc, dvp, dvp, qkv, dres, x, gain, w_qkv, qg, kg)


def local_step(x, target, gains, w, *, seq, tm=256, tm_ffn=256, tk=2048, shards=None):
    T, D = x.shape
    n_seq = T // seq
    nm, nf, qgain, kgain, sinks = gains
    H = D // HEAD_DIM
    tk, tk_long = min(tk, T), min(2 * tk, T)
    qg2, kg2 = jnp.tile(qgain, (1, 2)), jnp.tile(kgain, (1, 2))
    bias, sinkcol = _attn_tables(sinks, H)

    dist = shards is not None
    w = dict(w)

    plan = _Gather([shards["w_gu"][0], shards["w_d"][0]]) if dist else None
    (x1, bcx, y_conv, z16), got = conv_fwd(x, nm[0:1], w["w_in"], w["cw"], w["w_out"], seq=seq, tm=tm, plan=plan)
    if dist:
        w["w_gu"], w["w_d"] = [cols_from_shards(got[0]), None], [got[1].reshape(-1, D), None]
    plan = _Gather([shards["w_qkv"], shards["w_o"], shards["w_gu"][1], shards["w_d"][1]]) if dist else None
    (x2, gu0), got = ffn_fwd(x1, nf[0:1], w["w_gu"][0], w["w_d"][0], tm=tm_ffn, plan=plan)
    if dist:
        w["w_qkv"], w["w_o"] = cols_from_shards(got[0]), got[1].reshape(D, D)
        w["w_gu"][1], w["w_d"][1] = cols_from_shards(got[2]), got[3].reshape(-1, D)
    qkv, q16, kd, vd = qkv_proj(x2, nm[1:2], w["w_qkv"], qg2, kg2, tm=tm)
    ao = attn_fwd(q16, kd, vd, bias, sinkcol, seq=seq, n_seq=n_seq)
    (x3, dx4, gu1, sse), _ = ffn_fwd(x2, nf[1:2], w["w_gu"][1], w["w_d"][1], tm=tm_ffn, attn=(ao, w["w_o"]), target=target)

    by_dest = lambda a: a.reshape(N_DEV, -1, a.shape[-1])
    fresh = lambda a, *lead: jax.ShapeDtypeStruct((N_DEV,) + lead + a.shape[1:], a.dtype)
    gu_cols = 2 * MXU_TILE

    (dx3, a16, dgu, h16, d16, dnf1, dx3_16, dao), _ = ffn_bwd(
        dx4, x3, nf[1:2], gu1, w["w_gu"][1], w["w_d"][1], tm=tm, w_o=w["w_o"])
    g_gu1 = shards_from_cols(wgrad(h16, dgu, name="wgrad_gu1", b_cols=gu_cols, flat=True, tk=tk_long)[0])
    g_d1 = by_dest(wgrad(a16, d16, name="wgrad_d1", a_cols=a16.shape[1] // 2, tk=tk)[0])
    g_o = by_dest(wgrad(ao, dx3_16, name="wgrad_o", tk=tk)[0])
    dq, dkc, dkp, dvc, dvp, dsinks = attn_bwd(q16, kd, vd, dao, bias, sinkcol, seq=seq, n_seq=n_seq)
    dx2, dqkv16, h16, dnm1, dgains = qkv_bwd(dq, dkc, dkp, dvc, dvp, qkv, dx3, x2, nm[1:2], w["w_qkv"], qg2, kg2, seq=seq)
    g_qkv = shards_from_cols(wgrad(h16, dqkv16, name="wgrad_qkv", tk=tk)[0][0])

    plan = _Scatter([(g_gu1, fresh(g_gu1, 2), 1, False), (g_d1, fresh(g_d1, 2), 1, False)]) if dist else None
    (dx1, a16, dgu, h16, d16, dnf0), land_ffn = ffn_bwd(dx2, x1, nf[0:1], gu0, w["w_gu"][0], w["w_d"][0], tm=tm, plan=plan)
    plan = _Scatter([(g_o, fresh(g_o), None, False), (g_qkv, fresh(g_qkv), None, False)]) if dist else None
    g_gu0, land_attn = wgrad(h16, dgu, name="wgrad_gu0", b_cols=gu_cols, flat=True, tk=tk_long, plan=plan)
    g_gu0 = shards_from_cols(g_gu0)
    g_d0 = by_dest(wgrad(a16, d16, name="wgrad_d0", a_cols=a16.shape[1] // 2, tk=tk)[0])
    plan = _Scatter([(g_gu0, land_ffn[0], 0, False)]) if dist else None
    (gx, dbcx, h16, d16, dcw, dnm0), land_gu = conv_bwd(
        dx1, x, nm[0:1], bcx, y_conv, w["cw"], w["w_in"], w["w_out"], seq=seq, tm=tm, plan=plan)
    g_out = by_dest(wgrad(z16, d16, name="wgrad_out", tk=tk)[0])
    g_cw = dcw[0:3].reshape(3, N_DEV, D // N_DEV).transpose(1, 0, 2)
    plan = _Scatter([(g_d0, land_ffn[1], 0, False), (g_out, fresh(g_out), None, False), (g_cw, fresh(g_cw), None, False)]) if dist else None
    g_in, land_last = wgrad(h16, dbcx, name="wgrad_in", b_cols=3 * D // N_DEV, group=2, tk=tk_long, plan=plan)
    g = dict(w_in=g_in, cw=g_cw, w_out=g_out)
    if dist:
        g.update(w_gu=land_gu[0], w_d=land_last[0], w_out=land_last[1], cw=land_last[2], w_o=land_attn[0], w_qkv=land_attn[1])
    else:
        g.update(w_gu0=g_gu0, w_gu1=g_gu1, w_d0=g_d0, w_d1=g_d1, w_o=g_o, w_qkv=g_qkv)
    small = dict(nm0=dnm0, nm1=dnm1, nf0=dnf0, nf1=dnf1, gains=dgains, sinks=dsinks)
    return sse, gx, g, small


def _adamw_math(g, w, m, v):
    m = ADAM_B1 * m + (1.0 - ADAM_B1) * g
    v = ADAM_B2 * v + (1.0 - ADAM_B2) * (g * g)
    m_hat = m / (1.0 - ADAM_B1 ** ADAM_STEP)
    v_hat = v / (1.0 - ADAM_B2 ** ADAM_STEP)
    delta = -ADAM_LR * (m_hat / (jnp.sqrt(v_hat) + ADAM_EPS) + ADAM_WD * w)
    return delta, m, v


def adamw(parts, w, m, v, *, name):
    n, R, C = parts.shape
    tr = R
    for cand in (256, 128, 88, 64, 32, 16, 8):
        if R > cand and R % cand == 0:
            tr = cand
            break

    def body(p_ref, w_ref, m_ref, v_ref, g_ref, d_ref, mo_ref, vo_ref):
        g = p_ref[0].astype(F32)
        for s in range(1, n):
            g = g + p_ref[s].astype(F32)
        g_ref[...] = g
        d_ref[...], mo_ref[...], vo_ref[...] = _adamw_math(g, w_ref[...], m_ref[...], v_ref[...])

    blk = pl.BlockSpec((tr, C), lambda i: (i, 0))
    return pl.pallas_call(
        body,
        name=name,
        grid=(R // tr,),
        in_specs=[pl.BlockSpec((n, tr, C), lambda i: (0, i, 0)), blk, blk, blk],
        out_specs=[blk] * 4,
        out_shape=[jax.ShapeDtypeStruct((R, C), F32)] * 4,
        compiler_params=_params(1),
    )(parts, w, m, v)


def pack_small(small, sse, D):
    W = max(D, 2 * LANES)

    def body(nm0, nm1, nf0, nf1, gains, sinks, sse_ref, o_ref):
        o_ref[...] = jnp.zeros_like(o_ref)
        o_ref[0:1, :D] = nm0[0:1, :]
        o_ref[1:2, :D] = nm1[0:1, :]
        o_ref[2:3, :D] = nf0[0:1, :]
        o_ref[3:4, :D] = nf1[0:1, :]
        gq = gains[0:1, :] + pltpu.roll(gains[0:1, :], HEAD_DIM, 1)
        gk = gains[1:2, :] + pltpu.roll(gains[1:2, :], HEAD_DIM, 1)
        lane = lax.broadcasted_iota(jnp.int32, (1, LANES), 1)
        o_ref[4:5, :LANES] = jnp.where(lane < HEAD_DIM, gq, gk)
        o_ref[4:5, LANES : 2 * LANES] = sinks[0:1, :]
        o_ref[5:6, :LANES] = sse_ref[0:1, :] * (0.5 / D)

    return pl.pallas_call(
        body,
        name="pack_small",
        out_shape=jax.ShapeDtypeStruct((8, W), F32),
    )(small["nm0"], small["nm1"], small["nf0"], small["nf1"], small["gains"], small["sinks"], sse)


def _pack_small_params(nm, nf, qg, kg, sk, D):
    W = max(D, 2 * LANES)
    row4 = jnp.concatenate([qg.reshape(-1), kg.reshape(-1), jnp.zeros((LANES - 2 * HEAD_DIM,), F32), sk.reshape(-1)])
    row4 = jnp.pad(row4, (0, W - row4.shape[0]))
    rows = [jnp.pad(r, (0, W - D)) for r in (nm[0], nm[1], nf[0], nf[1])] + [row4]
    return jnp.concatenate([jnp.stack(rows), jnp.zeros((3, W), F32)], axis=0)


def _unpack_small(a, D, H):
    nm = a[0:2, :D]
    nf = a[2:4, :D]
    qg = a[4:5, 0:HEAD_DIM]
    kg = a[4:5, HEAD_DIM : 2 * HEAD_DIM]
    sk = a[4:5, LANES : LANES + H]
    return qg, kg, sk, nm, nf


def kernel(x, conv_w_in, conv_w, conv_w_out, attn_w_qkv, attn_q_gain, attn_k_gain, attn_sinks, attn_w_o, norm_mixer, norm_ffn, ffn_w_gate_up, ffn_w_down, loss_target, m_conv_w_in, m_conv_w, m_conv_w_out, m_attn_w_qkv, m_attn_q_gain, m_attn_k_gain, m_attn_sinks, m_attn_w_o, m_norm_mixer, m_norm_ffn, m_ffn_w_gate_up, m_ffn_w_down, v_conv_w_in, v_conv_w, v_conv_w_out, v_attn_w_qkv, v_attn_q_gain, v_attn_k_gain, v_attn_sinks, v_attn_w_o, v_norm_mixer, v_norm_ffn, v_ffn_w_gate_up, v_ffn_w_down):
    n_seq, seq, D = x.shape
    T = n_seq * seq
    H = D // HEAD_DIM
    L = ffn_w_gate_up.shape[0]

    full = run_plan(_Gather([conv_w_in[0].astype(BF16), conv_w[0], conv_w_out[0].astype(BF16)]), name="gather_conv_weights")
    w = dict(w_in=cols_from_shards(full[0]), cw=full[1].transpose(1, 0, 2).reshape(3, D),
             w_out=full[2].reshape(D, D))
    shards = dict(w_gu=[ffn_w_gate_up[l].astype(BF16) for l in range(L)], w_d=[ffn_w_down[l].astype(BF16) for l in range(L)],
                  w_qkv=attn_w_qkv[0].astype(BF16), w_o=attn_w_o[0].astype(BF16))
    gains = (norm_mixer, norm_ffn, attn_q_gain, attn_k_gain, attn_sinks)
    sse, gx, g, small = local_step(x.reshape(T, D), loss_target.reshape(T, D), gains, w, seq=seq, shards=shards)

    packed = pack_small(small, sse, D)
    items = [(g["w_in"], jax.ShapeDtypeStruct(g["w_in"].shape, BF16), None, False),
             (packed, jax.ShapeDtypeStruct((N_DEV,) + packed.shape, packed.dtype), None, True)]
    land_in, small_all = run_plan(_Scatter(items), name="exchange_last_grads")
    bufs = [land_in, g["cw"], g["w_out"], g["w_qkv"], g["w_o"], g["w_gu"], g["w_d"]]

    def flat(a):
        return a.reshape(-1, a.shape[-1])

    big = [conv_w_in, conv_w, conv_w_out, attn_w_qkv, attn_w_o, ffn_w_gate_up, ffn_w_down]
    big_m = [m_conv_w_in, m_conv_w, m_conv_w_out, m_attn_w_qkv, m_attn_w_o, m_ffn_w_gate_up, m_ffn_w_down]
    big_v = [v_conv_w_in, v_conv_w, v_conv_w_out, v_attn_w_qkv, v_attn_w_o, v_ffn_w_gate_up, v_ffn_w_down]
    tags = ["in", "cw", "out", "qkv", "o", "gu", "d"]
    res = []
    for b in range(7):
        parts = bufs[b].reshape(N_DEV, -1, bufs[b].shape[-1])
        outs = adamw(parts, flat(big[b]), flat(big_m[b]), flat(big_v[b]), name="adamw_" + tags[b])
        res.append([o.reshape(big[b].shape) for o in outs])
    sw = _pack_small_params(norm_mixer, norm_ffn, attn_q_gain, attn_k_gain, attn_sinks, D)
    sm = _pack_small_params(m_norm_mixer, m_norm_ffn, m_attn_q_gain, m_attn_k_gain, m_attn_sinks, D)
    sv = _pack_small_params(v_norm_mixer, v_norm_ffn, v_attn_q_gain, v_attn_k_gain, v_attn_sinks, D)
    souts = adamw(small_all, sw, sm, sv, name="adamw_small")
    sres = [_unpack_small(o, D, H) for o in souts]
    loss = souts[0][5, 0]

    def ordered(i):
        r, s = [r[i] for r in res], sres[i]
        return [r[0], r[1], r[2], r[3], s[0], s[1], s[2], r[4], s[3], s[4], r[5], r[6]]

    return (loss, gx.reshape(n_seq, seq, D), *ordered(0), *ordered(1), *ordered(2), *ordered(3))
```

```python
import functools
import math

import jax
import jax.numpy as jnp
from jax import lax
from jax.experimental import pallas as pl
from jax.experimental.pallas import tpu as pltpu

F32 = jnp.float32
BF16 = jnp.bfloat16

EPS = 1e-6
HEAD_DIM = 64
N_KV_HEADS = 4
BLOCK = 128
LANES = 128
N_DEV = 8
NEG = -1e30
SCALE = 1.0 / math.sqrt(HEAD_DIM)

ADAM_LR = 0.001
ADAM_B1 = 0.9
ADAM_B2 = 0.999
ADAM_EPS = 1e-08
ADAM_WD = 0.01
ADAM_STEP = 10

V7X_VMEM_BYTES = 64 * 1024 * 1024
VMEM_LIMIT = V7X_VMEM_BYTES - 8 * 1024 * 1024
MESH = pl.DeviceIdType.MESH

_NT = (((1,), (1,)), ((), ()))
_TN = (((0,), (0,)), ((), ()))


def _params(n_grid):
    return pltpu.CompilerParams(dimension_semantics=("arbitrary",) * n_grid, vmem_limit_bytes=VMEM_LIMIT)


def _resident(shape):
    nd = len(shape)
    return pl.BlockSpec(shape, lambda *_: (0,) * nd, pipeline_mode=pl.Buffered(1))


def _rms(x):
    return lax.rsqrt(jnp.mean(x * x, axis=-1, keepdims=True) + EPS)


def _rms_bwd(x, r, gain, dh):
    xn = x * r
    dxn = dh * gain
    dx = r * (dxn - xn * jnp.mean(dxn * xn, axis=-1, keepdims=True))
    return dx, jnp.sum(dh * xn, axis=0, keepdims=True)


def _dot(a, b):
    return jnp.dot(a, b, preferred_element_type=F32)


def _dot_nt(a, b):
    return lax.dot_general(a, b, _NT, preferred_element_type=F32)


def _dot_tn(a, b):
    return lax.dot_general(a, b, _TN, preferred_element_type=F32)


def _place():
    return lax.axis_index("x"), lax.axis_index("y"), lax.axis_index("c")


def _flip(v, bit):
    return 1 - v if bit else v


def _slot(px, py, pc):
    return 4 * px + 2 * py + pc


class _Gather:
    def __init__(self, shards):
        nt = len(shards)
        self.nt = nt
        self.inputs = list(shards)
        self.out_shapes = [jax.ShapeDtypeStruct((N_DEV,) + s.shape, s.dtype) for s in shards]
        self.scratch = [pltpu.SemaphoreType.DMA((nt, 7)), pltpu.SemaphoreType.DMA((nt, 7)), pltpu.SemaphoreType.DMA((nt,))]
        self.aliases = {}

    def phases(self, total):
        assert total >= 3
        return [(0, self.start), (total - 2, self.forward), (total - 1, self.finish)]

    def _copies(self, ins, outs, sems):
        send_sems, recv_sems, loc_sems = sems
        x, y, c = _place()
        me = _slot(x, y, c)
        sib = (x, y, 1 - c)
        chips = [(_flip(x, k >> 1), _flip(y, k & 1)) for k in (1, 2, 3)]

        def copy(t, k, src, dst_slot, to):
            return pltpu.make_async_remote_copy(
                src_ref=src, dst_ref=outs[t].at[dst_slot], send_sem=send_sems.at[t, k], recv_sem=recv_sems.at[t, k],
                device_id=to, device_id_type=MESH)

        local = [pltpu.make_async_copy(ins[t], outs[t].at[me], loc_sems.at[t]) for t in range(self.nt)]
        first, passed, arrive_ici, arrive_sib = [], [], [], []
        for t in range(self.nt):
            first.append(copy(t, 0, ins[t], me, sib))
            s = _slot(x, y, 1 - c)
            arrive_sib.append(copy(t, 0, outs[t].at[s], s, sib))
            for j, (px, py) in enumerate(chips):
                first.append(copy(t, 1 + j, ins[t], me, (px, py, c)))
                s = _slot(px, py, c)
                arrive_ici.append(copy(t, 1 + j, outs[t].at[s], s, sib))
                passed.append(copy(t, 4 + j, outs[t].at[s], s, sib))
                s = _slot(px, py, 1 - c)
                arrive_sib.append(copy(t, 4 + j, outs[t].at[s], s, sib))
        return local, first, passed, arrive_ici, arrive_sib

    def start(self, ins, outs, sems):
        local, first, _, _, _ = self._copies(ins, outs, sems)
        for cp in local + first:
            cp.start()

    def forward(self, ins, outs, sems):
        _, _, passed, arrive_ici, _ = self._copies(ins, outs, sems)
        for arrival, fwd in zip(arrive_ici, passed):
            arrival.wait_recv()
            fwd.start()

    def finish(self, ins, outs, sems):
        local, first, passed, _, arrive_sib = self._copies(ins, outs, sems)
        for cp in arrive_sib:
            cp.wait_recv()
        for cp in first + passed:
            cp.wait_send()
        for cp in local:
            cp.wait()


class _Scatter:
    def __init__(self, items):
        self.items = items
        nt = len(items)
        self.nt = nt
        reused = [(t, it[1]) for t, it in enumerate(items) if not isinstance(it[1], jax.ShapeDtypeStruct)]
        self.inputs = [it[0] for it in items] + [land for _, land in reused]
        self.out_shapes = [jax.ShapeDtypeStruct(it[1].shape, it[1].dtype) for it in items]
        self.aliases = {nt + i: t for i, (t, _) in enumerate(reused)}
        self.scratch = [pltpu.SemaphoreType.DMA((nt, 7)), pltpu.SemaphoreType.DMA((nt, 7)), pltpu.SemaphoreType.DMA((nt,))]

    def phases(self, total):
        assert total >= 2
        return [(0, self.start), (total - 1, self.finish)]

    def _copies(self, ins, outs, sems):
        send_sems, recv_sems, loc_sems = sems
        x, y, c = _place()
        me = _slot(x, y, c)

        def land(t, s):
            layer = self.items[t][2]
            return outs[t].at[s] if layer is None else outs[t].at[s, layer]

        def src(t, s):
            return ins[t] if self.items[t][3] else ins[t].at[s]

        local = [pltpu.make_async_copy(src(t, me), land(t, me), loc_sems.at[t]) for t in range(self.nt)]
        sends, arrivals = [], []
        for k in range(1, N_DEV):
            px, py, pc = _flip(x, (k >> 2) & 1), _flip(y, (k >> 1) & 1), _flip(c, k & 1)
            peer = _slot(px, py, pc)
            for t in range(self.nt):
                for dst, into in ((land(t, me), sends), (land(t, peer), arrivals)):
                    into.append(pltpu.make_async_remote_copy(
                        src_ref=src(t, peer), dst_ref=dst, send_sem=send_sems.at[t, k - 1], recv_sem=recv_sems.at[t, k - 1],
                        device_id=(px, py, pc), device_id_type=MESH))
        return local, sends, arrivals

    def start(self, ins, outs, sems):
        local, sends, _ = self._copies(ins, outs, sems)
        for cp in local + sends:
            cp.start()

    def finish(self, ins, outs, sems):
        local, sends, arrivals = self._copies(ins, outs, sems)
        for cp in arrivals:
            cp.wait_recv()
        for cp in sends:
            cp.wait_send()
        for cp in local:
            cp.wait()


def _any_specs(n):
    return [pl.BlockSpec(memory_space=pl.ANY)] * n


def run_plan(plan, *, name):
    def body(*refs):
        n_in, n_out = len(plan.inputs), len(plan.out_shapes)
        ins, outs, sems = refs[:n_in], refs[n_in : n_in + n_out], refs[n_in + n_out :]
        for _, phase in plan.phases(3):
            phase(ins, outs, sems)

    return pl.pallas_call(
        body,
        name=name,
        in_specs=_any_specs(len(plan.inputs)),
        out_specs=_any_specs(len(plan.out_shapes)),
        out_shape=plan.out_shapes,
        scratch_shapes=plan.scratch,
        input_output_aliases=plan.aliases,
    )(*plan.inputs)


_HBM = pl.BlockSpec(memory_space=pltpu.HBM)
_SEM = pl.BlockSpec(memory_space=pltpu.SEMAPHORE)
_DATAFLOW = pltpu.SideEffectType.DATAFLOW_SIDE_EFFECTING


def start_exchange(items, *, name):
    nt = len(items)

    def body(*refs):
        ins, lands, sems, token = refs[:nt], refs[nt : 2 * nt], refs[2 * nt : 4 * nt], refs[-1]
        x, y, c = _place()
        me = _slot(x, y, c)
        for k in range(1, N_DEV):
            px, py, pc = _flip(x, (k >> 2) & 1), _flip(y, (k >> 1) & 1), _flip(c, k & 1)
            for t, (_, whole) in enumerate(items):
                pltpu.make_async_remote_copy(
                    src_ref=ins[t] if whole else ins[t].at[_slot(px, py, pc)], dst_ref=lands[t].at[me],
                    send_sem=sems[2 * t], recv_sem=sems[2 * t + 1], device_id=(px, py, pc), device_id_type=MESH).start()
        token[...] = jnp.zeros_like(token)

    bufs = [pltpu.with_memory_space_constraint(a, pltpu.HBM) for a, _ in items]
    for a, whole in items:
        shape = (N_DEV,) + (a.shape if whole else a.shape[1:])
        bufs.append(pltpu.with_memory_space_constraint(lax.empty(shape, a.dtype), pltpu.HBM))
    outs = pl.pallas_call(
        body,
        name=name,
        in_specs=[_HBM] * (2 * nt),
        out_specs=[_SEM] * (2 * nt) + [_HBM] * (2 * nt) + [pl.BlockSpec(memory_space=pltpu.VMEM)],
        out_shape=[pltpu.SemaphoreType.DMA(())] * (2 * nt) + [pltpu.HBM(b.shape, b.dtype) for b in bufs]
        + [jax.ShapeDtypeStruct((8, LANES), F32)],
        input_output_aliases={i: 2 * nt + i for i in range(2 * nt)},
        compiler_params=pltpu.CompilerParams(has_side_effects=_DATAFLOW),
    )(*bufs)
    return outs[: 2 * nt], outs[2 * nt : 3 * nt], outs[3 * nt : 4 * nt], outs[-1]


def wait_exchange(sems, arrays, lands, after, *, name):
    nt = len(arrays)

    def body(*refs):
        zones, sem_refs = refs[nt : 2 * nt], refs[2 * nt : 4 * nt]
        x, y, c = _place()
        for t in range(nt):
            seven = zones[t].at[pl.ds(0, N_DEV - 1)]
            pltpu.make_async_remote_copy(
                src_ref=seven, dst_ref=seven, send_sem=sem_refs[2 * t], recv_sem=sem_refs[2 * t + 1],
                device_id=(x, y, c), device_id_type=MESH).wait()

    bufs = list(arrays) + list(lands)
    outs = pl.pallas_call(
        body,
        name=name,
        in_specs=[_HBM] * (2 * nt) + [_SEM] * (2 * nt) + _any_specs(len(after)),
        out_specs=[_HBM] * (2 * nt),
        out_shape=[pltpu.HBM(b.shape, b.dtype) for b in bufs],
        input_output_aliases={i: i for i in range(2 * nt)},
        compiler_params=pltpu.CompilerParams(has_side_effects=_DATAFLOW),
    )(*bufs, *sems, *after)
    return outs[nt:]


def _call(body, *, name, grid, in_specs, out_specs, out_shape, args, scratch=(), plan=None):
    n_in, n_out, n_scr = len(in_specs), len(out_specs), len(scratch)
    if plan is None:
        outs = pl.pallas_call(
            body, name=name, grid=grid, in_specs=in_specs, out_specs=out_specs, out_shape=out_shape,
            scratch_shapes=list(scratch), compiler_params=_params(len(grid)))(*args)
        return outs, None
    c_in, c_out = len(plan.inputs), len(plan.out_shapes)
    phases = plan.phases(math.prod(grid))

    def full(*refs):
        a, refs = refs[:n_in], refs[n_in:]
        ci, refs = refs[:c_in], refs[c_in:]
        o, refs = refs[:n_out], refs[n_out:]
        co, refs = refs[:c_out], refs[c_out:]
        s, cs = refs[:n_scr], refs[n_scr:]
        step = pl.program_id(0)
        for d in range(1, len(grid)):
            step = step * grid[d] + pl.program_id(d)
        for at, phase in phases:
            if at == 0:
                pl.when(step == 0)(functools.partial(phase, ci, co, cs))
        body(*a, *o, *s)
        for at, phase in phases:
            if at > 0:
                pl.when(step == at)(functools.partial(phase, ci, co, cs))

    outs = pl.pallas_call(
        full,
        name=name,
        grid=grid,
        in_specs=list(in_specs) + _any_specs(c_in),
        out_specs=list(out_specs) + _any_specs(c_out),
        out_shape=list(out_shape) + plan.out_shapes,
        scratch_shapes=list(scratch) + plan.scratch,
        input_output_aliases={n_in + i: n_out + t for i, t in plan.aliases.items()},
        compiler_params=_params(len(grid)),
    )(*args, *plan.inputs)
    return outs[:n_out], outs[n_out:]


def _row_tile(R):
    return 256 if R % 256 == 0 else R


def cols_from_shards(a):
    n, R, C = a.shape
    tr = _row_tile(R)

    def body(i_ref, o_ref):
        for s in range(n):
            o_ref[:, s * C : (s + 1) * C] = i_ref[s]

    return pl.pallas_call(
        body,
        name="cols_from_shards",
        grid=(R // tr,),
        in_specs=[pl.BlockSpec((n, tr, C), lambda i: (0, i, 0))],
        out_specs=pl.BlockSpec((tr, n * C), lambda i: (i, 0)),
        out_shape=jax.ShapeDtypeStruct((R, n * C), a.dtype),
        compiler_params=_params(1),
    )(a)


def shards_from_cols(a):
    R, W = a.shape
    C = W // N_DEV
    tr = _row_tile(R)

    def body(i_ref, o_ref):
        for s in range(N_DEV):
            o_ref[s] = i_ref[:, s * C : (s + 1) * C]

    return pl.pallas_call(
        body,
        name="shards_from_cols",
        grid=(R // tr,),
        in_specs=[pl.BlockSpec((tr, W), lambda i: (i, 0))],
        out_specs=pl.BlockSpec((N_DEV, tr, C), lambda i: (0, i, 0)),
        out_shape=jax.ShapeDtypeStruct((N_DEV, R, C), a.dtype),
        compiler_params=_params(1),
    )(a)


def _shift_down(u, prev8, row, n):
    out = pltpu.roll(u, n, 0)
    for k in range(n):
        out = jnp.where(row == k, prev8[8 - n + k : 8 - n + k + 1, :], out)
    return out


def _shift_up(u, next8, row, n, tm):
    out = pltpu.roll(u, tm - n, 0)
    for k in range(n):
        out = jnp.where(row == tm - n + k, next8[k : k + 1, :], out)
    return out


def conv_fwd(x, gain, w_in, cw, w_out, *, seq, tm, plan=None):
    T, D = x.shape
    tps = seq // tm

    def body(x_ref, g_ref, win_ref, cw_ref, wout_ref, x1_ref, bcx_ref, y_ref, z_ref, carry_ref):
        i = pl.program_id(0)

        @pl.when(i % tps == 0)
        def _():
            carry_ref[...] = jnp.zeros_like(carry_ref)

        xt = x_ref[...]
        h = ((xt * _rms(xt)) * g_ref[...]).astype(BF16)
        bcx = _dot(h, win_ref[...])
        bcx_ref[...] = bcx
        b, c, xv = bcx[:, :D], bcx[:, D : 2 * D], bcx[:, 2 * D :]
        u = b * xv
        row = lax.broadcasted_iota(jnp.int32, u.shape, 0)
        prev = carry_ref[...]
        u1 = _shift_down(u, prev, row, 1)
        u2 = _shift_down(u, prev, row, 2)
        carry_ref[...] = u[tm - 8 :, :]
        cwv = cw_ref[...]
        y = cwv[0:1, :] * u2 + cwv[1:2, :] * u1 + cwv[2:3, :] * u
        y_ref[...] = y
        z = (c * y).astype(BF16)
        z_ref[...] = z
        x1_ref[...] = xt + _dot(z, wout_ref[...])

    tile = pl.BlockSpec((tm, D), lambda i: (i, 0))
    return _call(
        body,
        plan=plan,
        args=(x, gain, w_in, cw, w_out),
        name="conv_fwd",
        grid=(T // tm,),
        in_specs=[
            pl.BlockSpec((tm, D), lambda i: (i, 0)),
            _resident((1, D)),
            _resident((D, 3 * D)),
            _resident((3, D)),
            _resident((D, D)),
        ],
        out_specs=[tile, pl.BlockSpec((tm, 3 * D), lambda i: (i, 0)), tile, tile],
        out_shape=[jax.ShapeDtypeStruct((T, D), F32), jax.ShapeDtypeStruct((T, 3 * D), F32),
                   jax.ShapeDtypeStruct((T, D), F32), jax.ShapeDtypeStruct((T, D), BF16)],
        scratch=[pltpu.VMEM((8, D), F32)],
    )


def conv_bwd(dx1, x, gain, bcx, y, cw, w_in, w_out, *, seq, tm, plan=None):
    T, D = x.shape
    n = T // tm
    tps = seq // tm

    def body(d_ref, x_ref, g_ref, bcx_ref, y_ref, cw_ref, win_ref, wout_ref,
             gx_ref, dbcx_ref, h_ref, d16_ref, dcw_ref, dg_ref, carry_ref):
        i = pl.program_id(0)
        t = n - 1 - i

        @pl.when(i == 0)
        def _():
            dcw_ref[...] = jnp.zeros_like(dcw_ref)
            dg_ref[...] = jnp.zeros_like(dg_ref)

        @pl.when(t % tps == tps - 1)
        def _():
            carry_ref[...] = jnp.zeros_like(carry_ref)

        d = d_ref[...]
        d16 = d.astype(BF16)
        d16_ref[...] = d16
        dz = _dot_nt(d16, wout_ref[...])
        bcx = bcx_ref[...]
        b, c, xv = bcx[:, :D], bcx[:, D : 2 * D], bcx[:, 2 * D :]
        u = b * xv
        row = lax.broadcasted_iota(jnp.int32, u.shape, 0)
        cwv = cw_ref[...]
        dc = dz * y_ref[...]
        dy = dz * c
        nxt = carry_ref[...]
        dy1 = _shift_up(dy, nxt, row, 1, tm)
        dy2 = _shift_up(dy, nxt, row, 2, tm)
        carry_ref[...] = dy[0:8, :]
        dcw_ref[0:1, :] += jnp.sum(dy2 * u, axis=0, keepdims=True)
        dcw_ref[1:2, :] += jnp.sum(dy1 * u, axis=0, keepdims=True)
        dcw_ref[2:3, :] += jnp.sum(dy * u, axis=0, keepdims=True)
        du = cwv[2:3, :] * dy + cwv[1:2, :] * dy1 + cwv[0:1, :] * dy2
        dbcx_ref[:, :D] = (du * xv).astype(BF16)
        dbcx_ref[:, D : 2 * D] = dc.astype(BF16)
        dbcx_ref[:, 2 * D :] = (du * b).astype(BF16)
        dh = _dot_nt(dbcx_ref[...], win_ref[...])
        xt = x_ref[...]
        r = _rms(xt)
        gn = g_ref[...]
        h_ref[...] = ((xt * r) * gn).astype(BF16)
        dx, dgn = _rms_bwd(xt, r, gn, dh)
        dg_ref[0:1, :] += dgn
        gx_ref[...] = d + dx

    rev = lambda i: (n - 1 - i, 0)
    return _call(
        body,
        plan=plan,
        args=(dx1, x, gain, bcx, y, cw, w_in, w_out),
        name="conv_bwd",
        grid=(n,),
        in_specs=[
            pl.BlockSpec((tm, D), rev),
            pl.BlockSpec((tm, D), rev),
            _resident((1, D)),
            pl.BlockSpec((tm, 3 * D), rev),
            pl.BlockSpec((tm, D), rev),
            _resident((3, D)),
            _resident((D, 3 * D)),
            _resident((D, D)),
        ],
        out_specs=[
            pl.BlockSpec((tm, D), rev),
            pl.BlockSpec((tm, 3 * D), rev),
            pl.BlockSpec((tm, D), rev),
            pl.BlockSpec((tm, D), rev),
            pl.BlockSpec((8, D), lambda i: (0, 0)),
            pl.BlockSpec((8, D), lambda i: (0, 0)),
        ],
        out_shape=[
            jax.ShapeDtypeStruct((T, D), F32),
            jax.ShapeDtypeStruct((T, 3 * D), BF16),
            jax.ShapeDtypeStruct((T, D), BF16),
            jax.ShapeDtypeStruct((T, D), BF16),
            jax.ShapeDtypeStruct((8, D), F32),
            jax.ShapeDtypeStruct((8, D), F32),
        ],
        scratch=[pltpu.VMEM((8, D), F32)],
    )


MXU_TILE = 256
FFN_CHUNK = 4 * MXU_TILE


def _sigmoid(g):
    return 1.0 / (1.0 + jnp.exp(-g))


def _ffn_chunks(F):
    assert F % MXU_TILE == 0
    return [(s, min(FFN_CHUNK, F - s)) for s in range(0, F, FFN_CHUNK)]


def ffn_fwd(x, gain, w_gu, w_d, *, tm, plan=None, attn=None, target=None):
    T, D = x.shape
    F = w_d.shape[0]
    row = lambda i: (i, 0)
    tile = pl.BlockSpec((tm, D), row)

    def body(*refs):
        refs = list(refs)
        x_ref, g_ref, wgu_ref, wd_ref = refs[:4]
        del refs[:4]
        if attn is not None:
            ao_ref, wo_ref = refs[:2]
            del refs[:2]
        if target is not None:
            t_ref = refs.pop(0)
        if attn is not None:
            xin_ref = refs.pop(0)
        xo_ref, gu_ref = refs[:2]
        xt = x_ref[...]
        if attn is not None:
            xt = xt + _dot(ao_ref[...], wo_ref[...])
            xin_ref[...] = xt
        h = ((xt * _rms(xt)) * g_ref[...]).astype(BF16)
        acc = xt
        for s, n in _ffn_chunks(F):
            g = _dot(h, wgu_ref[:, s : s + n])
            u = _dot(h, wgu_ref[:, F + s : F + s + n])
            gu_ref[:, s : s + n] = g
            gu_ref[:, F + s : F + s + n] = u
            a = ((g * _sigmoid(g)) * u).astype(BF16)
            acc = acc + _dot(a, wd_ref[s : s + n, :])
        if target is None:
            xo_ref[...] = acc
        else:
            s_ref = refs[2]

            @pl.when(pl.program_id(0) == 0)
            def _():
                s_ref[...] = jnp.zeros_like(s_ref)

            e = acc - t_ref[...]
            xo_ref[...] = e * (1.0 / D)
            s_ref[...] += jnp.sum(jnp.sum(e * e, axis=-1, keepdims=True), axis=0, keepdims=True)

    args = [x, gain, w_gu, w_d]
    in_specs = [tile, _resident((1, D)), _resident((D, 2 * F)), _resident((F, D))]
    out_specs = [tile, pl.BlockSpec((tm, 2 * F), row)]
    out_shape = [jax.ShapeDtypeStruct((T, D), F32), jax.ShapeDtypeStruct((T, 2 * F), F32)]
    if attn is not None:
        args += list(attn)
        in_specs += [pl.BlockSpec((tm, attn[0].shape[1]), row), _resident(attn[1].shape)]
        out_specs.insert(0, tile)
        out_shape.insert(0, jax.ShapeDtypeStruct((T, D), F32))
    if target is not None:
        args.append(target)
        in_specs.append(tile)
        out_specs.append(pl.BlockSpec((8, LANES), lambda i: (0, 0)))
        out_shape.append(jax.ShapeDtypeStruct((8, LANES), F32))
    return _call(body, plan=plan, args=args, name="ffn_fwd", grid=(T // tm,), in_specs=in_specs, out_specs=out_specs,
                 out_shape=out_shape)


def ffn_bwd(dxo, x, gain, gu, w_gu, w_d, *, tm, plan=None, w_o=None):
    T, D = x.shape
    F = w_d.shape[0]

    def body(d_ref, x_ref, g_ref, gu_ref, wgu_ref, wd_ref, *rest):
        if w_o is not None:
            wo_ref, rest = rest[0], rest[1:]
        dx_ref, a_ref, dgu_ref, h_ref, d16_ref, dg_ref = rest[:6]

        @pl.when(pl.program_id(0) == 0)
        def _():
            dg_ref[...] = jnp.zeros_like(dg_ref)

        d = d_ref[...]
        d16 = d.astype(BF16)
        d16_ref[...] = d16
        dh = jnp.zeros((tm, D), F32)
        for c0, n in _ffn_chunks(F):
            g = gu_ref[:, c0 : c0 + n]
            u = gu_ref[:, F + c0 : F + c0 + n]
            da = _dot_nt(d16, wd_ref[c0 : c0 + n, :])
            s = _sigmoid(g)
            sg = g * s
            a_ref[:, c0 : c0 + n] = (sg * u).astype(BF16)
            dg16 = (da * u * (s + sg * (1.0 - s))).astype(BF16)
            du16 = (da * sg).astype(BF16)
            dgu_ref[:, c0 : c0 + n] = dg16
            dgu_ref[:, F + c0 : F + c0 + n] = du16
            dh = dh + _dot_nt(dg16, wgu_ref[:, c0 : c0 + n]) + _dot_nt(du16, wgu_ref[:, F + c0 : F + c0 + n])
        xt = x_ref[...]
        r = _rms(xt)
        gn = g_ref[...]
        h_ref[...] = ((xt * r) * gn).astype(BF16)
        dx, dgn = _rms_bwd(xt, r, gn, dh)
        dg_ref[0:1, :] += dgn
        dxi = d + dx
        dx_ref[...] = dxi
        if w_o is not None:
            dxi16_ref, dao_ref = rest[6:8]
            dxi16 = dxi.astype(BF16)
            dxi16_ref[...] = dxi16
            dao_ref[...] = _dot_nt(dxi16, wo_ref[...]).astype(BF16)

    tile = pl.BlockSpec((tm, D), lambda i: (i, 0))
    args = [dxo, x, gain, gu, w_gu, w_d]
    wide = lambda n: pl.BlockSpec((tm, n), lambda i: (i, 0))
    in_specs = [tile, tile, _resident((1, D)), wide(2 * F), _resident((D, 2 * F)), _resident((F, D))]
    out_specs = [tile, wide(F), wide(2 * F), tile, tile, pl.BlockSpec((8, D), lambda i: (0, 0))]
    out_shape = [
        jax.ShapeDtypeStruct((T, D), F32),
        jax.ShapeDtypeStruct((T, F), BF16),
        jax.ShapeDtypeStruct((T, 2 * F), BF16),
        jax.ShapeDtypeStruct((T, D), BF16),
        jax.ShapeDtypeStruct((T, D), BF16),
        jax.ShapeDtypeStruct((8, D), F32),
    ]
    if w_o is not None:
        args.append(w_o)
        in_specs.append(_resident(w_o.shape))
        out_specs += [tile, pl.BlockSpec((tm, w_o.shape[0]), lambda i: (i, 0))]
        out_shape += [jax.ShapeDtypeStruct((T, D), BF16), jax.ShapeDtypeStruct((T, w_o.shape[0]), BF16)]
    return _call(body, plan=plan, args=args, name="ffn_bwd", grid=(T // tm,), in_specs=in_specs, out_specs=out_specs,
                 out_shape=out_shape)


def wgrad(a, b, *, name, a_cols=0, b_cols=0, group=1, flat=False, tk, out_dtype=BF16, plan=None):
    T, K = a.shape
    J = 1
    if a_cols:
        K = a_cols
        J = a.shape[1] // K
        a_spec = pl.BlockSpec((tk, K), lambda j, k: (k, j))
    else:
        a_spec = pl.BlockSpec((tk, K), lambda j, k: (k, 0))
    if b_cols:
        N = b_cols * group
        J = b.shape[1] // N
        b_spec = pl.BlockSpec((tk, N), lambda j, k: (k, j))
    else:
        N = b.shape[1]
        b_spec = pl.BlockSpec((tk, N), lambda j, k: (k, 0))
    nk = T // tk
    if flat:
        o_spec, o_shape = pl.BlockSpec((K, N), lambda j, k: (0, j)), (K, J * N)
    elif group > 1:
        o_spec, o_shape = pl.BlockSpec((group, K, b_cols), lambda j, k: (j, 0, 0)), (J * group, K, b_cols)
    else:
        o_spec, o_shape = pl.BlockSpec((None, K, N), lambda j, k: (j, 0, 0)), (J, K, N)

    def body(a_ref, b_ref, o_ref, acc_ref):
        k = pl.program_id(1)

        @pl.when(k == 0)
        def _():
            acc_ref[...] = jnp.zeros_like(acc_ref)

        acc_ref[...] += _dot_tn(a_ref[...], b_ref[...])

        @pl.when(k == nk - 1)
        def _():
            if group > 1 and not flat:
                for i in range(group):
                    o_ref[i] = acc_ref[:, i * b_cols : (i + 1) * b_cols].astype(out_dtype)
            else:
                o_ref[...] = acc_ref[...].astype(out_dtype)

    outs, sent = _call(
        body,
        plan=plan,
        args=(a, b),
        name=name,
        grid=(J, nk),
        in_specs=[a_spec, b_spec],
        out_specs=[o_spec],
        out_shape=[jax.ShapeDtypeStruct(o_shape, out_dtype)],
        scratch=[pltpu.VMEM((K, N), F32)],
    )
    return outs[0], sent


def _seg(xs, lo):
    s_lo = [jnp.sum(jnp.where(lo, x, 0.0), axis=-1, keepdims=True) for x in xs]
    s_hi = [jnp.sum(jnp.where(lo, 0.0, x), axis=-1, keepdims=True) for x in xs]
    return [jnp.where(lo, a, b) for a, b in zip(s_lo, s_hi)]


def _head_norm(xs, gains, lo):
    rs = [lax.rsqrt(s * (1.0 / HEAD_DIM) + EPS) for s in _seg([x * x for x in xs], lo)]
    return [(x * r) * g for x, r, g in zip(xs, rs, gains)], rs


def _head_norm_bwd(xs, rs, gains, dys, lo):
    xns = [x * r for x, r in zip(xs, rs)]
    dxns = [dy * g for dy, g in zip(dys, gains)]
    means = [s * (1.0 / HEAD_DIM) for s in _seg([a * b for a, b in zip(dxns, xns)], lo)]
    dxs = [r * (dxn - xn * m) for r, dxn, xn, m in zip(rs, dxns, xns, means)]
    return dxs, [jnp.sum(dy * xn, axis=0, keepdims=True) for dy, xn in zip(dys, xns)]


def _swap_halves(x):
    return pltpu.roll(x, HEAD_DIM, 1)


def qkv_proj(x, gain, w, qg, kg, *, tm):
    T, D = x.shape
    N = w.shape[1]
    kvw = N_KV_HEADS * HEAD_DIM
    nqt, nkt = D // LANES, kvw // LANES

    def body(x_ref, g_ref, w_ref, qg_ref, kg_ref, qkv_ref, q_ref, kd_ref, vd_ref):
        xt = x_ref[...]
        h = ((xt * _rms(xt)) * g_ref[...]).astype(BF16)
        qkv = _dot(h, w_ref[...])
        qkv_ref[...] = qkv
        lo = lax.broadcasted_iota(jnp.int32, (1, LANES), 1) < HEAD_DIM
        tiles = [qkv[:, t * LANES : (t + 1) * LANES] for t in range(nqt + nkt)]
        normed, _ = _head_norm(tiles, [qg_ref[...]] * nqt + [kg_ref[...]] * nkt, lo)
        for t in range(nqt):
            q_ref[:, t * LANES : (t + 1) * LANES] = (normed[t] * SCALE).astype(BF16)
        for t in range(nkt):
            kn = normed[nqt + t]
            v = qkv[:, D + kvw + t * LANES : D + kvw + (t + 1) * LANES]
            for src, dst in ((kn, kd_ref), (v, vd_ref)):
                sw = _swap_halves(src)
                dst[:, 2 * t * LANES : (2 * t + 1) * LANES] = jnp.where(lo, src, sw).astype(BF16)
                dst[:, (2 * t + 1) * LANES : (2 * t + 2) * LANES] = jnp.where(lo, sw, src).astype(BF16)

    row = lambda i: (i, 0)
    return pl.pallas_call(
        body,
        name="qkv_proj",
        grid=(T // tm,),
        in_specs=[pl.BlockSpec((tm, D), row), _resident((1, D)), _resident((D, N)), _resident((1, LANES)), _resident((1, LANES))],
        out_specs=[pl.BlockSpec((tm, N), row), pl.BlockSpec((tm, D), row), pl.BlockSpec((tm, 2 * kvw), row), pl.BlockSpec((tm, 2 * kvw), row)],
        out_shape=[
            jax.ShapeDtypeStruct((T, N), F32),
            jax.ShapeDtypeStruct((T, D), BF16),
            jax.ShapeDtypeStruct((T, 2 * kvw), BF16),
            jax.ShapeDtypeStruct((T, 2 * kvw), BF16),
        ],
        compiler_params=_params(1),
    )(x, gain, w, qg, kg)


def _attn_tables(sinks, n_q_heads):
    P = n_q_heads // N_KV_HEADS // 2
    h = jnp.arange(1, n_q_heads + 1, dtype=F32)
    slopes = jnp.exp2(-8.0 * h / n_q_heads).reshape(N_KV_HEADS, P, 1, 2, 1)
    qi = jnp.arange(BLOCK)[:, None]
    kj = jnp.arange(BLOCK)[None, :]
    dist = jnp.where(kj <= qi, qi - kj, qi + BLOCK - kj).astype(F32)
    shape = (N_KV_HEADS, P, BLOCK, 2, BLOCK)
    bias = jnp.broadcast_to(-slopes * dist[None, None, :, None, :], shape)
    sink = jnp.broadcast_to(sinks.astype(F32).reshape(N_KV_HEADS, P, 1, 2, 1), shape)
    return bias.reshape(N_KV_HEADS, P * BLOCK, 2 * BLOCK), sink.reshape(N_KV_HEADS, P * BLOCK, 2 * BLOCK)


def _attn_specs(D, nb):
    kvw2 = 2 * N_KV_HEADS * HEAD_DIM
    cur = lambda b, i: (b * nb + i, 0)
    prev = lambda b, i: (jnp.maximum(b * nb + i - 1, 0), 0)
    return [
        pl.BlockSpec((BLOCK, D), cur),
        pl.BlockSpec((BLOCK, kvw2), cur),
        pl.BlockSpec((BLOCK, kvw2), prev),
        pl.BlockSpec((BLOCK, kvw2), cur),
        pl.BlockSpec((BLOCK, kvw2), prev),
    ]


def _attn_operands(kh, P, lo, q_ref, kc_ref, kp_ref, vc_ref, vp_ref):
    sl = slice(kh * LANES, (kh + 1) * LANES)

    def cat(prev_ref, cur_ref):
        d = jnp.concatenate([prev_ref[:, sl], cur_ref[:, sl]], axis=0)
        z = jnp.zeros_like(d)
        return jnp.concatenate([jnp.where(lo, d, z), jnp.where(lo, z, d)], axis=0)

    qt = jnp.concatenate([q_ref[:, (kh * P + pr) * LANES : (kh * P + pr + 1) * LANES] for pr in range(P)], axis=0)
    return qt, cat(kp_ref, kc_ref), cat(vp_ref, vc_ref)


def _attn_exp(s_all, bias, sink, tri, first):
    out = []
    for par in range(2):
        c0 = 2 * par * BLOCK
        s = jnp.where(tri, s_all[:, c0 + BLOCK : c0 + 2 * BLOCK], jnp.where(first, NEG, s_all[:, c0 : c0 + BLOCK]))
        s = s + bias[:, par * BLOCK : (par + 1) * BLOCK]
        snk = sink[:, par * BLOCK : (par + 1) * BLOCK]
        m = jnp.maximum(jnp.max(s, axis=-1, keepdims=True), snk)
        out.append((jnp.exp(s - m), jnp.exp(snk - m)))
    return out


def _unfold(x, tri):
    z = jnp.zeros_like(x)
    return jnp.concatenate([jnp.where(tri, z, x), jnp.where(tri, x, z)], axis=1)


def _attn_masks(R):
    lane = lax.broadcasted_iota(jnp.int32, (1, LANES), 1)
    row = lax.broadcasted_iota(jnp.int32, (R, BLOCK), 0) & (BLOCK - 1)
    col = lax.broadcasted_iota(jnp.int32, (R, BLOCK), 1)
    return lane, lane < HEAD_DIM, col <= row


def attn_fwd(q16, kd, vd, bias, sink, *, seq, n_seq):
    T, D = q16.shape
    nb = seq // BLOCK
    P = D // HEAD_DIM // N_KV_HEADS // 2
    R = P * BLOCK
    KV = range(N_KV_HEADS)

    def body(q_ref, kc_ref, kp_ref, vc_ref, vp_ref, bias_ref, sink_ref, o_ref):
        first = pl.program_id(1) == 0
        _, lo, tri = _attn_masks(R)
        r4 = lax.broadcasted_iota(jnp.int32, (4 * BLOCK, LANES), 0)
        l4 = lax.broadcasted_iota(jnp.int32, (4 * BLOCK, LANES), 1)
        ones = ((r4 < 2 * BLOCK) == (l4 < HEAD_DIM)).astype(BF16)
        ops = [_attn_operands(kh, P, lo, q_ref, kc_ref, kp_ref, vc_ref, vp_ref) for kh in KV]
        s_all = [_dot_nt(ops[kh][0], ops[kh][1]) for kh in KV]
        ex = [_attn_exp(s_all[kh], bias_ref[kh], sink_ref[kh], tri, first) for kh in KV]
        lhs = [jnp.concatenate([_unfold(e, tri) for e, _ in ex[kh]], axis=1).astype(BF16) for kh in KV]
        o = [_dot(lhs[kh], ops[kh][2]) for kh in KV]
        den = [_dot(lhs[kh], ones) for kh in KV]
        for kh in KV:
            out = o[kh] / (den[kh] + jnp.where(lo, ex[kh][0][1], ex[kh][1][1]))
            for pr in range(P):
                t = kh * P + pr
                o_ref[:, t * LANES : (t + 1) * LANES] = out[pr * BLOCK : (pr + 1) * BLOCK, :].astype(BF16)

    return pl.pallas_call(
        body,
        name="attn_fwd",
        grid=(n_seq, nb),
        in_specs=_attn_specs(D, nb) + [_resident((N_KV_HEADS, R, 2 * BLOCK)), _resident((N_KV_HEADS, R, 2 * BLOCK))],
        out_specs=pl.BlockSpec((BLOCK, D), lambda b, i: (b * nb + i, 0)),
        out_shape=jax.ShapeDtypeStruct((T, D), BF16),
        compiler_params=_params(2),
    )(q16, kd, kd, vd, vd, bias, sink)


def attn_bwd(q16, kd, vd, do, bias, sink, *, seq, n_seq):
    T, D = q16.shape
    kvw2 = 2 * N_KV_HEADS * HEAD_DIM
    nb = seq // BLOCK
    G = D // HEAD_DIM // N_KV_HEADS
    P = G // 2
    R = P * BLOCK
    KV = range(N_KV_HEADS)

    def body(q_ref, kc_ref, kp_ref, vc_ref, vp_ref, do_ref, bias_ref, sink_ref,
             dq_ref, dkc_ref, dkp_ref, dvc_ref, dvp_ref, dsink_ref):
        first = pl.program_id(1) == 0

        @pl.when(jnp.logical_and(pl.program_id(0) == 0, first))
        def _():
            dsink_ref[...] = jnp.zeros_like(dsink_ref)

        lane, lo, tri = _attn_masks(R)
        r4 = lax.broadcasted_iota(jnp.int32, (4 * BLOCK, 2 * BLOCK), 0)
        c4 = lax.broadcasted_iota(jnp.int32, (4 * BLOCK, 2 * BLOCK), 1)
        ones = ((r4 < 2 * BLOCK) == (c4 < BLOCK)).astype(BF16)
        ops = [_attn_operands(kh, P, lo, q_ref, kc_ref, kp_ref, vc_ref, vp_ref) for kh in KV]
        do16 = [jnp.concatenate([do_ref[:, (kh * P + pr) * LANES : (kh * P + pr + 1) * LANES] for pr in range(P)], axis=0)
                for kh in KV]
        s_all = [_dot_nt(ops[kh][0], ops[kh][1]) for kh in KV]
        dp_all = [_dot_nt(do16[kh], ops[kh][2]) for kh in KV]
        ex = [_attn_exp(s_all[kh], bias_ref[kh], sink_ref[kh], tri, first) for kh in KV]
        den = [_dot(jnp.concatenate([_unfold(e, tri) for e, _ in ex[kh]], axis=1).astype(BF16), ones) for kh in KV]
        dsink = jnp.zeros((1, LANES), F32)
        pf, dsf = [], []
        for kh in KV:
            ps_, ds_ = [], []
            for par in range(2):
                e, es = ex[kh][par]
                inv = 1.0 / (den[kh][:, par * BLOCK : (par + 1) * BLOCK] + es)
                p = e * inv
                c0 = 2 * par * BLOCK
                dp = jnp.where(tri, dp_all[kh][:, c0 + BLOCK : c0 + 2 * BLOCK], dp_all[kh][:, c0 : c0 + BLOCK])
                delta = jnp.sum(p * dp, axis=-1, keepdims=True)
                ds_.append(_unfold(p * (dp - delta), tri))
                ps_.append(_unfold(p, tri))
                dsr = -((es * inv) * delta)
                for pr in range(P):
                    hq = kh * G + 2 * pr + par
                    tot = jnp.sum(dsr[pr * BLOCK : (pr + 1) * BLOCK, :], axis=0, keepdims=True)
                    dsink = dsink + jnp.where(lane == hq, tot, 0.0)
            pf.append(jnp.concatenate(ps_, axis=1).astype(BF16))
            dsf.append(jnp.concatenate(ds_, axis=1).astype(BF16))
        dq = [_dot(dsf[kh], ops[kh][1]) for kh in KV]
        dk = [_dot_tn(dsf[kh], ops[kh][0]) for kh in KV]
        dv = [_dot_tn(pf[kh], do16[kh]) for kh in KV]
        dsink_ref[0:1, :] += dsink
        for kh in KV:
            sl = slice(kh * LANES, (kh + 1) * LANES)
            for pr in range(P):
                t = kh * P + pr
                dq_ref[:, t * LANES : (t + 1) * LANES] = dq[kh][pr * BLOCK : (pr + 1) * BLOCK, :]
            for full, prev_ref, cur_ref in ((dk[kh], dkp_ref, dkc_ref), (dv[kh], dvp_ref, dvc_ref)):
                dup = jnp.where(lo, full[: 2 * BLOCK, :], full[2 * BLOCK :, :])
                prev_ref[:, sl] = dup[:BLOCK, :]
                cur_ref[:, sl] = dup[BLOCK:, :]

    cur = lambda b, i: (b * nb + i, 0)
    kv_spec = pl.BlockSpec((BLOCK, kvw2), cur)
    kv_shape = jax.ShapeDtypeStruct((T, kvw2), F32)
    return pl.pallas_call(
        body,
        name="attn_bwd",
        grid=(n_seq, nb),
        in_specs=_attn_specs(D, nb)
        + [pl.BlockSpec((BLOCK, D), cur), _resident((N_KV_HEADS, R, 2 * BLOCK)), _resident((N_KV_HEADS, R, 2 * BLOCK))],
        out_specs=[pl.BlockSpec((BLOCK, D), cur), kv_spec, kv_spec, kv_spec, kv_spec, pl.BlockSpec((8, LANES), lambda b, i: (0, 0))],
        out_shape=[jax.ShapeDtypeStruct((T, D), F32), kv_shape, kv_shape, kv_shape, kv_shape, jax.ShapeDtypeStruct((8, LANES), F32)],
        compiler_params=_params(2),
    )(q16, kd, kd, vd, vd, do, bias, sink)


def qkv_bwd(dq, dkc, dkp, dvc, dvp, qkv, dres, x, gain, w_qkv, qg, kg, *, seq):
    T, D = x.shape
    kvw2 = dkc.shape[1]
    kvw = kvw2 // 2
    nqt, nkt = D // LANES, kvw // LANES
    nb = seq // BLOCK
    tm = 2 * BLOCK
    n = T // tm

    def body(dq_ref, dkc_ref, dkpa_ref, dkpb_ref, dvc_ref, dvpa_ref, dvpb_ref, qkv_ref, dres_ref, x_ref, g_ref, w_ref,
             qg_ref, kg_ref, dx_ref, dqkv_ref, h_ref, dg_ref, hg_ref):
        i = pl.program_id(0)

        @pl.when(i == 0)
        def _():
            dg_ref[...] = jnp.zeros_like(dg_ref)
            hg_ref[...] = jnp.zeros_like(hg_ref)

        lo = lax.broadcasted_iota(jnp.int32, (1, LANES), 1) < HEAD_DIM
        last = (2 * i + 1) % nb == nb - 1
        dkd = dkc_ref[...] + jnp.concatenate([dkpa_ref[...], jnp.where(last, 0.0, dkpb_ref[...])], axis=0)
        dvd = dvc_ref[...] + jnp.concatenate([dvpa_ref[...], jnp.where(last, 0.0, dvpb_ref[...])], axis=0)

        def undup(d, t):
            a, b = d[:, 2 * t * LANES : (2 * t + 1) * LANES], d[:, (2 * t + 1) * LANES : (2 * t + 2) * LANES]
            return jnp.where(lo, a + _swap_halves(a), b + _swap_halves(b))

        tiles = [qkv_ref[:, t * LANES : (t + 1) * LANES] for t in range(nqt + nkt)]
        gains = [qg_ref[...]] * nqt + [kg_ref[...]] * nkt
        dys = [dq_ref[:, t * LANES : (t + 1) * LANES] * SCALE for t in range(nqt)] + [undup(dkd, t) for t in range(nkt)]
        _, rs = _head_norm(tiles, gains, lo)
        dxs, dgs = _head_norm_bwd(tiles, rs, gains, dys, lo)
        for t in range(nqt + nkt):
            dqkv_ref[:, t * LANES : (t + 1) * LANES] = dxs[t].astype(BF16)
        for t in range(nkt):
            dqkv_ref[:, D + kvw + t * LANES : D + kvw + (t + 1) * LANES] = undup(dvd, t).astype(BF16)
        hg_ref[0:1, :] += functools.reduce(lambda a, b: a + b, dgs[:nqt])
        hg_ref[1:2, :] += functools.reduce(lambda a, b: a + b, dgs[nqt:])
        dh = _dot_nt(dqkv_ref[...], w_ref[...])
        xt = x_ref[...]
        r = _rms(xt)
        gn = g_ref[...]
        h_ref[...] = ((xt * r) * gn).astype(BF16)
        dx, dgn = _rms_bwd(xt, r, gn, dh)
        dg_ref[0:1, :] += dgn
        dx_ref[...] = dres_ref[...] + dx

    row = lambda i: (i, 0)
    nxt_a = pl.BlockSpec((BLOCK, kvw2), lambda i: (2 * i + 1, 0))
    nxt_b = pl.BlockSpec((BLOCK, kvw2), lambda i: (jnp.minimum(2 * i + 2, 2 * n - 1), 0))
    return pl.pallas_call(
        body,
        name="qkv_bwd",
        grid=(n,),
        in_specs=[
            pl.BlockSpec((tm, D), row),
            pl.BlockSpec((tm, kvw2), row),
            nxt_a,
            nxt_b,
            pl.BlockSpec((tm, kvw2), row),
            nxt_a,
            nxt_b,
            pl.BlockSpec((tm, D + kvw2), row),
            pl.BlockSpec((tm, D), row),
            pl.BlockSpec((tm, D), row),
            _resident((1, D)),
            _resident((D, D + kvw2)),
            _resident((1, LANES)),
            _resident((1, LANES)),
        ],
        out_specs=[
            pl.BlockSpec((tm, D), row),
            pl.BlockSpec((tm, D + kvw2), row),
            pl.BlockSpec((tm, D), row),
            pl.BlockSpec((8, D), lambda i: (0, 0)),
            pl.BlockSpec((8, LANES), lambda i: (0, 0)),
        ],
        out_shape=[
            jax.ShapeDtypeStruct((T, D), F32),
            jax.ShapeDtypeStruct((T, D + kvw2), BF16),
            jax.ShapeDtypeStruct((T, D), BF16),
            jax.ShapeDtypeStruct((8, D), F32),
            jax.ShapeDtypeStruct((8, LANES), F32),
        ],
        compiler_params=_params(1),
    )(dq, dkc, dkp, dkp, dvc, dvp, dvp, qkv, dres, x, gain, w_qkv, qg, kg)


def local_step(x, target, gains, w, *, seq, tm=256, tm_ffn=256, tk=2048, shards=None):
    T, D = x.shape
    n_seq = T // seq
    nm, nf, qgain, kgain, sinks = gains
    H = D // HEAD_DIM
    tk, tk_long = min(tk, T), min(2 * tk, T)
    qg2, kg2 = jnp.tile(qgain, (1, 2)), jnp.tile(kgain, (1, 2))
    bias, sinkcol = _attn_tables(sinks, H)

    dist = shards is not None
    w = dict(w)

    plan = _Gather([shards["w_gu"][0], shards["w_d"][0]]) if dist else None
    (x1, bcx, y_conv, z16), got = conv_fwd(x, nm[0:1], w["w_in"], w["cw"], w["w_out"], seq=seq, tm=tm, plan=plan)
    if dist:
        w["w_gu"], w["w_d"] = [cols_from_shards(got[0]), None], [got[1].reshape(-1, D), None]
    plan = _Gather([shards["w_qkv"], shards["w_o"], shards["w_gu"][1], shards["w_d"][1]]) if dist else None
    (x2, gu0), got = ffn_fwd(x1, nf[0:1], w["w_gu"][0], w["w_d"][0], tm=tm_ffn, plan=plan)
    if dist:
        w["w_qkv"], w["w_o"] = cols_from_shards(got[0]), got[1].reshape(D, D)
        w["w_gu"][1], w["w_d"][1] = cols_from_shards(got[2]), got[3].reshape(-1, D)
    qkv, q16, kd, vd = qkv_proj(x2, nm[1:2], w["w_qkv"], qg2, kg2, tm=tm)
    ao = attn_fwd(q16, kd, vd, bias, sinkcol, seq=seq, n_seq=n_seq)
    (x3, dx4, gu1, sse), _ = ffn_fwd(x2, nf[1:2], w["w_gu"][1], w["w_d"][1], tm=tm_ffn, attn=(ao, w["w_o"]), target=target)

    by_dest = lambda a: a.reshape(N_DEV, -1, a.shape[-1])
    fresh = lambda a, *lead: jax.ShapeDtypeStruct((N_DEV,) + lead + a.shape[1:], a.dtype)
    gu_cols = 2 * MXU_TILE

    (dx3, a16, dgu, h16, d16, dnf1, dx3_16, dao), _ = ffn_bwd(
        dx4, x3, nf[1:2], gu1, w["w_gu"][1], w["w_d"][1], tm=tm, w_o=w["w_o"])
    g_gu1 = shards_from_cols(wgrad(h16, dgu, name="wgrad_gu1", b_cols=gu_cols, flat=True, tk=tk_long)[0])
    g_d1 = by_dest(wgrad(a16, d16, name="wgrad_d1", a_cols=a16.shape[1] // 2, tk=tk)[0])
    g_o = by_dest(wgrad(ao, dx3_16, name="wgrad_o", tk=tk)[0])
    dq, dkc, dkp, dvc, dvp, dsinks = attn_bwd(q16, kd, vd, dao, bias, sinkcol, seq=seq, n_seq=n_seq)
    dx2, dqkv16, h16, dnm1, dgains = qkv_bwd(dq, dkc, dkp, dvc, dvp, qkv, dx3, x2, nm[1:2], w["w_qkv"], qg2, kg2, seq=seq)
    g_qkv = shards_from_cols(wgrad(h16, dqkv16, name="wgrad_qkv", tk=tk)[0][0])

    plan = _Scatter([(g_gu1, fresh(g_gu1, 2), 1, False), (g_d1, fresh(g_d1, 2), 1, False)]) if dist else None
    (dx1, a16, dgu, h16, d16, dnf0), land_ffn = ffn_bwd(dx2, x1, nf[0:1], gu0, w["w_gu"][0], w["w_d"][0], tm=tm, plan=plan)
    plan = _Scatter([(g_o, fresh(g_o), None, False), (g_qkv, fresh(g_qkv), None, False)]) if dist else None
    g_gu0, land_attn = wgrad(h16, dgu, name="wgrad_gu0", b_cols=gu_cols, flat=True, tk=tk_long, plan=plan)
    g_gu0 = shards_from_cols(g_gu0)
    g_d0 = by_dest(wgrad(a16, d16, name="wgrad_d0", a_cols=a16.shape[1] // 2, tk=tk)[0])
    plan = _Scatter([(g_gu0, land_ffn[0], 0, False)]) if dist else None
    (gx, dbcx, h16, d16, dcw, dnm0), land_gu = conv_bwd(
        dx1, x, nm[0:1], bcx, y_conv, w["cw"], w["w_in"], w["w_out"], seq=seq, tm=tm, plan=plan)
    g_out = by_dest(wgrad(z16, d16, name="wgrad_out", tk=tk)[0])
    g_cw = dcw[0:3].reshape(3, N_DEV, D // N_DEV).transpose(1, 0, 2)
    plan = _Scatter([(g_d0, land_ffn[1], 0, False), (g_out, fresh(g_out), None, False), (g_cw, fresh(g_cw), None, False)]) if dist else None
    g_in, land_last = wgrad(h16, dbcx, name="wgrad_in", b_cols=3 * D // N_DEV, group=2, tk=tk_long, plan=plan)
    g = dict(w_in=g_in, cw=g_cw, w_out=g_out)
    if dist:
        g.update(w_gu=land_gu[0], w_d=land_last[0], w_out=land_last[1], cw=land_last[2], w_o=land_attn[0], w_qkv=land_attn[1])
    else:
        g.update(w_gu0=g_gu0, w_gu1=g_gu1, w_d0=g_d0, w_d1=g_d1, w_o=g_o, w_qkv=g_qkv)
    small = dict(nm0=dnm0, nm1=dnm1, nf0=dnf0, nf1=dnf1, gains=dgains, sinks=dsinks)
    return sse, gx, g, small


def _adamw_math(g, w, m, v):
    m = ADAM_B1 * m + (1.0 - ADAM_B1) * g
    v = ADAM_B2 * v + (1.0 - ADAM_B2) * (g * g)
    m_hat = m / (1.0 - ADAM_B1 ** ADAM_STEP)
    v_hat = v / (1.0 - ADAM_B2 ** ADAM_STEP)
    delta = -ADAM_LR * (m_hat / (jnp.sqrt(v_hat) + ADAM_EPS) + ADAM_WD * w)
    return delta, m, v


def adamw(parts, w, m, v, *, name, after=None):
    n, R, C = parts.shape
    tr = R
    for cand in (256, 128, 88, 64, 32, 16, 8):
        if R > cand and R % cand == 0:
            tr = cand
            break

    extra = [] if after is None else [after]

    def body(p_ref, w_ref, m_ref, v_ref, *rest):
        g_ref, d_ref, mo_ref, vo_ref = rest[len(extra) :]
        g = p_ref[0].astype(F32)
        for s in range(1, n):
            g = g + p_ref[s].astype(F32)
        g_ref[...] = g
        d_ref[...], mo_ref[...], vo_ref[...] = _adamw_math(g, w_ref[...], m_ref[...], v_ref[...])

    blk = pl.BlockSpec((tr, C), lambda i: (i, 0))
    return pl.pallas_call(
        body,
        name=name,
        grid=(R // tr,),
        in_specs=[pl.BlockSpec((n, tr, C), lambda i: (0, i, 0)), blk, blk, blk] + _any_specs(len(extra)),
        out_specs=[blk] * 4,
        out_shape=[jax.ShapeDtypeStruct((R, C), F32)] * 4,
        compiler_params=_params(1),
    )(parts, w, m, v, *extra)


def pack_small(small, sse, D):
    W = max(D, 2 * LANES)

    def body(nm0, nm1, nf0, nf1, gains, sinks, sse_ref, o_ref, kept_ref):
        o_ref[...] = jnp.zeros_like(o_ref)
        o_ref[0:1, :D] = nm0[0:1, :]
        o_ref[1:2, :D] = nm1[0:1, :]
        o_ref[2:3, :D] = nf0[0:1, :]
        o_ref[3:4, :D] = nf1[0:1, :]
        gq = gains[0:1, :] + pltpu.roll(gains[0:1, :], HEAD_DIM, 1)
        gk = gains[1:2, :] + pltpu.roll(gains[1:2, :], HEAD_DIM, 1)
        lane = lax.broadcasted_iota(jnp.int32, (1, LANES), 1)
        o_ref[4:5, :LANES] = jnp.where(lane < HEAD_DIM, gq, gk)
        o_ref[4:5, LANES : 2 * LANES] = sinks[0:1, :]
        o_ref[5:6, :LANES] = sse_ref[0:1, :] * (0.5 / D)
        kept_ref[0] = o_ref[...]

    return pl.pallas_call(
        body,
        name="pack_small",
        out_shape=[jax.ShapeDtypeStruct((8, W), F32), jax.ShapeDtypeStruct((1, 8, W), F32)],
    )(small["nm0"], small["nm1"], small["nf0"], small["nf1"], small["gains"], small["sinks"], sse)


def _pack_small_params(nm, nf, qg, kg, sk, D):
    W = max(D, 2 * LANES)
    row4 = jnp.concatenate([qg.reshape(-1), kg.reshape(-1), jnp.zeros((LANES - 2 * HEAD_DIM,), F32), sk.reshape(-1)])
    row4 = jnp.pad(row4, (0, W - row4.shape[0]))
    rows = [jnp.pad(r, (0, W - D)) for r in (nm[0], nm[1], nf[0], nf[1])] + [row4]
    return jnp.concatenate([jnp.stack(rows), jnp.zeros((3, W), F32)], axis=0)


def _unpack_small(a, D, H):
    nm = a[0:2, :D]
    nf = a[2:4, :D]
    qg = a[4:5, 0:HEAD_DIM]
    kg = a[4:5, HEAD_DIM : 2 * HEAD_DIM]
    sk = a[4:5, LANES : LANES + H]
    return qg, kg, sk, nm, nf


def kernel(x, conv_w_in, conv_w, conv_w_out, attn_w_qkv, attn_q_gain, attn_k_gain, attn_sinks, attn_w_o, norm_mixer, norm_ffn, ffn_w_gate_up, ffn_w_down, loss_target, m_conv_w_in, m_conv_w, m_conv_w_out, m_attn_w_qkv, m_attn_q_gain, m_attn_k_gain, m_attn_sinks, m_attn_w_o, m_norm_mixer, m_norm_ffn, m_ffn_w_gate_up, m_ffn_w_down, v_conv_w_in, v_conv_w, v_conv_w_out, v_attn_w_qkv, v_attn_q_gain, v_attn_k_gain, v_attn_sinks, v_attn_w_o, v_norm_mixer, v_norm_ffn, v_ffn_w_gate_up, v_ffn_w_down):
    n_seq, seq, D = x.shape
    T = n_seq * seq
    H = D // HEAD_DIM
    L = ffn_w_gate_up.shape[0]

    full = run_plan(_Gather([conv_w_in[0].astype(BF16), conv_w[0], conv_w_out[0].astype(BF16)]), name="gather_conv_weights")
    w = dict(w_in=cols_from_shards(full[0]), cw=full[1].transpose(1, 0, 2).reshape(3, D),
             w_out=full[2].reshape(D, D))
    shards = dict(w_gu=[ffn_w_gate_up[l].astype(BF16) for l in range(L)], w_d=[ffn_w_down[l].astype(BF16) for l in range(L)],
                  w_qkv=attn_w_qkv[0].astype(BF16), w_o=attn_w_o[0].astype(BF16))
    gains = (norm_mixer, norm_ffn, attn_q_gain, attn_k_gain, attn_sinks)
    sse, gx, g, small = local_step(x.reshape(T, D), loss_target.reshape(T, D), gains, w, seq=seq, shards=shards)

    packed, own_small = pack_small(small, sse, D)
    me = 4 * lax.axis_index("x") + 2 * lax.axis_index("y") + lax.axis_index("c")
    own_in = lax.dynamic_index_in_dim(g["w_in"], me, 0, keepdims=True)
    sems, sent, lands, token = start_exchange([(g["w_in"], False), (packed, True)], name="exchange_last_start")

    def flat(a):
        return a.reshape(-1, a.shape[-1])

    big = [conv_w_in, conv_w, conv_w_out, attn_w_qkv, attn_w_o, ffn_w_gate_up, ffn_w_down]
    big_m = [m_conv_w_in, m_conv_w, m_conv_w_out, m_attn_w_qkv, m_attn_w_o, m_ffn_w_gate_up, m_ffn_w_down]
    big_v = [v_conv_w_in, v_conv_w, v_conv_w_out, v_attn_w_qkv, v_attn_w_o, v_ffn_w_gate_up, v_ffn_w_down]
    tags = ["in", "cw", "out", "qkv", "o", "gu", "d"]
    bufs = [None, g["cw"], g["w_out"], g["w_qkv"], g["w_o"], g["w_gu"], g["w_d"]]

    def update(b, after=None):
        parts = bufs[b].reshape(N_DEV, -1, bufs[b].shape[-1])
        outs = adamw(parts, flat(big[b]), flat(big_m[b]), flat(big_v[b]), name="adamw_" + tags[b], after=after)
        return [o.reshape(big[b].shape) for o in outs]

    res = [None] + [update(b, after=token) for b in range(1, 7)]
    land_in, small_all = wait_exchange(sems, sent, lands, [r[0] for r in res[1:]], name="exchange_last_wait")
    bufs[0] = lax.dynamic_update_slice(land_in, own_in, (me, 0, 0))
    small_all = lax.dynamic_update_slice(small_all, own_small, (me, 0, 0))
    res[0] = update(0)
    sw = _pack_small_params(norm_mixer, norm_ffn, attn_q_gain, attn_k_gain, attn_sinks, D)
    sm = _pack_small_params(m_norm_mixer, m_norm_ffn, m_attn_q_gain, m_attn_k_gain, m_attn_sinks, D)
    sv = _pack_small_params(v_norm_mixer, v_norm_ffn, v_attn_q_gain, v_attn_k_gain, v_attn_sinks, D)
    souts = adamw(small_all, sw, sm, sv, name="adamw_small")
    sres = [_unpack_small(o, D, H) for o in souts]
    loss = souts[0][5, 0]

    def ordered(i):
        r, s = [r[i] for r in res], sres[i]
        return [r[0], r[1], r[2], r[3], s[0], s[1], s[2], r[4], s[3], s[4], r[5], r[6]]

    return (loss, gx.reshape(n_seq, seq, D), *ordered(0), *ordered(1), *ordered(2), *ordered(3))
```

```python
import functools
import math

import jax
import jax.numpy as jnp
from jax import lax
from jax.experimental import pallas as pl
from jax.experimental.pallas import tpu as pltpu

F32 = jnp.float32
BF16 = jnp.bfloat16

EPS = 1e-6
HEAD_DIM = 64
N_KV_HEADS = 4
BLOCK = 128
LANES = 128
N_DEV = 8
NEG = -1e30
SCALE = 1.0 / math.sqrt(HEAD_DIM)

ADAM_LR = 0.001
ADAM_B1 = 0.9
ADAM_B2 = 0.999
ADAM_EPS = 1e-08
ADAM_WD = 0.01
ADAM_STEP = 10

V7X_VMEM_BYTES = 64 * 1024 * 1024
VMEM_LIMIT = V7X_VMEM_BYTES - 8 * 1024 * 1024
MESH = pl.DeviceIdType.MESH

_NT = (((1,), (1,)), ((), ()))
_TN = (((0,), (0,)), ((), ()))


def _params(n_grid):
    return pltpu.CompilerParams(dimension_semantics=("arbitrary",) * n_grid, vmem_limit_bytes=VMEM_LIMIT)


def _resident(shape):
    nd = len(shape)
    return pl.BlockSpec(shape, lambda *_: (0,) * nd, pipeline_mode=pl.Buffered(1))


def _rms(x):
    return lax.rsqrt(jnp.mean(x * x, axis=-1, keepdims=True) + EPS)


def _rms_bwd(x, r, gain, dh):
    xn = x * r
    dxn = dh * gain
    dx = r * (dxn - xn * jnp.mean(dxn * xn, axis=-1, keepdims=True))
    return dx, jnp.sum(dh * xn, axis=0, keepdims=True)


def _dot(a, b):
    return jnp.dot(a, b, preferred_element_type=F32)


def _dot_nt(a, b):
    return lax.dot_general(a, b, _NT, preferred_element_type=F32)


def _dot_tn(a, b):
    return lax.dot_general(a, b, _TN, preferred_element_type=F32)


def _place():
    return lax.axis_index("x"), lax.axis_index("y"), lax.axis_index("c")


def _flip(v, bit):
    return 1 - v if bit else v


def _slot(px, py, pc):
    return 4 * px + 2 * py + pc


class _Gather:
    def __init__(self, shards):
        nt = len(shards)
        self.nt = nt
        self.inputs = list(shards)
        self.out_shapes = [jax.ShapeDtypeStruct((N_DEV,) + s.shape, s.dtype) for s in shards]
        self.scratch = [pltpu.SemaphoreType.DMA((nt, 7)), pltpu.SemaphoreType.DMA((nt, 7)), pltpu.SemaphoreType.DMA((nt,))]
        self.aliases = {}

    def phases(self, total):
        assert total >= 3
        return [(0, self.start), (total - 2, self.forward), (total - 1, self.finish)]

    def _copies(self, ins, outs, sems):
        send_sems, recv_sems, loc_sems = sems
        x, y, c = _place()
        me = _slot(x, y, c)
        sib = (x, y, 1 - c)
        chips = [(_flip(x, k >> 1), _flip(y, k & 1)) for k in (1, 2, 3)]

        def copy(t, k, src, dst_slot, to):
            return pltpu.make_async_remote_copy(
                src_ref=src, dst_ref=outs[t].at[dst_slot], send_sem=send_sems.at[t, k], recv_sem=recv_sems.at[t, k],
                device_id=to, device_id_type=MESH)

        local = [pltpu.make_async_copy(ins[t], outs[t].at[me], loc_sems.at[t]) for t in range(self.nt)]
        first, passed, arrive_ici, arrive_sib = [], [], [], []
        for t in range(self.nt):
            first.append(copy(t, 0, ins[t], me, sib))
            s = _slot(x, y, 1 - c)
            arrive_sib.append(copy(t, 0, outs[t].at[s], s, sib))
            for j, (px, py) in enumerate(chips):
                first.append(copy(t, 1 + j, ins[t], me, (px, py, c)))
                s = _slot(px, py, c)
                arrive_ici.append(copy(t, 1 + j, outs[t].at[s], s, sib))
                passed.append(copy(t, 4 + j, outs[t].at[s], s, sib))
                s = _slot(px, py, 1 - c)
                arrive_sib.append(copy(t, 4 + j, outs[t].at[s], s, sib))
        return local, first, passed, arrive_ici, arrive_sib

    def start(self, ins, outs, sems):
        local, first, _, _, _ = self._copies(ins, outs, sems)
        for cp in local + first:
            cp.start()

    def forward(self, ins, outs, sems):
        _, _, passed, arrive_ici, _ = self._copies(ins, outs, sems)
        for arrival, fwd in zip(arrive_ici, passed):
            arrival.wait_recv()
            fwd.start()

    def finish(self, ins, outs, sems):
        local, first, passed, _, arrive_sib = self._copies(ins, outs, sems)
        for cp in arrive_sib:
            cp.wait_recv()
        for cp in first + passed:
            cp.wait_send()
        for cp in local:
            cp.wait()


def _any_specs(n):
    return [pl.BlockSpec(memory_space=pl.ANY)] * n


def run_plan(plan, *, name):
    def body(*refs):
        n_in, n_out = len(plan.inputs), len(plan.out_shapes)
        ins, outs, sems = refs[:n_in], refs[n_in : n_in + n_out], refs[n_in + n_out :]
        for _, phase in plan.phases(3):
            phase(ins, outs, sems)

    return pl.pallas_call(
        body,
        name=name,
        in_specs=_any_specs(len(plan.inputs)),
        out_specs=_any_specs(len(plan.out_shapes)),
        out_shape=plan.out_shapes,
        scratch_shapes=plan.scratch,
        input_output_aliases=plan.aliases,
    )(*plan.inputs)


_HBM = pl.BlockSpec(memory_space=pltpu.HBM)
_SEM = pl.BlockSpec(memory_space=pltpu.SEMAPHORE)
_DATAFLOW = pltpu.SideEffectType.DATAFLOW_SIDE_EFFECTING


class Exchange:
    def __init__(self):
        self.zones = {}
        self.pending = []
        self.sources = []

    def start(self, items, *, name):
        nt = len(items)
        keys = list(dict.fromkeys(it[2] for it in items))
        for a, _, key, shape, _ in items:
            if key not in self.zones:
                self.zones[key] = lax.empty(shape, a.dtype)
        nz = len(keys)

        def body(*refs):
            ins, zones, sems, token = refs[:nt], refs[nt : nt + nz], refs[nt + nz : nt + nz + 2 * nt], refs[-1]
            x, y, c = _place()
            me = _slot(x, y, c)
            for k in range(1, N_DEV):
                px, py, pc = _flip(x, (k >> 2) & 1), _flip(y, (k >> 1) & 1), _flip(c, k & 1)
                for t, (_, whole, key, _, layer) in enumerate(items):
                    zone = zones[keys.index(key)]
                    pltpu.make_async_remote_copy(
                        src_ref=ins[t] if whole else ins[t].at[_slot(px, py, pc)],
                        dst_ref=zone.at[me] if layer is None else zone.at[me, layer],
                        send_sem=sems[2 * t], recv_sem=sems[2 * t + 1], device_id=(px, py, pc), device_id_type=MESH).start()
            token[...] = jnp.zeros_like(token)

        bufs = [pltpu.with_memory_space_constraint(b, pltpu.HBM) for b in [it[0] for it in items] + [self.zones[k] for k in keys]]
        outs = pl.pallas_call(
            body,
            name=name,
            in_specs=[_HBM] * (nt + nz),
            out_specs=[_SEM] * (2 * nt) + [_HBM] * (nt + nz) + [pl.BlockSpec(memory_space=pltpu.VMEM)],
            out_shape=[pltpu.SemaphoreType.DMA(())] * (2 * nt) + [pltpu.HBM(b.shape, b.dtype) for b in bufs]
            + [jax.ShapeDtypeStruct((8, LANES), F32)],
            input_output_aliases={i: 2 * nt + i for i in range(nt + nz)},
            compiler_params=pltpu.CompilerParams(has_side_effects=_DATAFLOW),
        )(*bufs)
        for t, (_, _, key, _, layer) in enumerate(items):
            self.pending.append((outs[2 * t], outs[2 * t + 1], key, layer))
        self.sources += [((it[2], it[4]), a) for it, a in zip(items, outs[2 * nt : 3 * nt])]
        for i, key in enumerate(keys):
            self.zones[key] = outs[3 * nt + i]
        return outs[-1]

    def wait(self, after, *, name):
        pending, keys = self.pending, list(self.zones)
        names, sources = [n for n, _ in self.sources], [a for _, a in self.sources]
        ns, nz, npend = len(sources), len(keys), len(pending)
        self.pending, self.sources = [], []

        def body(*refs):
            zones, sems = refs[ns : ns + nz], refs[ns + nz : ns + nz + 2 * npend]
            x, y, c = _place()
            for i, (_, _, key, layer) in enumerate(pending):
                zone = zones[keys.index(key)]
                rows = pl.ds(0, N_DEV - 1)
                seven = zone.at[rows] if layer is None else zone.at[rows, layer]
                pltpu.make_async_remote_copy(
                    src_ref=seven, dst_ref=seven, send_sem=sems[2 * i], recv_sem=sems[2 * i + 1],
                    device_id=(x, y, c), device_id_type=MESH).wait()

        bufs = list(sources) + [self.zones[k] for k in keys]
        flat_sems = [s for p in pending for s in p[:2]]
        outs = pl.pallas_call(
            body,
            name=name,
            in_specs=[_HBM] * (ns + nz) + [_SEM] * (2 * npend) + _any_specs(len(after)),
            out_specs=[_HBM] * (ns + nz),
            out_shape=[pltpu.HBM(b.shape, b.dtype) for b in bufs],
            input_output_aliases={i: i for i in range(ns + nz)},
            compiler_params=pltpu.CompilerParams(has_side_effects=_DATAFLOW),
        )(*bufs, *flat_sems, *after)
        self.zones = {}
        return dict(zip(keys, outs[ns:])), dict(zip(names, outs[:ns]))


def _call(body, *, name, grid, in_specs, out_specs, out_shape, args, scratch=(), plan=None, after=None):
    if after is not None:
        inner, n_real = body, len(in_specs)
        body = lambda *refs: inner(*refs[:n_real], *refs[n_real + 1 :])
        in_specs, args = list(in_specs) + _any_specs(1), list(args) + [after]
    n_in, n_out, n_scr = len(in_specs), len(out_specs), len(scratch)
    if plan is None:
        outs = pl.pallas_call(
            body, name=name, grid=grid, in_specs=in_specs, out_specs=out_specs, out_shape=out_shape,
            scratch_shapes=list(scratch), compiler_params=_params(len(grid)))(*args)
        return outs, None
    c_in, c_out = len(plan.inputs), len(plan.out_shapes)
    phases = plan.phases(math.prod(grid))

    def full(*refs):
        a, refs = refs[:n_in], refs[n_in:]
        ci, refs = refs[:c_in], refs[c_in:]
        o, refs = refs[:n_out], refs[n_out:]
        co, refs = refs[:c_out], refs[c_out:]
        s, cs = refs[:n_scr], refs[n_scr:]
        step = pl.program_id(0)
        for d in range(1, len(grid)):
            step = step * grid[d] + pl.program_id(d)
        for at, phase in phases:
            if at == 0:
                pl.when(step == 0)(functools.partial(phase, ci, co, cs))
        body(*a, *o, *s)
        for at, phase in phases:
            if at > 0:
                pl.when(step == at)(functools.partial(phase, ci, co, cs))

    outs = pl.pallas_call(
        full,
        name=name,
        grid=grid,
        in_specs=list(in_specs) + _any_specs(c_in),
        out_specs=list(out_specs) + _any_specs(c_out),
        out_shape=list(out_shape) + plan.out_shapes,
        scratch_shapes=list(scratch) + plan.scratch,
        input_output_aliases={n_in + i: n_out + t for i, t in plan.aliases.items()},
        compiler_params=_params(len(grid)),
    )(*args, *plan.inputs)
    return outs[:n_out], outs[n_out:]


def _row_tile(R):
    return 256 if R % 256 == 0 else R


def cols_from_shards(a):
    n, R, C = a.shape
    tr = _row_tile(R)

    def body(i_ref, o_ref):
        for s in range(n):
            o_ref[:, s * C : (s + 1) * C] = i_ref[s]

    return pl.pallas_call(
        body,
        name="cols_from_shards",
        grid=(R // tr,),
        in_specs=[pl.BlockSpec((n, tr, C), lambda i: (0, i, 0))],
        out_specs=pl.BlockSpec((tr, n * C), lambda i: (i, 0)),
        out_shape=jax.ShapeDtypeStruct((R, n * C), a.dtype),
        compiler_params=_params(1),
    )(a)


def shards_from_cols(a):
    R, W = a.shape
    C = W // N_DEV
    tr = _row_tile(R)

    def body(i_ref, o_ref):
        for s in range(N_DEV):
            o_ref[s] = i_ref[:, s * C : (s + 1) * C]

    return pl.pallas_call(
        body,
        name="shards_from_cols",
        grid=(R // tr,),
        in_specs=[pl.BlockSpec((tr, W), lambda i: (i, 0))],
        out_specs=pl.BlockSpec((N_DEV, tr, C), lambda i: (0, i, 0)),
        out_shape=jax.ShapeDtypeStruct((N_DEV, R, C), a.dtype),
        compiler_params=_params(1),
    )(a)


def _shift_down(u, prev8, row, n):
    out = pltpu.roll(u, n, 0)
    for k in range(n):
        out = jnp.where(row == k, prev8[8 - n + k : 8 - n + k + 1, :], out)
    return out


def _shift_up(u, next8, row, n, tm):
    out = pltpu.roll(u, tm - n, 0)
    for k in range(n):
        out = jnp.where(row == tm - n + k, next8[k : k + 1, :], out)
    return out


def conv_fwd(x, gain, w_in, cw, w_out, *, seq, tm, plan=None):
    T, D = x.shape
    tps = seq // tm

    def body(x_ref, g_ref, win_ref, cw_ref, wout_ref, x1_ref, bcx_ref, y_ref, z_ref, carry_ref):
        i = pl.program_id(0)

        @pl.when(i % tps == 0)
        def _():
            carry_ref[...] = jnp.zeros_like(carry_ref)

        xt = x_ref[...]
        h = ((xt * _rms(xt)) * g_ref[...]).astype(BF16)
        bcx = _dot(h, win_ref[...])
        bcx_ref[...] = bcx
        b, c, xv = bcx[:, :D], bcx[:, D : 2 * D], bcx[:, 2 * D :]
        u = b * xv
        row = lax.broadcasted_iota(jnp.int32, u.shape, 0)
        prev = carry_ref[...]
        u1 = _shift_down(u, prev, row, 1)
        u2 = _shift_down(u, prev, row, 2)
        carry_ref[...] = u[tm - 8 :, :]
        cwv = cw_ref[...]
        y = cwv[0:1, :] * u2 + cwv[1:2, :] * u1 + cwv[2:3, :] * u
        y_ref[...] = y
        z = (c * y).astype(BF16)
        z_ref[...] = z
        x1_ref[...] = xt + _dot(z, wout_ref[...])

    tile = pl.BlockSpec((tm, D), lambda i: (i, 0))
    return _call(
        body,
        plan=plan,
        args=(x, gain, w_in, cw, w_out),
        name="conv_fwd",
        grid=(T // tm,),
        in_specs=[
            pl.BlockSpec((tm, D), lambda i: (i, 0)),
            _resident((1, D)),
            _resident((D, 3 * D)),
            _resident((3, D)),
            _resident((D, D)),
        ],
        out_specs=[tile, pl.BlockSpec((tm, 3 * D), lambda i: (i, 0)), tile, tile],
        out_shape=[jax.ShapeDtypeStruct((T, D), F32), jax.ShapeDtypeStruct((T, 3 * D), F32),
                   jax.ShapeDtypeStruct((T, D), F32), jax.ShapeDtypeStruct((T, D), BF16)],
        scratch=[pltpu.VMEM((8, D), F32)],
    )


def conv_bwd(dx1, x, gain, bcx, y, cw, w_in, w_out, *, seq, tm, after=None):
    T, D = x.shape
    n = T // tm
    tps = seq // tm

    def body(d_ref, x_ref, g_ref, bcx_ref, y_ref, cw_ref, win_ref, wout_ref,
             gx_ref, dbcx_ref, h_ref, d16_ref, dcw_ref, dg_ref, carry_ref):
        i = pl.program_id(0)
        t = n - 1 - i

        @pl.when(i == 0)
        def _():
            dcw_ref[...] = jnp.zeros_like(dcw_ref)
            dg_ref[...] = jnp.zeros_like(dg_ref)

        @pl.when(t % tps == tps - 1)
        def _():
            carry_ref[...] = jnp.zeros_like(carry_ref)

        d = d_ref[...]
        d16 = d.astype(BF16)
        d16_ref[...] = d16
        dz = _dot_nt(d16, wout_ref[...])
        bcx = bcx_ref[...]
        b, c, xv = bcx[:, :D], bcx[:, D : 2 * D], bcx[:, 2 * D :]
        u = b * xv
        row = lax.broadcasted_iota(jnp.int32, u.shape, 0)
        cwv = cw_ref[...]
        dc = dz * y_ref[...]
        dy = dz * c
        nxt = carry_ref[...]
        dy1 = _shift_up(dy, nxt, row, 1, tm)
        dy2 = _shift_up(dy, nxt, row, 2, tm)
        carry_ref[...] = dy[0:8, :]
        dcw_ref[0:1, :] += jnp.sum(dy2 * u, axis=0, keepdims=True)
        dcw_ref[1:2, :] += jnp.sum(dy1 * u, axis=0, keepdims=True)
        dcw_ref[2:3, :] += jnp.sum(dy * u, axis=0, keepdims=True)
        du = cwv[2:3, :] * dy + cwv[1:2, :] * dy1 + cwv[0:1, :] * dy2
        dbcx_ref[:, :D] = (du * xv).astype(BF16)
        dbcx_ref[:, D : 2 * D] = dc.astype(BF16)
        dbcx_ref[:, 2 * D :] = (du * b).astype(BF16)
        dh = _dot_nt(dbcx_ref[...], win_ref[...])
        xt = x_ref[...]
        r = _rms(xt)
        gn = g_ref[...]
        h_ref[...] = ((xt * r) * gn).astype(BF16)
        dx, dgn = _rms_bwd(xt, r, gn, dh)
        dg_ref[0:1, :] += dgn
        gx_ref[...] = d + dx

    rev = lambda i: (n - 1 - i, 0)
    return _call(
        body,
        after=after,
        args=(dx1, x, gain, bcx, y, cw, w_in, w_out),
        name="conv_bwd",
        grid=(n,),
        in_specs=[
            pl.BlockSpec((tm, D), rev),
            pl.BlockSpec((tm, D), rev),
            _resident((1, D)),
            pl.BlockSpec((tm, 3 * D), rev),
            pl.BlockSpec((tm, D), rev),
            _resident((3, D)),
            _resident((D, 3 * D)),
            _resident((D, D)),
        ],
        out_specs=[
            pl.BlockSpec((tm, D), rev),
            pl.BlockSpec((tm, 3 * D), rev),
            pl.BlockSpec((tm, D), rev),
            pl.BlockSpec((tm, D), rev),
            pl.BlockSpec((8, D), lambda i: (0, 0)),
            pl.BlockSpec((8, D), lambda i: (0, 0)),
        ],
        out_shape=[
            jax.ShapeDtypeStruct((T, D), F32),
            jax.ShapeDtypeStruct((T, 3 * D), BF16),
            jax.ShapeDtypeStruct((T, D), BF16),
            jax.ShapeDtypeStruct((T, D), BF16),
            jax.ShapeDtypeStruct((8, D), F32),
            jax.ShapeDtypeStruct((8, D), F32),
        ],
        scratch=[pltpu.VMEM((8, D), F32)],
    )


MXU_TILE = 256
FFN_CHUNK = 4 * MXU_TILE


def _sigmoid(g):
    return 1.0 / (1.0 + jnp.exp(-g))


def _ffn_chunks(F):
    assert F % MXU_TILE == 0
    return [(s, min(FFN_CHUNK, F - s)) for s in range(0, F, FFN_CHUNK)]


def ffn_fwd(x, gain, w_gu, w_d, *, tm, plan=None, attn=None, target=None):
    T, D = x.shape
    F = w_d.shape[0]
    row = lambda i: (i, 0)
    tile = pl.BlockSpec((tm, D), row)

    def body(*refs):
        refs = list(refs)
        x_ref, g_ref, wgu_ref, wd_ref = refs[:4]
        del refs[:4]
        if attn is not None:
            ao_ref, wo_ref = refs[:2]
            del refs[:2]
        if target is not None:
            t_ref = refs.pop(0)
        if attn is not None:
            xin_ref = refs.pop(0)
        xo_ref, gu_ref = refs[:2]
        xt = x_ref[...]
        if attn is not None:
            xt = xt + _dot(ao_ref[...], wo_ref[...])
            xin_ref[...] = xt
        h = ((xt * _rms(xt)) * g_ref[...]).astype(BF16)
        acc = xt
        for s, n in _ffn_chunks(F):
            g = _dot(h, wgu_ref[:, s : s + n])
            u = _dot(h, wgu_ref[:, F + s : F + s + n])
            gu_ref[:, s : s + n] = g
            gu_ref[:, F + s : F + s + n] = u
            a = ((g * _sigmoid(g)) * u).astype(BF16)
            acc = acc + _dot(a, wd_ref[s : s + n, :])
        if target is None:
            xo_ref[...] = acc
        else:
            s_ref = refs[2]

            @pl.when(pl.program_id(0) == 0)
            def _():
                s_ref[...] = jnp.zeros_like(s_ref)

            e = acc - t_ref[...]
            xo_ref[...] = e * (1.0 / D)
            s_ref[...] += jnp.sum(jnp.sum(e * e, axis=-1, keepdims=True), axis=0, keepdims=True)

    args = [x, gain, w_gu, w_d]
    in_specs = [tile, _resident((1, D)), _resident((D, 2 * F)), _resident((F, D))]
    out_specs = [tile, pl.BlockSpec((tm, 2 * F), row)]
    out_shape = [jax.ShapeDtypeStruct((T, D), F32), jax.ShapeDtypeStruct((T, 2 * F), F32)]
    if attn is not None:
        args += list(attn)
        in_specs += [pl.BlockSpec((tm, attn[0].shape[1]), row), _resident(attn[1].shape)]
        out_specs.insert(0, tile)
        out_shape.insert(0, jax.ShapeDtypeStruct((T, D), F32))
    if target is not None:
        args.append(target)
        in_specs.append(tile)
        out_specs.append(pl.BlockSpec((8, LANES), lambda i: (0, 0)))
        out_shape.append(jax.ShapeDtypeStruct((8, LANES), F32))
    return _call(body, plan=plan, args=args, name="ffn_fwd", grid=(T // tm,), in_specs=in_specs, out_specs=out_specs,
                 out_shape=out_shape)


def ffn_bwd(dxo, x, gain, gu, w_gu, w_d, *, tm, after=None, w_o=None):
    T, D = x.shape
    F = w_d.shape[0]

    def body(d_ref, x_ref, g_ref, gu_ref, wgu_ref, wd_ref, *rest):
        if w_o is not None:
            wo_ref, rest = rest[0], rest[1:]
        dx_ref, a_ref, dgu_ref, h_ref, d16_ref, dg_ref = rest[:6]

        @pl.when(pl.program_id(0) == 0)
        def _():
            dg_ref[...] = jnp.zeros_like(dg_ref)

        d = d_ref[...]
        d16 = d.astype(BF16)
        d16_ref[...] = d16
        dh = jnp.zeros((tm, D), F32)
        for c0, n in _ffn_chunks(F):
            g = gu_ref[:, c0 : c0 + n]
            u = gu_ref[:, F + c0 : F + c0 + n]
            da = _dot_nt(d16, wd_ref[c0 : c0 + n, :])
            s = _sigmoid(g)
            sg = g * s
            a_ref[:, c0 : c0 + n] = (sg * u).astype(BF16)
            dg16 = (da * u * (s + sg * (1.0 - s))).astype(BF16)
            du16 = (da * sg).astype(BF16)
            dgu_ref[:, c0 : c0 + n] = dg16
            dgu_ref[:, F + c0 : F + c0 + n] = du16
            dh = dh + _dot_nt(dg16, wgu_ref[:, c0 : c0 + n]) + _dot_nt(du16, wgu_ref[:, F + c0 : F + c0 + n])
        xt = x_ref[...]
        r = _rms(xt)
        gn = g_ref[...]
        h_ref[...] = ((xt * r) * gn).astype(BF16)
        dx, dgn = _rms_bwd(xt, r, gn, dh)
        dg_ref[0:1, :] += dgn
        dxi = d + dx
        dx_ref[...] = dxi
        if w_o is not None:
            dxi16_ref, dao_ref = rest[6:8]
            dxi16 = dxi.astype(BF16)
            dxi16_ref[...] = dxi16
            dao_ref[...] = _dot_nt(dxi16, wo_ref[...]).astype(BF16)

    tile = pl.BlockSpec((tm, D), lambda i: (i, 0))
    args = [dxo, x, gain, gu, w_gu, w_d]
    wide = lambda n: pl.BlockSpec((tm, n), lambda i: (i, 0))
    in_specs = [tile, tile, _resident((1, D)), wide(2 * F), _resident((D, 2 * F)), _resident((F, D))]
    out_specs = [tile, wide(F), wide(2 * F), tile, tile, pl.BlockSpec((8, D), lambda i: (0, 0))]
    out_shape = [
        jax.ShapeDtypeStruct((T, D), F32),
        jax.ShapeDtypeStruct((T, F), BF16),
        jax.ShapeDtypeStruct((T, 2 * F), BF16),
        jax.ShapeDtypeStruct((T, D), BF16),
        jax.ShapeDtypeStruct((T, D), BF16),
        jax.ShapeDtypeStruct((8, D), F32),
    ]
    if w_o is not None:
        args.append(w_o)
        in_specs.append(_resident(w_o.shape))
        out_specs += [tile, pl.BlockSpec((tm, w_o.shape[0]), lambda i: (i, 0))]
        out_shape += [jax.ShapeDtypeStruct((T, D), BF16), jax.ShapeDtypeStruct((T, w_o.shape[0]), BF16)]
    return _call(body, after=after, args=args, name="ffn_bwd", grid=(T // tm,), in_specs=in_specs, out_specs=out_specs,
                 out_shape=out_shape)


def wgrad(a, b, *, name, a_cols=0, b_cols=0, group=1, flat=False, tk, out_dtype=BF16, after=None):
    T, K = a.shape
    J = 1
    if a_cols:
        K = a_cols
        J = a.shape[1] // K
        a_spec = pl.BlockSpec((tk, K), lambda j, k: (k, j))
    else:
        a_spec = pl.BlockSpec((tk, K), lambda j, k: (k, 0))
    if b_cols:
        N = b_cols * group
        J = b.shape[1] // N
        b_spec = pl.BlockSpec((tk, N), lambda j, k: (k, j))
    else:
        N = b.shape[1]
        b_spec = pl.BlockSpec((tk, N), lambda j, k: (k, 0))
    nk = T // tk
    if flat:
        o_spec, o_shape = pl.BlockSpec((K, N), lambda j, k: (0, j)), (K, J * N)
    elif group > 1:
        o_spec, o_shape = pl.BlockSpec((group, K, b_cols), lambda j, k: (j, 0, 0)), (J * group, K, b_cols)
    else:
        o_spec, o_shape = pl.BlockSpec((None, K, N), lambda j, k: (j, 0, 0)), (J, K, N)

    def body(a_ref, b_ref, o_ref, acc_ref):
        k = pl.program_id(1)

        @pl.when(k == 0)
        def _():
            acc_ref[...] = jnp.zeros_like(acc_ref)

        acc_ref[...] += _dot_tn(a_ref[...], b_ref[...])

        @pl.when(k == nk - 1)
        def _():
            if group > 1 and not flat:
                for i in range(group):
                    o_ref[i] = acc_ref[:, i * b_cols : (i + 1) * b_cols].astype(out_dtype)
            else:
                o_ref[...] = acc_ref[...].astype(out_dtype)

    outs, _ = _call(
        body,
        after=after,
        args=(a, b),
        name=name,
        grid=(J, nk),
        in_specs=[a_spec, b_spec],
        out_specs=[o_spec],
        out_shape=[jax.ShapeDtypeStruct(o_shape, out_dtype)],
        scratch=[pltpu.VMEM((K, N), F32)],
    )
    return outs[0]


def _seg(xs, lo):
    s_lo = [jnp.sum(jnp.where(lo, x, 0.0), axis=-1, keepdims=True) for x in xs]
    s_hi = [jnp.sum(jnp.where(lo, 0.0, x), axis=-1, keepdims=True) for x in xs]
    return [jnp.where(lo, a, b) for a, b in zip(s_lo, s_hi)]


def _head_norm(xs, gains, lo):
    rs = [lax.rsqrt(s * (1.0 / HEAD_DIM) + EPS) for s in _seg([x * x for x in xs], lo)]
    return [(x * r) * g for x, r, g in zip(xs, rs, gains)], rs


def _head_norm_bwd(xs, rs, gains, dys, lo):
    xns = [x * r for x, r in zip(xs, rs)]
    dxns = [dy * g for dy, g in zip(dys, gains)]
    means = [s * (1.0 / HEAD_DIM) for s in _seg([a * b for a, b in zip(dxns, xns)], lo)]
    dxs = [r * (dxn - xn * m) for r, dxn, xn, m in zip(rs, dxns, xns, means)]
    return dxs, [jnp.sum(dy * xn, axis=0, keepdims=True) for dy, xn in zip(dys, xns)]


def _swap_halves(x):
    return pltpu.roll(x, HEAD_DIM, 1)


def qkv_proj(x, gain, w, qg, kg, *, tm):
    T, D = x.shape
    N = w.shape[1]
    kvw = N_KV_HEADS * HEAD_DIM
    nqt, nkt = D // LANES, kvw // LANES

    def body(x_ref, g_ref, w_ref, qg_ref, kg_ref, qkv_ref, q_ref, kd_ref, vd_ref):
        xt = x_ref[...]
        h = ((xt * _rms(xt)) * g_ref[...]).astype(BF16)
        qkv = _dot(h, w_ref[...])
        qkv_ref[...] = qkv
        lo = lax.broadcasted_iota(jnp.int32, (1, LANES), 1) < HEAD_DIM
        tiles = [qkv[:, t * LANES : (t + 1) * LANES] for t in range(nqt + nkt)]
        normed, _ = _head_norm(tiles, [qg_ref[...]] * nqt + [kg_ref[...]] * nkt, lo)
        for t in range(nqt):
            q_ref[:, t * LANES : (t + 1) * LANES] = (normed[t] * SCALE).astype(BF16)
        for t in range(nkt):
            kn = normed[nqt + t]
            v = qkv[:, D + kvw + t * LANES : D + kvw + (t + 1) * LANES]
            for src, dst in ((kn, kd_ref), (v, vd_ref)):
                sw = _swap_halves(src)
                dst[:, 2 * t * LANES : (2 * t + 1) * LANES] = jnp.where(lo, src, sw).astype(BF16)
                dst[:, (2 * t + 1) * LANES : (2 * t + 2) * LANES] = jnp.where(lo, sw, src).astype(BF16)

    row = lambda i: (i, 0)
    return pl.pallas_call(
        body,
        name="qkv_proj",
        grid=(T // tm,),
        in_specs=[pl.BlockSpec((tm, D), row), _resident((1, D)), _resident((D, N)), _resident((1, LANES)), _resident((1, LANES))],
        out_specs=[pl.BlockSpec((tm, N), row), pl.BlockSpec((tm, D), row), pl.BlockSpec((tm, 2 * kvw), row), pl.BlockSpec((tm, 2 * kvw), row)],
        out_shape=[
            jax.ShapeDtypeStruct((T, N), F32),
            jax.ShapeDtypeStruct((T, D), BF16),
            jax.ShapeDtypeStruct((T, 2 * kvw), BF16),
            jax.ShapeDtypeStruct((T, 2 * kvw), BF16),
        ],
        compiler_params=_params(1),
    )(x, gain, w, qg, kg)


def _attn_tables(sinks, n_q_heads):
    P = n_q_heads // N_KV_HEADS // 2
    h = jnp.arange(1, n_q_heads + 1, dtype=F32)
    slopes = jnp.exp2(-8.0 * h / n_q_heads).reshape(N_KV_HEADS, P, 1, 2, 1)
    qi = jnp.arange(BLOCK)[:, None]
    kj = jnp.arange(BLOCK)[None, :]
    dist = jnp.where(kj <= qi, qi - kj, qi + BLOCK - kj).astype(F32)
    shape = (N_KV_HEADS, P, BLOCK, 2, BLOCK)
    bias = jnp.broadcast_to(-slopes * dist[None, None, :, None, :], shape)
    sink = jnp.broadcast_to(sinks.astype(F32).reshape(N_KV_HEADS, P, 1, 2, 1), shape)
    return bias.reshape(N_KV_HEADS, P * BLOCK, 2 * BLOCK), sink.reshape(N_KV_HEADS, P * BLOCK, 2 * BLOCK)


def _attn_specs(D, nb):
    kvw2 = 2 * N_KV_HEADS * HEAD_DIM
    cur = lambda b, i: (b * nb + i, 0)
    prev = lambda b, i: (jnp.maximum(b * nb + i - 1, 0), 0)
    return [
        pl.BlockSpec((BLOCK, D), cur),
        pl.BlockSpec((BLOCK, kvw2), cur),
        pl.BlockSpec((BLOCK, kvw2), prev),
        pl.BlockSpec((BLOCK, kvw2), cur),
        pl.BlockSpec((BLOCK, kvw2), prev),
    ]


def _attn_operands(kh, P, lo, q_ref, kc_ref, kp_ref, vc_ref, vp_ref):
    sl = slice(kh * LANES, (kh + 1) * LANES)

    def cat(prev_ref, cur_ref):
        d = jnp.concatenate([prev_ref[:, sl], cur_ref[:, sl]], axis=0)
        z = jnp.zeros_like(d)
        return jnp.concatenate([jnp.where(lo, d, z), jnp.where(lo, z, d)], axis=0)

    qt = jnp.concatenate([q_ref[:, (kh * P + pr) * LANES : (kh * P + pr + 1) * LANES] for pr in range(P)], axis=0)
    return qt, cat(kp_ref, kc_ref), cat(vp_ref, vc_ref)


def _attn_exp(s_all, bias, sink, tri, first):
    out = []
    for par in range(2):
        c0 = 2 * par * BLOCK
        s = jnp.where(tri, s_all[:, c0 + BLOCK : c0 + 2 * BLOCK], jnp.where(first, NEG, s_all[:, c0 : c0 + BLOCK]))
        s = s + bias[:, par * BLOCK : (par + 1) * BLOCK]
        snk = sink[:, par * BLOCK : (par + 1) * BLOCK]
        m = jnp.maximum(jnp.max(s, axis=-1, keepdims=True), snk)
        out.append((jnp.exp(s - m), jnp.exp(snk - m)))
    return out


def _unfold(x, tri):
    z = jnp.zeros_like(x)
    return jnp.concatenate([jnp.where(tri, z, x), jnp.where(tri, x, z)], axis=1)


def _attn_masks(R):
    lane = lax.broadcasted_iota(jnp.int32, (1, LANES), 1)
    row = lax.broadcasted_iota(jnp.int32, (R, BLOCK), 0) & (BLOCK - 1)
    col = lax.broadcasted_iota(jnp.int32, (R, BLOCK), 1)
    return lane, lane < HEAD_DIM, col <= row


def attn_fwd(q16, kd, vd, bias, sink, *, seq, n_seq):
    T, D = q16.shape
    nb = seq // BLOCK
    P = D // HEAD_DIM // N_KV_HEADS // 2
    R = P * BLOCK
    KV = range(N_KV_HEADS)

    def body(q_ref, kc_ref, kp_ref, vc_ref, vp_ref, bias_ref, sink_ref, o_ref):
        first = pl.program_id(1) == 0
        _, lo, tri = _attn_masks(R)
        r4 = lax.broadcasted_iota(jnp.int32, (4 * BLOCK, LANES), 0)
        l4 = lax.broadcasted_iota(jnp.int32, (4 * BLOCK, LANES), 1)
        ones = ((r4 < 2 * BLOCK) == (l4 < HEAD_DIM)).astype(BF16)
        ops = [_attn_operands(kh, P, lo, q_ref, kc_ref, kp_ref, vc_ref, vp_ref) for kh in KV]
        s_all = [_dot_nt(ops[kh][0], ops[kh][1]) for kh in KV]
        ex = [_attn_exp(s_all[kh], bias_ref[kh], sink_ref[kh], tri, first) for kh in KV]
        lhs = [jnp.concatenate([_unfold(e, tri) for e, _ in ex[kh]], axis=1).astype(BF16) for kh in KV]
        o = [_dot(lhs[kh], ops[kh][2]) for kh in KV]
        den = [_dot(lhs[kh], ones) for kh in KV]
        for kh in KV:
            out = o[kh] / (den[kh] + jnp.where(lo, ex[kh][0][1], ex[kh][1][1]))
            for pr in range(P):
                t = kh * P + pr
                o_ref[:, t * LANES : (t + 1) * LANES] = out[pr * BLOCK : (pr + 1) * BLOCK, :].astype(BF16)

    return pl.pallas_call(
        body,
        name="attn_fwd",
        grid=(n_seq, nb),
        in_specs=_attn_specs(D, nb) + [_resident((N_KV_HEADS, R, 2 * BLOCK)), _resident((N_KV_HEADS, R, 2 * BLOCK))],
        out_specs=pl.BlockSpec((BLOCK, D), lambda b, i: (b * nb + i, 0)),
        out_shape=jax.ShapeDtypeStruct((T, D), BF16),
        compiler_params=_params(2),
    )(q16, kd, kd, vd, vd, bias, sink)


def attn_bwd(q16, kd, vd, do, bias, sink, *, seq, n_seq):
    T, D = q16.shape
    kvw2 = 2 * N_KV_HEADS * HEAD_DIM
    nb = seq // BLOCK
    G = D // HEAD_DIM // N_KV_HEADS
    P = G // 2
    R = P * BLOCK
    KV = range(N_KV_HEADS)

    def body(q_ref, kc_ref, kp_ref, vc_ref, vp_ref, do_ref, bias_ref, sink_ref,
             dq_ref, dkc_ref, dkp_ref, dvc_ref, dvp_ref, dsink_ref):
        first = pl.program_id(1) == 0

        @pl.when(jnp.logical_and(pl.program_id(0) == 0, first))
        def _():
            dsink_ref[...] = jnp.zeros_like(dsink_ref)

        lane, lo, tri = _attn_masks(R)
        r4 = lax.broadcasted_iota(jnp.int32, (4 * BLOCK, 2 * BLOCK), 0)
        c4 = lax.broadcasted_iota(jnp.int32, (4 * BLOCK, 2 * BLOCK), 1)
        ones = ((r4 < 2 * BLOCK) == (c4 < BLOCK)).astype(BF16)
        ops = [_attn_operands(kh, P, lo, q_ref, kc_ref, kp_ref, vc_ref, vp_ref) for kh in KV]
        do16 = [jnp.concatenate([do_ref[:, (kh * P + pr) * LANES : (kh * P + pr + 1) * LANES] for pr in range(P)], axis=0)
                for kh in KV]
        s_all = [_dot_nt(ops[kh][0], ops[kh][1]) for kh in KV]
        dp_all = [_dot_nt(do16[kh], ops[kh][2]) for kh in KV]
        ex = [_attn_exp(s_all[kh], bias_ref[kh], sink_ref[kh], tri, first) for kh in KV]
        den = [_dot(jnp.concatenate([_unfold(e, tri) for e, _ in ex[kh]], axis=1).astype(BF16), ones) for kh in KV]
        dsink = jnp.zeros((1, LANES), F32)
        pf, dsf = [], []
        for kh in KV:
            ps_, ds_ = [], []
            for par in range(2):
                e, es = ex[kh][par]
                inv = 1.0 / (den[kh][:, par * BLOCK : (par + 1) * BLOCK] + es)
                p = e * inv
                c0 = 2 * par * BLOCK
                dp = jnp.where(tri, dp_all[kh][:, c0 + BLOCK : c0 + 2 * BLOCK], dp_all[kh][:, c0 : c0 + BLOCK])
                delta = jnp.sum(p * dp, axis=-1, keepdims=True)
                ds_.append(_unfold(p * (dp - delta), tri))
                ps_.append(_unfold(p, tri))
                dsr = -((es * inv) * delta)
                for pr in range(P):
                    hq = kh * G + 2 * pr + par
                    tot = jnp.sum(dsr[pr * BLOCK : (pr + 1) * BLOCK, :], axis=0, keepdims=True)
                    dsink = dsink + jnp.where(lane == hq, tot, 0.0)
            pf.append(jnp.concatenate(ps_, axis=1).astype(BF16))
            dsf.append(jnp.concatenate(ds_, axis=1).astype(BF16))
        dq = [_dot(dsf[kh], ops[kh][1]) for kh in KV]
        dk = [_dot_tn(dsf[kh], ops[kh][0]) for kh in KV]
        dv = [_dot_tn(pf[kh], do16[kh]) for kh in KV]
        dsink_ref[0:1, :] += dsink
        for kh in KV:
            sl = slice(kh * LANES, (kh + 1) * LANES)
            for pr in range(P):
                t = kh * P + pr
                dq_ref[:, t * LANES : (t + 1) * LANES] = dq[kh][pr * BLOCK : (pr + 1) * BLOCK, :]
            for full, prev_ref, cur_ref in ((dk[kh], dkp_ref, dkc_ref), (dv[kh], dvp_ref, dvc_ref)):
                dup = jnp.where(lo, full[: 2 * BLOCK, :], full[2 * BLOCK :, :])
                prev_ref[:, sl] = dup[:BLOCK, :]
                cur_ref[:, sl] = dup[BLOCK:, :]

    cur = lambda b, i: (b * nb + i, 0)
    kv_spec = pl.BlockSpec((BLOCK, kvw2), cur)
    kv_shape = jax.ShapeDtypeStruct((T, kvw2), F32)
    return pl.pallas_call(
        body,
        name="attn_bwd",
        grid=(n_seq, nb),
        in_specs=_attn_specs(D, nb)
        + [pl.BlockSpec((BLOCK, D), cur), _resident((N_KV_HEADS, R, 2 * BLOCK)), _resident((N_KV_HEADS, R, 2 * BLOCK))],
        out_specs=[pl.BlockSpec((BLOCK, D), cur), kv_spec, kv_spec, kv_spec, kv_spec, pl.BlockSpec((8, LANES), lambda b, i: (0, 0))],
        out_shape=[jax.ShapeDtypeStruct((T, D), F32), kv_shape, kv_shape, kv_shape, kv_shape, jax.ShapeDtypeStruct((8, LANES), F32)],
        compiler_params=_params(2),
    )(q16, kd, kd, vd, vd, do, bias, sink)


def qkv_bwd(dq, dkc, dkp, dvc, dvp, qkv, dres, x, gain, w_qkv, qg, kg, *, seq):
    T, D = x.shape
    kvw2 = dkc.shape[1]
    kvw = kvw2 // 2
    nqt, nkt = D // LANES, kvw // LANES
    nb = seq // BLOCK
    tm = 2 * BLOCK
    n = T // tm

    def body(dq_ref, dkc_ref, dkpa_ref, dkpb_ref, dvc_ref, dvpa_ref, dvpb_ref, qkv_ref, dres_ref, x_ref, g_ref, w_ref,
             qg_ref, kg_ref, dx_ref, dqkv_ref, h_ref, dg_ref, hg_ref):
        i = pl.program_id(0)

        @pl.when(i == 0)
        def _():
            dg_ref[...] = jnp.zeros_like(dg_ref)
            hg_ref[...] = jnp.zeros_like(hg_ref)

        lo = lax.broadcasted_iota(jnp.int32, (1, LANES), 1) < HEAD_DIM
        last = (2 * i + 1) % nb == nb - 1
        dkd = dkc_ref[...] + jnp.concatenate([dkpa_ref[...], jnp.where(last, 0.0, dkpb_ref[...])], axis=0)
        dvd = dvc_ref[...] + jnp.concatenate([dvpa_ref[...], jnp.where(last, 0.0, dvpb_ref[...])], axis=0)

        def undup(d, t):
            a, b = d[:, 2 * t * LANES : (2 * t + 1) * LANES], d[:, (2 * t + 1) * LANES : (2 * t + 2) * LANES]
            return jnp.where(lo, a + _swap_halves(a), b + _swap_halves(b))

        tiles = [qkv_ref[:, t * LANES : (t + 1) * LANES] for t in range(nqt + nkt)]
        gains = [qg_ref[...]] * nqt + [kg_ref[...]] * nkt
        dys = [dq_ref[:, t * LANES : (t + 1) * LANES] * SCALE for t in range(nqt)] + [undup(dkd, t) for t in range(nkt)]
        _, rs = _head_norm(tiles, gains, lo)
        dxs, dgs = _head_norm_bwd(tiles, rs, gains, dys, lo)
        for t in range(nqt + nkt):
            dqkv_ref[:, t * LANES : (t + 1) * LANES] = dxs[t].astype(BF16)
        for t in range(nkt):
            dqkv_ref[:, D + kvw + t * LANES : D + kvw + (t + 1) * LANES] = undup(dvd, t).astype(BF16)
        hg_ref[0:1, :] += functools.reduce(lambda a, b: a + b, dgs[:nqt])
        hg_ref[1:2, :] += functools.reduce(lambda a, b: a + b, dgs[nqt:])
        dh = _dot_nt(dqkv_ref[...], w_ref[...])
        xt = x_ref[...]
        r = _rms(xt)
        gn = g_ref[...]
        h_ref[...] = ((xt * r) * gn).astype(BF16)
        dx, dgn = _rms_bwd(xt, r, gn, dh)
        dg_ref[0:1, :] += dgn
        dx_ref[...] = dres_ref[...] + dx

    row = lambda i: (i, 0)
    nxt_a = pl.BlockSpec((BLOCK, kvw2), lambda i: (2 * i + 1, 0))
    nxt_b = pl.BlockSpec((BLOCK, kvw2), lambda i: (jnp.minimum(2 * i + 2, 2 * n - 1), 0))
    return pl.pallas_call(
        body,
        name="qkv_bwd",
        grid=(n,),
        in_specs=[
            pl.BlockSpec((tm, D), row),
            pl.BlockSpec((tm, kvw2), row),
            nxt_a,
            nxt_b,
            pl.BlockSpec((tm, kvw2), row),
            nxt_a,
            nxt_b,
            pl.BlockSpec((tm, D + kvw2), row),
            pl.BlockSpec((tm, D), row),
            pl.BlockSpec((tm, D), row),
            _resident((1, D)),
            _resident((D, D + kvw2)),
            _resident((1, LANES)),
            _resident((1, LANES)),
        ],
        out_specs=[
            pl.BlockSpec((tm, D), row),
            pl.BlockSpec((tm, D + kvw2), row),
            pl.BlockSpec((tm, D), row),
            pl.BlockSpec((8, D), lambda i: (0, 0)),
            pl.BlockSpec((8, LANES), lambda i: (0, 0)),
        ],
        out_shape=[
            jax.ShapeDtypeStruct((T, D), F32),
            jax.ShapeDtypeStruct((T, D + kvw2), BF16),
            jax.ShapeDtypeStruct((T, D), BF16),
            jax.ShapeDtypeStruct((8, D), F32),
            jax.ShapeDtypeStruct((8, LANES), F32),
        ],
        compiler_params=_params(1),
    )(dq, dkc, dkp, dkp, dvc, dvp, dvp, qkv, dres, x, gain, w_qkv, qg, kg)


def local_step(x, target, gains, w, *, seq, tm=256, tm_ffn=256, tk=2048, shards=None, ex=None):
    T, D = x.shape
    n_seq = T // seq
    nm, nf, qgain, kgain, sinks = gains
    H = D // HEAD_DIM
    tk, tk_long = min(tk, T), min(2 * tk, T)
    qg2, kg2 = jnp.tile(qgain, (1, 2)), jnp.tile(kgain, (1, 2))
    bias, sinkcol = _attn_tables(sinks, H)

    dist = shards is not None
    w = dict(w)

    plan = _Gather([shards["w_gu"][0], shards["w_d"][0]]) if dist else None
    (x1, bcx, y_conv, z16), got = conv_fwd(x, nm[0:1], w["w_in"], w["cw"], w["w_out"], seq=seq, tm=tm, plan=plan)
    if dist:
        w["w_gu"], w["w_d"] = [cols_from_shards(got[0]), None], [got[1].reshape(-1, D), None]
    plan = _Gather([shards["w_qkv"], shards["w_o"], shards["w_gu"][1], shards["w_d"][1]]) if dist else None
    (x2, gu0), got = ffn_fwd(x1, nf[0:1], w["w_gu"][0], w["w_d"][0], tm=tm_ffn, plan=plan)
    if dist:
        w["w_qkv"], w["w_o"] = cols_from_shards(got[0]), got[1].reshape(D, D)
        w["w_gu"][1], w["w_d"][1] = cols_from_shards(got[2]), got[3].reshape(-1, D)
    qkv, q16, kd, vd = qkv_proj(x2, nm[1:2], w["w_qkv"], qg2, kg2, tm=tm)
    ao = attn_fwd(q16, kd, vd, bias, sinkcol, seq=seq, n_seq=n_seq)
    (x3, dx4, gu1, sse), _ = ffn_fwd(x2, nf[1:2], w["w_gu"][1], w["w_d"][1], tm=tm_ffn, attn=(ao, w["w_o"]), target=target)

    by_dest = lambda a: a.reshape(N_DEV, -1, a.shape[-1])
    gu_cols = 2 * MXU_TILE

    def send(name, *entries):
        if ex is None:
            return None
        items = [(a, False, key, (N_DEV,) + (() if layers is None else (layers,)) + a.shape[1:], layer)
                 for a, key, layer, layers in entries]
        return ex.start(items, name=name)

    (dx3, a16, dgu, h16, d16, dnf1, dx3_16, dao), _ = ffn_bwd(
        dx4, x3, nf[1:2], gu1, w["w_gu"][1], w["w_d"][1], tm=tm, w_o=w["w_o"])
    g_gu1 = shards_from_cols(wgrad(h16, dgu, name="wgrad_gu1", b_cols=gu_cols, flat=True, tk=tk_long))
    g_d1 = by_dest(wgrad(a16, d16, name="wgrad_d1", a_cols=a16.shape[1] // 2, tk=tk))
    tok = send("exchange_ffn1", (g_gu1, "w_gu", 1, 2), (g_d1, "w_d", 1, 2))
    g_o = by_dest(wgrad(ao, dx3_16, name="wgrad_o", tk=tk, after=tok))
    dq, dkc, dkp, dvc, dvp, dsinks = attn_bwd(q16, kd, vd, dao, bias, sinkcol, seq=seq, n_seq=n_seq)
    dx2, dqkv16, h16, dnm1, dgains = qkv_bwd(dq, dkc, dkp, dvc, dvp, qkv, dx3, x2, nm[1:2], w["w_qkv"], qg2, kg2, seq=seq)
    g_qkv = shards_from_cols(wgrad(h16, dqkv16, name="wgrad_qkv", tk=tk)[0])
    tok = send("exchange_attn", (g_o, "w_o", None, None), (g_qkv, "w_qkv", None, None))
    (dx1, a16, dgu, h16, d16, dnf0), _ = ffn_bwd(dx2, x1, nf[0:1], gu0, w["w_gu"][0], w["w_d"][0], tm=tm, after=tok)
    g_gu0 = shards_from_cols(wgrad(h16, dgu, name="wgrad_gu0", b_cols=gu_cols, flat=True, tk=tk_long))
    tok = send("exchange_gu0", (g_gu0, "w_gu", 0, 2))
    g_d0 = by_dest(wgrad(a16, d16, name="wgrad_d0", a_cols=a16.shape[1] // 2, tk=tk, after=tok))
    tok = send("exchange_d0", (g_d0, "w_d", 0, 2))
    (gx, dbcx, h16, d16, dcw, dnm0), _ = conv_bwd(
        dx1, x, nm[0:1], bcx, y_conv, w["cw"], w["w_in"], w["w_out"], seq=seq, tm=tm, after=tok)
    g_out = by_dest(wgrad(z16, d16, name="wgrad_out", tk=tk))
    g_cw = dcw[0:3].reshape(3, N_DEV, D // N_DEV).transpose(1, 0, 2)
    tok = send("exchange_out", (g_out, "w_out", None, None), (g_cw, "cw", None, None))
    g_in = wgrad(h16, dbcx, name="wgrad_in", b_cols=3 * D // N_DEV, group=2, tk=tk_long, after=tok)
    g = dict(w_in=g_in, cw=g_cw, w_out=g_out, w_o=g_o, w_qkv=g_qkv, w_gu=[g_gu0, g_gu1], w_d=[g_d0, g_d1])
    small = dict(nm0=dnm0, nm1=dnm1, nf0=dnf0, nf1=dnf1, gains=dgains, sinks=dsinks)
    return sse, gx, g, small


def _adamw_math(g, w, m, v):
    m = ADAM_B1 * m + (1.0 - ADAM_B1) * g
    v = ADAM_B2 * v + (1.0 - ADAM_B2) * (g * g)
    m_hat = m / (1.0 - ADAM_B1 ** ADAM_STEP)
    v_hat = v / (1.0 - ADAM_B2 ** ADAM_STEP)
    delta = -ADAM_LR * (m_hat / (jnp.sqrt(v_hat) + ADAM_EPS) + ADAM_WD * w)
    return delta, m, v


def adamw(parts, owns, w, m, v, *, name, after=None):
    n, LR, C = parts.shape
    L = len(owns)
    R = LR // L
    tr = R
    for cand in (256, 128, 88, 64, 32, 16, 8):
        if R > cand and R % cand == 0:
            tr = cand
            break
    per_layer = R // tr
    extra = [] if after is None else [after]

    def body(me_ref, p_ref, *rest):
        own_refs, (w_ref, m_ref, v_ref) = rest[:L], rest[L : L + 3]
        g_ref, d_ref, mo_ref, vo_ref = rest[L + 3 + len(extra) :]
        layer = pl.program_id(0) // per_layer
        mine = own_refs[0][...].astype(F32)
        for j in range(1, L):
            mine = jnp.where(layer == j, own_refs[j][...].astype(F32), mine)
        g = None
        for s in range(n):
            share = jnp.where(me_ref[0] == s, mine, p_ref[s].astype(F32))
            g = share if g is None else g + share
        g_ref[...] = g
        d_ref[...], mo_ref[...], vo_ref[...] = _adamw_math(g, w_ref[...], m_ref[...], v_ref[...])

    blk = pl.BlockSpec((tr, C), lambda i, me: (i, 0))
    own_specs = [pl.BlockSpec((None, tr, C), lambda i, me: (me[0], i % per_layer, 0)) if o.ndim == 3
                 else pl.BlockSpec((tr, C), lambda i, me: (i % per_layer, 0)) for o in owns]
    me = (4 * lax.axis_index("x") + 2 * lax.axis_index("y") + lax.axis_index("c")).astype(jnp.int32).reshape(1)
    return pl.pallas_call(
        body,
        name=name,
        grid_spec=pltpu.PrefetchScalarGridSpec(
            num_scalar_prefetch=1,
            grid=(LR // tr,),
            in_specs=[pl.BlockSpec((n, tr, C), lambda i, me: (0, i, 0))] + own_specs + [blk, blk, blk] + _any_specs(len(extra)),
            out_specs=[blk] * 4,
        ),
        out_shape=[jax.ShapeDtypeStruct((LR, C), F32)] * 4,
        compiler_params=_params(1),
    )(me, parts, *owns, w, m, v, *extra)


def pack_small(small, sse, D):
    W = max(D, 2 * LANES)

    def body(nm0, nm1, nf0, nf1, gains, sinks, sse_ref, o_ref):
        o_ref[...] = jnp.zeros_like(o_ref)
        o_ref[0:1, :D] = nm0[0:1, :]
        o_ref[1:2, :D] = nm1[0:1, :]
        o_ref[2:3, :D] = nf0[0:1, :]
        o_ref[3:4, :D] = nf1[0:1, :]
        gq = gains[0:1, :] + pltpu.roll(gains[0:1, :], HEAD_DIM, 1)
        gk = gains[1:2, :] + pltpu.roll(gains[1:2, :], HEAD_DIM, 1)
        lane = lax.broadcasted_iota(jnp.int32, (1, LANES), 1)
        o_ref[4:5, :LANES] = jnp.where(lane < HEAD_DIM, gq, gk)
        o_ref[4:5, LANES : 2 * LANES] = sinks[0:1, :]
        o_ref[5:6, :LANES] = sse_ref[0:1, :] * (0.5 / D)

    return pl.pallas_call(
        body,
        name="pack_small",
        out_shape=jax.ShapeDtypeStruct((8, W), F32),
    )(small["nm0"], small["nm1"], small["nf0"], small["nf1"], small["gains"], small["sinks"], sse)


def _pack_small_params(nm, nf, qg, kg, sk, D):
    W = max(D, 2 * LANES)
    row4 = jnp.concatenate([qg.reshape(-1), kg.reshape(-1), jnp.zeros((LANES - 2 * HEAD_DIM,), F32), sk.reshape(-1)])
    row4 = jnp.pad(row4, (0, W - row4.shape[0]))
    rows = [jnp.pad(r, (0, W - D)) for r in (nm[0], nm[1], nf[0], nf[1])] + [row4]
    return jnp.concatenate([jnp.stack(rows), jnp.zeros((3, W), F32)], axis=0)


def _unpack_small(a, D, H):
    nm = a[0:2, :D]
    nf = a[2:4, :D]
    qg = a[4:5, 0:HEAD_DIM]
    kg = a[4:5, HEAD_DIM : 2 * HEAD_DIM]
    sk = a[4:5, LANES : LANES + H]
    return qg, kg, sk, nm, nf


def kernel(x, conv_w_in, conv_w, conv_w_out, attn_w_qkv, attn_q_gain, attn_k_gain, attn_sinks, attn_w_o, norm_mixer, norm_ffn, ffn_w_gate_up, ffn_w_down, loss_target, m_conv_w_in, m_conv_w, m_conv_w_out, m_attn_w_qkv, m_attn_q_gain, m_attn_k_gain, m_attn_sinks, m_attn_w_o, m_norm_mixer, m_norm_ffn, m_ffn_w_gate_up, m_ffn_w_down, v_conv_w_in, v_conv_w, v_conv_w_out, v_attn_w_qkv, v_attn_q_gain, v_attn_k_gain, v_attn_sinks, v_attn_w_o, v_norm_mixer, v_norm_ffn, v_ffn_w_gate_up, v_ffn_w_down):
    n_seq, seq, D = x.shape
    T = n_seq * seq
    H = D // HEAD_DIM
    L = ffn_w_gate_up.shape[0]

    full = run_plan(_Gather([conv_w_in[0].astype(BF16), conv_w[0], conv_w_out[0].astype(BF16)]), name="gather_conv_weights")
    w = dict(w_in=cols_from_shards(full[0]), cw=full[1].transpose(1, 0, 2).reshape(3, D),
             w_out=full[2].reshape(D, D))
    shards = dict(w_gu=[ffn_w_gate_up[l].astype(BF16) for l in range(L)], w_d=[ffn_w_down[l].astype(BF16) for l in range(L)],
                  w_qkv=attn_w_qkv[0].astype(BF16), w_o=attn_w_o[0].astype(BF16))
    gains = (norm_mixer, norm_ffn, attn_q_gain, attn_k_gain, attn_sinks)
    ex = Exchange()
    sse, gx, g, small = local_step(x.reshape(T, D), loss_target.reshape(T, D), gains, w, seq=seq, shards=shards, ex=ex)
    zones, own = ex.wait([g["w_in"]], name="exchange_wait")

    packed = pack_small(small, sse, D)
    token = ex.start([(g["w_in"], False, "w_in", g["w_in"].shape, None),
                      (packed, True, "small", (N_DEV,) + packed.shape, None)], name="exchange_last")

    def flat(a):
        return a.reshape(-1, a.shape[-1])

    big = [conv_w_in, conv_w, conv_w_out, attn_w_qkv, attn_w_o, ffn_w_gate_up, ffn_w_down]
    big_m = [m_conv_w_in, m_conv_w, m_conv_w_out, m_attn_w_qkv, m_attn_w_o, m_ffn_w_gate_up, m_ffn_w_down]
    big_v = [v_conv_w_in, v_conv_w, v_conv_w_out, v_attn_w_qkv, v_attn_w_o, v_ffn_w_gate_up, v_ffn_w_down]
    keys = ["w_in", "cw", "w_out", "w_qkv", "w_o", "w_gu", "w_d"]

    def update(b, zones, own, after=None):
        zone = zones[keys[b]]
        parts = zone.reshape(N_DEV, -1, zone.shape[-1])
        layers = [None] if zone.ndim == 3 else range(zone.shape[1])
        outs = adamw(parts, [own[(keys[b], l)] for l in layers], flat(big[b]), flat(big_m[b]), flat(big_v[b]),
                     name="adamw_" + keys[b], after=after)
        return [o.reshape(big[b].shape) for o in outs]

    res = [None] + [update(b, zones, own, after=token) for b in range(1, 7)]
    zones, own = ex.wait([r[0] for r in res[1:]], name="exchange_last_wait")
    res[0] = update(0, zones, own)
    sw = _pack_small_params(norm_mixer, norm_ffn, attn_q_gain, attn_k_gain, attn_sinks, D)
    sm = _pack_small_params(m_norm_mixer, m_norm_ffn, m_attn_q_gain, m_attn_k_gain, m_attn_sinks, D)
    sv = _pack_small_params(v_norm_mixer, v_norm_ffn, v_attn_q_gain, v_attn_k_gain, v_attn_sinks, D)
    souts = adamw(zones["small"], [own[("small", None)]], sw, sm, sv, name="adamw_small")
    sres = [_unpack_small(o, D, H) for o in souts]
    loss = souts[0][5, 0]

    def ordered(i):
        r, s = [r[i] for r in res], sres[i]
        return [r[0], r[1], r[2], r[3], s[0], s[1], s[2], r[4], s[3], s[4], r[5], r[6]]

    return (loss, gx.reshape(n_seq, seq, D), *ordered(0), *ordered(1), *ordered(2), *ordered(3))
```

```python
import functools
import math

import jax
import jax.numpy as jnp
from jax import lax
from jax.experimental import pallas as pl
from jax.experimental.pallas import tpu as pltpu

F32 = jnp.float32
BF16 = jnp.bfloat16

EPS = 1e-6
HEAD_DIM = 64
N_KV_HEADS = 4
BLOCK = 128
LANES = 128
N_DEV = 8
NEG = -1e30
SCALE = 1.0 / math.sqrt(HEAD_DIM)

ADAM_LR = 0.001
ADAM_B1 = 0.9
ADAM_B2 = 0.999
ADAM_EPS = 1e-08
ADAM_WD = 0.01
ADAM_STEP = 10

V7X_VMEM_BYTES = 64 * 1024 * 1024
VMEM_LIMIT = V7X_VMEM_BYTES - 8 * 1024 * 1024
MESH = pl.DeviceIdType.MESH

_NT = (((1,), (1,)), ((), ()))
_TN = (((0,), (0,)), ((), ()))


def _params(n_grid):
    return pltpu.CompilerParams(dimension_semantics=("arbitrary",) * n_grid, vmem_limit_bytes=VMEM_LIMIT)


def _resident(shape):
    nd = len(shape)
    return pl.BlockSpec(shape, lambda *_: (0,) * nd, pipeline_mode=pl.Buffered(1))


def _rms(x):
    return lax.rsqrt(jnp.mean(x * x, axis=-1, keepdims=True) + EPS)


def _rms_bwd(x, r, gain, dh):
    xn = x * r
    dxn = dh * gain
    dx = r * (dxn - xn * jnp.mean(dxn * xn, axis=-1, keepdims=True))
    return dx, jnp.sum(dh * xn, axis=0, keepdims=True)


def _dot(a, b):
    return jnp.dot(a, b, preferred_element_type=F32)


def _dot_nt(a, b):
    return lax.dot_general(a, b, _NT, preferred_element_type=F32)


def _dot_tn(a, b):
    return lax.dot_general(a, b, _TN, preferred_element_type=F32)


def _place():
    return lax.axis_index("x"), lax.axis_index("y"), lax.axis_index("c")


def _flip(v, bit):
    return 1 - v if bit else v


def _slot(px, py, pc):
    return 4 * px + 2 * py + pc


class _Gather:
    def __init__(self, shards):
        nt = len(shards)
        self.nt = nt
        self.inputs = list(shards)
        self.out_shapes = [jax.ShapeDtypeStruct((N_DEV,) + s.shape, s.dtype) for s in shards]
        self.scratch = [pltpu.SemaphoreType.DMA((nt, 7)), pltpu.SemaphoreType.DMA((nt, 7)), pltpu.SemaphoreType.DMA((nt,))]
        self.aliases = {}

    def phases(self, total):
        assert total >= 3
        return [(0, self.start), (total - 2, self.forward), (total - 1, self.finish)]

    def _copies(self, ins, outs, sems):
        send_sems, recv_sems, loc_sems = sems
        x, y, c = _place()
        me = _slot(x, y, c)
        sib = (x, y, 1 - c)
        chips = [(_flip(x, k >> 1), _flip(y, k & 1)) for k in (1, 2, 3)]

        def copy(t, k, src, dst_slot, to):
            return pltpu.make_async_remote_copy(
                src_ref=src, dst_ref=outs[t].at[dst_slot], send_sem=send_sems.at[t, k], recv_sem=recv_sems.at[t, k],
                device_id=to, device_id_type=MESH)

        local = [pltpu.make_async_copy(ins[t], outs[t].at[me], loc_sems.at[t]) for t in range(self.nt)]
        first, passed, arrive_ici, arrive_sib = [], [], [], []
        for t in range(self.nt):
            first.append(copy(t, 0, ins[t], me, sib))
            s = _slot(x, y, 1 - c)
            arrive_sib.append(copy(t, 0, outs[t].at[s], s, sib))
            for j, (px, py) in enumerate(chips):
                first.append(copy(t, 1 + j, ins[t], me, (px, py, c)))
                s = _slot(px, py, c)
                arrive_ici.append(copy(t, 1 + j, outs[t].at[s], s, sib))
                passed.append(copy(t, 4 + j, outs[t].at[s], s, sib))
                s = _slot(px, py, 1 - c)
                arrive_sib.append(copy(t, 4 + j, outs[t].at[s], s, sib))
        return local, first, passed, arrive_ici, arrive_sib

    def start(self, ins, outs, sems):
        local, first, _, _, _ = self._copies(ins, outs, sems)
        for cp in local + first:
            cp.start()

    def forward(self, ins, outs, sems):
        _, _, passed, arrive_ici, _ = self._copies(ins, outs, sems)
        for arrival, fwd in zip(arrive_ici, passed):
            arrival.wait_recv()
            fwd.start()

    def finish(self, ins, outs, sems):
        local, first, passed, _, arrive_sib = self._copies(ins, outs, sems)
        for cp in arrive_sib:
            cp.wait_recv()
        for cp in first + passed:
            cp.wait_send()
        for cp in local:
            cp.wait()


def _any_specs(n):
    return [pl.BlockSpec(memory_space=pl.ANY)] * n


def run_plan(plan, *, name):
    def body(*refs):
        n_in, n_out = len(plan.inputs), len(plan.out_shapes)
        ins, outs, sems = refs[:n_in], refs[n_in : n_in + n_out], refs[n_in + n_out :]
        for _, phase in plan.phases(3):
            phase(ins, outs, sems)

    return pl.pallas_call(
        body,
        name=name,
        in_specs=_any_specs(len(plan.inputs)),
        out_specs=_any_specs(len(plan.out_shapes)),
        out_shape=plan.out_shapes,
        scratch_shapes=plan.scratch,
        input_output_aliases=plan.aliases,
    )(*plan.inputs)


_HBM = pl.BlockSpec(memory_space=pltpu.HBM)
_SEM = pl.BlockSpec(memory_space=pltpu.SEMAPHORE)
_DATAFLOW = pltpu.SideEffectType.DATAFLOW_SIDE_EFFECTING


class Exchange:
    def __init__(self):
        self.zones = {}
        self.pending = []
        self.sources = []

    def start(self, items, *, name):
        nt = len(items)
        keys = list(dict.fromkeys(it[2] for it in items))
        for a, _, key, shape, _ in items:
            if key not in self.zones:
                self.zones[key] = lax.empty(shape, a.dtype)
        nz = len(keys)

        def body(*refs):
            ins, zones, sems, token = refs[:nt], refs[nt : nt + nz], refs[nt + nz : nt + nz + 2 * nt], refs[-1]
            x, y, c = _place()
            me = _slot(x, y, c)
            for k in range(1, N_DEV):
                px, py, pc = _flip(x, (k >> 2) & 1), _flip(y, (k >> 1) & 1), _flip(c, k & 1)
                for t, (_, whole, key, _, layer) in enumerate(items):
                    zone = zones[keys.index(key)]
                    pltpu.make_async_remote_copy(
                        src_ref=ins[t] if whole else ins[t].at[_slot(px, py, pc)],
                        dst_ref=zone.at[me] if layer is None else zone.at[me, layer],
                        send_sem=sems[2 * t], recv_sem=sems[2 * t + 1], device_id=(px, py, pc), device_id_type=MESH).start()
            token[...] = jnp.zeros_like(token)

        bufs = [pltpu.with_memory_space_constraint(b, pltpu.HBM) for b in [it[0] for it in items] + [self.zones[k] for k in keys]]
        outs = pl.pallas_call(
            body,
            name=name,
            in_specs=[_HBM] * (nt + nz),
            out_specs=[_SEM] * (2 * nt) + [_HBM] * (nt + nz) + [pl.BlockSpec(memory_space=pltpu.VMEM)],
            out_shape=[pltpu.SemaphoreType.DMA(())] * (2 * nt) + [pltpu.HBM(b.shape, b.dtype) for b in bufs]
            + [jax.ShapeDtypeStruct((8, LANES), F32)],
            input_output_aliases={i: 2 * nt + i for i in range(nt + nz)},
            compiler_params=pltpu.CompilerParams(has_side_effects=_DATAFLOW),
        )(*bufs)
        for t, (_, _, key, _, layer) in enumerate(items):
            self.pending.append((outs[2 * t], outs[2 * t + 1], key, layer))
        self.sources += [((it[2], it[4]), a) for it, a in zip(items, outs[2 * nt : 3 * nt])]
        for i, key in enumerate(keys):
            self.zones[key] = outs[3 * nt + i]
        return outs[-1]

    def wait(self, after, *, name):
        pending, keys = self.pending, list(self.zones)
        names, sources = [n for n, _ in self.sources], [a for _, a in self.sources]
        ns, nz, npend = len(sources), len(keys), len(pending)
        self.pending, self.sources = [], []

        def body(*refs):
            zones, sems = refs[ns : ns + nz], refs[ns + nz : ns + nz + 2 * npend]
            x, y, c = _place()
            for i, (_, _, key, layer) in enumerate(pending):
                zone = zones[keys.index(key)]
                rows = pl.ds(0, N_DEV - 1)
                seven = zone.at[rows] if layer is None else zone.at[rows, layer]
                pltpu.make_async_remote_copy(
                    src_ref=seven, dst_ref=seven, send_sem=sems[2 * i], recv_sem=sems[2 * i + 1],
                    device_id=(x, y, c), device_id_type=MESH).wait()

        bufs = list(sources) + [self.zones[k] for k in keys]
        flat_sems = [s for p in pending for s in p[:2]]
        outs = pl.pallas_call(
            body,
            name=name,
            in_specs=[_HBM] * (ns + nz) + [_SEM] * (2 * npend) + _any_specs(len(after)),
            out_specs=[_HBM] * (ns + nz),
            out_shape=[pltpu.HBM(b.shape, b.dtype) for b in bufs],
            input_output_aliases={i: i for i in range(ns + nz)},
            compiler_params=pltpu.CompilerParams(has_side_effects=_DATAFLOW),
        )(*bufs, *flat_sems, *after)
        self.zones = {}
        return dict(zip(keys, outs[ns:])), dict(zip(names, outs[:ns]))


def _call(body, *, name, grid, in_specs, out_specs, out_shape, args, scratch=(), plan=None, after=None):
    if after is not None:
        inner, n_real = body, len(in_specs)
        body = lambda *refs: inner(*refs[:n_real], *refs[n_real + 1 :])
        in_specs, args = list(in_specs) + _any_specs(1), list(args) + [after]
    n_in, n_out, n_scr = len(in_specs), len(out_specs), len(scratch)
    if plan is None:
        outs = pl.pallas_call(
            body, name=name, grid=grid, in_specs=in_specs, out_specs=out_specs, out_shape=out_shape,
            scratch_shapes=list(scratch), compiler_params=_params(len(grid)))(*args)
        return outs, None
    c_in, c_out = len(plan.inputs), len(plan.out_shapes)
    phases = plan.phases(math.prod(grid))

    def full(*refs):
        a, refs = refs[:n_in], refs[n_in:]
        ci, refs = refs[:c_in], refs[c_in:]
        o, refs = refs[:n_out], refs[n_out:]
        co, refs = refs[:c_out], refs[c_out:]
        s, cs = refs[:n_scr], refs[n_scr:]
        step = pl.program_id(0)
        for d in range(1, len(grid)):
            step = step * grid[d] + pl.program_id(d)
        for at, phase in phases:
            if at == 0:
                pl.when(step == 0)(functools.partial(phase, ci, co, cs))
        body(*a, *o, *s)
        for at, phase in phases:
            if at > 0:
                pl.when(step == at)(functools.partial(phase, ci, co, cs))

    outs = pl.pallas_call(
        full,
        name=name,
        grid=grid,
        in_specs=list(in_specs) + _any_specs(c_in),
        out_specs=list(out_specs) + _any_specs(c_out),
        out_shape=list(out_shape) + plan.out_shapes,
        scratch_shapes=list(scratch) + plan.scratch,
        input_output_aliases={n_in + i: n_out + t for i, t in plan.aliases.items()},
        compiler_params=_params(len(grid)),
    )(*args, *plan.inputs)
    return outs[:n_out], outs[n_out:]


def _row_tile(R):
    return 256 if R % 256 == 0 else R


def cols_from_shards(a):
    n, R, C = a.shape
    tr = _row_tile(R)

    def body(i_ref, o_ref):
        for s in range(n):
            o_ref[:, s * C : (s + 1) * C] = i_ref[s]

    return pl.pallas_call(
        body,
        name="cols_from_shards",
        grid=(R // tr,),
        in_specs=[pl.BlockSpec((n, tr, C), lambda i: (0, i, 0))],
        out_specs=pl.BlockSpec((tr, n * C), lambda i: (i, 0)),
        out_shape=jax.ShapeDtypeStruct((R, n * C), a.dtype),
        compiler_params=_params(1),
    )(a)


def shards_from_cols(a):
    R, W = a.shape
    C = W // N_DEV
    tr = _row_tile(R)

    def body(i_ref, o_ref):
        for s in range(N_DEV):
            o_ref[s] = i_ref[:, s * C : (s + 1) * C]

    return pl.pallas_call(
        body,
        name="shards_from_cols",
        grid=(R // tr,),
        in_specs=[pl.BlockSpec((tr, W), lambda i: (i, 0))],
        out_specs=pl.BlockSpec((N_DEV, tr, C), lambda i: (0, i, 0)),
        out_shape=jax.ShapeDtypeStruct((N_DEV, R, C), a.dtype),
        compiler_params=_params(1),
    )(a)


def _shift_down(u, prev8, row, n):
    out = pltpu.roll(u, n, 0)
    for k in range(n):
        out = jnp.where(row == k, prev8[8 - n + k : 8 - n + k + 1, :], out)
    return out


def _shift_up(u, next8, row, n, tm):
    out = pltpu.roll(u, tm - n, 0)
    for k in range(n):
        out = jnp.where(row == tm - n + k, next8[k : k + 1, :], out)
    return out


def conv_fwd(x, gain, w_in, cw, w_out, *, seq, tm, plan=None):
    T, D = x.shape
    tps = seq // tm

    def body(x_ref, g_ref, win_ref, cw_ref, wout_ref, x1_ref, bcx_ref, y_ref, z_ref, carry_ref):
        i = pl.program_id(0)

        @pl.when(i % tps == 0)
        def _():
            carry_ref[...] = jnp.zeros_like(carry_ref)

        xt = x_ref[...]
        h = ((xt * _rms(xt)) * g_ref[...]).astype(BF16)
        bcx = _dot(h, win_ref[...])
        bcx_ref[...] = bcx
        b, c, xv = bcx[:, :D], bcx[:, D : 2 * D], bcx[:, 2 * D :]
        u = b * xv
        row = lax.broadcasted_iota(jnp.int32, u.shape, 0)
        prev = carry_ref[...]
        u1 = _shift_down(u, prev, row, 1)
        u2 = _shift_down(u, prev, row, 2)
        carry_ref[...] = u[tm - 8 :, :]
        cwv = cw_ref[...]
        y = cwv[0:1, :] * u2 + cwv[1:2, :] * u1 + cwv[2:3, :] * u
        y_ref[...] = y
        z = (c * y).astype(BF16)
        z_ref[...] = z
        x1_ref[...] = xt + _dot(z, wout_ref[...])

    tile = pl.BlockSpec((tm, D), lambda i: (i, 0))
    return _call(
        body,
        plan=plan,
        args=(x, gain, w_in, cw, w_out),
        name="conv_fwd",
        grid=(T // tm,),
        in_specs=[
            pl.BlockSpec((tm, D), lambda i: (i, 0)),
            _resident((1, D)),
            _resident((D, 3 * D)),
            _resident((3, D)),
            _resident((D, D)),
        ],
        out_specs=[tile, pl.BlockSpec((tm, 3 * D), lambda i: (i, 0)), tile, tile],
        out_shape=[jax.ShapeDtypeStruct((T, D), F32), jax.ShapeDtypeStruct((T, 3 * D), F32),
                   jax.ShapeDtypeStruct((T, D), F32), jax.ShapeDtypeStruct((T, D), BF16)],
        scratch=[pltpu.VMEM((8, D), F32)],
    )


def conv_bwd(dx1, x, gain, bcx, y, cw, w_in, w_out, *, seq, tm, after=None):
    T, D = x.shape
    n = T // tm
    tps = seq // tm

    def body(d_ref, x_ref, g_ref, bcx_ref, y_ref, cw_ref, win_ref, wout_ref,
             gx_ref, dbcx_ref, h_ref, d16_ref, dcw_ref, dg_ref, carry_ref):
        i = pl.program_id(0)
        t = n - 1 - i

        @pl.when(i == 0)
        def _():
            dcw_ref[...] = jnp.zeros_like(dcw_ref)
            dg_ref[...] = jnp.zeros_like(dg_ref)

        @pl.when(t % tps == tps - 1)
        def _():
            carry_ref[...] = jnp.zeros_like(carry_ref)

        d = d_ref[...]
        d16 = d.astype(BF16)
        d16_ref[...] = d16
        dz = _dot_nt(d16, wout_ref[...])
        bcx = bcx_ref[...]
        b, c, xv = bcx[:, :D], bcx[:, D : 2 * D], bcx[:, 2 * D :]
        u = b * xv
        row = lax.broadcasted_iota(jnp.int32, u.shape, 0)
        cwv = cw_ref[...]
        dc = dz * y_ref[...]
        dy = dz * c
        nxt = carry_ref[...]
        dy1 = _shift_up(dy, nxt, row, 1, tm)
        dy2 = _shift_up(dy, nxt, row, 2, tm)
        carry_ref[...] = dy[0:8, :]
        dcw_ref[0:1, :] += jnp.sum(dy2 * u, axis=0, keepdims=True)
        dcw_ref[1:2, :] += jnp.sum(dy1 * u, axis=0, keepdims=True)
        dcw_ref[2:3, :] += jnp.sum(dy * u, axis=0, keepdims=True)
        du = cwv[2:3, :] * dy + cwv[1:2, :] * dy1 + cwv[0:1, :] * dy2
        dbcx_ref[:, :D] = (du * xv).astype(BF16)
        dbcx_ref[:, D : 2 * D] = dc.astype(BF16)
        dbcx_ref[:, 2 * D :] = (du * b).astype(BF16)
        dh = _dot_nt(dbcx_ref[...], win_ref[...])
        xt = x_ref[...]
        r = _rms(xt)
        gn = g_ref[...]
        h_ref[...] = ((xt * r) * gn).astype(BF16)
        dx, dgn = _rms_bwd(xt, r, gn, dh)
        dg_ref[0:1, :] += dgn
        gx_ref[...] = d + dx

    rev = lambda i: (n - 1 - i, 0)
    return _call(
        body,
        after=after,
        args=(dx1, x, gain, bcx, y, cw, w_in, w_out),
        name="conv_bwd",
        grid=(n,),
        in_specs=[
            pl.BlockSpec((tm, D), rev),
            pl.BlockSpec((tm, D), rev),
            _resident((1, D)),
            pl.BlockSpec((tm, 3 * D), rev),
            pl.BlockSpec((tm, D), rev),
            _resident((3, D)),
            _resident((D, 3 * D)),
            _resident((D, D)),
        ],
        out_specs=[
            pl.BlockSpec((tm, D), rev),
            pl.BlockSpec((tm, 3 * D), rev),
            pl.BlockSpec((tm, D), rev),
            pl.BlockSpec((tm, D), rev),
            pl.BlockSpec((8, D), lambda i: (0, 0)),
            pl.BlockSpec((8, D), lambda i: (0, 0)),
        ],
        out_shape=[
            jax.ShapeDtypeStruct((T, D), F32),
            jax.ShapeDtypeStruct((T, 3 * D), BF16),
            jax.ShapeDtypeStruct((T, D), BF16),
            jax.ShapeDtypeStruct((T, D), BF16),
            jax.ShapeDtypeStruct((8, D), F32),
            jax.ShapeDtypeStruct((8, D), F32),
        ],
        scratch=[pltpu.VMEM((8, D), F32)],
    )


MXU_TILE = 256
FFN_CHUNK = 4 * MXU_TILE


def _sigmoid(g):
    return 1.0 / (1.0 + jnp.exp(-g))


def _ffn_chunks(F):
    assert F % MXU_TILE == 0
    return [(s, min(FFN_CHUNK, F - s)) for s in range(0, F, FFN_CHUNK)]


def ffn_fwd(x, gain, w_gu, w_d, *, tm, plan=None, attn=None, target=None):
    T, D = x.shape
    F = w_d.shape[0]
    row = lambda i: (i, 0)
    tile = pl.BlockSpec((tm, D), row)

    def body(*refs):
        refs = list(refs)
        x_ref, g_ref, wgu_ref, wd_ref = refs[:4]
        del refs[:4]
        if attn is not None:
            ao_ref, wo_ref = refs[:2]
            del refs[:2]
        if target is not None:
            t_ref = refs.pop(0)
        if attn is not None:
            xin_ref = refs.pop(0)
        xo_ref, gu_ref = refs[:2]
        xt = x_ref[...]
        if attn is not None:
            xt = xt + _dot(ao_ref[...], wo_ref[...])
            xin_ref[...] = xt
        h = ((xt * _rms(xt)) * g_ref[...]).astype(BF16)
        acc = xt
        for s, n in _ffn_chunks(F):
            g = _dot(h, wgu_ref[:, s : s + n])
            u = _dot(h, wgu_ref[:, F + s : F + s + n])
            gu_ref[:, s : s + n] = g
            gu_ref[:, F + s : F + s + n] = u
            a = ((g * _sigmoid(g)) * u).astype(BF16)
            acc = acc + _dot(a, wd_ref[s : s + n, :])
        if target is None:
            xo_ref[...] = acc
        else:
            s_ref = refs[2]

            @pl.when(pl.program_id(0) == 0)
            def _():
                s_ref[...] = jnp.zeros_like(s_ref)

            e = acc - t_ref[...]
            xo_ref[...] = e * (1.0 / D)
            s_ref[...] += jnp.sum(jnp.sum(e * e, axis=-1, keepdims=True), axis=0, keepdims=True)

    args = [x, gain, w_gu, w_d]
    in_specs = [tile, _resident((1, D)), _resident((D, 2 * F)), _resident((F, D))]
    out_specs = [tile, pl.BlockSpec((tm, 2 * F), row)]
    out_shape = [jax.ShapeDtypeStruct((T, D), F32), jax.ShapeDtypeStruct((T, 2 * F), F32)]
    if attn is not None:
        args += list(attn)
        in_specs += [pl.BlockSpec((tm, attn[0].shape[1]), row), _resident(attn[1].shape)]
        out_specs.insert(0, tile)
        out_shape.insert(0, jax.ShapeDtypeStruct((T, D), F32))
    if target is not None:
        args.append(target)
        in_specs.append(tile)
        out_specs.append(pl.BlockSpec((8, LANES), lambda i: (0, 0)))
        out_shape.append(jax.ShapeDtypeStruct((8, LANES), F32))
    return _call(body, plan=plan, args=args, name="ffn_fwd", grid=(T // tm,), in_specs=in_specs, out_specs=out_specs,
                 out_shape=out_shape)


def ffn_bwd(dxo, x, gain, gu, w_gu, w_d, *, tm, after=None, w_o=None):
    T, D = x.shape
    F = w_d.shape[0]

    def body(d_ref, x_ref, g_ref, gu_ref, wgu_ref, wd_ref, *rest):
        if w_o is not None:
            wo_ref, rest = rest[0], rest[1:]
        dx_ref, a_ref, dgu_ref, h_ref, d16_ref, dg_ref = rest[:6]

        @pl.when(pl.program_id(0) == 0)
        def _():
            dg_ref[...] = jnp.zeros_like(dg_ref)

        d = d_ref[...]
        d16 = d.astype(BF16)
        d16_ref[...] = d16
        dh = jnp.zeros((tm, D), F32)
        for c0, n in _ffn_chunks(F):
            g = gu_ref[:, c0 : c0 + n]
            u = gu_ref[:, F + c0 : F + c0 + n]
            da = _dot_nt(d16, wd_ref[c0 : c0 + n, :])
            s = _sigmoid(g)
            sg = g * s
            a_ref[:, c0 : c0 + n] = (sg * u).astype(BF16)
            dg16 = (da * u * (s + sg * (1.0 - s))).astype(BF16)
            du16 = (da * sg).astype(BF16)
            dgu_ref[:, c0 : c0 + n] = dg16
            dgu_ref[:, F + c0 : F + c0 + n] = du16
            dh = dh + _dot_nt(dg16, wgu_ref[:, c0 : c0 + n]) + _dot_nt(du16, wgu_ref[:, F + c0 : F + c0 + n])
        xt = x_ref[...]
        r = _rms(xt)
        gn = g_ref[...]
        h_ref[...] = ((xt * r) * gn).astype(BF16)
        dx, dgn = _rms_bwd(xt, r, gn, dh)
        dg_ref[0:1, :] += dgn
        dxi = d + dx
        dx_ref[...] = dxi
        if w_o is not None:
            dxi16_ref, dao_ref = rest[6:8]
            dxi16 = dxi.astype(BF16)
            dxi16_ref[...] = dxi16
            dao_ref[...] = _dot_nt(dxi16, wo_ref[...]).astype(BF16)

    tile = pl.BlockSpec((tm, D), lambda i: (i, 0))
    args = [dxo, x, gain, gu, w_gu, w_d]
    wide = lambda n: pl.BlockSpec((tm, n), lambda i: (i, 0))
    in_specs = [tile, tile, _resident((1, D)), wide(2 * F), _resident((D, 2 * F)), _resident((F, D))]
    out_specs = [tile, wide(F), wide(2 * F), tile, tile, pl.BlockSpec((8, D), lambda i: (0, 0))]
    out_shape = [
        jax.ShapeDtypeStruct((T, D), F32),
        jax.ShapeDtypeStruct((T, F), BF16),
        jax.ShapeDtypeStruct((T, 2 * F), BF16),
        jax.ShapeDtypeStruct((T, D), BF16),
        jax.ShapeDtypeStruct((T, D), BF16),
        jax.ShapeDtypeStruct((8, D), F32),
    ]
    if w_o is not None:
        args.append(w_o)
        in_specs.append(_resident(w_o.shape))
        out_specs += [tile, pl.BlockSpec((tm, w_o.shape[0]), lambda i: (i, 0))]
        out_shape += [jax.ShapeDtypeStruct((T, D), BF16), jax.ShapeDtypeStruct((T, w_o.shape[0]), BF16)]
    return _call(body, after=after, args=args, name="ffn_bwd", grid=(T // tm,), in_specs=in_specs, out_specs=out_specs,
                 out_shape=out_shape)


def wgrad(a, b, *, name, a_cols=0, b_cols=0, group=1, flat=False, tk, out_dtype=BF16, after=None):
    T, K = a.shape
    J = 1
    if a_cols:
        K = a_cols
        J = a.shape[1] // K
        a_spec = pl.BlockSpec((tk, K), lambda j, k: (k, j))
    else:
        a_spec = pl.BlockSpec((tk, K), lambda j, k: (k, 0))
    if b_cols:
        N = b_cols * group
        J = b.shape[1] // N
        b_spec = pl.BlockSpec((tk, N), lambda j, k: (k, j))
    else:
        N = b.shape[1]
        b_spec = pl.BlockSpec((tk, N), lambda j, k: (k, 0))
    nk = T // tk
    if flat:
        o_spec, o_shape = pl.BlockSpec((K, N), lambda j, k: (0, j)), (K, J * N)
    elif group > 1:
        o_spec, o_shape = pl.BlockSpec((group, K, b_cols), lambda j, k: (j, 0, 0)), (J * group, K, b_cols)
    else:
        o_spec, o_shape = pl.BlockSpec((None, K, N), lambda j, k: (j, 0, 0)), (J, K, N)

    def body(a_ref, b_ref, o_ref, acc_ref):
        k = pl.program_id(1)

        @pl.when(k == 0)
        def _():
            acc_ref[...] = jnp.zeros_like(acc_ref)

        acc_ref[...] += _dot_tn(a_ref[...], b_ref[...])

        @pl.when(k == nk - 1)
        def _():
            if group > 1 and not flat:
                for i in range(group):
                    o_ref[i] = acc_ref[:, i * b_cols : (i + 1) * b_cols].astype(out_dtype)
            else:
                o_ref[...] = acc_ref[...].astype(out_dtype)

    outs, _ = _call(
        body,
        after=after,
        args=(a, b),
        name=name,
        grid=(J, nk),
        in_specs=[a_spec, b_spec],
        out_specs=[o_spec],
        out_shape=[jax.ShapeDtypeStruct(o_shape, out_dtype)],
        scratch=[pltpu.VMEM((K, N), F32)],
    )
    return outs[0]


def _seg(xs, lo):
    s_lo = [jnp.sum(jnp.where(lo, x, 0.0), axis=-1, keepdims=True) for x in xs]
    s_hi = [jnp.sum(jnp.where(lo, 0.0, x), axis=-1, keepdims=True) for x in xs]
    return [jnp.where(lo, a, b) for a, b in zip(s_lo, s_hi)]


def _head_norm(xs, gains, lo):
    rs = [lax.rsqrt(s * (1.0 / HEAD_DIM) + EPS) for s in _seg([x * x for x in xs], lo)]
    return [(x * r) * g for x, r, g in zip(xs, rs, gains)], rs


def _head_norm_bwd(xs, rs, gains, dys, lo):
    xns = [x * r for x, r in zip(xs, rs)]
    dxns = [dy * g for dy, g in zip(dys, gains)]
    means = [s * (1.0 / HEAD_DIM) for s in _seg([a * b for a, b in zip(dxns, xns)], lo)]
    dxs = [r * (dxn - xn * m) for r, dxn, xn, m in zip(rs, dxns, xns, means)]
    return dxs, [jnp.sum(dy * xn, axis=0, keepdims=True) for dy, xn in zip(dys, xns)]


def _swap_halves(x):
    return pltpu.roll(x, HEAD_DIM, 1)


def qkv_proj(x, gain, w, qg, kg, *, tm):
    T, D = x.shape
    N = w.shape[1]
    kvw = N_KV_HEADS * HEAD_DIM
    nqt, nkt = D // LANES, kvw // LANES

    def body(x_ref, g_ref, w_ref, qg_ref, kg_ref, qkv_ref, q_ref, kd_ref, vd_ref):
        xt = x_ref[...]
        h = ((xt * _rms(xt)) * g_ref[...]).astype(BF16)
        qkv = _dot(h, w_ref[...])
        qkv_ref[...] = qkv
        lo = lax.broadcasted_iota(jnp.int32, (1, LANES), 1) < HEAD_DIM
        tiles = [qkv[:, t * LANES : (t + 1) * LANES] for t in range(nqt + nkt)]
        normed, _ = _head_norm(tiles, [qg_ref[...]] * nqt + [kg_ref[...]] * nkt, lo)
        for t in range(nqt):
            q_ref[:, t * LANES : (t + 1) * LANES] = (normed[t] * SCALE).astype(BF16)
        for t in range(nkt):
            kn = normed[nqt + t]
            v = qkv[:, D + kvw + t * LANES : D + kvw + (t + 1) * LANES]
            for src, dst in ((kn, kd_ref), (v, vd_ref)):
                sw = _swap_halves(src)
                dst[:, 2 * t * LANES : (2 * t + 1) * LANES] = jnp.where(lo, src, sw).astype(BF16)
                dst[:, (2 * t + 1) * LANES : (2 * t + 2) * LANES] = jnp.where(lo, sw, src).astype(BF16)

    row = lambda i: (i, 0)
    return pl.pallas_call(
        body,
        name="qkv_proj",
        grid=(T // tm,),
        in_specs=[pl.BlockSpec((tm, D), row), _resident((1, D)), _resident((D, N)), _resident((1, LANES)), _resident((1, LANES))],
        out_specs=[pl.BlockSpec((tm, N), row), pl.BlockSpec((tm, D), row), pl.BlockSpec((tm, 2 * kvw), row), pl.BlockSpec((tm, 2 * kvw), row)],
        out_shape=[
            jax.ShapeDtypeStruct((T, N), F32),
            jax.ShapeDtypeStruct((T, D), BF16),
            jax.ShapeDtypeStruct((T, 2 * kvw), BF16),
            jax.ShapeDtypeStruct((T, 2 * kvw), BF16),
        ],
        compiler_params=_params(1),
    )(x, gain, w, qg, kg)


def _attn_tables(sinks, n_q_heads):
    P = n_q_heads // N_KV_HEADS // 2
    h = jnp.arange(1, n_q_heads + 1, dtype=F32)
    slopes = jnp.exp2(-8.0 * h / n_q_heads).reshape(N_KV_HEADS, P, 1, 2, 1)
    qi = jnp.arange(BLOCK)[:, None]
    kj = jnp.arange(BLOCK)[None, :]
    dist = jnp.where(kj <= qi, qi - kj, qi + BLOCK - kj).astype(F32)
    shape = (N_KV_HEADS, P, BLOCK, 2, BLOCK)
    bias = jnp.broadcast_to(-slopes * dist[None, None, :, None, :], shape)
    sink = jnp.broadcast_to(sinks.astype(F32).reshape(N_KV_HEADS, P, 1, 2, 1), shape)
    return bias.reshape(N_KV_HEADS, P * BLOCK, 2 * BLOCK), sink.reshape(N_KV_HEADS, P * BLOCK, 2 * BLOCK)


def _attn_specs(D, nb):
    kvw2 = 2 * N_KV_HEADS * HEAD_DIM
    cur = lambda b, i: (b * nb + i, 0)
    prev = lambda b, i: (jnp.maximum(b * nb + i - 1, 0), 0)
    return [
        pl.BlockSpec((BLOCK, D), cur),
        pl.BlockSpec((BLOCK, kvw2), cur),
        pl.BlockSpec((BLOCK, kvw2), prev),
        pl.BlockSpec((BLOCK, kvw2), cur),
        pl.BlockSpec((BLOCK, kvw2), prev),
    ]


def _attn_operands(kh, P, lo, q_ref, kc_ref, kp_ref, vc_ref, vp_ref):
    sl = slice(kh * LANES, (kh + 1) * LANES)

    def cat(prev_ref, cur_ref):
        d = jnp.concatenate([prev_ref[:, sl], cur_ref[:, sl]], axis=0)
        z = jnp.zeros_like(d)
        return jnp.concatenate([jnp.where(lo, d, z), jnp.where(lo, z, d)], axis=0)

    qt = jnp.concatenate([q_ref[:, (kh * P + pr) * LANES : (kh * P + pr + 1) * LANES] for pr in range(P)], axis=0)
    return qt, cat(kp_ref, kc_ref), cat(vp_ref, vc_ref)


def _attn_exp(s_all, bias, sink, tri, first):
    out = []
    for par in range(2):
        c0 = 2 * par * BLOCK
        s = jnp.where(tri, s_all[:, c0 + BLOCK : c0 + 2 * BLOCK], jnp.where(first, NEG, s_all[:, c0 : c0 + BLOCK]))
        s = s + bias[:, par * BLOCK : (par + 1) * BLOCK]
        snk = sink[:, par * BLOCK : (par + 1) * BLOCK]
        m = jnp.maximum(jnp.max(s, axis=-1, keepdims=True), snk)
        out.append((jnp.exp(s - m), jnp.exp(snk - m)))
    return out


def _unfold(x, tri):
    z = jnp.zeros_like(x)
    return jnp.concatenate([jnp.where(tri, z, x), jnp.where(tri, x, z)], axis=1)


def _attn_masks(R):
    lane = lax.broadcasted_iota(jnp.int32, (1, LANES), 1)
    row = lax.broadcasted_iota(jnp.int32, (R, BLOCK), 0) & (BLOCK - 1)
    col = lax.broadcasted_iota(jnp.int32, (R, BLOCK), 1)
    return lane, lane < HEAD_DIM, col <= row


def attn_fwd(q16, kd, vd, bias, sink, *, seq, n_seq):
    T, D = q16.shape
    nb = seq // BLOCK
    P = D // HEAD_DIM // N_KV_HEADS // 2
    R = P * BLOCK
    KV = range(N_KV_HEADS)

    def body(q_ref, kc_ref, kp_ref, vc_ref, vp_ref, bias_ref, sink_ref, o_ref):
        first = pl.program_id(1) == 0
        _, lo, tri = _attn_masks(R)
        r4 = lax.broadcasted_iota(jnp.int32, (4 * BLOCK, LANES), 0)
        l4 = lax.broadcasted_iota(jnp.int32, (4 * BLOCK, LANES), 1)
        ones = ((r4 < 2 * BLOCK) == (l4 < HEAD_DIM)).astype(BF16)
        ops = [_attn_operands(kh, P, lo, q_ref, kc_ref, kp_ref, vc_ref, vp_ref) for kh in KV]
        s_all = [_dot_nt(ops[kh][0], ops[kh][1]) for kh in KV]
        ex = [_attn_exp(s_all[kh], bias_ref[kh], sink_ref[kh], tri, first) for kh in KV]
        lhs = [jnp.concatenate([_unfold(e, tri) for e, _ in ex[kh]], axis=1).astype(BF16) for kh in KV]
        o = [_dot(lhs[kh], ops[kh][2]) for kh in KV]
        den = [_dot(lhs[kh], ones) for kh in KV]
        for kh in KV:
            out = o[kh] / (den[kh] + jnp.where(lo, ex[kh][0][1], ex[kh][1][1]))
            for pr in range(P):
                t = kh * P + pr
                o_ref[:, t * LANES : (t + 1) * LANES] = out[pr * BLOCK : (pr + 1) * BLOCK, :].astype(BF16)

    return pl.pallas_call(
        body,
        name="attn_fwd",
        grid=(n_seq, nb),
        in_specs=_attn_specs(D, nb) + [_resident((N_KV_HEADS, R, 2 * BLOCK)), _resident((N_KV_HEADS, R, 2 * BLOCK))],
        out_specs=pl.BlockSpec((BLOCK, D), lambda b, i: (b * nb + i, 0)),
        out_shape=jax.ShapeDtypeStruct((T, D), BF16),
        compiler_params=_params(2),
    )(q16, kd, kd, vd, vd, bias, sink)


def attn_bwd(q16, kd, vd, do, bias, sink, *, seq, n_seq):
    T, D = q16.shape
    kvw2 = 2 * N_KV_HEADS * HEAD_DIM
    nb = seq // BLOCK
    G = D // HEAD_DIM // N_KV_HEADS
    P = G // 2
    R = P * BLOCK
    KV = range(N_KV_HEADS)

    def body(q_ref, kc_ref, kp_ref, vc_ref, vp_ref, do_ref, bias_ref, sink_ref,
             dq_ref, dkc_ref, dkp_ref, dvc_ref, dvp_ref, dsink_ref):
        first = pl.program_id(1) == 0

        @pl.when(jnp.logical_and(pl.program_id(0) == 0, first))
        def _():
            dsink_ref[...] = jnp.zeros_like(dsink_ref)

        lane, lo, tri = _attn_masks(R)
        r4 = lax.broadcasted_iota(jnp.int32, (4 * BLOCK, 2 * BLOCK), 0)
        c4 = lax.broadcasted_iota(jnp.int32, (4 * BLOCK, 2 * BLOCK), 1)
        ones = ((r4 < 2 * BLOCK) == (c4 < BLOCK)).astype(BF16)
        ops = [_attn_operands(kh, P, lo, q_ref, kc_ref, kp_ref, vc_ref, vp_ref) for kh in KV]
        do16 = [jnp.concatenate([do_ref[:, (kh * P + pr) * LANES : (kh * P + pr + 1) * LANES] for pr in range(P)], axis=0)
                for kh in KV]
        s_all = [_dot_nt(ops[kh][0], ops[kh][1]) for kh in KV]
        dp_all = [_dot_nt(do16[kh], ops[kh][2]) for kh in KV]
        ex = [_attn_exp(s_all[kh], bias_ref[kh], sink_ref[kh], tri, first) for kh in KV]
        den = [_dot(jnp.concatenate([_unfold(e, tri) for e, _ in ex[kh]], axis=1).astype(BF16), ones) for kh in KV]
        dsink = jnp.zeros((1, LANES), F32)
        pf, dsf = [], []
        for kh in KV:
            ps_, ds_ = [], []
            for par in range(2):
                e, es = ex[kh][par]
                inv = 1.0 / (den[kh][:, par * BLOCK : (par + 1) * BLOCK] + es)
                p = e * inv
                c0 = 2 * par * BLOCK
                dp = jnp.where(tri, dp_all[kh][:, c0 + BLOCK : c0 + 2 * BLOCK], dp_all[kh][:, c0 : c0 + BLOCK])
                delta = jnp.sum(p * dp, axis=-1, keepdims=True)
                ds_.append(_unfold(p * (dp - delta), tri))
                ps_.append(_unfold(p, tri))
                dsr = -((es * inv) * delta)
                for pr in range(P):
                    hq = kh * G + 2 * pr + par
                    tot = jnp.sum(dsr[pr * BLOCK : (pr + 1) * BLOCK, :], axis=0, keepdims=True)
                    dsink = dsink + jnp.where(lane == hq, tot, 0.0)
            pf.append(jnp.concatenate(ps_, axis=1).astype(BF16))
            dsf.append(jnp.concatenate(ds_, axis=1).astype(BF16))
        dq = [_dot(dsf[kh], ops[kh][1]) for kh in KV]
        dk = [_dot_tn(dsf[kh], ops[kh][0]) for kh in KV]
        dv = [_dot_tn(pf[kh], do16[kh]) for kh in KV]
        dsink_ref[0:1, :] += dsink
        for kh in KV:
            sl = slice(kh * LANES, (kh + 1) * LANES)
            for pr in range(P):
                t = kh * P + pr
                dq_ref[:, t * LANES : (t + 1) * LANES] = dq[kh][pr * BLOCK : (pr + 1) * BLOCK, :]
            for full, prev_ref, cur_ref in ((dk[kh], dkp_ref, dkc_ref), (dv[kh], dvp_ref, dvc_ref)):
                dup = jnp.where(lo, full[: 2 * BLOCK, :], full[2 * BLOCK :, :])
                prev_ref[:, sl] = dup[:BLOCK, :]
                cur_ref[:, sl] = dup[BLOCK:, :]

    cur = lambda b, i: (b * nb + i, 0)
    kv_spec = pl.BlockSpec((BLOCK, kvw2), cur)
    kv_shape = jax.ShapeDtypeStruct((T, kvw2), F32)
    return pl.pallas_call(
        body,
        name="attn_bwd",
        grid=(n_seq, nb),
        in_specs=_attn_specs(D, nb)
        + [pl.BlockSpec((BLOCK, D), cur), _resident((N_KV_HEADS, R, 2 * BLOCK)), _resident((N_KV_HEADS, R, 2 * BLOCK))],
        out_specs=[pl.BlockSpec((BLOCK, D), cur), kv_spec, kv_spec, kv_spec, kv_spec, pl.BlockSpec((8, LANES), lambda b, i: (0, 0))],
        out_shape=[jax.ShapeDtypeStruct((T, D), F32), kv_shape, kv_shape, kv_shape, kv_shape, jax.ShapeDtypeStruct((8, LANES), F32)],
        compiler_params=_params(2),
    )(q16, kd, kd, vd, vd, do, bias, sink)


def qkv_bwd(dq, dkc, dkp, dvc, dvp, qkv, dres, x, gain, w_qkv, qg, kg, *, seq):
    T, D = x.shape
    kvw2 = dkc.shape[1]
    kvw = kvw2 // 2
    nqt, nkt = D // LANES, kvw // LANES
    nb = seq // BLOCK
    tm = 2 * BLOCK
    n = T // tm

    def body(dq_ref, dkc_ref, dkpa_ref, dkpb_ref, dvc_ref, dvpa_ref, dvpb_ref, qkv_ref, dres_ref, x_ref, g_ref, w_ref,
             qg_ref, kg_ref, dx_ref, dqkv_ref, h_ref, dg_ref, hg_ref):
        i = pl.program_id(0)

        @pl.when(i == 0)
        def _():
            dg_ref[...] = jnp.zeros_like(dg_ref)
            hg_ref[...] = jnp.zeros_like(hg_ref)

        lo = lax.broadcasted_iota(jnp.int32, (1, LANES), 1) < HEAD_DIM
        last = (2 * i + 1) % nb == nb - 1
        dkd = dkc_ref[...] + jnp.concatenate([dkpa_ref[...], jnp.where(last, 0.0, dkpb_ref[...])], axis=0)
        dvd = dvc_ref[...] + jnp.concatenate([dvpa_ref[...], jnp.where(last, 0.0, dvpb_ref[...])], axis=0)

        def undup(d, t):
            a, b = d[:, 2 * t * LANES : (2 * t + 1) * LANES], d[:, (2 * t + 1) * LANES : (2 * t + 2) * LANES]
            return jnp.where(lo, a + _swap_halves(a), b + _swap_halves(b))

        tiles = [qkv_ref[:, t * LANES : (t + 1) * LANES] for t in range(nqt + nkt)]
        gains = [qg_ref[...]] * nqt + [kg_ref[...]] * nkt
        dys = [dq_ref[:, t * LANES : (t + 1) * LANES] * SCALE for t in range(nqt)] + [undup(dkd, t) for t in range(nkt)]
        _, rs = _head_norm(tiles, gains, lo)
        dxs, dgs = _head_norm_bwd(tiles, rs, gains, dys, lo)
        for t in range(nqt + nkt):
            dqkv_ref[:, t * LANES : (t + 1) * LANES] = dxs[t].astype(BF16)
        for t in range(nkt):
            dqkv_ref[:, D + kvw + t * LANES : D + kvw + (t + 1) * LANES] = undup(dvd, t).astype(BF16)
        hg_ref[0:1, :] += functools.reduce(lambda a, b: a + b, dgs[:nqt])
        hg_ref[1:2, :] += functools.reduce(lambda a, b: a + b, dgs[nqt:])
        dh = _dot_nt(dqkv_ref[...], w_ref[...])
        xt = x_ref[...]
        r = _rms(xt)
        gn = g_ref[...]
        h_ref[...] = ((xt * r) * gn).astype(BF16)
        dx, dgn = _rms_bwd(xt, r, gn, dh)
        dg_ref[0:1, :] += dgn
        dx_ref[...] = dres_ref[...] + dx

    row = lambda i: (i, 0)
    nxt_a = pl.BlockSpec((BLOCK, kvw2), lambda i: (2 * i + 1, 0))
    nxt_b = pl.BlockSpec((BLOCK, kvw2), lambda i: (jnp.minimum(2 * i + 2, 2 * n - 1), 0))
    return pl.pallas_call(
        body,
        name="qkv_bwd",
        grid=(n,),
        in_specs=[
            pl.BlockSpec((tm, D), row),
            pl.BlockSpec((tm, kvw2), row),
            nxt_a,
            nxt_b,
            pl.BlockSpec((tm, kvw2), row),
            nxt_a,
            nxt_b,
            pl.BlockSpec((tm, D + kvw2), row),
            pl.BlockSpec((tm, D), row),
            pl.BlockSpec((tm, D), row),
            _resident((1, D)),
            _resident((D, D + kvw2)),
            _resident((1, LANES)),
            _resident((1, LANES)),
        ],
        out_specs=[
            pl.BlockSpec((tm, D), row),
            pl.BlockSpec((tm, D + kvw2), row),
            pl.BlockSpec((tm, D), row),
            pl.BlockSpec((8, D), lambda i: (0, 0)),
            pl.BlockSpec((8, LANES), lambda i: (0, 0)),
        ],
        out_shape=[
            jax.ShapeDtypeStruct((T, D), F32),
            jax.ShapeDtypeStruct((T, D + kvw2), BF16),
            jax.ShapeDtypeStruct((T, D), BF16),
            jax.ShapeDtypeStruct((8, D), F32),
            jax.ShapeDtypeStruct((8, LANES), F32),
        ],
        compiler_params=_params(1),
    )(dq, dkc, dkp, dkp, dvc, dvp, dvp, qkv, dres, x, gain, w_qkv, qg, kg)


def local_step(x, target, gains, w, *, seq, tm=256, tm_ffn=256, tm_conv=512, tk=2048, shards=None, ex=None):
    T, D = x.shape
    n_seq = T // seq
    nm, nf, qgain, kgain, sinks = gains
    H = D // HEAD_DIM
    tk, tk_long = min(tk, T), min(2 * tk, T)
    qg2, kg2 = jnp.tile(qgain, (1, 2)), jnp.tile(kgain, (1, 2))
    bias, sinkcol = _attn_tables(sinks, H)

    dist = shards is not None
    w = dict(w)

    plan = _Gather([shards["w_gu"][0], shards["w_d"][0]]) if dist else None
    (x1, bcx, y_conv, z16), got = conv_fwd(x, nm[0:1], w["w_in"], w["cw"], w["w_out"], seq=seq, tm=tm_conv, plan=plan)
    if dist:
        w["w_gu"], w["w_d"] = [cols_from_shards(got[0]), None], [got[1].reshape(-1, D), None]
    plan = _Gather([shards["w_qkv"], shards["w_o"], shards["w_gu"][1], shards["w_d"][1]]) if dist else None
    (x2, gu0), got = ffn_fwd(x1, nf[0:1], w["w_gu"][0], w["w_d"][0], tm=tm_ffn, plan=plan)
    if dist:
        w["w_qkv"], w["w_o"] = cols_from_shards(got[0]), got[1].reshape(D, D)
        w["w_gu"][1], w["w_d"][1] = cols_from_shards(got[2]), got[3].reshape(-1, D)
    qkv, q16, kd, vd = qkv_proj(x2, nm[1:2], w["w_qkv"], qg2, kg2, tm=tm_conv)
    ao = attn_fwd(q16, kd, vd, bias, sinkcol, seq=seq, n_seq=n_seq)
    (x3, dx4, gu1, sse), _ = ffn_fwd(x2, nf[1:2], w["w_gu"][1], w["w_d"][1], tm=tm_ffn, attn=(ao, w["w_o"]), target=target)

    by_dest = lambda a: a.reshape(N_DEV, -1, a.shape[-1])
    gu_cols = 2 * MXU_TILE

    def send(name, *entries):
        if ex is None:
            return None
        items = [(a, False, key, (N_DEV,) + (() if layers is None else (layers,)) + a.shape[1:], layer)
                 for a, key, layer, layers in entries]
        return ex.start(items, name=name)

    (dx3, a16, dgu, h16, d16, dnf1, dx3_16, dao), _ = ffn_bwd(
        dx4, x3, nf[1:2], gu1, w["w_gu"][1], w["w_d"][1], tm=tm, w_o=w["w_o"])
    g_gu1 = shards_from_cols(wgrad(h16, dgu, name="wgrad_gu1", b_cols=gu_cols, flat=True, tk=tk_long))
    g_d1 = by_dest(wgrad(a16, d16, name="wgrad_d1", a_cols=a16.shape[1] // 2, tk=tk))
    tok = send("exchange_ffn1", (g_gu1, "w_gu", 1, 2), (g_d1, "w_d", 1, 2))
    g_o = by_dest(wgrad(ao, dx3_16, name="wgrad_o", tk=tk, after=tok))
    dq, dkc, dkp, dvc, dvp, dsinks = attn_bwd(q16, kd, vd, dao, bias, sinkcol, seq=seq, n_seq=n_seq)
    dx2, dqkv16, h16, dnm1, dgains = qkv_bwd(dq, dkc, dkp, dvc, dvp, qkv, dx3, x2, nm[1:2], w["w_qkv"], qg2, kg2, seq=seq)
    g_qkv = shards_from_cols(wgrad(h16, dqkv16, name="wgrad_qkv", tk=tk)[0])
    tok = send("exchange_attn", (g_o, "w_o", None, None), (g_qkv, "w_qkv", None, None))
    (dx1, a16, dgu, h16, d16, dnf0), _ = ffn_bwd(dx2, x1, nf[0:1], gu0, w["w_gu"][0], w["w_d"][0], tm=tm, after=tok)
    g_gu0 = shards_from_cols(wgrad(h16, dgu, name="wgrad_gu0", b_cols=gu_cols, flat=True, tk=tk_long))
    tok = send("exchange_gu0", (g_gu0, "w_gu", 0, 2))
    g_d0 = by_dest(wgrad(a16, d16, name="wgrad_d0", a_cols=a16.shape[1] // 2, tk=tk, after=tok))
    tok = send("exchange_d0", (g_d0, "w_d", 0, 2))
    (gx, dbcx, h16, d16, dcw, dnm0), _ = conv_bwd(
        dx1, x, nm[0:1], bcx, y_conv, w["cw"], w["w_in"], w["w_out"], seq=seq, tm=tm_conv, after=tok)
    g_out = by_dest(wgrad(z16, d16, name="wgrad_out", tk=tk))
    g_cw = dcw[0:3].reshape(3, N_DEV, D // N_DEV).transpose(1, 0, 2)
    tok = send("exchange_out", (g_out, "w_out", None, None), (g_cw, "cw", None, None))
    g_in = wgrad(h16, dbcx, name="wgrad_in", b_cols=3 * D // N_DEV, group=2, tk=tk_long, after=tok)
    g = dict(w_in=g_in, cw=g_cw, w_out=g_out, w_o=g_o, w_qkv=g_qkv, w_gu=[g_gu0, g_gu1], w_d=[g_d0, g_d1])
    small = dict(nm0=dnm0, nm1=dnm1, nf0=dnf0, nf1=dnf1, gains=dgains, sinks=dsinks)
    return sse, gx, g, small


def _adamw_math(g, w, m, v):
    m = ADAM_B1 * m + (1.0 - ADAM_B1) * g
    v = ADAM_B2 * v + (1.0 - ADAM_B2) * (g * g)
    m_hat = m / (1.0 - ADAM_B1 ** ADAM_STEP)
    v_hat = v / (1.0 - ADAM_B2 ** ADAM_STEP)
    delta = -ADAM_LR * (m_hat / (jnp.sqrt(v_hat) + ADAM_EPS) + ADAM_WD * w)
    return delta, m, v


def adamw(parts, owns, w, m, v, *, name, after=None):
    n, LR, C = parts.shape
    L = len(owns)
    R = LR // L
    tr = R
    for cand in (256, 128, 88, 64, 32, 16, 8):
        if R > cand and R % cand == 0:
            tr = cand
            break
    per_layer = R // tr
    extra = [] if after is None else [after]

    def body(me_ref, p_ref, *rest):
        own_refs, (w_ref, m_ref, v_ref) = rest[:L], rest[L : L + 3]
        g_ref, d_ref, mo_ref, vo_ref = rest[L + 3 + len(extra) :]
        layer = pl.program_id(0) // per_layer
        mine = own_refs[0][...].astype(F32)
        for j in range(1, L):
            mine = jnp.where(layer == j, own_refs[j][...].astype(F32), mine)
        g = None
        for s in range(n):
            share = jnp.where(me_ref[0] == s, mine, p_ref[s].astype(F32))
            g = share if g is None else g + share
        g_ref[...] = g
        d_ref[...], mo_ref[...], vo_ref[...] = _adamw_math(g, w_ref[...], m_ref[...], v_ref[...])

    blk = pl.BlockSpec((tr, C), lambda i, me: (i, 0))
    own_specs = [pl.BlockSpec((None, tr, C), lambda i, me: (me[0], i % per_layer, 0)) if o.ndim == 3
                 else pl.BlockSpec((tr, C), lambda i, me: (i % per_layer, 0)) for o in owns]
    me = (4 * lax.axis_index("x") + 2 * lax.axis_index("y") + lax.axis_index("c")).astype(jnp.int32).reshape(1)
    return pl.pallas_call(
        body,
        name=name,
        grid_spec=pltpu.PrefetchScalarGridSpec(
            num_scalar_prefetch=1,
            grid=(LR // tr,),
            in_specs=[pl.BlockSpec((n, tr, C), lambda i, me: (0, i, 0))] + own_specs + [blk, blk, blk] + _any_specs(len(extra)),
            out_specs=[blk] * 4,
        ),
        out_shape=[jax.ShapeDtypeStruct((LR, C), F32)] * 4,
        compiler_params=_params(1),
    )(me, parts, *owns, w, m, v, *extra)


def pack_small(small, sse, D):
    W = max(D, 2 * LANES)

    def body(nm0, nm1, nf0, nf1, gains, sinks, sse_ref, o_ref):
        o_ref[...] = jnp.zeros_like(o_ref)
        o_ref[0:1, :D] = nm0[0:1, :]
        o_ref[1:2, :D] = nm1[0:1, :]
        o_ref[2:3, :D] = nf0[0:1, :]
        o_ref[3:4, :D] = nf1[0:1, :]
        gq = gains[0:1, :] + pltpu.roll(gains[0:1, :], HEAD_DIM, 1)
        gk = gains[1:2, :] + pltpu.roll(gains[1:2, :], HEAD_DIM, 1)
        lane = lax.broadcasted_iota(jnp.int32, (1, LANES), 1)
        o_ref[4:5, :LANES] = jnp.where(lane < HEAD_DIM, gq, gk)
        o_ref[4:5, LANES : 2 * LANES] = sinks[0:1, :]
        o_ref[5:6, :LANES] = sse_ref[0:1, :] * (0.5 / D)

    return pl.pallas_call(
        body,
        name="pack_small",
        out_shape=jax.ShapeDtypeStruct((8, W), F32),
    )(small["nm0"], small["nm1"], small["nf0"], small["nf1"], small["gains"], small["sinks"], sse)


def _pack_small_params(nm, nf, qg, kg, sk, D):
    W = max(D, 2 * LANES)
    row4 = jnp.concatenate([qg.reshape(-1), kg.reshape(-1), jnp.zeros((LANES - 2 * HEAD_DIM,), F32), sk.reshape(-1)])
    row4 = jnp.pad(row4, (0, W - row4.shape[0]))
    rows = [jnp.pad(r, (0, W - D)) for r in (nm[0], nm[1], nf[0], nf[1])] + [row4]
    return jnp.concatenate([jnp.stack(rows), jnp.zeros((3, W), F32)], axis=0)


def _unpack_small(a, D, H):
    nm = a[0:2, :D]
    nf = a[2:4, :D]
    qg = a[4:5, 0:HEAD_DIM]
    kg = a[4:5, HEAD_DIM : 2 * HEAD_DIM]
    sk = a[4:5, LANES : LANES + H]
    return qg, kg, sk, nm, nf


def kernel(x, conv_w_in, conv_w, conv_w_out, attn_w_qkv, attn_q_gain, attn_k_gain, attn_sinks, attn_w_o, norm_mixer, norm_ffn, ffn_w_gate_up, ffn_w_down, loss_target, m_conv_w_in, m_conv_w, m_conv_w_out, m_attn_w_qkv, m_attn_q_gain, m_attn_k_gain, m_attn_sinks, m_attn_w_o, m_norm_mixer, m_norm_ffn, m_ffn_w_gate_up, m_ffn_w_down, v_conv_w_in, v_conv_w, v_conv_w_out, v_attn_w_qkv, v_attn_q_gain, v_attn_k_gain, v_attn_sinks, v_attn_w_o, v_norm_mixer, v_norm_ffn, v_ffn_w_gate_up, v_ffn_w_down):
    n_seq, seq, D = x.shape
    T = n_seq * seq
    H = D // HEAD_DIM
    L = ffn_w_gate_up.shape[0]

    full = run_plan(_Gather([conv_w_in[0].astype(BF16), conv_w[0], conv_w_out[0].astype(BF16)]), name="gather_conv_weights")
    w = dict(w_in=cols_from_shards(full[0]), cw=full[1].transpose(1, 0, 2).reshape(3, D),
             w_out=full[2].reshape(D, D))
    shards = dict(w_gu=[ffn_w_gate_up[l].astype(BF16) for l in range(L)], w_d=[ffn_w_down[l].astype(BF16) for l in range(L)],
                  w_qkv=attn_w_qkv[0].astype(BF16), w_o=attn_w_o[0].astype(BF16))
    gains = (norm_mixer, norm_ffn, attn_q_gain, attn_k_gain, attn_sinks)
    ex = Exchange()
    sse, gx, g, small = local_step(x.reshape(T, D), loss_target.reshape(T, D), gains, w, seq=seq, shards=shards, ex=ex)
    zones, own = ex.wait([g["w_in"]], name="exchange_wait")

    packed = pack_small(small, sse, D)
    token = ex.start([(g["w_in"], False, "w_in", g["w_in"].shape, None),
                      (packed, True, "small", (N_DEV,) + packed.shape, None)], name="exchange_last")

    def flat(a):
        return a.reshape(-1, a.shape[-1])

    big = [conv_w_in, conv_w, conv_w_out, attn_w_qkv, attn_w_o, ffn_w_gate_up, ffn_w_down]
    big_m = [m_conv_w_in, m_conv_w, m_conv_w_out, m_attn_w_qkv, m_attn_w_o, m_ffn_w_gate_up, m_ffn_w_down]
    big_v = [v_conv_w_in, v_conv_w, v_conv_w_out, v_attn_w_qkv, v_attn_w_o, v_ffn_w_gate_up, v_ffn_w_down]
    keys = ["w_in", "cw", "w_out", "w_qkv", "w_o", "w_gu", "w_d"]

    def update(b, zones, own, after=None):
        zone = zones[keys[b]]
        parts = zone.reshape(N_DEV, -1, zone.shape[-1])
        layers = [None] if zone.ndim == 3 else range(zone.shape[1])
        outs = adamw(parts, [own[(keys[b], l)] for l in layers], flat(big[b]), flat(big_m[b]), flat(big_v[b]),
                     name="adamw_" + keys[b], after=after)
        return [o.reshape(big[b].shape) for o in outs]

    res = [None] + [update(b, zones, own, after=token) for b in range(1, 7)]
    zones, own = ex.wait([r[0] for r in res[1:]], name="exchange_last_wait")
    res[0] = update(0, zones, own)
    sw = _pack_small_params(norm_mixer, norm_ffn, attn_q_gain, attn_k_gain, attn_sinks, D)
    sm = _pack_small_params(m_norm_mixer, m_norm_ffn, m_attn_q_gain, m_attn_k_gain, m_attn_sinks, D)
    sv = _pack_small_params(v_norm_mixer, v_norm_ffn, v_attn_q_gain, v_attn_k_gain, v_attn_sinks, D)
    souts = adamw(zones["small"], [own[("small", None)]], sw, sm, sv, name="adamw_small")
    sres = [_unpack_small(o, D, H) for o in souts]
    loss = souts[0][5, 0]

    def ordered(i):
        r, s = [r[i] for r in res], sres[i]
        return [r[0], r[1], r[2], r[3], s[0], s[1], s[2], r[4], s[3], s[4], r[5], r[6]]

    return (loss, gx.reshape(n_seq, seq, D), *ordered(0), *ordered(1), *ordered(2), *ordered(3))
```

```python
import functools
import math

import jax
import jax.numpy as jnp
from jax import lax
from jax.experimental import pallas as pl
from jax.experimental.pallas import tpu as pltpu

F32 = jnp.float32
BF16 = jnp.bfloat16

EPS = 1e-6
HEAD_DIM = 64
N_KV_HEADS = 4
BLOCK = 128
LANES = 128
N_DEV = 8
NEG = -1e30
SCALE = 1.0 / math.sqrt(HEAD_DIM)

ADAM_LR = 0.001
ADAM_B1 = 0.9
ADAM_B2 = 0.999
ADAM_EPS = 1e-08
ADAM_WD = 0.01
ADAM_STEP = 10

V7X_VMEM_BYTES = 64 * 1024 * 1024
VMEM_LIMIT = V7X_VMEM_BYTES - 2 * 1024 * 1024
MESH = pl.DeviceIdType.MESH

_NT = (((1,), (1,)), ((), ()))
_TN = (((0,), (0,)), ((), ()))


def _params(n_grid):
    return pltpu.CompilerParams(dimension_semantics=("arbitrary",) * n_grid, vmem_limit_bytes=VMEM_LIMIT)


def _resident(shape):
    nd = len(shape)
    return pl.BlockSpec(shape, lambda *_: (0,) * nd, pipeline_mode=pl.Buffered(1))


def _rms(x):
    return lax.rsqrt(jnp.mean(x * x, axis=-1, keepdims=True) + EPS)


def _rms_bwd(x, r, gain, dh):
    xn = x * r
    dxn = dh * gain
    dx = r * (dxn - xn * jnp.mean(dxn * xn, axis=-1, keepdims=True))
    return dx, jnp.sum(dh * xn, axis=0, keepdims=True)


def _dot(a, b):
    return jnp.dot(a, b, preferred_element_type=F32)


def _dot_nt(a, b):
    return lax.dot_general(a, b, _NT, preferred_element_type=F32)


def _dot_tn(a, b):
    return lax.dot_general(a, b, _TN, preferred_element_type=F32)


def _place():
    return lax.axis_index("x"), lax.axis_index("y"), lax.axis_index("c")


def _flip(v, bit):
    return 1 - v if bit else v


def _slot(px, py, pc):
    return 4 * px + 2 * py + pc


class _Gather:
    def __init__(self, shards):
        nt = len(shards)
        self.nt = nt
        self.inputs = list(shards)
        self.out_shapes = [jax.ShapeDtypeStruct((N_DEV,) + s.shape, s.dtype) for s in shards]
        self.scratch = [pltpu.SemaphoreType.DMA((nt, 7)), pltpu.SemaphoreType.DMA((nt, 7)), pltpu.SemaphoreType.DMA((nt,))]
        self.aliases = {}

    def phases(self, total):
        assert total >= 3
        return [(0, self.start), (total - 2, self.forward), (total - 1, self.finish)]

    def _copies(self, ins, outs, sems):
        send_sems, recv_sems, loc_sems = sems
        x, y, c = _place()
        me = _slot(x, y, c)
        sib = (x, y, 1 - c)
        chips = [(_flip(x, k >> 1), _flip(y, k & 1)) for k in (1, 2, 3)]

        def copy(t, k, src, dst_slot, to):
            return pltpu.make_async_remote_copy(
                src_ref=src, dst_ref=outs[t].at[dst_slot], send_sem=send_sems.at[t, k], recv_sem=recv_sems.at[t, k],
                device_id=to, device_id_type=MESH)

        local = [pltpu.make_async_copy(ins[t], outs[t].at[me], loc_sems.at[t]) for t in range(self.nt)]
        first, passed, arrive_ici, arrive_sib = [], [], [], []
        for t in range(self.nt):
            first.append(copy(t, 0, ins[t], me, sib))
            s = _slot(x, y, 1 - c)
            arrive_sib.append(copy(t, 0, outs[t].at[s], s, sib))
            for j, (px, py) in enumerate(chips):
                first.append(copy(t, 1 + j, ins[t], me, (px, py, c)))
                s = _slot(px, py, c)
                arrive_ici.append(copy(t, 1 + j, outs[t].at[s], s, sib))
                passed.append(copy(t, 4 + j, outs[t].at[s], s, sib))
                s = _slot(px, py, 1 - c)
                arrive_sib.append(copy(t, 4 + j, outs[t].at[s], s, sib))
        return local, first, passed, arrive_ici, arrive_sib

    def start(self, ins, outs, sems):
        local, first, _, _, _ = self._copies(ins, outs, sems)
        for cp in local + first:
            cp.start()

    def forward(self, ins, outs, sems):
        _, _, passed, arrive_ici, _ = self._copies(ins, outs, sems)
        for arrival, fwd in zip(arrive_ici, passed):
            arrival.wait_recv()
            fwd.start()

    def finish(self, ins, outs, sems):
        local, first, passed, _, arrive_sib = self._copies(ins, outs, sems)
        for cp in arrive_sib:
            cp.wait_recv()
        for cp in first + passed:
            cp.wait_send()
        for cp in local:
            cp.wait()


def _any_specs(n):
    return [pl.BlockSpec(memory_space=pl.ANY)] * n


def run_plan(plan, *, name):
    def body(*refs):
        n_in, n_out = len(plan.inputs), len(plan.out_shapes)
        ins, outs, sems = refs[:n_in], refs[n_in : n_in + n_out], refs[n_in + n_out :]
        for _, phase in plan.phases(3):
            phase(ins, outs, sems)

    return pl.pallas_call(
        body,
        name=name,
        in_specs=_any_specs(len(plan.inputs)),
        out_specs=_any_specs(len(plan.out_shapes)),
        out_shape=plan.out_shapes,
        scratch_shapes=plan.scratch,
        input_output_aliases=plan.aliases,
    )(*plan.inputs)


_HBM = pl.BlockSpec(memory_space=pltpu.HBM)
_SEM = pl.BlockSpec(memory_space=pltpu.SEMAPHORE)
_DATAFLOW = pltpu.SideEffectType.DATAFLOW_SIDE_EFFECTING


class Exchange:
    def __init__(self):
        self.zones = {}
        self.pending = []
        self.sources = []

    def start(self, items, *, name):
        nt = len(items)
        keys = list(dict.fromkeys(it[2] for it in items))
        for a, _, key, shape, _ in items:
            if key not in self.zones:
                self.zones[key] = lax.empty(shape, a.dtype)
        nz = len(keys)

        def body(*refs):
            ins, zones, sems, token = refs[:nt], refs[nt : nt + nz], refs[nt + nz : nt + nz + 2 * nt], refs[-1]
            x, y, c = _place()
            me = _slot(x, y, c)
            for k in range(1, N_DEV):
                px, py, pc = _flip(x, (k >> 2) & 1), _flip(y, (k >> 1) & 1), _flip(c, k & 1)
                for t, (_, whole, key, _, layer) in enumerate(items):
                    zone = zones[keys.index(key)]
                    pltpu.make_async_remote_copy(
                        src_ref=ins[t] if whole else ins[t].at[_slot(px, py, pc)],
                        dst_ref=zone.at[me] if layer is None else zone.at[me, layer],
                        send_sem=sems[2 * t], recv_sem=sems[2 * t + 1], device_id=(px, py, pc), device_id_type=MESH).start()
            token[...] = jnp.zeros_like(token)

        bufs = [pltpu.with_memory_space_constraint(b, pltpu.HBM) for b in [it[0] for it in items] + [self.zones[k] for k in keys]]
        outs = pl.pallas_call(
            body,
            name=name,
            in_specs=[_HBM] * (nt + nz),
            out_specs=[_SEM] * (2 * nt) + [_HBM] * (nt + nz) + [pl.BlockSpec(memory_space=pltpu.VMEM)],
            out_shape=[pltpu.SemaphoreType.DMA(())] * (2 * nt) + [pltpu.HBM(b.shape, b.dtype) for b in bufs]
            + [jax.ShapeDtypeStruct((8, LANES), F32)],
            input_output_aliases={i: 2 * nt + i for i in range(nt + nz)},
            compiler_params=pltpu.CompilerParams(has_side_effects=_DATAFLOW),
        )(*bufs)
        for t, (_, _, key, _, layer) in enumerate(items):
            self.pending.append((outs[2 * t], outs[2 * t + 1], key, layer))
        self.sources += [((it[2], it[4]), a) for it, a in zip(items, outs[2 * nt : 3 * nt])]
        for i, key in enumerate(keys):
            self.zones[key] = outs[3 * nt + i]
        return outs[-1]

    def wait(self, after, *, name):
        pending, keys = self.pending, list(self.zones)
        names, sources = [n for n, _ in self.sources], [a for _, a in self.sources]
        ns, nz, npend = len(sources), len(keys), len(pending)
        self.pending, self.sources = [], []

        def body(*refs):
            zones, sems = refs[ns : ns + nz], refs[ns + nz : ns + nz + 2 * npend]
            x, y, c = _place()
            for i, (_, _, key, layer) in enumerate(pending):
                zone = zones[keys.index(key)]
                rows = pl.ds(0, N_DEV - 1)
                seven = zone.at[rows] if layer is None else zone.at[rows, layer]
                pltpu.make_async_remote_copy(
                    src_ref=seven, dst_ref=seven, send_sem=sems[2 * i], recv_sem=sems[2 * i + 1],
                    device_id=(x, y, c), device_id_type=MESH).wait()

        bufs = list(sources) + [self.zones[k] for k in keys]
        flat_sems = [s for p in pending for s in p[:2]]
        outs = pl.pallas_call(
            body,
            name=name,
            in_specs=[_HBM] * (ns + nz) + [_SEM] * (2 * npend) + _any_specs(len(after)),
            out_specs=[_HBM] * (ns + nz),
            out_shape=[pltpu.HBM(b.shape, b.dtype) for b in bufs],
            input_output_aliases={i: i for i in range(ns + nz)},
            compiler_params=pltpu.CompilerParams(has_side_effects=_DATAFLOW),
        )(*bufs, *flat_sems, *after)
        self.zones = {}
        return dict(zip(keys, outs[ns:])), dict(zip(names, outs[:ns]))


def _call(body, *, name, grid, in_specs, out_specs, out_shape, args, scratch=(), plan=None, after=None):
    if after is not None:
        inner, n_real = body, len(in_specs)
        body = lambda *refs: inner(*refs[:n_real], *refs[n_real + 1 :])
        in_specs, args = list(in_specs) + _any_specs(1), list(args) + [after]
    n_in, n_out, n_scr = len(in_specs), len(out_specs), len(scratch)
    if plan is None:
        outs = pl.pallas_call(
            body, name=name, grid=grid, in_specs=in_specs, out_specs=out_specs, out_shape=out_shape,
            scratch_shapes=list(scratch), compiler_params=_params(len(grid)))(*args)
        return outs, None
    c_in, c_out = len(plan.inputs), len(plan.out_shapes)
    phases = plan.phases(math.prod(grid))

    def full(*refs):
        a, refs = refs[:n_in], refs[n_in:]
        ci, refs = refs[:c_in], refs[c_in:]
        o, refs = refs[:n_out], refs[n_out:]
        co, refs = refs[:c_out], refs[c_out:]
        s, cs = refs[:n_scr], refs[n_scr:]
        step = pl.program_id(0)
        for d in range(1, len(grid)):
            step = step * grid[d] + pl.program_id(d)
        for at, phase in phases:
            if at == 0:
                pl.when(step == 0)(functools.partial(phase, ci, co, cs))
        body(*a, *o, *s)
        for at, phase in phases:
            if at > 0:
                pl.when(step == at)(functools.partial(phase, ci, co, cs))

    outs = pl.pallas_call(
        full,
        name=name,
        grid=grid,
        in_specs=list(in_specs) + _any_specs(c_in),
        out_specs=list(out_specs) + _any_specs(c_out),
        out_shape=list(out_shape) + plan.out_shapes,
        scratch_shapes=list(scratch) + plan.scratch,
        input_output_aliases={n_in + i: n_out + t for i, t in plan.aliases.items()},
        compiler_params=_params(len(grid)),
    )(*args, *plan.inputs)
    return outs[:n_out], outs[n_out:]


def _row_tile(R):
    return 256 if R % 256 == 0 else R


def cols_from_shards(a):
    n, R, C = a.shape
    tr = _row_tile(R)

    def body(i_ref, o_ref):
        for s in range(n):
            o_ref[:, s * C : (s + 1) * C] = i_ref[s]

    return pl.pallas_call(
        body,
        name="cols_from_shards",
        grid=(R // tr,),
        in_specs=[pl.BlockSpec((n, tr, C), lambda i: (0, i, 0))],
        out_specs=pl.BlockSpec((tr, n * C), lambda i: (i, 0)),
        out_shape=jax.ShapeDtypeStruct((R, n * C), a.dtype),
        compiler_params=_params(1),
    )(a)


def shards_from_cols(a):
    R, W = a.shape
    C = W // N_DEV
    tr = _row_tile(R)

    def body(i_ref, o_ref):
        for s in range(N_DEV):
            o_ref[s] = i_ref[:, s * C : (s + 1) * C]

    return pl.pallas_call(
        body,
        name="shards_from_cols",
        grid=(R // tr,),
        in_specs=[pl.BlockSpec((tr, W), lambda i: (i, 0))],
        out_specs=pl.BlockSpec((N_DEV, tr, C), lambda i: (0, i, 0)),
        out_shape=jax.ShapeDtypeStruct((N_DEV, R, C), a.dtype),
        compiler_params=_params(1),
    )(a)


def _shift_down(u, prev8, row, n):
    out = pltpu.roll(u, n, 0)
    for k in range(n):
        out = jnp.where(row == k, prev8[8 - n + k : 8 - n + k + 1, :], out)
    return out


def _shift_up(u, next8, row, n, tm):
    out = pltpu.roll(u, tm - n, 0)
    for k in range(n):
        out = jnp.where(row == tm - n + k, next8[k : k + 1, :], out)
    return out


def conv_fwd(x, gain, w_in, cw, w_out, *, seq, tm, plan=None):
    T, D = x.shape
    tps = seq // tm

    def body(x_ref, g_ref, win_ref, cw_ref, wout_ref, x1_ref, bcx_ref, y_ref, z_ref, carry_ref):
        i = pl.program_id(0)

        @pl.when(i % tps == 0)
        def _():
            carry_ref[...] = jnp.zeros_like(carry_ref)

        xt = x_ref[...]
        h = ((xt * _rms(xt)) * g_ref[...]).astype(BF16)
        bcx = _dot(h, win_ref[...])
        bcx_ref[...] = bcx
        b, c, xv = bcx[:, :D], bcx[:, D : 2 * D], bcx[:, 2 * D :]
        u = b * xv
        row = lax.broadcasted_iota(jnp.int32, u.shape, 0)
        prev = carry_ref[...]
        u1 = _shift_down(u, prev, row, 1)
        u2 = _shift_down(u, prev, row, 2)
        carry_ref[...] = u[tm - 8 :, :]
        cwv = cw_ref[...]
        y = cwv[0:1, :] * u2 + cwv[1:2, :] * u1 + cwv[2:3, :] * u
        y_ref[...] = y
        z = (c * y).astype(BF16)
        z_ref[...] = z
        x1_ref[...] = xt + _dot(z, wout_ref[...])

    tile = pl.BlockSpec((tm, D), lambda i: (i, 0))
    return _call(
        body,
        plan=plan,
        args=(x, gain, w_in, cw, w_out),
        name="conv_fwd",
        grid=(T // tm,),
        in_specs=[
            pl.BlockSpec((tm, D), lambda i: (i, 0)),
            _resident((1, D)),
            _resident((D, 3 * D)),
            _resident((3, D)),
            _resident((D, D)),
        ],
        out_specs=[tile, pl.BlockSpec((tm, 3 * D), lambda i: (i, 0)), tile, tile],
        out_shape=[jax.ShapeDtypeStruct((T, D), F32), jax.ShapeDtypeStruct((T, 3 * D), F32),
                   jax.ShapeDtypeStruct((T, D), F32), jax.ShapeDtypeStruct((T, D), BF16)],
        scratch=[pltpu.VMEM((8, D), F32)],
    )


def conv_bwd(dx1, x, gain, bcx, y, cw, w_in, w_out, *, seq, tm, after=None):
    T, D = x.shape
    n = T // tm
    tps = seq // tm

    def body(d_ref, x_ref, g_ref, bcx_ref, y_ref, cw_ref, win_ref, wout_ref,
             gx_ref, dbcx_ref, h_ref, d16_ref, dcw_ref, dg_ref, carry_ref):
        i = pl.program_id(0)
        t = n - 1 - i

        @pl.when(i == 0)
        def _():
            dcw_ref[...] = jnp.zeros_like(dcw_ref)
            dg_ref[...] = jnp.zeros_like(dg_ref)

        @pl.when(t % tps == tps - 1)
        def _():
            carry_ref[...] = jnp.zeros_like(carry_ref)

        d = d_ref[...]
        d16 = d.astype(BF16)
        d16_ref[...] = d16
        dz = _dot_nt(d16, wout_ref[...])
        bcx = bcx_ref[...]
        b, c, xv = bcx[:, :D], bcx[:, D : 2 * D], bcx[:, 2 * D :]
        u = b * xv
        row = lax.broadcasted_iota(jnp.int32, u.shape, 0)
        cwv = cw_ref[...]
        dc = dz * y_ref[...]
        dy = dz * c
        nxt = carry_ref[...]
        dy1 = _shift_up(dy, nxt, row, 1, tm)
        dy2 = _shift_up(dy, nxt, row, 2, tm)
        carry_ref[...] = dy[0:8, :]
        dcw_ref[0:1, :] += jnp.sum(dy2 * u, axis=0, keepdims=True)
        dcw_ref[1:2, :] += jnp.sum(dy1 * u, axis=0, keepdims=True)
        dcw_ref[2:3, :] += jnp.sum(dy * u, axis=0, keepdims=True)
        du = cwv[2:3, :] * dy + cwv[1:2, :] * dy1 + cwv[0:1, :] * dy2
        dbcx_ref[:, :D] = (du * xv).astype(BF16)
        dbcx_ref[:, D : 2 * D] = dc.astype(BF16)
        dbcx_ref[:, 2 * D :] = (du * b).astype(BF16)
        dh = _dot_nt(dbcx_ref[...], win_ref[...])
        xt = x_ref[...]
        r = _rms(xt)
        gn = g_ref[...]
        h_ref[...] = ((xt * r) * gn).astype(BF16)
        dx, dgn = _rms_bwd(xt, r, gn, dh)
        dg_ref[0:1, :] += dgn
        gx_ref[...] = d + dx

    rev = lambda i: (n - 1 - i, 0)
    return _call(
        body,
        after=after,
        args=(dx1, x, gain, bcx, y, cw, w_in, w_out),
        name="conv_bwd",
        grid=(n,),
        in_specs=[
            pl.BlockSpec((tm, D), rev),
            pl.BlockSpec((tm, D), rev),
            _resident((1, D)),
            pl.BlockSpec((tm, 3 * D), rev),
            pl.BlockSpec((tm, D), rev),
            _resident((3, D)),
            _resident((D, 3 * D)),
            _resident((D, D)),
        ],
        out_specs=[
            pl.BlockSpec((tm, D), rev),
            pl.BlockSpec((tm, 3 * D), rev),
            pl.BlockSpec((tm, D), rev),
            pl.BlockSpec((tm, D), rev),
            pl.BlockSpec((8, D), lambda i: (0, 0)),
            pl.BlockSpec((8, D), lambda i: (0, 0)),
        ],
        out_shape=[
            jax.ShapeDtypeStruct((T, D), F32),
            jax.ShapeDtypeStruct((T, 3 * D), BF16),
            jax.ShapeDtypeStruct((T, D), BF16),
            jax.ShapeDtypeStruct((T, D), BF16),
            jax.ShapeDtypeStruct((8, D), F32),
            jax.ShapeDtypeStruct((8, D), F32),
        ],
        scratch=[pltpu.VMEM((8, D), F32)],
    )


MXU_TILE = 256
FFN_CHUNK = 4 * MXU_TILE


def _sigmoid(g):
    return 1.0 / (1.0 + jnp.exp(-g))


def _ffn_chunks(F):
    assert F % MXU_TILE == 0
    return [(s, min(FFN_CHUNK, F - s)) for s in range(0, F, FFN_CHUNK)]


def ffn_fwd(x, gain, w_gu, w_d, *, tm, plan=None, attn=None, target=None):
    T, D = x.shape
    F = w_d.shape[0]
    row = lambda i: (i, 0)
    tile = pl.BlockSpec((tm, D), row)

    def body(*refs):
        refs = list(refs)
        x_ref, g_ref, wgu_ref, wd_ref = refs[:4]
        del refs[:4]
        if attn is not None:
            ao_ref, wo_ref = refs[:2]
            del refs[:2]
        if target is not None:
            t_ref = refs.pop(0)
        if attn is not None:
            xin_ref = refs.pop(0)
        xo_ref, gu_ref = refs[:2]
        xt = x_ref[...]
        if attn is not None:
            xt = xt + _dot(ao_ref[...], wo_ref[...])
            xin_ref[...] = xt
        h = ((xt * _rms(xt)) * g_ref[...]).astype(BF16)
        acc = xt
        for s, n in _ffn_chunks(F):
            g = _dot(h, wgu_ref[:, s : s + n])
            u = _dot(h, wgu_ref[:, F + s : F + s + n])
            gu_ref[:, s : s + n] = g
            gu_ref[:, F + s : F + s + n] = u
            a = ((g * _sigmoid(g)) * u).astype(BF16)
            acc = acc + _dot(a, wd_ref[s : s + n, :])
        if target is None:
            xo_ref[...] = acc
        else:
            s_ref = refs[2]

            @pl.when(pl.program_id(0) == 0)
            def _():
                s_ref[...] = jnp.zeros_like(s_ref)

            e = acc - t_ref[...]
            xo_ref[...] = e * (1.0 / D)
            s_ref[...] += jnp.sum(jnp.sum(e * e, axis=-1, keepdims=True), axis=0, keepdims=True)

    args = [x, gain, w_gu, w_d]
    in_specs = [tile, _resident((1, D)), _resident((D, 2 * F)), _resident((F, D))]
    out_specs = [tile, pl.BlockSpec((tm, 2 * F), row)]
    out_shape = [jax.ShapeDtypeStruct((T, D), F32), jax.ShapeDtypeStruct((T, 2 * F), F32)]
    if attn is not None:
        args += list(attn)
        in_specs += [pl.BlockSpec((tm, attn[0].shape[1]), row), _resident(attn[1].shape)]
        out_specs.insert(0, tile)
        out_shape.insert(0, jax.ShapeDtypeStruct((T, D), F32))
    if target is not None:
        args.append(target)
        in_specs.append(tile)
        out_specs.append(pl.BlockSpec((8, LANES), lambda i: (0, 0)))
        out_shape.append(jax.ShapeDtypeStruct((8, LANES), F32))
    return _call(body, plan=plan, args=args, name="ffn_fwd", grid=(T // tm,), in_specs=in_specs, out_specs=out_specs,
                 out_shape=out_shape)


def ffn_bwd(dxo, x, gain, gu, w_gu, w_d, *, tm, after=None, w_o=None):
    T, D = x.shape
    F = w_d.shape[0]

    def body(d_ref, x_ref, g_ref, gu_ref, wgu_ref, wd_ref, *rest):
        if w_o is not None:
            wo_ref, rest = rest[0], rest[1:]
        dx_ref, a_ref, dgu_ref, h_ref, d16_ref, dg_ref = rest[:6]

        @pl.when(pl.program_id(0) == 0)
        def _():
            dg_ref[...] = jnp.zeros_like(dg_ref)

        d = d_ref[...]
        d16 = d.astype(BF16)
        d16_ref[...] = d16
        dh = jnp.zeros((tm, D), F32)
        for c0, n in _ffn_chunks(F):
            g = gu_ref[:, c0 : c0 + n]
            u = gu_ref[:, F + c0 : F + c0 + n]
            da = _dot_nt(d16, wd_ref[c0 : c0 + n, :])
            s = _sigmoid(g)
            sg = g * s
            a_ref[:, c0 : c0 + n] = (sg * u).astype(BF16)
            dg16 = (da * u * (s + sg * (1.0 - s))).astype(BF16)
            du16 = (da * sg).astype(BF16)
            dgu_ref[:, c0 : c0 + n] = dg16
            dgu_ref[:, F + c0 : F + c0 + n] = du16
            dh = dh + _dot_nt(dg16, wgu_ref[:, c0 : c0 + n]) + _dot_nt(du16, wgu_ref[:, F + c0 : F + c0 + n])
        xt = x_ref[...]
        r = _rms(xt)
        gn = g_ref[...]
        h_ref[...] = ((xt * r) * gn).astype(BF16)
        dx, dgn = _rms_bwd(xt, r, gn, dh)
        dg_ref[0:1, :] += dgn
        dxi = d + dx
        dx_ref[...] = dxi
        if w_o is not None:
            dxi16_ref, dao_ref = rest[6:8]
            dxi16 = dxi.astype(BF16)
            dxi16_ref[...] = dxi16
            dao_ref[...] = _dot_nt(dxi16, wo_ref[...]).astype(BF16)

    tile = pl.BlockSpec((tm, D), lambda i: (i, 0))
    args = [dxo, x, gain, gu, w_gu, w_d]
    wide = lambda n: pl.BlockSpec((tm, n), lambda i: (i, 0))
    in_specs = [tile, tile, _resident((1, D)), wide(2 * F), _resident((D, 2 * F)), _resident((F, D))]
    out_specs = [tile, wide(F), wide(2 * F), tile, tile, pl.BlockSpec((8, D), lambda i: (0, 0))]
    out_shape = [
        jax.ShapeDtypeStruct((T, D), F32),
        jax.ShapeDtypeStruct((T, F), BF16),
        jax.ShapeDtypeStruct((T, 2 * F), BF16),
        jax.ShapeDtypeStruct((T, D), BF16),
        jax.ShapeDtypeStruct((T, D), BF16),
        jax.ShapeDtypeStruct((8, D), F32),
    ]
    if w_o is not None:
        args.append(w_o)
        in_specs.append(_resident(w_o.shape))
        out_specs += [tile, pl.BlockSpec((tm, w_o.shape[0]), lambda i: (i, 0))]
        out_shape += [jax.ShapeDtypeStruct((T, D), BF16), jax.ShapeDtypeStruct((T, w_o.shape[0]), BF16)]
    return _call(body, after=after, args=args, name="ffn_bwd", grid=(T // tm,), in_specs=in_specs, out_specs=out_specs,
                 out_shape=out_shape)


def wgrad(a, b, *, name, a_cols=0, b_cols=0, group=1, flat=False, tk, out_dtype=BF16, after=None):
    T, K = a.shape
    J = 1
    if a_cols:
        K = a_cols
        J = a.shape[1] // K
        a_spec = pl.BlockSpec((tk, K), lambda j, k: (k, j))
    else:
        a_spec = pl.BlockSpec((tk, K), lambda j, k: (k, 0))
    if b_cols:
        N = b_cols * group
        J = b.shape[1] // N
        b_spec = pl.BlockSpec((tk, N), lambda j, k: (k, j))
    else:
        N = b.shape[1]
        b_spec = pl.BlockSpec((tk, N), lambda j, k: (k, 0))
    nk = T // tk
    if flat:
        o_spec, o_shape = pl.BlockSpec((K, N), lambda j, k: (0, j)), (K, J * N)
    elif group > 1:
        o_spec, o_shape = pl.BlockSpec((group, K, b_cols), lambda j, k: (j, 0, 0)), (J * group, K, b_cols)
    else:
        o_spec, o_shape = pl.BlockSpec((None, K, N), lambda j, k: (j, 0, 0)), (J, K, N)

    def body(a_ref, b_ref, o_ref, acc_ref):
        k = pl.program_id(1)

        @pl.when(k == 0)
        def _():
            acc_ref[...] = jnp.zeros_like(acc_ref)

        acc_ref[...] += _dot_tn(a_ref[...], b_ref[...])

        @pl.when(k == nk - 1)
        def _():
            if group > 1 and not flat:
                for i in range(group):
                    o_ref[i] = acc_ref[:, i * b_cols : (i + 1) * b_cols].astype(out_dtype)
            else:
                o_ref[...] = acc_ref[...].astype(out_dtype)

    outs, _ = _call(
        body,
        after=after,
        name=name,
        grid=(J, nk),
        in_specs=[a_spec, b_spec],
        out_specs=[o_spec],
        out_shape=[jax.ShapeDtypeStruct(o_shape, out_dtype)],
        args=(a, b),
        scratch=[pltpu.VMEM((K, N), F32)],
    )
    return outs[0]


def _seg(xs, lo):
    s_lo = [jnp.sum(jnp.where(lo, x, 0.0), axis=-1, keepdims=True) for x in xs]
    s_hi = [jnp.sum(jnp.where(lo, 0.0, x), axis=-1, keepdims=True) for x in xs]
    return [jnp.where(lo, a, b) for a, b in zip(s_lo, s_hi)]


def _head_norm(xs, gains, lo):
    rs = [lax.rsqrt(s * (1.0 / HEAD_DIM) + EPS) for s in _seg([x * x for x in xs], lo)]
    return [(x * r) * g for x, r, g in zip(xs, rs, gains)], rs


def _head_norm_bwd(xs, rs, gains, dys, lo):
    xns = [x * r for x, r in zip(xs, rs)]
    dxns = [dy * g for dy, g in zip(dys, gains)]
    means = [s * (1.0 / HEAD_DIM) for s in _seg([a * b for a, b in zip(dxns, xns)], lo)]
    dxs = [r * (dxn - xn * m) for r, dxn, xn, m in zip(rs, dxns, xns, means)]
    return dxs, [jnp.sum(dy * xn, axis=0, keepdims=True) for dy, xn in zip(dys, xns)]


def _swap_halves(x):
    return pltpu.roll(x, HEAD_DIM, 1)


def qkv_proj(x, gain, w, qg, kg, *, tm):
    T, D = x.shape
    N = w.shape[1]
    kvw = N_KV_HEADS * HEAD_DIM
    nqt, nkt = D // LANES, kvw // LANES

    def body(x_ref, g_ref, w_ref, qg_ref, kg_ref, qkv_ref, q_ref, kd_ref, vd_ref):
        xt = x_ref[...]
        h = ((xt * _rms(xt)) * g_ref[...]).astype(BF16)
        qkv = _dot(h, w_ref[...])
        qkv_ref[...] = qkv
        lo = lax.broadcasted_iota(jnp.int32, (1, LANES), 1) < HEAD_DIM
        tiles = [qkv[:, t * LANES : (t + 1) * LANES] for t in range(nqt + nkt)]
        normed, _ = _head_norm(tiles, [qg_ref[...]] * nqt + [kg_ref[...]] * nkt, lo)
        for t in range(nqt):
            q_ref[:, t * LANES : (t + 1) * LANES] = (normed[t] * SCALE).astype(BF16)
        for t in range(nkt):
            kn = normed[nqt + t]
            v = qkv[:, D + kvw + t * LANES : D + kvw + (t + 1) * LANES]
            for src, dst in ((kn, kd_ref), (v, vd_ref)):
                sw = _swap_halves(src)
                dst[:, 2 * t * LANES : (2 * t + 1) * LANES] = jnp.where(lo, src, sw).astype(BF16)
                dst[:, (2 * t + 1) * LANES : (2 * t + 2) * LANES] = jnp.where(lo, sw, src).astype(BF16)

    row = lambda i: (i, 0)
    return pl.pallas_call(
        body,
        name="qkv_proj",
        grid=(T // tm,),
        in_specs=[pl.BlockSpec((tm, D), row), _resident((1, D)), _resident((D, N)), _resident((1, LANES)), _resident((1, LANES))],
        out_specs=[pl.BlockSpec((tm, N), row), pl.BlockSpec((tm, D), row), pl.BlockSpec((tm, 2 * kvw), row), pl.BlockSpec((tm, 2 * kvw), row)],
        out_shape=[
            jax.ShapeDtypeStruct((T, N), F32),
            jax.ShapeDtypeStruct((T, D), BF16),
            jax.ShapeDtypeStruct((T, 2 * kvw), BF16),
            jax.ShapeDtypeStruct((T, 2 * kvw), BF16),
        ],
        compiler_params=_params(1),
    )(x, gain, w, qg, kg)


def _attn_tables(sinks, n_q_heads):
    P = n_q_heads // N_KV_HEADS // 2
    h = jnp.arange(1, n_q_heads + 1, dtype=F32)
    slopes = jnp.exp2(-8.0 * h / n_q_heads).reshape(N_KV_HEADS, P, 1, 2, 1)
    qi = jnp.arange(BLOCK)[:, None]
    kj = jnp.arange(BLOCK)[None, :]
    dist = jnp.where(kj <= qi, qi - kj, qi + BLOCK - kj).astype(F32)
    shape = (N_KV_HEADS, P, BLOCK, 2, BLOCK)
    bias = jnp.broadcast_to(-slopes * dist[None, None, :, None, :], shape)
    sink = jnp.broadcast_to(sinks.astype(F32).reshape(N_KV_HEADS, P, 1, 2, 1), shape)
    return bias.reshape(N_KV_HEADS, P * BLOCK, 2 * BLOCK), sink.reshape(N_KV_HEADS, P * BLOCK, 2 * BLOCK)


def _attn_specs(D, nb):
    kvw2 = 2 * N_KV_HEADS * HEAD_DIM
    cur = lambda b, i: (b * nb + i, 0)
    prev = lambda b, i: (jnp.maximum(b * nb + i - 1, 0), 0)
    return [
        pl.BlockSpec((BLOCK, D), cur),
        pl.BlockSpec((BLOCK, kvw2), cur),
        pl.BlockSpec((BLOCK, kvw2), prev),
        pl.BlockSpec((BLOCK, kvw2), cur),
        pl.BlockSpec((BLOCK, kvw2), prev),
    ]


def _attn_operands(kh, P, lo, q_ref, kc_ref, kp_ref, vc_ref, vp_ref):
    sl = slice(kh * LANES, (kh + 1) * LANES)

    def cat(prev_ref, cur_ref):
        d = jnp.concatenate([prev_ref[:, sl], cur_ref[:, sl]], axis=0)
        z = jnp.zeros_like(d)
        return jnp.concatenate([jnp.where(lo, d, z), jnp.where(lo, z, d)], axis=0)

    qt = jnp.concatenate([q_ref[:, (kh * P + pr) * LANES : (kh * P + pr + 1) * LANES] for pr in range(P)], axis=0)
    return qt, cat(kp_ref, kc_ref), cat(vp_ref, vc_ref)


def _attn_exp(s_all, bias, sink, tri, first):
    out = []
    for par in range(2):
        c0 = 2 * par * BLOCK
        s = jnp.where(tri, s_all[:, c0 + BLOCK : c0 + 2 * BLOCK], jnp.where(first, NEG, s_all[:, c0 : c0 + BLOCK]))
        s = s + bias[:, par * BLOCK : (par + 1) * BLOCK]
        snk = sink[:, par * BLOCK : (par + 1) * BLOCK]
        m = jnp.maximum(jnp.max(s, axis=-1, keepdims=True), snk)
        out.append((jnp.exp(s - m), jnp.exp(snk - m)))
    return out


def _unfold(x, tri):
    z = jnp.zeros_like(x)
    return jnp.concatenate([jnp.where(tri, z, x), jnp.where(tri, x, z)], axis=1)


def _attn_masks(R):
    lane = lax.broadcasted_iota(jnp.int32, (1, LANES), 1)
    row = lax.broadcasted_iota(jnp.int32, (R, BLOCK), 0) & (BLOCK - 1)
    col = lax.broadcasted_iota(jnp.int32, (R, BLOCK), 1)
    return lane, lane < HEAD_DIM, col <= row


def attn_fwd(q16, kd, vd, bias, sink, *, seq, n_seq):
    T, D = q16.shape
    nb = seq // BLOCK
    P = D // HEAD_DIM // N_KV_HEADS // 2
    R = P * BLOCK
    KV = range(N_KV_HEADS)

    def body(q_ref, kc_ref, kp_ref, vc_ref, vp_ref, bias_ref, sink_ref, o_ref):
        first = pl.program_id(1) == 0
        _, lo, tri = _attn_masks(R)
        r4 = lax.broadcasted_iota(jnp.int32, (4 * BLOCK, LANES), 0)
        l4 = lax.broadcasted_iota(jnp.int32, (4 * BLOCK, LANES), 1)
        ones = ((r4 < 2 * BLOCK) == (l4 < HEAD_DIM)).astype(BF16)
        ops = [_attn_operands(kh, P, lo, q_ref, kc_ref, kp_ref, vc_ref, vp_ref) for kh in KV]
        s_all = [_dot_nt(ops[kh][0], ops[kh][1]) for kh in KV]
        ex = [_attn_exp(s_all[kh], bias_ref[kh], sink_ref[kh], tri, first) for kh in KV]
        lhs = [jnp.concatenate([_unfold(e, tri) for e, _ in ex[kh]], axis=1).astype(BF16) for kh in KV]
        o = [_dot(lhs[kh], ops[kh][2]) for kh in KV]
        den = [_dot(lhs[kh], ones) for kh in KV]
        for kh in KV:
            out = o[kh] / (den[kh] + jnp.where(lo, ex[kh][0][1], ex[kh][1][1]))
            for pr in range(P):
                t = kh * P + pr
                o_ref[:, t * LANES : (t + 1) * LANES] = out[pr * BLOCK : (pr + 1) * BLOCK, :].astype(BF16)

    return pl.pallas_call(
        body,
        name="attn_fwd",
        grid=(n_seq, nb),
        in_specs=_attn_specs(D, nb) + [_resident((N_KV_HEADS, R, 2 * BLOCK)), _resident((N_KV_HEADS, R, 2 * BLOCK))],
        out_specs=pl.BlockSpec((BLOCK, D), lambda b, i: (b * nb + i, 0)),
        out_shape=jax.ShapeDtypeStruct((T, D), BF16),
        compiler_params=_params(2),
    )(q16, kd, kd, vd, vd, bias, sink)


def attn_bwd(q16, kd, vd, do, bias, sink, *, seq, n_seq):
    T, D = q16.shape
    kvw2 = 2 * N_KV_HEADS * HEAD_DIM
    nb = seq // BLOCK
    G = D // HEAD_DIM // N_KV_HEADS
    P = G // 2
    R = P * BLOCK
    KV = range(N_KV_HEADS)

    def body(q_ref, kc_ref, kp_ref, vc_ref, vp_ref, do_ref, bias_ref, sink_ref,
             dq_ref, dkc_ref, dkp_ref, dvc_ref, dvp_ref, dsink_ref):
        first = pl.program_id(1) == 0

        @pl.when(jnp.logical_and(pl.program_id(0) == 0, first))
        def _():
            dsink_ref[...] = jnp.zeros_like(dsink_ref)

        lane, lo, tri = _attn_masks(R)
        r4 = lax.broadcasted_iota(jnp.int32, (4 * BLOCK, 2 * BLOCK), 0)
        c4 = lax.broadcasted_iota(jnp.int32, (4 * BLOCK, 2 * BLOCK), 1)
        ones = ((r4 < 2 * BLOCK) == (c4 < BLOCK)).astype(BF16)
        ops = [_attn_operands(kh, P, lo, q_ref, kc_ref, kp_ref, vc_ref, vp_ref) for kh in KV]
        do16 = [jnp.concatenate([do_ref[:, (kh * P + pr) * LANES : (kh * P + pr + 1) * LANES] for pr in range(P)], axis=0)
                for kh in KV]
        s_all = [_dot_nt(ops[kh][0], ops[kh][1]) for kh in KV]
        dp_all = [_dot_nt(do16[kh], ops[kh][2]) for kh in KV]
        ex = [_attn_exp(s_all[kh], bias_ref[kh], sink_ref[kh], tri, first) for kh in KV]
        den = [_dot(jnp.concatenate([_unfold(e, tri) for e, _ in ex[kh]], axis=1).astype(BF16), ones) for kh in KV]
        dsink = jnp.zeros((1, LANES), F32)
        pf, dsf = [], []
        for kh in KV:
            ps_, ds_ = [], []
            for par in range(2):
                e, es = ex[kh][par]
                inv = 1.0 / (den[kh][:, par * BLOCK : (par + 1) * BLOCK] + es)
                p = e * inv
                c0 = 2 * par * BLOCK
                dp = jnp.where(tri, dp_all[kh][:, c0 + BLOCK : c0 + 2 * BLOCK], dp_all[kh][:, c0 : c0 + BLOCK])
                delta = jnp.sum(p * dp, axis=-1, keepdims=True)
                ds_.append(_unfold(p * (dp - delta), tri))
                ps_.append(_unfold(p, tri))
                dsr = -((es * inv) * delta)
                for pr in range(P):
                    hq = kh * G + 2 * pr + par
                    tot = jnp.sum(dsr[pr * BLOCK : (pr + 1) * BLOCK, :], axis=0, keepdims=True)
                    dsink = dsink + jnp.where(lane == hq, tot, 0.0)
            pf.append(jnp.concatenate(ps_, axis=1).astype(BF16))
            dsf.append(jnp.concatenate(ds_, axis=1).astype(BF16))
        dq = [_dot(dsf[kh], ops[kh][1]) for kh in KV]
        dk = [_dot_tn(dsf[kh], ops[kh][0]) for kh in KV]
        dv = [_dot_tn(pf[kh], do16[kh]) for kh in KV]
        dsink_ref[0:1, :] += dsink
        for kh in KV:
            sl = slice(kh * LANES, (kh + 1) * LANES)
            for pr in range(P):
                t = kh * P + pr
                dq_ref[:, t * LANES : (t + 1) * LANES] = dq[kh][pr * BLOCK : (pr + 1) * BLOCK, :]
            for full, prev_ref, cur_ref in ((dk[kh], dkp_ref, dkc_ref), (dv[kh], dvp_ref, dvc_ref)):
                dup = jnp.where(lo, full[: 2 * BLOCK, :], full[2 * BLOCK :, :])
                prev_ref[:, sl] = dup[:BLOCK, :]
                cur_ref[:, sl] = dup[BLOCK:, :]

    cur = lambda b, i: (b * nb + i, 0)
    kv_spec = pl.BlockSpec((BLOCK, kvw2), cur)
    kv_shape = jax.ShapeDtypeStruct((T, kvw2), F32)
    return pl.pallas_call(
        body,
        name="attn_bwd",
        grid=(n_seq, nb),
        in_specs=_attn_specs(D, nb)
        + [pl.BlockSpec((BLOCK, D), cur), _resident((N_KV_HEADS, R, 2 * BLOCK)), _resident((N_KV_HEADS, R, 2 * BLOCK))],
        out_specs=[pl.BlockSpec((BLOCK, D), cur), kv_spec, kv_spec, kv_spec, kv_spec, pl.BlockSpec((8, LANES), lambda b, i: (0, 0))],
        out_shape=[jax.ShapeDtypeStruct((T, D), F32), kv_shape, kv_shape, kv_shape, kv_shape, jax.ShapeDtypeStruct((8, LANES), F32)],
        compiler_params=_params(2),
    )(q16, kd, kd, vd, vd, do, bias, sink)


def qkv_bwd(dq, dkc, dkp, dvc, dvp, qkv, dres, x, gain, w_qkv, qg, kg, *, seq):
    T, D = x.shape
    kvw2 = dkc.shape[1]
    kvw = kvw2 // 2
    nqt, nkt = D // LANES, kvw // LANES
    nb = seq // BLOCK
    tm = 2 * BLOCK
    n = T // tm

    def body(dq_ref, dkc_ref, dkpa_ref, dkpb_ref, dvc_ref, dvpa_ref, dvpb_ref, qkv_ref, dres_ref, x_ref, g_ref, w_ref,
             qg_ref, kg_ref, dx_ref, dqkv_ref, h_ref, dg_ref, hg_ref):
        i = pl.program_id(0)

        @pl.when(i == 0)
        def _():
            dg_ref[...] = jnp.zeros_like(dg_ref)
            hg_ref[...] = jnp.zeros_like(hg_ref)

        lo = lax.broadcasted_iota(jnp.int32, (1, LANES), 1) < HEAD_DIM
        last = (2 * i + 1) % nb == nb - 1
        dkd = dkc_ref[...] + jnp.concatenate([dkpa_ref[...], jnp.where(last, 0.0, dkpb_ref[...])], axis=0)
        dvd = dvc_ref[...] + jnp.concatenate([dvpa_ref[...], jnp.where(last, 0.0, dvpb_ref[...])], axis=0)

        def undup(d, t):
            a, b = d[:, 2 * t * LANES : (2 * t + 1) * LANES], d[:, (2 * t + 1) * LANES : (2 * t + 2) * LANES]
            return jnp.where(lo, a + _swap_halves(a), b + _swap_halves(b))

        tiles = [qkv_ref[:, t * LANES : (t + 1) * LANES] for t in range(nqt + nkt)]
        gains = [qg_ref[...]] * nqt + [kg_ref[...]] * nkt
        dys = [dq_ref[:, t * LANES : (t + 1) * LANES] * SCALE for t in range(nqt)] + [undup(dkd, t) for t in range(nkt)]
        _, rs = _head_norm(tiles, gains, lo)
        dxs, dgs = _head_norm_bwd(tiles, rs, gains, dys, lo)
        for t in range(nqt + nkt):
            dqkv_ref[:, t * LANES : (t + 1) * LANES] = dxs[t].astype(BF16)
        for t in range(nkt):
            dqkv_ref[:, D + kvw + t * LANES : D + kvw + (t + 1) * LANES] = undup(dvd, t).astype(BF16)
        hg_ref[0:1, :] += functools.reduce(lambda a, b: a + b, dgs[:nqt])
        hg_ref[1:2, :] += functools.reduce(lambda a, b: a + b, dgs[nqt:])
        dh = _dot_nt(dqkv_ref[...], w_ref[...])
        xt = x_ref[...]
        r = _rms(xt)
        gn = g_ref[...]
        h_ref[...] = ((xt * r) * gn).astype(BF16)
        dx, dgn = _rms_bwd(xt, r, gn, dh)
        dg_ref[0:1, :] += dgn
        dx_ref[...] = dres_ref[...] + dx

    row = lambda i: (i, 0)
    nxt_a = pl.BlockSpec((BLOCK, kvw2), lambda i: (2 * i + 1, 0))
    nxt_b = pl.BlockSpec((BLOCK, kvw2), lambda i: (jnp.minimum(2 * i + 2, 2 * n - 1), 0))
    return pl.pallas_call(
        body,
        name="qkv_bwd",
        grid=(n,),
        in_specs=[
            pl.BlockSpec((tm, D), row),
            pl.BlockSpec((tm, kvw2), row),
            nxt_a,
            nxt_b,
            pl.BlockSpec((tm, kvw2), row),
            nxt_a,
            nxt_b,
            pl.BlockSpec((tm, D + kvw2), row),
            pl.BlockSpec((tm, D), row),
            pl.BlockSpec((tm, D), row),
            _resident((1, D)),
            _resident((D, D + kvw2)),
            _resident((1, LANES)),
            _resident((1, LANES)),
        ],
        out_specs=[
            pl.BlockSpec((tm, D), row),
            pl.BlockSpec((tm, D + kvw2), row),
            pl.BlockSpec((tm, D), row),
            pl.BlockSpec((8, D), lambda i: (0, 0)),
            pl.BlockSpec((8, LANES), lambda i: (0, 0)),
        ],
        out_shape=[
            jax.ShapeDtypeStruct((T, D), F32),
            jax.ShapeDtypeStruct((T, D + kvw2), BF16),
            jax.ShapeDtypeStruct((T, D), BF16),
            jax.ShapeDtypeStruct((8, D), F32),
            jax.ShapeDtypeStruct((8, LANES), F32),
        ],
        compiler_params=_params(1),
    )(dq, dkc, dkp, dkp, dvc, dvp, dvp, qkv, dres, x, gain, w_qkv, qg, kg)


def local_step(x, target, gains, w, *, seq, tm=256, tm_ffn=256, tm_conv=512, tk=2048, shards=None, ex=None):
    T, D = x.shape
    n_seq = T // seq
    nm, nf, qgain, kgain, sinks = gains
    H = D // HEAD_DIM
    tk, tk_long = min(tk, T), min(2 * tk, T)
    qg2, kg2 = jnp.tile(qgain, (1, 2)), jnp.tile(kgain, (1, 2))
    bias, sinkcol = _attn_tables(sinks, H)

    dist = shards is not None
    w = dict(w)

    plan = _Gather([shards["w_gu"][0], shards["w_d"][0]]) if dist else None
    (x1, bcx, y_conv, z16), got = conv_fwd(x, nm[0:1], w["w_in"], w["cw"], w["w_out"], seq=seq, tm=tm_conv, plan=plan)
    if dist:
        w["w_gu"], w["w_d"] = [cols_from_shards(got[0]), None], [got[1].reshape(-1, D), None]
    plan = _Gather([shards["w_qkv"], shards["w_o"], shards["w_gu"][1], shards["w_d"][1]]) if dist else None
    (x2, gu0), got = ffn_fwd(x1, nf[0:1], w["w_gu"][0], w["w_d"][0], tm=2 * tm_ffn, plan=plan)
    if dist:
        w["w_qkv"], w["w_o"] = cols_from_shards(got[0]), got[1].reshape(D, D)
        w["w_gu"][1], w["w_d"][1] = cols_from_shards(got[2]), got[3].reshape(-1, D)
    qkv, q16, kd, vd = qkv_proj(x2, nm[1:2], w["w_qkv"], qg2, kg2, tm=tm_conv)
    ao = attn_fwd(q16, kd, vd, bias, sinkcol, seq=seq, n_seq=n_seq)
    (x3, dx4, gu1, sse), _ = ffn_fwd(x2, nf[1:2], w["w_gu"][1], w["w_d"][1], tm=tm_ffn, attn=(ao, w["w_o"]), target=target)

    by_dest = lambda a: a.reshape(N_DEV, -1, a.shape[-1])
    gu_cols = 2 * MXU_TILE

    def send(name, *entries):
        if ex is None:
            return None
        items = [(a, False, key, (N_DEV,) + (() if layers is None else (layers,)) + a.shape[1:], layer)
                 for a, key, layer, layers in entries]
        return ex.start(items, name=name)

    (dx3, a16, dgu, h16, d16, dnf1, dx3_16, dao), _ = ffn_bwd(
        dx4, x3, nf[1:2], gu1, w["w_gu"][1], w["w_d"][1], tm=tm, w_o=w["w_o"])
    g_gu1 = shards_from_cols(wgrad(h16, dgu, name="wgrad_gu1", b_cols=gu_cols, flat=True, tk=tk_long))
    g_d1 = by_dest(wgrad(a16, d16, name="wgrad_d1", a_cols=a16.shape[1] // 2, tk=tk))
    tok = send("exchange_ffn1", (g_gu1, "w_gu", 1, 2), (g_d1, "w_d", 1, 2))
    g_o = by_dest(wgrad(ao, dx3_16, name="wgrad_o", tk=tk, after=tok))
    dq, dkc, dkp, dvc, dvp, dsinks = attn_bwd(q16, kd, vd, dao, bias, sinkcol, seq=seq, n_seq=n_seq)
    dx2, dqkv16, h16, dnm1, dgains = qkv_bwd(dq, dkc, dkp, dvc, dvp, qkv, dx3, x2, nm[1:2], w["w_qkv"], qg2, kg2, seq=seq)
    g_qkv = shards_from_cols(wgrad(h16, dqkv16, name="wgrad_qkv", tk=tk)[0])
    tok = send("exchange_attn", (g_o, "w_o", None, None), (g_qkv, "w_qkv", None, None))
    (dx1, a16, dgu, h16, d16, dnf0), _ = ffn_bwd(dx2, x1, nf[0:1], gu0, w["w_gu"][0], w["w_d"][0], tm=tm, after=tok)
    g_gu0 = shards_from_cols(wgrad(h16, dgu, name="wgrad_gu0", b_cols=gu_cols, flat=True, tk=tk_long))
    tok = send("exchange_gu0", (g_gu0, "w_gu", 0, 2))
    g_d0 = by_dest(wgrad(a16, d16, name="wgrad_d0", a_cols=a16.shape[1] // 2, tk=tk, after=tok))
    tok = send("exchange_d0", (g_d0, "w_d", 0, 2))
    (gx, dbcx, h16, d16, dcw, dnm0), _ = conv_bwd(
        dx1, x, nm[0:1], bcx, y_conv, w["cw"], w["w_in"], w["w_out"], seq=seq, tm=tm_conv, after=tok)
    g_out = by_dest(wgrad(z16, d16, name="wgrad_out", tk=tk))
    g_cw = dcw[0:3].reshape(3, N_DEV, D // N_DEV).transpose(1, 0, 2)
    tok = send("exchange_out", (g_out, "w_out", None, None), (g_cw, "cw", None, None))
    g_in = wgrad(h16, dbcx, name="wgrad_in", b_cols=3 * D // N_DEV, group=2, tk=tk_long, after=tok)
    g = dict(w_in=g_in, cw=g_cw, w_out=g_out, w_o=g_o, w_qkv=g_qkv, w_gu=[g_gu0, g_gu1], w_d=[g_d0, g_d1])
    small = dict(nm0=dnm0, nm1=dnm1, nf0=dnf0, nf1=dnf1, gains=dgains, sinks=dsinks)
    return sse, gx, g, small


def _adamw_math(g, w, m, v):
    m = ADAM_B1 * m + (1.0 - ADAM_B1) * g
    v = ADAM_B2 * v + (1.0 - ADAM_B2) * (g * g)
    m_hat = m / (1.0 - ADAM_B1 ** ADAM_STEP)
    v_hat = v / (1.0 - ADAM_B2 ** ADAM_STEP)
    delta = -ADAM_LR * (m_hat / (jnp.sqrt(v_hat) + ADAM_EPS) + ADAM_WD * w)
    return delta, m, v


def adamw(parts, owns, w, m, v, *, name, after=None):
    n, LR, C = parts.shape
    L = len(owns)
    R = LR // L
    tr = R
    for cand in (256, 128, 88, 64, 32, 16, 8):
        if R > cand and R % cand == 0:
            tr = cand
            break
    per_layer = R // tr
    extra = [] if after is None else [after]

    def body(me_ref, p_ref, *rest):
        own_refs, (w_ref, m_ref, v_ref) = rest[:L], rest[L : L + 3]
        g_ref, d_ref, mo_ref, vo_ref = rest[L + 3 + len(extra) :]
        layer = pl.program_id(0) // per_layer
        mine = own_refs[0][...].astype(F32)
        for j in range(1, L):
            mine = jnp.where(layer == j, own_refs[j][...].astype(F32), mine)
        g = None
        for s in range(n):
            share = jnp.where(me_ref[0] == s, mine, p_ref[s].astype(F32))
            g = share if g is None else g + share
        g_ref[...] = g
        d_ref[...], mo_ref[...], vo_ref[...] = _adamw_math(g, w_ref[...], m_ref[...], v_ref[...])

    blk = pl.BlockSpec((tr, C), lambda i, me: (i, 0))
    own_specs = [pl.BlockSpec((None, tr, C), lambda i, me: (me[0], i % per_layer, 0)) if o.ndim == 3
                 else pl.BlockSpec((tr, C), lambda i, me: (i % per_layer, 0)) for o in owns]
    me = (4 * lax.axis_index("x") + 2 * lax.axis_index("y") + lax.axis_index("c")).astype(jnp.int32).reshape(1)
    return pl.pallas_call(
        body,
        name=name,
        grid_spec=pltpu.PrefetchScalarGridSpec(
            num_scalar_prefetch=1,
            grid=(LR // tr,),
            in_specs=[pl.BlockSpec((n, tr, C), lambda i, me: (0, i, 0))] + own_specs + [blk, blk, blk] + _any_specs(len(extra)),
            out_specs=[blk] * 4,
        ),
        out_shape=[jax.ShapeDtypeStruct((LR, C), F32)] * 4,
        compiler_params=_params(1),
    )(me, parts, *owns, w, m, v, *extra)


def pack_small(small, sse, D):
    W = max(D, 2 * LANES)

    def body(nm0, nm1, nf0, nf1, gains, sinks, sse_ref, o_ref):
        o_ref[...] = jnp.zeros_like(o_ref)
        o_ref[0:1, :D] = nm0[0:1, :]
        o_ref[1:2, :D] = nm1[0:1, :]
        o_ref[2:3, :D] = nf0[0:1, :]
        o_ref[3:4, :D] = nf1[0:1, :]
        gq = gains[0:1, :] + pltpu.roll(gains[0:1, :], HEAD_DIM, 1)
        gk = gains[1:2, :] + pltpu.roll(gains[1:2, :], HEAD_DIM, 1)
        lane = lax.broadcasted_iota(jnp.int32, (1, LANES), 1)
        o_ref[4:5, :LANES] = jnp.where(lane < HEAD_DIM, gq, gk)
        o_ref[4:5, LANES : 2 * LANES] = sinks[0:1, :]
        o_ref[5:6, :LANES] = sse_ref[0:1, :] * (0.5 / D)

    return pl.pallas_call(
        body,
        name="pack_small",
        out_shape=jax.ShapeDtypeStruct((8, W), F32),
    )(small["nm0"], small["nm1"], small["nf0"], small["nf1"], small["gains"], small["sinks"], sse)


def _pack_small_params(nm, nf, qg, kg, sk, D):
    W = max(D, 2 * LANES)
    row4 = jnp.concatenate([qg.reshape(-1), kg.reshape(-1), jnp.zeros((LANES - 2 * HEAD_DIM,), F32), sk.reshape(-1)])
    row4 = jnp.pad(row4, (0, W - row4.shape[0]))
    rows = [jnp.pad(r, (0, W - D)) for r in (nm[0], nm[1], nf[0], nf[1])] + [row4]
    return jnp.concatenate([jnp.stack(rows), jnp.zeros((3, W), F32)], axis=0)


def _unpack_small(a, D, H):
    nm = a[0:2, :D]
    nf = a[2:4, :D]
    qg = a[4:5, 0:HEAD_DIM]
    kg = a[4:5, HEAD_DIM : 2 * HEAD_DIM]
    sk = a[4:5, LANES : LANES + H]
    return qg, kg, sk, nm, nf


def kernel(x, conv_w_in, conv_w, conv_w_out, attn_w_qkv, attn_q_gain, attn_k_gain, attn_sinks, attn_w_o, norm_mixer, norm_ffn, ffn_w_gate_up, ffn_w_down, loss_target, m_conv_w_in, m_conv_w, m_conv_w_out, m_attn_w_qkv, m_attn_q_gain, m_attn_k_gain, m_attn_sinks, m_attn_w_o, m_norm_mixer, m_norm_ffn, m_ffn_w_gate_up, m_ffn_w_down, v_conv_w_in, v_conv_w, v_conv_w_out, v_attn_w_qkv, v_attn_q_gain, v_attn_k_gain, v_attn_sinks, v_attn_w_o, v_norm_mixer, v_norm_ffn, v_ffn_w_gate_up, v_ffn_w_down):
    n_seq, seq, D = x.shape
    T = n_seq * seq
    H = D // HEAD_DIM
    L = ffn_w_gate_up.shape[0]

    full = run_plan(_Gather([conv_w_in[0].astype(BF16), conv_w[0], conv_w_out[0].astype(BF16)]), name="gather_conv_weights")
    w = dict(w_in=cols_from_shards(full[0]), cw=full[1].transpose(1, 0, 2).reshape(3, D),
             w_out=full[2].reshape(D, D))
    shards = dict(w_gu=[ffn_w_gate_up[l].astype(BF16) for l in range(L)], w_d=[ffn_w_down[l].astype(BF16) for l in range(L)],
                  w_qkv=attn_w_qkv[0].astype(BF16), w_o=attn_w_o[0].astype(BF16))
    gains = (norm_mixer, norm_ffn, attn_q_gain, attn_k_gain, attn_sinks)
    ex = Exchange()
    sse, gx, g, small = local_step(x.reshape(T, D), loss_target.reshape(T, D), gains, w, seq=seq, shards=shards, ex=ex)
    zones, own = ex.wait([g["w_in"]], name="exchange_wait")

    packed = pack_small(small, sse, D)
    token = ex.start([(g["w_in"], False, "w_in", g["w_in"].shape, None),
                      (packed, True, "small", (N_DEV,) + packed.shape, None)], name="exchange_last")

    def flat(a):
        return a.reshape(-1, a.shape[-1])

    big = [conv_w_in, conv_w, conv_w_out, attn_w_qkv, attn_w_o, ffn_w_gate_up, ffn_w_down]
    big_m = [m_conv_w_in, m_conv_w, m_conv_w_out, m_attn_w_qkv, m_attn_w_o, m_ffn_w_gate_up, m_ffn_w_down]
    big_v = [v_conv_w_in, v_conv_w, v_conv_w_out, v_attn_w_qkv, v_attn_w_o, v_ffn_w_gate_up, v_ffn_w_down]
    keys = ["w_in", "cw", "w_out", "w_qkv", "w_o", "w_gu", "w_d"]

    def update(b, zones, own, after=None):
        zone = zones[keys[b]]
        parts = zone.reshape(N_DEV, -1, zone.shape[-1])
        layers = [None] if zone.ndim == 3 else range(zone.shape[1])
        outs = adamw(parts, [own[(keys[b], l)] for l in layers], flat(big[b]), flat(big_m[b]), flat(big_v[b]),
                     name="adamw_" + keys[b], after=after)
        return [o.reshape(big[b].shape) for o in outs]

    res = [None] + [update(b, zones, own, after=token) for b in range(1, 7)]
    zones, own = ex.wait([r[0] for r in res[1:]], name="exchange_last_wait")
    res[0] = update(0, zones, own)
    sw = _pack_small_params(norm_mixer, norm_ffn, attn_q_gain, attn_k_gain, attn_sinks, D)
    sm = _pack_small_params(m_norm_mixer, m_norm_ffn, m_attn_q_gain, m_attn_k_gain, m_attn_sinks, D)
    sv = _pack_small_params(v_norm_mixer, v_norm_ffn, v_attn_q_gain, v_attn_k_gain, v_attn_sinks, D)
    souts = adamw(zones["small"], [own[("small", None)]], sw, sm, sv, name="adamw_small")
    sres = [_unpack_small(o, D, H) for o in souts]
    loss = souts[0][5, 0]

    def ordered(i):
        r, s = [r[i] for r in res], sres[i]
        return [r[0], r[1], r[2], r[3], s[0], s[1], s[2], r[4], s[3], s[4], r[5], r[6]]

    return (loss, gx.reshape(n_seq, seq, D), *ordered(0), *ordered(1), *ordered(2), *ordered(3))
```

```python
import functools
import math

import jax
import jax.numpy as jnp
from jax import lax
from jax.experimental import pallas as pl
from jax.experimental.pallas import tpu as pltpu

F32 = jnp.float32
BF16 = jnp.bfloat16

EPS = 1e-6
HEAD_DIM = 64
N_KV_HEADS = 4
BLOCK = 128
LANES = 128
N_DEV = 8
NEG = -1e30
SCALE = 1.0 / math.sqrt(HEAD_DIM)

ADAM_LR = 0.001
ADAM_B1 = 0.9
ADAM_B2 = 0.999
ADAM_EPS = 1e-08
ADAM_WD = 0.01
ADAM_STEP = 10

V7X_VMEM_BYTES = 64 * 1024 * 1024
VMEM_LIMIT = V7X_VMEM_BYTES - 2 * 1024 * 1024
MESH = pl.DeviceIdType.MESH

_NT = (((1,), (1,)), ((), ()))
_TN = (((0,), (0,)), ((), ()))


def _params(n_grid):
    return pltpu.CompilerParams(dimension_semantics=("arbitrary",) * n_grid, vmem_limit_bytes=VMEM_LIMIT)


def _resident(shape):
    nd = len(shape)
    return pl.BlockSpec(shape, lambda *_: (0,) * nd, pipeline_mode=pl.Buffered(1))


def _rms(x):
    return lax.rsqrt(jnp.mean(x * x, axis=-1, keepdims=True) + EPS)


def _rms_bwd(x, r, gain, dh):
    xn = x * r
    dxn = dh * gain
    dx = r * (dxn - xn * jnp.mean(dxn * xn, axis=-1, keepdims=True))
    return dx, jnp.sum(dh * xn, axis=0, keepdims=True)


def _dot(a, b):
    return jnp.dot(a, b, preferred_element_type=F32)


def _dot_nt(a, b):
    return lax.dot_general(a, b, _NT, preferred_element_type=F32)


def _dot_tn(a, b):
    return lax.dot_general(a, b, _TN, preferred_element_type=F32)


def _place():
    return lax.axis_index("x"), lax.axis_index("y"), lax.axis_index("c")


def _flip(v, bit):
    return 1 - v if bit else v


def _slot(px, py, pc):
    return 4 * px + 2 * py + pc


class _Gather:
    def __init__(self, shards):
        nt = len(shards)
        self.nt = nt
        self.inputs = list(shards)
        self.out_shapes = [jax.ShapeDtypeStruct((N_DEV,) + s.shape, s.dtype) for s in shards]
        self.scratch = [pltpu.SemaphoreType.DMA((nt, 7)), pltpu.SemaphoreType.DMA((nt, 7)), pltpu.SemaphoreType.DMA((nt,))]
        self.aliases = {}

    def phases(self, total):
        assert total >= 3
        return [(0, self.start), (total - 2, self.forward), (total - 1, self.finish)]

    def _copies(self, ins, outs, sems):
        send_sems, recv_sems, loc_sems = sems
        x, y, c = _place()
        me = _slot(x, y, c)
        sib = (x, y, 1 - c)
        chips = [(_flip(x, k >> 1), _flip(y, k & 1)) for k in (1, 2, 3)]

        def copy(t, k, src, dst_slot, to):
            return pltpu.make_async_remote_copy(
                src_ref=src, dst_ref=outs[t].at[dst_slot], send_sem=send_sems.at[t, k], recv_sem=recv_sems.at[t, k],
                device_id=to, device_id_type=MESH)

        local = [pltpu.make_async_copy(ins[t], outs[t].at[me], loc_sems.at[t]) for t in range(self.nt)]
        first, passed, arrive_ici, arrive_sib = [], [], [], []
        for t in range(self.nt):
            first.append(copy(t, 0, ins[t], me, sib))
            s = _slot(x, y, 1 - c)
            arrive_sib.append(copy(t, 0, outs[t].at[s], s, sib))
            for j, (px, py) in enumerate(chips):
                first.append(copy(t, 1 + j, ins[t], me, (px, py, c)))
                s = _slot(px, py, c)
                arrive_ici.append(copy(t, 1 + j, outs[t].at[s], s, sib))
                passed.append(copy(t, 4 + j, outs[t].at[s], s, sib))
                s = _slot(px, py, 1 - c)
                arrive_sib.append(copy(t, 4 + j, outs[t].at[s], s, sib))
        return local, first, passed, arrive_ici, arrive_sib

    def start(self, ins, outs, sems):
        local, first, _, _, _ = self._copies(ins, outs, sems)
        for cp in local + first:
            cp.start()

    def forward(self, ins, outs, sems):
        _, _, passed, arrive_ici, _ = self._copies(ins, outs, sems)
        for arrival, fwd in zip(arrive_ici, passed):
            arrival.wait_recv()
            fwd.start()

    def finish(self, ins, outs, sems):
        local, first, passed, _, arrive_sib = self._copies(ins, outs, sems)
        for cp in arrive_sib:
            cp.wait_recv()
        for cp in first + passed:
            cp.wait_send()
        for cp in local:
            cp.wait()


def _any_specs(n):
    return [pl.BlockSpec(memory_space=pl.ANY)] * n


def run_plan(plan, *, name):
    def body(*refs):
        n_in, n_out = len(plan.inputs), len(plan.out_shapes)
        ins, outs, sems = refs[:n_in], refs[n_in : n_in + n_out], refs[n_in + n_out :]
        for _, phase in plan.phases(3):
            phase(ins, outs, sems)

    return pl.pallas_call(
        body,
        name=name,
        in_specs=_any_specs(len(plan.inputs)),
        out_specs=_any_specs(len(plan.out_shapes)),
        out_shape=plan.out_shapes,
        scratch_shapes=plan.scratch,
        input_output_aliases=plan.aliases,
    )(*plan.inputs)


_HBM = pl.BlockSpec(memory_space=pltpu.HBM)
_SEM = pl.BlockSpec(memory_space=pltpu.SEMAPHORE)
_DATAFLOW = pltpu.SideEffectType.DATAFLOW_SIDE_EFFECTING


class Exchange:
    def __init__(self):
        self.zones = {}
        self.pending = []
        self.sources = []

    def start(self, items, *, name):
        nt = len(items)
        keys = list(dict.fromkeys(it[2] for it in items))
        for a, _, key, shape, _ in items:
            if key not in self.zones:
                self.zones[key] = lax.empty(shape, a.dtype)
        nz = len(keys)

        def body(*refs):
            ins, zones, sems, token = refs[:nt], refs[nt : nt + nz], refs[nt + nz : nt + nz + 2 * nt], refs[-1]
            x, y, c = _place()
            me = _slot(x, y, c)
            for k in range(1, N_DEV):
                px, py, pc = _flip(x, (k >> 2) & 1), _flip(y, (k >> 1) & 1), _flip(c, k & 1)
                for t, (_, whole, key, _, layer) in enumerate(items):
                    zone = zones[keys.index(key)]
                    pltpu.make_async_remote_copy(
                        src_ref=ins[t] if whole else ins[t].at[_slot(px, py, pc)],
                        dst_ref=zone.at[me] if layer is None else zone.at[me, layer],
                        send_sem=sems[2 * t], recv_sem=sems[2 * t + 1], device_id=(px, py, pc), device_id_type=MESH).start()
            token[...] = jnp.zeros_like(token)

        bufs = [pltpu.with_memory_space_constraint(b, pltpu.HBM) for b in [it[0] for it in items] + [self.zones[k] for k in keys]]
        outs = pl.pallas_call(
            body,
            name=name,
            in_specs=[_HBM] * (nt + nz),
            out_specs=[_SEM] * (2 * nt) + [_HBM] * (nt + nz) + [pl.BlockSpec(memory_space=pltpu.VMEM)],
            out_shape=[pltpu.SemaphoreType.DMA(())] * (2 * nt) + [pltpu.HBM(b.shape, b.dtype) for b in bufs]
            + [jax.ShapeDtypeStruct((8, LANES), F32)],
            input_output_aliases={i: 2 * nt + i for i in range(nt + nz)},
            compiler_params=pltpu.CompilerParams(has_side_effects=_DATAFLOW),
        )(*bufs)
        for t, (_, _, key, _, layer) in enumerate(items):
            self.pending.append((outs[2 * t], outs[2 * t + 1], key, layer))
        self.sources += [((it[2], it[4]), a) for it, a in zip(items, outs[2 * nt : 3 * nt])]
        for i, key in enumerate(keys):
            self.zones[key] = outs[3 * nt + i]
        return outs[-1]

    def wait(self, after, *, name):
        pending, keys = self.pending, list(self.zones)
        names, sources = [n for n, _ in self.sources], [a for _, a in self.sources]
        ns, nz, npend = len(sources), len(keys), len(pending)
        self.pending, self.sources = [], []

        def body(*refs):
            zones, sems = refs[ns : ns + nz], refs[ns + nz : ns + nz + 2 * npend]
            x, y, c = _place()
            for i, (_, _, key, layer) in enumerate(pending):
                zone = zones[keys.index(key)]
                rows = pl.ds(0, N_DEV - 1)
                seven = zone.at[rows] if layer is None else zone.at[rows, layer]
                pltpu.make_async_remote_copy(
                    src_ref=seven, dst_ref=seven, send_sem=sems[2 * i], recv_sem=sems[2 * i + 1],
                    device_id=(x, y, c), device_id_type=MESH).wait()

        bufs = list(sources) + [self.zones[k] for k in keys]
        flat_sems = [s for p in pending for s in p[:2]]
        outs = pl.pallas_call(
            body,
            name=name,
            in_specs=[_HBM] * (ns + nz) + [_SEM] * (2 * npend) + _any_specs(len(after)),
            out_specs=[_HBM] * (ns + nz),
            out_shape=[pltpu.HBM(b.shape, b.dtype) for b in bufs],
            input_output_aliases={i: i for i in range(ns + nz)},
            compiler_params=pltpu.CompilerParams(has_side_effects=_DATAFLOW),
        )(*bufs, *flat_sems, *after)
        self.zones = {}
        return dict(zip(keys, outs[ns:])), dict(zip(names, outs[:ns]))


def _call(body, *, name, grid, in_specs, out_specs, out_shape, args, scratch=(), plan=None, after=None):
    if after is not None:
        inner, n_real = body, len(in_specs)
        body = lambda *refs: inner(*refs[:n_real], *refs[n_real + 1 :])
        in_specs, args = list(in_specs) + _any_specs(1), list(args) + [after]
    n_in, n_out, n_scr = len(in_specs), len(out_specs), len(scratch)
    if plan is None:
        outs = pl.pallas_call(
            body, name=name, grid=grid, in_specs=in_specs, out_specs=out_specs, out_shape=out_shape,
            scratch_shapes=list(scratch), compiler_params=_params(len(grid)))(*args)
        return outs, None
    c_in, c_out = len(plan.inputs), len(plan.out_shapes)
    phases = plan.phases(math.prod(grid))

    def full(*refs):
        a, refs = refs[:n_in], refs[n_in:]
        ci, refs = refs[:c_in], refs[c_in:]
        o, refs = refs[:n_out], refs[n_out:]
        co, refs = refs[:c_out], refs[c_out:]
        s, cs = refs[:n_scr], refs[n_scr:]
        step = pl.program_id(0)
        for d in range(1, len(grid)):
            step = step * grid[d] + pl.program_id(d)
        for at, phase in phases:
            if at == 0:
                pl.when(step == 0)(functools.partial(phase, ci, co, cs))
        body(*a, *o, *s)
        for at, phase in phases:
            if at > 0:
                pl.when(step == at)(functools.partial(phase, ci, co, cs))

    outs = pl.pallas_call(
        full,
        name=name,
        grid=grid,
        in_specs=list(in_specs) + _any_specs(c_in),
        out_specs=list(out_specs) + _any_specs(c_out),
        out_shape=list(out_shape) + plan.out_shapes,
        scratch_shapes=list(scratch) + plan.scratch,
        input_output_aliases={n_in + i: n_out + t for i, t in plan.aliases.items()},
        compiler_params=_params(len(grid)),
    )(*args, *plan.inputs)
    return outs[:n_out], outs[n_out:]


def _row_tile(R):
    return 256 if R % 256 == 0 else R


def cols_from_shards(a):
    n, R, C = a.shape
    tr = _row_tile(R)

    def body(i_ref, o_ref):
        for s in range(n):
            o_ref[:, s * C : (s + 1) * C] = i_ref[s]

    return pl.pallas_call(
        body,
        name="cols_from_shards",
        grid=(R // tr,),
        in_specs=[pl.BlockSpec((n, tr, C), lambda i: (0, i, 0))],
        out_specs=pl.BlockSpec((tr, n * C), lambda i: (i, 0)),
        out_shape=jax.ShapeDtypeStruct((R, n * C), a.dtype),
        compiler_params=_params(1),
    )(a)


def shards_from_cols(a):
    R, W = a.shape
    C = W // N_DEV
    tr = _row_tile(R)

    def body(i_ref, o_ref):
        for s in range(N_DEV):
            o_ref[s] = i_ref[:, s * C : (s + 1) * C]

    return pl.pallas_call(
        body,
        name="shards_from_cols",
        grid=(R // tr,),
        in_specs=[pl.BlockSpec((tr, W), lambda i: (i, 0))],
        out_specs=pl.BlockSpec((N_DEV, tr, C), lambda i: (0, i, 0)),
        out_shape=jax.ShapeDtypeStruct((N_DEV, R, C), a.dtype),
        compiler_params=_params(1),
    )(a)


def _shift_down(u, prev8, row, n):
    out = pltpu.roll(u, n, 0)
    for k in range(n):
        out = jnp.where(row == k, prev8[8 - n + k : 8 - n + k + 1, :], out)
    return out


def _shift_up(u, next8, row, n, tm):
    out = pltpu.roll(u, tm - n, 0)
    for k in range(n):
        out = jnp.where(row == tm - n + k, next8[k : k + 1, :], out)
    return out


def conv_fwd(x, gain, w_in, cw, w_out, *, seq, tm, plan=None):
    T, D = x.shape
    tps = seq // tm

    def body(x_ref, g_ref, win_ref, cw_ref, wout_ref, x1_ref, bcx_ref, y_ref, z_ref, carry_ref):
        i = pl.program_id(0)

        @pl.when(i % tps == 0)
        def _():
            carry_ref[...] = jnp.zeros_like(carry_ref)

        xt = x_ref[...]
        h = ((xt * _rms(xt)) * g_ref[...]).astype(BF16)
        bcx = _dot(h, win_ref[...])
        bcx_ref[...] = bcx
        b, c, xv = bcx[:, :D], bcx[:, D : 2 * D], bcx[:, 2 * D :]
        u = b * xv
        row = lax.broadcasted_iota(jnp.int32, u.shape, 0)
        prev = carry_ref[...]
        u1 = _shift_down(u, prev, row, 1)
        u2 = _shift_down(u, prev, row, 2)
        carry_ref[...] = u[tm - 8 :, :]
        cwv = cw_ref[...]
        y = cwv[0:1, :] * u2 + cwv[1:2, :] * u1 + cwv[2:3, :] * u
        y_ref[...] = y
        z = (c * y).astype(BF16)
        z_ref[...] = z
        x1_ref[...] = xt + _dot(z, wout_ref[...])

    tile = pl.BlockSpec((tm, D), lambda i: (i, 0))
    return _call(
        body,
        plan=plan,
        args=(x, gain, w_in, cw, w_out),
        name="conv_fwd",
        grid=(T // tm,),
        in_specs=[
            pl.BlockSpec((tm, D), lambda i: (i, 0)),
            _resident((1, D)),
            _resident((D, 3 * D)),
            _resident((3, D)),
            _resident((D, D)),
        ],
        out_specs=[tile, pl.BlockSpec((tm, 3 * D), lambda i: (i, 0)), tile, tile],
        out_shape=[jax.ShapeDtypeStruct((T, D), F32), jax.ShapeDtypeStruct((T, 3 * D), F32),
                   jax.ShapeDtypeStruct((T, D), F32), jax.ShapeDtypeStruct((T, D), BF16)],
        scratch=[pltpu.VMEM((8, D), F32)],
    )


def conv_bwd(dx1, x, gain, bcx, y, cw, w_in, w_out, *, seq, tm, after=None):
    T, D = x.shape
    n = T // tm
    tps = seq // tm

    def body(d_ref, x_ref, g_ref, bcx_ref, y_ref, cw_ref, win_ref, wout_ref,
             gx_ref, dbcx_ref, h_ref, d16_ref, dcw_ref, dg_ref, carry_ref):
        i = pl.program_id(0)
        t = n - 1 - i

        @pl.when(i == 0)
        def _():
            dcw_ref[...] = jnp.zeros_like(dcw_ref)
            dg_ref[...] = jnp.zeros_like(dg_ref)

        @pl.when(t % tps == tps - 1)
        def _():
            carry_ref[...] = jnp.zeros_like(carry_ref)

        d = d_ref[...]
        d16 = d.astype(BF16)
        d16_ref[...] = d16
        dz = _dot_nt(d16, wout_ref[...])
        bcx = bcx_ref[...]
        b, c, xv = bcx[:, :D], bcx[:, D : 2 * D], bcx[:, 2 * D :]
        u = b * xv
        row = lax.broadcasted_iota(jnp.int32, u.shape, 0)
        cwv = cw_ref[...]
        dc = dz * y_ref[...]
        dy = dz * c
        nxt = carry_ref[...]
        dy1 = _shift_up(dy, nxt, row, 1, tm)
        dy2 = _shift_up(dy, nxt, row, 2, tm)
        carry_ref[...] = dy[0:8, :]
        dcw_ref[0:1, :] += jnp.sum(dy2 * u, axis=0, keepdims=True)
        dcw_ref[1:2, :] += jnp.sum(dy1 * u, axis=0, keepdims=True)
        dcw_ref[2:3, :] += jnp.sum(dy * u, axis=0, keepdims=True)
        du = cwv[2:3, :] * dy + cwv[1:2, :] * dy1 + cwv[0:1, :] * dy2
        dbcx_ref[:, :D] = (du * xv).astype(BF16)
        dbcx_ref[:, D : 2 * D] = dc.astype(BF16)
        dbcx_ref[:, 2 * D :] = (du * b).astype(BF16)
        dh = _dot_nt(dbcx_ref[...], win_ref[...])
        xt = x_ref[...]
        r = _rms(xt)
        gn = g_ref[...]
        h_ref[...] = ((xt * r) * gn).astype(BF16)
        dx, dgn = _rms_bwd(xt, r, gn, dh)
        dg_ref[0:1, :] += dgn
        gx_ref[...] = d + dx

    rev = lambda i: (n - 1 - i, 0)
    return _call(
        body,
        after=after,
        args=(dx1, x, gain, bcx, y, cw, w_in, w_out),
        name="conv_bwd",
        grid=(n,),
        in_specs=[
            pl.BlockSpec((tm, D), rev),
            pl.BlockSpec((tm, D), rev),
            _resident((1, D)),
            pl.BlockSpec((tm, 3 * D), rev),
            pl.BlockSpec((tm, D), rev),
            _resident((3, D)),
            _resident((D, 3 * D)),
            _resident((D, D)),
        ],
        out_specs=[
            pl.BlockSpec((tm, D), rev),
            pl.BlockSpec((tm, 3 * D), rev),
            pl.BlockSpec((tm, D), rev),
            pl.BlockSpec((tm, D), rev),
            pl.BlockSpec((8, D), lambda i: (0, 0)),
            pl.BlockSpec((8, D), lambda i: (0, 0)),
        ],
        out_shape=[
            jax.ShapeDtypeStruct((T, D), F32),
            jax.ShapeDtypeStruct((T, 3 * D), BF16),
            jax.ShapeDtypeStruct((T, D), BF16),
            jax.ShapeDtypeStruct((T, D), BF16),
            jax.ShapeDtypeStruct((8, D), F32),
            jax.ShapeDtypeStruct((8, D), F32),
        ],
        scratch=[pltpu.VMEM((8, D), F32)],
    )


MXU_TILE = 256
FFN_CHUNK = 4 * MXU_TILE


def _sigmoid(g):
    return 1.0 / (1.0 + jnp.exp(-g))


def _ffn_chunks(F):
    assert F % MXU_TILE == 0
    return [(s, min(FFN_CHUNK, F - s)) for s in range(0, F, FFN_CHUNK)]


def ffn_fwd(x, gain, w_gu, w_d, *, tm, plan=None, attn=None, target=None):
    T, D = x.shape
    F = w_d.shape[0]
    row = lambda i: (i, 0)
    tile = pl.BlockSpec((tm, D), row)

    def body(*refs):
        refs = list(refs)
        x_ref, g_ref, wgu_ref, wd_ref = refs[:4]
        del refs[:4]
        if attn is not None:
            ao_ref, wo_ref = refs[:2]
            del refs[:2]
        if target is not None:
            t_ref = refs.pop(0)
        if attn is not None:
            xin_ref = refs.pop(0)
        xo_ref, gu_ref = refs[:2]
        xt = x_ref[...]
        if attn is not None:
            xt = xt + _dot(ao_ref[...], wo_ref[...])
            xin_ref[...] = xt
        h = ((xt * _rms(xt)) * g_ref[...]).astype(BF16)
        acc = xt
        for s, n in _ffn_chunks(F):
            g = _dot(h, wgu_ref[:, s : s + n])
            u = _dot(h, wgu_ref[:, F + s : F + s + n])
            gu_ref[:, s : s + n] = g
            gu_ref[:, F + s : F + s + n] = u
            a = ((g * _sigmoid(g)) * u).astype(BF16)
            acc = acc + _dot(a, wd_ref[s : s + n, :])
        if target is None:
            xo_ref[...] = acc
        else:
            s_ref = refs[2]

            @pl.when(pl.program_id(0) == 0)
            def _():
                s_ref[...] = jnp.zeros_like(s_ref)

            e = acc - t_ref[...]
            xo_ref[...] = e * (1.0 / D)
            s_ref[...] += jnp.sum(jnp.sum(e * e, axis=-1, keepdims=True), axis=0, keepdims=True)

    args = [x, gain, w_gu, w_d]
    in_specs = [tile, _resident((1, D)), _resident((D, 2 * F)), _resident((F, D))]
    out_specs = [tile, pl.BlockSpec((tm, 2 * F), row)]
    out_shape = [jax.ShapeDtypeStruct((T, D), F32), jax.ShapeDtypeStruct((T, 2 * F), F32)]
    if attn is not None:
        args += list(attn)
        in_specs += [pl.BlockSpec((tm, attn[0].shape[1]), row), _resident(attn[1].shape)]
        out_specs.insert(0, tile)
        out_shape.insert(0, jax.ShapeDtypeStruct((T, D), F32))
    if target is not None:
        args.append(target)
        in_specs.append(tile)
        out_specs.append(pl.BlockSpec((8, LANES), lambda i: (0, 0)))
        out_shape.append(jax.ShapeDtypeStruct((8, LANES), F32))
    return _call(body, plan=plan, args=args, name="ffn_fwd", grid=(T // tm,), in_specs=in_specs, out_specs=out_specs,
                 out_shape=out_shape)


def ffn_bwd(dxo, x, gain, gu, w_gu, w_d, *, tm, after=None, w_o=None):
    T, D = x.shape
    F = w_d.shape[0]

    def body(d_ref, x_ref, g_ref, gu_ref, wgu_ref, wd_ref, *rest):
        if w_o is not None:
            wo_ref, rest = rest[0], rest[1:]
        dx_ref, a_ref, dgu_ref, h_ref, d16_ref, dg_ref = rest[:6]

        @pl.when(pl.program_id(0) == 0)
        def _():
            dg_ref[...] = jnp.zeros_like(dg_ref)

        d = d_ref[...]
        d16 = d.astype(BF16)
        d16_ref[...] = d16
        dh = jnp.zeros((tm, D), F32)
        for c0, n in _ffn_chunks(F):
            g = gu_ref[:, c0 : c0 + n]
            u = gu_ref[:, F + c0 : F + c0 + n]
            da = _dot_nt(d16, wd_ref[c0 : c0 + n, :])
            s = _sigmoid(g)
            sg = g * s
            a_ref[:, c0 : c0 + n] = (sg * u).astype(BF16)
            dg16 = (da * u * (s + sg * (1.0 - s))).astype(BF16)
            du16 = (da * sg).astype(BF16)
            dgu_ref[:, c0 : c0 + n] = dg16
            dgu_ref[:, F + c0 : F + c0 + n] = du16
            dh = dh + _dot_nt(dg16, wgu_ref[:, c0 : c0 + n]) + _dot_nt(du16, wgu_ref[:, F + c0 : F + c0 + n])
        xt = x_ref[...]
        r = _rms(xt)
        gn = g_ref[...]
        h_ref[...] = ((xt * r) * gn).astype(BF16)
        dx, dgn = _rms_bwd(xt, r, gn, dh)
        dg_ref[0:1, :] += dgn
        dxi = d + dx
        dx_ref[...] = dxi
        if w_o is not None:
            dxi16_ref, dao_ref = rest[6:8]
            dxi16 = dxi.astype(BF16)
            dxi16_ref[...] = dxi16
            dao_ref[...] = _dot_nt(dxi16, wo_ref[...]).astype(BF16)

    tile = pl.BlockSpec((tm, D), lambda i: (i, 0))
    args = [dxo, x, gain, gu, w_gu, w_d]
    wide = lambda n: pl.BlockSpec((tm, n), lambda i: (i, 0))
    in_specs = [tile, tile, _resident((1, D)), wide(2 * F), _resident((D, 2 * F)), _resident((F, D))]
    out_specs = [tile, wide(F), wide(2 * F), tile, tile, pl.BlockSpec((8, D), lambda i: (0, 0))]
    out_shape = [
        jax.ShapeDtypeStruct((T, D), F32),
        jax.ShapeDtypeStruct((T, F), BF16),
        jax.ShapeDtypeStruct((T, 2 * F), BF16),
        jax.ShapeDtypeStruct((T, D), BF16),
        jax.ShapeDtypeStruct((T, D), BF16),
        jax.ShapeDtypeStruct((8, D), F32),
    ]
    if w_o is not None:
        args.append(w_o)
        in_specs.append(_resident(w_o.shape))
        out_specs += [tile, pl.BlockSpec((tm, w_o.shape[0]), lambda i: (i, 0))]
        out_shape += [jax.ShapeDtypeStruct((T, D), BF16), jax.ShapeDtypeStruct((T, w_o.shape[0]), BF16)]
    return _call(body, after=after, args=args, name="ffn_bwd", grid=(T // tm,), in_specs=in_specs, out_specs=out_specs,
                 out_shape=out_shape)


def wgrad(a, b, *, name, a_cols=0, b_cols=0, group=1, flat=False, tk, out_dtype=BF16, after=None):
    T, K = a.shape
    J = 1
    if a_cols:
        K = a_cols
        J = a.shape[1] // K
        a_spec = pl.BlockSpec((tk, K), lambda j, k: (k, j))
    else:
        a_spec = pl.BlockSpec((tk, K), lambda j, k: (k, 0))
    if b_cols:
        N = b_cols * group
        J = b.shape[1] // N
        b_spec = pl.BlockSpec((tk, N), lambda j, k: (k, j))
    else:
        N = b.shape[1]
        b_spec = pl.BlockSpec((tk, N), lambda j, k: (k, 0))
    nk = T // tk
    if flat:
        o_spec, o_shape = pl.BlockSpec((K, N), lambda j, k: (0, j)), (K, J * N)
    elif group > 1:
        o_spec, o_shape = pl.BlockSpec((group, K, b_cols), lambda j, k: (j, 0, 0)), (J * group, K, b_cols)
    else:
        o_spec, o_shape = pl.BlockSpec((None, K, N), lambda j, k: (j, 0, 0)), (J, K, N)

    def body(a_ref, b_ref, o_ref, acc_ref):
        k = pl.program_id(1)

        @pl.when(k == 0)
        def _():
            acc_ref[...] = jnp.zeros_like(acc_ref)

        acc_ref[...] += _dot_tn(a_ref[...], b_ref[...])

        @pl.when(k == nk - 1)
        def _():
            if group > 1 and not flat:
                for i in range(group):
                    o_ref[i] = acc_ref[:, i * b_cols : (i + 1) * b_cols].astype(out_dtype)
            else:
                o_ref[...] = acc_ref[...].astype(out_dtype)

    outs, _ = _call(
        body,
        after=after,
        name=name,
        grid=(J, nk),
        in_specs=[a_spec, b_spec],
        out_specs=[o_spec],
        out_shape=[jax.ShapeDtypeStruct(o_shape, out_dtype)],
        args=(a, b),
        scratch=[pltpu.VMEM((K, N), F32)],
    )
    return outs[0]


def _seg(xs, lo):
    s_lo = [jnp.sum(jnp.where(lo, x, 0.0), axis=-1, keepdims=True) for x in xs]
    s_hi = [jnp.sum(jnp.where(lo, 0.0, x), axis=-1, keepdims=True) for x in xs]
    return [jnp.where(lo, a, b) for a, b in zip(s_lo, s_hi)]


def _head_norm(xs, gains, lo):
    rs = [lax.rsqrt(s * (1.0 / HEAD_DIM) + EPS) for s in _seg([x * x for x in xs], lo)]
    return [(x * r) * g for x, r, g in zip(xs, rs, gains)], rs


def _head_norm_bwd(xs, rs, gains, dys, lo):
    xns = [x * r for x, r in zip(xs, rs)]
    dxns = [dy * g for dy, g in zip(dys, gains)]
    means = [s * (1.0 / HEAD_DIM) for s in _seg([a * b for a, b in zip(dxns, xns)], lo)]
    dxs = [r * (dxn - xn * m) for r, dxn, xn, m in zip(rs, dxns, xns, means)]
    return dxs, [jnp.sum(dy * xn, axis=0, keepdims=True) for dy, xn in zip(dys, xns)]


def _swap_halves(x):
    return pltpu.roll(x, HEAD_DIM, 1)


def qkv_proj(x, gain, w, qg, kg, *, tm):
    T, D = x.shape
    N = w.shape[1]
    kvw = N_KV_HEADS * HEAD_DIM
    nqt, nkt = D // LANES, kvw // LANES

    def body(x_ref, g_ref, w_ref, qg_ref, kg_ref, qkv_ref, q_ref, kd_ref, vd_ref):
        xt = x_ref[...]
        h = ((xt * _rms(xt)) * g_ref[...]).astype(BF16)
        qkv = _dot(h, w_ref[...])
        qkv_ref[...] = qkv
        lo = lax.broadcasted_iota(jnp.int32, (1, LANES), 1) < HEAD_DIM
        tiles = [qkv[:, t * LANES : (t + 1) * LANES] for t in range(nqt + nkt)]
        normed, _ = _head_norm(tiles, [qg_ref[...]] * nqt + [kg_ref[...]] * nkt, lo)
        for t in range(nqt):
            q_ref[:, t * LANES : (t + 1) * LANES] = (normed[t] * SCALE).astype(BF16)
        for t in range(nkt):
            kn = normed[nqt + t]
            v = qkv[:, D + kvw + t * LANES : D + kvw + (t + 1) * LANES]
            for src, dst in ((kn, kd_ref), (v, vd_ref)):
                sw = _swap_halves(src)
                dst[:, 2 * t * LANES : (2 * t + 1) * LANES] = jnp.where(lo, src, sw).astype(BF16)
                dst[:, (2 * t + 1) * LANES : (2 * t + 2) * LANES] = jnp.where(lo, sw, src).astype(BF16)

    row = lambda i: (i, 0)
    return pl.pallas_call(
        body,
        name="qkv_proj",
        grid=(T // tm,),
        in_specs=[pl.BlockSpec((tm, D), row), _resident((1, D)), _resident((D, N)), _resident((1, LANES)), _resident((1, LANES))],
        out_specs=[pl.BlockSpec((tm, N), row), pl.BlockSpec((tm, D), row), pl.BlockSpec((tm, 2 * kvw), row), pl.BlockSpec((tm, 2 * kvw), row)],
        out_shape=[
            jax.ShapeDtypeStruct((T, N), F32),
            jax.ShapeDtypeStruct((T, D), BF16),
            jax.ShapeDtypeStruct((T, 2 * kvw), BF16),
            jax.ShapeDtypeStruct((T, 2 * kvw), BF16),
        ],
        compiler_params=_params(1),
    )(x, gain, w, qg, kg)


def _attn_tables(sinks, n_q_heads):
    P = n_q_heads // N_KV_HEADS // 2
    h = jnp.arange(1, n_q_heads + 1, dtype=F32)
    slopes = jnp.exp2(-8.0 * h / n_q_heads).reshape(N_KV_HEADS, P, 1, 2, 1)
    qi = jnp.arange(BLOCK)[:, None]
    kj = jnp.arange(BLOCK)[None, :]
    dist = jnp.where(kj <= qi, qi - kj, qi + BLOCK - kj).astype(F32)
    shape = (N_KV_HEADS, P, BLOCK, 2, BLOCK)
    bias = jnp.broadcast_to(-slopes * dist[None, None, :, None, :], shape)
    sink = jnp.broadcast_to(sinks.astype(F32).reshape(N_KV_HEADS, P, 1, 2, 1), shape)
    return bias.reshape(N_KV_HEADS, P * BLOCK, 2 * BLOCK), sink.reshape(N_KV_HEADS, P * BLOCK, 2 * BLOCK)


def _attn_specs(D, nb):
    kvw2 = 2 * N_KV_HEADS * HEAD_DIM
    cur = lambda b, i: (b * nb + i, 0)
    prev = lambda b, i: (jnp.maximum(b * nb + i - 1, 0), 0)
    return [
        pl.BlockSpec((BLOCK, D), cur),
        pl.BlockSpec((BLOCK, kvw2), cur),
        pl.BlockSpec((BLOCK, kvw2), prev),
        pl.BlockSpec((BLOCK, kvw2), cur),
        pl.BlockSpec((BLOCK, kvw2), prev),
    ]


def _attn_operands(kh, P, lo, q_ref, kc_ref, kp_ref, vc_ref, vp_ref):
    sl = slice(kh * LANES, (kh + 1) * LANES)

    def cat(prev_ref, cur_ref):
        d = jnp.concatenate([prev_ref[:, sl], cur_ref[:, sl]], axis=0)
        z = jnp.zeros_like(d)
        return jnp.concatenate([jnp.where(lo, d, z), jnp.where(lo, z, d)], axis=0)

    qt = jnp.concatenate([q_ref[:, (kh * P + pr) * LANES : (kh * P + pr + 1) * LANES] for pr in range(P)], axis=0)
    return qt, cat(kp_ref, kc_ref), cat(vp_ref, vc_ref)


def _attn_exp(s_all, bias, sink, tri, first):
    out = []
    for par in range(2):
        c0 = 2 * par * BLOCK
        s = jnp.where(tri, s_all[:, c0 + BLOCK : c0 + 2 * BLOCK], jnp.where(first, NEG, s_all[:, c0 : c0 + BLOCK]))
        s = s + bias[:, par * BLOCK : (par + 1) * BLOCK]
        snk = sink[:, par * BLOCK : (par + 1) * BLOCK]
        m = jnp.maximum(jnp.max(s, axis=-1, keepdims=True), snk)
        out.append((jnp.exp(s - m), jnp.exp(snk - m)))
    return out


def _unfold(x, tri):
    z = jnp.zeros_like(x)
    return jnp.concatenate([jnp.where(tri, z, x), jnp.where(tri, x, z)], axis=1)


def _attn_masks(R):
    lane = lax.broadcasted_iota(jnp.int32, (1, LANES), 1)
    row = lax.broadcasted_iota(jnp.int32, (R, BLOCK), 0) & (BLOCK - 1)
    col = lax.broadcasted_iota(jnp.int32, (R, BLOCK), 1)
    return lane, lane < HEAD_DIM, col <= row


def attn_fwd(q16, kd, vd, bias, sink, *, seq, n_seq):
    T, D = q16.shape
    nb = seq // BLOCK
    P = D // HEAD_DIM // N_KV_HEADS // 2
    R = P * BLOCK
    KV = range(N_KV_HEADS)

    def body(q_ref, kc_ref, kp_ref, vc_ref, vp_ref, bias_ref, sink_ref, o_ref):
        first = pl.program_id(1) == 0
        _, lo, tri = _attn_masks(R)
        ops = [_attn_operands(kh, P, lo, q_ref, kc_ref, kp_ref, vc_ref, vp_ref) for kh in KV]
        s_all = [_dot_nt(ops[kh][0], ops[kh][1]) for kh in KV]
        ex = [_attn_exp(s_all[kh], bias_ref[kh], sink_ref[kh], tri, first) for kh in KV]
        den = [[jnp.sum(e, axis=-1, keepdims=True) + es for e, es in ex[kh]] for kh in KV]
        lhs = [jnp.concatenate([_unfold(e, tri) for e, _ in ex[kh]], axis=1).astype(BF16) for kh in KV]
        o = [_dot(lhs[kh], ops[kh][2]) for kh in KV]
        for kh in KV:
            out = o[kh] / jnp.where(lo, den[kh][0], den[kh][1])
            for pr in range(P):
                t = kh * P + pr
                o_ref[:, t * LANES : (t + 1) * LANES] = out[pr * BLOCK : (pr + 1) * BLOCK, :].astype(BF16)

    return pl.pallas_call(
        body,
        name="attn_fwd",
        grid=(n_seq, nb),
        in_specs=_attn_specs(D, nb) + [_resident((N_KV_HEADS, R, 2 * BLOCK)), _resident((N_KV_HEADS, R, 2 * BLOCK))],
        out_specs=pl.BlockSpec((BLOCK, D), lambda b, i: (b * nb + i, 0)),
        out_shape=jax.ShapeDtypeStruct((T, D), BF16),
        compiler_params=_params(2),
    )(q16, kd, kd, vd, vd, bias, sink)


def attn_bwd(q16, kd, vd, do, bias, sink, *, seq, n_seq):
    T, D = q16.shape
    kvw2 = 2 * N_KV_HEADS * HEAD_DIM
    nb = seq // BLOCK
    G = D // HEAD_DIM // N_KV_HEADS
    P = G // 2
    R = P * BLOCK
    KV = range(N_KV_HEADS)

    def body(q_ref, kc_ref, kp_ref, vc_ref, vp_ref, do_ref, bias_ref, sink_ref,
             dq_ref, dkc_ref, dkp_ref, dvc_ref, dvp_ref, dsink_ref):
        first = pl.program_id(1) == 0

        @pl.when(jnp.logical_and(pl.program_id(0) == 0, first))
        def _():
            dsink_ref[...] = jnp.zeros_like(dsink_ref)

        lane, lo, tri = _attn_masks(R)
        ops = [_attn_operands(kh, P, lo, q_ref, kc_ref, kp_ref, vc_ref, vp_ref) for kh in KV]
        do16 = [jnp.concatenate([do_ref[:, (kh * P + pr) * LANES : (kh * P + pr + 1) * LANES] for pr in range(P)], axis=0)
                for kh in KV]
        s_all = [_dot_nt(ops[kh][0], ops[kh][1]) for kh in KV]
        dp_all = [_dot_nt(do16[kh], ops[kh][2]) for kh in KV]
        ex = [_attn_exp(s_all[kh], bias_ref[kh], sink_ref[kh], tri, first) for kh in KV]
        den = [[jnp.sum(e, axis=-1, keepdims=True) for e, _ in ex[kh]] for kh in KV]
        dsink = jnp.zeros((1, LANES), F32)
        pf, dsf = [], []
        for kh in KV:
            ps_, ds_ = [], []
            for par in range(2):
                e, es = ex[kh][par]
                inv = 1.0 / (den[kh][par] + es)
                p = e * inv
                c0 = 2 * par * BLOCK
                dp = jnp.where(tri, dp_all[kh][:, c0 + BLOCK : c0 + 2 * BLOCK], dp_all[kh][:, c0 : c0 + BLOCK])
                delta = jnp.sum(p * dp, axis=-1, keepdims=True)
                ds_.append(_unfold(p * (dp - delta), tri))
                ps_.append(_unfold(p, tri))
                dsr = -((es * inv) * delta)
                for pr in range(P):
                    hq = kh * G + 2 * pr + par
                    tot = jnp.sum(dsr[pr * BLOCK : (pr + 1) * BLOCK, :], axis=0, keepdims=True)
                    dsink = dsink + jnp.where(lane == hq, tot, 0.0)
            pf.append(jnp.concatenate(ps_, axis=1).astype(BF16))
            dsf.append(jnp.concatenate(ds_, axis=1).astype(BF16))
        dq = [_dot(dsf[kh], ops[kh][1]) for kh in KV]
        dk = [_dot_tn(dsf[kh], ops[kh][0]) for kh in KV]
        dv = [_dot_tn(pf[kh], do16[kh]) for kh in KV]
        dsink_ref[0:1, :] += dsink
        for kh in KV:
            sl = slice(kh * LANES, (kh + 1) * LANES)
            for pr in range(P):
                t = kh * P + pr
                dq_ref[:, t * LANES : (t + 1) * LANES] = dq[kh][pr * BLOCK : (pr + 1) * BLOCK, :]
            for full, prev_ref, cur_ref in ((dk[kh], dkp_ref, dkc_ref), (dv[kh], dvp_ref, dvc_ref)):
                dup = jnp.where(lo, full[: 2 * BLOCK, :], full[2 * BLOCK :, :])
                prev_ref[:, sl] = dup[:BLOCK, :]
                cur_ref[:, sl] = dup[BLOCK:, :]

    cur = lambda b, i: (b * nb + i, 0)
    kv_spec = pl.BlockSpec((BLOCK, kvw2), cur)
    kv_shape = jax.ShapeDtypeStruct((T, kvw2), F32)
    return pl.pallas_call(
        body,
        name="attn_bwd",
        grid=(n_seq, nb),
        in_specs=_attn_specs(D, nb)
        + [pl.BlockSpec((BLOCK, D), cur), _resident((N_KV_HEADS, R, 2 * BLOCK)), _resident((N_KV_HEADS, R, 2 * BLOCK))],
        out_specs=[pl.BlockSpec((BLOCK, D), cur), kv_spec, kv_spec, kv_spec, kv_spec, pl.BlockSpec((8, LANES), lambda b, i: (0, 0))],
        out_shape=[jax.ShapeDtypeStruct((T, D), F32), kv_shape, kv_shape, kv_shape, kv_shape, jax.ShapeDtypeStruct((8, LANES), F32)],
        compiler_params=_params(2),
    )(q16, kd, kd, vd, vd, do, bias, sink)


def qkv_bwd(dq, dkc, dkp, dvc, dvp, qkv, dres, x, gain, w_qkv, qg, kg, *, seq):
    T, D = x.shape
    kvw2 = dkc.shape[1]
    kvw = kvw2 // 2
    nqt, nkt = D // LANES, kvw // LANES
    nb = seq // BLOCK
    tm = 2 * BLOCK
    n = T // tm

    def body(dq_ref, dkc_ref, dkpa_ref, dkpb_ref, dvc_ref, dvpa_ref, dvpb_ref, qkv_ref, dres_ref, x_ref, g_ref, w_ref,
             qg_ref, kg_ref, dx_ref, dqkv_ref, h_ref, dg_ref, hg_ref):
        i = pl.program_id(0)

        @pl.when(i == 0)
        def _():
            dg_ref[...] = jnp.zeros_like(dg_ref)
            hg_ref[...] = jnp.zeros_like(hg_ref)

        lo = lax.broadcasted_iota(jnp.int32, (1, LANES), 1) < HEAD_DIM
        last = (2 * i + 1) % nb == nb - 1
        dkd = dkc_ref[...] + jnp.concatenate([dkpa_ref[...], jnp.where(last, 0.0, dkpb_ref[...])], axis=0)
        dvd = dvc_ref[...] + jnp.concatenate([dvpa_ref[...], jnp.where(last, 0.0, dvpb_ref[...])], axis=0)

        def undup(d, t):
            a, b = d[:, 2 * t * LANES : (2 * t + 1) * LANES], d[:, (2 * t + 1) * LANES : (2 * t + 2) * LANES]
            return jnp.where(lo, a + _swap_halves(a), b + _swap_halves(b))

        tiles = [qkv_ref[:, t * LANES : (t + 1) * LANES] for t in range(nqt + nkt)]
        gains = [qg_ref[...]] * nqt + [kg_ref[...]] * nkt
        dys = [dq_ref[:, t * LANES : (t + 1) * LANES] * SCALE for t in range(nqt)] + [undup(dkd, t) for t in range(nkt)]
        _, rs = _head_norm(tiles, gains, lo)
        dxs, dgs = _head_norm_bwd(tiles, rs, gains, dys, lo)
        for t in range(nqt + nkt):
            dqkv_ref[:, t * LANES : (t + 1) * LANES] = dxs[t].astype(BF16)
        for t in range(nkt):
            dqkv_ref[:, D + kvw + t * LANES : D + kvw + (t + 1) * LANES] = undup(dvd, t).astype(BF16)
        hg_ref[0:1, :] += functools.reduce(lambda a, b: a + b, dgs[:nqt])
        hg_ref[1:2, :] += functools.reduce(lambda a, b: a + b, dgs[nqt:])
        dh = _dot_nt(dqkv_ref[...], w_ref[...])
        xt = x_ref[...]
        r = _rms(xt)
        gn = g_ref[...]
        h_ref[...] = ((xt * r) * gn).astype(BF16)
        dx, dgn = _rms_bwd(xt, r, gn, dh)
        dg_ref[0:1, :] += dgn
        dx_ref[...] = dres_ref[...] + dx

    row = lambda i: (i, 0)
    nxt_a = pl.BlockSpec((BLOCK, kvw2), lambda i: (2 * i + 1, 0))
    nxt_b = pl.BlockSpec((BLOCK, kvw2), lambda i: (jnp.minimum(2 * i + 2, 2 * n - 1), 0))
    return pl.pallas_call(
        body,
        name="qkv_bwd",
        grid=(n,),
        in_specs=[
            pl.BlockSpec((tm, D), row),
            pl.BlockSpec((tm, kvw2), row),
            nxt_a,
            nxt_b,
            pl.BlockSpec((tm, kvw2), row),
            nxt_a,
            nxt_b,
            pl.BlockSpec((tm, D + kvw2), row),
            pl.BlockSpec((tm, D), row),
            pl.BlockSpec((tm, D), row),
            _resident((1, D)),
            _resident((D, D + kvw2)),
            _resident((1, LANES)),
            _resident((1, LANES)),
        ],
        out_specs=[
            pl.BlockSpec((tm, D), row),
            pl.BlockSpec((tm, D + kvw2), row),
            pl.BlockSpec((tm, D), row),
            pl.BlockSpec((8, D), lambda i: (0, 0)),
            pl.BlockSpec((8, LANES), lambda i: (0, 0)),
        ],
        out_shape=[
            jax.ShapeDtypeStruct((T, D), F32),
            jax.ShapeDtypeStruct((T, D + kvw2), BF16),
            jax.ShapeDtypeStruct((T, D), BF16),
            jax.ShapeDtypeStruct((8, D), F32),
            jax.ShapeDtypeStruct((8, LANES), F32),
        ],
        compiler_params=_params(1),
    )(dq, dkc, dkp, dkp, dvc, dvp, dvp, qkv, dres, x, gain, w_qkv, qg, kg)


def local_step(x, target, gains, w, *, seq, tm=256, tm_ffn=256, tm_conv=512, tk=2048, shards=None, ex=None):
    T, D = x.shape
    n_seq = T // seq
    nm, nf, qgain, kgain, sinks = gains
    H = D // HEAD_DIM
    tk, tk_long = min(tk, T), min(2 * tk, T)
    qg2, kg2 = jnp.tile(qgain, (1, 2)), jnp.tile(kgain, (1, 2))
    bias, sinkcol = _attn_tables(sinks, H)

    dist = shards is not None
    w = dict(w)

    plan = _Gather([shards["w_gu"][0], shards["w_d"][0]]) if dist else None
    (x1, bcx, y_conv, z16), got = conv_fwd(x, nm[0:1], w["w_in"], w["cw"], w["w_out"], seq=seq, tm=tm_conv, plan=plan)
    if dist:
        w["w_gu"], w["w_d"] = [cols_from_shards(got[0]), None], [got[1].reshape(-1, D), None]
    plan = _Gather([shards["w_qkv"], shards["w_o"], shards["w_gu"][1], shards["w_d"][1]]) if dist else None
    (x2, gu0), got = ffn_fwd(x1, nf[0:1], w["w_gu"][0], w["w_d"][0], tm=2 * tm_ffn, plan=plan)
    if dist:
        w["w_qkv"], w["w_o"] = cols_from_shards(got[0]), got[1].reshape(D, D)
        w["w_gu"][1], w["w_d"][1] = cols_from_shards(got[2]), got[3].reshape(-1, D)
    qkv, q16, kd, vd = qkv_proj(x2, nm[1:2], w["w_qkv"], qg2, kg2, tm=tm_conv)
    ao = attn_fwd(q16, kd, vd, bias, sinkcol, seq=seq, n_seq=n_seq)
    (x3, dx4, gu1, sse), _ = ffn_fwd(x2, nf[1:2], w["w_gu"][1], w["w_d"][1], tm=tm_ffn, attn=(ao, w["w_o"]), target=target)

    by_dest = lambda a: a.reshape(N_DEV, -1, a.shape[-1])
    gu_cols = 2 * MXU_TILE

    def send(name, *entries):
        if ex is None:
            return None
        items = [(a, False, key, (N_DEV,) + (() if layers is None else (layers,)) + a.shape[1:], layer)
                 for a, key, layer, layers in entries]
        return ex.start(items, name=name)

    (dx3, a16, dgu, h16, d16, dnf1, dx3_16, dao), _ = ffn_bwd(
        dx4, x3, nf[1:2], gu1, w["w_gu"][1], w["w_d"][1], tm=tm, w_o=w["w_o"])
    g_gu1 = shards_from_cols(wgrad(h16, dgu, name="wgrad_gu1", b_cols=gu_cols, flat=True, tk=tk_long))
    g_d1 = by_dest(wgrad(a16, d16, name="wgrad_d1", a_cols=a16.shape[1] // 2, tk=tk))
    tok = send("exchange_ffn1", (g_gu1, "w_gu", 1, 2), (g_d1, "w_d", 1, 2))
    g_o = by_dest(wgrad(ao, dx3_16, name="wgrad_o", tk=tk, after=tok))
    dq, dkc, dkp, dvc, dvp, dsinks = attn_bwd(q16, kd, vd, dao, bias, sinkcol, seq=seq, n_seq=n_seq)
    dx2, dqkv16, h16, dnm1, dgains = qkv_bwd(dq, dkc, dkp, dvc, dvp, qkv, dx3, x2, nm[1:2], w["w_qkv"], qg2, kg2, seq=seq)
    g_qkv = shards_from_cols(wgrad(h16, dqkv16, name="wgrad_qkv", tk=tk)[0])
    tok = send("exchange_attn", (g_o, "w_o", None, None), (g_qkv, "w_qkv", None, None))
    (dx1, a16, dgu, h16, d16, dnf0), _ = ffn_bwd(dx2, x1, nf[0:1], gu0, w["w_gu"][0], w["w_d"][0], tm=tm, after=tok)
    g_gu0 = shards_from_cols(wgrad(h16, dgu, name="wgrad_gu0", b_cols=gu_cols, flat=True, tk=tk_long))
    tok = send("exchange_gu0", (g_gu0, "w_gu", 0, 2))
    g_d0 = by_dest(wgrad(a16, d16, name="wgrad_d0", a_cols=a16.shape[1] // 2, tk=tk, after=tok))
    tok = send("exchange_d0", (g_d0, "w_d", 0, 2))
    (gx, dbcx, h16, d16, dcw, dnm0), _ = conv_bwd(
        dx1, x, nm[0:1], bcx, y_conv, w["cw"], w["w_in"], w["w_out"], seq=seq, tm=tm_conv, after=tok)
    g_out = by_dest(wgrad(z16, d16, name="wgrad_out", tk=tk))
    g_cw = dcw[0:3].reshape(3, N_DEV, D // N_DEV).transpose(1, 0, 2)
    tok = send("exchange_out", (g_out, "w_out", None, None), (g_cw, "cw", None, None))
    g_in = wgrad(h16, dbcx, name="wgrad_in", b_cols=3 * D // N_DEV, group=2, tk=tk_long, after=tok)
    g = dict(w_in=g_in, cw=g_cw, w_out=g_out, w_o=g_o, w_qkv=g_qkv, w_gu=[g_gu0, g_gu1], w_d=[g_d0, g_d1])
    small = dict(nm0=dnm0, nm1=dnm1, nf0=dnf0, nf1=dnf1, gains=dgains, sinks=dsinks)
    return sse, gx, g, small


def _adamw_math(g, w, m, v):
    m = ADAM_B1 * m + (1.0 - ADAM_B1) * g
    v = ADAM_B2 * v + (1.0 - ADAM_B2) * (g * g)
    m_hat = m / (1.0 - ADAM_B1 ** ADAM_STEP)
    v_hat = v / (1.0 - ADAM_B2 ** ADAM_STEP)
    delta = -ADAM_LR * (m_hat / (jnp.sqrt(v_hat) + ADAM_EPS) + ADAM_WD * w)
    return delta, m, v


def adamw(parts, owns, w, m, v, *, name, after=None):
    n, LR, C = parts.shape
    L = len(owns)
    R = LR // L
    tr = R
    for cand in (256, 128, 88, 64, 32, 16, 8):
        if R > cand and R % cand == 0:
            tr = cand
            break
    per_layer = R // tr
    extra = [] if after is None else [after]

    def body(me_ref, p_ref, *rest):
        own_refs, (w_ref, m_ref, v_ref) = rest[:L], rest[L : L + 3]
        g_ref, d_ref, mo_ref, vo_ref = rest[L + 3 + len(extra) :]
        layer = pl.program_id(0) // per_layer
        mine = own_refs[0][...].astype(F32)
        for j in range(1, L):
            mine = jnp.where(layer == j, own_refs[j][...].astype(F32), mine)
        g = None
        for s in range(n):
            share = jnp.where(me_ref[0] == s, mine, p_ref[s].astype(F32))
            g = share if g is None else g + share
        g_ref[...] = g
        d_ref[...], mo_ref[...], vo_ref[...] = _adamw_math(g, w_ref[...], m_ref[...], v_ref[...])

    blk = pl.BlockSpec((tr, C), lambda i, me: (i, 0))
    own_specs = [pl.BlockSpec((None, tr, C), lambda i, me: (me[0], i % per_layer, 0)) if o.ndim == 3
                 else pl.BlockSpec((tr, C), lambda i, me: (i % per_layer, 0)) for o in owns]
    me = (4 * lax.axis_index("x") + 2 * lax.axis_index("y") + lax.axis_index("c")).astype(jnp.int32).reshape(1)
    return pl.pallas_call(
        body,
        name=name,
        grid_spec=pltpu.PrefetchScalarGridSpec(
            num_scalar_prefetch=1,
            grid=(LR // tr,),
            in_specs=[pl.BlockSpec((n, tr, C), lambda i, me: (0, i, 0))] + own_specs + [blk, blk, blk] + _any_specs(len(extra)),
            out_specs=[blk] * 4,
        ),
        out_shape=[jax.ShapeDtypeStruct((LR, C), F32)] * 4,
        compiler_params=_params(1),
    )(me, parts, *owns, w, m, v, *extra)


def pack_small(small, sse, D):
    W = max(D, 2 * LANES)

    def body(nm0, nm1, nf0, nf1, gains, sinks, sse_ref, o_ref):
        o_ref[...] = jnp.zeros_like(o_ref)
        o_ref[0:1, :D] = nm0[0:1, :]
        o_ref[1:2, :D] = nm1[0:1, :]
        o_ref[2:3, :D] = nf0[0:1, :]
        o_ref[3:4, :D] = nf1[0:1, :]
        gq = gains[0:1, :] + pltpu.roll(gains[0:1, :], HEAD_DIM, 1)
        gk = gains[1:2, :] + pltpu.roll(gains[1:2, :], HEAD_DIM, 1)
        lane = lax.broadcasted_iota(jnp.int32, (1, LANES), 1)
        o_ref[4:5, :LANES] = jnp.where(lane < HEAD_DIM, gq, gk)
        o_ref[4:5, LANES : 2 * LANES] = sinks[0:1, :]
        o_ref[5:6, :LANES] = sse_ref[0:1, :] * (0.5 / D)

    return pl.pallas_call(
        body,
        name="pack_small",
        out_shape=jax.ShapeDtypeStruct((8, W), F32),
    )(small["nm0"], small["nm1"], small["nf0"], small["nf1"], small["gains"], small["sinks"], sse)


def _pack_small_params(nm, nf, qg, kg, sk, D):
    W = max(D, 2 * LANES)
    row4 = jnp.concatenate([qg.reshape(-1), kg.reshape(-1), jnp.zeros((LANES - 2 * HEAD_DIM,), F32), sk.reshape(-1)])
    row4 = jnp.pad(row4, (0, W - row4.shape[0]))
    rows = [jnp.pad(r, (0, W - D)) for r in (nm[0], nm[1], nf[0], nf[1])] + [row4]
    return jnp.concatenate([jnp.stack(rows), jnp.zeros((3, W), F32)], axis=0)


def _unpack_small(a, D, H):
    nm = a[0:2, :D]
    nf = a[2:4, :D]
    qg = a[4:5, 0:HEAD_DIM]
    kg = a[4:5, HEAD_DIM : 2 * HEAD_DIM]
    sk = a[4:5, LANES : LANES + H]
    return qg, kg, sk, nm, nf


def kernel(x, conv_w_in, conv_w, conv_w_out, attn_w_qkv, attn_q_gain, attn_k_gain, attn_sinks, attn_w_o, norm_mixer, norm_ffn, ffn_w_gate_up, ffn_w_down, loss_target, m_conv_w_in, m_conv_w, m_conv_w_out, m_attn_w_qkv, m_attn_q_gain, m_attn_k_gain, m_attn_sinks, m_attn_w_o, m_norm_mixer, m_norm_ffn, m_ffn_w_gate_up, m_ffn_w_down, v_conv_w_in, v_conv_w, v_conv_w_out, v_attn_w_qkv, v_attn_q_gain, v_attn_k_gain, v_attn_sinks, v_attn_w_o, v_norm_mixer, v_norm_ffn, v_ffn_w_gate_up, v_ffn_w_down):
    n_seq, seq, D = x.shape
    T = n_seq * seq
    H = D // HEAD_DIM
    L = ffn_w_gate_up.shape[0]

    full = run_plan(_Gather([conv_w_in[0].astype(BF16), conv_w[0], conv_w_out[0].astype(BF16)]), name="gather_conv_weights")
    w = dict(w_in=cols_from_shards(full[0]), cw=full[1].transpose(1, 0, 2).reshape(3, D),
             w_out=full[2].reshape(D, D))
    shards = dict(w_gu=[ffn_w_gate_up[l].astype(BF16) for l in range(L)], w_d=[ffn_w_down[l].astype(BF16) for l in range(L)],
                  w_qkv=attn_w_qkv[0].astype(BF16), w_o=attn_w_o[0].astype(BF16))
    gains = (norm_mixer, norm_ffn, attn_q_gain, attn_k_gain, attn_sinks)
    ex = Exchange()
    sse, gx, g, small = local_step(x.reshape(T, D), loss_target.reshape(T, D), gains, w, seq=seq, shards=shards, ex=ex)
    zones, own = ex.wait([g["w_in"]], name="exchange_wait")

    packed = pack_small(small, sse, D)
    token = ex.start([(g["w_in"], False, "w_in", g["w_in"].shape, None),
                      (packed, True, "small", (N_DEV,) + packed.shape, None)], name="exchange_last")

    def flat(a):
        return a.reshape(-1, a.shape[-1])

    big = [conv_w_in, conv_w, conv_w_out, attn_w_qkv, attn_w_o, ffn_w_gate_up, ffn_w_down]
    big_m = [m_conv_w_in, m_conv_w, m_conv_w_out, m_attn_w_qkv, m_attn_w_o, m_ffn_w_gate_up, m_ffn_w_down]
    big_v = [v_conv_w_in, v_conv_w, v_conv_w_out, v_attn_w_qkv, v_attn_w_o, v_ffn_w_gate_up, v_ffn_w_down]
    keys = ["w_in", "cw", "w_out", "w_qkv", "w_o", "w_gu", "w_d"]

    def update(b, zones, own, after=None):
        zone = zones[keys[b]]
        parts = zone.reshape(N_DEV, -1, zone.shape[-1])
        layers = [None] if zone.ndim == 3 else range(zone.shape[1])
        outs = adamw(parts, [own[(keys[b], l)] for l in layers], flat(big[b]), flat(big_m[b]), flat(big_v[b]),
                     name="adamw_" + keys[b], after=after)
        return [o.reshape(big[b].shape) for o in outs]

    res = [None] + [update(b, zones, own, after=token) for b in range(1, 7)]
    zones, own = ex.wait([r[0] for r in res[1:]], name="exchange_last_wait")
    res[0] = update(0, zones, own)
    sw = _pack_small_params(norm_mixer, norm_ffn, attn_q_gain, attn_k_gain, attn_sinks, D)
    sm = _pack_small_params(m_norm_mixer, m_norm_ffn, m_attn_q_gain, m_attn_k_gain, m_attn_sinks, D)
    sv = _pack_small_params(v_norm_mixer, v_norm_ffn, v_attn_q_gain, v_attn_k_gain, v_attn_sinks, D)
    souts = adamw(zones["small"], [own[("small", None)]], sw, sm, sv, name="adamw_small")
    sres = [_unpack_small(o, D, H) for o in souts]
    loss = souts[0][5, 0]

    def ordered(i):
        r, s = [r[i] for r in res], sres[i]
        return [r[0], r[1], r[2], r[3], s[0], s[1], s[2], r[4], s[3], s[4], r[5], r[6]]

    return (loss, gx.reshape(n_seq, seq, D), *ordered(0), *ordered(1), *ordered(2), *ordered(3))
```

```python
import functools
import math

import jax
import jax.numpy as jnp
from jax import lax
from jax.experimental import pallas as pl
from jax.experimental.pallas import tpu as pltpu

F32 = jnp.float32
BF16 = jnp.bfloat16

EPS = 1e-6
HEAD_DIM = 64
N_KV_HEADS = 4
BLOCK = 128
LANES = 128
N_DEV = 8
NEG = -1e30
SCALE = 1.0 / math.sqrt(HEAD_DIM)

ADAM_LR = 0.001
ADAM_B1 = 0.9
ADAM_B2 = 0.999
ADAM_EPS = 1e-08
ADAM_WD = 0.01
ADAM_STEP = 10

V7X_VMEM_BYTES = 64 * 1024 * 1024
VMEM_LIMIT = V7X_VMEM_BYTES - 2 * 1024 * 1024
MESH = pl.DeviceIdType.MESH

_NT = (((1,), (1,)), ((), ()))
_TN = (((0,), (0,)), ((), ()))


def _params(n_grid):
    return pltpu.CompilerParams(dimension_semantics=("arbitrary",) * n_grid, vmem_limit_bytes=VMEM_LIMIT)


def _resident(shape):
    nd = len(shape)
    return pl.BlockSpec(shape, lambda *_: (0,) * nd, pipeline_mode=pl.Buffered(1))


def _rms(x):
    return lax.rsqrt(jnp.mean(x * x, axis=-1, keepdims=True) + EPS)


def _rms_bwd(x, r, gain, dh):
    xn = x * r
    dxn = dh * gain
    dx = r * (dxn - xn * jnp.mean(dxn * xn, axis=-1, keepdims=True))
    return dx, jnp.sum(dh * xn, axis=0, keepdims=True)


def _dot(a, b):
    return jnp.dot(a, b, preferred_element_type=F32)


def _dot_nt(a, b):
    return lax.dot_general(a, b, _NT, preferred_element_type=F32)


def _dot_tn(a, b):
    return lax.dot_general(a, b, _TN, preferred_element_type=F32)


def _place():
    return lax.axis_index("x"), lax.axis_index("y"), lax.axis_index("c")


def _flip(v, bit):
    return 1 - v if bit else v


def _slot(px, py, pc):
    return 4 * px + 2 * py + pc


class _Gather:
    def __init__(self, shards):
        nt = len(shards)
        self.nt = nt
        self.inputs = list(shards)
        self.out_shapes = [jax.ShapeDtypeStruct((N_DEV,) + s.shape, s.dtype) for s in shards]
        self.scratch = [pltpu.SemaphoreType.DMA((nt, 7)), pltpu.SemaphoreType.DMA((nt, 7)), pltpu.SemaphoreType.DMA((nt,))]
        self.aliases = {}

    def phases(self, total):
        assert total >= 3
        return [(0, self.start), (total - 2, self.forward), (total - 1, self.finish)]

    def _copies(self, ins, outs, sems):
        send_sems, recv_sems, loc_sems = sems
        x, y, c = _place()
        me = _slot(x, y, c)
        sib = (x, y, 1 - c)
        chips = [(_flip(x, k >> 1), _flip(y, k & 1)) for k in (1, 2, 3)]

        def copy(t, k, src, dst_slot, to):
            return pltpu.make_async_remote_copy(
                src_ref=src, dst_ref=outs[t].at[dst_slot], send_sem=send_sems.at[t, k], recv_sem=recv_sems.at[t, k],
                device_id=to, device_id_type=MESH)

        local = [pltpu.make_async_copy(ins[t], outs[t].at[me], loc_sems.at[t]) for t in range(self.nt)]
        first, passed, arrive_ici, arrive_sib = [], [], [], []
        for t in range(self.nt):
            first.append(copy(t, 0, ins[t], me, sib))
            s = _slot(x, y, 1 - c)
            arrive_sib.append(copy(t, 0, outs[t].at[s], s, sib))
            for j, (px, py) in enumerate(chips):
                first.append(copy(t, 1 + j, ins[t], me, (px, py, c)))
                s = _slot(px, py, c)
                arrive_ici.append(copy(t, 1 + j, outs[t].at[s], s, sib))
                passed.append(copy(t, 4 + j, outs[t].at[s], s, sib))
                s = _slot(px, py, 1 - c)
                arrive_sib.append(copy(t, 4 + j, outs[t].at[s], s, sib))
        return local, first, passed, arrive_ici, arrive_sib

    def start(self, ins, outs, sems):
        local, first, _, _, _ = self._copies(ins, outs, sems)
        for cp in local + first:
            cp.start()

    def forward(self, ins, outs, sems):
        _, _, passed, arrive_ici, _ = self._copies(ins, outs, sems)
        for arrival, fwd in zip(arrive_ici, passed):
            arrival.wait_recv()
            fwd.start()

    def finish(self, ins, outs, sems):
        local, first, passed, _, arrive_sib = self._copies(ins, outs, sems)
        for cp in arrive_sib:
            cp.wait_recv()
        for cp in first + passed:
            cp.wait_send()
        for cp in local:
            cp.wait()


def _any_specs(n):
    return [pl.BlockSpec(memory_space=pl.ANY)] * n


def run_plan(plan, *, name):
    def body(*refs):
        n_in, n_out = len(plan.inputs), len(plan.out_shapes)
        ins, outs, sems = refs[:n_in], refs[n_in : n_in + n_out], refs[n_in + n_out :]
        for _, phase in plan.phases(3):
            phase(ins, outs, sems)

    return pl.pallas_call(
        body,
        name=name,
        in_specs=_any_specs(len(plan.inputs)),
        out_specs=_any_specs(len(plan.out_shapes)),
        out_shape=plan.out_shapes,
        scratch_shapes=plan.scratch,
        input_output_aliases=plan.aliases,
    )(*plan.inputs)


_HBM = pl.BlockSpec(memory_space=pltpu.HBM)
_SEM = pl.BlockSpec(memory_space=pltpu.SEMAPHORE)
_DATAFLOW = pltpu.SideEffectType.DATAFLOW_SIDE_EFFECTING


class Exchange:
    def __init__(self):
        self.zones = {}
        self.pending = []
        self.sources = []

    def start(self, items, *, name):
        nt = len(items)
        keys = list(dict.fromkeys(it[2] for it in items))
        for a, _, key, shape, _ in items:
            if key not in self.zones:
                self.zones[key] = lax.empty(shape, a.dtype)
        nz = len(keys)

        def body(*refs):
            ins, zones, sems, token = refs[:nt], refs[nt : nt + nz], refs[nt + nz : nt + nz + 2 * nt], refs[-1]
            x, y, c = _place()
            me = _slot(x, y, c)
            for k in range(1, N_DEV):
                px, py, pc = _flip(x, (k >> 2) & 1), _flip(y, (k >> 1) & 1), _flip(c, k & 1)
                for t, (_, whole, key, _, layer) in enumerate(items):
                    zone = zones[keys.index(key)]
                    pltpu.make_async_remote_copy(
                        src_ref=ins[t] if whole else ins[t].at[_slot(px, py, pc)],
                        dst_ref=zone.at[me] if layer is None else zone.at[me, layer],
                        send_sem=sems[2 * t], recv_sem=sems[2 * t + 1], device_id=(px, py, pc), device_id_type=MESH).start()
            token[...] = jnp.zeros_like(token)

        bufs = [pltpu.with_memory_space_constraint(b, pltpu.HBM) for b in [it[0] for it in items] + [self.zones[k] for k in keys]]
        outs = pl.pallas_call(
            body,
            name=name,
            in_specs=[_HBM] * (nt + nz),
            out_specs=[_SEM] * (2 * nt) + [_HBM] * (nt + nz) + [pl.BlockSpec(memory_space=pltpu.VMEM)],
            out_shape=[pltpu.SemaphoreType.DMA(())] * (2 * nt) + [pltpu.HBM(b.shape, b.dtype) for b in bufs]
            + [jax.ShapeDtypeStruct((8, LANES), F32)],
            input_output_aliases={i: 2 * nt + i for i in range(nt + nz)},
            compiler_params=pltpu.CompilerParams(has_side_effects=_DATAFLOW),
        )(*bufs)
        for t, (_, _, key, _, layer) in enumerate(items):
            self.pending.append((outs[2 * t], outs[2 * t + 1], key, layer))
        self.sources += [((it[2], it[4]), a) for it, a in zip(items, outs[2 * nt : 3 * nt])]
        for i, key in enumerate(keys):
            self.zones[key] = outs[3 * nt + i]
        return outs[-1]

    def wait(self, after, *, name):
        pending, keys = self.pending, list(self.zones)
        names, sources = [n for n, _ in self.sources], [a for _, a in self.sources]
        ns, nz, npend = len(sources), len(keys), len(pending)
        self.pending, self.sources = [], []

        def body(*refs):
            zones, sems = refs[ns : ns + nz], refs[ns + nz : ns + nz + 2 * npend]
            x, y, c = _place()
            for i, (_, _, key, layer) in enumerate(pending):
                zone = zones[keys.index(key)]
                rows = pl.ds(0, N_DEV - 1)
                seven = zone.at[rows] if layer is None else zone.at[rows, layer]
                pltpu.make_async_remote_copy(
                    src_ref=seven, dst_ref=seven, send_sem=sems[2 * i], recv_sem=sems[2 * i + 1],
                    device_id=(x, y, c), device_id_type=MESH).wait()

        bufs = list(sources) + [self.zones[k] for k in keys]
        flat_sems = [s for p in pending for s in p[:2]]
        outs = pl.pallas_call(
            body,
            name=name,
            in_specs=[_HBM] * (ns + nz) + [_SEM] * (2 * npend) + _any_specs(len(after)),
            out_specs=[_HBM] * (ns + nz),
            out_shape=[pltpu.HBM(b.shape, b.dtype) for b in bufs],
            input_output_aliases={i: i for i in range(ns + nz)},
            compiler_params=pltpu.CompilerParams(has_side_effects=_DATAFLOW),
        )(*bufs, *flat_sems, *after)
        self.zones = {}
        return dict(zip(keys, outs[ns:])), dict(zip(names, outs[:ns]))


def _call(body, *, name, grid, in_specs, out_specs, out_shape, args, scratch=(), plan=None, after=None):
    if after is not None:
        inner, n_real = body, len(in_specs)
        body = lambda *refs: inner(*refs[:n_real], *refs[n_real + 1 :])
        in_specs, args = list(in_specs) + _any_specs(1), list(args) + [after]
    n_in, n_out, n_scr = len(in_specs), len(out_specs), len(scratch)
    if plan is None:
        outs = pl.pallas_call(
            body, name=name, grid=grid, in_specs=in_specs, out_specs=out_specs, out_shape=out_shape,
            scratch_shapes=list(scratch), compiler_params=_params(len(grid)))(*args)
        return outs, None
    c_in, c_out = len(plan.inputs), len(plan.out_shapes)
    phases = plan.phases(math.prod(grid))

    def full(*refs):
        a, refs = refs[:n_in], refs[n_in:]
        ci, refs = refs[:c_in], refs[c_in:]
        o, refs = refs[:n_out], refs[n_out:]
        co, refs = refs[:c_out], refs[c_out:]
        s, cs = refs[:n_scr], refs[n_scr:]
        step = pl.program_id(0)
        for d in range(1, len(grid)):
            step = step * grid[d] + pl.program_id(d)
        for at, phase in phases:
            if at == 0:
                pl.when(step == 0)(functools.partial(phase, ci, co, cs))
        body(*a, *o, *s)
        for at, phase in phases:
            if at > 0:
                pl.when(step == at)(functools.partial(phase, ci, co, cs))

    outs = pl.pallas_call(
        full,
        name=name,
        grid=grid,
        in_specs=list(in_specs) + _any_specs(c_in),
        out_specs=list(out_specs) + _any_specs(c_out),
        out_shape=list(out_shape) + plan.out_shapes,
        scratch_shapes=list(scratch) + plan.scratch,
        input_output_aliases={n_in + i: n_out + t for i, t in plan.aliases.items()},
        compiler_params=_params(len(grid)),
    )(*args, *plan.inputs)
    return outs[:n_out], outs[n_out:]


def _row_tile(R):
    return 256 if R % 256 == 0 else R


def cols_from_shards(a):
    n, R, C = a.shape
    tr = _row_tile(R)

    def body(i_ref, o_ref):
        for s in range(n):
            o_ref[:, s * C : (s + 1) * C] = i_ref[s]

    return pl.pallas_call(
        body,
        name="cols_from_shards",
        grid=(R // tr,),
        in_specs=[pl.BlockSpec((n, tr, C), lambda i: (0, i, 0))],
        out_specs=pl.BlockSpec((tr, n * C), lambda i: (i, 0)),
        out_shape=jax.ShapeDtypeStruct((R, n * C), a.dtype),
        compiler_params=_params(1),
    )(a)


def shards_from_cols(a):
    R, W = a.shape
    C = W // N_DEV
    tr = _row_tile(R)

    def body(i_ref, o_ref):
        for s in range(N_DEV):
            o_ref[s] = i_ref[:, s * C : (s + 1) * C]

    return pl.pallas_call(
        body,
        name="shards_from_cols",
        grid=(R // tr,),
        in_specs=[pl.BlockSpec((tr, W), lambda i: (i, 0))],
        out_specs=pl.BlockSpec((N_DEV, tr, C), lambda i: (0, i, 0)),
        out_shape=jax.ShapeDtypeStruct((N_DEV, R, C), a.dtype),
        compiler_params=_params(1),
    )(a)


def _shift_down(u, prev8, row, n):
    out = pltpu.roll(u, n, 0)
    for k in range(n):
        out = jnp.where(row == k, prev8[8 - n + k : 8 - n + k + 1, :], out)
    return out


def _shift_up(u, next8, row, n, tm):
    out = pltpu.roll(u, tm - n, 0)
    for k in range(n):
        out = jnp.where(row == tm - n + k, next8[k : k + 1, :], out)
    return out


def conv_fwd(x, gain, w_in, cw, w_out, *, seq, tm, plan=None):
    T, D = x.shape
    tps = seq // tm

    def body(x_ref, g_ref, win_ref, cw_ref, wout_ref, x1_ref, bcx_ref, y_ref, z_ref, carry_ref):
        i = pl.program_id(0)

        @pl.when(i % tps == 0)
        def _():
            carry_ref[...] = jnp.zeros_like(carry_ref)

        xt = x_ref[...]
        h = ((xt * _rms(xt)) * g_ref[...]).astype(BF16)
        bcx = _dot(h, win_ref[...])
        bcx_ref[...] = bcx
        b, c, xv = bcx[:, :D], bcx[:, D : 2 * D], bcx[:, 2 * D :]
        u = b * xv
        row = lax.broadcasted_iota(jnp.int32, u.shape, 0)
        prev = carry_ref[...]
        u1 = _shift_down(u, prev, row, 1)
        u2 = _shift_down(u, prev, row, 2)
        carry_ref[...] = u[tm - 8 :, :]
        cwv = cw_ref[...]
        y = cwv[0:1, :] * u2 + cwv[1:2, :] * u1 + cwv[2:3, :] * u
        y_ref[...] = y
        z = (c * y).astype(BF16)
        z_ref[...] = z
        x1_ref[...] = xt + _dot(z, wout_ref[...])

    tile = pl.BlockSpec((tm, D), lambda i: (i, 0))
    return _call(
        body,
        plan=plan,
        args=(x, gain, w_in, cw, w_out),
        name="conv_fwd",
        grid=(T // tm,),
        in_specs=[
            pl.BlockSpec((tm, D), lambda i: (i, 0)),
            _resident((1, D)),
            _resident((D, 3 * D)),
            _resident((3, D)),
            _resident((D, D)),
        ],
        out_specs=[tile, pl.BlockSpec((tm, 3 * D), lambda i: (i, 0)), tile, tile],
        out_shape=[jax.ShapeDtypeStruct((T, D), F32), jax.ShapeDtypeStruct((T, 3 * D), F32),
                   jax.ShapeDtypeStruct((T, D), F32), jax.ShapeDtypeStruct((T, D), BF16)],
        scratch=[pltpu.VMEM((8, D), F32)],
    )


def conv_bwd(dx1, x, gain, bcx, y, cw, w_in, w_out, *, seq, tm, after=None):
    T, D = x.shape
    n = T // tm
    tps = seq // tm

    def body(d_ref, x_ref, g_ref, bcx_ref, y_ref, cw_ref, win_ref, wout_ref,
             gx_ref, dbcx_ref, h_ref, d16_ref, dcw_ref, dg_ref, carry_ref):
        i = pl.program_id(0)
        t = n - 1 - i

        @pl.when(i == 0)
        def _():
            dcw_ref[...] = jnp.zeros_like(dcw_ref)
            dg_ref[...] = jnp.zeros_like(dg_ref)

        @pl.when(t % tps == tps - 1)
        def _():
            carry_ref[...] = jnp.zeros_like(carry_ref)

        d = d_ref[...]
        d16 = d.astype(BF16)
        d16_ref[...] = d16
        dz = _dot_nt(d16, wout_ref[...])
        bcx = bcx_ref[...]
        b, c, xv = bcx[:, :D], bcx[:, D : 2 * D], bcx[:, 2 * D :]
        u = b * xv
        row = lax.broadcasted_iota(jnp.int32, u.shape, 0)
        cwv = cw_ref[...]
        dc = dz * y_ref[...]
        dy = dz * c
        nxt = carry_ref[...]
        dy1 = _shift_up(dy, nxt, row, 1, tm)
        dy2 = _shift_up(dy, nxt, row, 2, tm)
        carry_ref[...] = dy[0:8, :]
        dcw_ref[0:1, :] += jnp.sum(dy2 * u, axis=0, keepdims=True)
        dcw_ref[1:2, :] += jnp.sum(dy1 * u, axis=0, keepdims=True)
        dcw_ref[2:3, :] += jnp.sum(dy * u, axis=0, keepdims=True)
        du = cwv[2:3, :] * dy + cwv[1:2, :] * dy1 + cwv[0:1, :] * dy2
        dbcx_ref[:, :D] = (du * xv).astype(BF16)
        dbcx_ref[:, D : 2 * D] = dc.astype(BF16)
        dbcx_ref[:, 2 * D :] = (du * b).astype(BF16)
        dh = _dot_nt(dbcx_ref[...], win_ref[...])
        xt = x_ref[...]
        r = _rms(xt)
        gn = g_ref[...]
        h_ref[...] = ((xt * r) * gn).astype(BF16)
        dx, dgn = _rms_bwd(xt, r, gn, dh)
        dg_ref[0:1, :] += dgn
        gx_ref[...] = d + dx

    rev = lambda i: (n - 1 - i, 0)
    return _call(
        body,
        after=after,
        args=(dx1, x, gain, bcx, y, cw, w_in, w_out),
        name="conv_bwd",
        grid=(n,),
        in_specs=[
            pl.BlockSpec((tm, D), rev),
            pl.BlockSpec((tm, D), rev),
            _resident((1, D)),
            pl.BlockSpec((tm, 3 * D), rev),
            pl.BlockSpec((tm, D), rev),
            _resident((3, D)),
            _resident((D, 3 * D)),
            _resident((D, D)),
        ],
        out_specs=[
            pl.BlockSpec((tm, D), rev),
            pl.BlockSpec((tm, 3 * D), rev),
            pl.BlockSpec((tm, D), rev),
            pl.BlockSpec((tm, D), rev),
            pl.BlockSpec((8, D), lambda i: (0, 0)),
            pl.BlockSpec((8, D), lambda i: (0, 0)),
        ],
        out_shape=[
            jax.ShapeDtypeStruct((T, D), F32),
            jax.ShapeDtypeStruct((T, 3 * D), BF16),
            jax.ShapeDtypeStruct((T, D), BF16),
            jax.ShapeDtypeStruct((T, D), BF16),
            jax.ShapeDtypeStruct((8, D), F32),
            jax.ShapeDtypeStruct((8, D), F32),
        ],
        scratch=[pltpu.VMEM((8, D), F32)],
    )


MXU_TILE = 256
FFN_CHUNK = 4 * MXU_TILE


def _sigmoid(g):
    return 1.0 / (1.0 + jnp.exp(-g))


def _ffn_chunks(F):
    assert F % MXU_TILE == 0
    return [(s, min(FFN_CHUNK, F - s)) for s in range(0, F, FFN_CHUNK)]


def ffn_fwd(x, gain, w_gu, w_d, *, tm, plan=None, attn=None, target=None):
    T, D = x.shape
    F = w_d.shape[0]
    row = lambda i: (i, 0)
    tile = pl.BlockSpec((tm, D), row)

    def body(*refs):
        refs = list(refs)
        x_ref, g_ref, wgu_ref, wd_ref = refs[:4]
        del refs[:4]
        if attn is not None:
            ao_ref, wo_ref = refs[:2]
            del refs[:2]
        if target is not None:
            t_ref = refs.pop(0)
        if attn is not None:
            xin_ref = refs.pop(0)
        xo_ref, gu_ref = refs[:2]
        xt = x_ref[...]
        if attn is not None:
            xt = xt + _dot(ao_ref[...], wo_ref[...])
            xin_ref[...] = xt
        h = ((xt * _rms(xt)) * g_ref[...]).astype(BF16)
        acc = xt
        for s, n in _ffn_chunks(F):
            g = _dot(h, wgu_ref[:, s : s + n])
            u = _dot(h, wgu_ref[:, F + s : F + s + n])
            gu_ref[:, s : s + n] = g
            gu_ref[:, F + s : F + s + n] = u
            a = ((g * _sigmoid(g)) * u).astype(BF16)
            acc = acc + _dot(a, wd_ref[s : s + n, :])
        if target is None:
            xo_ref[...] = acc
        else:
            s_ref = refs[2]

            @pl.when(pl.program_id(0) == 0)
            def _():
                s_ref[...] = jnp.zeros_like(s_ref)

            e = acc - t_ref[...]
            xo_ref[...] = e * (1.0 / D)
            s_ref[...] += jnp.sum(jnp.sum(e * e, axis=-1, keepdims=True), axis=0, keepdims=True)

    args = [x, gain, w_gu, w_d]
    in_specs = [tile, _resident((1, D)), _resident((D, 2 * F)), _resident((F, D))]
    out_specs = [tile, pl.BlockSpec((tm, 2 * F), row)]
    out_shape = [jax.ShapeDtypeStruct((T, D), F32), jax.ShapeDtypeStruct((T, 2 * F), F32)]
    if attn is not None:
        args += list(attn)
        in_specs += [pl.BlockSpec((tm, attn[0].shape[1]), row), _resident(attn[1].shape)]
        out_specs.insert(0, tile)
        out_shape.insert(0, jax.ShapeDtypeStruct((T, D), F32))
    if target is not None:
        args.append(target)
        in_specs.append(tile)
        out_specs.append(pl.BlockSpec((8, LANES), lambda i: (0, 0)))
        out_shape.append(jax.ShapeDtypeStruct((8, LANES), F32))
    return _call(body, plan=plan, args=args, name="ffn_fwd", grid=(T // tm,), in_specs=in_specs, out_specs=out_specs,
                 out_shape=out_shape)


def ffn_bwd(dxo, x, gain, gu, w_gu, w_d, *, tm, after=None, w_o=None):
    T, D = x.shape
    F = w_d.shape[0]

    def body(d_ref, x_ref, g_ref, gu_ref, wgu_ref, wd_ref, *rest):
        if w_o is not None:
            wo_ref, rest = rest[0], rest[1:]
        dx_ref, a_ref, dgu_ref, h_ref, d16_ref, dg_ref = rest[:6]

        @pl.when(pl.program_id(0) == 0)
        def _():
            dg_ref[...] = jnp.zeros_like(dg_ref)

        d = d_ref[...]
        d16 = d.astype(BF16)
        d16_ref[...] = d16
        dh = jnp.zeros((tm, D), F32)
        for c0, n in _ffn_chunks(F):
            g = gu_ref[:, c0 : c0 + n]
            u = gu_ref[:, F + c0 : F + c0 + n]
            da = _dot_nt(d16, wd_ref[c0 : c0 + n, :])
            s = _sigmoid(g)
            sg = g * s
            a_ref[:, c0 : c0 + n] = (sg * u).astype(BF16)
            dg16 = (da * u * (s + sg * (1.0 - s))).astype(BF16)
            du16 = (da * sg).astype(BF16)
            dgu_ref[:, c0 : c0 + n] = dg16
            dgu_ref[:, F + c0 : F + c0 + n] = du16
            dh = dh + _dot_nt(dg16, wgu_ref[:, c0 : c0 + n]) + _dot_nt(du16, wgu_ref[:, F + c0 : F + c0 + n])
        xt = x_ref[...]
        r = _rms(xt)
        gn = g_ref[...]
        h_ref[...] = ((xt * r) * gn).astype(BF16)
        dx, dgn = _rms_bwd(xt, r, gn, dh)
        dg_ref[0:1, :] += dgn
        dxi = d + dx
        dx_ref[...] = dxi
        if w_o is not None:
            dxi16_ref, dao_ref = rest[6:8]
            dxi16 = dxi.astype(BF16)
            dxi16_ref[...] = dxi16
            dao_ref[...] = _dot_nt(dxi16, wo_ref[...]).astype(BF16)

    tile = pl.BlockSpec((tm, D), lambda i: (i, 0))
    args = [dxo, x, gain, gu, w_gu, w_d]
    wide = lambda n: pl.BlockSpec((tm, n), lambda i: (i, 0))
    in_specs = [tile, tile, _resident((1, D)), wide(2 * F), _resident((D, 2 * F)), _resident((F, D))]
    out_specs = [tile, wide(F), wide(2 * F), tile, tile, pl.BlockSpec((8, D), lambda i: (0, 0))]
    out_shape = [
        jax.ShapeDtypeStruct((T, D), F32),
        jax.ShapeDtypeStruct((T, F), BF16),
        jax.ShapeDtypeStruct((T, 2 * F), BF16),
        jax.ShapeDtypeStruct((T, D), BF16),
        jax.ShapeDtypeStruct((T, D), BF16),
        jax.ShapeDtypeStruct((8, D), F32),
    ]
    if w_o is not None:
        args.append(w_o)
        in_specs.append(_resident(w_o.shape))
        out_specs += [tile, pl.BlockSpec((tm, w_o.shape[0]), lambda i: (i, 0))]
        out_shape += [jax.ShapeDtypeStruct((T, D), BF16), jax.ShapeDtypeStruct((T, w_o.shape[0]), BF16)]
    return _call(body, after=after, args=args, name="ffn_bwd", grid=(T // tm,), in_specs=in_specs, out_specs=out_specs,
                 out_shape=out_shape)


def wgrad(a, b, *, name, a_cols=0, b_cols=0, group=1, flat=False, tk, out_dtype=BF16, after=None):
    T, K = a.shape
    J = 1
    if a_cols:
        K = a_cols
        J = a.shape[1] // K
        a_spec = pl.BlockSpec((tk, K), lambda j, k: (k, j))
    else:
        a_spec = pl.BlockSpec((tk, K), lambda j, k: (k, 0))
    if b_cols:
        N = b_cols * group
        J = b.shape[1] // N
        b_spec = pl.BlockSpec((tk, N), lambda j, k: (k, j))
    else:
        N = b.shape[1]
        b_spec = pl.BlockSpec((tk, N), lambda j, k: (k, 0))
    nk = T // tk
    if flat:
        o_spec, o_shape = pl.BlockSpec((K, N), lambda j, k: (0, j)), (K, J * N)
    elif group > 1:
        o_spec, o_shape = pl.BlockSpec((group, K, b_cols), lambda j, k: (j, 0, 0)), (J * group, K, b_cols)
    else:
        o_spec, o_shape = pl.BlockSpec((None, K, N), lambda j, k: (j, 0, 0)), (J, K, N)

    def body(a_ref, b_ref, o_ref, acc_ref):
        k = pl.program_id(1)

        @pl.when(k == 0)
        def _():
            acc_ref[...] = jnp.zeros_like(acc_ref)

        acc_ref[...] += _dot_tn(a_ref[...], b_ref[...])

        @pl.when(k == nk - 1)
        def _():
            if group > 1 and not flat:
                for i in range(group):
                    o_ref[i] = acc_ref[:, i * b_cols : (i + 1) * b_cols].astype(out_dtype)
            else:
                o_ref[...] = acc_ref[...].astype(out_dtype)

    outs, _ = _call(
        body,
        after=after,
        name=name,
        grid=(J, nk),
        in_specs=[a_spec, b_spec],
        out_specs=[o_spec],
        out_shape=[jax.ShapeDtypeStruct(o_shape, out_dtype)],
        args=(a, b),
        scratch=[pltpu.VMEM((K, N), F32)],
    )
    return outs[0]


def _seg(xs, lo):
    s_lo = [jnp.sum(jnp.where(lo, x, 0.0), axis=-1, keepdims=True) for x in xs]
    s_hi = [jnp.sum(jnp.where(lo, 0.0, x), axis=-1, keepdims=True) for x in xs]
    return [jnp.where(lo, a, b) for a, b in zip(s_lo, s_hi)]


def _head_norm(xs, gains, lo):
    rs = [lax.rsqrt(s * (1.0 / HEAD_DIM) + EPS) for s in _seg([x * x for x in xs], lo)]
    return [(x * r) * g for x, r, g in zip(xs, rs, gains)], rs


def _head_norm_bwd(xs, rs, gains, dys, lo):
    xns = [x * r for x, r in zip(xs, rs)]
    dxns = [dy * g for dy, g in zip(dys, gains)]
    means = [s * (1.0 / HEAD_DIM) for s in _seg([a * b for a, b in zip(dxns, xns)], lo)]
    dxs = [r * (dxn - xn * m) for r, dxn, xn, m in zip(rs, dxns, xns, means)]
    return dxs, [jnp.sum(dy * xn, axis=0, keepdims=True) for dy, xn in zip(dys, xns)]


def _swap_halves(x):
    return pltpu.roll(x, HEAD_DIM, 1)


def qkv_proj(x, gain, w, qg, kg, *, tm):
    T, D = x.shape
    N = w.shape[1]
    kvw = N_KV_HEADS * HEAD_DIM
    nqt, nkt = D // LANES, kvw // LANES

    def body(x_ref, g_ref, w_ref, qg_ref, kg_ref, qkv_ref, q_ref, kd_ref, vd_ref, r_ref):
        xt = x_ref[...]
        h = ((xt * _rms(xt)) * g_ref[...]).astype(BF16)
        qkv = _dot(h, w_ref[...])
        qkv_ref[...] = qkv
        lo = lax.broadcasted_iota(jnp.int32, (1, LANES), 1) < HEAD_DIM
        tiles = [qkv[:, t * LANES : (t + 1) * LANES] for t in range(nqt + nkt)]
        normed, rs = _head_norm(tiles, [qg_ref[...]] * nqt + [kg_ref[...]] * nkt, lo)
        for t in range(nqt + nkt):
            r_ref[:, t * LANES : (t + 1) * LANES] = rs[t]
        for t in range(nqt):
            q_ref[:, t * LANES : (t + 1) * LANES] = (normed[t] * SCALE).astype(BF16)
        for t in range(nkt):
            kn = normed[nqt + t]
            v = qkv[:, D + kvw + t * LANES : D + kvw + (t + 1) * LANES]
            for src, dst in ((kn, kd_ref), (v, vd_ref)):
                sw = _swap_halves(src)
                dst[:, 2 * t * LANES : (2 * t + 1) * LANES] = jnp.where(lo, src, sw).astype(BF16)
                dst[:, (2 * t + 1) * LANES : (2 * t + 2) * LANES] = jnp.where(lo, sw, src).astype(BF16)

    row = lambda i: (i, 0)
    return pl.pallas_call(
        body,
        name="qkv_proj",
        grid=(T // tm,),
        in_specs=[pl.BlockSpec((tm, D), row), _resident((1, D)), _resident((D, N)), _resident((1, LANES)), _resident((1, LANES))],
        out_specs=[pl.BlockSpec((tm, N), row), pl.BlockSpec((tm, D), row), pl.BlockSpec((tm, 2 * kvw), row), pl.BlockSpec((tm, 2 * kvw), row),
                   pl.BlockSpec((tm, D + kvw), row)],
        out_shape=[
            jax.ShapeDtypeStruct((T, N), F32),
            jax.ShapeDtypeStruct((T, D), BF16),
            jax.ShapeDtypeStruct((T, 2 * kvw), BF16),
            jax.ShapeDtypeStruct((T, 2 * kvw), BF16),
            jax.ShapeDtypeStruct((T, D + kvw), F32),
        ],
        compiler_params=_params(1),
    )(x, gain, w, qg, kg)


def _attn_tables(sinks, n_q_heads):
    P = n_q_heads // N_KV_HEADS // 2
    h = jnp.arange(1, n_q_heads + 1, dtype=F32)
    slopes = jnp.exp2(-8.0 * h / n_q_heads).reshape(N_KV_HEADS, P, 1, 2, 1)
    qi = jnp.arange(BLOCK)[:, None]
    kj = jnp.arange(BLOCK)[None, :]
    dist = jnp.where(kj <= qi, qi - kj, qi + BLOCK - kj).astype(F32)
    shape = (N_KV_HEADS, P, BLOCK, 2, BLOCK)
    bias = jnp.broadcast_to(-slopes * dist[None, None, :, None, :], shape)
    sink = jnp.broadcast_to(sinks.astype(F32).reshape(N_KV_HEADS, P, 1, 2, 1), shape)
    return bias.reshape(N_KV_HEADS, P * BLOCK, 2 * BLOCK), sink.reshape(N_KV_HEADS, P * BLOCK, 2 * BLOCK)


def _attn_specs(D, nb):
    kvw2 = 2 * N_KV_HEADS * HEAD_DIM
    cur = lambda b, i: (b * nb + i, 0)
    prev = lambda b, i: (jnp.maximum(b * nb + i - 1, 0), 0)
    return [
        pl.BlockSpec((BLOCK, D), cur),
        pl.BlockSpec((BLOCK, kvw2), cur),
        pl.BlockSpec((BLOCK, kvw2), prev),
        pl.BlockSpec((BLOCK, kvw2), cur),
        pl.BlockSpec((BLOCK, kvw2), prev),
    ]


def _attn_operands(kh, P, lo, q_ref, kc_ref, kp_ref, vc_ref, vp_ref):
    sl = slice(kh * LANES, (kh + 1) * LANES)

    def cat(prev_ref, cur_ref):
        d = jnp.concatenate([prev_ref[:, sl], cur_ref[:, sl]], axis=0)
        z = jnp.zeros_like(d)
        return jnp.concatenate([jnp.where(lo, d, z), jnp.where(lo, z, d)], axis=0)

    qt = jnp.concatenate([q_ref[:, (kh * P + pr) * LANES : (kh * P + pr + 1) * LANES] for pr in range(P)], axis=0)
    return qt, cat(kp_ref, kc_ref), cat(vp_ref, vc_ref)


def _attn_exp(s_all, bias, sink, tri, first):
    out = []
    for par in range(2):
        c0 = 2 * par * BLOCK
        s = jnp.where(tri, s_all[:, c0 + BLOCK : c0 + 2 * BLOCK], jnp.where(first, NEG, s_all[:, c0 : c0 + BLOCK]))
        s = s + bias[:, par * BLOCK : (par + 1) * BLOCK]
        snk = sink[:, par * BLOCK : (par + 1) * BLOCK]
        m = jnp.maximum(jnp.max(s, axis=-1, keepdims=True), snk)
        out.append((jnp.exp(s - m), jnp.exp(snk - m)))
    return out


def _unfold(x, tri):
    z = jnp.zeros_like(x)
    return jnp.concatenate([jnp.where(tri, z, x), jnp.where(tri, x, z)], axis=1)


def _attn_masks(R):
    lane = lax.broadcasted_iota(jnp.int32, (1, LANES), 1)
    row = lax.broadcasted_iota(jnp.int32, (R, BLOCK), 0) & (BLOCK - 1)
    col = lax.broadcasted_iota(jnp.int32, (R, BLOCK), 1)
    return lane, lane < HEAD_DIM, col <= row


def attn_fwd(q16, kd, vd, bias, sink, *, seq, n_seq):
    T, D = q16.shape
    nb = seq // BLOCK
    P = D // HEAD_DIM // N_KV_HEADS // 2
    R = P * BLOCK
    KV = range(N_KV_HEADS)

    def body(q_ref, kc_ref, kp_ref, vc_ref, vp_ref, bias_ref, sink_ref, o_ref):
        first = pl.program_id(1) == 0
        _, lo, tri = _attn_masks(R)
        ops = [_attn_operands(kh, P, lo, q_ref, kc_ref, kp_ref, vc_ref, vp_ref) for kh in KV]
        s_all = [_dot_nt(ops[kh][0], ops[kh][1]) for kh in KV]
        ex = [_attn_exp(s_all[kh], bias_ref[kh], sink_ref[kh], tri, first) for kh in KV]
        den = [[jnp.sum(e, axis=-1, keepdims=True) + es for e, es in ex[kh]] for kh in KV]
        lhs = [jnp.concatenate([_unfold(e, tri) for e, _ in ex[kh]], axis=1).astype(BF16) for kh in KV]
        o = [_dot(lhs[kh], ops[kh][2]) for kh in KV]
        for kh in KV:
            out = o[kh] / jnp.where(lo, den[kh][0], den[kh][1])
            for pr in range(P):
                t = kh * P + pr
                o_ref[:, t * LANES : (t + 1) * LANES] = out[pr * BLOCK : (pr + 1) * BLOCK, :].astype(BF16)

    return pl.pallas_call(
        body,
        name="attn_fwd",
        grid=(n_seq, nb),
        in_specs=_attn_specs(D, nb) + [_resident((N_KV_HEADS, R, 2 * BLOCK)), _resident((N_KV_HEADS, R, 2 * BLOCK))],
        out_specs=pl.BlockSpec((BLOCK, D), lambda b, i: (b * nb + i, 0)),
        out_shape=jax.ShapeDtypeStruct((T, D), BF16),
        compiler_params=_params(2),
    )(q16, kd, kd, vd, vd, bias, sink)


def attn_bwd(q16, kd, vd, do, bias, sink, *, seq, n_seq):
    T, D = q16.shape
    kvw2 = 2 * N_KV_HEADS * HEAD_DIM
    nb = seq // BLOCK
    G = D // HEAD_DIM // N_KV_HEADS
    P = G // 2
    R = P * BLOCK
    KV = range(N_KV_HEADS)

    def body(q_ref, kc_ref, kp_ref, vc_ref, vp_ref, do_ref, bias_ref, sink_ref,
             dq_ref, dkc_ref, dkp_ref, dvc_ref, dvp_ref, dsink_ref):
        first = pl.program_id(1) == 0

        @pl.when(jnp.logical_and(pl.program_id(0) == 0, first))
        def _():
            dsink_ref[...] = jnp.zeros_like(dsink_ref)

        lane, lo, tri = _attn_masks(R)
        ops = [_attn_operands(kh, P, lo, q_ref, kc_ref, kp_ref, vc_ref, vp_ref) for kh in KV]
        do16 = [jnp.concatenate([do_ref[:, (kh * P + pr) * LANES : (kh * P + pr + 1) * LANES] for pr in range(P)], axis=0)
                for kh in KV]
        s_all = [_dot_nt(ops[kh][0], ops[kh][1]) for kh in KV]
        dp_all = [_dot_nt(do16[kh], ops[kh][2]) for kh in KV]
        ex = [_attn_exp(s_all[kh], bias_ref[kh], sink_ref[kh], tri, first) for kh in KV]
        den = [[jnp.sum(e, axis=-1, keepdims=True) for e, _ in ex[kh]] for kh in KV]
        dsink = jnp.zeros((1, LANES), F32)
        pf, dsf = [], []
        for kh in KV:
            ps_, ds_ = [], []
            for par in range(2):
                e, es = ex[kh][par]
                inv = 1.0 / (den[kh][par] + es)
                p = e * inv
                c0 = 2 * par * BLOCK
                dp = jnp.where(tri, dp_all[kh][:, c0 + BLOCK : c0 + 2 * BLOCK], dp_all[kh][:, c0 : c0 + BLOCK])
                delta = jnp.sum(p * dp, axis=-1, keepdims=True)
                ds_.append(_unfold(p * (dp - delta), tri))
                ps_.append(_unfold(p, tri))
                dsr = -((es * inv) * delta)
                for pr in range(P):
                    hq = kh * G + 2 * pr + par
                    tot = jnp.sum(dsr[pr * BLOCK : (pr + 1) * BLOCK, :], axis=0, keepdims=True)
                    dsink = dsink + jnp.where(lane == hq, tot, 0.0)
            pf.append(jnp.concatenate(ps_, axis=1).astype(BF16))
            dsf.append(jnp.concatenate(ds_, axis=1).astype(BF16))
        dq = [_dot(dsf[kh], ops[kh][1]) for kh in KV]
        dk = [_dot_tn(dsf[kh], ops[kh][0]) for kh in KV]
        dv = [_dot_tn(pf[kh], do16[kh]) for kh in KV]
        dsink_ref[0:1, :] += dsink
        for kh in KV:
            sl = slice(kh * LANES, (kh + 1) * LANES)
            for pr in range(P):
                t = kh * P + pr
                dq_ref[:, t * LANES : (t + 1) * LANES] = dq[kh][pr * BLOCK : (pr + 1) * BLOCK, :]
            for full, prev_ref, cur_ref in ((dk[kh], dkp_ref, dkc_ref), (dv[kh], dvp_ref, dvc_ref)):
                dup = jnp.where(lo, full[: 2 * BLOCK, :], full[2 * BLOCK :, :])
                prev_ref[:, sl] = dup[:BLOCK, :]
                cur_ref[:, sl] = dup[BLOCK:, :]

    cur = lambda b, i: (b * nb + i, 0)
    kv_spec = pl.BlockSpec((BLOCK, kvw2), cur)
    kv_shape = jax.ShapeDtypeStruct((T, kvw2), F32)
    return pl.pallas_call(
        body,
        name="attn_bwd",
        grid=(n_seq, nb),
        in_specs=_attn_specs(D, nb)
        + [pl.BlockSpec((BLOCK, D), cur), _resident((N_KV_HEADS, R, 2 * BLOCK)), _resident((N_KV_HEADS, R, 2 * BLOCK))],
        out_specs=[pl.BlockSpec((BLOCK, D), cur), kv_spec, kv_spec, kv_spec, kv_spec, pl.BlockSpec((8, LANES), lambda b, i: (0, 0))],
        out_shape=[jax.ShapeDtypeStruct((T, D), F32), kv_shape, kv_shape, kv_shape, kv_shape, jax.ShapeDtypeStruct((8, LANES), F32)],
        compiler_params=_params(2),
    )(q16, kd, kd, vd, vd, do, bias, sink)


def qkv_bwd(dq, dkc, dkp, dvc, dvp, qkv, rr, dres, x, gain, w_qkv, qg, kg, *, seq):
    T, D = x.shape
    kvw2 = dkc.shape[1]
    kvw = kvw2 // 2
    nqt, nkt = D // LANES, kvw // LANES
    nb = seq // BLOCK
    tm = 2 * BLOCK
    n = T // tm

    def body(dq_ref, dkc_ref, dkpa_ref, dkpb_ref, dvc_ref, dvpa_ref, dvpb_ref, qkv_ref, rr_ref, dres_ref, x_ref, g_ref,
             w_ref, qg_ref, kg_ref, dx_ref, dqkv_ref, h_ref, dg_ref, hg_ref):
        i = pl.program_id(0)

        @pl.when(i == 0)
        def _():
            dg_ref[...] = jnp.zeros_like(dg_ref)
            hg_ref[...] = jnp.zeros_like(hg_ref)

        lo = lax.broadcasted_iota(jnp.int32, (1, LANES), 1) < HEAD_DIM
        last = (2 * i + 1) % nb == nb - 1
        dkd = dkc_ref[...] + jnp.concatenate([dkpa_ref[...], jnp.where(last, 0.0, dkpb_ref[...])], axis=0)
        dvd = dvc_ref[...] + jnp.concatenate([dvpa_ref[...], jnp.where(last, 0.0, dvpb_ref[...])], axis=0)

        def undup(d, t):
            a, b = d[:, 2 * t * LANES : (2 * t + 1) * LANES], d[:, (2 * t + 1) * LANES : (2 * t + 2) * LANES]
            return jnp.where(lo, a + _swap_halves(a), b + _swap_halves(b))

        tiles = [qkv_ref[:, t * LANES : (t + 1) * LANES] for t in range(nqt + nkt)]
        gains = [qg_ref[...]] * nqt + [kg_ref[...]] * nkt
        dys = [dq_ref[:, t * LANES : (t + 1) * LANES] * SCALE for t in range(nqt)] + [undup(dkd, t) for t in range(nkt)]
        rs = [rr_ref[:, t * LANES : (t + 1) * LANES] for t in range(nqt + nkt)]
        dxs, dgs = _head_norm_bwd(tiles, rs, gains, dys, lo)
        for t in range(nqt + nkt):
            dqkv_ref[:, t * LANES : (t + 1) * LANES] = dxs[t].astype(BF16)
        for t in range(nkt):
            dqkv_ref[:, D + kvw + t * LANES : D + kvw + (t + 1) * LANES] = undup(dvd, t).astype(BF16)
        hg_ref[0:1, :] += functools.reduce(lambda a, b: a + b, dgs[:nqt])
        hg_ref[1:2, :] += functools.reduce(lambda a, b: a + b, dgs[nqt:])
        dh = _dot_nt(dqkv_ref[...], w_ref[...])
        xt = x_ref[...]
        r = _rms(xt)
        gn = g_ref[...]
        h_ref[...] = ((xt * r) * gn).astype(BF16)
        dx, dgn = _rms_bwd(xt, r, gn, dh)
        dg_ref[0:1, :] += dgn
        dx_ref[...] = dres_ref[...] + dx

    row = lambda i: (i, 0)
    nxt_a = pl.BlockSpec((BLOCK, kvw2), lambda i: (2 * i + 1, 0))
    nxt_b = pl.BlockSpec((BLOCK, kvw2), lambda i: (jnp.minimum(2 * i + 2, 2 * n - 1), 0))
    return pl.pallas_call(
        body,
        name="qkv_bwd",
        grid=(n,),
        in_specs=[
            pl.BlockSpec((tm, D), row),
            pl.BlockSpec((tm, kvw2), row),
            nxt_a,
            nxt_b,
            pl.BlockSpec((tm, kvw2), row),
            nxt_a,
            nxt_b,
            pl.BlockSpec((tm, D + kvw2), row),
            pl.BlockSpec((tm, D + kvw), row),
            pl.BlockSpec((tm, D), row),
            pl.BlockSpec((tm, D), row),
            _resident((1, D)),
            _resident((D, D + kvw2)),
            _resident((1, LANES)),
            _resident((1, LANES)),
        ],
        out_specs=[
            pl.BlockSpec((tm, D), row),
            pl.BlockSpec((tm, D + kvw2), row),
            pl.BlockSpec((tm, D), row),
            pl.BlockSpec((8, D), lambda i: (0, 0)),
            pl.BlockSpec((8, LANES), lambda i: (0, 0)),
        ],
        out_shape=[
            jax.ShapeDtypeStruct((T, D), F32),
            jax.ShapeDtypeStruct((T, D + kvw2), BF16),
            jax.ShapeDtypeStruct((T, D), BF16),
            jax.ShapeDtypeStruct((8, D), F32),
            jax.ShapeDtypeStruct((8, LANES), F32),
        ],
        compiler_params=_params(1),
    )(dq, dkc, dkp, dkp, dvc, dvp, dvp, qkv, rr, dres, x, gain, w_qkv, qg, kg)


def local_step(x, target, gains, w, *, seq, tm=256, tm_ffn=256, tm_conv=512, tk=2048, shards=None, ex=None):
    T, D = x.shape
    n_seq = T // seq
    nm, nf, qgain, kgain, sinks = gains
    H = D // HEAD_DIM
    tk, tk_long = min(tk, T), min(2 * tk, T)
    qg2, kg2 = jnp.tile(qgain, (1, 2)), jnp.tile(kgain, (1, 2))
    bias, sinkcol = _attn_tables(sinks, H)

    dist = shards is not None
    w = dict(w)

    plan = _Gather([shards["w_gu"][0], shards["w_d"][0]]) if dist else None
    (x1, bcx, y_conv, z16), got = conv_fwd(x, nm[0:1], w["w_in"], w["cw"], w["w_out"], seq=seq, tm=tm_conv, plan=plan)
    if dist:
        w["w_gu"], w["w_d"] = [cols_from_shards(got[0]), None], [got[1].reshape(-1, D), None]
    plan = _Gather([shards["w_qkv"], shards["w_o"], shards["w_gu"][1], shards["w_d"][1]]) if dist else None
    (x2, gu0), got = ffn_fwd(x1, nf[0:1], w["w_gu"][0], w["w_d"][0], tm=2 * tm_ffn, plan=plan)
    if dist:
        w["w_qkv"], w["w_o"] = cols_from_shards(got[0]), got[1].reshape(D, D)
        w["w_gu"][1], w["w_d"][1] = cols_from_shards(got[2]), got[3].reshape(-1, D)
    qkv, q16, kd, vd, rr = qkv_proj(x2, nm[1:2], w["w_qkv"], qg2, kg2, tm=tm_conv)
    ao = attn_fwd(q16, kd, vd, bias, sinkcol, seq=seq, n_seq=n_seq)
    (x3, dx4, gu1, sse), _ = ffn_fwd(x2, nf[1:2], w["w_gu"][1], w["w_d"][1], tm=tm_ffn, attn=(ao, w["w_o"]), target=target)

    by_dest = lambda a: a.reshape(N_DEV, -1, a.shape[-1])
    gu_cols = 2 * MXU_TILE

    def send(name, *entries):
        if ex is None:
            return None
        items = [(a, False, key, (N_DEV,) + (() if layers is None else (layers,)) + a.shape[1:], layer)
                 for a, key, layer, layers in entries]
        return ex.start(items, name=name)

    (dx3, a16, dgu, h16, d16, dnf1, dx3_16, dao), _ = ffn_bwd(
        dx4, x3, nf[1:2], gu1, w["w_gu"][1], w["w_d"][1], tm=tm, w_o=w["w_o"])
    g_gu1 = shards_from_cols(wgrad(h16, dgu, name="wgrad_gu1", b_cols=gu_cols, flat=True, tk=tk_long))
    g_d1 = by_dest(wgrad(a16, d16, name="wgrad_d1", a_cols=a16.shape[1] // 2, tk=tk))
    tok = send("exchange_ffn1", (g_gu1, "w_gu", 1, 2), (g_d1, "w_d", 1, 2))
    g_o = by_dest(wgrad(ao, dx3_16, name="wgrad_o", tk=tk_long, after=tok))
    dq, dkc, dkp, dvc, dvp, dsinks = attn_bwd(q16, kd, vd, dao, bias, sinkcol, seq=seq, n_seq=n_seq)
    dx2, dqkv16, h16, dnm1, dgains = qkv_bwd(dq, dkc, dkp, dvc, dvp, qkv, rr, dx3, x2, nm[1:2], w["w_qkv"], qg2, kg2, seq=seq)
    g_qkv = shards_from_cols(wgrad(h16, dqkv16, name="wgrad_qkv", tk=tk_long)[0])
    tok = send("exchange_attn", (g_o, "w_o", None, None), (g_qkv, "w_qkv", None, None))
    (dx1, a16, dgu, h16, d16, dnf0), _ = ffn_bwd(dx2, x1, nf[0:1], gu0, w["w_gu"][0], w["w_d"][0], tm=tm, after=tok)
    g_gu0 = shards_from_cols(wgrad(h16, dgu, name="wgrad_gu0", b_cols=gu_cols, flat=True, tk=tk_long))
    tok = send("exchange_gu0", (g_gu0, "w_gu", 0, 2))
    g_d0 = by_dest(wgrad(a16, d16, name="wgrad_d0", a_cols=a16.shape[1] // 2, tk=tk, after=tok))
    tok = send("exchange_d0", (g_d0, "w_d", 0, 2))
    (gx, dbcx, h16, d16, dcw, dnm0), _ = conv_bwd(
        dx1, x, nm[0:1], bcx, y_conv, w["cw"], w["w_in"], w["w_out"], seq=seq, tm=tm_conv, after=tok)
    g_out = by_dest(wgrad(z16, d16, name="wgrad_out", tk=tk_long))
    g_cw = dcw[0:3].reshape(3, N_DEV, D // N_DEV).transpose(1, 0, 2)
    tok = send("exchange_out", (g_out, "w_out", None, None), (g_cw, "cw", None, None))
    g_in = wgrad(h16, dbcx, name="wgrad_in", b_cols=3 * D // N_DEV, group=2, tk=tk_long, after=tok)
    g = dict(w_in=g_in, cw=g_cw, w_out=g_out, w_o=g_o, w_qkv=g_qkv, w_gu=[g_gu0, g_gu1], w_d=[g_d0, g_d1])
    small = dict(nm0=dnm0, nm1=dnm1, nf0=dnf0, nf1=dnf1, gains=dgains, sinks=dsinks)
    return sse, gx, g, small


def _adamw_math(g, w, m, v):
    m = ADAM_B1 * m + (1.0 - ADAM_B1) * g
    v = ADAM_B2 * v + (1.0 - ADAM_B2) * (g * g)
    m_hat = m / (1.0 - ADAM_B1 ** ADAM_STEP)
    v_hat = v / (1.0 - ADAM_B2 ** ADAM_STEP)
    delta = -ADAM_LR * (m_hat / (jnp.sqrt(v_hat) + ADAM_EPS) + ADAM_WD * w)
    return delta, m, v


def adamw(parts, owns, w, m, v, *, name, after=None):
    n, LR, C = parts.shape
    L = len(owns)
    R = LR // L
    tr = R
    for cand in (256, 128, 88, 64, 32, 16, 8):
        if R > cand and R % cand == 0:
            tr = cand
            break
    per_layer = R // tr
    extra = [] if after is None else [after]

    def body(me_ref, p_ref, *rest):
        own_refs, (w_ref, m_ref, v_ref) = rest[:L], rest[L : L + 3]
        g_ref, d_ref, mo_ref, vo_ref = rest[L + 3 + len(extra) :]
        layer = pl.program_id(0) // per_layer
        mine = own_refs[0][...].astype(F32)
        for j in range(1, L):
            mine = jnp.where(layer == j, own_refs[j][...].astype(F32), mine)
        g = None
        for s in range(n):
            share = jnp.where(me_ref[0] == s, mine, p_ref[s].astype(F32))
            g = share if g is None else g + share
        g_ref[...] = g
        d_ref[...], mo_ref[...], vo_ref[...] = _adamw_math(g, w_ref[...], m_ref[...], v_ref[...])

    blk = pl.BlockSpec((tr, C), lambda i, me: (i, 0))
    own_specs = [pl.BlockSpec((None, tr, C), lambda i, me: (me[0], i % per_layer, 0)) if o.ndim == 3
                 else pl.BlockSpec((tr, C), lambda i, me: (i % per_layer, 0)) for o in owns]
    me = (4 * lax.axis_index("x") + 2 * lax.axis_index("y") + lax.axis_index("c")).astype(jnp.int32).reshape(1)
    return pl.pallas_call(
        body,
        name=name,
        grid_spec=pltpu.PrefetchScalarGridSpec(
            num_scalar_prefetch=1,
            grid=(LR // tr,),
            in_specs=[pl.BlockSpec((n, tr, C), lambda i, me: (0, i, 0))] + own_specs + [blk, blk, blk] + _any_specs(len(extra)),
            out_specs=[blk] * 4,
        ),
        out_shape=[jax.ShapeDtypeStruct((LR, C), F32)] * 4,
        compiler_params=_params(1),
    )(me, parts, *owns, w, m, v, *extra)


def pack_small(small, sse, D):
    W = max(D, 2 * LANES)

    def body(nm0, nm1, nf0, nf1, gains, sinks, sse_ref, o_ref):
        o_ref[...] = jnp.zeros_like(o_ref)
        o_ref[0:1, :D] = nm0[0:1, :]
        o_ref[1:2, :D] = nm1[0:1, :]
        o_ref[2:3, :D] = nf0[0:1, :]
        o_ref[3:4, :D] = nf1[0:1, :]
        gq = gains[0:1, :] + pltpu.roll(gains[0:1, :], HEAD_DIM, 1)
        gk = gains[1:2, :] + pltpu.roll(gains[1:2, :], HEAD_DIM, 1)
        lane = lax.broadcasted_iota(jnp.int32, (1, LANES), 1)
        o_ref[4:5, :LANES] = jnp.where(lane < HEAD_DIM, gq, gk)
        o_ref[4:5, LANES : 2 * LANES] = sinks[0:1, :]
        o_ref[5:6, :LANES] = sse_ref[0:1, :] * (0.5 / D)

    return pl.pallas_call(
        body,
        name="pack_small",
        out_shape=jax.ShapeDtypeStruct((8, W), F32),
    )(small["nm0"], small["nm1"], small["nf0"], small["nf1"], small["gains"], small["sinks"], sse)


def _pack_small_params(nm, nf, qg, kg, sk, D):
    W = max(D, 2 * LANES)
    row4 = jnp.concatenate([qg.reshape(-1), kg.reshape(-1), jnp.zeros((LANES - 2 * HEAD_DIM,), F32), sk.reshape(-1)])
    row4 = jnp.pad(row4, (0, W - row4.shape[0]))
    rows = [jnp.pad(r, (0, W - D)) for r in (nm[0], nm[1], nf[0], nf[1])] + [row4]
    return jnp.concatenate([jnp.stack(rows), jnp.zeros((3, W), F32)], axis=0)


def _unpack_small(a, D, H):
    nm = a[0:2, :D]
    nf = a[2:4, :D]
    qg = a[4:5, 0:HEAD_DIM]
    kg = a[4:5, HEAD_DIM : 2 * HEAD_DIM]
    sk = a[4:5, LANES : LANES + H]
    return qg, kg, sk, nm, nf


def kernel(x, conv_w_in, conv_w, conv_w_out, attn_w_qkv, attn_q_gain, attn_k_gain, attn_sinks, attn_w_o, norm_mixer, norm_ffn, ffn_w_gate_up, ffn_w_down, loss_target, m_conv_w_in, m_conv_w, m_conv_w_out, m_attn_w_qkv, m_attn_q_gain, m_attn_k_gain, m_attn_sinks, m_attn_w_o, m_norm_mixer, m_norm_ffn, m_ffn_w_gate_up, m_ffn_w_down, v_conv_w_in, v_conv_w, v_conv_w_out, v_attn_w_qkv, v_attn_q_gain, v_attn_k_gain, v_attn_sinks, v_attn_w_o, v_norm_mixer, v_norm_ffn, v_ffn_w_gate_up, v_ffn_w_down):
    n_seq, seq, D = x.shape
    T = n_seq * seq
    H = D // HEAD_DIM
    L = ffn_w_gate_up.shape[0]

    full = run_plan(_Gather([conv_w_in[0].astype(BF16), conv_w[0], conv_w_out[0].astype(BF16)]), name="gather_conv_weights")
    w = dict(w_in=cols_from_shards(full[0]), cw=full[1].transpose(1, 0, 2).reshape(3, D),
             w_out=full[2].reshape(D, D))
    shards = dict(w_gu=[ffn_w_gate_up[l].astype(BF16) for l in range(L)], w_d=[ffn_w_down[l].astype(BF16) for l in range(L)],
                  w_qkv=attn_w_qkv[0].astype(BF16), w_o=attn_w_o[0].astype(BF16))
    gains = (norm_mixer, norm_ffn, attn_q_gain, attn_k_gain, attn_sinks)
    ex = Exchange()
    sse, gx, g, small = local_step(x.reshape(T, D), loss_target.reshape(T, D), gains, w, seq=seq, shards=shards, ex=ex)
    zones, own = ex.wait([g["w_in"]], name="exchange_wait")

    packed = pack_small(small, sse, D)
    token = ex.start([(g["w_in"], False, "w_in", g["w_in"].shape, None),
                      (packed, True, "small", (N_DEV,) + packed.shape, None)], name="exchange_last")

    def flat(a):
        return a.reshape(-1, a.shape[-1])

    big = [conv_w_in, conv_w, conv_w_out, attn_w_qkv, attn_w_o, ffn_w_gate_up, ffn_w_down]
    big_m = [m_conv_w_in, m_conv_w, m_conv_w_out, m_attn_w_qkv, m_attn_w_o, m_ffn_w_gate_up, m_ffn_w_down]
    big_v = [v_conv_w_in, v_conv_w, v_conv_w_out, v_attn_w_qkv, v_attn_w_o, v_ffn_w_gate_up, v_ffn_w_down]
    keys = ["w_in", "cw", "w_out", "w_qkv", "w_o", "w_gu", "w_d"]

    def update(b, zones, own, after=None):
        zone = zones[keys[b]]
        parts = zone.reshape(N_DEV, -1, zone.shape[-1])
        layers = [None] if zone.ndim == 3 else range(zone.shape[1])
        outs = adamw(parts, [own[(keys[b], l)] for l in layers], flat(big[b]), flat(big_m[b]), flat(big_v[b]),
                     name="adamw_" + keys[b], after=after)
        return [o.reshape(big[b].shape) for o in outs]

    res = [None] + [update(b, zones, own, after=token) for b in range(1, 7)]
    zones, own = ex.wait([r[0] for r in res[1:]], name="exchange_last_wait")
    res[0] = update(0, zones, own)
    sw = _pack_small_params(norm_mixer, norm_ffn, attn_q_gain, attn_k_gain, attn_sinks, D)
    sm = _pack_small_params(m_norm_mixer, m_norm_ffn, m_attn_q_gain, m_attn_k_gain, m_attn_sinks, D)
    sv = _pack_small_params(v_norm_mixer, v_norm_ffn, v_attn_q_gain, v_attn_k_gain, v_attn_sinks, D)
    souts = adamw(zones["small"], [own[("small", None)]], sw, sm, sv, name="adamw_small")
    sres = [_unpack_small(o, D, H) for o in souts]
    loss = souts[0][5, 0]

    def ordered(i):
        r, s = [r[i] for r in res], sres[i]
        return [r[0], r[1], r[2], r[3], s[0], s[1], s[2], r[4], s[3], s[4], r[5], r[6]]

    return (loss, gx.reshape(n_seq, seq, D), *ordered(0), *ordered(1), *ordered(2), *ordered(3))
```

```python
import functools
import math

import jax
import jax.numpy as jnp
from jax import lax
from jax.experimental import pallas as pl
from jax.experimental.pallas import tpu as pltpu

F32 = jnp.float32
BF16 = jnp.bfloat16

EPS = 1e-6
HEAD_DIM = 64
N_KV_HEADS = 4
BLOCK = 128
LANES = 128
N_DEV = 8
NEG = -1e30
SCALE = 1.0 / math.sqrt(HEAD_DIM)

ADAM_LR = 0.001
ADAM_B1 = 0.9
ADAM_B2 = 0.999
ADAM_EPS = 1e-08
ADAM_WD = 0.01
ADAM_STEP = 10

V7X_VMEM_BYTES = 64 * 1024 * 1024
VMEM_LIMIT = V7X_VMEM_BYTES - 2 * 1024 * 1024
MESH = pl.DeviceIdType.MESH

_NT = (((1,), (1,)), ((), ()))
_TN = (((0,), (0,)), ((), ()))


def _params(n_grid):
    return pltpu.CompilerParams(dimension_semantics=("arbitrary",) * n_grid, vmem_limit_bytes=VMEM_LIMIT)


def _resident(shape):
    nd = len(shape)
    return pl.BlockSpec(shape, lambda *_: (0,) * nd, pipeline_mode=pl.Buffered(1))


def _rms(x):
    return lax.rsqrt(jnp.mean(x * x, axis=-1, keepdims=True) + EPS)


def _rms_bwd(x, r, gain, dh):
    xn = x * r
    dxn = dh * gain
    dx = r * (dxn - xn * jnp.mean(dxn * xn, axis=-1, keepdims=True))
    return dx, jnp.sum(dh * xn, axis=0, keepdims=True)


def _dot(a, b):
    return jnp.dot(a, b, preferred_element_type=F32)


def _dot_nt(a, b):
    return lax.dot_general(a, b, _NT, preferred_element_type=F32)


def _dot_tn(a, b):
    return lax.dot_general(a, b, _TN, preferred_element_type=F32)


def _place():
    return lax.axis_index("x"), lax.axis_index("y"), lax.axis_index("c")


def _flip(v, bit):
    return 1 - v if bit else v


def _slot(px, py, pc):
    return 4 * px + 2 * py + pc


class _Gather:
    def __init__(self, shards):
        nt = len(shards)
        self.nt = nt
        self.inputs = list(shards)
        self.out_shapes = [jax.ShapeDtypeStruct((N_DEV,) + s.shape, s.dtype) for s in shards]
        self.scratch = [pltpu.SemaphoreType.DMA((nt, 7)), pltpu.SemaphoreType.DMA((nt, 7)), pltpu.SemaphoreType.DMA((nt,))]
        self.aliases = {}

    def phases(self, total):
        assert total >= 3
        return [(0, self.start), (total - 2, self.forward), (total - 1, self.finish)]

    def _copies(self, ins, outs, sems):
        send_sems, recv_sems, loc_sems = sems
        x, y, c = _place()
        me = _slot(x, y, c)
        sib = (x, y, 1 - c)
        chips = [(_flip(x, k >> 1), _flip(y, k & 1)) for k in (1, 2, 3)]

        def copy(t, k, src, dst_slot, to):
            return pltpu.make_async_remote_copy(
                src_ref=src, dst_ref=outs[t].at[dst_slot], send_sem=send_sems.at[t, k], recv_sem=recv_sems.at[t, k],
                device_id=to, device_id_type=MESH)

        local = [pltpu.make_async_copy(ins[t], outs[t].at[me], loc_sems.at[t]) for t in range(self.nt)]
        first, passed, arrive_ici, arrive_sib = [], [], [], []
        for t in range(self.nt):
            first.append(copy(t, 0, ins[t], me, sib))
            s = _slot(x, y, 1 - c)
            arrive_sib.append(copy(t, 0, outs[t].at[s], s, sib))
            for j, (px, py) in enumerate(chips):
                first.append(copy(t, 1 + j, ins[t], me, (px, py, c)))
                s = _slot(px, py, c)
                arrive_ici.append(copy(t, 1 + j, outs[t].at[s], s, sib))
                passed.append(copy(t, 4 + j, outs[t].at[s], s, sib))
                s = _slot(px, py, 1 - c)
                arrive_sib.append(copy(t, 4 + j, outs[t].at[s], s, sib))
        return local, first, passed, arrive_ici, arrive_sib

    def start(self, ins, outs, sems):
        local, first, _, _, _ = self._copies(ins, outs, sems)
        for cp in local + first:
            cp.start()

    def forward(self, ins, outs, sems):
        _, _, passed, arrive_ici, _ = self._copies(ins, outs, sems)
        for arrival, fwd in zip(arrive_ici, passed):
            arrival.wait_recv()
            fwd.start()

    def finish(self, ins, outs, sems):
        local, first, passed, _, arrive_sib = self._copies(ins, outs, sems)
        for cp in arrive_sib:
            cp.wait_recv()
        for cp in first + passed:
            cp.wait_send()
        for cp in local:
            cp.wait()


def _any_specs(n):
    return [pl.BlockSpec(memory_space=pl.ANY)] * n


def run_plan(plan, *, name):
    def body(*refs):
        n_in, n_out = len(plan.inputs), len(plan.out_shapes)
        ins, outs, sems = refs[:n_in], refs[n_in : n_in + n_out], refs[n_in + n_out :]
        for _, phase in plan.phases(3):
            phase(ins, outs, sems)

    return pl.pallas_call(
        body,
        name=name,
        in_specs=_any_specs(len(plan.inputs)),
        out_specs=_any_specs(len(plan.out_shapes)),
        out_shape=plan.out_shapes,
        scratch_shapes=plan.scratch,
        input_output_aliases=plan.aliases,
    )(*plan.inputs)


_HBM = pl.BlockSpec(memory_space=pltpu.HBM)
_SEM = pl.BlockSpec(memory_space=pltpu.SEMAPHORE)
_DATAFLOW = pltpu.SideEffectType.DATAFLOW_SIDE_EFFECTING


class Exchange:
    def __init__(self):
        self.zones = {}
        self.pending = []
        self.sources = []

    def start(self, items, *, name):
        nt = len(items)
        keys = list(dict.fromkeys(it[2] for it in items))
        for a, _, key, shape, _ in items:
            if key not in self.zones:
                self.zones[key] = lax.empty(shape, a.dtype)
        nz = len(keys)

        def body(*refs):
            ins, zones, sems, token = refs[:nt], refs[nt : nt + nz], refs[nt + nz : nt + nz + 2 * nt], refs[-1]
            x, y, c = _place()
            me = _slot(x, y, c)
            for k in range(1, N_DEV):
                px, py, pc = _flip(x, (k >> 2) & 1), _flip(y, (k >> 1) & 1), _flip(c, k & 1)
                for t, (_, whole, key, _, layer) in enumerate(items):
                    zone = zones[keys.index(key)]
                    pltpu.make_async_remote_copy(
                        src_ref=ins[t] if whole else ins[t].at[_slot(px, py, pc)],
                        dst_ref=zone.at[me] if layer is None else zone.at[me, layer],
                        send_sem=sems[2 * t], recv_sem=sems[2 * t + 1], device_id=(px, py, pc), device_id_type=MESH).start()
            token[...] = jnp.zeros_like(token)

        bufs = [pltpu.with_memory_space_constraint(b, pltpu.HBM) for b in [it[0] for it in items] + [self.zones[k] for k in keys]]
        outs = pl.pallas_call(
            body,
            name=name,
            in_specs=[_HBM] * (nt + nz),
            out_specs=[_SEM] * (2 * nt) + [_HBM] * (nt + nz) + [pl.BlockSpec(memory_space=pltpu.VMEM)],
            out_shape=[pltpu.SemaphoreType.DMA(())] * (2 * nt) + [pltpu.HBM(b.shape, b.dtype) for b in bufs]
            + [jax.ShapeDtypeStruct((8, LANES), F32)],
            input_output_aliases={i: 2 * nt + i for i in range(nt + nz)},
            compiler_params=pltpu.CompilerParams(has_side_effects=_DATAFLOW),
        )(*bufs)
        for t, (_, _, key, _, layer) in enumerate(items):
            self.pending.append((outs[2 * t], outs[2 * t + 1], key, layer))
        self.sources += [((it[2], it[4]), a) for it, a in zip(items, outs[2 * nt : 3 * nt])]
        for i, key in enumerate(keys):
            self.zones[key] = outs[3 * nt + i]
        return outs[-1]

    def wait(self, after, *, name):
        pending, keys = self.pending, list(self.zones)
        names, sources = [n for n, _ in self.sources], [a for _, a in self.sources]
        ns, nz, npend = len(sources), len(keys), len(pending)
        self.pending, self.sources = [], []

        def body(*refs):
            zones, sems = refs[ns : ns + nz], refs[ns + nz : ns + nz + 2 * npend]
            x, y, c = _place()
            for i, (_, _, key, layer) in enumerate(pending):
                zone = zones[keys.index(key)]
                rows = pl.ds(0, N_DEV - 1)
                seven = zone.at[rows] if layer is None else zone.at[rows, layer]
                pltpu.make_async_remote_copy(
                    src_ref=seven, dst_ref=seven, send_sem=sems[2 * i], recv_sem=sems[2 * i + 1],
                    device_id=(x, y, c), device_id_type=MESH).wait()

        bufs = list(sources) + [self.zones[k] for k in keys]
        flat_sems = [s for p in pending for s in p[:2]]
        outs = pl.pallas_call(
            body,
            name=name,
            in_specs=[_HBM] * (ns + nz) + [_SEM] * (2 * npend) + _any_specs(len(after)),
            out_specs=[_HBM] * (ns + nz),
            out_shape=[pltpu.HBM(b.shape, b.dtype) for b in bufs],
            input_output_aliases={i: i for i in range(ns + nz)},
            compiler_params=pltpu.CompilerParams(has_side_effects=_DATAFLOW),
        )(*bufs, *flat_sems, *after)
        self.zones = {}
        return dict(zip(keys, outs[ns:])), dict(zip(names, outs[:ns]))


def _call(body, *, name, grid, in_specs, out_specs, out_shape, args, scratch=(), plan=None, after=None):
    if after is not None:
        inner, n_real = body, len(in_specs)
        body = lambda *refs: inner(*refs[:n_real], *refs[n_real + 1 :])
        in_specs, args = list(in_specs) + _any_specs(1), list(args) + [after]
    n_in, n_out, n_scr = len(in_specs), len(out_specs), len(scratch)
    if plan is None:
        outs = pl.pallas_call(
            body, name=name, grid=grid, in_specs=in_specs, out_specs=out_specs, out_shape=out_shape,
            scratch_shapes=list(scratch), compiler_params=_params(len(grid)))(*args)
        return outs, None
    c_in, c_out = len(plan.inputs), len(plan.out_shapes)
    phases = plan.phases(math.prod(grid))

    def full(*refs):
        a, refs = refs[:n_in], refs[n_in:]
        ci, refs = refs[:c_in], refs[c_in:]
        o, refs = refs[:n_out], refs[n_out:]
        co, refs = refs[:c_out], refs[c_out:]
        s, cs = refs[:n_scr], refs[n_scr:]
        step = pl.program_id(0)
        for d in range(1, len(grid)):
            step = step * grid[d] + pl.program_id(d)
        for at, phase in phases:
            if at == 0:
                pl.when(step == 0)(functools.partial(phase, ci, co, cs))
        body(*a, *o, *s)
        for at, phase in phases:
            if at > 0:
                pl.when(step == at)(functools.partial(phase, ci, co, cs))

    outs = pl.pallas_call(
        full,
        name=name,
        grid=grid,
        in_specs=list(in_specs) + _any_specs(c_in),
        out_specs=list(out_specs) + _any_specs(c_out),
        out_shape=list(out_shape) + plan.out_shapes,
        scratch_shapes=list(scratch) + plan.scratch,
        input_output_aliases={n_in + i: n_out + t for i, t in plan.aliases.items()},
        compiler_params=_params(len(grid)),
    )(*args, *plan.inputs)
    return outs[:n_out], outs[n_out:]


def _row_tile(R):
    return 256 if R % 256 == 0 else R


def cols_from_shards(a):
    n, R, C = a.shape
    tr = _row_tile(R)

    def body(i_ref, o_ref):
        for s in range(n):
            o_ref[:, s * C : (s + 1) * C] = i_ref[s]

    return pl.pallas_call(
        body,
        name="cols_from_shards",
        grid=(R // tr,),
        in_specs=[pl.BlockSpec((n, tr, C), lambda i: (0, i, 0))],
        out_specs=pl.BlockSpec((tr, n * C), lambda i: (i, 0)),
        out_shape=jax.ShapeDtypeStruct((R, n * C), a.dtype),
        compiler_params=_params(1),
    )(a)


def shards_from_cols(a):
    R, W = a.shape
    C = W // N_DEV
    tr = _row_tile(R)

    def body(i_ref, o_ref):
        for s in range(N_DEV):
            o_ref[s] = i_ref[:, s * C : (s + 1) * C]

    return pl.pallas_call(
        body,
        name="shards_from_cols",
        grid=(R // tr,),
        in_specs=[pl.BlockSpec((tr, W), lambda i: (i, 0))],
        out_specs=pl.BlockSpec((N_DEV, tr, C), lambda i: (0, i, 0)),
        out_shape=jax.ShapeDtypeStruct((N_DEV, R, C), a.dtype),
        compiler_params=_params(1),
    )(a)


def _shift_down(u, prev8, row, n):
    out = pltpu.roll(u, n, 0)
    for k in range(n):
        out = jnp.where(row == k, prev8[8 - n + k : 8 - n + k + 1, :], out)
    return out


def _shift_up(u, next8, row, n, tm):
    out = pltpu.roll(u, tm - n, 0)
    for k in range(n):
        out = jnp.where(row == tm - n + k, next8[k : k + 1, :], out)
    return out


def conv_fwd(x, gain, w_in, cw, w_out, *, seq, tm, plan=None):
    T, D = x.shape
    tps = seq // tm

    def body(x_ref, g_ref, win_ref, cw_ref, wout_ref, x1_ref, bcx_ref, y_ref, z_ref, carry_ref):
        i = pl.program_id(0)

        @pl.when(i % tps == 0)
        def _():
            carry_ref[...] = jnp.zeros_like(carry_ref)

        xt = x_ref[...]
        h = ((xt * _rms(xt)) * g_ref[...]).astype(BF16)
        bcx = _dot(h, win_ref[...])
        bcx_ref[...] = bcx.astype(BF16)
        b, c, xv = bcx[:, :D], bcx[:, D : 2 * D], bcx[:, 2 * D :]
        u = b * xv
        row = lax.broadcasted_iota(jnp.int32, u.shape, 0)
        prev = carry_ref[...]
        u1 = _shift_down(u, prev, row, 1)
        u2 = _shift_down(u, prev, row, 2)
        carry_ref[...] = u[tm - 8 :, :]
        cwv = cw_ref[...]
        y = cwv[0:1, :] * u2 + cwv[1:2, :] * u1 + cwv[2:3, :] * u
        y_ref[...] = y
        z = (c * y).astype(BF16)
        z_ref[...] = z
        x1_ref[...] = xt + _dot(z, wout_ref[...])

    tile = pl.BlockSpec((tm, D), lambda i: (i, 0))
    return _call(
        body,
        plan=plan,
        args=(x, gain, w_in, cw, w_out),
        name="conv_fwd",
        grid=(T // tm,),
        in_specs=[
            pl.BlockSpec((tm, D), lambda i: (i, 0)),
            _resident((1, D)),
            _resident((D, 3 * D)),
            _resident((3, D)),
            _resident((D, D)),
        ],
        out_specs=[tile, pl.BlockSpec((tm, 3 * D), lambda i: (i, 0)), tile, tile],
        out_shape=[jax.ShapeDtypeStruct((T, D), F32), jax.ShapeDtypeStruct((T, 3 * D), BF16),
                   jax.ShapeDtypeStruct((T, D), F32), jax.ShapeDtypeStruct((T, D), BF16)],
        scratch=[pltpu.VMEM((8, D), F32)],
    )


def conv_bwd(dx1, x, gain, bcx, y, cw, w_in, w_out, *, seq, tm, after=None):
    T, D = x.shape
    n = T // tm
    tps = seq // tm

    def body(d_ref, x_ref, g_ref, bcx_ref, y_ref, cw_ref, win_ref, wout_ref,
             gx_ref, dbcx_ref, h_ref, d16_ref, dcw_ref, dg_ref, carry_ref):
        i = pl.program_id(0)
        t = n - 1 - i

        @pl.when(i == 0)
        def _():
            dcw_ref[...] = jnp.zeros_like(dcw_ref)
            dg_ref[...] = jnp.zeros_like(dg_ref)

        @pl.when(t % tps == tps - 1)
        def _():
            carry_ref[...] = jnp.zeros_like(carry_ref)

        d = d_ref[...]
        d16 = d.astype(BF16)
        d16_ref[...] = d16
        dz = _dot_nt(d16, wout_ref[...])
        bcx = bcx_ref[...].astype(F32)
        b, c, xv = bcx[:, :D], bcx[:, D : 2 * D], bcx[:, 2 * D :]
        u = b * xv
        row = lax.broadcasted_iota(jnp.int32, u.shape, 0)
        cwv = cw_ref[...]
        dc = dz * y_ref[...]
        dy = dz * c
        nxt = carry_ref[...]
        dy1 = _shift_up(dy, nxt, row, 1, tm)
        dy2 = _shift_up(dy, nxt, row, 2, tm)
        carry_ref[...] = dy[0:8, :]
        dcw_ref[0:1, :] += jnp.sum(dy2 * u, axis=0, keepdims=True)
        dcw_ref[1:2, :] += jnp.sum(dy1 * u, axis=0, keepdims=True)
        dcw_ref[2:3, :] += jnp.sum(dy * u, axis=0, keepdims=True)
        du = cwv[2:3, :] * dy + cwv[1:2, :] * dy1 + cwv[0:1, :] * dy2
        dbcx_ref[:, :D] = (du * xv).astype(BF16)
        dbcx_ref[:, D : 2 * D] = dc.astype(BF16)
        dbcx_ref[:, 2 * D :] = (du * b).astype(BF16)
        dh = _dot_nt(dbcx_ref[...], win_ref[...])
        xt = x_ref[...]
        r = _rms(xt)
        gn = g_ref[...]
        h_ref[...] = ((xt * r) * gn).astype(BF16)
        dx, dgn = _rms_bwd(xt, r, gn, dh)
        dg_ref[0:1, :] += dgn
        gx_ref[...] = d + dx

    rev = lambda i: (n - 1 - i, 0)
    return _call(
        body,
        after=after,
        args=(dx1, x, gain, bcx, y, cw, w_in, w_out),
        name="conv_bwd",
        grid=(n,),
        in_specs=[
            pl.BlockSpec((tm, D), rev),
            pl.BlockSpec((tm, D), rev),
            _resident((1, D)),
            pl.BlockSpec((tm, 3 * D), rev),
            pl.BlockSpec((tm, D), rev),
            _resident((3, D)),
            _resident((D, 3 * D)),
            _resident((D, D)),
        ],
        out_specs=[
            pl.BlockSpec((tm, D), rev),
            pl.BlockSpec((tm, 3 * D), rev),
            pl.BlockSpec((tm, D), rev),
            pl.BlockSpec((tm, D), rev),
            pl.BlockSpec((8, D), lambda i: (0, 0)),
            pl.BlockSpec((8, D), lambda i: (0, 0)),
        ],
        out_shape=[
            jax.ShapeDtypeStruct((T, D), F32),
            jax.ShapeDtypeStruct((T, 3 * D), BF16),
            jax.ShapeDtypeStruct((T, D), BF16),
            jax.ShapeDtypeStruct((T, D), BF16),
            jax.ShapeDtypeStruct((8, D), F32),
            jax.ShapeDtypeStruct((8, D), F32),
        ],
        scratch=[pltpu.VMEM((8, D), F32)],
    )


MXU_TILE = 256
FFN_CHUNK = 4 * MXU_TILE


def _sigmoid(g):
    return 1.0 / (1.0 + jnp.exp(-g))


def _ffn_chunks(F):
    assert F % MXU_TILE == 0
    return [(s, min(FFN_CHUNK, F - s)) for s in range(0, F, FFN_CHUNK)]


def ffn_fwd(x, gain, w_gu, w_d, *, tm, plan=None, attn=None, target=None):
    T, D = x.shape
    F = w_d.shape[0]
    row = lambda i: (i, 0)
    tile = pl.BlockSpec((tm, D), row)

    def body(*refs):
        refs = list(refs)
        x_ref, g_ref, wgu_ref, wd_ref = refs[:4]
        del refs[:4]
        if attn is not None:
            ao_ref, wo_ref = refs[:2]
            del refs[:2]
        if target is not None:
            t_ref = refs.pop(0)
        if attn is not None:
            xin_ref = refs.pop(0)
        xo_ref, gu_ref = refs[:2]
        xt = x_ref[...]
        if attn is not None:
            xt = xt + _dot(ao_ref[...], wo_ref[...])
            xin_ref[...] = xt
        h = ((xt * _rms(xt)) * g_ref[...]).astype(BF16)
        acc = xt
        for s, n in _ffn_chunks(F):
            g = _dot(h, wgu_ref[:, s : s + n])
            u = _dot(h, wgu_ref[:, F + s : F + s + n])
            gu_ref[:, s : s + n] = g
            gu_ref[:, F + s : F + s + n] = u
            a = ((g * _sigmoid(g)) * u).astype(BF16)
            acc = acc + _dot(a, wd_ref[s : s + n, :])
        if target is None:
            xo_ref[...] = acc
        else:
            s_ref = refs[2]

            @pl.when(pl.program_id(0) == 0)
            def _():
                s_ref[...] = jnp.zeros_like(s_ref)

            e = acc - t_ref[...]
            xo_ref[...] = e * (1.0 / D)
            s_ref[...] += jnp.sum(jnp.sum(e * e, axis=-1, keepdims=True), axis=0, keepdims=True)

    args = [x, gain, w_gu, w_d]
    in_specs = [tile, _resident((1, D)), _resident((D, 2 * F)), _resident((F, D))]
    out_specs = [tile, pl.BlockSpec((tm, 2 * F), row)]
    out_shape = [jax.ShapeDtypeStruct((T, D), F32), jax.ShapeDtypeStruct((T, 2 * F), F32)]
    if attn is not None:
        args += list(attn)
        in_specs += [pl.BlockSpec((tm, attn[0].shape[1]), row), _resident(attn[1].shape)]
        out_specs.insert(0, tile)
        out_shape.insert(0, jax.ShapeDtypeStruct((T, D), F32))
    if target is not None:
        args.append(target)
        in_specs.append(tile)
        out_specs.append(pl.BlockSpec((8, LANES), lambda i: (0, 0)))
        out_shape.append(jax.ShapeDtypeStruct((8, LANES), F32))
    return _call(body, plan=plan, args=args, name="ffn_fwd", grid=(T // tm,), in_specs=in_specs, out_specs=out_specs,
                 out_shape=out_shape)


def ffn_bwd(dxo, x, gain, gu, w_gu, w_d, *, tm, after=None, w_o=None):
    T, D = x.shape
    F = w_d.shape[0]

    def body(d_ref, x_ref, g_ref, gu_ref, wgu_ref, wd_ref, *rest):
        if w_o is not None:
            wo_ref, rest = rest[0], rest[1:]
        dx_ref, a_ref, dgu_ref, h_ref, d16_ref, dg_ref = rest[:6]

        @pl.when(pl.program_id(0) == 0)
        def _():
            dg_ref[...] = jnp.zeros_like(dg_ref)

        d = d_ref[...]
        d16 = d.astype(BF16)
        d16_ref[...] = d16
        dh = jnp.zeros((tm, D), F32)
        for c0, n in _ffn_chunks(F):
            g = gu_ref[:, c0 : c0 + n]
            u = gu_ref[:, F + c0 : F + c0 + n]
            da = _dot_nt(d16, wd_ref[c0 : c0 + n, :])
            s = _sigmoid(g)
            sg = g * s
            a_ref[:, c0 : c0 + n] = (sg * u).astype(BF16)
            dg16 = (da * u * (s + sg * (1.0 - s))).astype(BF16)
            du16 = (da * sg).astype(BF16)
            dgu_ref[:, c0 : c0 + n] = dg16
            dgu_ref[:, F + c0 : F + c0 + n] = du16
            dh = dh + _dot_nt(dg16, wgu_ref[:, c0 : c0 + n]) + _dot_nt(du16, wgu_ref[:, F + c0 : F + c0 + n])
        xt = x_ref[...]
        r = _rms(xt)
        gn = g_ref[...]
        h_ref[...] = ((xt * r) * gn).astype(BF16)
        dx, dgn = _rms_bwd(xt, r, gn, dh)
        dg_ref[0:1, :] += dgn
        dxi = d + dx
        dx_ref[...] = dxi
        if w_o is not None:
            dxi16_ref, dao_ref = rest[6:8]
            dxi16 = dxi.astype(BF16)
            dxi16_ref[...] = dxi16
            dao_ref[...] = _dot_nt(dxi16, wo_ref[...]).astype(BF16)

    tile = pl.BlockSpec((tm, D), lambda i: (i, 0))
    args = [dxo, x, gain, gu, w_gu, w_d]
    wide = lambda n: pl.BlockSpec((tm, n), lambda i: (i, 0))
    in_specs = [tile, tile, _resident((1, D)), wide(2 * F), _resident((D, 2 * F)), _resident((F, D))]
    out_specs = [tile, wide(F), wide(2 * F), tile, tile, pl.BlockSpec((8, D), lambda i: (0, 0))]
    out_shape = [
        jax.ShapeDtypeStruct((T, D), F32),
        jax.ShapeDtypeStruct((T, F), BF16),
        jax.ShapeDtypeStruct((T, 2 * F), BF16),
        jax.ShapeDtypeStruct((T, D), BF16),
        jax.ShapeDtypeStruct((T, D), BF16),
        jax.ShapeDtypeStruct((8, D), F32),
    ]
    if w_o is not None:
        args.append(w_o)
        in_specs.append(_resident(w_o.shape))
        out_specs += [tile, pl.BlockSpec((tm, w_o.shape[0]), lambda i: (i, 0))]
        out_shape += [jax.ShapeDtypeStruct((T, D), BF16), jax.ShapeDtypeStruct((T, w_o.shape[0]), BF16)]
    return _call(body, after=after, args=args, name="ffn_bwd", grid=(T // tm,), in_specs=in_specs, out_specs=out_specs,
                 out_shape=out_shape)


def wgrad(a, b, *, name, a_cols=0, b_cols=0, group=1, flat=False, tk, out_dtype=BF16, after=None):
    T, K = a.shape
    J = 1
    if a_cols:
        K = a_cols
        J = a.shape[1] // K
        a_spec = pl.BlockSpec((tk, K), lambda j, k: (k, j))
    else:
        a_spec = pl.BlockSpec((tk, K), lambda j, k: (k, 0))
    if b_cols:
        N = b_cols * group
        J = b.shape[1] // N
        b_spec = pl.BlockSpec((tk, N), lambda j, k: (k, j))
    else:
        N = b.shape[1]
        b_spec = pl.BlockSpec((tk, N), lambda j, k: (k, 0))
    nk = T // tk
    if flat:
        o_spec, o_shape = pl.BlockSpec((K, N), lambda j, k: (0, j)), (K, J * N)
    elif group > 1:
        o_spec, o_shape = pl.BlockSpec((group, K, b_cols), lambda j, k: (j, 0, 0)), (J * group, K, b_cols)
    else:
        o_spec, o_shape = pl.BlockSpec((None, K, N), lambda j, k: (j, 0, 0)), (J, K, N)

    def body(a_ref, b_ref, o_ref, acc_ref):
        k = pl.program_id(1)

        @pl.when(k == 0)
        def _():
            acc_ref[...] = jnp.zeros_like(acc_ref)

        acc_ref[...] += _dot_tn(a_ref[...], b_ref[...])

        @pl.when(k == nk - 1)
        def _():
            if group > 1 and not flat:
                for i in range(group):
                    o_ref[i] = acc_ref[:, i * b_cols : (i + 1) * b_cols].astype(out_dtype)
            else:
                o_ref[...] = acc_ref[...].astype(out_dtype)

    outs, _ = _call(
        body,
        after=after,
        name=name,
        grid=(J, nk),
        in_specs=[a_spec, b_spec],
        out_specs=[o_spec],
        out_shape=[jax.ShapeDtypeStruct(o_shape, out_dtype)],
        args=(a, b),
        scratch=[pltpu.VMEM((K, N), F32)],
    )
    return outs[0]


def _seg(xs, lo):
    s_lo = [jnp.sum(jnp.where(lo, x, 0.0), axis=-1, keepdims=True) for x in xs]
    s_hi = [jnp.sum(jnp.where(lo, 0.0, x), axis=-1, keepdims=True) for x in xs]
    return [jnp.where(lo, a, b) for a, b in zip(s_lo, s_hi)]


def _head_norm(xs, gains, lo):
    rs = [lax.rsqrt(s * (1.0 / HEAD_DIM) + EPS) for s in _seg([x * x for x in xs], lo)]
    return [(x * r) * g for x, r, g in zip(xs, rs, gains)], rs


def _head_norm_bwd(xs, rs, gains, dys, lo):
    xns = [x * r for x, r in zip(xs, rs)]
    dxns = [dy * g for dy, g in zip(dys, gains)]
    means = [s * (1.0 / HEAD_DIM) for s in _seg([a * b for a, b in zip(dxns, xns)], lo)]
    dxs = [r * (dxn - xn * m) for r, dxn, xn, m in zip(rs, dxns, xns, means)]
    return dxs, [jnp.sum(dy * xn, axis=0, keepdims=True) for dy, xn in zip(dys, xns)]


def _swap_halves(x):
    return pltpu.roll(x, HEAD_DIM, 1)


def qkv_proj(x, gain, w, qg, kg, *, tm):
    T, D = x.shape
    N = w.shape[1]
    kvw = N_KV_HEADS * HEAD_DIM
    nqt, nkt = D // LANES, kvw // LANES

    def body(x_ref, g_ref, w_ref, qg_ref, kg_ref, qkv_ref, q_ref, kd_ref, vd_ref):
        xt = x_ref[...]
        h = ((xt * _rms(xt)) * g_ref[...]).astype(BF16)
        qkv = _dot(h, w_ref[...])
        qkv_ref[...] = qkv
        lo = lax.broadcasted_iota(jnp.int32, (1, LANES), 1) < HEAD_DIM
        tiles = [qkv[:, t * LANES : (t + 1) * LANES] for t in range(nqt + nkt)]
        normed, _ = _head_norm(tiles, [qg_ref[...]] * nqt + [kg_ref[...]] * nkt, lo)
        for t in range(nqt):
            q_ref[:, t * LANES : (t + 1) * LANES] = (normed[t] * SCALE).astype(BF16)
        for t in range(nkt):
            kn = normed[nqt + t]
            v = qkv[:, D + kvw + t * LANES : D + kvw + (t + 1) * LANES]
            for src, dst in ((kn, kd_ref), (v, vd_ref)):
                sw = _swap_halves(src)
                dst[:, 2 * t * LANES : (2 * t + 1) * LANES] = jnp.where(lo, src, sw).astype(BF16)
                dst[:, (2 * t + 1) * LANES : (2 * t + 2) * LANES] = jnp.where(lo, sw, src).astype(BF16)

    row = lambda i: (i, 0)
    return pl.pallas_call(
        body,
        name="qkv_proj",
        grid=(T // tm,),
        in_specs=[pl.BlockSpec((tm, D), row), _resident((1, D)), _resident((D, N)), _resident((1, LANES)), _resident((1, LANES))],
        out_specs=[pl.BlockSpec((tm, N), row), pl.BlockSpec((tm, D), row), pl.BlockSpec((tm, 2 * kvw), row), pl.BlockSpec((tm, 2 * kvw), row)],
        out_shape=[
            jax.ShapeDtypeStruct((T, N), F32),
            jax.ShapeDtypeStruct((T, D), BF16),
            jax.ShapeDtypeStruct((T, 2 * kvw), BF16),
            jax.ShapeDtypeStruct((T, 2 * kvw), BF16),
        ],
        compiler_params=_params(1),
    )(x, gain, w, qg, kg)


def _attn_tables(sinks, n_q_heads):
    P = n_q_heads // N_KV_HEADS // 2
    h = jnp.arange(1, n_q_heads + 1, dtype=F32)
    slopes = jnp.exp2(-8.0 * h / n_q_heads).reshape(N_KV_HEADS, P, 1, 2, 1)
    qi = jnp.arange(BLOCK)[:, None]
    kj = jnp.arange(BLOCK)[None, :]
    dist = jnp.where(kj <= qi, qi - kj, qi + BLOCK - kj).astype(F32)
    shape = (N_KV_HEADS, P, BLOCK, 2, BLOCK)
    bias = jnp.broadcast_to(-slopes * dist[None, None, :, None, :], shape)
    sink = jnp.broadcast_to(sinks.astype(F32).reshape(N_KV_HEADS, P, 1, 2, 1), shape)
    return bias.reshape(N_KV_HEADS, P * BLOCK, 2 * BLOCK), sink.reshape(N_KV_HEADS, P * BLOCK, 2 * BLOCK)


def _attn_specs(D, nb):
    kvw2 = 2 * N_KV_HEADS * HEAD_DIM
    cur = lambda b, i: (b * nb + i, 0)
    prev = lambda b, i: (jnp.maximum(b * nb + i - 1, 0), 0)
    return [
        pl.BlockSpec((BLOCK, D), cur),
        pl.BlockSpec((BLOCK, kvw2), cur),
        pl.BlockSpec((BLOCK, kvw2), prev),
        pl.BlockSpec((BLOCK, kvw2), cur),
        pl.BlockSpec((BLOCK, kvw2), prev),
    ]


def _attn_operands(kh, P, lo, q_ref, kc_ref, kp_ref, vc_ref, vp_ref):
    sl = slice(kh * LANES, (kh + 1) * LANES)

    def cat(prev_ref, cur_ref):
        d = jnp.concatenate([prev_ref[:, sl], cur_ref[:, sl]], axis=0)
        z = jnp.zeros_like(d)
        return jnp.concatenate([jnp.where(lo, d, z), jnp.where(lo, z, d)], axis=0)

    qt = jnp.concatenate([q_ref[:, (kh * P + pr) * LANES : (kh * P + pr + 1) * LANES] for pr in range(P)], axis=0)
    return qt, cat(kp_ref, kc_ref), cat(vp_ref, vc_ref)


def _attn_exp(s_all, bias, sink, tri, first):
    out = []
    for par in range(2):
        c0 = 2 * par * BLOCK
        s = jnp.where(tri, s_all[:, c0 + BLOCK : c0 + 2 * BLOCK], jnp.where(first, NEG, s_all[:, c0 : c0 + BLOCK]))
        s = s + bias[:, par * BLOCK : (par + 1) * BLOCK]
        snk = sink[:, par * BLOCK : (par + 1) * BLOCK]
        m = jnp.maximum(jnp.max(s, axis=-1, keepdims=True), snk)
        out.append((jnp.exp(s - m), jnp.exp(snk - m)))
    return out


def _unfold(x, tri):
    z = jnp.zeros_like(x)
    return jnp.concatenate([jnp.where(tri, z, x), jnp.where(tri, x, z)], axis=1)


def _attn_masks(R):
    lane = lax.broadcasted_iota(jnp.int32, (1, LANES), 1)
    row = lax.broadcasted_iota(jnp.int32, (R, BLOCK), 0) & (BLOCK - 1)
    col = lax.broadcasted_iota(jnp.int32, (R, BLOCK), 1)
    return lane, lane < HEAD_DIM, col <= row


def attn_fwd(q16, kd, vd, bias, sink, *, seq, n_seq):
    T, D = q16.shape
    nb = seq // BLOCK
    P = D // HEAD_DIM // N_KV_HEADS // 2
    R = P * BLOCK
    KV = range(N_KV_HEADS)

    def body(q_ref, kc_ref, kp_ref, vc_ref, vp_ref, bias_ref, sink_ref, o_ref):
        first = pl.program_id(1) == 0
        _, lo, tri = _attn_masks(R)
        ops = [_attn_operands(kh, P, lo, q_ref, kc_ref, kp_ref, vc_ref, vp_ref) for kh in KV]
        s_all = [_dot_nt(ops[kh][0], ops[kh][1]) for kh in KV]
        ex = [_attn_exp(s_all[kh], bias_ref[kh], sink_ref[kh], tri, first) for kh in KV]
        den = [[jnp.sum(e, axis=-1, keepdims=True) + es for e, es in ex[kh]] for kh in KV]
        lhs = [jnp.concatenate([_unfold(e, tri) for e, _ in ex[kh]], axis=1).astype(BF16) for kh in KV]
        o = [_dot(lhs[kh], ops[kh][2]) for kh in KV]
        for kh in KV:
            out = o[kh] / jnp.where(lo, den[kh][0], den[kh][1])
            for pr in range(P):
                t = kh * P + pr
                o_ref[:, t * LANES : (t + 1) * LANES] = out[pr * BLOCK : (pr + 1) * BLOCK, :].astype(BF16)

    return pl.pallas_call(
        body,
        name="attn_fwd",
        grid=(n_seq, nb),
        in_specs=_attn_specs(D, nb) + [_resident((N_KV_HEADS, R, 2 * BLOCK)), _resident((N_KV_HEADS, R, 2 * BLOCK))],
        out_specs=pl.BlockSpec((BLOCK, D), lambda b, i: (b * nb + i, 0)),
        out_shape=jax.ShapeDtypeStruct((T, D), BF16),
        compiler_params=_params(2),
    )(q16, kd, kd, vd, vd, bias, sink)


def attn_bwd(q16, kd, vd, do, bias, sink, *, seq, n_seq):
    T, D = q16.shape
    kvw2 = 2 * N_KV_HEADS * HEAD_DIM
    nb = seq // BLOCK
    G = D // HEAD_DIM // N_KV_HEADS
    P = G // 2
    R = P * BLOCK
    KV = range(N_KV_HEADS)

    def body(q_ref, kc_ref, kp_ref, vc_ref, vp_ref, do_ref, bias_ref, sink_ref,
             dq_ref, dkc_ref, dkp_ref, dvc_ref, dvp_ref, dsink_ref):
        first = pl.program_id(1) == 0

        @pl.when(jnp.logical_and(pl.program_id(0) == 0, first))
        def _():
            dsink_ref[...] = jnp.zeros_like(dsink_ref)

        lane, lo, tri = _attn_masks(R)
        ops = [_attn_operands(kh, P, lo, q_ref, kc_ref, kp_ref, vc_ref, vp_ref) for kh in KV]
        do16 = [jnp.concatenate([do_ref[:, (kh * P + pr) * LANES : (kh * P + pr + 1) * LANES] for pr in range(P)], axis=0)
                for kh in KV]
        s_all = [_dot_nt(ops[kh][0], ops[kh][1]) for kh in KV]
        dp_all = [_dot_nt(do16[kh], ops[kh][2]) for kh in KV]
        ex = [_attn_exp(s_all[kh], bias_ref[kh], sink_ref[kh], tri, first) for kh in KV]
        den = [[jnp.sum(e, axis=-1, keepdims=True) for e, _ in ex[kh]] for kh in KV]
        dsink = jnp.zeros((1, LANES), F32)
        pf, dsf = [], []
        for kh in KV:
            ps_, ds_ = [], []
            for par in range(2):
                e, es = ex[kh][par]
                inv = 1.0 / (den[kh][par] + es)
                p = e * inv
                c0 = 2 * par * BLOCK
                dp = jnp.where(tri, dp_all[kh][:, c0 + BLOCK : c0 + 2 * BLOCK], dp_all[kh][:, c0 : c0 + BLOCK])
                delta = jnp.sum(p * dp, axis=-1, keepdims=True)
                ds_.append(_unfold(p * (dp - delta), tri))
                ps_.append(_unfold(p, tri))
                dsr = -((es * inv) * delta)
                for pr in range(P):
                    hq = kh * G + 2 * pr + par
                    tot = jnp.sum(dsr[pr * BLOCK : (pr + 1) * BLOCK, :], axis=0, keepdims=True)
                    dsink = dsink + jnp.where(lane == hq, tot, 0.0)
            pf.append(jnp.concatenate(ps_, axis=1).astype(BF16))
            dsf.append(jnp.concatenate(ds_, axis=1).astype(BF16))
        dq = [_dot(dsf[kh], ops[kh][1]) for kh in KV]
        dk = [_dot_tn(dsf[kh], ops[kh][0]) for kh in KV]
        dv = [_dot_tn(pf[kh], do16[kh]) for kh in KV]
        dsink_ref[0:1, :] += dsink
        for kh in KV:
            sl = slice(kh * LANES, (kh + 1) * LANES)
            for pr in range(P):
                t = kh * P + pr
                dq_ref[:, t * LANES : (t + 1) * LANES] = dq[kh][pr * BLOCK : (pr + 1) * BLOCK, :]
            for full, prev_ref, cur_ref in ((dk[kh], dkp_ref, dkc_ref), (dv[kh], dvp_ref, dvc_ref)):
                dup = jnp.where(lo, full[: 2 * BLOCK, :], full[2 * BLOCK :, :])
                prev_ref[:, sl] = dup[:BLOCK, :].astype(BF16)
                cur_ref[:, sl] = dup[BLOCK:, :].astype(BF16)

    cur = lambda b, i: (b * nb + i, 0)
    kv_spec = pl.BlockSpec((BLOCK, kvw2), cur)
    kv_shape = jax.ShapeDtypeStruct((T, kvw2), BF16)
    return pl.pallas_call(
        body,
        name="attn_bwd",
        grid=(n_seq, nb),
        in_specs=_attn_specs(D, nb)
        + [pl.BlockSpec((BLOCK, D), cur), _resident((N_KV_HEADS, R, 2 * BLOCK)), _resident((N_KV_HEADS, R, 2 * BLOCK))],
        out_specs=[pl.BlockSpec((BLOCK, D), cur), kv_spec, kv_spec, kv_spec, kv_spec, pl.BlockSpec((8, LANES), lambda b, i: (0, 0))],
        out_shape=[jax.ShapeDtypeStruct((T, D), F32), kv_shape, kv_shape, kv_shape, kv_shape, jax.ShapeDtypeStruct((8, LANES), F32)],
        compiler_params=_params(2),
    )(q16, kd, kd, vd, vd, do, bias, sink)


def qkv_bwd(dq, dkc, dkp, dvc, dvp, qkv, dres, x, gain, w_qkv, qg, kg, *, seq):
    T, D = x.shape
    kvw2 = dkc.shape[1]
    kvw = kvw2 // 2
    nqt, nkt = D // LANES, kvw // LANES
    nb = seq // BLOCK
    tm = 2 * BLOCK
    n = T // tm

    def body(dq_ref, dkc_ref, dkpa_ref, dkpb_ref, dvc_ref, dvpa_ref, dvpb_ref, qkv_ref, dres_ref, x_ref, g_ref, w_ref,
             qg_ref, kg_ref, dx_ref, dqkv_ref, h_ref, dg_ref, hg_ref):
        i = pl.program_id(0)

        @pl.when(i == 0)
        def _():
            dg_ref[...] = jnp.zeros_like(dg_ref)
            hg_ref[...] = jnp.zeros_like(hg_ref)

        lo = lax.broadcasted_iota(jnp.int32, (1, LANES), 1) < HEAD_DIM
        last = (2 * i + 1) % nb == nb - 1
        up = lambda ref: ref[...].astype(F32)
        dkd = up(dkc_ref) + jnp.concatenate([up(dkpa_ref), jnp.where(last, 0.0, up(dkpb_ref))], axis=0)
        dvd = up(dvc_ref) + jnp.concatenate([up(dvpa_ref), jnp.where(last, 0.0, up(dvpb_ref))], axis=0)

        def undup(d, t):
            a, b = d[:, 2 * t * LANES : (2 * t + 1) * LANES], d[:, (2 * t + 1) * LANES : (2 * t + 2) * LANES]
            return jnp.where(lo, a + _swap_halves(a), b + _swap_halves(b))

        tiles = [qkv_ref[:, t * LANES : (t + 1) * LANES] for t in range(nqt + nkt)]
        gains = [qg_ref[...]] * nqt + [kg_ref[...]] * nkt
        dys = [dq_ref[:, t * LANES : (t + 1) * LANES] * SCALE for t in range(nqt)] + [undup(dkd, t) for t in range(nkt)]
        _, rs = _head_norm(tiles, gains, lo)
        dxs, dgs = _head_norm_bwd(tiles, rs, gains, dys, lo)
        for t in range(nqt + nkt):
            dqkv_ref[:, t * LANES : (t + 1) * LANES] = dxs[t].astype(BF16)
        for t in range(nkt):
            dqkv_ref[:, D + kvw + t * LANES : D + kvw + (t + 1) * LANES] = undup(dvd, t).astype(BF16)
        hg_ref[0:1, :] += functools.reduce(lambda a, b: a + b, dgs[:nqt])
        hg_ref[1:2, :] += functools.reduce(lambda a, b: a + b, dgs[nqt:])
        dh = _dot_nt(dqkv_ref[...], w_ref[...])
        xt = x_ref[...]
        r = _rms(xt)
        gn = g_ref[...]
        h_ref[...] = ((xt * r) * gn).astype(BF16)
        dx, dgn = _rms_bwd(xt, r, gn, dh)
        dg_ref[0:1, :] += dgn
        dx_ref[...] = dres_ref[...] + dx

    row = lambda i: (i, 0)
    nxt_a = pl.BlockSpec((BLOCK, kvw2), lambda i: (2 * i + 1, 0))
    nxt_b = pl.BlockSpec((BLOCK, kvw2), lambda i: (jnp.minimum(2 * i + 2, 2 * n - 1), 0))
    return pl.pallas_call(
        body,
        name="qkv_bwd",
        grid=(n,),
        in_specs=[
            pl.BlockSpec((tm, D), row),
            pl.BlockSpec((tm, kvw2), row),
            nxt_a,
            nxt_b,
            pl.BlockSpec((tm, kvw2), row),
            nxt_a,
            nxt_b,
            pl.BlockSpec((tm, D + kvw2), row),
            pl.BlockSpec((tm, D), row),
            pl.BlockSpec((tm, D), row),
            _resident((1, D)),
            _resident((D, D + kvw2)),
            _resident((1, LANES)),
            _resident((1, LANES)),
        ],
        out_specs=[
            pl.BlockSpec((tm, D), row),
            pl.BlockSpec((tm, D + kvw2), row),
            pl.BlockSpec((tm, D), row),
            pl.BlockSpec((8, D), lambda i: (0, 0)),
            pl.BlockSpec((8, LANES), lambda i: (0, 0)),
        ],
        out_shape=[
            jax.ShapeDtypeStruct((T, D), F32),
            jax.ShapeDtypeStruct((T, D + kvw2), BF16),
            jax.ShapeDtypeStruct((T, D), BF16),
            jax.ShapeDtypeStruct((8, D), F32),
            jax.ShapeDtypeStruct((8, LANES), F32),
        ],
        compiler_params=_params(1),
    )(dq, dkc, dkp, dkp, dvc, dvp, dvp, qkv, dres, x, gain, w_qkv, qg, kg)


def local_step(x, target, gains, w, *, seq, tm=256, tm_ffn=256, tm_conv=512, tk=2048, shards=None, ex=None):
    T, D = x.shape
    n_seq = T // seq
    nm, nf, qgain, kgain, sinks = gains
    H = D // HEAD_DIM
    tk, tk_long = min(tk, T), min(2 * tk, T)
    qg2, kg2 = jnp.tile(qgain, (1, 2)), jnp.tile(kgain, (1, 2))
    bias, sinkcol = _attn_tables(sinks, H)

    dist = shards is not None
    w = dict(w)

    plan = _Gather([shards["w_gu"][0], shards["w_d"][0]]) if dist else None
    (x1, bcx, y_conv, z16), got = conv_fwd(x, nm[0:1], w["w_in"], w["cw"], w["w_out"], seq=seq, tm=tm_conv, plan=plan)
    if dist:
        w["w_gu"], w["w_d"] = [cols_from_shards(got[0]), None], [got[1].reshape(-1, D), None]
    plan = _Gather([shards["w_qkv"], shards["w_o"], shards["w_gu"][1], shards["w_d"][1]]) if dist else None
    (x2, gu0), got = ffn_fwd(x1, nf[0:1], w["w_gu"][0], w["w_d"][0], tm=2 * tm_ffn, plan=plan)
    if dist:
        w["w_qkv"], w["w_o"] = cols_from_shards(got[0]), got[1].reshape(D, D)
        w["w_gu"][1], w["w_d"][1] = cols_from_shards(got[2]), got[3].reshape(-1, D)
    qkv, q16, kd, vd = qkv_proj(x2, nm[1:2], w["w_qkv"], qg2, kg2, tm=tm_conv)
    ao = attn_fwd(q16, kd, vd, bias, sinkcol, seq=seq, n_seq=n_seq)
    (x3, dx4, gu1, sse), _ = ffn_fwd(x2, nf[1:2], w["w_gu"][1], w["w_d"][1], tm=tm_ffn, attn=(ao, w["w_o"]), target=target)

    by_dest = lambda a: a.reshape(N_DEV, -1, a.shape[-1])
    gu_cols = 2 * MXU_TILE

    def send(name, *entries):
        if ex is None:
            return None
        items = [(a, False, key, (N_DEV,) + (() if layers is None else (layers,)) + a.shape[1:], layer)
                 for a, key, layer, layers in entries]
        return ex.start(items, name=name)

    (dx3, a16, dgu, h16, d16, dnf1, dx3_16, dao), _ = ffn_bwd(
        dx4, x3, nf[1:2], gu1, w["w_gu"][1], w["w_d"][1], tm=tm, w_o=w["w_o"])
    g_gu1 = shards_from_cols(wgrad(h16, dgu, name="wgrad_gu1", b_cols=gu_cols, flat=True, tk=tk_long))
    g_d1 = by_dest(wgrad(a16, d16, name="wgrad_d1", a_cols=a16.shape[1] // 2, tk=tk))
    tok = send("exchange_ffn1", (g_gu1, "w_gu", 1, 2), (g_d1, "w_d", 1, 2))
    g_o = by_dest(wgrad(ao, dx3_16, name="wgrad_o", tk=tk, after=tok))
    dq, dkc, dkp, dvc, dvp, dsinks = attn_bwd(q16, kd, vd, dao, bias, sinkcol, seq=seq, n_seq=n_seq)
    dx2, dqkv16, h16, dnm1, dgains = qkv_bwd(dq, dkc, dkp, dvc, dvp, qkv, dx3, x2, nm[1:2], w["w_qkv"], qg2, kg2, seq=seq)
    g_qkv = shards_from_cols(wgrad(h16, dqkv16, name="wgrad_qkv", tk=tk)[0])
    tok = send("exchange_attn", (g_o, "w_o", None, None), (g_qkv, "w_qkv", None, None))
    (dx1, a16, dgu, h16, d16, dnf0), _ = ffn_bwd(dx2, x1, nf[0:1], gu0, w["w_gu"][0], w["w_d"][0], tm=tm, after=tok)
    g_gu0 = shards_from_cols(wgrad(h16, dgu, name="wgrad_gu0", b_cols=gu_cols, flat=True, tk=tk_long))
    tok = send("exchange_gu0", (g_gu0, "w_gu", 0, 2))
    g_d0 = by_dest(wgrad(a16, d16, name="wgrad_d0", a_cols=a16.shape[1] // 2, tk=tk, after=tok))
    tok = send("exchange_d0", (g_d0, "w_d", 0, 2))
    (gx, dbcx, h16, d16, dcw, dnm0), _ = conv_bwd(
        dx1, x, nm[0:1], bcx, y_conv, w["cw"], w["w_in"], w["w_out"], seq=seq, tm=tm_conv, after=tok)
    g_out = by_dest(wgrad(z16, d16, name="wgrad_out", tk=tk))
    g_cw = dcw[0:3].reshape(3, N_DEV, D // N_DEV).transpose(1, 0, 2)
    tok = send("exchange_out", (g_out, "w_out", None, None), (g_cw, "cw", None, None))
    g_in = wgrad(h16, dbcx, name="wgrad_in", b_cols=3 * D // N_DEV, group=2, tk=tk_long, after=tok)
    g = dict(w_in=g_in, cw=g_cw, w_out=g_out, w_o=g_o, w_qkv=g_qkv, w_gu=[g_gu0, g_gu1], w_d=[g_d0, g_d1])
    small = dict(nm0=dnm0, nm1=dnm1, nf0=dnf0, nf1=dnf1, gains=dgains, sinks=dsinks)
    return sse, gx, g, small


def _adamw_math(g, w, m, v):
    m = ADAM_B1 * m + (1.0 - ADAM_B1) * g
    v = ADAM_B2 * v + (1.0 - ADAM_B2) * (g * g)
    m_hat = m / (1.0 - ADAM_B1 ** ADAM_STEP)
    v_hat = v / (1.0 - ADAM_B2 ** ADAM_STEP)
    delta = -ADAM_LR * (m_hat / (jnp.sqrt(v_hat) + ADAM_EPS) + ADAM_WD * w)
    return delta, m, v


def adamw(parts, owns, w, m, v, *, name, after=None):
    n, LR, C = parts.shape
    L = len(owns)
    R = LR // L
    tr = R
    for cand in (256, 128, 88, 64, 32, 16, 8):
        if R > cand and R % cand == 0:
            tr = cand
            break
    per_layer = R // tr
    extra = [] if after is None else [after]

    def body(me_ref, p_ref, *rest):
        own_refs, (w_ref, m_ref, v_ref) = rest[:L], rest[L : L + 3]
        g_ref, d_ref, mo_ref, vo_ref = rest[L + 3 + len(extra) :]
        layer = pl.program_id(0) // per_layer
        mine = own_refs[0][...].astype(F32)
        for j in range(1, L):
            mine = jnp.where(layer == j, own_refs[j][...].astype(F32), mine)
        g = None
        for s in range(n):
            share = jnp.where(me_ref[0] == s, mine, p_ref[s].astype(F32))
            g = share if g is None else g + share
        g_ref[...] = g
        d_ref[...], mo_ref[...], vo_ref[...] = _adamw_math(g, w_ref[...], m_ref[...], v_ref[...])

    blk = pl.BlockSpec((tr, C), lambda i, me: (i, 0))
    own_specs = [pl.BlockSpec((None, tr, C), lambda i, me: (me[0], i % per_layer, 0)) if o.ndim == 3
                 else pl.BlockSpec((tr, C), lambda i, me: (i % per_layer, 0)) for o in owns]
    me = (4 * lax.axis_index("x") + 2 * lax.axis_index("y") + lax.axis_index("c")).astype(jnp.int32).reshape(1)
    return pl.pallas_call(
        body,
        name=name,
        grid_spec=pltpu.PrefetchScalarGridSpec(
            num_scalar_prefetch=1,
            grid=(LR // tr,),
            in_specs=[pl.BlockSpec((n, tr, C), lambda i, me: (0, i, 0))] + own_specs + [blk, blk, blk] + _any_specs(len(extra)),
            out_specs=[blk] * 4,
        ),
        out_shape=[jax.ShapeDtypeStruct((LR, C), F32)] * 4,
        compiler_params=_params(1),
    )(me, parts, *owns, w, m, v, *extra)


def pack_small(small, sse, D):
    W = max(D, 2 * LANES)

    def body(nm0, nm1, nf0, nf1, gains, sinks, sse_ref, o_ref):
        o_ref[...] = jnp.zeros_like(o_ref)
        o_ref[0:1, :D] = nm0[0:1, :]
        o_ref[1:2, :D] = nm1[0:1, :]
        o_ref[2:3, :D] = nf0[0:1, :]
        o_ref[3:4, :D] = nf1[0:1, :]
        gq = gains[0:1, :] + pltpu.roll(gains[0:1, :], HEAD_DIM, 1)
        gk = gains[1:2, :] + pltpu.roll(gains[1:2, :], HEAD_DIM, 1)
        lane = lax.broadcasted_iota(jnp.int32, (1, LANES), 1)
        o_ref[4:5, :LANES] = jnp.where(lane < HEAD_DIM, gq, gk)
        o_ref[4:5, LANES : 2 * LANES] = sinks[0:1, :]
        o_ref[5:6, :LANES] = sse_ref[0:1, :] * (0.5 / D)

    return pl.pallas_call(
        body,
        name="pack_small",
        out_shape=jax.ShapeDtypeStruct((8, W), F32),
    )(small["nm0"], small["nm1"], small["nf0"], small["nf1"], small["gains"], small["sinks"], sse)


def _pack_small_params(nm, nf, qg, kg, sk, D):
    W = max(D, 2 * LANES)
    row4 = jnp.concatenate([qg.reshape(-1), kg.reshape(-1), jnp.zeros((LANES - 2 * HEAD_DIM,), F32), sk.reshape(-1)])
    row4 = jnp.pad(row4, (0, W - row4.shape[0]))
    rows = [jnp.pad(r, (0, W - D)) for r in (nm[0], nm[1], nf[0], nf[1])] + [row4]
    return jnp.concatenate([jnp.stack(rows), jnp.zeros((3, W), F32)], axis=0)


def _unpack_small(a, D, H):
    nm = a[0:2, :D]
    nf = a[2:4, :D]
    qg = a[4:5, 0:HEAD_DIM]
    kg = a[4:5, HEAD_DIM : 2 * HEAD_DIM]
    sk = a[4:5, LANES : LANES + H]
    return qg, kg, sk, nm, nf


def kernel(x, conv_w_in, conv_w, conv_w_out, attn_w_qkv, attn_q_gain, attn_k_gain, attn_sinks, attn_w_o, norm_mixer, norm_ffn, ffn_w_gate_up, ffn_w_down, loss_target, m_conv_w_in, m_conv_w, m_conv_w_out, m_attn_w_qkv, m_attn_q_gain, m_attn_k_gain, m_attn_sinks, m_attn_w_o, m_norm_mixer, m_norm_ffn, m_ffn_w_gate_up, m_ffn_w_down, v_conv_w_in, v_conv_w, v_conv_w_out, v_attn_w_qkv, v_attn_q_gain, v_attn_k_gain, v_attn_sinks, v_attn_w_o, v_norm_mixer, v_norm_ffn, v_ffn_w_gate_up, v_ffn_w_down):
    n_seq, seq, D = x.shape
    T = n_seq * seq
    H = D // HEAD_DIM
    L = ffn_w_gate_up.shape[0]

    full = run_plan(_Gather([conv_w_in[0].astype(BF16), conv_w[0], conv_w_out[0].astype(BF16)]), name="gather_conv_weights")
    w = dict(w_in=cols_from_shards(full[0]), cw=full[1].transpose(1, 0, 2).reshape(3, D),
             w_out=full[2].reshape(D, D))
    shards = dict(w_gu=[ffn_w_gate_up[l].astype(BF16) for l in range(L)], w_d=[ffn_w_down[l].astype(BF16) for l in range(L)],
                  w_qkv=attn_w_qkv[0].astype(BF16), w_o=attn_w_o[0].astype(BF16))
    gains = (norm_mixer, norm_ffn, attn_q_gain, attn_k_gain, attn_sinks)
    ex = Exchange()
    sse, gx, g, small = local_step(x.reshape(T, D), loss_target.reshape(T, D), gains, w, seq=seq, shards=shards, ex=ex)
    zones, own = ex.wait([g["w_in"]], name="exchange_wait")

    packed = pack_small(small, sse, D)
    token = ex.start([(g["w_in"], False, "w_in", g["w_in"].shape, None),
                      (packed, True, "small", (N_DEV,) + packed.shape, None)], name="exchange_last")

    def flat(a):
        return a.reshape(-1, a.shape[-1])

    big = [conv_w_in, conv_w, conv_w_out, attn_w_qkv, attn_w_o, ffn_w_gate_up, ffn_w_down]
    big_m = [m_conv_w_in, m_conv_w, m_conv_w_out, m_attn_w_qkv, m_attn_w_o, m_ffn_w_gate_up, m_ffn_w_down]
    big_v = [v_conv_w_in, v_conv_w, v_conv_w_out, v_attn_w_qkv, v_attn_w_o, v_ffn_w_gate_up, v_ffn_w_down]
    keys = ["w_in", "cw", "w_out", "w_qkv", "w_o", "w_gu", "w_d"]

    def update(b, zones, own, after=None):
        zone = zones[keys[b]]
        parts = zone.reshape(N_DEV, -1, zone.shape[-1])
        layers = [None] if zone.ndim == 3 else range(zone.shape[1])
        outs = adamw(parts, [own[(keys[b], l)] for l in layers], flat(big[b]), flat(big_m[b]), flat(big_v[b]),
                     name="adamw_" + keys[b], after=after)
        return [o.reshape(big[b].shape) for o in outs]

    res = [None] + [update(b, zones, own, after=token) for b in range(1, 7)]
    zones, own = ex.wait([r[0] for r in res[1:]], name="exchange_last_wait")
    res[0] = update(0, zones, own)
    sw = _pack_small_params(norm_mixer, norm_ffn, attn_q_gain, attn_k_gain, attn_sinks, D)
    sm = _pack_small_params(m_norm_mixer, m_norm_ffn, m_attn_q_gain, m_attn_k_gain, m_attn_sinks, D)
    sv = _pack_small_params(v_norm_mixer, v_norm_ffn, v_attn_q_gain, v_attn_k_gain, v_attn_sinks, D)
    souts = adamw(zones["small"], [own[("small", None)]], sw, sm, sv, name="adamw_small")
    sres = [_unpack_small(o, D, H) for o in souts]
    loss = souts[0][5, 0]

    def ordered(i):
        r, s = [r[i] for r in res], sres[i]
        return [r[0], r[1], r[2], r[3], s[0], s[1], s[2], r[4], s[3], s[4], r[5], r[6]]

    return (loss, gx.reshape(n_seq, seq, D), *ordered(0), *ordered(1), *ordered(2), *ordered(3))
```

```python
import functools
import math

import jax
import jax.numpy as jnp
from jax import lax
from jax.experimental import pallas as pl
from jax.experimental.pallas import tpu as pltpu

F32 = jnp.float32
BF16 = jnp.bfloat16

EPS = 1e-6
HEAD_DIM = 64
N_KV_HEADS = 4
BLOCK = 128
LANES = 128
N_DEV = 8
NEG = -1e30
SCALE = 1.0 / math.sqrt(HEAD_DIM)

ADAM_LR = 0.001
ADAM_B1 = 0.9
ADAM_B2 = 0.999
ADAM_EPS = 1e-08
ADAM_WD = 0.01
ADAM_STEP = 10

V7X_VMEM_BYTES = 64 * 1024 * 1024
VMEM_LIMIT = V7X_VMEM_BYTES - 2 * 1024 * 1024
MESH = pl.DeviceIdType.MESH

_NT = (((1,), (1,)), ((), ()))
_TN = (((0,), (0,)), ((), ()))


def _params(n_grid):
    return pltpu.CompilerParams(dimension_semantics=("arbitrary",) * n_grid, vmem_limit_bytes=VMEM_LIMIT)


def _resident(shape):
    nd = len(shape)
    return pl.BlockSpec(shape, lambda *_: (0,) * nd, pipeline_mode=pl.Buffered(1))


def _rms(x):
    return lax.rsqrt(jnp.mean(x * x, axis=-1, keepdims=True) + EPS)


def _rms_bwd(x, r, gain, dh):
    xn = x * r
    dxn = dh * gain
    dx = r * (dxn - xn * jnp.mean(dxn * xn, axis=-1, keepdims=True))
    return dx, jnp.sum(dh * xn, axis=0, keepdims=True)


def _dot(a, b):
    return jnp.dot(a, b, preferred_element_type=F32)


def _dot_nt(a, b):
    return lax.dot_general(a, b, _NT, preferred_element_type=F32)


def _dot_tn(a, b):
    return lax.dot_general(a, b, _TN, preferred_element_type=F32)


def _place():
    return lax.axis_index("x"), lax.axis_index("y"), lax.axis_index("c")


def _flip(v, bit):
    return 1 - v if bit else v


def _slot(px, py, pc):
    return 4 * px + 2 * py + pc


class _Gather:
    def __init__(self, shards):
        nt = len(shards)
        self.nt = nt
        self.inputs = list(shards)
        self.out_shapes = [jax.ShapeDtypeStruct((N_DEV,) + s.shape, s.dtype) for s in shards]
        self.scratch = [pltpu.SemaphoreType.DMA((nt, 10)), pltpu.SemaphoreType.DMA((nt, 10)), pltpu.SemaphoreType.DMA((nt,))]
        self.aliases = {}
        self.split = []
        for s in shards:
            rows, tile = s.shape[0], 16 if s.dtype == BF16 else 8
            self.split.append(rows // 2 if rows % (2 * tile) == 0 else rows)

    def phases(self, total):
        assert total >= 4
        return [(0, self.start), (total // 2, self.second), (total - 2, self.forward), (total - 1, self.finish)]

    def _copies(self, ins, outs, sems):
        send_sems, recv_sems, loc_sems = sems
        x, y, c = _place()
        xn, yn, sib = (1 - x, y, c), (x, 1 - y, c), (x, y, 1 - c)
        i_me, i_xn, i_yn, i_dn = _slot(x, y, c), _slot(1 - x, y, c), _slot(x, 1 - y, c), _slot(1 - x, 1 - y, c)
        j_me, j_xn, j_yn, j_dn = _slot(x, y, 1 - c), _slot(1 - x, y, 1 - c), _slot(x, 1 - y, 1 - c), _slot(1 - x, 1 - y, 1 - c)
        local, start, need1, second, need2, forward, need3 = [], [], [], [], [], [], []
        for t in range(self.nt):
            o, rows, h = outs[t], self.inputs[t].shape[0], self.split[t]
            lo = pl.ds(0, h)
            hi = pl.ds(h, rows - h) if h < rows else None

            def rc(k, src, dst, to, t=t):
                return pltpu.make_async_remote_copy(
                    src_ref=src, dst_ref=dst, send_sem=send_sems.at[t, k], recv_sem=recv_sems.at[t, k], device_id=to,
                    device_id_type=MESH)

            def landed(k, slot, part, frm):
                ref = o.at[slot] if part is None else o.at[slot, part]
                return rc(k, ref, ref, frm)

            local.append(pltpu.make_async_copy(ins[t], o.at[i_me], loc_sems.at[t]))
            start += [rc(0, ins[t], o.at[i_me], sib), rc(1, ins[t].at[lo], o.at[i_me, lo], xn),
                      rc(4, ins[t].at[lo], o.at[i_me, lo], yn)]
            need1.append(landed(1, i_xn, lo, xn))
            second.append(rc(3, o.at[i_xn, lo], o.at[i_xn, lo], yn))
            need2 += [landed(4, i_yn, lo, yn), landed(3, i_dn, lo, yn)]
            if hi is not None:
                start += [rc(2, ins[t].at[hi], o.at[i_me, hi], yn), rc(6, ins[t].at[hi], o.at[i_me, hi], xn)]
                need1.append(landed(2, i_yn, hi, yn))
                second.append(rc(5, o.at[i_yn, hi], o.at[i_yn, hi], xn))
                need2 += [landed(6, i_xn, hi, xn), landed(5, i_dn, hi, xn)]
            forward += [rc(7, o.at[i_xn], o.at[i_xn], sib), rc(8, o.at[i_yn], o.at[i_yn], sib), rc(9, o.at[i_dn], o.at[i_dn], sib)]
            need3 += [landed(0, j_me, None, sib), landed(7, j_xn, None, sib), landed(8, j_yn, None, sib), landed(9, j_dn, None, sib)]
        return local, start, need1, second, need2, forward, need3

    def start(self, ins, outs, sems):
        local, start, *_ = self._copies(ins, outs, sems)
        for cp in local + start:
            cp.start()

    def second(self, ins, outs, sems):
        _, _, need1, second, *_ = self._copies(ins, outs, sems)
        for cp in need1:
            cp.wait_recv()
        for cp in second:
            cp.start()

    def forward(self, ins, outs, sems):
        _, _, _, _, need2, forward, _ = self._copies(ins, outs, sems)
        for cp in need2:
            cp.wait_recv()
        for cp in forward:
            cp.start()

    def finish(self, ins, outs, sems):
        local, start, _, second, _, forward, need3 = self._copies(ins, outs, sems)
        for cp in need3:
            cp.wait_recv()
        for cp in start + second + forward:
            cp.wait_send()
        for cp in local:
            cp.wait()


def _any_specs(n):
    return [pl.BlockSpec(memory_space=pl.ANY)] * n


def run_plan(plan, *, name):
    def body(*refs):
        n_in, n_out = len(plan.inputs), len(plan.out_shapes)
        ins, outs, sems = refs[:n_in], refs[n_in : n_in + n_out], refs[n_in + n_out :]
        for _, phase in plan.phases(4):
            phase(ins, outs, sems)

    return pl.pallas_call(
        body,
        name=name,
        in_specs=_any_specs(len(plan.inputs)),
        out_specs=_any_specs(len(plan.out_shapes)),
        out_shape=plan.out_shapes,
        scratch_shapes=plan.scratch,
        input_output_aliases=plan.aliases,
    )(*plan.inputs)


_HBM = pl.BlockSpec(memory_space=pltpu.HBM)
_SEM = pl.BlockSpec(memory_space=pltpu.SEMAPHORE)
_DATAFLOW = pltpu.SideEffectType.DATAFLOW_SIDE_EFFECTING


class Exchange:
    def __init__(self):
        self.zones = {}
        self.pending = []
        self.sources = []

    def start(self, items, *, name):
        nt = len(items)
        keys = list(dict.fromkeys(it[2] for it in items))
        for a, _, key, shape, _ in items:
            if key not in self.zones:
                self.zones[key] = lax.empty(shape, a.dtype)
        nz = len(keys)

        def body(*refs):
            ins, zones, sems, token = refs[:nt], refs[nt : nt + nz], refs[nt + nz : nt + nz + 2 * nt], refs[-1]
            x, y, c = _place()
            me = _slot(x, y, c)
            for k in range(1, N_DEV):
                px, py, pc = _flip(x, (k >> 2) & 1), _flip(y, (k >> 1) & 1), _flip(c, k & 1)
                for t, (_, whole, key, _, layer) in enumerate(items):
                    zone = zones[keys.index(key)]
                    pltpu.make_async_remote_copy(
                        src_ref=ins[t] if whole else ins[t].at[_slot(px, py, pc)],
                        dst_ref=zone.at[me] if layer is None else zone.at[me, layer],
                        send_sem=sems[2 * t], recv_sem=sems[2 * t + 1], device_id=(px, py, pc), device_id_type=MESH).start()
            token[...] = jnp.zeros_like(token)

        bufs = [pltpu.with_memory_space_constraint(b, pltpu.HBM) for b in [it[0] for it in items] + [self.zones[k] for k in keys]]
        outs = pl.pallas_call(
            body,
            name=name,
            in_specs=[_HBM] * (nt + nz),
            out_specs=[_SEM] * (2 * nt) + [_HBM] * (nt + nz) + [pl.BlockSpec(memory_space=pltpu.VMEM)],
            out_shape=[pltpu.SemaphoreType.DMA(())] * (2 * nt) + [pltpu.HBM(b.shape, b.dtype) for b in bufs]
            + [jax.ShapeDtypeStruct((8, LANES), F32)],
            input_output_aliases={i: 2 * nt + i for i in range(nt + nz)},
            compiler_params=pltpu.CompilerParams(has_side_effects=_DATAFLOW),
        )(*bufs)
        for t, (_, _, key, _, layer) in enumerate(items):
            self.pending.append((outs[2 * t], outs[2 * t + 1], key, layer))
        self.sources += [((it[2], it[4]), a) for it, a in zip(items, outs[2 * nt : 3 * nt])]
        for i, key in enumerate(keys):
            self.zones[key] = outs[3 * nt + i]
        return outs[-1]

    def wait(self, after, *, name):
        pending, keys = self.pending, list(self.zones)
        names, sources = [n for n, _ in self.sources], [a for _, a in self.sources]
        ns, nz, npend = len(sources), len(keys), len(pending)
        self.pending, self.sources = [], []

        def body(*refs):
            zones, sems = refs[ns : ns + nz], refs[ns + nz : ns + nz + 2 * npend]
            x, y, c = _place()
            for i, (_, _, key, layer) in enumerate(pending):
                zone = zones[keys.index(key)]
                rows = pl.ds(0, N_DEV - 1)
                seven = zone.at[rows] if layer is None else zone.at[rows, layer]
                pltpu.make_async_remote_copy(
                    src_ref=seven, dst_ref=seven, send_sem=sems[2 * i], recv_sem=sems[2 * i + 1],
                    device_id=(x, y, c), device_id_type=MESH).wait()

        bufs = list(sources) + [self.zones[k] for k in keys]
        flat_sems = [s for p in pending for s in p[:2]]
        outs = pl.pallas_call(
            body,
            name=name,
            in_specs=[_HBM] * (ns + nz) + [_SEM] * (2 * npend) + _any_specs(len(after)),
            out_specs=[_HBM] * (ns + nz),
            out_shape=[pltpu.HBM(b.shape, b.dtype) for b in bufs],
            input_output_aliases={i: i for i in range(ns + nz)},
            compiler_params=pltpu.CompilerParams(has_side_effects=_DATAFLOW),
        )(*bufs, *flat_sems, *after)
        self.zones = {}
        return dict(zip(keys, outs[ns:])), dict(zip(names, outs[:ns]))


def _call(body, *, name, grid, in_specs, out_specs, out_shape, args, scratch=(), plan=None, after=None):
    if after is not None:
        inner, n_real = body, len(in_specs)
        body = lambda *refs: inner(*refs[:n_real], *refs[n_real + 1 :])
        in_specs, args = list(in_specs) + _any_specs(1), list(args) + [after]
    n_in, n_out, n_scr = len(in_specs), len(out_specs), len(scratch)
    if plan is None:
        outs = pl.pallas_call(
            body, name=name, grid=grid, in_specs=in_specs, out_specs=out_specs, out_shape=out_shape,
            scratch_shapes=list(scratch), compiler_params=_params(len(grid)))(*args)
        return outs, None
    c_in, c_out = len(plan.inputs), len(plan.out_shapes)
    phases = plan.phases(math.prod(grid))

    def full(*refs):
        a, refs = refs[:n_in], refs[n_in:]
        ci, refs = refs[:c_in], refs[c_in:]
        o, refs = refs[:n_out], refs[n_out:]
        co, refs = refs[:c_out], refs[c_out:]
        s, cs = refs[:n_scr], refs[n_scr:]
        step = pl.program_id(0)
        for d in range(1, len(grid)):
            step = step * grid[d] + pl.program_id(d)
        for at, phase in phases:
            if at == 0:
                pl.when(step == 0)(functools.partial(phase, ci, co, cs))
        body(*a, *o, *s)
        for at, phase in phases:
            if at > 0:
                pl.when(step == at)(functools.partial(phase, ci, co, cs))

    outs = pl.pallas_call(
        full,
        name=name,
        grid=grid,
        in_specs=list(in_specs) + _any_specs(c_in),
        out_specs=list(out_specs) + _any_specs(c_out),
        out_shape=list(out_shape) + plan.out_shapes,
        scratch_shapes=list(scratch) + plan.scratch,
        input_output_aliases={n_in + i: n_out + t for i, t in plan.aliases.items()},
        compiler_params=_params(len(grid)),
    )(*args, *plan.inputs)
    return outs[:n_out], outs[n_out:]


def _row_tile(R):
    return 256 if R % 256 == 0 else R


def cols_from_shards(a):
    n, R, C = a.shape
    tr = _row_tile(R)

    def body(i_ref, o_ref):
        for s in range(n):
            o_ref[:, s * C : (s + 1) * C] = i_ref[s]

    return pl.pallas_call(
        body,
        name="cols_from_shards",
        grid=(R // tr,),
        in_specs=[pl.BlockSpec((n, tr, C), lambda i: (0, i, 0))],
        out_specs=pl.BlockSpec((tr, n * C), lambda i: (i, 0)),
        out_shape=jax.ShapeDtypeStruct((R, n * C), a.dtype),
        compiler_params=_params(1),
    )(a)


def shards_from_cols(a):
    R, W = a.shape
    C = W // N_DEV
    tr = _row_tile(R)

    def body(i_ref, o_ref):
        for s in range(N_DEV):
            o_ref[s] = i_ref[:, s * C : (s + 1) * C]

    return pl.pallas_call(
        body,
        name="shards_from_cols",
        grid=(R // tr,),
        in_specs=[pl.BlockSpec((tr, W), lambda i: (i, 0))],
        out_specs=pl.BlockSpec((N_DEV, tr, C), lambda i: (0, i, 0)),
        out_shape=jax.ShapeDtypeStruct((N_DEV, R, C), a.dtype),
        compiler_params=_params(1),
    )(a)


def _shift_down(u, prev8, row, n):
    out = pltpu.roll(u, n, 0)
    for k in range(n):
        out = jnp.where(row == k, prev8[8 - n + k : 8 - n + k + 1, :], out)
    return out


def _shift_up(u, next8, row, n, tm):
    out = pltpu.roll(u, tm - n, 0)
    for k in range(n):
        out = jnp.where(row == tm - n + k, next8[k : k + 1, :], out)
    return out


def conv_fwd(x, gain, w_in, cw, w_out, *, seq, tm, plan=None):
    T, D = x.shape
    tps = seq // tm

    def body(x_ref, g_ref, win_ref, cw_ref, wout_ref, x1_ref, bcx_ref, y_ref, z_ref, carry_ref):
        i = pl.program_id(0)

        @pl.when(i % tps == 0)
        def _():
            carry_ref[...] = jnp.zeros_like(carry_ref)

        xt = x_ref[...]
        h = ((xt * _rms(xt)) * g_ref[...]).astype(BF16)
        bcx = _dot(h, win_ref[...])
        bcx_ref[...] = bcx.astype(BF16)
        b, c, xv = bcx[:, :D], bcx[:, D : 2 * D], bcx[:, 2 * D :]
        u = b * xv
        row = lax.broadcasted_iota(jnp.int32, u.shape, 0)
        prev = carry_ref[...]
        u1 = _shift_down(u, prev, row, 1)
        u2 = _shift_down(u, prev, row, 2)
        carry_ref[...] = u[tm - 8 :, :]
        cwv = cw_ref[...]
        y = cwv[0:1, :] * u2 + cwv[1:2, :] * u1 + cwv[2:3, :] * u
        y_ref[...] = y
        z = (c * y).astype(BF16)
        z_ref[...] = z
        x1_ref[...] = xt + _dot(z, wout_ref[...])

    tile = pl.BlockSpec((tm, D), lambda i: (i, 0))
    return _call(
        body,
        plan=plan,
        args=(x, gain, w_in, cw, w_out),
        name="conv_fwd",
        grid=(T // tm,),
        in_specs=[
            pl.BlockSpec((tm, D), lambda i: (i, 0)),
            _resident((1, D)),
            _resident((D, 3 * D)),
            _resident((3, D)),
            _resident((D, D)),
        ],
        out_specs=[tile, pl.BlockSpec((tm, 3 * D), lambda i: (i, 0)), tile, tile],
        out_shape=[jax.ShapeDtypeStruct((T, D), F32), jax.ShapeDtypeStruct((T, 3 * D), BF16),
                   jax.ShapeDtypeStruct((T, D), F32), jax.ShapeDtypeStruct((T, D), BF16)],
        scratch=[pltpu.VMEM((8, D), F32)],
    )


def conv_bwd(dx1, x, gain, bcx, y, cw, w_in, w_out, *, seq, tm, after=None):
    T, D = x.shape
    n = T // tm
    tps = seq // tm

    def body(d_ref, x_ref, g_ref, bcx_ref, y_ref, cw_ref, win_ref, wout_ref,
             gx_ref, dbcx_ref, h_ref, d16_ref, dcw_ref, dg_ref, carry_ref):
        i = pl.program_id(0)
        t = n - 1 - i

        @pl.when(i == 0)
        def _():
            dcw_ref[...] = jnp.zeros_like(dcw_ref)
            dg_ref[...] = jnp.zeros_like(dg_ref)

        @pl.when(t % tps == tps - 1)
        def _():
            carry_ref[...] = jnp.zeros_like(carry_ref)

        d = d_ref[...]
        d16 = d.astype(BF16)
        d16_ref[...] = d16
        dz = _dot_nt(d16, wout_ref[...])
        bcx = bcx_ref[...].astype(F32)
        b, c, xv = bcx[:, :D], bcx[:, D : 2 * D], bcx[:, 2 * D :]
        u = b * xv
        row = lax.broadcasted_iota(jnp.int32, u.shape, 0)
        cwv = cw_ref[...]
        dc = dz * y_ref[...]
        dy = dz * c
        nxt = carry_ref[...]
        dy1 = _shift_up(dy, nxt, row, 1, tm)
        dy2 = _shift_up(dy, nxt, row, 2, tm)
        carry_ref[...] = dy[0:8, :]
        dcw_ref[0:1, :] += jnp.sum(dy2 * u, axis=0, keepdims=True)
        dcw_ref[1:2, :] += jnp.sum(dy1 * u, axis=0, keepdims=True)
        dcw_ref[2:3, :] += jnp.sum(dy * u, axis=0, keepdims=True)
        du = cwv[2:3, :] * dy + cwv[1:2, :] * dy1 + cwv[0:1, :] * dy2
        dbcx_ref[:, :D] = (du * xv).astype(BF16)
        dbcx_ref[:, D : 2 * D] = dc.astype(BF16)
        dbcx_ref[:, 2 * D :] = (du * b).astype(BF16)
        dh = _dot_nt(dbcx_ref[...], win_ref[...])
        xt = x_ref[...]
        r = _rms(xt)
        gn = g_ref[...]
        h_ref[...] = ((xt * r) * gn).astype(BF16)
        dx, dgn = _rms_bwd(xt, r, gn, dh)
        dg_ref[0:1, :] += dgn
        gx_ref[...] = d + dx

    rev = lambda i: (n - 1 - i, 0)
    return _call(
        body,
        after=after,
        args=(dx1, x, gain, bcx, y, cw, w_in, w_out),
        name="conv_bwd",
        grid=(n,),
        in_specs=[
            pl.BlockSpec((tm, D), rev),
            pl.BlockSpec((tm, D), rev),
            _resident((1, D)),
            pl.BlockSpec((tm, 3 * D), rev),
            pl.BlockSpec((tm, D), rev),
            _resident((3, D)),
            _resident((D, 3 * D)),
            _resident((D, D)),
        ],
        out_specs=[
            pl.BlockSpec((tm, D), rev),
            pl.BlockSpec((tm, 3 * D), rev),
            pl.BlockSpec((tm, D), rev),
            pl.BlockSpec((tm, D), rev),
            pl.BlockSpec((8, D), lambda i: (0, 0)),
            pl.BlockSpec((8, D), lambda i: (0, 0)),
        ],
        out_shape=[
            jax.ShapeDtypeStruct((T, D), F32),
            jax.ShapeDtypeStruct((T, 3 * D), BF16),
            jax.ShapeDtypeStruct((T, D), BF16),
            jax.ShapeDtypeStruct((T, D), BF16),
            jax.ShapeDtypeStruct((8, D), F32),
            jax.ShapeDtypeStruct((8, D), F32),
        ],
        scratch=[pltpu.VMEM((8, D), F32)],
    )


MXU_TILE = 256
FFN_CHUNK = 4 * MXU_TILE


def _sigmoid(g):
    return 1.0 / (1.0 + jnp.exp(-g))


def _ffn_chunks(F):
    assert F % MXU_TILE == 0
    return [(s, min(FFN_CHUNK, F - s)) for s in range(0, F, FFN_CHUNK)]


def ffn_fwd(x, gain, w_gu, w_d, *, tm, plan=None, attn=None, target=None):
    T, D = x.shape
    F = w_d.shape[0]
    row = lambda i: (i, 0)
    tile = pl.BlockSpec((tm, D), row)

    def body(*refs):
        refs = list(refs)
        x_ref, g_ref, wgu_ref, wd_ref = refs[:4]
        del refs[:4]
        if attn is not None:
            ao_ref, wo_ref = refs[:2]
            del refs[:2]
        if target is not None:
            t_ref = refs.pop(0)
        if attn is not None:
            xin_ref = refs.pop(0)
        xo_ref, gu_ref = refs[:2]
        xt = x_ref[...]
        if attn is not None:
            xt = xt + _dot(ao_ref[...], wo_ref[...])
            xin_ref[...] = xt
        h = ((xt * _rms(xt)) * g_ref[...]).astype(BF16)
        acc = xt
        for s, n in _ffn_chunks(F):
            g = _dot(h, wgu_ref[:, s : s + n])
            u = _dot(h, wgu_ref[:, F + s : F + s + n])
            gu_ref[:, s : s + n] = g
            gu_ref[:, F + s : F + s + n] = u
            a = ((g * _sigmoid(g)) * u).astype(BF16)
            acc = acc + _dot(a, wd_ref[s : s + n, :])
        if target is None:
            xo_ref[...] = acc
        else:
            s_ref = refs[2]

            @pl.when(pl.program_id(0) == 0)
            def _():
                s_ref[...] = jnp.zeros_like(s_ref)

            e = acc - t_ref[...]
            xo_ref[...] = e * (1.0 / D)
            s_ref[...] += jnp.sum(jnp.sum(e * e, axis=-1, keepdims=True), axis=0, keepdims=True)

    args = [x, gain, w_gu, w_d]
    in_specs = [tile, _resident((1, D)), _resident((D, 2 * F)), _resident((F, D))]
    out_specs = [tile, pl.BlockSpec((tm, 2 * F), row)]
    out_shape = [jax.ShapeDtypeStruct((T, D), F32), jax.ShapeDtypeStruct((T, 2 * F), F32)]
    if attn is not None:
        args += list(attn)
        in_specs += [pl.BlockSpec((tm, attn[0].shape[1]), row), _resident(attn[1].shape)]
        out_specs.insert(0, tile)
        out_shape.insert(0, jax.ShapeDtypeStruct((T, D), F32))
    if target is not None:
        args.append(target)
        in_specs.append(tile)
        out_specs.append(pl.BlockSpec((8, LANES), lambda i: (0, 0)))
        out_shape.append(jax.ShapeDtypeStruct((8, LANES), F32))
    return _call(body, plan=plan, args=args, name="ffn_fwd", grid=(T // tm,), in_specs=in_specs, out_specs=out_specs,
                 out_shape=out_shape)


def ffn_bwd(dxo, x, gain, gu, w_gu, w_d, *, tm, after=None, w_o=None):
    T, D = x.shape
    F = w_d.shape[0]

    def body(d_ref, x_ref, g_ref, gu_ref, wgu_ref, wd_ref, *rest):
        if w_o is not None:
            wo_ref, rest = rest[0], rest[1:]
        dx_ref, a_ref, dgu_ref, h_ref, d16_ref, dg_ref = rest[:6]

        @pl.when(pl.program_id(0) == 0)
        def _():
            dg_ref[...] = jnp.zeros_like(dg_ref)

        d = d_ref[...]
        d16 = d.astype(BF16)
        d16_ref[...] = d16
        dh = jnp.zeros((tm, D), F32)
        for c0, n in _ffn_chunks(F):
            g = gu_ref[:, c0 : c0 + n]
            u = gu_ref[:, F + c0 : F + c0 + n]
            da = _dot_nt(d16, wd_ref[c0 : c0 + n, :])
            s = _sigmoid(g)
            sg = g * s
            a_ref[:, c0 : c0 + n] = (sg * u).astype(BF16)
            dg16 = (da * u * (s + sg * (1.0 - s))).astype(BF16)
            du16 = (da * sg).astype(BF16)
            dgu_ref[:, c0 : c0 + n] = dg16
            dgu_ref[:, F + c0 : F + c0 + n] = du16
            dh = dh + _dot_nt(dg16, wgu_ref[:, c0 : c0 + n]) + _dot_nt(du16, wgu_ref[:, F + c0 : F + c0 + n])
        xt = x_ref[...]
        r = _rms(xt)
        gn = g_ref[...]
        h_ref[...] = ((xt * r) * gn).astype(BF16)
        dx, dgn = _rms_bwd(xt, r, gn, dh)
        dg_ref[0:1, :] += dgn
        dxi = d + dx
        dx_ref[...] = dxi
        if w_o is not None:
            dxi16_ref, dao_ref = rest[6:8]
            dxi16 = dxi.astype(BF16)
            dxi16_ref[...] = dxi16
            dao_ref[...] = _dot_nt(dxi16, wo_ref[...]).astype(BF16)

    tile = pl.BlockSpec((tm, D), lambda i: (i, 0))
    args = [dxo, x, gain, gu, w_gu, w_d]
    wide = lambda n: pl.BlockSpec((tm, n), lambda i: (i, 0))
    in_specs = [tile, tile, _resident((1, D)), wide(2 * F), _resident((D, 2 * F)), _resident((F, D))]
    out_specs = [tile, wide(F), wide(2 * F), tile, tile, pl.BlockSpec((8, D), lambda i: (0, 0))]
    out_shape = [
        jax.ShapeDtypeStruct((T, D), F32),
        jax.ShapeDtypeStruct((T, F), BF16),
        jax.ShapeDtypeStruct((T, 2 * F), BF16),
        jax.ShapeDtypeStruct((T, D), BF16),
        jax.ShapeDtypeStruct((T, D), BF16),
        jax.ShapeDtypeStruct((8, D), F32),
    ]
    if w_o is not None:
        args.append(w_o)
        in_specs.append(_resident(w_o.shape))
        out_specs += [tile, pl.BlockSpec((tm, w_o.shape[0]), lambda i: (i, 0))]
        out_shape += [jax.ShapeDtypeStruct((T, D), BF16), jax.ShapeDtypeStruct((T, w_o.shape[0]), BF16)]
    return _call(body, after=after, args=args, name="ffn_bwd", grid=(T // tm,), in_specs=in_specs, out_specs=out_specs,
                 out_shape=out_shape)


def wgrad(a, b, *, name, a_cols=0, b_cols=0, group=1, flat=False, tk, out_dtype=BF16, after=None):
    T, K = a.shape
    J = 1
    if a_cols:
        K = a_cols
        J = a.shape[1] // K
        a_spec = pl.BlockSpec((tk, K), lambda j, k: (k, j))
    else:
        a_spec = pl.BlockSpec((tk, K), lambda j, k: (k, 0))
    if b_cols:
        N = b_cols * group
        J = b.shape[1] // N
        b_spec = pl.BlockSpec((tk, N), lambda j, k: (k, j))
    else:
        N = b.shape[1]
        b_spec = pl.BlockSpec((tk, N), lambda j, k: (k, 0))
    nk = T // tk
    if flat:
        o_spec, o_shape = pl.BlockSpec((K, N), lambda j, k: (0, j)), (K, J * N)
    elif group > 1:
        o_spec, o_shape = pl.BlockSpec((group, K, b_cols), lambda j, k: (j, 0, 0)), (J * group, K, b_cols)
    else:
        o_spec, o_shape = pl.BlockSpec((None, K, N), lambda j, k: (j, 0, 0)), (J, K, N)

    def body(a_ref, b_ref, o_ref, acc_ref):
        k = pl.program_id(1)

        @pl.when(k == 0)
        def _():
            acc_ref[...] = jnp.zeros_like(acc_ref)

        acc_ref[...] += _dot_tn(a_ref[...], b_ref[...])

        @pl.when(k == nk - 1)
        def _():
            if group > 1 and not flat:
                for i in range(group):
                    o_ref[i] = acc_ref[:, i * b_cols : (i + 1) * b_cols].astype(out_dtype)
            else:
                o_ref[...] = acc_ref[...].astype(out_dtype)

    outs, _ = _call(
        body,
        after=after,
        name=name,
        grid=(J, nk),
        in_specs=[a_spec, b_spec],
        out_specs=[o_spec],
        out_shape=[jax.ShapeDtypeStruct(o_shape, out_dtype)],
        args=(a, b),
        scratch=[pltpu.VMEM((K, N), F32)],
    )
    return outs[0]


def _seg(xs, lo):
    s_lo = [jnp.sum(jnp.where(lo, x, 0.0), axis=-1, keepdims=True) for x in xs]
    s_hi = [jnp.sum(jnp.where(lo, 0.0, x), axis=-1, keepdims=True) for x in xs]
    return [jnp.where(lo, a, b) for a, b in zip(s_lo, s_hi)]


def _head_norm(xs, gains, lo):
    rs = [lax.rsqrt(s * (1.0 / HEAD_DIM) + EPS) for s in _seg([x * x for x in xs], lo)]
    return [(x * r) * g for x, r, g in zip(xs, rs, gains)], rs


def _head_norm_bwd(xs, rs, gains, dys, lo):
    xns = [x * r for x, r in zip(xs, rs)]
    dxns = [dy * g for dy, g in zip(dys, gains)]
    means = [s * (1.0 / HEAD_DIM) for s in _seg([a * b for a, b in zip(dxns, xns)], lo)]
    dxs = [r * (dxn - xn * m) for r, dxn, xn, m in zip(rs, dxns, xns, means)]
    return dxs, [jnp.sum(dy * xn, axis=0, keepdims=True) for dy, xn in zip(dys, xns)]


def _swap_halves(x):
    return pltpu.roll(x, HEAD_DIM, 1)


def qkv_proj(x, gain, w, qg, kg, *, tm):
    T, D = x.shape
    N = w.shape[1]
    kvw = N_KV_HEADS * HEAD_DIM
    nqt, nkt = D // LANES, kvw // LANES

    def body(x_ref, g_ref, w_ref, qg_ref, kg_ref, qkv_ref, q_ref, kd_ref, vd_ref):
        xt = x_ref[...]
        h = ((xt * _rms(xt)) * g_ref[...]).astype(BF16)
        qkv = _dot(h, w_ref[...])
        qkv_ref[...] = qkv
        lo = lax.broadcasted_iota(jnp.int32, (1, LANES), 1) < HEAD_DIM
        tiles = [qkv[:, t * LANES : (t + 1) * LANES] for t in range(nqt + nkt)]
        normed, _ = _head_norm(tiles, [qg_ref[...]] * nqt + [kg_ref[...]] * nkt, lo)
        for t in range(nqt):
            q_ref[:, t * LANES : (t + 1) * LANES] = (normed[t] * SCALE).astype(BF16)
        for t in range(nkt):
            kn = normed[nqt + t]
            v = qkv[:, D + kvw + t * LANES : D + kvw + (t + 1) * LANES]
            for src, dst in ((kn, kd_ref), (v, vd_ref)):
                sw = _swap_halves(src)
                dst[:, 2 * t * LANES : (2 * t + 1) * LANES] = jnp.where(lo, src, sw).astype(BF16)
                dst[:, (2 * t + 1) * LANES : (2 * t + 2) * LANES] = jnp.where(lo, sw, src).astype(BF16)

    row = lambda i: (i, 0)
    return pl.pallas_call(
        body,
        name="qkv_proj",
        grid=(T // tm,),
        in_specs=[pl.BlockSpec((tm, D), row), _resident((1, D)), _resident((D, N)), _resident((1, LANES)), _resident((1, LANES))],
        out_specs=[pl.BlockSpec((tm, N), row), pl.BlockSpec((tm, D), row), pl.BlockSpec((tm, 2 * kvw), row), pl.BlockSpec((tm, 2 * kvw), row)],
        out_shape=[
            jax.ShapeDtypeStruct((T, N), F32),
            jax.ShapeDtypeStruct((T, D), BF16),
            jax.ShapeDtypeStruct((T, 2 * kvw), BF16),
            jax.ShapeDtypeStruct((T, 2 * kvw), BF16),
        ],
        compiler_params=_params(1),
    )(x, gain, w, qg, kg)


def _attn_tables(sinks, n_q_heads):
    P = n_q_heads // N_KV_HEADS // 2
    h = jnp.arange(1, n_q_heads + 1, dtype=F32)
    slopes = jnp.exp2(-8.0 * h / n_q_heads).reshape(N_KV_HEADS, P, 1, 2, 1)
    qi = jnp.arange(BLOCK)[:, None]
    kj = jnp.arange(BLOCK)[None, :]
    dist = jnp.where(kj <= qi, qi - kj, qi + BLOCK - kj).astype(F32)
    shape = (N_KV_HEADS, P, BLOCK, 2, BLOCK)
    bias = jnp.broadcast_to(-slopes * dist[None, None, :, None, :], shape)
    sink = jnp.broadcast_to(sinks.astype(F32).reshape(N_KV_HEADS, P, 1, 2, 1), shape)
    return bias.reshape(N_KV_HEADS, P * BLOCK, 2 * BLOCK), sink.reshape(N_KV_HEADS, P * BLOCK, 2 * BLOCK)


def _attn_specs(D, nb):
    kvw2 = 2 * N_KV_HEADS * HEAD_DIM
    cur = lambda b, i: (b * nb + i, 0)
    prev = lambda b, i: (jnp.maximum(b * nb + i - 1, 0), 0)
    return [
        pl.BlockSpec((BLOCK, D), cur),
        pl.BlockSpec((BLOCK, kvw2), cur),
        pl.BlockSpec((BLOCK, kvw2), prev),
        pl.BlockSpec((BLOCK, kvw2), cur),
        pl.BlockSpec((BLOCK, kvw2), prev),
    ]


def _attn_operands(kh, P, lo, q_ref, kc_ref, kp_ref, vc_ref, vp_ref):
    sl = slice(kh * LANES, (kh + 1) * LANES)

    def cat(prev_ref, cur_ref):
        d = jnp.concatenate([prev_ref[:, sl], cur_ref[:, sl]], axis=0)
        z = jnp.zeros_like(d)
        return jnp.concatenate([jnp.where(lo, d, z), jnp.where(lo, z, d)], axis=0)

    qt = jnp.concatenate([q_ref[:, (kh * P + pr) * LANES : (kh * P + pr + 1) * LANES] for pr in range(P)], axis=0)
    return qt, cat(kp_ref, kc_ref), cat(vp_ref, vc_ref)


def _attn_exp(s_all, bias, sink, tri, first):
    out = []
    for par in range(2):
        c0 = 2 * par * BLOCK
        s = jnp.where(tri, s_all[:, c0 + BLOCK : c0 + 2 * BLOCK], jnp.where(first, NEG, s_all[:, c0 : c0 + BLOCK]))
        s = s + bias[:, par * BLOCK : (par + 1) * BLOCK]
        snk = sink[:, par * BLOCK : (par + 1) * BLOCK]
        m = jnp.maximum(jnp.max(s, axis=-1, keepdims=True), snk)
        out.append((jnp.exp(s - m), jnp.exp(snk - m)))
    return out


def _unfold(x, tri):
    z = jnp.zeros_like(x)
    return jnp.concatenate([jnp.where(tri, z, x), jnp.where(tri, x, z)], axis=1)


def _attn_masks(R):
    lane = lax.broadcasted_iota(jnp.int32, (1, LANES), 1)
    row = lax.broadcasted_iota(jnp.int32, (R, BLOCK), 0) & (BLOCK - 1)
    col = lax.broadcasted_iota(jnp.int32, (R, BLOCK), 1)
    return lane, lane < HEAD_DIM, col <= row


def attn_fwd(q16, kd, vd, bias, sink, *, seq, n_seq):
    T, D = q16.shape
    nb = seq // BLOCK
    P = D // HEAD_DIM // N_KV_HEADS // 2
    R = P * BLOCK
    KV = range(N_KV_HEADS)

    def body(q_ref, kc_ref, kp_ref, vc_ref, vp_ref, bias_ref, sink_ref, o_ref):
        first = pl.program_id(1) == 0
        _, lo, tri = _attn_masks(R)
        ops = [_attn_operands(kh, P, lo, q_ref, kc_ref, kp_ref, vc_ref, vp_ref) for kh in KV]
        s_all = [_dot_nt(ops[kh][0], ops[kh][1]) for kh in KV]
        ex = [_attn_exp(s_all[kh], bias_ref[kh], sink_ref[kh], tri, first) for kh in KV]
        den = [[jnp.sum(e, axis=-1, keepdims=True) + es for e, es in ex[kh]] for kh in KV]
        lhs = [jnp.concatenate([_unfold(e, tri) for e, _ in ex[kh]], axis=1).astype(BF16) for kh in KV]
        o = [_dot(lhs[kh], ops[kh][2]) for kh in KV]
        for kh in KV:
            out = o[kh] / jnp.where(lo, den[kh][0], den[kh][1])
            for pr in range(P):
                t = kh * P + pr
                o_ref[:, t * LANES : (t + 1) * LANES] = out[pr * BLOCK : (pr + 1) * BLOCK, :].astype(BF16)

    return pl.pallas_call(
        body,
        name="attn_fwd",
        grid=(n_seq, nb),
        in_specs=_attn_specs(D, nb) + [_resident((N_KV_HEADS, R, 2 * BLOCK)), _resident((N_KV_HEADS, R, 2 * BLOCK))],
        out_specs=pl.BlockSpec((BLOCK, D), lambda b, i: (b * nb + i, 0)),
        out_shape=jax.ShapeDtypeStruct((T, D), BF16),
        compiler_params=_params(2),
    )(q16, kd, kd, vd, vd, bias, sink)


def attn_bwd(q16, kd, vd, do, bias, sink, *, seq, n_seq):
    T, D = q16.shape
    kvw2 = 2 * N_KV_HEADS * HEAD_DIM
    nb = seq // BLOCK
    G = D // HEAD_DIM // N_KV_HEADS
    P = G // 2
    R = P * BLOCK
    KV = range(N_KV_HEADS)

    def body(q_ref, kc_ref, kp_ref, vc_ref, vp_ref, do_ref, bias_ref, sink_ref,
             dq_ref, dkc_ref, dkp_ref, dvc_ref, dvp_ref, dsink_ref):
        first = pl.program_id(1) == 0

        @pl.when(jnp.logical_and(pl.program_id(0) == 0, first))
        def _():
            dsink_ref[...] = jnp.zeros_like(dsink_ref)

        lane, lo, tri = _attn_masks(R)
        ops = [_attn_operands(kh, P, lo, q_ref, kc_ref, kp_ref, vc_ref, vp_ref) for kh in KV]
        do16 = [jnp.concatenate([do_ref[:, (kh * P + pr) * LANES : (kh * P + pr + 1) * LANES] for pr in range(P)], axis=0)
                for kh in KV]
        s_all = [_dot_nt(ops[kh][0], ops[kh][1]) for kh in KV]
        dp_all = [_dot_nt(do16[kh], ops[kh][2]) for kh in KV]
        ex = [_attn_exp(s_all[kh], bias_ref[kh], sink_ref[kh], tri, first) for kh in KV]
        den = [[jnp.sum(e, axis=-1, keepdims=True) for e, _ in ex[kh]] for kh in KV]
        dsink = jnp.zeros((1, LANES), F32)
        pf, dsf = [], []
        for kh in KV:
            ps_, ds_ = [], []
            for par in range(2):
                e, es = ex[kh][par]
                inv = 1.0 / (den[kh][par] + es)
                p = e * inv
                c0 = 2 * par * BLOCK
                dp = jnp.where(tri, dp_all[kh][:, c0 + BLOCK : c0 + 2 * BLOCK], dp_all[kh][:, c0 : c0 + BLOCK])
                delta = jnp.sum(p * dp, axis=-1, keepdims=True)
                ds_.append(_unfold(p * (dp - delta), tri))
                ps_.append(_unfold(p, tri))
                dsr = -((es * inv) * delta)
                for pr in range(P):
                    hq = kh * G + 2 * pr + par
                    tot = jnp.sum(dsr[pr * BLOCK : (pr + 1) * BLOCK, :], axis=0, keepdims=True)
                    dsink = dsink + jnp.where(lane == hq, tot, 0.0)
            pf.append(jnp.concatenate(ps_, axis=1).astype(BF16))
            dsf.append(jnp.concatenate(ds_, axis=1).astype(BF16))
        dq = [_dot(dsf[kh], ops[kh][1]) for kh in KV]
        dk = [_dot_tn(dsf[kh], ops[kh][0]) for kh in KV]
        dv = [_dot_tn(pf[kh], do16[kh]) for kh in KV]
        dsink_ref[0:1, :] += dsink
        for kh in KV:
            sl = slice(kh * LANES, (kh + 1) * LANES)
            for pr in range(P):
                t = kh * P + pr
                dq_ref[:, t * LANES : (t + 1) * LANES] = dq[kh][pr * BLOCK : (pr + 1) * BLOCK, :]
            for full, prev_ref, cur_ref in ((dk[kh], dkp_ref, dkc_ref), (dv[kh], dvp_ref, dvc_ref)):
                dup = jnp.where(lo, full[: 2 * BLOCK, :], full[2 * BLOCK :, :])
                prev_ref[:, sl] = dup[:BLOCK, :].astype(BF16)
                cur_ref[:, sl] = dup[BLOCK:, :].astype(BF16)

    cur = lambda b, i: (b * nb + i, 0)
    kv_spec = pl.BlockSpec((BLOCK, kvw2), cur)
    kv_shape = jax.ShapeDtypeStruct((T, kvw2), BF16)
    return pl.pallas_call(
        body,
        name="attn_bwd",
        grid=(n_seq, nb),
        in_specs=_attn_specs(D, nb)
        + [pl.BlockSpec((BLOCK, D), cur), _resident((N_KV_HEADS, R, 2 * BLOCK)), _resident((N_KV_HEADS, R, 2 * BLOCK))],
        out_specs=[pl.BlockSpec((BLOCK, D), cur), kv_spec, kv_spec, kv_spec, kv_spec, pl.BlockSpec((8, LANES), lambda b, i: (0, 0))],
        out_shape=[jax.ShapeDtypeStruct((T, D), F32), kv_shape, kv_shape, kv_shape, kv_shape, jax.ShapeDtypeStruct((8, LANES), F32)],
        compiler_params=_params(2),
    )(q16, kd, kd, vd, vd, do, bias, sink)


def qkv_bwd(dq, dkc, dkp, dvc, dvp, qkv, dres, x, gain, w_qkv, qg, kg, *, seq):
    T, D = x.shape
    kvw2 = dkc.shape[1]
    kvw = kvw2 // 2
    nqt, nkt = D // LANES, kvw // LANES
    nb = seq // BLOCK
    tm = 2 * BLOCK
    n = T // tm

    def body(dq_ref, dkc_ref, dkpa_ref, dkpb_ref, dvc_ref, dvpa_ref, dvpb_ref, qkv_ref, dres_ref, x_ref, g_ref, w_ref,
             qg_ref, kg_ref, dx_ref, dqkv_ref, h_ref, dg_ref, hg_ref):
        i = pl.program_id(0)

        @pl.when(i == 0)
        def _():
            dg_ref[...] = jnp.zeros_like(dg_ref)
            hg_ref[...] = jnp.zeros_like(hg_ref)

        lo = lax.broadcasted_iota(jnp.int32, (1, LANES), 1) < HEAD_DIM
        last = (2 * i + 1) % nb == nb - 1
        up = lambda ref: ref[...].astype(F32)
        dkd = up(dkc_ref) + jnp.concatenate([up(dkpa_ref), jnp.where(last, 0.0, up(dkpb_ref))], axis=0)
        dvd = up(dvc_ref) + jnp.concatenate([up(dvpa_ref), jnp.where(last, 0.0, up(dvpb_ref))], axis=0)

        def undup(d, t):
            a, b = d[:, 2 * t * LANES : (2 * t + 1) * LANES], d[:, (2 * t + 1) * LANES : (2 * t + 2) * LANES]
            return jnp.where(lo, a + _swap_halves(a), b + _swap_halves(b))

        tiles = [qkv_ref[:, t * LANES : (t + 1) * LANES] for t in range(nqt + nkt)]
        gains = [qg_ref[...]] * nqt + [kg_ref[...]] * nkt
        dys = [dq_ref[:, t * LANES : (t + 1) * LANES] * SCALE for t in range(nqt)] + [undup(dkd, t) for t in range(nkt)]
        _, rs = _head_norm(tiles, gains, lo)
        dxs, dgs = _head_norm_bwd(tiles, rs, gains, dys, lo)
        for t in range(nqt + nkt):
            dqkv_ref[:, t * LANES : (t + 1) * LANES] = dxs[t].astype(BF16)
        for t in range(nkt):
            dqkv_ref[:, D + kvw + t * LANES : D + kvw + (t + 1) * LANES] = undup(dvd, t).astype(BF16)
        hg_ref[0:1, :] += functools.reduce(lambda a, b: a + b, dgs[:nqt])
        hg_ref[1:2, :] += functools.reduce(lambda a, b: a + b, dgs[nqt:])
        dh = _dot_nt(dqkv_ref[...], w_ref[...])
        xt = x_ref[...]
        r = _rms(xt)
        gn = g_ref[...]
        h_ref[...] = ((xt * r) * gn).astype(BF16)
        dx, dgn = _rms_bwd(xt, r, gn, dh)
        dg_ref[0:1, :] += dgn
        dx_ref[...] = dres_ref[...] + dx

    row = lambda i: (i, 0)
    nxt_a = pl.BlockSpec((BLOCK, kvw2), lambda i: (2 * i + 1, 0))
    nxt_b = pl.BlockSpec((BLOCK, kvw2), lambda i: (jnp.minimum(2 * i + 2, 2 * n - 1), 0))
    return pl.pallas_call(
        body,
        name="qkv_bwd",
        grid=(n,),
        in_specs=[
            pl.BlockSpec((tm, D), row),
            pl.BlockSpec((tm, kvw2), row),
            nxt_a,
            nxt_b,
            pl.BlockSpec((tm, kvw2), row),
            nxt_a,
            nxt_b,
            pl.BlockSpec((tm, D + kvw2), row),
            pl.BlockSpec((tm, D), row),
            pl.BlockSpec((tm, D), row),
            _resident((1, D)),
            _resident((D, D + kvw2)),
            _resident((1, LANES)),
            _resident((1, LANES)),
        ],
        out_specs=[
            pl.BlockSpec((tm, D), row),
            pl.BlockSpec((tm, D + kvw2), row),
            pl.BlockSpec((tm, D), row),
            pl.BlockSpec((8, D), lambda i: (0, 0)),
            pl.BlockSpec((8, LANES), lambda i: (0, 0)),
        ],
        out_shape=[
            jax.ShapeDtypeStruct((T, D), F32),
            jax.ShapeDtypeStruct((T, D + kvw2), BF16),
            jax.ShapeDtypeStruct((T, D), BF16),
            jax.ShapeDtypeStruct((8, D), F32),
            jax.ShapeDtypeStruct((8, LANES), F32),
        ],
        compiler_params=_params(1),
    )(dq, dkc, dkp, dkp, dvc, dvp, dvp, qkv, dres, x, gain, w_qkv, qg, kg)


def local_step(x, target, gains, w, *, seq, tm=256, tm_ffn=256, tm_conv=512, tk=2048, shards=None, ex=None):
    T, D = x.shape
    n_seq = T // seq
    nm, nf, qgain, kgain, sinks = gains
    H = D // HEAD_DIM
    tk, tk_long = min(tk, T), min(2 * tk, T)
    qg2, kg2 = jnp.tile(qgain, (1, 2)), jnp.tile(kgain, (1, 2))
    bias, sinkcol = _attn_tables(sinks, H)

    dist = shards is not None
    w = dict(w)

    plan = _Gather([shards["w_gu"][0], shards["w_d"][0]]) if dist else None
    (x1, bcx, y_conv, z16), got = conv_fwd(x, nm[0:1], w["w_in"], w["cw"], w["w_out"], seq=seq, tm=tm_conv, plan=plan)
    if dist:
        w["w_gu"], w["w_d"] = [cols_from_shards(got[0]), None], [got[1].reshape(-1, D), None]
    plan = _Gather([shards["w_qkv"], shards["w_o"], shards["w_gu"][1], shards["w_d"][1]]) if dist else None
    (x2, gu0), got = ffn_fwd(x1, nf[0:1], w["w_gu"][0], w["w_d"][0], tm=2 * tm_ffn, plan=plan)
    if dist:
        w["w_qkv"], w["w_o"] = cols_from_shards(got[0]), got[1].reshape(D, D)
        w["w_gu"][1], w["w_d"][1] = cols_from_shards(got[2]), got[3].reshape(-1, D)
    qkv, q16, kd, vd = qkv_proj(x2, nm[1:2], w["w_qkv"], qg2, kg2, tm=tm_conv)
    ao = attn_fwd(q16, kd, vd, bias, sinkcol, seq=seq, n_seq=n_seq)
    (x3, dx4, gu1, sse), _ = ffn_fwd(x2, nf[1:2], w["w_gu"][1], w["w_d"][1], tm=tm_ffn, attn=(ao, w["w_o"]), target=target)

    by_dest = lambda a: a.reshape(N_DEV, -1, a.shape[-1])
    gu_cols = 2 * MXU_TILE

    def send(name, *entries):
        if ex is None:
            return None
        items = [(a, False, key, (N_DEV,) + (() if layers is None else (layers,)) + a.shape[1:], layer)
                 for a, key, layer, layers in entries]
        return ex.start(items, name=name)

    (dx3, a16, dgu, h16, d16, dnf1, dx3_16, dao), _ = ffn_bwd(
        dx4, x3, nf[1:2], gu1, w["w_gu"][1], w["w_d"][1], tm=tm, w_o=w["w_o"])
    g_gu1 = shards_from_cols(wgrad(h16, dgu, name="wgrad_gu1", b_cols=gu_cols, flat=True, tk=tk_long))
    g_d1 = by_dest(wgrad(a16, d16, name="wgrad_d1", a_cols=a16.shape[1] // 2, tk=tk))
    tok = send("exchange_ffn1", (g_gu1, "w_gu", 1, 2), (g_d1, "w_d", 1, 2))
    g_o = by_dest(wgrad(ao, dx3_16, name="wgrad_o", tk=tk, after=tok))
    dq, dkc, dkp, dvc, dvp, dsinks = attn_bwd(q16, kd, vd, dao, bias, sinkcol, seq=seq, n_seq=n_seq)
    dx2, dqkv16, h16, dnm1, dgains = qkv_bwd(dq, dkc, dkp, dvc, dvp, qkv, dx3, x2, nm[1:2], w["w_qkv"], qg2, kg2, seq=seq)
    g_qkv = shards_from_cols(wgrad(h16, dqkv16, name="wgrad_qkv", tk=tk)[0])
    tok = send("exchange_attn", (g_o, "w_o", None, None), (g_qkv, "w_qkv", None, None))
    (dx1, a16, dgu, h16, d16, dnf0), _ = ffn_bwd(dx2, x1, nf[0:1], gu0, w["w_gu"][0], w["w_d"][0], tm=tm, after=tok)
    g_gu0 = shards_from_cols(wgrad(h16, dgu, name="wgrad_gu0", b_cols=gu_cols, flat=True, tk=tk_long))
    tok = send("exchange_gu0", (g_gu0, "w_gu", 0, 2))
    g_d0 = by_dest(wgrad(a16, d16, name="wgrad_d0", a_cols=a16.shape[1] // 2, tk=tk, after=tok))
    tok = send("exchange_d0", (g_d0, "w_d", 0, 2))
    (gx, dbcx, h16, d16, dcw, dnm0), _ = conv_bwd(
        dx1, x, nm[0:1], bcx, y_conv, w["cw"], w["w_in"], w["w_out"], seq=seq, tm=tm_conv, after=tok)
    g_out = by_dest(wgrad(z16, d16, name="wgrad_out", tk=tk))
    g_cw = dcw[0:3].reshape(3, N_DEV, D // N_DEV).transpose(1, 0, 2)
    tok = send("exchange_out", (g_out, "w_out", None, None), (g_cw, "cw", None, None))
    g_in = wgrad(h16, dbcx, name="wgrad_in", b_cols=3 * D // N_DEV, group=2, tk=tk_long, after=tok)
    g = dict(w_in=g_in, cw=g_cw, w_out=g_out, w_o=g_o, w_qkv=g_qkv, w_gu=[g_gu0, g_gu1], w_d=[g_d0, g_d1])
    small = dict(nm0=dnm0, nm1=dnm1, nf0=dnf0, nf1=dnf1, gains=dgains, sinks=dsinks)
    return sse, gx, g, small


def _adamw_math(g, w, m, v):
    m = ADAM_B1 * m + (1.0 - ADAM_B1) * g
    v = ADAM_B2 * v + (1.0 - ADAM_B2) * (g * g)
    m_hat = m / (1.0 - ADAM_B1 ** ADAM_STEP)
    v_hat = v / (1.0 - ADAM_B2 ** ADAM_STEP)
    delta = -ADAM_LR * (m_hat / (jnp.sqrt(v_hat) + ADAM_EPS) + ADAM_WD * w)
    return delta, m, v


def adamw(parts, owns, w, m, v, *, name, after=None):
    n, LR, C = parts.shape
    L = len(owns)
    R = LR // L
    tr = R
    for cand in (256, 128, 88, 64, 32, 16, 8):
        if R > cand and R % cand == 0:
            tr = cand
            break
    per_layer = R // tr
    extra = [] if after is None else [after]

    def body(me_ref, p_ref, *rest):
        own_refs, (w_ref, m_ref, v_ref) = rest[:L], rest[L : L + 3]
        g_ref, d_ref, mo_ref, vo_ref = rest[L + 3 + len(extra) :]
        layer = pl.program_id(0) // per_layer
        mine = own_refs[0][...].astype(F32)
        for j in range(1, L):
            mine = jnp.where(layer == j, own_refs[j][...].astype(F32), mine)
        g = None
        for s in range(n):
            share = jnp.where(me_ref[0] == s, mine, p_ref[s].astype(F32))
            g = share if g is None else g + share
        g_ref[...] = g
        d_ref[...], mo_ref[...], vo_ref[...] = _adamw_math(g, w_ref[...], m_ref[...], v_ref[...])

    blk = pl.BlockSpec((tr, C), lambda i, me: (i, 0))
    own_specs = [pl.BlockSpec((None, tr, C), lambda i, me: (me[0], i % per_layer, 0)) if o.ndim == 3
                 else pl.BlockSpec((tr, C), lambda i, me: (i % per_layer, 0)) for o in owns]
    me = (4 * lax.axis_index("x") + 2 * lax.axis_index("y") + lax.axis_index("c")).astype(jnp.int32).reshape(1)
    return pl.pallas_call(
        body,
        name=name,
        grid_spec=pltpu.PrefetchScalarGridSpec(
            num_scalar_prefetch=1,
            grid=(LR // tr,),
            in_specs=[pl.BlockSpec((n, tr, C), lambda i, me: (0, i, 0))] + own_specs + [blk, blk, blk] + _any_specs(len(extra)),
            out_specs=[blk] * 4,
        ),
        out_shape=[jax.ShapeDtypeStruct((LR, C), F32)] * 4,
        compiler_params=_params(1),
    )(me, parts, *owns, w, m, v, *extra)


def pack_small(small, sse, D):
    W = max(D, 2 * LANES)

    def body(nm0, nm1, nf0, nf1, gains, sinks, sse_ref, o_ref):
        o_ref[...] = jnp.zeros_like(o_ref)
        o_ref[0:1, :D] = nm0[0:1, :]
        o_ref[1:2, :D] = nm1[0:1, :]
        o_ref[2:3, :D] = nf0[0:1, :]
        o_ref[3:4, :D] = nf1[0:1, :]
        gq = gains[0:1, :] + pltpu.roll(gains[0:1, :], HEAD_DIM, 1)
        gk = gains[1:2, :] + pltpu.roll(gains[1:2, :], HEAD_DIM, 1)
        lane = lax.broadcasted_iota(jnp.int32, (1, LANES), 1)
        o_ref[4:5, :LANES] = jnp.where(lane < HEAD_DIM, gq, gk)
        o_ref[4:5, LANES : 2 * LANES] = sinks[0:1, :]
        o_ref[5:6, :LANES] = sse_ref[0:1, :] * (0.5 / D)

    return pl.pallas_call(
        body,
        name="pack_small",
        out_shape=jax.ShapeDtypeStruct((8, W), F32),
    )(small["nm0"], small["nm1"], small["nf0"], small["nf1"], small["gains"], small["sinks"], sse)


def _pack_small_params(nm, nf, qg, kg, sk, D):
    W = max(D, 2 * LANES)
    row4 = jnp.concatenate([qg.reshape(-1), kg.reshape(-1), jnp.zeros((LANES - 2 * HEAD_DIM,), F32), sk.reshape(-1)])
    row4 = jnp.pad(row4, (0, W - row4.shape[0]))
    rows = [jnp.pad(r, (0, W - D)) for r in (nm[0], nm[1], nf[0], nf[1])] + [row4]
    return jnp.concatenate([jnp.stack(rows), jnp.zeros((3, W), F32)], axis=0)


def _unpack_small(a, D, H):
    nm = a[0:2, :D]
    nf = a[2:4, :D]
    qg = a[4:5, 0:HEAD_DIM]
    kg = a[4:5, HEAD_DIM : 2 * HEAD_DIM]
    sk = a[4:5, LANES : LANES + H]
    return qg, kg, sk, nm, nf


def kernel(x, conv_w_in, conv_w, conv_w_out, attn_w_qkv, attn_q_gain, attn_k_gain, attn_sinks, attn_w_o, norm_mixer, norm_ffn, ffn_w_gate_up, ffn_w_down, loss_target, m_conv_w_in, m_conv_w, m_conv_w_out, m_attn_w_qkv, m_attn_q_gain, m_attn_k_gain, m_attn_sinks, m_attn_w_o, m_norm_mixer, m_norm_ffn, m_ffn_w_gate_up, m_ffn_w_down, v_conv_w_in, v_conv_w, v_conv_w_out, v_attn_w_qkv, v_attn_q_gain, v_attn_k_gain, v_attn_sinks, v_attn_w_o, v_norm_mixer, v_norm_ffn, v_ffn_w_gate_up, v_ffn_w_down):
    n_seq, seq, D = x.shape
    T = n_seq * seq
    H = D // HEAD_DIM
    L = ffn_w_gate_up.shape[0]

    full = run_plan(_Gather([conv_w_in[0].astype(BF16), conv_w[0], conv_w_out[0].astype(BF16)]), name="gather_conv_weights")
    w = dict(w_in=cols_from_shards(full[0]), cw=full[1].transpose(1, 0, 2).reshape(3, D),
             w_out=full[2].reshape(D, D))
    shards = dict(w_gu=[ffn_w_gate_up[l].astype(BF16) for l in range(L)], w_d=[ffn_w_down[l].astype(BF16) for l in range(L)],
                  w_qkv=attn_w_qkv[0].astype(BF16), w_o=attn_w_o[0].astype(BF16))
    gains = (norm_mixer, norm_ffn, attn_q_gain, attn_k_gain, attn_sinks)
    ex = Exchange()
    sse, gx, g, small = local_step(x.reshape(T, D), loss_target.reshape(T, D), gains, w, seq=seq, shards=shards, ex=ex)
    zones, own = ex.wait([g["w_in"]], name="exchange_wait")

    packed = pack_small(small, sse, D)
    token = ex.start([(g["w_in"], False, "w_in", g["w_in"].shape, None),
                      (packed, True, "small", (N_DEV,) + packed.shape, None)], name="exchange_last")

    def flat(a):
        return a.reshape(-1, a.shape[-1])

    big = [conv_w_in, conv_w, conv_w_out, attn_w_qkv, attn_w_o, ffn_w_gate_up, ffn_w_down]
    big_m = [m_conv_w_in, m_conv_w, m_conv_w_out, m_attn_w_qkv, m_attn_w_o, m_ffn_w_gate_up, m_ffn_w_down]
    big_v = [v_conv_w_in, v_conv_w, v_conv_w_out, v_attn_w_qkv, v_attn_w_o, v_ffn_w_gate_up, v_ffn_w_down]
    keys = ["w_in", "cw", "w_out", "w_qkv", "w_o", "w_gu", "w_d"]

    def update(b, zones, own, after=None):
        zone = zones[keys[b]]
        parts = zone.reshape(N_DEV, -1, zone.shape[-1])
        layers = [None] if zone.ndim == 3 else range(zone.shape[1])
        outs = adamw(parts, [own[(keys[b], l)] for l in layers], flat(big[b]), flat(big_m[b]), flat(big_v[b]),
                     name="adamw_" + keys[b], after=after)
        return [o.reshape(big[b].shape) for o in outs]

    res = [None] + [update(b, zones, own, after=token) for b in range(1, 7)]
    zones, own = ex.wait([r[0] for r in res[1:]], name="exchange_last_wait")
    res[0] = update(0, zones, own)
    sw = _pack_small_params(norm_mixer, norm_ffn, attn_q_gain, attn_k_gain, attn_sinks, D)
    sm = _pack_small_params(m_norm_mixer, m_norm_ffn, m_attn_q_gain, m_attn_k_gain, m_attn_sinks, D)
    sv = _pack_small_params(v_norm_mixer, v_norm_ffn, v_attn_q_gain, v_attn_k_gain, v_attn_sinks, D)
    souts = adamw(zones["small"], [own[("small", None)]], sw, sm, sv, name="adamw_small")
    sres = [_unpack_small(o, D, H) for o in souts]
    loss = souts[0][5, 0]

    def ordered(i):
        r, s = [r[i] for r in res], sres[i]
        return [r[0], r[1], r[2], r[3], s[0], s[1], s[2], r[4], s[3], s[4], r[5], r[6]]

    return (loss, gx.reshape(n_seq, seq, D), *ordered(0), *ordered(1), *ordered(2), *ordered(3))
```

```python
import functools
import math

import jax
import jax.numpy as jnp
from jax import lax
from jax.experimental import pallas as pl
from jax.experimental.pallas import tpu as pltpu

F32 = jnp.float32
BF16 = jnp.bfloat16

EPS = 1e-6
HEAD_DIM = 64
N_KV_HEADS = 4
BLOCK = 128
LANES = 128
N_DEV = 8
NEG = -1e30
SCALE = 1.0 / math.sqrt(HEAD_DIM)

ADAM_LR = 0.001
ADAM_B1 = 0.9
ADAM_B2 = 0.999
ADAM_EPS = 1e-08
ADAM_WD = 0.01
ADAM_STEP = 10

V7X_VMEM_BYTES = 64 * 1024 * 1024
VMEM_LIMIT = V7X_VMEM_BYTES - 2 * 1024 * 1024
MESH = pl.DeviceIdType.MESH

_NT = (((1,), (1,)), ((), ()))
_TN = (((0,), (0,)), ((), ()))


def _params(n_grid):
    return pltpu.CompilerParams(dimension_semantics=("arbitrary",) * n_grid, vmem_limit_bytes=VMEM_LIMIT)


def _resident(shape):
    nd = len(shape)
    return pl.BlockSpec(shape, lambda *_: (0,) * nd, pipeline_mode=pl.Buffered(1))


def _rms(x):
    return lax.rsqrt(jnp.mean(x * x, axis=-1, keepdims=True) + EPS)


def _rms_bwd(x, r, gain, dh):
    xn = x * r
    dxn = dh * gain
    dx = r * (dxn - xn * jnp.mean(dxn * xn, axis=-1, keepdims=True))
    return dx, jnp.sum(dh * xn, axis=0, keepdims=True)


def _dot(a, b):
    return jnp.dot(a, b, preferred_element_type=F32)


def _dot_nt(a, b):
    return lax.dot_general(a, b, _NT, preferred_element_type=F32)


def _dot_tn(a, b):
    return lax.dot_general(a, b, _TN, preferred_element_type=F32)


def _place():
    return lax.axis_index("x"), lax.axis_index("y"), lax.axis_index("c")


def _flip(v, bit):
    return 1 - v if bit else v


def _slot(px, py, pc):
    return 4 * px + 2 * py + pc


class _Gather:
    def __init__(self, shards):
        nt = len(shards)
        self.nt = nt
        self.inputs = list(shards)
        self.out_shapes = [jax.ShapeDtypeStruct((N_DEV,) + s.shape, s.dtype) for s in shards]
        self.scratch = [pltpu.SemaphoreType.DMA((nt, 10)), pltpu.SemaphoreType.DMA((nt, 10)), pltpu.SemaphoreType.DMA((nt,))]
        self.aliases = {}
        self.split = []
        for s in shards:
            rows, tile = s.shape[0], 16 if s.dtype == BF16 else 8
            self.split.append(rows // 2 if rows % (2 * tile) == 0 else rows)

    def phases(self, total):
        assert total >= 8
        return [(0, self.start), (total // 2, self.second), (total - 3, self.forward), (total - 1, self.finish)]

    def _copies(self, ins, outs, sems):
        send_sems, recv_sems, loc_sems = sems
        x, y, c = _place()
        xn, yn, sib = (1 - x, y, c), (x, 1 - y, c), (x, y, 1 - c)
        i_me, i_xn, i_yn, i_dn = _slot(x, y, c), _slot(1 - x, y, c), _slot(x, 1 - y, c), _slot(1 - x, 1 - y, c)
        j_me, j_xn, j_yn, j_dn = _slot(x, y, 1 - c), _slot(1 - x, y, 1 - c), _slot(x, 1 - y, 1 - c), _slot(1 - x, 1 - y, 1 - c)
        local, start, need1, second, need2, forward, need3 = [], [], [], [], [], [], []
        for t in range(self.nt):
            o, rows, h = outs[t], self.inputs[t].shape[0], self.split[t]
            lo = pl.ds(0, h)
            hi = pl.ds(h, rows - h) if h < rows else None

            def rc(k, src, dst, to, t=t):
                return pltpu.make_async_remote_copy(
                    src_ref=src, dst_ref=dst, send_sem=send_sems.at[t, k], recv_sem=recv_sems.at[t, k], device_id=to,
                    device_id_type=MESH)

            def landed(k, slot, part, frm):
                ref = o.at[slot] if part is None else o.at[slot, part]
                return rc(k, ref, ref, frm)

            local.append(pltpu.make_async_copy(ins[t], o.at[i_me], loc_sems.at[t]))
            start += [rc(0, ins[t], o.at[i_me], sib), rc(1, ins[t].at[lo], o.at[i_me, lo], xn),
                      rc(4, ins[t].at[lo], o.at[i_me, lo], yn)]
            need1.append(landed(1, i_xn, lo, xn))
            second.append(rc(3, o.at[i_xn, lo], o.at[i_xn, lo], yn))
            need2 += [landed(4, i_yn, lo, yn), landed(3, i_dn, lo, yn)]
            if hi is not None:
                start += [rc(2, ins[t].at[hi], o.at[i_me, hi], yn), rc(6, ins[t].at[hi], o.at[i_me, hi], xn)]
                need1.append(landed(2, i_yn, hi, yn))
                second.append(rc(5, o.at[i_yn, hi], o.at[i_yn, hi], xn))
                need2 += [landed(6, i_xn, hi, xn), landed(5, i_dn, hi, xn)]
            forward += [rc(7, o.at[i_xn], o.at[i_xn], sib), rc(8, o.at[i_yn], o.at[i_yn], sib), rc(9, o.at[i_dn], o.at[i_dn], sib)]
            need3 += [landed(0, j_me, None, sib), landed(7, j_xn, None, sib), landed(8, j_yn, None, sib), landed(9, j_dn, None, sib)]
        return local, start, need1, second, need2, forward, need3

    def start(self, ins, outs, sems):
        local, start, *_ = self._copies(ins, outs, sems)
        for cp in local + start:
            cp.start()

    def second(self, ins, outs, sems):
        _, _, need1, second, *_ = self._copies(ins, outs, sems)
        for cp in need1:
            cp.wait_recv()
        for cp in second:
            cp.start()

    def forward(self, ins, outs, sems):
        _, _, _, _, need2, forward, _ = self._copies(ins, outs, sems)
        for cp in need2:
            cp.wait_recv()
        for cp in forward:
            cp.start()

    def finish(self, ins, outs, sems):
        local, start, _, second, _, forward, need3 = self._copies(ins, outs, sems)
        for cp in need3:
            cp.wait_recv()
        for cp in start + second + forward:
            cp.wait_send()
        for cp in local:
            cp.wait()


def _any_specs(n):
    return [pl.BlockSpec(memory_space=pl.ANY)] * n


def run_plan(plan, *, name):
    def body(*refs):
        n_in, n_out = len(plan.inputs), len(plan.out_shapes)
        ins, outs, sems = refs[:n_in], refs[n_in : n_in + n_out], refs[n_in + n_out :]
        for _, phase in plan.phases(8):
            phase(ins, outs, sems)

    return pl.pallas_call(
        body,
        name=name,
        in_specs=_any_specs(len(plan.inputs)),
        out_specs=_any_specs(len(plan.out_shapes)),
        out_shape=plan.out_shapes,
        scratch_shapes=plan.scratch,
        input_output_aliases=plan.aliases,
    )(*plan.inputs)


_HBM = pl.BlockSpec(memory_space=pltpu.HBM)
_SEM = pl.BlockSpec(memory_space=pltpu.SEMAPHORE)
_DATAFLOW = pltpu.SideEffectType.DATAFLOW_SIDE_EFFECTING


class Exchange:
    def __init__(self):
        self.zones = {}
        self.pending = []
        self.sources = []

    def start(self, items, *, name):
        nt = len(items)
        keys = list(dict.fromkeys(it[2] for it in items))
        for a, _, key, shape, _ in items:
            if key not in self.zones:
                self.zones[key] = lax.empty(shape, a.dtype)
        nz = len(keys)

        def body(*refs):
            ins, zones, sems, token = refs[:nt], refs[nt : nt + nz], refs[nt + nz : nt + nz + 2 * nt], refs[-1]
            x, y, c = _place()
            me = _slot(x, y, c)
            for k in range(1, N_DEV):
                px, py, pc = _flip(x, (k >> 2) & 1), _flip(y, (k >> 1) & 1), _flip(c, k & 1)
                for t, (_, whole, key, _, layer) in enumerate(items):
                    zone = zones[keys.index(key)]
                    pltpu.make_async_remote_copy(
                        src_ref=ins[t] if whole else ins[t].at[_slot(px, py, pc)],
                        dst_ref=zone.at[me] if layer is None else zone.at[me, layer],
                        send_sem=sems[2 * t], recv_sem=sems[2 * t + 1], device_id=(px, py, pc), device_id_type=MESH).start()
            token[...] = jnp.zeros_like(token)

        bufs = [pltpu.with_memory_space_constraint(b, pltpu.HBM) for b in [it[0] for it in items] + [self.zones[k] for k in keys]]
        outs = pl.pallas_call(
            body,
            name=name,
            in_specs=[_HBM] * (nt + nz),
            out_specs=[_SEM] * (2 * nt) + [_HBM] * (nt + nz) + [pl.BlockSpec(memory_space=pltpu.VMEM)],
            out_shape=[pltpu.SemaphoreType.DMA(())] * (2 * nt) + [pltpu.HBM(b.shape, b.dtype) for b in bufs]
            + [jax.ShapeDtypeStruct((8, LANES), F32)],
            input_output_aliases={i: 2 * nt + i for i in range(nt + nz)},
            compiler_params=pltpu.CompilerParams(has_side_effects=_DATAFLOW),
        )(*bufs)
        for t, (_, _, key, _, layer) in enumerate(items):
            self.pending.append((outs[2 * t], outs[2 * t + 1], key, layer))
        self.sources += [((it[2], it[4]), a) for it, a in zip(items, outs[2 * nt : 3 * nt])]
        for i, key in enumerate(keys):
            self.zones[key] = outs[3 * nt + i]
        return outs[-1]

    def wait(self, after, *, name):
        pending, keys = self.pending, list(self.zones)
        names, sources = [n for n, _ in self.sources], [a for _, a in self.sources]
        ns, nz, npend = len(sources), len(keys), len(pending)
        self.pending, self.sources = [], []

        def body(*refs):
            zones, sems = refs[ns : ns + nz], refs[ns + nz : ns + nz + 2 * npend]
            x, y, c = _place()
            for i, (_, _, key, layer) in enumerate(pending):
                zone = zones[keys.index(key)]
                rows = pl.ds(0, N_DEV - 1)
                seven = zone.at[rows] if layer is None else zone.at[rows, layer]
                pltpu.make_async_remote_copy(
                    src_ref=seven, dst_ref=seven, send_sem=sems[2 * i], recv_sem=sems[2 * i + 1],
                    device_id=(x, y, c), device_id_type=MESH).wait()

        bufs = list(sources) + [self.zones[k] for k in keys]
        flat_sems = [s for p in pending for s in p[:2]]
        outs = pl.pallas_call(
            body,
            name=name,
            in_specs=[_HBM] * (ns + nz) + [_SEM] * (2 * npend) + _any_specs(len(after)),
            out_specs=[_HBM] * (ns + nz),
            out_shape=[pltpu.HBM(b.shape, b.dtype) for b in bufs],
            input_output_aliases={i: i for i in range(ns + nz)},
            compiler_params=pltpu.CompilerParams(has_side_effects=_DATAFLOW),
        )(*bufs, *flat_sems, *after)
        self.zones = {}
        return dict(zip(keys, outs[ns:])), dict(zip(names, outs[:ns]))


def _call(body, *, name, grid, in_specs, out_specs, out_shape, args, scratch=(), plan=None, after=None):
    if after is not None:
        inner, n_real = body, len(in_specs)
        body = lambda *refs: inner(*refs[:n_real], *refs[n_real + 1 :])
        in_specs, args = list(in_specs) + _any_specs(1), list(args) + [after]
    n_in, n_out, n_scr = len(in_specs), len(out_specs), len(scratch)
    if plan is None:
        outs = pl.pallas_call(
            body, name=name, grid=grid, in_specs=in_specs, out_specs=out_specs, out_shape=out_shape,
            scratch_shapes=list(scratch), compiler_params=_params(len(grid)))(*args)
        return outs, None
    c_in, c_out = len(plan.inputs), len(plan.out_shapes)
    phases = plan.phases(math.prod(grid))

    def full(*refs):
        a, refs = refs[:n_in], refs[n_in:]
        ci, refs = refs[:c_in], refs[c_in:]
        o, refs = refs[:n_out], refs[n_out:]
        co, refs = refs[:c_out], refs[c_out:]
        s, cs = refs[:n_scr], refs[n_scr:]
        step = pl.program_id(0)
        for d in range(1, len(grid)):
            step = step * grid[d] + pl.program_id(d)
        for at, phase in phases:
            if at == 0:
                pl.when(step == 0)(functools.partial(phase, ci, co, cs))
        body(*a, *o, *s)
        for at, phase in phases:
            if at > 0:
                pl.when(step == at)(functools.partial(phase, ci, co, cs))

    outs = pl.pallas_call(
        full,
        name=name,
        grid=grid,
        in_specs=list(in_specs) + _any_specs(c_in),
        out_specs=list(out_specs) + _any_specs(c_out),
        out_shape=list(out_shape) + plan.out_shapes,
        scratch_shapes=list(scratch) + plan.scratch,
        input_output_aliases={n_in + i: n_out + t for i, t in plan.aliases.items()},
        compiler_params=_params(len(grid)),
    )(*args, *plan.inputs)
    return outs[:n_out], outs[n_out:]


def _row_tile(R):
    return 256 if R % 256 == 0 else R


def cols_from_shards(a):
    n, R, C = a.shape
    tr = _row_tile(R)

    def body(i_ref, o_ref):
        for s in range(n):
            o_ref[:, s * C : (s + 1) * C] = i_ref[s]

    return pl.pallas_call(
        body,
        name="cols_from_shards",
        grid=(R // tr,),
        in_specs=[pl.BlockSpec((n, tr, C), lambda i: (0, i, 0))],
        out_specs=pl.BlockSpec((tr, n * C), lambda i: (i, 0)),
        out_shape=jax.ShapeDtypeStruct((R, n * C), a.dtype),
        compiler_params=_params(1),
    )(a)


def shards_from_cols(a):
    R, W = a.shape
    C = W // N_DEV
    tr = _row_tile(R)

    def body(i_ref, o_ref):
        for s in range(N_DEV):
            o_ref[s] = i_ref[:, s * C : (s + 1) * C]

    return pl.pallas_call(
        body,
        name="shards_from_cols",
        grid=(R // tr,),
        in_specs=[pl.BlockSpec((tr, W), lambda i: (i, 0))],
        out_specs=pl.BlockSpec((N_DEV, tr, C), lambda i: (0, i, 0)),
        out_shape=jax.ShapeDtypeStruct((N_DEV, R, C), a.dtype),
        compiler_params=_params(1),
    )(a)


def _shift_down(u, prev8, row, n):
    out = pltpu.roll(u, n, 0)
    for k in range(n):
        out = jnp.where(row == k, prev8[8 - n + k : 8 - n + k + 1, :], out)
    return out


def _shift_up(u, next8, row, n, tm):
    out = pltpu.roll(u, tm - n, 0)
    for k in range(n):
        out = jnp.where(row == tm - n + k, next8[k : k + 1, :], out)
    return out


def conv_fwd(x, gain, w_in, cw, w_out, *, seq, tm, plan=None):
    T, D = x.shape
    tps = seq // tm

    def body(x_ref, g_ref, win_ref, cw_ref, wout_ref, x1_ref, bcx_ref, y_ref, z_ref, carry_ref):
        i = pl.program_id(0)

        @pl.when(i % tps == 0)
        def _():
            carry_ref[...] = jnp.zeros_like(carry_ref)

        xt = x_ref[...]
        h = ((xt * _rms(xt)) * g_ref[...]).astype(BF16)
        bcx = _dot(h, win_ref[...])
        bcx_ref[...] = bcx.astype(BF16)
        b, c, xv = bcx[:, :D], bcx[:, D : 2 * D], bcx[:, 2 * D :]
        u = b * xv
        row = lax.broadcasted_iota(jnp.int32, u.shape, 0)
        prev = carry_ref[...]
        u1 = _shift_down(u, prev, row, 1)
        u2 = _shift_down(u, prev, row, 2)
        carry_ref[...] = u[tm - 8 :, :]
        cwv = cw_ref[...]
        y = cwv[0:1, :] * u2 + cwv[1:2, :] * u1 + cwv[2:3, :] * u
        y_ref[...] = y
        z = (c * y).astype(BF16)
        z_ref[...] = z
        x1_ref[...] = xt + _dot(z, wout_ref[...])

    tile = pl.BlockSpec((tm, D), lambda i: (i, 0))
    return _call(
        body,
        plan=plan,
        args=(x, gain, w_in, cw, w_out),
        name="conv_fwd",
        grid=(T // tm,),
        in_specs=[
            pl.BlockSpec((tm, D), lambda i: (i, 0)),
            _resident((1, D)),
            _resident((D, 3 * D)),
            _resident((3, D)),
            _resident((D, D)),
        ],
        out_specs=[tile, pl.BlockSpec((tm, 3 * D), lambda i: (i, 0)), tile, tile],
        out_shape=[jax.ShapeDtypeStruct((T, D), F32), jax.ShapeDtypeStruct((T, 3 * D), BF16),
                   jax.ShapeDtypeStruct((T, D), F32), jax.ShapeDtypeStruct((T, D), BF16)],
        scratch=[pltpu.VMEM((8, D), F32)],
    )


def conv_bwd(dx1, x, gain, bcx, y, cw, w_in, w_out, *, seq, tm, after=None):
    T, D = x.shape
    n = T // tm
    tps = seq // tm

    def body(d_ref, x_ref, g_ref, bcx_ref, y_ref, cw_ref, win_ref, wout_ref,
             gx_ref, dbcx_ref, h_ref, d16_ref, dcw_ref, dg_ref, carry_ref):
        i = pl.program_id(0)
        t = n - 1 - i

        @pl.when(i == 0)
        def _():
            dcw_ref[...] = jnp.zeros_like(dcw_ref)
            dg_ref[...] = jnp.zeros_like(dg_ref)

        @pl.when(t % tps == tps - 1)
        def _():
            carry_ref[...] = jnp.zeros_like(carry_ref)

        d = d_ref[...]
        d16 = d.astype(BF16)
        d16_ref[...] = d16
        dz = _dot_nt(d16, wout_ref[...])
        bcx = bcx_ref[...].astype(F32)
        b, c, xv = bcx[:, :D], bcx[:, D : 2 * D], bcx[:, 2 * D :]
        u = b * xv
        row = lax.broadcasted_iota(jnp.int32, u.shape, 0)
        cwv = cw_ref[...]
        dc = dz * y_ref[...]
        dy = dz * c
        nxt = carry_ref[...]
        dy1 = _shift_up(dy, nxt, row, 1, tm)
        dy2 = _shift_up(dy, nxt, row, 2, tm)
        carry_ref[...] = dy[0:8, :]
        dcw_ref[0:1, :] += jnp.sum(dy2 * u, axis=0, keepdims=True)
        dcw_ref[1:2, :] += jnp.sum(dy1 * u, axis=0, keepdims=True)
        dcw_ref[2:3, :] += jnp.sum(dy * u, axis=0, keepdims=True)
        du = cwv[2:3, :] * dy + cwv[1:2, :] * dy1 + cwv[0:1, :] * dy2
        dbcx_ref[:, :D] = (du * xv).astype(BF16)
        dbcx_ref[:, D : 2 * D] = dc.astype(BF16)
        dbcx_ref[:, 2 * D :] = (du * b).astype(BF16)
        dh = _dot_nt(dbcx_ref[...], win_ref[...])
        xt = x_ref[...]
        r = _rms(xt)
        gn = g_ref[...]
        h_ref[...] = ((xt * r) * gn).astype(BF16)
        dx, dgn = _rms_bwd(xt, r, gn, dh)
        dg_ref[0:1, :] += dgn
        gx_ref[...] = d + dx

    rev = lambda i: (n - 1 - i, 0)
    return _call(
        body,
        after=after,
        args=(dx1, x, gain, bcx, y, cw, w_in, w_out),
        name="conv_bwd",
        grid=(n,),
        in_specs=[
            pl.BlockSpec((tm, D), rev),
            pl.BlockSpec((tm, D), rev),
            _resident((1, D)),
            pl.BlockSpec((tm, 3 * D), rev),
            pl.BlockSpec((tm, D), rev),
            _resident((3, D)),
            _resident((D, 3 * D)),
            _resident((D, D)),
        ],
        out_specs=[
            pl.BlockSpec((tm, D), rev),
            pl.BlockSpec((tm, 3 * D), rev),
            pl.BlockSpec((tm, D), rev),
            pl.BlockSpec((tm, D), rev),
            pl.BlockSpec((8, D), lambda i: (0, 0)),
            pl.BlockSpec((8, D), lambda i: (0, 0)),
        ],
        out_shape=[
            jax.ShapeDtypeStruct((T, D), F32),
            jax.ShapeDtypeStruct((T, 3 * D), BF16),
            jax.ShapeDtypeStruct((T, D), BF16),
            jax.ShapeDtypeStruct((T, D), BF16),
            jax.ShapeDtypeStruct((8, D), F32),
            jax.ShapeDtypeStruct((8, D), F32),
        ],
        scratch=[pltpu.VMEM((8, D), F32)],
    )


MXU_TILE = 256
FFN_CHUNK = 4 * MXU_TILE


def _sigmoid(g):
    return 1.0 / (1.0 + jnp.exp(-g))


def _ffn_chunks(F):
    assert F % MXU_TILE == 0
    return [(s, min(FFN_CHUNK, F - s)) for s in range(0, F, FFN_CHUNK)]


def ffn_fwd(x, gain, w_gu, w_d, *, tm, plan=None, attn=None, target=None):
    T, D = x.shape
    F = w_d.shape[0]
    row = lambda i: (i, 0)
    tile = pl.BlockSpec((tm, D), row)

    def body(*refs):
        refs = list(refs)
        x_ref, g_ref, wgu_ref, wd_ref = refs[:4]
        del refs[:4]
        if attn is not None:
            ao_ref, wo_ref = refs[:2]
            del refs[:2]
        if target is not None:
            t_ref = refs.pop(0)
        if attn is not None:
            xin_ref = refs.pop(0)
        xo_ref, gu_ref = refs[:2]
        xt = x_ref[...]
        if attn is not None:
            xt = xt + _dot(ao_ref[...], wo_ref[...])
            xin_ref[...] = xt
        h = ((xt * _rms(xt)) * g_ref[...]).astype(BF16)
        acc = xt
        for s, n in _ffn_chunks(F):
            g = _dot(h, wgu_ref[:, s : s + n])
            u = _dot(h, wgu_ref[:, F + s : F + s + n])
            gu_ref[:, s : s + n] = g
            gu_ref[:, F + s : F + s + n] = u
            a = ((g * _sigmoid(g)) * u).astype(BF16)
            acc = acc + _dot(a, wd_ref[s : s + n, :])
        if target is None:
            xo_ref[...] = acc
        else:
            s_ref = refs[2]

            @pl.when(pl.program_id(0) == 0)
            def _():
                s_ref[...] = jnp.zeros_like(s_ref)

            e = acc - t_ref[...]
            xo_ref[...] = e * (1.0 / D)
            s_ref[...] += jnp.sum(jnp.sum(e * e, axis=-1, keepdims=True), axis=0, keepdims=True)

    args = [x, gain, w_gu, w_d]
    in_specs = [tile, _resident((1, D)), _resident((D, 2 * F)), _resident((F, D))]
    out_specs = [tile, pl.BlockSpec((tm, 2 * F), row)]
    out_shape = [jax.ShapeDtypeStruct((T, D), F32), jax.ShapeDtypeStruct((T, 2 * F), F32)]
    if attn is not None:
        args += list(attn)
        in_specs += [pl.BlockSpec((tm, attn[0].shape[1]), row), _resident(attn[1].shape)]
        out_specs.insert(0, tile)
        out_shape.insert(0, jax.ShapeDtypeStruct((T, D), F32))
    if target is not None:
        args.append(target)
        in_specs.append(tile)
        out_specs.append(pl.BlockSpec((8, LANES), lambda i: (0, 0)))
        out_shape.append(jax.ShapeDtypeStruct((8, LANES), F32))
    return _call(body, plan=plan, args=args, name="ffn_fwd", grid=(T // tm,), in_specs=in_specs, out_specs=out_specs,
                 out_shape=out_shape)


def ffn_bwd(dxo, x, gain, gu, w_gu, w_d, *, tm, after=None, w_o=None):
    T, D = x.shape
    F = w_d.shape[0]

    def body(d_ref, x_ref, g_ref, gu_ref, wgu_ref, wd_ref, *rest):
        if w_o is not None:
            wo_ref, rest = rest[0], rest[1:]
        dx_ref, a_ref, dgu_ref, h_ref, d16_ref, dg_ref = rest[:6]

        @pl.when(pl.program_id(0) == 0)
        def _():
            dg_ref[...] = jnp.zeros_like(dg_ref)

        d = d_ref[...]
        d16 = d.astype(BF16)
        d16_ref[...] = d16
        dh = jnp.zeros((tm, D), F32)
        for c0, n in _ffn_chunks(F):
            g = gu_ref[:, c0 : c0 + n]
            u = gu_ref[:, F + c0 : F + c0 + n]
            da = _dot_nt(d16, wd_ref[c0 : c0 + n, :])
            s = _sigmoid(g)
            sg = g * s
            a_ref[:, c0 : c0 + n] = (sg * u).astype(BF16)
            dg16 = (da * u * (s + sg * (1.0 - s))).astype(BF16)
            du16 = (da * sg).astype(BF16)
            dgu_ref[:, c0 : c0 + n] = dg16
            dgu_ref[:, F + c0 : F + c0 + n] = du16
            dh = dh + _dot_nt(dg16, wgu_ref[:, c0 : c0 + n]) + _dot_nt(du16, wgu_ref[:, F + c0 : F + c0 + n])
        xt = x_ref[...]
        r = _rms(xt)
        gn = g_ref[...]
        h_ref[...] = ((xt * r) * gn).astype(BF16)
        dx, dgn = _rms_bwd(xt, r, gn, dh)
        dg_ref[0:1, :] += dgn
        dxi = d + dx
        dx_ref[...] = dxi
        if w_o is not None:
            dxi16_ref, dao_ref = rest[6:8]
            dxi16 = dxi.astype(BF16)
            dxi16_ref[...] = dxi16
            dao_ref[...] = _dot_nt(dxi16, wo_ref[...]).astype(BF16)

    tile = pl.BlockSpec((tm, D), lambda i: (i, 0))
    args = [dxo, x, gain, gu, w_gu, w_d]
    wide = lambda n: pl.BlockSpec((tm, n), lambda i: (i, 0))
    in_specs = [tile, tile, _resident((1, D)), wide(2 * F), _resident((D, 2 * F)), _resident((F, D))]
    out_specs = [tile, wide(F), wide(2 * F), tile, tile, pl.BlockSpec((8, D), lambda i: (0, 0))]
    out_shape = [
        jax.ShapeDtypeStruct((T, D), F32),
        jax.ShapeDtypeStruct((T, F), BF16),
        jax.ShapeDtypeStruct((T, 2 * F), BF16),
        jax.ShapeDtypeStruct((T, D), BF16),
        jax.ShapeDtypeStruct((T, D), BF16),
        jax.ShapeDtypeStruct((8, D), F32),
    ]
    if w_o is not None:
        args.append(w_o)
        in_specs.append(_resident(w_o.shape))
        out_specs += [tile, pl.BlockSpec((tm, w_o.shape[0]), lambda i: (i, 0))]
        out_shape += [jax.ShapeDtypeStruct((T, D), BF16), jax.ShapeDtypeStruct((T, w_o.shape[0]), BF16)]
    return _call(body, after=after, args=args, name="ffn_bwd", grid=(T // tm,), in_specs=in_specs, out_specs=out_specs,
                 out_shape=out_shape)


def wgrad(a, b, *, name, a_cols=0, b_cols=0, group=1, flat=False, tk, out_dtype=BF16, after=None):
    T, K = a.shape
    J = 1
    if a_cols:
        K = a_cols
        J = a.shape[1] // K
        a_spec = pl.BlockSpec((tk, K), lambda j, k: (k, j))
    else:
        a_spec = pl.BlockSpec((tk, K), lambda j, k: (k, 0))
    if b_cols:
        N = b_cols * group
        J = b.shape[1] // N
        b_spec = pl.BlockSpec((tk, N), lambda j, k: (k, j))
    else:
        N = b.shape[1]
        b_spec = pl.BlockSpec((tk, N), lambda j, k: (k, 0))
    nk = T // tk
    if flat:
        o_spec, o_shape = pl.BlockSpec((K, N), lambda j, k: (0, j)), (K, J * N)
    elif group > 1:
        o_spec, o_shape = pl.BlockSpec((group, K, b_cols), lambda j, k: (j, 0, 0)), (J * group, K, b_cols)
    else:
        o_spec, o_shape = pl.BlockSpec((None, K, N), lambda j, k: (j, 0, 0)), (J, K, N)

    def body(a_ref, b_ref, o_ref, acc_ref):
        k = pl.program_id(1)

        @pl.when(k == 0)
        def _():
            acc_ref[...] = jnp.zeros_like(acc_ref)

        acc_ref[...] += _dot_tn(a_ref[...], b_ref[...])

        @pl.when(k == nk - 1)
        def _():
            if group > 1 and not flat:
                for i in range(group):
                    o_ref[i] = acc_ref[:, i * b_cols : (i + 1) * b_cols].astype(out_dtype)
            else:
                o_ref[...] = acc_ref[...].astype(out_dtype)

    outs, _ = _call(
        body,
        after=after,
        name=name,
        grid=(J, nk),
        in_specs=[a_spec, b_spec],
        out_specs=[o_spec],
        out_shape=[jax.ShapeDtypeStruct(o_shape, out_dtype)],
        args=(a, b),
        scratch=[pltpu.VMEM((K, N), F32)],
    )
    return outs[0]


def _seg(xs, lo):
    s_lo = [jnp.sum(jnp.where(lo, x, 0.0), axis=-1, keepdims=True) for x in xs]
    s_hi = [jnp.sum(jnp.where(lo, 0.0, x), axis=-1, keepdims=True) for x in xs]
    return [jnp.where(lo, a, b) for a, b in zip(s_lo, s_hi)]


def _head_norm(xs, gains, lo):
    rs = [lax.rsqrt(s * (1.0 / HEAD_DIM) + EPS) for s in _seg([x * x for x in xs], lo)]
    return [(x * r) * g for x, r, g in zip(xs, rs, gains)], rs


def _head_norm_bwd(xs, rs, gains, dys, lo):
    xns = [x * r for x, r in zip(xs, rs)]
    dxns = [dy * g for dy, g in zip(dys, gains)]
    means = [s * (1.0 / HEAD_DIM) for s in _seg([a * b for a, b in zip(dxns, xns)], lo)]
    dxs = [r * (dxn - xn * m) for r, dxn, xn, m in zip(rs, dxns, xns, means)]
    return dxs, [jnp.sum(dy * xn, axis=0, keepdims=True) for dy, xn in zip(dys, xns)]


def _swap_halves(x):
    return pltpu.roll(x, HEAD_DIM, 1)


def qkv_proj(x, gain, w, qg, kg, *, tm):
    T, D = x.shape
    N = w.shape[1]
    kvw = N_KV_HEADS * HEAD_DIM
    nqt, nkt = D // LANES, kvw // LANES

    def body(x_ref, g_ref, w_ref, qg_ref, kg_ref, qkv_ref, q_ref, kd_ref, vd_ref):
        xt = x_ref[...]
        h = ((xt * _rms(xt)) * g_ref[...]).astype(BF16)
        qkv = _dot(h, w_ref[...])
        qkv_ref[...] = qkv
        lo = lax.broadcasted_iota(jnp.int32, (1, LANES), 1) < HEAD_DIM
        tiles = [qkv[:, t * LANES : (t + 1) * LANES] for t in range(nqt + nkt)]
        normed, _ = _head_norm(tiles, [qg_ref[...]] * nqt + [kg_ref[...]] * nkt, lo)
        for t in range(nqt):
            q_ref[:, t * LANES : (t + 1) * LANES] = (normed[t] * SCALE).astype(BF16)
        for t in range(nkt):
            kn = normed[nqt + t]
            v = qkv[:, D + kvw + t * LANES : D + kvw + (t + 1) * LANES]
            for src, dst in ((kn, kd_ref), (v, vd_ref)):
                sw = _swap_halves(src)
                dst[:, 2 * t * LANES : (2 * t + 1) * LANES] = jnp.where(lo, src, sw).astype(BF16)
                dst[:, (2 * t + 1) * LANES : (2 * t + 2) * LANES] = jnp.where(lo, sw, src).astype(BF16)

    row = lambda i: (i, 0)
    return pl.pallas_call(
        body,
        name="qkv_proj",
        grid=(T // tm,),
        in_specs=[pl.BlockSpec((tm, D), row), _resident((1, D)), _resident((D, N)), _resident((1, LANES)), _resident((1, LANES))],
        out_specs=[pl.BlockSpec((tm, N), row), pl.BlockSpec((tm, D), row), pl.BlockSpec((tm, 2 * kvw), row), pl.BlockSpec((tm, 2 * kvw), row)],
        out_shape=[
            jax.ShapeDtypeStruct((T, N), F32),
            jax.ShapeDtypeStruct((T, D), BF16),
            jax.ShapeDtypeStruct((T, 2 * kvw), BF16),
            jax.ShapeDtypeStruct((T, 2 * kvw), BF16),
        ],
        compiler_params=_params(1),
    )(x, gain, w, qg, kg)


def _attn_tables(sinks, n_q_heads):
    P = n_q_heads // N_KV_HEADS // 2
    h = jnp.arange(1, n_q_heads + 1, dtype=F32)
    slopes = jnp.exp2(-8.0 * h / n_q_heads).reshape(N_KV_HEADS, P, 1, 2, 1)
    qi = jnp.arange(BLOCK)[:, None]
    kj = jnp.arange(BLOCK)[None, :]
    dist = jnp.where(kj <= qi, qi - kj, qi + BLOCK - kj).astype(F32)
    shape = (N_KV_HEADS, P, BLOCK, 2, BLOCK)
    bias = jnp.broadcast_to(-slopes * dist[None, None, :, None, :], shape)
    sink = jnp.broadcast_to(sinks.astype(F32).reshape(N_KV_HEADS, P, 1, 2, 1), shape)
    return bias.reshape(N_KV_HEADS, P * BLOCK, 2 * BLOCK), sink.reshape(N_KV_HEADS, P * BLOCK, 2 * BLOCK)


def _attn_specs(D, nb):
    kvw2 = 2 * N_KV_HEADS * HEAD_DIM
    cur = lambda b, i: (b * nb + i, 0)
    prev = lambda b, i: (jnp.maximum(b * nb + i - 1, 0), 0)
    return [
        pl.BlockSpec((BLOCK, D), cur),
        pl.BlockSpec((BLOCK, kvw2), cur),
        pl.BlockSpec((BLOCK, kvw2), prev),
        pl.BlockSpec((BLOCK, kvw2), cur),
        pl.BlockSpec((BLOCK, kvw2), prev),
    ]


def _attn_operands(kh, P, lo, q_ref, kc_ref, kp_ref, vc_ref, vp_ref):
    sl = slice(kh * LANES, (kh + 1) * LANES)

    def cat(prev_ref, cur_ref):
        d = jnp.concatenate([prev_ref[:, sl], cur_ref[:, sl]], axis=0)
        z = jnp.zeros_like(d)
        return jnp.concatenate([jnp.where(lo, d, z), jnp.where(lo, z, d)], axis=0)

    qt = jnp.concatenate([q_ref[:, (kh * P + pr) * LANES : (kh * P + pr + 1) * LANES] for pr in range(P)], axis=0)
    return qt, cat(kp_ref, kc_ref), cat(vp_ref, vc_ref)


def _attn_exp(s_all, bias, sink, tri, first):
    out = []
    for par in range(2):
        c0 = 2 * par * BLOCK
        s = jnp.where(tri, s_all[:, c0 + BLOCK : c0 + 2 * BLOCK], jnp.where(first, NEG, s_all[:, c0 : c0 + BLOCK]))
        s = s + bias[:, par * BLOCK : (par + 1) * BLOCK]
        snk = sink[:, par * BLOCK : (par + 1) * BLOCK]
        m = jnp.maximum(jnp.max(s, axis=-1, keepdims=True), snk)
        out.append((jnp.exp(s - m), jnp.exp(snk - m)))
    return out


def _unfold(x, tri):
    z = jnp.zeros_like(x)
    return jnp.concatenate([jnp.where(tri, z, x), jnp.where(tri, x, z)], axis=1)


def _attn_masks(R):
    lane = lax.broadcasted_iota(jnp.int32, (1, LANES), 1)
    row = lax.broadcasted_iota(jnp.int32, (R, BLOCK), 0) & (BLOCK - 1)
    col = lax.broadcasted_iota(jnp.int32, (R, BLOCK), 1)
    return lane, lane < HEAD_DIM, col <= row


def attn_fwd(q16, kd, vd, bias, sink, *, seq, n_seq):
    T, D = q16.shape
    nb = seq // BLOCK
    P = D // HEAD_DIM // N_KV_HEADS // 2
    R = P * BLOCK
    KV = range(N_KV_HEADS)

    def body(q_ref, kc_ref, kp_ref, vc_ref, vp_ref, bias_ref, sink_ref, o_ref):
        first = pl.program_id(1) == 0
        _, lo, tri = _attn_masks(R)
        ops = [_attn_operands(kh, P, lo, q_ref, kc_ref, kp_ref, vc_ref, vp_ref) for kh in KV]
        s_all = [_dot_nt(ops[kh][0], ops[kh][1]) for kh in KV]
        ex = [_attn_exp(s_all[kh], bias_ref[kh], sink_ref[kh], tri, first) for kh in KV]
        den = [[jnp.sum(e, axis=-1, keepdims=True) + es for e, es in ex[kh]] for kh in KV]
        lhs = [jnp.concatenate([_unfold(e, tri) for e, _ in ex[kh]], axis=1).astype(BF16) for kh in KV]
        o = [_dot(lhs[kh], ops[kh][2]) for kh in KV]
        for kh in KV:
            out = o[kh] / jnp.where(lo, den[kh][0], den[kh][1])
            for pr in range(P):
                t = kh * P + pr
                o_ref[:, t * LANES : (t + 1) * LANES] = out[pr * BLOCK : (pr + 1) * BLOCK, :].astype(BF16)

    return pl.pallas_call(
        body,
        name="attn_fwd",
        grid=(n_seq, nb),
        in_specs=_attn_specs(D, nb) + [_resident((N_KV_HEADS, R, 2 * BLOCK)), _resident((N_KV_HEADS, R, 2 * BLOCK))],
        out_specs=pl.BlockSpec((BLOCK, D), lambda b, i: (b * nb + i, 0)),
        out_shape=jax.ShapeDtypeStruct((T, D), BF16),
        compiler_params=_params(2),
    )(q16, kd, kd, vd, vd, bias, sink)


def attn_bwd(q16, kd, vd, do, bias, sink, *, seq, n_seq):
    T, D = q16.shape
    kvw2 = 2 * N_KV_HEADS * HEAD_DIM
    nb = seq // BLOCK
    G = D // HEAD_DIM // N_KV_HEADS
    P = G // 2
    R = P * BLOCK
    KV = range(N_KV_HEADS)

    def body(q_ref, kc_ref, kp_ref, vc_ref, vp_ref, do_ref, bias_ref, sink_ref,
             dq_ref, dkc_ref, dkp_ref, dvc_ref, dvp_ref, dsink_ref):
        first = pl.program_id(1) == 0

        @pl.when(jnp.logical_and(pl.program_id(0) == 0, first))
        def _():
            dsink_ref[...] = jnp.zeros_like(dsink_ref)

        lane, lo, tri = _attn_masks(R)
        ops = [_attn_operands(kh, P, lo, q_ref, kc_ref, kp_ref, vc_ref, vp_ref) for kh in KV]
        do16 = [jnp.concatenate([do_ref[:, (kh * P + pr) * LANES : (kh * P + pr + 1) * LANES] for pr in range(P)], axis=0)
                for kh in KV]
        s_all = [_dot_nt(ops[kh][0], ops[kh][1]) for kh in KV]
        dp_all = [_dot_nt(do16[kh], ops[kh][2]) for kh in KV]
        ex = [_attn_exp(s_all[kh], bias_ref[kh], sink_ref[kh], tri, first) for kh in KV]
        den = [[jnp.sum(e, axis=-1, keepdims=True) for e, _ in ex[kh]] for kh in KV]
        dsink = jnp.zeros((1, LANES), F32)
        pf, dsf = [], []
        for kh in KV:
            ps_, ds_ = [], []
            for par in range(2):
                e, es = ex[kh][par]
                inv = 1.0 / (den[kh][par] + es)
                p = e * inv
                c0 = 2 * par * BLOCK
                dp = jnp.where(tri, dp_all[kh][:, c0 + BLOCK : c0 + 2 * BLOCK], dp_all[kh][:, c0 : c0 + BLOCK])
                delta = jnp.sum(p * dp, axis=-1, keepdims=True)
                ds_.append(_unfold(p * (dp - delta), tri))
                ps_.append(_unfold(p, tri))
                dsr = -((es * inv) * delta)
                for pr in range(P):
                    hq = kh * G + 2 * pr + par
                    tot = jnp.sum(dsr[pr * BLOCK : (pr + 1) * BLOCK, :], axis=0, keepdims=True)
                    dsink = dsink + jnp.where(lane == hq, tot, 0.0)
            pf.append(jnp.concatenate(ps_, axis=1).astype(BF16))
            dsf.append(jnp.concatenate(ds_, axis=1).astype(BF16))
        dq = [_dot(dsf[kh], ops[kh][1]) for kh in KV]
        dk = [_dot_tn(dsf[kh], ops[kh][0]) for kh in KV]
        dv = [_dot_tn(pf[kh], do16[kh]) for kh in KV]
        dsink_ref[0:1, :] += dsink
        for kh in KV:
            sl = slice(kh * LANES, (kh + 1) * LANES)
            for pr in range(P):
                t = kh * P + pr
                dq_ref[:, t * LANES : (t + 1) * LANES] = dq[kh][pr * BLOCK : (pr + 1) * BLOCK, :]
            for full, prev_ref, cur_ref in ((dk[kh], dkp_ref, dkc_ref), (dv[kh], dvp_ref, dvc_ref)):
                dup = jnp.where(lo, full[: 2 * BLOCK, :], full[2 * BLOCK :, :])
                prev_ref[:, sl] = dup[:BLOCK, :].astype(BF16)
                cur_ref[:, sl] = dup[BLOCK:, :].astype(BF16)

    cur = lambda b, i: (b * nb + i, 0)
    kv_spec = pl.BlockSpec((BLOCK, kvw2), cur)
    kv_shape = jax.ShapeDtypeStruct((T, kvw2), BF16)
    return pl.pallas_call(
        body,
        name="attn_bwd",
        grid=(n_seq, nb),
        in_specs=_attn_specs(D, nb)
        + [pl.BlockSpec((BLOCK, D), cur), _resident((N_KV_HEADS, R, 2 * BLOCK)), _resident((N_KV_HEADS, R, 2 * BLOCK))],
        out_specs=[pl.BlockSpec((BLOCK, D), cur), kv_spec, kv_spec, kv_spec, kv_spec, pl.BlockSpec((8, LANES), lambda b, i: (0, 0))],
        out_shape=[jax.ShapeDtypeStruct((T, D), F32), kv_shape, kv_shape, kv_shape, kv_shape, jax.ShapeDtypeStruct((8, LANES), F32)],
        compiler_params=_params(2),
    )(q16, kd, kd, vd, vd, do, bias, sink)


def qkv_bwd(dq, dkc, dkp, dvc, dvp, qkv, dres, x, gain, w_qkv, qg, kg, *, seq):
    T, D = x.shape
    kvw2 = dkc.shape[1]
    kvw = kvw2 // 2
    nqt, nkt = D // LANES, kvw // LANES
    nb = seq // BLOCK
    tm = 2 * BLOCK
    n = T // tm

    def body(dq_ref, dkc_ref, dkpa_ref, dkpb_ref, dvc_ref, dvpa_ref, dvpb_ref, qkv_ref, dres_ref, x_ref, g_ref, w_ref,
             qg_ref, kg_ref, dx_ref, dqkv_ref, h_ref, dg_ref, hg_ref):
        i = pl.program_id(0)

        @pl.when(i == 0)
        def _():
            dg_ref[...] = jnp.zeros_like(dg_ref)
            hg_ref[...] = jnp.zeros_like(hg_ref)

        lo = lax.broadcasted_iota(jnp.int32, (1, LANES), 1) < HEAD_DIM
        last = (2 * i + 1) % nb == nb - 1
        up = lambda ref: ref[...].astype(F32)
        dkd = up(dkc_ref) + jnp.concatenate([up(dkpa_ref), jnp.where(last, 0.0, up(dkpb_ref))], axis=0)
        dvd = up(dvc_ref) + jnp.concatenate([up(dvpa_ref), jnp.where(last, 0.0, up(dvpb_ref))], axis=0)

        def undup(d, t):
            a, b = d[:, 2 * t * LANES : (2 * t + 1) * LANES], d[:, (2 * t + 1) * LANES : (2 * t + 2) * LANES]
            return jnp.where(lo, a + _swap_halves(a), b + _swap_halves(b))

        tiles = [qkv_ref[:, t * LANES : (t + 1) * LANES] for t in range(nqt + nkt)]
        gains = [qg_ref[...]] * nqt + [kg_ref[...]] * nkt
        dys = [dq_ref[:, t * LANES : (t + 1) * LANES] * SCALE for t in range(nqt)] + [undup(dkd, t) for t in range(nkt)]
        _, rs = _head_norm(tiles, gains, lo)
        dxs, dgs = _head_norm_bwd(tiles, rs, gains, dys, lo)
        for t in range(nqt + nkt):
            dqkv_ref[:, t * LANES : (t + 1) * LANES] = dxs[t].astype(BF16)
        for t in range(nkt):
            dqkv_ref[:, D + kvw + t * LANES : D + kvw + (t + 1) * LANES] = undup(dvd, t).astype(BF16)
        hg_ref[0:1, :] += functools.reduce(lambda a, b: a + b, dgs[:nqt])
        hg_ref[1:2, :] += functools.reduce(lambda a, b: a + b, dgs[nqt:])
        dh = _dot_nt(dqkv_ref[...], w_ref[...])
        xt = x_ref[...]
        r = _rms(xt)
        gn = g_ref[...]
        h_ref[...] = ((xt * r) * gn).astype(BF16)
        dx, dgn = _rms_bwd(xt, r, gn, dh)
        dg_ref[0:1, :] += dgn
        dx_ref[...] = dres_ref[...] + dx

    row = lambda i: (i, 0)
    nxt_a = pl.BlockSpec((BLOCK, kvw2), lambda i: (2 * i + 1, 0))
    nxt_b = pl.BlockSpec((BLOCK, kvw2), lambda i: (jnp.minimum(2 * i + 2, 2 * n - 1), 0))
    return pl.pallas_call(
        body,
        name="qkv_bwd",
        grid=(n,),
        in_specs=[
            pl.BlockSpec((tm, D), row),
            pl.BlockSpec((tm, kvw2), row),
            nxt_a,
            nxt_b,
            pl.BlockSpec((tm, kvw2), row),
            nxt_a,
            nxt_b,
            pl.BlockSpec((tm, D + kvw2), row),
            pl.BlockSpec((tm, D), row),
            pl.BlockSpec((tm, D), row),
            _resident((1, D)),
            _resident((D, D + kvw2)),
            _resident((1, LANES)),
            _resident((1, LANES)),
        ],
        out_specs=[
            pl.BlockSpec((tm, D), row),
            pl.BlockSpec((tm, D + kvw2), row),
            pl.BlockSpec((tm, D), row),
            pl.BlockSpec((8, D), lambda i: (0, 0)),
            pl.BlockSpec((8, LANES), lambda i: (0, 0)),
        ],
        out_shape=[
            jax.ShapeDtypeStruct((T, D), F32),
            jax.ShapeDtypeStruct((T, D + kvw2), BF16),
            jax.ShapeDtypeStruct((T, D), BF16),
            jax.ShapeDtypeStruct((8, D), F32),
            jax.ShapeDtypeStruct((8, LANES), F32),
        ],
        compiler_params=_params(1),
    )(dq, dkc, dkp, dkp, dvc, dvp, dvp, qkv, dres, x, gain, w_qkv, qg, kg)


def local_step(x, target, gains, w, *, seq, tm=256, tm_ffn=256, tm_conv=512, tk=2048, shards=None, ex=None):
    T, D = x.shape
    n_seq = T // seq
    nm, nf, qgain, kgain, sinks = gains
    H = D // HEAD_DIM
    tk, tk_long = min(tk, T), min(2 * tk, T)
    qg2, kg2 = jnp.tile(qgain, (1, 2)), jnp.tile(kgain, (1, 2))
    bias, sinkcol = _attn_tables(sinks, H)

    dist = shards is not None
    w = dict(w)

    plan = _Gather([shards["w_gu"][0], shards["w_d"][0]]) if dist else None
    (x1, bcx, y_conv, z16), got = conv_fwd(x, nm[0:1], w["w_in"], w["cw"], w["w_out"], seq=seq, tm=tm_conv, plan=plan)
    if dist:
        w["w_gu"], w["w_d"] = [cols_from_shards(got[0]), None], [got[1].reshape(-1, D), None]
    plan = _Gather([shards["w_qkv"], shards["w_o"], shards["w_gu"][1], shards["w_d"][1]]) if dist else None
    (x2, gu0), got = ffn_fwd(x1, nf[0:1], w["w_gu"][0], w["w_d"][0], tm=2 * tm_ffn, plan=plan)
    if dist:
        w["w_qkv"], w["w_o"] = cols_from_shards(got[0]), got[1].reshape(D, D)
        w["w_gu"][1], w["w_d"][1] = cols_from_shards(got[2]), got[3].reshape(-1, D)
    qkv, q16, kd, vd = qkv_proj(x2, nm[1:2], w["w_qkv"], qg2, kg2, tm=tm_conv)
    ao = attn_fwd(q16, kd, vd, bias, sinkcol, seq=seq, n_seq=n_seq)
    (x3, dx4, gu1, sse), _ = ffn_fwd(x2, nf[1:2], w["w_gu"][1], w["w_d"][1], tm=tm_ffn, attn=(ao, w["w_o"]), target=target)

    by_dest = lambda a: a.reshape(N_DEV, -1, a.shape[-1])
    gu_cols = 2 * MXU_TILE

    def send(name, *entries):
        if ex is None:
            return None
        items = [(a, False, key, (N_DEV,) + (() if layers is None else (layers,)) + a.shape[1:], layer)
                 for a, key, layer, layers in entries]
        return ex.start(items, name=name)

    (dx3, a16, dgu, h16, d16, dnf1, dx3_16, dao), _ = ffn_bwd(
        dx4, x3, nf[1:2], gu1, w["w_gu"][1], w["w_d"][1], tm=tm, w_o=w["w_o"])
    g_gu1 = shards_from_cols(wgrad(h16, dgu, name="wgrad_gu1", b_cols=gu_cols, flat=True, tk=tk_long))
    g_d1 = by_dest(wgrad(a16, d16, name="wgrad_d1", a_cols=a16.shape[1] // 2, tk=tk))
    tok = send("exchange_ffn1", (g_gu1, "w_gu", 1, 2), (g_d1, "w_d", 1, 2))
    g_o = by_dest(wgrad(ao, dx3_16, name="wgrad_o", tk=tk, after=tok))
    dq, dkc, dkp, dvc, dvp, dsinks = attn_bwd(q16, kd, vd, dao, bias, sinkcol, seq=seq, n_seq=n_seq)
    dx2, dqkv16, h16, dnm1, dgains = qkv_bwd(dq, dkc, dkp, dvc, dvp, qkv, dx3, x2, nm[1:2], w["w_qkv"], qg2, kg2, seq=seq)
    g_qkv = shards_from_cols(wgrad(h16, dqkv16, name="wgrad_qkv", tk=tk)[0])
    tok = send("exchange_attn", (g_o, "w_o", None, None), (g_qkv, "w_qkv", None, None))
    (dx1, a16, dgu, h16, d16, dnf0), _ = ffn_bwd(dx2, x1, nf[0:1], gu0, w["w_gu"][0], w["w_d"][0], tm=tm, after=tok)
    g_gu0 = shards_from_cols(wgrad(h16, dgu, name="wgrad_gu0", b_cols=gu_cols, flat=True, tk=tk_long))
    tok = send("exchange_gu0", (g_gu0, "w_gu", 0, 2))
    g_d0 = by_dest(wgrad(a16, d16, name="wgrad_d0", a_cols=a16.shape[1] // 2, tk=tk, after=tok))
    tok = send("exchange_d0", (g_d0, "w_d", 0, 2))
    (gx, dbcx, h16, d16, dcw, dnm0), _ = conv_bwd(
        dx1, x, nm[0:1], bcx, y_conv, w["cw"], w["w_in"], w["w_out"], seq=seq, tm=tm_conv, after=tok)
    g_out = by_dest(wgrad(z16, d16, name="wgrad_out", tk=tk))
    g_cw = dcw[0:3].reshape(3, N_DEV, D // N_DEV).transpose(1, 0, 2)
    tok = send("exchange_out", (g_out, "w_out", None, None), (g_cw, "cw", None, None))
    g_in = wgrad(h16, dbcx, name="wgrad_in", b_cols=3 * D // N_DEV, group=2, tk=tk_long, after=tok)
    g = dict(w_in=g_in, cw=g_cw, w_out=g_out, w_o=g_o, w_qkv=g_qkv, w_gu=[g_gu0, g_gu1], w_d=[g_d0, g_d1])
    small = dict(nm0=dnm0, nm1=dnm1, nf0=dnf0, nf1=dnf1, gains=dgains, sinks=dsinks)
    return sse, gx, g, small


def _adamw_math(g, w, m, v):
    m = ADAM_B1 * m + (1.0 - ADAM_B1) * g
    v = ADAM_B2 * v + (1.0 - ADAM_B2) * (g * g)
    m_hat = m / (1.0 - ADAM_B1 ** ADAM_STEP)
    v_hat = v / (1.0 - ADAM_B2 ** ADAM_STEP)
    delta = -ADAM_LR * (m_hat / (jnp.sqrt(v_hat) + ADAM_EPS) + ADAM_WD * w)
    return delta, m, v


def adamw(parts, owns, w, m, v, *, name, after=None):
    n, LR, C = parts.shape
    L = len(owns)
    R = LR // L
    tr = R
    for cand in (256, 128, 88, 64, 32, 16, 8):
        if R > cand and R % cand == 0:
            tr = cand
            break
    per_layer = R // tr
    extra = [] if after is None else [after]

    def body(me_ref, p_ref, *rest):
        own_refs, (w_ref, m_ref, v_ref) = rest[:L], rest[L : L + 3]
        g_ref, d_ref, mo_ref, vo_ref = rest[L + 3 + len(extra) :]
        layer = pl.program_id(0) // per_layer
        mine = own_refs[0][...].astype(F32)
        for j in range(1, L):
            mine = jnp.where(layer == j, own_refs[j][...].astype(F32), mine)
        g = None
        for s in range(n):
            share = jnp.where(me_ref[0] == s, mine, p_ref[s].astype(F32))
            g = share if g is None else g + share
        g_ref[...] = g
        d_ref[...], mo_ref[...], vo_ref[...] = _adamw_math(g, w_ref[...], m_ref[...], v_ref[...])

    blk = pl.BlockSpec((tr, C), lambda i, me: (i, 0))
    own_specs = [pl.BlockSpec((None, tr, C), lambda i, me: (me[0], i % per_layer, 0)) if o.ndim == 3
                 else pl.BlockSpec((tr, C), lambda i, me: (i % per_layer, 0)) for o in owns]
    me = (4 * lax.axis_index("x") + 2 * lax.axis_index("y") + lax.axis_index("c")).astype(jnp.int32).reshape(1)
    return pl.pallas_call(
        body,
        name=name,
        grid_spec=pltpu.PrefetchScalarGridSpec(
            num_scalar_prefetch=1,
            grid=(LR // tr,),
            in_specs=[pl.BlockSpec((n, tr, C), lambda i, me: (0, i, 0))] + own_specs + [blk, blk, blk] + _any_specs(len(extra)),
            out_specs=[blk] * 4,
        ),
        out_shape=[jax.ShapeDtypeStruct((LR, C), F32)] * 4,
        compiler_params=_params(1),
    )(me, parts, *owns, w, m, v, *extra)


def pack_small(small, sse, D):
    W = max(D, 2 * LANES)

    def body(nm0, nm1, nf0, nf1, gains, sinks, sse_ref, o_ref):
        o_ref[...] = jnp.zeros_like(o_ref)
        o_ref[0:1, :D] = nm0[0:1, :]
        o_ref[1:2, :D] = nm1[0:1, :]
        o_ref[2:3, :D] = nf0[0:1, :]
        o_ref[3:4, :D] = nf1[0:1, :]
        gq = gains[0:1, :] + pltpu.roll(gains[0:1, :], HEAD_DIM, 1)
        gk = gains[1:2, :] + pltpu.roll(gains[1:2, :], HEAD_DIM, 1)
        lane = lax.broadcasted_iota(jnp.int32, (1, LANES), 1)
        o_ref[4:5, :LANES] = jnp.where(lane < HEAD_DIM, gq, gk)
        o_ref[4:5, LANES : 2 * LANES] = sinks[0:1, :]
        o_ref[5:6, :LANES] = sse_ref[0:1, :] * (0.5 / D)

    return pl.pallas_call(
        body,
        name="pack_small",
        out_shape=jax.ShapeDtypeStruct((8, W), F32),
    )(small["nm0"], small["nm1"], small["nf0"], small["nf1"], small["gains"], small["sinks"], sse)


def _pack_small_params(nm, nf, qg, kg, sk, D):
    W = max(D, 2 * LANES)
    row4 = jnp.concatenate([qg.reshape(-1), kg.reshape(-1), jnp.zeros((LANES - 2 * HEAD_DIM,), F32), sk.reshape(-1)])
    row4 = jnp.pad(row4, (0, W - row4.shape[0]))
    rows = [jnp.pad(r, (0, W - D)) for r in (nm[0], nm[1], nf[0], nf[1])] + [row4]
    return jnp.concatenate([jnp.stack(rows), jnp.zeros((3, W), F32)], axis=0)


def _unpack_small(a, D, H):
    nm = a[0:2, :D]
    nf = a[2:4, :D]
    qg = a[4:5, 0:HEAD_DIM]
    kg = a[4:5, HEAD_DIM : 2 * HEAD_DIM]
    sk = a[4:5, LANES : LANES + H]
    return qg, kg, sk, nm, nf


def kernel(x, conv_w_in, conv_w, conv_w_out, attn_w_qkv, attn_q_gain, attn_k_gain, attn_sinks, attn_w_o, norm_mixer, norm_ffn, ffn_w_gate_up, ffn_w_down, loss_target, m_conv_w_in, m_conv_w, m_conv_w_out, m_attn_w_qkv, m_attn_q_gain, m_attn_k_gain, m_attn_sinks, m_attn_w_o, m_norm_mixer, m_norm_ffn, m_ffn_w_gate_up, m_ffn_w_down, v_conv_w_in, v_conv_w, v_conv_w_out, v_attn_w_qkv, v_attn_q_gain, v_attn_k_gain, v_attn_sinks, v_attn_w_o, v_norm_mixer, v_norm_ffn, v_ffn_w_gate_up, v_ffn_w_down):
    n_seq, seq, D = x.shape
    T = n_seq * seq
    H = D // HEAD_DIM
    L = ffn_w_gate_up.shape[0]

    full = run_plan(_Gather([conv_w_in[0].astype(BF16), conv_w[0], conv_w_out[0].astype(BF16)]), name="gather_conv_weights")
    w = dict(w_in=cols_from_shards(full[0]), cw=full[1].transpose(1, 0, 2).reshape(3, D),
             w_out=full[2].reshape(D, D))
    shards = dict(w_gu=[ffn_w_gate_up[l].astype(BF16) for l in range(L)], w_d=[ffn_w_down[l].astype(BF16) for l in range(L)],
                  w_qkv=attn_w_qkv[0].astype(BF16), w_o=attn_w_o[0].astype(BF16))
    gains = (norm_mixer, norm_ffn, attn_q_gain, attn_k_gain, attn_sinks)
    ex = Exchange()
    sse, gx, g, small = local_step(x.reshape(T, D), loss_target.reshape(T, D), gains, w, seq=seq, shards=shards, ex=ex)
    zones, own = ex.wait([g["w_in"]], name="exchange_wait")

    packed = pack_small(small, sse, D)
    token = ex.start([(g["w_in"], False, "w_in", g["w_in"].shape, None),
                      (packed, True, "small", (N_DEV,) + packed.shape, None)], name="exchange_last")

    def flat(a):
        return a.reshape(-1, a.shape[-1])

    big = [conv_w_in, conv_w, conv_w_out, attn_w_qkv, attn_w_o, ffn_w_gate_up, ffn_w_down]
    big_m = [m_conv_w_in, m_conv_w, m_conv_w_out, m_attn_w_qkv, m_attn_w_o, m_ffn_w_gate_up, m_ffn_w_down]
    big_v = [v_conv_w_in, v_conv_w, v_conv_w_out, v_attn_w_qkv, v_attn_w_o, v_ffn_w_gate_up, v_ffn_w_down]
    keys = ["w_in", "cw", "w_out", "w_qkv", "w_o", "w_gu", "w_d"]

    def update(b, zones, own, after=None):
        zone = zones[keys[b]]
        parts = zone.reshape(N_DEV, -1, zone.shape[-1])
        layers = [None] if zone.ndim == 3 else range(zone.shape[1])
        outs = adamw(parts, [own[(keys[b], l)] for l in layers], flat(big[b]), flat(big_m[b]), flat(big_v[b]),
                     name="adamw_" + keys[b], after=after)
        return [o.reshape(big[b].shape) for o in outs]

    res = [None] + [update(b, zones, own, after=token) for b in range(1, 7)]
    zones, own = ex.wait([r[0] for r in res[1:]], name="exchange_last_wait")
    res[0] = update(0, zones, own)
    sw = _pack_small_params(norm_mixer, norm_ffn, attn_q_gain, attn_k_gain, attn_sinks, D)
    sm = _pack_small_params(m_norm_mixer, m_norm_ffn, m_attn_q_gain, m_attn_k_gain, m_attn_sinks, D)
    sv = _pack_small_params(v_norm_mixer, v_norm_ffn, v_attn_q_gain, v_attn_k_gain, v_attn_sinks, D)
    souts = adamw(zones["small"], [own[("small", None)]], sw, sm, sv, name="adamw_small")
    sres = [_unpack_small(o, D, H) for o in souts]
    loss = souts[0][5, 0]

    def ordered(i):
        r, s = [r[i] for r in res], sres[i]
        return [r[0], r[1], r[2], r[3], s[0], s[1], s[2], r[4], s[3], s[4], r[5], r[6]]

    return (loss, gx.reshape(n_seq, seq, D), *ordered(0), *ordered(1), *ordered(2), *ordered(3))
```

```python
import functools
import math

import jax
import jax.numpy as jnp
from jax import lax
from jax.experimental import pallas as pl
from jax.experimental.pallas import tpu as pltpu

F32 = jnp.float32
BF16 = jnp.bfloat16

EPS = 1e-6
HEAD_DIM = 64
N_KV_HEADS = 4
BLOCK = 128
LANES = 128
N_DEV = 8
NEG = -1e30
SCALE = 1.0 / math.sqrt(HEAD_DIM)

ADAM_LR = 0.001
ADAM_B1 = 0.9
ADAM_B2 = 0.999
ADAM_EPS = 1e-08
ADAM_WD = 0.01
ADAM_STEP = 10

V7X_VMEM_BYTES = 64 * 1024 * 1024
VMEM_LIMIT = V7X_VMEM_BYTES - 2 * 1024 * 1024
MESH = pl.DeviceIdType.MESH

_NT = (((1,), (1,)), ((), ()))
_TN = (((0,), (0,)), ((), ()))


def _params(n_grid):
    return pltpu.CompilerParams(dimension_semantics=("arbitrary",) * n_grid, vmem_limit_bytes=VMEM_LIMIT)


def _resident(shape):
    nd = len(shape)
    return pl.BlockSpec(shape, lambda *_: (0,) * nd, pipeline_mode=pl.Buffered(1))


def _rms(x):
    return lax.rsqrt(jnp.mean(x * x, axis=-1, keepdims=True) + EPS)


def _rms_bwd(x, r, gain, dh):
    xn = x * r
    dxn = dh * gain
    dx = r * (dxn - xn * jnp.mean(dxn * xn, axis=-1, keepdims=True))
    return dx, jnp.sum(dh * xn, axis=0, keepdims=True)


def _dot(a, b):
    return jnp.dot(a, b, preferred_element_type=F32)


def _dot_nt(a, b):
    return lax.dot_general(a, b, _NT, preferred_element_type=F32)


def _dot_tn(a, b):
    return lax.dot_general(a, b, _TN, preferred_element_type=F32)


def _place():
    return lax.axis_index("x"), lax.axis_index("y"), lax.axis_index("c")


def _flip(v, bit):
    return 1 - v if bit else v


def _slot(px, py, pc):
    return 4 * px + 2 * py + pc


class _Gather:
    def __init__(self, shards):
        nt = len(shards)
        self.nt = nt
        self.inputs = list(shards)
        self.out_shapes = [jax.ShapeDtypeStruct((N_DEV,) + s.shape, s.dtype) for s in shards]
        self.scratch = [pltpu.SemaphoreType.DMA((nt, 10)), pltpu.SemaphoreType.DMA((nt, 10)), pltpu.SemaphoreType.DMA((nt,))]
        self.aliases = {}
        self.split = []
        for s in shards:
            rows, tile = s.shape[0], 16 if s.dtype == BF16 else 8
            self.split.append(rows // 2 if rows % (2 * tile) == 0 else rows)

    def phases(self, total):
        assert total >= 8
        return [(0, self.start), (total // 2, self.second), (total - 3, self.forward), (total - 1, self.finish)]

    def _copies(self, ins, outs, sems):
        send_sems, recv_sems, loc_sems = sems
        x, y, c = _place()
        xn, yn, sib = (1 - x, y, c), (x, 1 - y, c), (x, y, 1 - c)
        i_me, i_xn, i_yn, i_dn = _slot(x, y, c), _slot(1 - x, y, c), _slot(x, 1 - y, c), _slot(1 - x, 1 - y, c)
        j_me, j_xn, j_yn, j_dn = _slot(x, y, 1 - c), _slot(1 - x, y, 1 - c), _slot(x, 1 - y, 1 - c), _slot(1 - x, 1 - y, 1 - c)
        local, start, need1, second, need2, forward, need3 = [], [], [], [], [], [], []
        for t in range(self.nt):
            o, rows, h = outs[t], self.inputs[t].shape[0], self.split[t]
            lo = pl.ds(0, h)
            hi = pl.ds(h, rows - h) if h < rows else None

            def rc(k, src, dst, to, t=t):
                return pltpu.make_async_remote_copy(
                    src_ref=src, dst_ref=dst, send_sem=send_sems.at[t, k], recv_sem=recv_sems.at[t, k], device_id=to,
                    device_id_type=MESH)

            def landed(k, slot, part, frm):
                ref = o.at[slot] if part is None else o.at[slot, part]
                return rc(k, ref, ref, frm)

            local.append(pltpu.make_async_copy(ins[t], o.at[i_me], loc_sems.at[t]))
            start += [rc(0, ins[t], o.at[i_me], sib), rc(1, ins[t].at[lo], o.at[i_me, lo], xn),
                      rc(4, ins[t].at[lo], o.at[i_me, lo], yn)]
            need1.append(landed(1, i_xn, lo, xn))
            second.append(rc(3, o.at[i_xn, lo], o.at[i_xn, lo], yn))
            need2 += [landed(4, i_yn, lo, yn), landed(3, i_dn, lo, yn)]
            if hi is not None:
                start += [rc(2, ins[t].at[hi], o.at[i_me, hi], yn), rc(6, ins[t].at[hi], o.at[i_me, hi], xn)]
                need1.append(landed(2, i_yn, hi, yn))
                second.append(rc(5, o.at[i_yn, hi], o.at[i_yn, hi], xn))
                need2 += [landed(6, i_xn, hi, xn), landed(5, i_dn, hi, xn)]
            forward += [rc(7, o.at[i_xn], o.at[i_xn], sib), rc(8, o.at[i_yn], o.at[i_yn], sib), rc(9, o.at[i_dn], o.at[i_dn], sib)]
            need3 += [landed(0, j_me, None, sib), landed(7, j_xn, None, sib), landed(8, j_yn, None, sib), landed(9, j_dn, None, sib)]
        return local, start, need1, second, need2, forward, need3

    def start(self, ins, outs, sems):
        local, start, *_ = self._copies(ins, outs, sems)
        for cp in local + start:
            cp.start()

    def second(self, ins, outs, sems):
        _, _, need1, second, *_ = self._copies(ins, outs, sems)
        for cp in need1:
            cp.wait_recv()
        for cp in second:
            cp.start()

    def forward(self, ins, outs, sems):
        _, _, _, _, need2, forward, _ = self._copies(ins, outs, sems)
        for cp in need2:
            cp.wait_recv()
        for cp in forward:
            cp.start()

    def finish(self, ins, outs, sems):
        local, start, _, second, _, forward, need3 = self._copies(ins, outs, sems)
        for cp in need3:
            cp.wait_recv()
        for cp in start + second + forward:
            cp.wait_send()
        for cp in local:
            cp.wait()


def _any_specs(n):
    return [pl.BlockSpec(memory_space=pl.ANY)] * n


def run_plan(plan, *, name):
    def body(*refs):
        n_in, n_out = len(plan.inputs), len(plan.out_shapes)
        ins, outs, sems = refs[:n_in], refs[n_in : n_in + n_out], refs[n_in + n_out :]
        for _, phase in plan.phases(8):
            phase(ins, outs, sems)

    return pl.pallas_call(
        body,
        name=name,
        in_specs=_any_specs(len(plan.inputs)),
        out_specs=_any_specs(len(plan.out_shapes)),
        out_shape=plan.out_shapes,
        scratch_shapes=plan.scratch,
        input_output_aliases=plan.aliases,
    )(*plan.inputs)


_HBM = pl.BlockSpec(memory_space=pltpu.HBM)
_SEM = pl.BlockSpec(memory_space=pltpu.SEMAPHORE)
_DATAFLOW = pltpu.SideEffectType.DATAFLOW_SIDE_EFFECTING


class Exchange:
    def __init__(self):
        self.zones = {}
        self.pending = []
        self.sources = []

    def start(self, items, *, name):
        nt = len(items)
        keys = list(dict.fromkeys(it[2] for it in items))
        for a, _, key, shape, _ in items:
            if key not in self.zones:
                self.zones[key] = lax.empty(shape, a.dtype)
        nz = len(keys)

        def body(*refs):
            ins, zones, sems, token = refs[:nt], refs[nt : nt + nz], refs[nt + nz : nt + nz + 2 * nt], refs[-1]
            x, y, c = _place()
            me = _slot(x, y, c)
            for k in range(1, N_DEV):
                px, py, pc = _flip(x, (k >> 2) & 1), _flip(y, (k >> 1) & 1), _flip(c, k & 1)
                for t, (_, whole, key, _, layer) in enumerate(items):
                    zone = zones[keys.index(key)]
                    pltpu.make_async_remote_copy(
                        src_ref=ins[t] if whole else ins[t].at[_slot(px, py, pc)],
                        dst_ref=zone.at[me] if layer is None else zone.at[me, layer],
                        send_sem=sems[2 * t], recv_sem=sems[2 * t + 1], device_id=(px, py, pc), device_id_type=MESH).start()
            token[...] = jnp.zeros_like(token)

        bufs = [pltpu.with_memory_space_constraint(b, pltpu.HBM) for b in [it[0] for it in items] + [self.zones[k] for k in keys]]
        outs = pl.pallas_call(
            body,
            name=name,
            in_specs=[_HBM] * (nt + nz),
            out_specs=[_SEM] * (2 * nt) + [_HBM] * (nt + nz) + [pl.BlockSpec(memory_space=pltpu.VMEM)],
            out_shape=[pltpu.SemaphoreType.DMA(())] * (2 * nt) + [pltpu.HBM(b.shape, b.dtype) for b in bufs]
            + [jax.ShapeDtypeStruct((8, LANES), F32)],
            input_output_aliases={i: 2 * nt + i for i in range(nt + nz)},
            compiler_params=pltpu.CompilerParams(has_side_effects=_DATAFLOW),
        )(*bufs)
        for t, (_, _, key, _, layer) in enumerate(items):
            self.pending.append((outs[2 * t], outs[2 * t + 1], key, layer))
        self.sources += [((it[2], it[4]), a) for it, a in zip(items, outs[2 * nt : 3 * nt])]
        for i, key in enumerate(keys):
            self.zones[key] = outs[3 * nt + i]
        return outs[-1]

    def wait(self, after, *, name):
        pending, keys = self.pending, list(self.zones)
        names, sources = [n for n, _ in self.sources], [a for _, a in self.sources]
        ns, nz, npend = len(sources), len(keys), len(pending)
        self.pending, self.sources = [], []

        def body(*refs):
            zones, sems = refs[ns : ns + nz], refs[ns + nz : ns + nz + 2 * npend]
            x, y, c = _place()
            for i, (_, _, key, layer) in enumerate(pending):
                zone = zones[keys.index(key)]
                rows = pl.ds(0, N_DEV - 1)
                seven = zone.at[rows] if layer is None else zone.at[rows, layer]
                pltpu.make_async_remote_copy(
                    src_ref=seven, dst_ref=seven, send_sem=sems[2 * i], recv_sem=sems[2 * i + 1],
                    device_id=(x, y, c), device_id_type=MESH).wait()

        bufs = list(sources) + [self.zones[k] for k in keys]
        flat_sems = [s for p in pending for s in p[:2]]
        outs = pl.pallas_call(
            body,
            name=name,
            in_specs=[_HBM] * (ns + nz) + [_SEM] * (2 * npend) + _any_specs(len(after)),
            out_specs=[_HBM] * (ns + nz),
            out_shape=[pltpu.HBM(b.shape, b.dtype) for b in bufs],
            input_output_aliases={i: i for i in range(ns + nz)},
            compiler_params=pltpu.CompilerParams(has_side_effects=_DATAFLOW),
        )(*bufs, *flat_sems, *after)
        self.zones = {}
        return dict(zip(keys, outs[ns:])), dict(zip(names, outs[:ns]))


def _call(body, *, name, grid, in_specs, out_specs, out_shape, args, scratch=(), plan=None, after=None):
    if after is not None:
        inner, n_real = body, len(in_specs)
        body = lambda *refs: inner(*refs[:n_real], *refs[n_real + 1 :])
        in_specs, args = list(in_specs) + _any_specs(1), list(args) + [after]
    n_in, n_out, n_scr = len(in_specs), len(out_specs), len(scratch)
    if plan is None:
        outs = pl.pallas_call(
            body, name=name, grid=grid, in_specs=in_specs, out_specs=out_specs, out_shape=out_shape,
            scratch_shapes=list(scratch), compiler_params=_params(len(grid)))(*args)
        return outs, None
    c_in, c_out = len(plan.inputs), len(plan.out_shapes)
    phases = plan.phases(math.prod(grid))

    def full(*refs):
        a, refs = refs[:n_in], refs[n_in:]
        ci, refs = refs[:c_in], refs[c_in:]
        o, refs = refs[:n_out], refs[n_out:]
        co, refs = refs[:c_out], refs[c_out:]
        s, cs = refs[:n_scr], refs[n_scr:]
        step = pl.program_id(0)
        for d in range(1, len(grid)):
            step = step * grid[d] + pl.program_id(d)
        for at, phase in phases:
            if at == 0:
                pl.when(step == 0)(functools.partial(phase, ci, co, cs))
        body(*a, *o, *s)
        for at, phase in phases:
            if at > 0:
                pl.when(step == at)(functools.partial(phase, ci, co, cs))

    outs = pl.pallas_call(
        full,
        name=name,
        grid=grid,
        in_specs=list(in_specs) + _any_specs(c_in),
        out_specs=list(out_specs) + _any_specs(c_out),
        out_shape=list(out_shape) + plan.out_shapes,
        scratch_shapes=list(scratch) + plan.scratch,
        input_output_aliases={n_in + i: n_out + t for i, t in plan.aliases.items()},
        compiler_params=_params(len(grid)),
    )(*args, *plan.inputs)
    return outs[:n_out], outs[n_out:]


def _row_tile(R):
    return 256 if R % 256 == 0 else R


def cols_from_shards(a):
    n, R, C = a.shape
    tr = _row_tile(R)

    def body(i_ref, o_ref):
        for s in range(n):
            o_ref[:, s * C : (s + 1) * C] = i_ref[s]

    return pl.pallas_call(
        body,
        name="cols_from_shards",
        grid=(R // tr,),
        in_specs=[pl.BlockSpec((n, tr, C), lambda i: (0, i, 0))],
        out_specs=pl.BlockSpec((tr, n * C), lambda i: (i, 0)),
        out_shape=jax.ShapeDtypeStruct((R, n * C), a.dtype),
        compiler_params=_params(1),
    )(a)


def shards_from_cols(a):
    R, W = a.shape
    C = W // N_DEV
    tr = _row_tile(R)

    def body(i_ref, o_ref):
        for s in range(N_DEV):
            o_ref[s] = i_ref[:, s * C : (s + 1) * C]

    return pl.pallas_call(
        body,
        name="shards_from_cols",
        grid=(R // tr,),
        in_specs=[pl.BlockSpec((tr, W), lambda i: (i, 0))],
        out_specs=pl.BlockSpec((N_DEV, tr, C), lambda i: (0, i, 0)),
        out_shape=jax.ShapeDtypeStruct((N_DEV, R, C), a.dtype),
        compiler_params=_params(1),
    )(a)


def _shift_down(u, prev8, row, n):
    out = pltpu.roll(u, n, 0)
    for k in range(n):
        out = jnp.where(row == k, prev8[8 - n + k : 8 - n + k + 1, :], out)
    return out


def _shift_up(u, next8, row, n, tm):
    out = pltpu.roll(u, tm - n, 0)
    for k in range(n):
        out = jnp.where(row == tm - n + k, next8[k : k + 1, :], out)
    return out


def conv_fwd(x, gain, w_in, cw, w_out, *, seq, tm, plan=None):
    T, D = x.shape
    tps = seq // tm

    def body(x_ref, g_ref, win_ref, cw_ref, wout_ref, x1_ref, bcx_ref, y_ref, z_ref, carry_ref):
        i = pl.program_id(0)

        @pl.when(i % tps == 0)
        def _():
            carry_ref[...] = jnp.zeros_like(carry_ref)

        xt = x_ref[...]
        h = ((xt * _rms(xt)) * g_ref[...]).astype(BF16)
        bcx = _dot(h, win_ref[...])
        bcx_ref[...] = bcx.astype(BF16)
        b, c, xv = bcx[:, :D], bcx[:, D : 2 * D], bcx[:, 2 * D :]
        u = b * xv
        row = lax.broadcasted_iota(jnp.int32, u.shape, 0)
        prev = carry_ref[...]
        u1 = _shift_down(u, prev, row, 1)
        u2 = _shift_down(u, prev, row, 2)
        carry_ref[...] = u[tm - 8 :, :]
        cwv = cw_ref[...]
        y = cwv[0:1, :] * u2 + cwv[1:2, :] * u1 + cwv[2:3, :] * u
        y_ref[...] = y
        z = (c * y).astype(BF16)
        z_ref[...] = z
        x1_ref[...] = xt + _dot(z, wout_ref[...])

    tile = pl.BlockSpec((tm, D), lambda i: (i, 0))
    return _call(
        body,
        plan=plan,
        args=(x, gain, w_in, cw, w_out),
        name="conv_fwd",
        grid=(T // tm,),
        in_specs=[
            pl.BlockSpec((tm, D), lambda i: (i, 0)),
            _resident((1, D)),
            _resident((D, 3 * D)),
            _resident((3, D)),
            _resident((D, D)),
        ],
        out_specs=[tile, pl.BlockSpec((tm, 3 * D), lambda i: (i, 0)), tile, tile],
        out_shape=[jax.ShapeDtypeStruct((T, D), F32), jax.ShapeDtypeStruct((T, 3 * D), BF16),
                   jax.ShapeDtypeStruct((T, D), F32), jax.ShapeDtypeStruct((T, D), BF16)],
        scratch=[pltpu.VMEM((8, D), F32)],
    )


def conv_bwd(dx1, x, gain, bcx, y, cw, w_in, w_out, *, seq, tm, after=None):
    T, D = x.shape
    n = T // tm
    tps = seq // tm

    def body(d_ref, x_ref, g_ref, bcx_ref, y_ref, cw_ref, win_ref, wout_ref,
             gx_ref, dbcx_ref, h_ref, d16_ref, dcw_ref, dg_ref, carry_ref):
        i = pl.program_id(0)
        t = n - 1 - i

        @pl.when(i == 0)
        def _():
            dcw_ref[...] = jnp.zeros_like(dcw_ref)
            dg_ref[...] = jnp.zeros_like(dg_ref)

        @pl.when(t % tps == tps - 1)
        def _():
            carry_ref[...] = jnp.zeros_like(carry_ref)

        d = d_ref[...]
        d16 = d.astype(BF16)
        d16_ref[...] = d16
        dz = _dot_nt(d16, wout_ref[...])
        bcx = bcx_ref[...].astype(F32)
        b, c, xv = bcx[:, :D], bcx[:, D : 2 * D], bcx[:, 2 * D :]
        u = b * xv
        row = lax.broadcasted_iota(jnp.int32, u.shape, 0)
        cwv = cw_ref[...]
        dc = dz * y_ref[...]
        dy = dz * c
        nxt = carry_ref[...]
        dy1 = _shift_up(dy, nxt, row, 1, tm)
        dy2 = _shift_up(dy, nxt, row, 2, tm)
        carry_ref[...] = dy[0:8, :]
        dcw_ref[0:1, :] += jnp.sum(dy2 * u, axis=0, keepdims=True)
        dcw_ref[1:2, :] += jnp.sum(dy1 * u, axis=0, keepdims=True)
        dcw_ref[2:3, :] += jnp.sum(dy * u, axis=0, keepdims=True)
        du = cwv[2:3, :] * dy + cwv[1:2, :] * dy1 + cwv[0:1, :] * dy2
        dbcx_ref[:, :D] = (du * xv).astype(BF16)
        dbcx_ref[:, D : 2 * D] = dc.astype(BF16)
        dbcx_ref[:, 2 * D :] = (du * b).astype(BF16)
        dh = _dot_nt(dbcx_ref[...], win_ref[...])
        xt = x_ref[...]
        r = _rms(xt)
        gn = g_ref[...]
        h_ref[...] = ((xt * r) * gn).astype(BF16)
        dx, dgn = _rms_bwd(xt, r, gn, dh)
        dg_ref[0:1, :] += dgn
        gx_ref[...] = d + dx

    rev = lambda i: (n - 1 - i, 0)
    return _call(
        body,
        after=after,
        args=(dx1, x, gain, bcx, y, cw, w_in, w_out),
        name="conv_bwd",
        grid=(n,),
        in_specs=[
            pl.BlockSpec((tm, D), rev),
            pl.BlockSpec((tm, D), rev),
            _resident((1, D)),
            pl.BlockSpec((tm, 3 * D), rev),
            pl.BlockSpec((tm, D), rev),
            _resident((3, D)),
            _resident((D, 3 * D)),
            _resident((D, D)),
        ],
        out_specs=[
            pl.BlockSpec((tm, D), rev),
            pl.BlockSpec((tm, 3 * D), rev),
            pl.BlockSpec((tm, D), rev),
            pl.BlockSpec((tm, D), rev),
            pl.BlockSpec((8, D), lambda i: (0, 0)),
            pl.BlockSpec((8, D), lambda i: (0, 0)),
        ],
        out_shape=[
            jax.ShapeDtypeStruct((T, D), F32),
            jax.ShapeDtypeStruct((T, 3 * D), BF16),
            jax.ShapeDtypeStruct((T, D), BF16),
            jax.ShapeDtypeStruct((T, D), BF16),
            jax.ShapeDtypeStruct((8, D), F32),
            jax.ShapeDtypeStruct((8, D), F32),
        ],
        scratch=[pltpu.VMEM((8, D), F32)],
    )


MXU_TILE = 256
FFN_CHUNK = 4 * MXU_TILE


def _sigmoid(g):
    return 1.0 / (1.0 + jnp.exp(-g))


def _ffn_chunks(F):
    assert F % MXU_TILE == 0
    return [(s, min(FFN_CHUNK, F - s)) for s in range(0, F, FFN_CHUNK)]


def ffn_fwd(x, gain, w_gu, w_d, *, tm, plan=None, attn=None, target=None):
    T, D = x.shape
    F = w_d.shape[0]
    row = lambda i: (i, 0)
    tile = pl.BlockSpec((tm, D), row)

    def body(*refs):
        refs = list(refs)
        x_ref, g_ref, wgu_ref, wd_ref = refs[:4]
        del refs[:4]
        if attn is not None:
            ao_ref, wo_ref = refs[:2]
            del refs[:2]
        if target is not None:
            t_ref = refs.pop(0)
        if attn is not None:
            xin_ref = refs.pop(0)
        xo_ref, gu_ref = refs[:2]
        xt = x_ref[...]
        if attn is not None:
            xt = xt + _dot(ao_ref[...], wo_ref[...])
            xin_ref[...] = xt
        h = ((xt * _rms(xt)) * g_ref[...]).astype(BF16)
        acc = xt
        for s, n in _ffn_chunks(F):
            g = _dot(h, wgu_ref[:, s : s + n])
            u = _dot(h, wgu_ref[:, F + s : F + s + n])
            gu_ref[:, s : s + n] = g
            gu_ref[:, F + s : F + s + n] = u
            a = ((g * _sigmoid(g)) * u).astype(BF16)
            acc = acc + _dot(a, wd_ref[s : s + n, :])
        if target is None:
            xo_ref[...] = acc
        else:
            s_ref = refs[2]

            @pl.when(pl.program_id(0) == 0)
            def _():
                s_ref[...] = jnp.zeros_like(s_ref)

            e = acc - t_ref[...]
            xo_ref[...] = e * (1.0 / D)
            s_ref[...] += jnp.sum(jnp.sum(e * e, axis=-1, keepdims=True), axis=0, keepdims=True)

    args = [x, gain, w_gu, w_d]
    in_specs = [tile, _resident((1, D)), _resident((D, 2 * F)), _resident((F, D))]
    out_specs = [tile, pl.BlockSpec((tm, 2 * F), row)]
    out_shape = [jax.ShapeDtypeStruct((T, D), F32), jax.ShapeDtypeStruct((T, 2 * F), F32)]
    if attn is not None:
        args += list(attn)
        in_specs += [pl.BlockSpec((tm, attn[0].shape[1]), row), _resident(attn[1].shape)]
        out_specs.insert(0, tile)
        out_shape.insert(0, jax.ShapeDtypeStruct((T, D), F32))
    if target is not None:
        args.append(target)
        in_specs.append(tile)
        out_specs.append(pl.BlockSpec((8, LANES), lambda i: (0, 0)))
        out_shape.append(jax.ShapeDtypeStruct((8, LANES), F32))
    return _call(body, plan=plan, args=args, name="ffn_fwd", grid=(T // tm,), in_specs=in_specs, out_specs=out_specs,
                 out_shape=out_shape)


def ffn_bwd(dxo, x, gain, gu, w_gu, w_d, *, tm, after=None, w_o=None):
    T, D = x.shape
    F = w_d.shape[0]

    def body(d_ref, x_ref, g_ref, gu_ref, wgu_ref, wd_ref, *rest):
        if w_o is not None:
            wo_ref, rest = rest[0], rest[1:]
        dx_ref, a_ref, dgu_ref, h_ref, d16_ref, dg_ref = rest[:6]

        @pl.when(pl.program_id(0) == 0)
        def _():
            dg_ref[...] = jnp.zeros_like(dg_ref)

        d = d_ref[...]
        d16 = d.astype(BF16)
        d16_ref[...] = d16
        dh = jnp.zeros((tm, D), F32)
        for c0, n in _ffn_chunks(F):
            g = gu_ref[:, c0 : c0 + n]
            u = gu_ref[:, F + c0 : F + c0 + n]
            da = _dot_nt(d16, wd_ref[c0 : c0 + n, :])
            s = _sigmoid(g)
            sg = g * s
            a_ref[:, c0 : c0 + n] = (sg * u).astype(BF16)
            dg16 = (da * u * (s + sg * (1.0 - s))).astype(BF16)
            du16 = (da * sg).astype(BF16)
            dgu_ref[:, c0 : c0 + n] = dg16
            dgu_ref[:, F + c0 : F + c0 + n] = du16
            dh = dh + _dot_nt(dg16, wgu_ref[:, c0 : c0 + n]) + _dot_nt(du16, wgu_ref[:, F + c0 : F + c0 + n])
        xt = x_ref[...]
        r = _rms(xt)
        gn = g_ref[...]
        h_ref[...] = ((xt * r) * gn).astype(BF16)
        dx, dgn = _rms_bwd(xt, r, gn, dh)
        dg_ref[0:1, :] += dgn
        dxi = d + dx
        dx_ref[...] = dxi
        if w_o is not None:
            dxi16_ref, dao_ref = rest[6:8]
            dxi16 = dxi.astype(BF16)
            dxi16_ref[...] = dxi16
            dao_ref[...] = _dot_nt(dxi16, wo_ref[...]).astype(BF16)

    tile = pl.BlockSpec((tm, D), lambda i: (i, 0))
    args = [dxo, x, gain, gu, w_gu, w_d]
    wide = lambda n: pl.BlockSpec((tm, n), lambda i: (i, 0))
    in_specs = [tile, tile, _resident((1, D)), wide(2 * F), _resident((D, 2 * F)), _resident((F, D))]
    out_specs = [tile, wide(F), wide(2 * F), tile, tile, pl.BlockSpec((8, D), lambda i: (0, 0))]
    out_shape = [
        jax.ShapeDtypeStruct((T, D), F32),
        jax.ShapeDtypeStruct((T, F), BF16),
        jax.ShapeDtypeStruct((T, 2 * F), BF16),
        jax.ShapeDtypeStruct((T, D), BF16),
        jax.ShapeDtypeStruct((T, D), BF16),
        jax.ShapeDtypeStruct((8, D), F32),
    ]
    if w_o is not None:
        args.append(w_o)
        in_specs.append(_resident(w_o.shape))
        out_specs += [tile, pl.BlockSpec((tm, w_o.shape[0]), lambda i: (i, 0))]
        out_shape += [jax.ShapeDtypeStruct((T, D), BF16), jax.ShapeDtypeStruct((T, w_o.shape[0]), BF16)]
    return _call(body, after=after, args=args, name="ffn_bwd", grid=(T // tm,), in_specs=in_specs, out_specs=out_specs,
                 out_shape=out_shape)


def wgrad(a, b, *, name, a_cols=0, b_cols=0, group=1, flat=False, tk, out_dtype=BF16, after=None):
    T, K = a.shape
    J = 1
    if a_cols:
        K = a_cols
        J = a.shape[1] // K
        a_spec = pl.BlockSpec((tk, K), lambda j, k: (k, j))
    else:
        a_spec = pl.BlockSpec((tk, K), lambda j, k: (k, 0))
    if b_cols:
        N = b_cols * group
        J = b.shape[1] // N
        b_spec = pl.BlockSpec((tk, N), lambda j, k: (k, j))
    else:
        N = b.shape[1]
        b_spec = pl.BlockSpec((tk, N), lambda j, k: (k, 0))
    nk = T // tk
    if flat:
        o_spec, o_shape = pl.BlockSpec((K, N), lambda j, k: (0, j)), (K, J * N)
    elif group > 1:
        o_spec, o_shape = pl.BlockSpec((group, K, b_cols), lambda j, k: (j, 0, 0)), (J * group, K, b_cols)
    else:
        o_spec, o_shape = pl.BlockSpec((None, K, N), lambda j, k: (j, 0, 0)), (J, K, N)

    def body(a_ref, b_ref, o_ref, acc_ref):
        k = pl.program_id(1)

        @pl.when(k == 0)
        def _():
            acc_ref[...] = jnp.zeros_like(acc_ref)

        acc_ref[...] += _dot_tn(a_ref[...], b_ref[...])

        @pl.when(k == nk - 1)
        def _():
            if group > 1 and not flat:
                for i in range(group):
                    o_ref[i] = acc_ref[:, i * b_cols : (i + 1) * b_cols].astype(out_dtype)
            else:
                o_ref[...] = acc_ref[...].astype(out_dtype)

    outs, _ = _call(
        body,
        after=after,
        name=name,
        grid=(J, nk),
        in_specs=[a_spec, b_spec],
        out_specs=[o_spec],
        out_shape=[jax.ShapeDtypeStruct(o_shape, out_dtype)],
        args=(a, b),
        scratch=[pltpu.VMEM((K, N), F32)],
    )
    return outs[0]


def _seg(xs, lo):
    s_lo = [jnp.sum(jnp.where(lo, x, 0.0), axis=-1, keepdims=True) for x in xs]
    s_hi = [jnp.sum(jnp.where(lo, 0.0, x), axis=-1, keepdims=True) for x in xs]
    return [jnp.where(lo, a, b) for a, b in zip(s_lo, s_hi)]


def _head_norm(xs, gains, lo):
    rs = [lax.rsqrt(s * (1.0 / HEAD_DIM) + EPS) for s in _seg([x * x for x in xs], lo)]
    return [(x * r) * g for x, r, g in zip(xs, rs, gains)], rs


def _head_norm_bwd(xs, rs, gains, dys, lo):
    xns = [x * r for x, r in zip(xs, rs)]
    dxns = [dy * g for dy, g in zip(dys, gains)]
    means = [s * (1.0 / HEAD_DIM) for s in _seg([a * b for a, b in zip(dxns, xns)], lo)]
    dxs = [r * (dxn - xn * m) for r, dxn, xn, m in zip(rs, dxns, xns, means)]
    return dxs, [jnp.sum(dy * xn, axis=0, keepdims=True) for dy, xn in zip(dys, xns)]


def _swap_halves(x):
    return pltpu.roll(x, HEAD_DIM, 1)


def qkv_proj(x, gain, w, qg, kg, *, tm):
    T, D = x.shape
    N = w.shape[1]
    kvw = N_KV_HEADS * HEAD_DIM
    nqt, nkt = D // LANES, kvw // LANES

    def body(x_ref, g_ref, w_ref, qg_ref, kg_ref, qkv_ref, q_ref, kd_ref, vd_ref):
        xt = x_ref[...]
        h = ((xt * _rms(xt)) * g_ref[...]).astype(BF16)
        qkv = _dot(h, w_ref[...])
        qkv_ref[...] = qkv
        lo = lax.broadcasted_iota(jnp.int32, (1, LANES), 1) < HEAD_DIM
        tiles = [qkv[:, t * LANES : (t + 1) * LANES] for t in range(nqt + nkt)]
        normed, _ = _head_norm(tiles, [qg_ref[...]] * nqt + [kg_ref[...]] * nkt, lo)
        for t in range(nqt):
            q_ref[:, t * LANES : (t + 1) * LANES] = (normed[t] * SCALE).astype(BF16)
        for t in range(nkt):
            kn = normed[nqt + t]
            v = qkv[:, D + kvw + t * LANES : D + kvw + (t + 1) * LANES]
            for src, dst in ((kn, kd_ref), (v, vd_ref)):
                sw = _swap_halves(src)
                dst[:, 2 * t * LANES : (2 * t + 1) * LANES] = jnp.where(lo, src, sw).astype(BF16)
                dst[:, (2 * t + 1) * LANES : (2 * t + 2) * LANES] = jnp.where(lo, sw, src).astype(BF16)

    row = lambda i: (i, 0)
    return pl.pallas_call(
        body,
        name="qkv_proj",
        grid=(T // tm,),
        in_specs=[pl.BlockSpec((tm, D), row), _resident((1, D)), _resident((D, N)), _resident((1, LANES)), _resident((1, LANES))],
        out_specs=[pl.BlockSpec((tm, N), row), pl.BlockSpec((tm, D), row), pl.BlockSpec((tm, 2 * kvw), row), pl.BlockSpec((tm, 2 * kvw), row)],
        out_shape=[
            jax.ShapeDtypeStruct((T, N), F32),
            jax.ShapeDtypeStruct((T, D), BF16),
            jax.ShapeDtypeStruct((T, 2 * kvw), BF16),
            jax.ShapeDtypeStruct((T, 2 * kvw), BF16),
        ],
        compiler_params=_params(1),
    )(x, gain, w, qg, kg)


def _attn_tables(sinks, n_q_heads):
    P = n_q_heads // N_KV_HEADS // 2
    h = jnp.arange(1, n_q_heads + 1, dtype=F32)
    slopes = jnp.exp2(-8.0 * h / n_q_heads).reshape(N_KV_HEADS, P, 1, 2, 1)
    qi = jnp.arange(BLOCK)[:, None]
    kj = jnp.arange(BLOCK)[None, :]
    dist = jnp.where(kj <= qi, qi - kj, qi + BLOCK - kj).astype(F32)
    shape = (N_KV_HEADS, P, BLOCK, 2, BLOCK)
    bias = jnp.broadcast_to(-slopes * dist[None, None, :, None, :], shape)
    sink = jnp.broadcast_to(sinks.astype(F32).reshape(N_KV_HEADS, P, 1, 2, 1), shape)
    return bias.reshape(N_KV_HEADS, P * BLOCK, 2 * BLOCK), sink.reshape(N_KV_HEADS, P * BLOCK, 2 * BLOCK)


def _attn_specs(D, nb):
    kvw2 = 2 * N_KV_HEADS * HEAD_DIM
    cur = lambda b, i: (b * nb + i, 0)
    prev = lambda b, i: (jnp.maximum(b * nb + i - 1, 0), 0)
    return [
        pl.BlockSpec((BLOCK, D), cur),
        pl.BlockSpec((BLOCK, kvw2), cur),
        pl.BlockSpec((BLOCK, kvw2), prev),
        pl.BlockSpec((BLOCK, kvw2), cur),
        pl.BlockSpec((BLOCK, kvw2), prev),
    ]


def _attn_operands(kh, P, lo, q_ref, kc_ref, kp_ref, vc_ref, vp_ref):
    sl = slice(kh * LANES, (kh + 1) * LANES)

    def cat(prev_ref, cur_ref):
        d = jnp.concatenate([prev_ref[:, sl], cur_ref[:, sl]], axis=0)
        z = jnp.zeros_like(d)
        return jnp.concatenate([jnp.where(lo, d, z), jnp.where(lo, z, d)], axis=0)

    qt = jnp.concatenate([q_ref[:, (kh * P + pr) * LANES : (kh * P + pr + 1) * LANES] for pr in range(P)], axis=0)
    return qt, cat(kp_ref, kc_ref), cat(vp_ref, vc_ref)


def _attn_exp(s_all, bias, sink, tri, first):
    out = []
    for par in range(2):
        c0 = 2 * par * BLOCK
        s = jnp.where(tri, s_all[:, c0 + BLOCK : c0 + 2 * BLOCK], jnp.where(first, NEG, s_all[:, c0 : c0 + BLOCK]))
        s = s + bias[:, par * BLOCK : (par + 1) * BLOCK]
        snk = sink[:, par * BLOCK : (par + 1) * BLOCK]
        m = jnp.maximum(jnp.max(s, axis=-1, keepdims=True), snk)
        out.append((jnp.exp(s - m), jnp.exp(snk - m)))
    return out


def _unfold(x, tri):
    z = jnp.zeros_like(x)
    return jnp.concatenate([jnp.where(tri, z, x), jnp.where(tri, x, z)], axis=1)


def _attn_masks(R):
    lane = lax.broadcasted_iota(jnp.int32, (1, LANES), 1)
    row = lax.broadcasted_iota(jnp.int32, (R, BLOCK), 0) & (BLOCK - 1)
    col = lax.broadcasted_iota(jnp.int32, (R, BLOCK), 1)
    return lane, lane < HEAD_DIM, col <= row


def attn_fwd(q16, kd, vd, bias, sink, *, seq, n_seq):
    T, D = q16.shape
    nb = seq // BLOCK
    P = D // HEAD_DIM // N_KV_HEADS // 2
    R = P * BLOCK
    KV = range(N_KV_HEADS)

    def body(q_ref, kc_ref, kp_ref, vc_ref, vp_ref, bias_ref, sink_ref, o_ref):
        first = pl.program_id(1) == 0
        _, lo, tri = _attn_masks(R)
        ops = [_attn_operands(kh, P, lo, q_ref, kc_ref, kp_ref, vc_ref, vp_ref) for kh in KV]
        s_all = [_dot_nt(ops[kh][0], ops[kh][1]) for kh in KV]
        ex = [_attn_exp(s_all[kh], bias_ref[kh], sink_ref[kh], tri, first) for kh in KV]
        den = [[jnp.sum(e, axis=-1, keepdims=True) + es for e, es in ex[kh]] for kh in KV]
        lhs = [jnp.concatenate([_unfold(e, tri) for e, _ in ex[kh]], axis=1).astype(BF16) for kh in KV]
        o = [_dot(lhs[kh], ops[kh][2]) for kh in KV]
        for kh in KV:
            out = o[kh] / jnp.where(lo, den[kh][0], den[kh][1])
            for pr in range(P):
                t = kh * P + pr
                o_ref[:, t * LANES : (t + 1) * LANES] = out[pr * BLOCK : (pr + 1) * BLOCK, :].astype(BF16)

    return pl.pallas_call(
        body,
        name="attn_fwd",
        grid=(n_seq, nb),
        in_specs=_attn_specs(D, nb) + [_resident((N_KV_HEADS, R, 2 * BLOCK)), _resident((N_KV_HEADS, R, 2 * BLOCK))],
        out_specs=pl.BlockSpec((BLOCK, D), lambda b, i: (b * nb + i, 0)),
        out_shape=jax.ShapeDtypeStruct((T, D), BF16),
        compiler_params=_params(2),
    )(q16, kd, kd, vd, vd, bias, sink)


def attn_bwd(q16, kd, vd, do, bias, sink, *, seq, n_seq):
    T, D = q16.shape
    kvw2 = 2 * N_KV_HEADS * HEAD_DIM
    nb = seq // BLOCK
    G = D // HEAD_DIM // N_KV_HEADS
    P = G // 2
    R = P * BLOCK
    KV = range(N_KV_HEADS)

    def body(q_ref, kc_ref, kp_ref, vc_ref, vp_ref, do_ref, bias_ref, sink_ref,
             dq_ref, dkc_ref, dkp_ref, dvc_ref, dvp_ref, dsink_ref):
        first = pl.program_id(1) == 0

        @pl.when(jnp.logical_and(pl.program_id(0) == 0, first))
        def _():
            dsink_ref[...] = jnp.zeros_like(dsink_ref)

        lane, lo, tri = _attn_masks(R)
        ops = [_attn_operands(kh, P, lo, q_ref, kc_ref, kp_ref, vc_ref, vp_ref) for kh in KV]
        do16 = [jnp.concatenate([do_ref[:, (kh * P + pr) * LANES : (kh * P + pr + 1) * LANES] for pr in range(P)], axis=0)
                for kh in KV]
        s_all = [_dot_nt(ops[kh][0], ops[kh][1]) for kh in KV]
        dp_all = [_dot_nt(do16[kh], ops[kh][2]) for kh in KV]
        ex = [_attn_exp(s_all[kh], bias_ref[kh], sink_ref[kh], tri, first) for kh in KV]
        den = [[jnp.sum(e, axis=-1, keepdims=True) for e, _ in ex[kh]] for kh in KV]
        dsink = jnp.zeros((1, LANES), F32)
        pf, dsf = [], []
        for kh in KV:
            ps_, ds_ = [], []
            for par in range(2):
                e, es = ex[kh][par]
                inv = 1.0 / (den[kh][par] + es)
                p = e * inv
                c0 = 2 * par * BLOCK
                dp = jnp.where(tri, dp_all[kh][:, c0 + BLOCK : c0 + 2 * BLOCK], dp_all[kh][:, c0 : c0 + BLOCK])
                delta = jnp.sum(p * dp, axis=-1, keepdims=True)
                ds_.append(_unfold(p * (dp - delta), tri))
                ps_.append(_unfold(p, tri))
                dsr = -((es * inv) * delta)
                for pr in range(P):
                    hq = kh * G + 2 * pr + par
                    tot = jnp.sum(dsr[pr * BLOCK : (pr + 1) * BLOCK, :], axis=0, keepdims=True)
                    dsink = dsink + jnp.where(lane == hq, tot, 0.0)
            pf.append(jnp.concatenate(ps_, axis=1).astype(BF16))
            dsf.append(jnp.concatenate(ds_, axis=1).astype(BF16))
        dq = [_dot(dsf[kh], ops[kh][1]) for kh in KV]
        dk = [_dot_tn(dsf[kh], ops[kh][0]) for kh in KV]
        dv = [_dot_tn(pf[kh], do16[kh]) for kh in KV]
        dsink_ref[0:1, :] += dsink
        for kh in KV:
            sl = slice(kh * LANES, (kh + 1) * LANES)
            for pr in range(P):
                t = kh * P + pr
                dq_ref[:, t * LANES : (t + 1) * LANES] = dq[kh][pr * BLOCK : (pr + 1) * BLOCK, :]
            for full, prev_ref, cur_ref in ((dk[kh], dkp_ref, dkc_ref), (dv[kh], dvp_ref, dvc_ref)):
                dup = jnp.where(lo, full[: 2 * BLOCK, :], full[2 * BLOCK :, :])
                prev_ref[:, sl] = dup[:BLOCK, :].astype(BF16)
                cur_ref[:, sl] = dup[BLOCK:, :].astype(BF16)

    cur = lambda b, i: (b * nb + i, 0)
    kv_spec = pl.BlockSpec((BLOCK, kvw2), cur)
    kv_shape = jax.ShapeDtypeStruct((T, kvw2), BF16)
    return pl.pallas_call(
        body,
        name="attn_bwd",
        grid=(n_seq, nb),
        in_specs=_attn_specs(D, nb)
        + [pl.BlockSpec((BLOCK, D), cur), _resident((N_KV_HEADS, R, 2 * BLOCK)), _resident((N_KV_HEADS, R, 2 * BLOCK))],
        out_specs=[pl.BlockSpec((BLOCK, D), cur), kv_spec, kv_spec, kv_spec, kv_spec, pl.BlockSpec((8, LANES), lambda b, i: (0, 0))],
        out_shape=[jax.ShapeDtypeStruct((T, D), F32), kv_shape, kv_shape, kv_shape, kv_shape, jax.ShapeDtypeStruct((8, LANES), F32)],
        compiler_params=_params(2),
    )(q16, kd, kd, vd, vd, do, bias, sink)


def qkv_bwd(dq, dkc, dkp, dvc, dvp, qkv, dres, x, gain, w_qkv, qg, kg, *, seq, tm):
    T, D = x.shape
    kvw2 = dkc.shape[1]
    kvw = kvw2 // 2
    nqt, nkt = D // LANES, kvw // LANES
    nb = seq // BLOCK
    nbt = tm // BLOCK
    assert nb % nbt == 0
    n = T // tm

    def body(dq_ref, dkc_ref, dkpa_ref, dkpb_ref, dvc_ref, dvpa_ref, dvpb_ref, qkv_ref, dres_ref, x_ref, g_ref, w_ref,
             qg_ref, kg_ref, dx_ref, dqkv_ref, h_ref, dg_ref, hg_ref):
        i = pl.program_id(0)

        @pl.when(i == 0)
        def _():
            dg_ref[...] = jnp.zeros_like(dg_ref)
            hg_ref[...] = jnp.zeros_like(hg_ref)

        lo = lax.broadcasted_iota(jnp.int32, (1, LANES), 1) < HEAD_DIM
        last = ((i + 1) * nbt) % nb == 0
        up = lambda ref: ref[...].astype(F32)

        def with_next(cur_ref, own_ref, next_ref):
            return up(cur_ref) + jnp.concatenate([up(own_ref)[BLOCK:, :], jnp.where(last, 0.0, up(next_ref))], axis=0)

        dkd = with_next(dkc_ref, dkpa_ref, dkpb_ref)
        dvd = with_next(dvc_ref, dvpa_ref, dvpb_ref)

        def undup(d, t):
            a, b = d[:, 2 * t * LANES : (2 * t + 1) * LANES], d[:, (2 * t + 1) * LANES : (2 * t + 2) * LANES]
            return jnp.where(lo, a + _swap_halves(a), b + _swap_halves(b))

        tiles = [qkv_ref[:, t * LANES : (t + 1) * LANES] for t in range(nqt + nkt)]
        gains = [qg_ref[...]] * nqt + [kg_ref[...]] * nkt
        dys = [dq_ref[:, t * LANES : (t + 1) * LANES] * SCALE for t in range(nqt)] + [undup(dkd, t) for t in range(nkt)]
        _, rs = _head_norm(tiles, gains, lo)
        dxs, dgs = _head_norm_bwd(tiles, rs, gains, dys, lo)
        for t in range(nqt + nkt):
            dqkv_ref[:, t * LANES : (t + 1) * LANES] = dxs[t].astype(BF16)
        for t in range(nkt):
            dqkv_ref[:, D + kvw + t * LANES : D + kvw + (t + 1) * LANES] = undup(dvd, t).astype(BF16)
        hg_ref[0:1, :] += functools.reduce(lambda a, b: a + b, dgs[:nqt])
        hg_ref[1:2, :] += functools.reduce(lambda a, b: a + b, dgs[nqt:])
        dh = _dot_nt(dqkv_ref[...], w_ref[...])
        xt = x_ref[...]
        r = _rms(xt)
        gn = g_ref[...]
        h_ref[...] = ((xt * r) * gn).astype(BF16)
        dx, dgn = _rms_bwd(xt, r, gn, dh)
        dg_ref[0:1, :] += dgn
        dx_ref[...] = dres_ref[...] + dx

    row = lambda i: (i, 0)
    nxt_a = pl.BlockSpec((tm, kvw2), row)
    nxt_b = pl.BlockSpec((BLOCK, kvw2), lambda i: (jnp.minimum((i + 1) * nbt, n * nbt - 1), 0))
    return pl.pallas_call(
        body,
        name="qkv_bwd",
        grid=(n,),
        in_specs=[
            pl.BlockSpec((tm, D), row),
            pl.BlockSpec((tm, kvw2), row),
            nxt_a,
            nxt_b,
            pl.BlockSpec((tm, kvw2), row),
            nxt_a,
            nxt_b,
            pl.BlockSpec((tm, D + kvw2), row),
            pl.BlockSpec((tm, D), row),
            pl.BlockSpec((tm, D), row),
            _resident((1, D)),
            _resident((D, D + kvw2)),
            _resident((1, LANES)),
            _resident((1, LANES)),
        ],
        out_specs=[
            pl.BlockSpec((tm, D), row),
            pl.BlockSpec((tm, D + kvw2), row),
            pl.BlockSpec((tm, D), row),
            pl.BlockSpec((8, D), lambda i: (0, 0)),
            pl.BlockSpec((8, LANES), lambda i: (0, 0)),
        ],
        out_shape=[
            jax.ShapeDtypeStruct((T, D), F32),
            jax.ShapeDtypeStruct((T, D + kvw2), BF16),
            jax.ShapeDtypeStruct((T, D), BF16),
            jax.ShapeDtypeStruct((8, D), F32),
            jax.ShapeDtypeStruct((8, LANES), F32),
        ],
        compiler_params=_params(1),
    )(dq, dkc, dkp, dkp, dvc, dvp, dvp, qkv, dres, x, gain, w_qkv, qg, kg)


def local_step(x, target, gains, w, *, seq, tm=256, tm_ffn=256, tm_conv=512, tk=2048, shards=None, ex=None):
    T, D = x.shape
    n_seq = T // seq
    nm, nf, qgain, kgain, sinks = gains
    H = D // HEAD_DIM
    tk, tk_long = min(tk, T), min(2 * tk, T)
    qg2, kg2 = jnp.tile(qgain, (1, 2)), jnp.tile(kgain, (1, 2))
    bias, sinkcol = _attn_tables(sinks, H)

    dist = shards is not None
    w = dict(w)

    plan = _Gather([shards["w_gu"][0], shards["w_d"][0]]) if dist else None
    (x1, bcx, y_conv, z16), got = conv_fwd(x, nm[0:1], w["w_in"], w["cw"], w["w_out"], seq=seq, tm=tm_conv, plan=plan)
    if dist:
        w["w_gu"], w["w_d"] = [cols_from_shards(got[0]), None], [got[1].reshape(-1, D), None]
    plan = _Gather([shards["w_qkv"], shards["w_o"], shards["w_gu"][1], shards["w_d"][1]]) if dist else None
    (x2, gu0), got = ffn_fwd(x1, nf[0:1], w["w_gu"][0], w["w_d"][0], tm=2 * tm_ffn, plan=plan)
    if dist:
        w["w_qkv"], w["w_o"] = cols_from_shards(got[0]), got[1].reshape(D, D)
        w["w_gu"][1], w["w_d"][1] = cols_from_shards(got[2]), got[3].reshape(-1, D)
    qkv, q16, kd, vd = qkv_proj(x2, nm[1:2], w["w_qkv"], qg2, kg2, tm=tm_conv)
    ao = attn_fwd(q16, kd, vd, bias, sinkcol, seq=seq, n_seq=n_seq)
    (x3, dx4, gu1, sse), _ = ffn_fwd(x2, nf[1:2], w["w_gu"][1], w["w_d"][1], tm=tm_ffn, attn=(ao, w["w_o"]), target=target)

    by_dest = lambda a: a.reshape(N_DEV, -1, a.shape[-1])
    gu_cols = 2 * MXU_TILE

    def send(name, *entries):
        if ex is None:
            return None
        items = [(a, False, key, (N_DEV,) + (() if layers is None else (layers,)) + a.shape[1:], layer)
                 for a, key, layer, layers in entries]
        return ex.start(items, name=name)

    (dx3, a16, dgu, h16, d16, dnf1, dx3_16, dao), _ = ffn_bwd(
        dx4, x3, nf[1:2], gu1, w["w_gu"][1], w["w_d"][1], tm=tm, w_o=w["w_o"])
    g_gu1 = shards_from_cols(wgrad(h16, dgu, name="wgrad_gu1", b_cols=gu_cols, flat=True, tk=tk_long))
    g_d1 = by_dest(wgrad(a16, d16, name="wgrad_d1", a_cols=a16.shape[1] // 2, tk=tk))
    g_o = by_dest(wgrad(ao, dx3_16, name="wgrad_o", tk=tk))
    dq, dkc, dkp, dvc, dvp, dsinks = attn_bwd(q16, kd, vd, dao, bias, sinkcol, seq=seq, n_seq=n_seq)
    dx2, dqkv16, h16, dnm1, dgains = qkv_bwd(dq, dkc, dkp, dvc, dvp, qkv, dx3, x2, nm[1:2], w["w_qkv"], qg2, kg2, seq=seq, tm=tm_conv)
    g_qkv = shards_from_cols(wgrad(h16, dqkv16, name="wgrad_qkv", tk=tk)[0])
    tok = send("exchange_layer1", (g_gu1, "w_gu", 1, 2), (g_d1, "w_d", 1, 2), (g_o, "w_o", None, None),
               (g_qkv, "w_qkv", None, None))
    (dx1, a16, dgu, h16, d16, dnf0), _ = ffn_bwd(dx2, x1, nf[0:1], gu0, w["w_gu"][0], w["w_d"][0], tm=tm, after=tok)
    g_gu0 = shards_from_cols(wgrad(h16, dgu, name="wgrad_gu0", b_cols=gu_cols, flat=True, tk=tk_long))
    tok = send("exchange_gu0", (g_gu0, "w_gu", 0, 2))
    g_d0 = by_dest(wgrad(a16, d16, name="wgrad_d0", a_cols=a16.shape[1] // 2, tk=tk, after=tok))
    tok = send("exchange_d0", (g_d0, "w_d", 0, 2))
    (gx, dbcx, h16, d16, dcw, dnm0), _ = conv_bwd(
        dx1, x, nm[0:1], bcx, y_conv, w["cw"], w["w_in"], w["w_out"], seq=seq, tm=tm_conv, after=tok)
    g_out = by_dest(wgrad(z16, d16, name="wgrad_out", tk=tk))
    g_cw = dcw[0:3].reshape(3, N_DEV, D // N_DEV).transpose(1, 0, 2)
    tok = send("exchange_out", (g_out, "w_out", None, None), (g_cw, "cw", None, None))
    g_in = wgrad(h16, dbcx, name="wgrad_in", b_cols=3 * D // N_DEV, group=2, tk=tk_long, after=tok)
    g = dict(w_in=g_in, cw=g_cw, w_out=g_out, w_o=g_o, w_qkv=g_qkv, w_gu=[g_gu0, g_gu1], w_d=[g_d0, g_d1])
    small = dict(nm0=dnm0, nm1=dnm1, nf0=dnf0, nf1=dnf1, gains=dgains, sinks=dsinks)
    return sse, gx, g, small


def _adamw_math(g, w, m, v):
    m = ADAM_B1 * m + (1.0 - ADAM_B1) * g
    v = ADAM_B2 * v + (1.0 - ADAM_B2) * (g * g)
    m_hat = m / (1.0 - ADAM_B1 ** ADAM_STEP)
    v_hat = v / (1.0 - ADAM_B2 ** ADAM_STEP)
    delta = -ADAM_LR * (m_hat / (jnp.sqrt(v_hat) + ADAM_EPS) + ADAM_WD * w)
    return delta, m, v


def adamw(parts, owns, w, m, v, *, name, after=None):
    n, LR, C = parts.shape
    L = len(owns)
    R = LR // L
    tr = R
    for cand in (256, 128, 88, 64, 32, 16, 8):
        if R > cand and R % cand == 0:
            tr = cand
            break
    per_layer = R // tr
    extra = [] if after is None else [after]

    def body(me_ref, p_ref, *rest):
        own_refs, (w_ref, m_ref, v_ref) = rest[:L], rest[L : L + 3]
        g_ref, d_ref, mo_ref, vo_ref = rest[L + 3 + len(extra) :]
        layer = pl.program_id(0) // per_layer
        mine = own_refs[0][...].astype(F32)
        for j in range(1, L):
            mine = jnp.where(layer == j, own_refs[j][...].astype(F32), mine)
        g = None
        for s in range(n):
            share = jnp.where(me_ref[0] == s, mine, p_ref[s].astype(F32))
            g = share if g is None else g + share
        g_ref[...] = g
        d_ref[...], mo_ref[...], vo_ref[...] = _adamw_math(g, w_ref[...], m_ref[...], v_ref[...])

    blk = pl.BlockSpec((tr, C), lambda i, me: (i, 0))
    own_specs = [pl.BlockSpec((None, tr, C), lambda i, me: (me[0], i % per_layer, 0)) if o.ndim == 3
                 else pl.BlockSpec((tr, C), lambda i, me: (i % per_layer, 0)) for o in owns]
    me = (4 * lax.axis_index("x") + 2 * lax.axis_index("y") + lax.axis_index("c")).astype(jnp.int32).reshape(1)
    return pl.pallas_call(
        body,
        name=name,
        grid_spec=pltpu.PrefetchScalarGridSpec(
            num_scalar_prefetch=1,
            grid=(LR // tr,),
            in_specs=[pl.BlockSpec((n, tr, C), lambda i, me: (0, i, 0))] + own_specs + [blk, blk, blk] + _any_specs(len(extra)),
            out_specs=[blk] * 4,
        ),
        out_shape=[jax.ShapeDtypeStruct((LR, C), F32)] * 4,
        compiler_params=_params(1),
    )(me, parts, *owns, w, m, v, *extra)


def pack_small(small, sse, D):
    W = max(D, 2 * LANES)

    def body(nm0, nm1, nf0, nf1, gains, sinks, sse_ref, o_ref):
        o_ref[...] = jnp.zeros_like(o_ref)
        o_ref[0:1, :D] = nm0[0:1, :]
        o_ref[1:2, :D] = nm1[0:1, :]
        o_ref[2:3, :D] = nf0[0:1, :]
        o_ref[3:4, :D] = nf1[0:1, :]
        gq = gains[0:1, :] + pltpu.roll(gains[0:1, :], HEAD_DIM, 1)
        gk = gains[1:2, :] + pltpu.roll(gains[1:2, :], HEAD_DIM, 1)
        lane = lax.broadcasted_iota(jnp.int32, (1, LANES), 1)
        o_ref[4:5, :LANES] = jnp.where(lane < HEAD_DIM, gq, gk)
        o_ref[4:5, LANES : 2 * LANES] = sinks[0:1, :]
        o_ref[5:6, :LANES] = sse_ref[0:1, :] * (0.5 / D)

    return pl.pallas_call(
        body,
        name="pack_small",
        out_shape=jax.ShapeDtypeStruct((8, W), F32),
    )(small["nm0"], small["nm1"], small["nf0"], small["nf1"], small["gains"], small["sinks"], sse)


def _pack_small_params(nm, nf, qg, kg, sk, D):
    W = max(D, 2 * LANES)
    row4 = jnp.concatenate([qg.reshape(-1), kg.reshape(-1), jnp.zeros((LANES - 2 * HEAD_DIM,), F32), sk.reshape(-1)])
    row4 = jnp.pad(row4, (0, W - row4.shape[0]))
    rows = [jnp.pad(r, (0, W - D)) for r in (nm[0], nm[1], nf[0], nf[1])] + [row4]
    return jnp.concatenate([jnp.stack(rows), jnp.zeros((3, W), F32)], axis=0)


def _unpack_small(a, D, H):
    nm = a[0:2, :D]
    nf = a[2:4, :D]
    qg = a[4:5, 0:HEAD_DIM]
    kg = a[4:5, HEAD_DIM : 2 * HEAD_DIM]
    sk = a[4:5, LANES : LANES + H]
    return qg, kg, sk, nm, nf


def kernel(x, conv_w_in, conv_w, conv_w_out, attn_w_qkv, attn_q_gain, attn_k_gain, attn_sinks, attn_w_o, norm_mixer, norm_ffn, ffn_w_gate_up, ffn_w_down, loss_target, m_conv_w_in, m_conv_w, m_conv_w_out, m_attn_w_qkv, m_attn_q_gain, m_attn_k_gain, m_attn_sinks, m_attn_w_o, m_norm_mixer, m_norm_ffn, m_ffn_w_gate_up, m_ffn_w_down, v_conv_w_in, v_conv_w, v_conv_w_out, v_attn_w_qkv, v_attn_q_gain, v_attn_k_gain, v_attn_sinks, v_attn_w_o, v_norm_mixer, v_norm_ffn, v_ffn_w_gate_up, v_ffn_w_down):
    n_seq, seq, D = x.shape
    T = n_seq * seq
    H = D // HEAD_DIM
    L = ffn_w_gate_up.shape[0]

    full = run_plan(_Gather([conv_w_in[0].astype(BF16), conv_w[0], conv_w_out[0].astype(BF16)]), name="gather_conv_weights")
    w = dict(w_in=cols_from_shards(full[0]), cw=full[1].transpose(1, 0, 2).reshape(3, D),
             w_out=full[2].reshape(D, D))
    shards = dict(w_gu=[ffn_w_gate_up[l].astype(BF16) for l in range(L)], w_d=[ffn_w_down[l].astype(BF16) for l in range(L)],
                  w_qkv=attn_w_qkv[0].astype(BF16), w_o=attn_w_o[0].astype(BF16))
    gains = (norm_mixer, norm_ffn, attn_q_gain, attn_k_gain, attn_sinks)
    ex = Exchange()
    sse, gx, g, small = local_step(x.reshape(T, D), loss_target.reshape(T, D), gains, w, seq=seq, shards=shards, ex=ex)
    zones, own = ex.wait([g["w_in"]], name="exchange_wait")

    packed = pack_small(small, sse, D)
    token = ex.start([(g["w_in"], False, "w_in", g["w_in"].shape, None),
                      (packed, True, "small", (N_DEV,) + packed.shape, None)], name="exchange_last")

    def flat(a):
        return a.reshape(-1, a.shape[-1])

    big = [conv_w_in, conv_w, conv_w_out, attn_w_qkv, attn_w_o, ffn_w_gate_up, ffn_w_down]
    big_m = [m_conv_w_in, m_conv_w, m_conv_w_out, m_attn_w_qkv, m_attn_w_o, m_ffn_w_gate_up, m_ffn_w_down]
    big_v = [v_conv_w_in, v_conv_w, v_conv_w_out, v_attn_w_qkv, v_attn_w_o, v_ffn_w_gate_up, v_ffn_w_down]
    keys = ["w_in", "cw", "w_out", "w_qkv", "w_o", "w_gu", "w_d"]

    def update(b, zones, own, after=None):
        zone = zones[keys[b]]
        parts = zone.reshape(N_DEV, -1, zone.shape[-1])
        layers = [None] if zone.ndim == 3 else range(zone.shape[1])
        outs = adamw(parts, [own[(keys[b], l)] for l in layers], flat(big[b]), flat(big_m[b]), flat(big_v[b]),
                     name="adamw_" + keys[b], after=after)
        return [o.reshape(big[b].shape) for o in outs]

    res = [None] + [update(b, zones, own, after=token) for b in range(1, 7)]
    zones, own = ex.wait([r[0] for r in res[1:]], name="exchange_last_wait")
    res[0] = update(0, zones, own)
    sw = _pack_small_params(norm_mixer, norm_ffn, attn_q_gain, attn_k_gain, attn_sinks, D)
    sm = _pack_small_params(m_norm_mixer, m_norm_ffn, m_attn_q_gain, m_attn_k_gain, m_attn_sinks, D)
    sv = _pack_small_params(v_norm_mixer, v_norm_ffn, v_attn_q_gain, v_attn_k_gain, v_attn_sinks, D)
    souts = adamw(zones["small"], [own[("small", None)]], sw, sm, sv, name="adamw_small")
    sres = [_unpack_small(o, D, H) for o in souts]
    loss = souts[0][5, 0]

    def ordered(i):
        r, s = [r[i] for r in res], sres[i]
        return [r[0], r[1], r[2], r[3], s[0], s[1], s[2], r[4], s[3], s[4], r[5], r[6]]

    return (loss, gx.reshape(n_seq, seq, D), *ordered(0), *ordered(1), *ordered(2), *ordered(3))
```

```python
import functools
import math

import jax
import jax.numpy as jnp
from jax import lax
from jax.experimental import pallas as pl
from jax.experimental.pallas import tpu as pltpu

F32 = jnp.float32
BF16 = jnp.bfloat16

EPS = 1e-6
HEAD_DIM = 64
N_KV_HEADS = 4
BLOCK = 128
LANES = 128
N_DEV = 8
NEG = -1e30
SCALE = 1.0 / math.sqrt(HEAD_DIM)

ADAM_LR = 0.001
ADAM_B1 = 0.9
ADAM_B2 = 0.999
ADAM_EPS = 1e-08
ADAM_WD = 0.01
ADAM_STEP = 10

V7X_VMEM_BYTES = 64 * 1024 * 1024
VMEM_LIMIT = V7X_VMEM_BYTES - 2 * 1024 * 1024
MESH = pl.DeviceIdType.MESH

_NT = (((1,), (1,)), ((), ()))
_TN = (((0,), (0,)), ((), ()))


def _params(n_grid):
    return pltpu.CompilerParams(dimension_semantics=("arbitrary",) * n_grid, vmem_limit_bytes=VMEM_LIMIT)


def _resident(shape):
    nd = len(shape)
    return pl.BlockSpec(shape, lambda *_: (0,) * nd, pipeline_mode=pl.Buffered(1))


def _rms(x):
    return lax.rsqrt(jnp.mean(x * x, axis=-1, keepdims=True) + EPS)


def _rms_bwd(x, r, gain, dh):
    xn = x * r
    dxn = dh * gain
    dx = r * (dxn - xn * jnp.mean(dxn * xn, axis=-1, keepdims=True))
    return dx, jnp.sum(dh * xn, axis=0, keepdims=True)


def _dot(a, b):
    return jnp.dot(a, b, preferred_element_type=F32)


def _dot_nt(a, b):
    return lax.dot_general(a, b, _NT, preferred_element_type=F32)


def _dot_tn(a, b):
    return lax.dot_general(a, b, _TN, preferred_element_type=F32)


def _place():
    return lax.axis_index("x"), lax.axis_index("y"), lax.axis_index("c")


def _flip(v, bit):
    return 1 - v if bit else v


def _slot(px, py, pc):
    return 4 * px + 2 * py + pc


class _Gather:
    def __init__(self, shards):
        nt = len(shards)
        self.nt = nt
        self.inputs = list(shards)
        self.out_shapes = [jax.ShapeDtypeStruct((N_DEV,) + s.shape, s.dtype) for s in shards]
        self.scratch = [pltpu.SemaphoreType.DMA((nt, 10)), pltpu.SemaphoreType.DMA((nt, 10)), pltpu.SemaphoreType.DMA((nt,))]
        self.aliases = {}
        self.split = []
        for s in shards:
            rows, tile = s.shape[0], 16 if s.dtype == BF16 else 8
            self.split.append(rows // 2 if rows % (2 * tile) == 0 else rows)

    def phases(self, total):
        assert total >= 8
        return [(0, self.start), (total // 2, self.second), (total - 3, self.forward), (total - 1, self.finish)]

    def _copies(self, ins, outs, sems):
        send_sems, recv_sems, loc_sems = sems
        x, y, c = _place()
        xn, yn, sib = (1 - x, y, c), (x, 1 - y, c), (x, y, 1 - c)
        i_me, i_xn, i_yn, i_dn = _slot(x, y, c), _slot(1 - x, y, c), _slot(x, 1 - y, c), _slot(1 - x, 1 - y, c)
        j_me, j_xn, j_yn, j_dn = _slot(x, y, 1 - c), _slot(1 - x, y, 1 - c), _slot(x, 1 - y, 1 - c), _slot(1 - x, 1 - y, 1 - c)
        local, start, need1, second, need2, forward, need3 = [], [], [], [], [], [], []
        for t in range(self.nt):
            o, rows, h = outs[t], self.inputs[t].shape[0], self.split[t]
            lo = pl.ds(0, h)
            hi = pl.ds(h, rows - h) if h < rows else None

            def rc(k, src, dst, to, t=t):
                return pltpu.make_async_remote_copy(
                    src_ref=src, dst_ref=dst, send_sem=send_sems.at[t, k], recv_sem=recv_sems.at[t, k], device_id=to,
                    device_id_type=MESH)

            def landed(k, slot, part, frm):
                ref = o.at[slot] if part is None else o.at[slot, part]
                return rc(k, ref, ref, frm)

            local.append(pltpu.make_async_copy(ins[t], o.at[i_me], loc_sems.at[t]))
            start += [rc(0, ins[t], o.at[i_me], sib), rc(1, ins[t].at[lo], o.at[i_me, lo], xn),
                      rc(4, ins[t].at[lo], o.at[i_me, lo], yn)]
            need1.append(landed(1, i_xn, lo, xn))
            second.append(rc(3, o.at[i_xn, lo], o.at[i_xn, lo], yn))
            need2 += [landed(4, i_yn, lo, yn), landed(3, i_dn, lo, yn)]
            if hi is not None:
                start += [rc(2, ins[t].at[hi], o.at[i_me, hi], yn), rc(6, ins[t].at[hi], o.at[i_me, hi], xn)]
                need1.append(landed(2, i_yn, hi, yn))
                second.append(rc(5, o.at[i_yn, hi], o.at[i_yn, hi], xn))
                need2 += [landed(6, i_xn, hi, xn), landed(5, i_dn, hi, xn)]
            forward += [rc(7, o.at[i_xn], o.at[i_xn], sib), rc(8, o.at[i_yn], o.at[i_yn], sib), rc(9, o.at[i_dn], o.at[i_dn], sib)]
            need3 += [landed(0, j_me, None, sib), landed(7, j_xn, None, sib), landed(8, j_yn, None, sib), landed(9, j_dn, None, sib)]
        return local, start, need1, second, need2, forward, need3

    def start(self, ins, outs, sems):
        local, start, *_ = self._copies(ins, outs, sems)
        for cp in local + start:
            cp.start()

    def second(self, ins, outs, sems):
        _, _, need1, second, *_ = self._copies(ins, outs, sems)
        for cp in need1:
            cp.wait_recv()
        for cp in second:
            cp.start()

    def forward(self, ins, outs, sems):
        _, _, _, _, need2, forward, _ = self._copies(ins, outs, sems)
        for cp in need2:
            cp.wait_recv()
        for cp in forward:
            cp.start()

    def finish(self, ins, outs, sems):
        local, start, _, second, _, forward, need3 = self._copies(ins, outs, sems)
        for cp in need3:
            cp.wait_recv()
        for cp in start + second + forward:
            cp.wait_send()
        for cp in local:
            cp.wait()


def _any_specs(n):
    return [pl.BlockSpec(memory_space=pl.ANY)] * n


def run_plan(plan, *, name):
    def body(*refs):
        n_in, n_out = len(plan.inputs), len(plan.out_shapes)
        ins, outs, sems = refs[:n_in], refs[n_in : n_in + n_out], refs[n_in + n_out :]
        for _, phase in plan.phases(8):
            phase(ins, outs, sems)

    return pl.pallas_call(
        body,
        name=name,
        in_specs=_any_specs(len(plan.inputs)),
        out_specs=_any_specs(len(plan.out_shapes)),
        out_shape=plan.out_shapes,
        scratch_shapes=plan.scratch,
        input_output_aliases=plan.aliases,
    )(*plan.inputs)


_HBM = pl.BlockSpec(memory_space=pltpu.HBM)
_SEM = pl.BlockSpec(memory_space=pltpu.SEMAPHORE)
_DATAFLOW = pltpu.SideEffectType.DATAFLOW_SIDE_EFFECTING


class Exchange:
    def __init__(self):
        self.zones = {}
        self.pending = []
        self.sources = []

    def start(self, items, *, name):
        nt = len(items)
        keys = list(dict.fromkeys(it[2] for it in items))
        for a, _, key, shape, _ in items:
            if key not in self.zones:
                self.zones[key] = lax.empty(shape, a.dtype)
        nz = len(keys)

        def body(*refs):
            ins, zones, sems, token = refs[:nt], refs[nt : nt + nz], refs[nt + nz : nt + nz + 2 * nt], refs[-1]
            x, y, c = _place()
            me = _slot(x, y, c)
            for k in range(1, N_DEV):
                px, py, pc = _flip(x, (k >> 2) & 1), _flip(y, (k >> 1) & 1), _flip(c, k & 1)
                for t, (_, whole, key, _, layer) in enumerate(items):
                    zone = zones[keys.index(key)]
                    pltpu.make_async_remote_copy(
                        src_ref=ins[t] if whole else ins[t].at[_slot(px, py, pc)],
                        dst_ref=zone.at[me] if layer is None else zone.at[me, layer],
                        send_sem=sems[2 * t], recv_sem=sems[2 * t + 1], device_id=(px, py, pc), device_id_type=MESH).start()
            token[...] = jnp.zeros_like(token)

        bufs = [pltpu.with_memory_space_constraint(b, pltpu.HBM) for b in [it[0] for it in items] + [self.zones[k] for k in keys]]
        outs = pl.pallas_call(
            body,
            name=name,
            in_specs=[_HBM] * (nt + nz),
            out_specs=[_SEM] * (2 * nt) + [_HBM] * (nt + nz) + [pl.BlockSpec(memory_space=pltpu.VMEM)],
            out_shape=[pltpu.SemaphoreType.DMA(())] * (2 * nt) + [pltpu.HBM(b.shape, b.dtype) for b in bufs]
            + [jax.ShapeDtypeStruct((8, LANES), F32)],
            input_output_aliases={i: 2 * nt + i for i in range(nt + nz)},
            compiler_params=pltpu.CompilerParams(has_side_effects=_DATAFLOW),
        )(*bufs)
        for t, (_, _, key, _, layer) in enumerate(items):
            self.pending.append((outs[2 * t], outs[2 * t + 1], key, layer))
        self.sources += [((it[2], it[4]), a) for it, a in zip(items, outs[2 * nt : 3 * nt])]
        for i, key in enumerate(keys):
            self.zones[key] = outs[3 * nt + i]
        return outs[-1]

    def wait(self, after, *, name):
        pending, keys = self.pending, list(self.zones)
        names, sources = [n for n, _ in self.sources], [a for _, a in self.sources]
        ns, nz, npend = len(sources), len(keys), len(pending)
        self.pending, self.sources = [], []

        def body(*refs):
            zones, sems = refs[ns : ns + nz], refs[ns + nz : ns + nz + 2 * npend]
            x, y, c = _place()
            for i, (_, _, key, layer) in enumerate(pending):
                zone = zones[keys.index(key)]
                rows = pl.ds(0, N_DEV - 1)
                seven = zone.at[rows] if layer is None else zone.at[rows, layer]
                pltpu.make_async_remote_copy(
                    src_ref=seven, dst_ref=seven, send_sem=sems[2 * i], recv_sem=sems[2 * i + 1],
                    device_id=(x, y, c), device_id_type=MESH).wait()

        bufs = list(sources) + [self.zones[k] for k in keys]
        flat_sems = [s for p in pending for s in p[:2]]
        outs = pl.pallas_call(
            body,
            name=name,
            in_specs=[_HBM] * (ns + nz) + [_SEM] * (2 * npend) + _any_specs(len(after)),
            out_specs=[_HBM] * (ns + nz),
            out_shape=[pltpu.HBM(b.shape, b.dtype) for b in bufs],
            input_output_aliases={i: i for i in range(ns + nz)},
            compiler_params=pltpu.CompilerParams(has_side_effects=_DATAFLOW),
        )(*bufs, *flat_sems, *after)
        self.zones = {}
        return dict(zip(keys, outs[ns:])), dict(zip(names, outs[:ns]))


def _call(body, *, name, grid, in_specs, out_specs, out_shape, args, scratch=(), plan=None, after=None):
    if after is not None:
        inner, n_real = body, len(in_specs)
        body = lambda *refs: inner(*refs[:n_real], *refs[n_real + 1 :])
        in_specs, args = list(in_specs) + _any_specs(1), list(args) + [after]
    n_in, n_out, n_scr = len(in_specs), len(out_specs), len(scratch)
    if plan is None:
        outs = pl.pallas_call(
            body, name=name, grid=grid, in_specs=in_specs, out_specs=out_specs, out_shape=out_shape,
            scratch_shapes=list(scratch), compiler_params=_params(len(grid)))(*args)
        return outs, None
    c_in, c_out = len(plan.inputs), len(plan.out_shapes)
    phases = plan.phases(math.prod(grid))

    def full(*refs):
        a, refs = refs[:n_in], refs[n_in:]
        ci, refs = refs[:c_in], refs[c_in:]
        o, refs = refs[:n_out], refs[n_out:]
        co, refs = refs[:c_out], refs[c_out:]
        s, cs = refs[:n_scr], refs[n_scr:]
        step = pl.program_id(0)
        for d in range(1, len(grid)):
            step = step * grid[d] + pl.program_id(d)
        for at, phase in phases:
            if at == 0:
                pl.when(step == 0)(functools.partial(phase, ci, co, cs))
        body(*a, *o, *s)
        for at, phase in phases:
            if at > 0:
                pl.when(step == at)(functools.partial(phase, ci, co, cs))

    outs = pl.pallas_call(
        full,
        name=name,
        grid=grid,
        in_specs=list(in_specs) + _any_specs(c_in),
        out_specs=list(out_specs) + _any_specs(c_out),
        out_shape=list(out_shape) + plan.out_shapes,
        scratch_shapes=list(scratch) + plan.scratch,
        input_output_aliases={n_in + i: n_out + t for i, t in plan.aliases.items()},
        compiler_params=_params(len(grid)),
    )(*args, *plan.inputs)
    return outs[:n_out], outs[n_out:]


def _row_tile(R):
    return 256 if R % 256 == 0 else R


def cols_from_shards(a):
    n, R, C = a.shape
    tr = _row_tile(R)

    def body(i_ref, o_ref):
        for s in range(n):
            o_ref[:, s * C : (s + 1) * C] = i_ref[s]

    return pl.pallas_call(
        body,
        name="cols_from_shards",
        grid=(R // tr,),
        in_specs=[pl.BlockSpec((n, tr, C), lambda i: (0, i, 0))],
        out_specs=pl.BlockSpec((tr, n * C), lambda i: (i, 0)),
        out_shape=jax.ShapeDtypeStruct((R, n * C), a.dtype),
        compiler_params=_params(1),
    )(a)


def shards_from_cols(a):
    R, W = a.shape
    C = W // N_DEV
    tr = _row_tile(R)

    def body(i_ref, o_ref):
        for s in range(N_DEV):
            o_ref[s] = i_ref[:, s * C : (s + 1) * C]

    return pl.pallas_call(
        body,
        name="shards_from_cols",
        grid=(R // tr,),
        in_specs=[pl.BlockSpec((tr, W), lambda i: (i, 0))],
        out_specs=pl.BlockSpec((N_DEV, tr, C), lambda i: (0, i, 0)),
        out_shape=jax.ShapeDtypeStruct((N_DEV, R, C), a.dtype),
        compiler_params=_params(1),
    )(a)


def _shift_down(u, prev8, row, n):
    out = pltpu.roll(u, n, 0)
    for k in range(n):
        out = jnp.where(row == k, prev8[8 - n + k : 8 - n + k + 1, :], out)
    return out


def _shift_up(u, next8, row, n, tm):
    out = pltpu.roll(u, tm - n, 0)
    for k in range(n):
        out = jnp.where(row == tm - n + k, next8[k : k + 1, :], out)
    return out


def conv_fwd(x, gain, w_in, cw, w_out, *, seq, tm, plan=None):
    T, D = x.shape
    tps = seq // tm

    def body(x_ref, g_ref, win_ref, cw_ref, wout_ref, x1_ref, bcx_ref, y_ref, z_ref, carry_ref):
        i = pl.program_id(0)

        @pl.when(i % tps == 0)
        def _():
            carry_ref[...] = jnp.zeros_like(carry_ref)

        xt = x_ref[...]
        h = ((xt * _rms(xt)) * g_ref[...]).astype(BF16)
        bcx = _dot(h, win_ref[...])
        bcx_ref[...] = bcx.astype(BF16)
        b, c, xv = bcx[:, :D], bcx[:, D : 2 * D], bcx[:, 2 * D :]
        u = b * xv
        row = lax.broadcasted_iota(jnp.int32, u.shape, 0)
        prev = carry_ref[...]
        u1 = _shift_down(u, prev, row, 1)
        u2 = _shift_down(u, prev, row, 2)
        carry_ref[...] = u[tm - 8 :, :]
        cwv = cw_ref[...]
        y = cwv[0:1, :] * u2 + cwv[1:2, :] * u1 + cwv[2:3, :] * u
        y_ref[...] = y
        z = (c * y).astype(BF16)
        z_ref[...] = z
        x1_ref[...] = xt + _dot(z, wout_ref[...])

    tile = pl.BlockSpec((tm, D), lambda i: (i, 0))
    return _call(
        body,
        plan=plan,
        args=(x, gain, w_in, cw, w_out),
        name="conv_fwd",
        grid=(T // tm,),
        in_specs=[
            pl.BlockSpec((tm, D), lambda i: (i, 0)),
            _resident((1, D)),
            _resident((D, 3 * D)),
            _resident((3, D)),
            _resident((D, D)),
        ],
        out_specs=[tile, pl.BlockSpec((tm, 3 * D), lambda i: (i, 0)), tile, tile],
        out_shape=[jax.ShapeDtypeStruct((T, D), F32), jax.ShapeDtypeStruct((T, 3 * D), BF16),
                   jax.ShapeDtypeStruct((T, D), F32), jax.ShapeDtypeStruct((T, D), BF16)],
        scratch=[pltpu.VMEM((8, D), F32)],
    )


def conv_bwd(dx1, x, gain, bcx, y, cw, w_in, w_out, *, seq, tm, after=None):
    T, D = x.shape
    n = T // tm
    tps = seq // tm

    def body(d_ref, x_ref, g_ref, bcx_ref, y_ref, cw_ref, win_ref, wout_ref,
             gx_ref, dbcx_ref, h_ref, d16_ref, dcw_ref, dg_ref, carry_ref):
        i = pl.program_id(0)
        t = n - 1 - i

        @pl.when(i == 0)
        def _():
            dcw_ref[...] = jnp.zeros_like(dcw_ref)
            dg_ref[...] = jnp.zeros_like(dg_ref)

        @pl.when(t % tps == tps - 1)
        def _():
            carry_ref[...] = jnp.zeros_like(carry_ref)

        d = d_ref[...]
        d16 = d.astype(BF16)
        d16_ref[...] = d16
        dz = _dot_nt(d16, wout_ref[...])
        bcx = bcx_ref[...].astype(F32)
        b, c, xv = bcx[:, :D], bcx[:, D : 2 * D], bcx[:, 2 * D :]
        u = b * xv
        row = lax.broadcasted_iota(jnp.int32, u.shape, 0)
        cwv = cw_ref[...]
        dc = dz * y_ref[...]
        dy = dz * c
        nxt = carry_ref[...]
        dy1 = _shift_up(dy, nxt, row, 1, tm)
        dy2 = _shift_up(dy, nxt, row, 2, tm)
        carry_ref[...] = dy[0:8, :]
        dcw_ref[0:1, :] += jnp.sum(dy2 * u, axis=0, keepdims=True)
        dcw_ref[1:2, :] += jnp.sum(dy1 * u, axis=0, keepdims=True)
        dcw_ref[2:3, :] += jnp.sum(dy * u, axis=0, keepdims=True)
        du = cwv[2:3, :] * dy + cwv[1:2, :] * dy1 + cwv[0:1, :] * dy2
        dbcx_ref[:, :D] = (du * xv).astype(BF16)
        dbcx_ref[:, D : 2 * D] = dc.astype(BF16)
        dbcx_ref[:, 2 * D :] = (du * b).astype(BF16)
        dh = _dot_nt(dbcx_ref[...], win_ref[...])
        xt = x_ref[...]
        r = _rms(xt)
        gn = g_ref[...]
        h_ref[...] = ((xt * r) * gn).astype(BF16)
        dx, dgn = _rms_bwd(xt, r, gn, dh)
        dg_ref[0:1, :] += dgn
        gx_ref[...] = d + dx

    rev = lambda i: (n - 1 - i, 0)
    return _call(
        body,
        after=after,
        args=(dx1, x, gain, bcx, y, cw, w_in, w_out),
        name="conv_bwd",
        grid=(n,),
        in_specs=[
            pl.BlockSpec((tm, D), rev),
            pl.BlockSpec((tm, D), rev),
            _resident((1, D)),
            pl.BlockSpec((tm, 3 * D), rev),
            pl.BlockSpec((tm, D), rev),
            _resident((3, D)),
            _resident((D, 3 * D)),
            _resident((D, D)),
        ],
        out_specs=[
            pl.BlockSpec((tm, D), rev),
            pl.BlockSpec((tm, 3 * D), rev),
            pl.BlockSpec((tm, D), rev),
            pl.BlockSpec((tm, D), rev),
            pl.BlockSpec((8, D), lambda i: (0, 0)),
            pl.BlockSpec((8, D), lambda i: (0, 0)),
        ],
        out_shape=[
            jax.ShapeDtypeStruct((T, D), F32),
            jax.ShapeDtypeStruct((T, 3 * D), BF16),
            jax.ShapeDtypeStruct((T, D), BF16),
            jax.ShapeDtypeStruct((T, D), BF16),
            jax.ShapeDtypeStruct((8, D), F32),
            jax.ShapeDtypeStruct((8, D), F32),
        ],
        scratch=[pltpu.VMEM((8, D), F32)],
    )


MXU_TILE = 256
FFN_CHUNK = 4 * MXU_TILE


def _sigmoid(g):
    return 1.0 / (1.0 + jnp.exp(-g))


def _ffn_chunks(F):
    assert F % MXU_TILE == 0
    return [(s, min(FFN_CHUNK, F - s)) for s in range(0, F, FFN_CHUNK)]


def ffn_fwd(x, gain, w_gu, w_d, *, tm, plan=None, attn=None, target=None):
    T, D = x.shape
    F = w_d.shape[0]
    row = lambda i: (i, 0)
    tile = pl.BlockSpec((tm, D), row)

    def body(*refs):
        refs = list(refs)
        x_ref, g_ref, wgu_ref, wd_ref = refs[:4]
        del refs[:4]
        if attn is not None:
            ao_ref, wo_ref = refs[:2]
            del refs[:2]
        if target is not None:
            t_ref = refs.pop(0)
        if attn is not None:
            xin_ref = refs.pop(0)
        xo_ref, gu_ref = refs[:2]
        xt = x_ref[...]
        if attn is not None:
            xt = xt + _dot(ao_ref[...], wo_ref[...])
            xin_ref[...] = xt
        h = ((xt * _rms(xt)) * g_ref[...]).astype(BF16)
        acc = xt
        for s, n in _ffn_chunks(F):
            g = _dot(h, wgu_ref[:, s : s + n])
            u = _dot(h, wgu_ref[:, F + s : F + s + n])
            gu_ref[:, s : s + n] = g
            gu_ref[:, F + s : F + s + n] = u
            a = ((g * _sigmoid(g)) * u).astype(BF16)
            acc = acc + _dot(a, wd_ref[s : s + n, :])
        if target is None:
            xo_ref[...] = acc
        else:
            s_ref = refs[2]

            @pl.when(pl.program_id(0) == 0)
            def _():
                s_ref[...] = jnp.zeros_like(s_ref)

            e = acc - t_ref[...]
            xo_ref[...] = e * (1.0 / D)
            s_ref[...] += jnp.sum(jnp.sum(e * e, axis=-1, keepdims=True), axis=0, keepdims=True)

    args = [x, gain, w_gu, w_d]
    in_specs = [tile, _resident((1, D)), _resident((D, 2 * F)), _resident((F, D))]
    out_specs = [tile, pl.BlockSpec((tm, 2 * F), row)]
    out_shape = [jax.ShapeDtypeStruct((T, D), F32), jax.ShapeDtypeStruct((T, 2 * F), F32)]
    if attn is not None:
        args += list(attn)
        in_specs += [pl.BlockSpec((tm, attn[0].shape[1]), row), _resident(attn[1].shape)]
        out_specs.insert(0, tile)
        out_shape.insert(0, jax.ShapeDtypeStruct((T, D), F32))
    if target is not None:
        args.append(target)
        in_specs.append(tile)
        out_specs.append(pl.BlockSpec((8, LANES), lambda i: (0, 0)))
        out_shape.append(jax.ShapeDtypeStruct((8, LANES), F32))
    return _call(body, plan=plan, args=args, name="ffn_fwd", grid=(T // tm,), in_specs=in_specs, out_specs=out_specs,
                 out_shape=out_shape)


def ffn_bwd(dxo, x, gain, gu, w_gu, w_d, *, tm, after=None, w_o=None):
    T, D = x.shape
    F = w_d.shape[0]

    def body(d_ref, x_ref, g_ref, gu_ref, wgu_ref, wd_ref, *rest):
        if w_o is not None:
            wo_ref, rest = rest[0], rest[1:]
        dx_ref, a_ref, dgu_ref, h_ref, d16_ref, dg_ref = rest[:6]

        @pl.when(pl.program_id(0) == 0)
        def _():
            dg_ref[...] = jnp.zeros_like(dg_ref)

        d = d_ref[...]
        d16 = d.astype(BF16)
        d16_ref[...] = d16
        dh = jnp.zeros((tm, D), F32)
        for c0, n in _ffn_chunks(F):
            g = gu_ref[:, c0 : c0 + n]
            u = gu_ref[:, F + c0 : F + c0 + n]
            da = _dot_nt(d16, wd_ref[c0 : c0 + n, :])
            s = _sigmoid(g)
            sg = g * s
            a_ref[:, c0 : c0 + n] = (sg * u).astype(BF16)
            dg16 = (da * u * (s + sg * (1.0 - s))).astype(BF16)
            du16 = (da * sg).astype(BF16)
            dgu_ref[:, c0 : c0 + n] = dg16
            dgu_ref[:, F + c0 : F + c0 + n] = du16
            dh = dh + _dot_nt(dg16, wgu_ref[:, c0 : c0 + n]) + _dot_nt(du16, wgu_ref[:, F + c0 : F + c0 + n])
        xt = x_ref[...]
        r = _rms(xt)
        gn = g_ref[...]
        h_ref[...] = ((xt * r) * gn).astype(BF16)
        dx, dgn = _rms_bwd(xt, r, gn, dh)
        dg_ref[0:1, :] += dgn
        dxi = d + dx
        dx_ref[...] = dxi
        if w_o is not None:
            dxi16_ref, dao_ref = rest[6:8]
            dxi16 = dxi.astype(BF16)
            dxi16_ref[...] = dxi16
            dao_ref[...] = _dot_nt(dxi16, wo_ref[...]).astype(BF16)

    tile = pl.BlockSpec((tm, D), lambda i: (i, 0))
    args = [dxo, x, gain, gu, w_gu, w_d]
    wide = lambda n: pl.BlockSpec((tm, n), lambda i: (i, 0))
    in_specs = [tile, tile, _resident((1, D)), wide(2 * F), _resident((D, 2 * F)), _resident((F, D))]
    out_specs = [tile, wide(F), wide(2 * F), tile, tile, pl.BlockSpec((8, D), lambda i: (0, 0))]
    out_shape = [
        jax.ShapeDtypeStruct((T, D), F32),
        jax.ShapeDtypeStruct((T, F), BF16),
        jax.ShapeDtypeStruct((T, 2 * F), BF16),
        jax.ShapeDtypeStruct((T, D), BF16),
        jax.ShapeDtypeStruct((T, D), BF16),
        jax.ShapeDtypeStruct((8, D), F32),
    ]
    if w_o is not None:
        args.append(w_o)
        in_specs.append(_resident(w_o.shape))
        out_specs += [tile, pl.BlockSpec((tm, w_o.shape[0]), lambda i: (i, 0))]
        out_shape += [jax.ShapeDtypeStruct((T, D), BF16), jax.ShapeDtypeStruct((T, w_o.shape[0]), BF16)]
    return _call(body, after=after, args=args, name="ffn_bwd", grid=(T // tm,), in_specs=in_specs, out_specs=out_specs,
                 out_shape=out_shape)


def wgrad(a, b, *, name, a_cols=0, b_cols=0, group=1, flat=False, tk, out_dtype=BF16, after=None):
    T, K = a.shape
    J = 1
    if a_cols:
        K = a_cols
        J = a.shape[1] // K
        a_spec = pl.BlockSpec((tk, K), lambda j, k: (k, j))
    else:
        a_spec = pl.BlockSpec((tk, K), lambda j, k: (k, 0))
    if b_cols:
        N = b_cols * group
        J = b.shape[1] // N
        b_spec = pl.BlockSpec((tk, N), lambda j, k: (k, j))
    else:
        N = b.shape[1]
        b_spec = pl.BlockSpec((tk, N), lambda j, k: (k, 0))
    nk = T // tk
    if flat:
        o_spec, o_shape = pl.BlockSpec((K, N), lambda j, k: (0, j)), (K, J * N)
    elif group > 1:
        o_spec, o_shape = pl.BlockSpec((group, K, b_cols), lambda j, k: (j, 0, 0)), (J * group, K, b_cols)
    else:
        o_spec, o_shape = pl.BlockSpec((None, K, N), lambda j, k: (j, 0, 0)), (J, K, N)

    def body(a_ref, b_ref, o_ref, acc_ref):
        k = pl.program_id(1)

        @pl.when(k == 0)
        def _():
            acc_ref[...] = jnp.zeros_like(acc_ref)

        acc_ref[...] += _dot_tn(a_ref[...], b_ref[...])

        @pl.when(k == nk - 1)
        def _():
            if group > 1 and not flat:
                for i in range(group):
                    o_ref[i] = acc_ref[:, i * b_cols : (i + 1) * b_cols].astype(out_dtype)
            else:
                o_ref[...] = acc_ref[...].astype(out_dtype)

    outs, _ = _call(
        body,
        after=after,
        name=name,
        grid=(J, nk),
        in_specs=[a_spec, b_spec],
        out_specs=[o_spec],
        out_shape=[jax.ShapeDtypeStruct(o_shape, out_dtype)],
        args=(a, b),
        scratch=[pltpu.VMEM((K, N), F32)],
    )
    return outs[0]


def _seg(xs, lo):
    s_lo = [jnp.sum(jnp.where(lo, x, 0.0), axis=-1, keepdims=True) for x in xs]
    s_hi = [jnp.sum(jnp.where(lo, 0.0, x), axis=-1, keepdims=True) for x in xs]
    return [jnp.where(lo, a, b) for a, b in zip(s_lo, s_hi)]


def _head_norm(xs, gains, lo):
    rs = [lax.rsqrt(s * (1.0 / HEAD_DIM) + EPS) for s in _seg([x * x for x in xs], lo)]
    return [(x * r) * g for x, r, g in zip(xs, rs, gains)], rs


def _head_norm_bwd(xs, rs, gains, dys, lo):
    xns = [x * r for x, r in zip(xs, rs)]
    dxns = [dy * g for dy, g in zip(dys, gains)]
    means = [s * (1.0 / HEAD_DIM) for s in _seg([a * b for a, b in zip(dxns, xns)], lo)]
    dxs = [r * (dxn - xn * m) for r, dxn, xn, m in zip(rs, dxns, xns, means)]
    return dxs, [jnp.sum(dy * xn, axis=0, keepdims=True) for dy, xn in zip(dys, xns)]


def _swap_halves(x):
    return pltpu.roll(x, HEAD_DIM, 1)


def qkv_proj(x, gain, w, qg, kg, *, tm):
    T, D = x.shape
    N = w.shape[1]
    kvw = N_KV_HEADS * HEAD_DIM
    nqt, nkt = D // LANES, kvw // LANES

    def body(x_ref, g_ref, w_ref, qg_ref, kg_ref, qkv_ref, q_ref, kd_ref, vd_ref):
        xt = x_ref[...]
        h = ((xt * _rms(xt)) * g_ref[...]).astype(BF16)
        qkv = _dot(h, w_ref[...])
        qkv_ref[...] = qkv
        lo = lax.broadcasted_iota(jnp.int32, (1, LANES), 1) < HEAD_DIM
        tiles = [qkv[:, t * LANES : (t + 1) * LANES] for t in range(nqt + nkt)]
        normed, _ = _head_norm(tiles, [qg_ref[...]] * nqt + [kg_ref[...]] * nkt, lo)
        for t in range(nqt):
            q_ref[:, t * LANES : (t + 1) * LANES] = (normed[t] * SCALE).astype(BF16)
        for t in range(nkt):
            kn = normed[nqt + t]
            v = qkv[:, D + kvw + t * LANES : D + kvw + (t + 1) * LANES]
            for src, dst in ((kn, kd_ref), (v, vd_ref)):
                sw = _swap_halves(src)
                dst[:, 2 * t * LANES : (2 * t + 1) * LANES] = jnp.where(lo, src, sw).astype(BF16)
                dst[:, (2 * t + 1) * LANES : (2 * t + 2) * LANES] = jnp.where(lo, sw, src).astype(BF16)

    row = lambda i: (i, 0)
    return pl.pallas_call(
        body,
        name="qkv_proj",
        grid=(T // tm,),
        in_specs=[pl.BlockSpec((tm, D), row), _resident((1, D)), _resident((D, N)), _resident((1, LANES)), _resident((1, LANES))],
        out_specs=[pl.BlockSpec((tm, N), row), pl.BlockSpec((tm, D), row), pl.BlockSpec((tm, 2 * kvw), row), pl.BlockSpec((tm, 2 * kvw), row)],
        out_shape=[
            jax.ShapeDtypeStruct((T, N), F32),
            jax.ShapeDtypeStruct((T, D), BF16),
            jax.ShapeDtypeStruct((T, 2 * kvw), BF16),
            jax.ShapeDtypeStruct((T, 2 * kvw), BF16),
        ],
        compiler_params=_params(1),
    )(x, gain, w, qg, kg)


def _attn_tables(sinks, n_q_heads):
    P = n_q_heads // N_KV_HEADS // 2
    h = jnp.arange(1, n_q_heads + 1, dtype=F32)
    slopes = jnp.exp2(-8.0 * h / n_q_heads).reshape(N_KV_HEADS, P, 1, 2, 1)
    qi = jnp.arange(BLOCK)[:, None]
    kj = jnp.arange(BLOCK)[None, :]
    dist = jnp.where(kj <= qi, qi - kj, qi + BLOCK - kj).astype(F32)
    shape = (N_KV_HEADS, P, BLOCK, 2, BLOCK)
    bias = jnp.broadcast_to(-slopes * dist[None, None, :, None, :], shape)
    sink = jnp.broadcast_to(sinks.astype(F32).reshape(N_KV_HEADS, P, 1, 2, 1), shape)
    return bias.reshape(N_KV_HEADS, P * BLOCK, 2 * BLOCK), sink.reshape(N_KV_HEADS, P * BLOCK, 2 * BLOCK)


def _attn_specs(D, nb):
    kvw2 = 2 * N_KV_HEADS * HEAD_DIM
    cur = lambda b, i: (b * nb + i, 0)
    prev = lambda b, i: (jnp.maximum(b * nb + i - 1, 0), 0)
    return [
        pl.BlockSpec((BLOCK, D), cur),
        pl.BlockSpec((BLOCK, kvw2), cur),
        pl.BlockSpec((BLOCK, kvw2), prev),
        pl.BlockSpec((BLOCK, kvw2), cur),
        pl.BlockSpec((BLOCK, kvw2), prev),
    ]


def _attn_operands(kh, P, lo, q_ref, kc_ref, kp_ref, vc_ref, vp_ref):
    sl = slice(kh * LANES, (kh + 1) * LANES)

    def cat(prev_ref, cur_ref):
        d = jnp.concatenate([prev_ref[:, sl], cur_ref[:, sl]], axis=0)
        z = jnp.zeros_like(d)
        return jnp.concatenate([jnp.where(lo, d, z), jnp.where(lo, z, d)], axis=0)

    qt = jnp.concatenate([q_ref[:, (kh * P + pr) * LANES : (kh * P + pr + 1) * LANES] for pr in range(P)], axis=0)
    return qt, cat(kp_ref, kc_ref), cat(vp_ref, vc_ref)


def _attn_exp(s_all, bias, sink, tri, first):
    out = []
    for par in range(2):
        c0 = 2 * par * BLOCK
        s = jnp.where(tri, s_all[:, c0 + BLOCK : c0 + 2 * BLOCK], jnp.where(first, NEG, s_all[:, c0 : c0 + BLOCK]))
        s = s + bias[:, par * BLOCK : (par + 1) * BLOCK]
        snk = sink[:, par * BLOCK : (par + 1) * BLOCK]
        m = jnp.maximum(jnp.max(s, axis=-1, keepdims=True), snk)
        out.append((jnp.exp(s - m), jnp.exp(snk - m)))
    return out


def _unfold(x, tri):
    z = jnp.zeros_like(x)
    return jnp.concatenate([jnp.where(tri, z, x), jnp.where(tri, x, z)], axis=1)


def _attn_masks(R):
    lane = lax.broadcasted_iota(jnp.int32, (1, LANES), 1)
    row = lax.broadcasted_iota(jnp.int32, (R, BLOCK), 0) & (BLOCK - 1)
    col = lax.broadcasted_iota(jnp.int32, (R, BLOCK), 1)
    return lane, lane < HEAD_DIM, col <= row


def attn_fwd(q16, kd, vd, bias, sink, *, seq, n_seq):
    T, D = q16.shape
    nb = seq // BLOCK
    P = D // HEAD_DIM // N_KV_HEADS // 2
    R = P * BLOCK
    KV = range(N_KV_HEADS)

    def body(q_ref, kc_ref, kp_ref, vc_ref, vp_ref, bias_ref, sink_ref, o_ref):
        first = pl.program_id(1) == 0
        _, lo, tri = _attn_masks(R)
        ops = [_attn_operands(kh, P, lo, q_ref, kc_ref, kp_ref, vc_ref, vp_ref) for kh in KV]
        s_all = [_dot_nt(ops[kh][0], ops[kh][1]) for kh in KV]
        ex = [_attn_exp(s_all[kh], bias_ref[kh], sink_ref[kh], tri, first) for kh in KV]
        den = [[jnp.sum(e, axis=-1, keepdims=True) + es for e, es in ex[kh]] for kh in KV]
        lhs = [jnp.concatenate([_unfold(e, tri) for e, _ in ex[kh]], axis=1).astype(BF16) for kh in KV]
        o = [_dot(lhs[kh], ops[kh][2]) for kh in KV]
        for kh in KV:
            out = o[kh] / jnp.where(lo, den[kh][0], den[kh][1])
            for pr in range(P):
                t = kh * P + pr
                o_ref[:, t * LANES : (t + 1) * LANES] = out[pr * BLOCK : (pr + 1) * BLOCK, :].astype(BF16)

    return pl.pallas_call(
        body,
        name="attn_fwd",
        grid=(n_seq, nb),
        in_specs=_attn_specs(D, nb) + [_resident((N_KV_HEADS, R, 2 * BLOCK)), _resident((N_KV_HEADS, R, 2 * BLOCK))],
        out_specs=pl.BlockSpec((BLOCK, D), lambda b, i: (b * nb + i, 0)),
        out_shape=jax.ShapeDtypeStruct((T, D), BF16),
        compiler_params=_params(2),
    )(q16, kd, kd, vd, vd, bias, sink)


def attn_bwd(q16, kd, vd, do, bias, sink, *, seq, n_seq):
    T, D = q16.shape
    kvw2 = 2 * N_KV_HEADS * HEAD_DIM
    nb = seq // BLOCK
    G = D // HEAD_DIM // N_KV_HEADS
    P = G // 2
    R = P * BLOCK
    KV = range(N_KV_HEADS)

    def body(q_ref, kc_ref, kp_ref, vc_ref, vp_ref, do_ref, bias_ref, sink_ref,
             dq_ref, dkc_ref, dkp_ref, dvc_ref, dvp_ref, dsink_ref):
        first = pl.program_id(1) == 0

        @pl.when(jnp.logical_and(pl.program_id(0) == 0, first))
        def _():
            dsink_ref[...] = jnp.zeros_like(dsink_ref)

        lane, lo, tri = _attn_masks(R)
        ops = [_attn_operands(kh, P, lo, q_ref, kc_ref, kp_ref, vc_ref, vp_ref) for kh in KV]
        do16 = [jnp.concatenate([do_ref[:, (kh * P + pr) * LANES : (kh * P + pr + 1) * LANES] for pr in range(P)], axis=0)
                for kh in KV]
        s_all = [_dot_nt(ops[kh][0], ops[kh][1]) for kh in KV]
        dp_all = [_dot_nt(do16[kh], ops[kh][2]) for kh in KV]
        ex = [_attn_exp(s_all[kh], bias_ref[kh], sink_ref[kh], tri, first) for kh in KV]
        den = [[jnp.sum(e, axis=-1, keepdims=True) for e, _ in ex[kh]] for kh in KV]
        dsink = jnp.zeros((1, LANES), F32)
        pf, dsf = [], []
        for kh in KV:
            ps_, ds_ = [], []
            for par in range(2):
                e, es = ex[kh][par]
                inv = 1.0 / (den[kh][par] + es)
                p = e * inv
                c0 = 2 * par * BLOCK
                dp = jnp.where(tri, dp_all[kh][:, c0 + BLOCK : c0 + 2 * BLOCK], dp_all[kh][:, c0 : c0 + BLOCK])
                delta = jnp.sum(p * dp, axis=-1, keepdims=True)
                ds_.append(_unfold(p * (dp - delta), tri))
                ps_.append(_unfold(p, tri))
                dsr = -((es * inv) * delta)
                for pr in range(P):
                    hq = kh * G + 2 * pr + par
                    tot = jnp.sum(dsr[pr * BLOCK : (pr + 1) * BLOCK, :], axis=0, keepdims=True)
                    dsink = dsink + jnp.where(lane == hq, tot, 0.0)
            pf.append(jnp.concatenate(ps_, axis=1).astype(BF16))
            dsf.append(jnp.concatenate(ds_, axis=1).astype(BF16))
        dq = [_dot(dsf[kh], ops[kh][1]) for kh in KV]
        dk = [_dot_tn(dsf[kh], ops[kh][0]) for kh in KV]
        dv = [_dot_tn(pf[kh], do16[kh]) for kh in KV]
        dsink_ref[0:1, :] += dsink
        for kh in KV:
            sl = slice(kh * LANES, (kh + 1) * LANES)
            for pr in range(P):
                t = kh * P + pr
                dq_ref[:, t * LANES : (t + 1) * LANES] = dq[kh][pr * BLOCK : (pr + 1) * BLOCK, :]
            for full, prev_ref, cur_ref in ((dk[kh], dkp_ref, dkc_ref), (dv[kh], dvp_ref, dvc_ref)):
                dup = jnp.where(lo, full[: 2 * BLOCK, :], full[2 * BLOCK :, :])
                prev_ref[:, sl] = dup[:BLOCK, :].astype(BF16)
                cur_ref[:, sl] = dup[BLOCK:, :].astype(BF16)

    cur = lambda b, i: (b * nb + i, 0)
    kv_spec = pl.BlockSpec((BLOCK, kvw2), cur)
    kv_shape = jax.ShapeDtypeStruct((T, kvw2), BF16)
    return pl.pallas_call(
        body,
        name="attn_bwd",
        grid=(n_seq, nb),
        in_specs=_attn_specs(D, nb)
        + [pl.BlockSpec((BLOCK, D), cur), _resident((N_KV_HEADS, R, 2 * BLOCK)), _resident((N_KV_HEADS, R, 2 * BLOCK))],
        out_specs=[pl.BlockSpec((BLOCK, D), cur), kv_spec, kv_spec, kv_spec, kv_spec, pl.BlockSpec((8, LANES), lambda b, i: (0, 0))],
        out_shape=[jax.ShapeDtypeStruct((T, D), F32), kv_shape, kv_shape, kv_shape, kv_shape, jax.ShapeDtypeStruct((8, LANES), F32)],
        compiler_params=_params(2),
    )(q16, kd, kd, vd, vd, do, bias, sink)


def qkv_bwd(dq, dkc, dkp, dvc, dvp, qkv, dres, x, gain, w_qkv, qg, kg, *, seq, tm):
    T, D = x.shape
    kvw2 = dkc.shape[1]
    kvw = kvw2 // 2
    nqt, nkt = D // LANES, kvw // LANES
    nb = seq // BLOCK
    nbt = tm // BLOCK
    assert nb % nbt == 0
    n = T // tm

    def body(dq_ref, dkc_ref, dkpa_ref, dkpb_ref, dvc_ref, dvpa_ref, dvpb_ref, qkv_ref, dres_ref, x_ref, g_ref, w_ref,
             qg_ref, kg_ref, dx_ref, dqkv_ref, h_ref, dg_ref, hg_ref):
        i = pl.program_id(0)

        @pl.when(i == 0)
        def _():
            dg_ref[...] = jnp.zeros_like(dg_ref)
            hg_ref[...] = jnp.zeros_like(hg_ref)

        lo = lax.broadcasted_iota(jnp.int32, (1, LANES), 1) < HEAD_DIM
        last = ((i + 1) * nbt) % nb == 0
        up = lambda ref: ref[...].astype(F32)

        def with_next(cur_ref, own_ref, next_ref):
            return up(cur_ref) + jnp.concatenate([up(own_ref)[BLOCK:, :], jnp.where(last, 0.0, up(next_ref))], axis=0)

        dkd = with_next(dkc_ref, dkpa_ref, dkpb_ref)
        dvd = with_next(dvc_ref, dvpa_ref, dvpb_ref)

        def undup(d, t):
            a, b = d[:, 2 * t * LANES : (2 * t + 1) * LANES], d[:, (2 * t + 1) * LANES : (2 * t + 2) * LANES]
            return jnp.where(lo, a + _swap_halves(a), b + _swap_halves(b))

        tiles = [qkv_ref[:, t * LANES : (t + 1) * LANES] for t in range(nqt + nkt)]
        gains = [qg_ref[...]] * nqt + [kg_ref[...]] * nkt
        dys = [dq_ref[:, t * LANES : (t + 1) * LANES] * SCALE for t in range(nqt)] + [undup(dkd, t) for t in range(nkt)]
        _, rs = _head_norm(tiles, gains, lo)
        dxs, dgs = _head_norm_bwd(tiles, rs, gains, dys, lo)
        for t in range(nqt + nkt):
            dqkv_ref[:, t * LANES : (t + 1) * LANES] = dxs[t].astype(BF16)
        for t in range(nkt):
            dqkv_ref[:, D + kvw + t * LANES : D + kvw + (t + 1) * LANES] = undup(dvd, t).astype(BF16)
        hg_ref[0:1, :] += functools.reduce(lambda a, b: a + b, dgs[:nqt])
        hg_ref[1:2, :] += functools.reduce(lambda a, b: a + b, dgs[nqt:])
        dh = _dot_nt(dqkv_ref[...], w_ref[...])
        xt = x_ref[...]
        r = _rms(xt)
        gn = g_ref[...]
        h_ref[...] = ((xt * r) * gn).astype(BF16)
        dx, dgn = _rms_bwd(xt, r, gn, dh)
        dg_ref[0:1, :] += dgn
        dx_ref[...] = dres_ref[...] + dx

    row = lambda i: (i, 0)
    nxt_a = pl.BlockSpec((tm, kvw2), row)
    nxt_b = pl.BlockSpec((BLOCK, kvw2), lambda i: (jnp.minimum((i + 1) * nbt, n * nbt - 1), 0))
    return pl.pallas_call(
        body,
        name="qkv_bwd",
        grid=(n,),
        in_specs=[
            pl.BlockSpec((tm, D), row),
            pl.BlockSpec((tm, kvw2), row),
            nxt_a,
            nxt_b,
            pl.BlockSpec((tm, kvw2), row),
            nxt_a,
            nxt_b,
            pl.BlockSpec((tm, D + kvw2), row),
            pl.BlockSpec((tm, D), row),
            pl.BlockSpec((tm, D), row),
            _resident((1, D)),
            _resident((D, D + kvw2)),
            _resident((1, LANES)),
            _resident((1, LANES)),
        ],
        out_specs=[
            pl.BlockSpec((tm, D), row),
            pl.BlockSpec((tm, D + kvw2), row),
            pl.BlockSpec((tm, D), row),
            pl.BlockSpec((8, D), lambda i: (0, 0)),
            pl.BlockSpec((8, LANES), lambda i: (0, 0)),
        ],
        out_shape=[
            jax.ShapeDtypeStruct((T, D), F32),
            jax.ShapeDtypeStruct((T, D + kvw2), BF16),
            jax.ShapeDtypeStruct((T, D), BF16),
            jax.ShapeDtypeStruct((8, D), F32),
            jax.ShapeDtypeStruct((8, LANES), F32),
        ],
        compiler_params=_params(1),
    )(dq, dkc, dkp, dkp, dvc, dvp, dvp, qkv, dres, x, gain, w_qkv, qg, kg)


def local_step(x, target, gains, w, *, seq, tm=256, tm_ffn=256, tm_conv=512, tk=2048, shards=None, ex=None):
    T, D = x.shape
    n_seq = T // seq
    nm, nf, qgain, kgain, sinks = gains
    H = D // HEAD_DIM
    tk, tk_long = min(tk, T), min(2 * tk, T)
    qg2, kg2 = jnp.tile(qgain, (1, 2)), jnp.tile(kgain, (1, 2))
    bias, sinkcol = _attn_tables(sinks, H)

    dist = shards is not None
    w = dict(w)

    plan = _Gather([shards["w_gu"][0], shards["w_d"][0]]) if dist else None
    (x1, bcx, y_conv, z16), got = conv_fwd(x, nm[0:1], w["w_in"], w["cw"], w["w_out"], seq=seq, tm=tm_conv, plan=plan)
    if dist:
        w["w_gu"], w["w_d"] = [cols_from_shards(got[0]), None], [got[1].reshape(-1, D), None]
    plan = _Gather([shards["w_qkv"], shards["w_o"], shards["w_gu"][1], shards["w_d"][1]]) if dist else None
    (x2, gu0), got = ffn_fwd(x1, nf[0:1], w["w_gu"][0], w["w_d"][0], tm=2 * tm_ffn, plan=plan)
    if dist:
        w["w_qkv"], w["w_o"] = cols_from_shards(got[0]), got[1].reshape(D, D)
        w["w_gu"][1], w["w_d"][1] = cols_from_shards(got[2]), got[3].reshape(-1, D)
    qkv, q16, kd, vd = qkv_proj(x2, nm[1:2], w["w_qkv"], qg2, kg2, tm=tm_conv)
    ao = attn_fwd(q16, kd, vd, bias, sinkcol, seq=seq, n_seq=n_seq)
    (x3, dx4, gu1, sse), _ = ffn_fwd(x2, nf[1:2], w["w_gu"][1], w["w_d"][1], tm=tm_ffn, attn=(ao, w["w_o"]), target=target)

    by_dest = lambda a: a.reshape(N_DEV, -1, a.shape[-1])
    gu_cols = 2 * MXU_TILE

    def send(name, *entries):
        if ex is None:
            return None
        items = [(a, False, key, (N_DEV,) + (() if layers is None else (layers,)) + a.shape[1:], layer)
                 for a, key, layer, layers in entries]
        return ex.start(items, name=name)

    (dx3, a16, dgu, h16, d16, dnf1, dx3_16, dao), _ = ffn_bwd(
        dx4, x3, nf[1:2], gu1, w["w_gu"][1], w["w_d"][1], tm=tm, w_o=w["w_o"])
    g_gu1 = shards_from_cols(wgrad(h16, dgu, name="wgrad_gu1", b_cols=gu_cols, flat=True, tk=tk_long))
    g_d1 = by_dest(wgrad(a16, d16, name="wgrad_d1", a_cols=a16.shape[1] // 2, tk=tk))
    g_o = by_dest(wgrad(ao, dx3_16, name="wgrad_o", tk=tk))
    dq, dkc, dkp, dvc, dvp, dsinks = attn_bwd(q16, kd, vd, dao, bias, sinkcol, seq=seq, n_seq=n_seq)
    dx2, dqkv16, h16, dnm1, dgains = qkv_bwd(dq, dkc, dkp, dvc, dvp, qkv, dx3, x2, nm[1:2], w["w_qkv"], qg2, kg2, seq=seq, tm=tm_conv)
    g_qkv = shards_from_cols(wgrad(h16, dqkv16, name="wgrad_qkv", tk=tk)[0])
    tok = send("exchange_layer1", (g_gu1, "w_gu", 1, 2), (g_d1, "w_d", 1, 2), (g_o, "w_o", None, None),
               (g_qkv, "w_qkv", None, None))
    (dx1, a16, dgu, h16, d16, dnf0), _ = ffn_bwd(dx2, x1, nf[0:1], gu0, w["w_gu"][0], w["w_d"][0], tm=tm, after=tok)
    g_gu0 = shards_from_cols(wgrad(h16, dgu, name="wgrad_gu0", b_cols=gu_cols, flat=True, tk=tk_long))
    g_d0 = by_dest(wgrad(a16, d16, name="wgrad_d0", a_cols=a16.shape[1] // 2, tk=tk))
    tok = send("exchange_ffn0", (g_gu0, "w_gu", 0, 2), (g_d0, "w_d", 0, 2))
    (gx, dbcx, h16, d16, dcw, dnm0), _ = conv_bwd(
        dx1, x, nm[0:1], bcx, y_conv, w["cw"], w["w_in"], w["w_out"], seq=seq, tm=tm_conv, after=tok)
    g_out = by_dest(wgrad(z16, d16, name="wgrad_out", tk=tk))
    g_cw = dcw[0:3].reshape(3, N_DEV, D // N_DEV).transpose(1, 0, 2)
    tok = send("exchange_out", (g_out, "w_out", None, None), (g_cw, "cw", None, None))
    g_in = wgrad(h16, dbcx, name="wgrad_in", b_cols=3 * D // N_DEV, group=2, tk=tk_long, after=tok)
    g = dict(w_in=g_in, cw=g_cw, w_out=g_out, w_o=g_o, w_qkv=g_qkv, w_gu=[g_gu0, g_gu1], w_d=[g_d0, g_d1])
    small = dict(nm0=dnm0, nm1=dnm1, nf0=dnf0, nf1=dnf1, gains=dgains, sinks=dsinks)
    return sse, gx, g, small


def _adamw_math(g, w, m, v):
    m = ADAM_B1 * m + (1.0 - ADAM_B1) * g
    v = ADAM_B2 * v + (1.0 - ADAM_B2) * (g * g)
    m_hat = m / (1.0 - ADAM_B1 ** ADAM_STEP)
    v_hat = v / (1.0 - ADAM_B2 ** ADAM_STEP)
    delta = -ADAM_LR * (m_hat / (jnp.sqrt(v_hat) + ADAM_EPS) + ADAM_WD * w)
    return delta, m, v


def adamw(parts, owns, w, m, v, *, name, after=None):
    n, LR, C = parts.shape
    L = len(owns)
    R = LR // L
    tr = R
    for cand in (256, 128, 88, 64, 32, 16, 8):
        if R > cand and R % cand == 0:
            tr = cand
            break
    per_layer = R // tr
    extra = [] if after is None else [after]

    def body(me_ref, p_ref, *rest):
        own_refs, (w_ref, m_ref, v_ref) = rest[:L], rest[L : L + 3]
        g_ref, d_ref, mo_ref, vo_ref = rest[L + 3 + len(extra) :]
        layer = pl.program_id(0) // per_layer
        mine = own_refs[0][...].astype(F32)
        for j in range(1, L):
            mine = jnp.where(layer == j, own_refs[j][...].astype(F32), mine)
        g = None
        for s in range(n):
            share = jnp.where(me_ref[0] == s, mine, p_ref[s].astype(F32))
            g = share if g is None else g + share
        g_ref[...] = g
        d_ref[...], mo_ref[...], vo_ref[...] = _adamw_math(g, w_ref[...], m_ref[...], v_ref[...])

    blk = pl.BlockSpec((tr, C), lambda i, me: (i, 0))
    own_specs = [pl.BlockSpec((None, tr, C), lambda i, me: (me[0], i % per_layer, 0)) if o.ndim == 3
                 else pl.BlockSpec((tr, C), lambda i, me: (i % per_layer, 0)) for o in owns]
    me = (4 * lax.axis_index("x") + 2 * lax.axis_index("y") + lax.axis_index("c")).astype(jnp.int32).reshape(1)
    return pl.pallas_call(
        body,
        name=name,
        grid_spec=pltpu.PrefetchScalarGridSpec(
            num_scalar_prefetch=1,
            grid=(LR // tr,),
            in_specs=[pl.BlockSpec((n, tr, C), lambda i, me: (0, i, 0))] + own_specs + [blk, blk, blk] + _any_specs(len(extra)),
            out_specs=[blk] * 4,
        ),
        out_shape=[jax.ShapeDtypeStruct((LR, C), F32)] * 4,
        compiler_params=_params(1),
    )(me, parts, *owns, w, m, v, *extra)


def pack_small(small, sse, D):
    W = max(D, 2 * LANES)

    def body(nm0, nm1, nf0, nf1, gains, sinks, sse_ref, o_ref):
        o_ref[...] = jnp.zeros_like(o_ref)
        o_ref[0:1, :D] = nm0[0:1, :]
        o_ref[1:2, :D] = nm1[0:1, :]
        o_ref[2:3, :D] = nf0[0:1, :]
        o_ref[3:4, :D] = nf1[0:1, :]
        gq = gains[0:1, :] + pltpu.roll(gains[0:1, :], HEAD_DIM, 1)
        gk = gains[1:2, :] + pltpu.roll(gains[1:2, :], HEAD_DIM, 1)
        lane = lax.broadcasted_iota(jnp.int32, (1, LANES), 1)
        o_ref[4:5, :LANES] = jnp.where(lane < HEAD_DIM, gq, gk)
        o_ref[4:5, LANES : 2 * LANES] = sinks[0:1, :]
        o_ref[5:6, :LANES] = sse_ref[0:1, :] * (0.5 / D)

    return pl.pallas_call(
        body,
        name="pack_small",
        out_shape=jax.ShapeDtypeStruct((8, W), F32),
    )(small["nm0"], small["nm1"], small["nf0"], small["nf1"], small["gains"], small["sinks"], sse)


def _pack_small_params(nm, nf, qg, kg, sk, D):
    W = max(D, 2 * LANES)
    row4 = jnp.concatenate([qg.reshape(-1), kg.reshape(-1), jnp.zeros((LANES - 2 * HEAD_DIM,), F32), sk.reshape(-1)])
    row4 = jnp.pad(row4, (0, W - row4.shape[0]))
    rows = [jnp.pad(r, (0, W - D)) for r in (nm[0], nm[1], nf[0], nf[1])] + [row4]
    return jnp.concatenate([jnp.stack(rows), jnp.zeros((3, W), F32)], axis=0)


def _unpack_small(a, D, H):
    nm = a[0:2, :D]
    nf = a[2:4, :D]
    qg = a[4:5, 0:HEAD_DIM]
    kg = a[4:5, HEAD_DIM : 2 * HEAD_DIM]
    sk = a[4:5, LANES : LANES + H]
    return qg, kg, sk, nm, nf


def kernel(x, conv_w_in, conv_w, conv_w_out, attn_w_qkv, attn_q_gain, attn_k_gain, attn_sinks, attn_w_o, norm_mixer, norm_ffn, ffn_w_gate_up, ffn_w_down, loss_target, m_conv_w_in, m_conv_w, m_conv_w_out, m_attn_w_qkv, m_attn_q_gain, m_attn_k_gain, m_attn_sinks, m_attn_w_o, m_norm_mixer, m_norm_ffn, m_ffn_w_gate_up, m_ffn_w_down, v_conv_w_in, v_conv_w, v_conv_w_out, v_attn_w_qkv, v_attn_q_gain, v_attn_k_gain, v_attn_sinks, v_attn_w_o, v_norm_mixer, v_norm_ffn, v_ffn_w_gate_up, v_ffn_w_down):
    n_seq, seq, D = x.shape
    T = n_seq * seq
    H = D // HEAD_DIM
    L = ffn_w_gate_up.shape[0]

    full = run_plan(_Gather([conv_w_in[0].astype(BF16), conv_w[0], conv_w_out[0].astype(BF16)]), name="gather_conv_weights")
    w = dict(w_in=cols_from_shards(full[0]), cw=full[1].transpose(1, 0, 2).reshape(3, D),
             w_out=full[2].reshape(D, D))
    shards = dict(w_gu=[ffn_w_gate_up[l].astype(BF16) for l in range(L)], w_d=[ffn_w_down[l].astype(BF16) for l in range(L)],
                  w_qkv=attn_w_qkv[0].astype(BF16), w_o=attn_w_o[0].astype(BF16))
    gains = (norm_mixer, norm_ffn, attn_q_gain, attn_k_gain, attn_sinks)
    ex = Exchange()
    sse, gx, g, small = local_step(x.reshape(T, D), loss_target.reshape(T, D), gains, w, seq=seq, shards=shards, ex=ex)
    zones, own = ex.wait([g["w_in"]], name="exchange_wait")

    packed = pack_small(small, sse, D)
    token = ex.start([(g["w_in"], False, "w_in", g["w_in"].shape, None),
                      (packed, True, "small", (N_DEV,) + packed.shape, None)], name="exchange_last")

    def flat(a):
        return a.reshape(-1, a.shape[-1])

    big = [conv_w_in, conv_w, conv_w_out, attn_w_qkv, attn_w_o, ffn_w_gate_up, ffn_w_down]
    big_m = [m_conv_w_in, m_conv_w, m_conv_w_out, m_attn_w_qkv, m_attn_w_o, m_ffn_w_gate_up, m_ffn_w_down]
    big_v = [v_conv_w_in, v_conv_w, v_conv_w_out, v_attn_w_qkv, v_attn_w_o, v_ffn_w_gate_up, v_ffn_w_down]
    keys = ["w_in", "cw", "w_out", "w_qkv", "w_o", "w_gu", "w_d"]

    def update(b, zones, own, after=None):
        zone = zones[keys[b]]
        parts = zone.reshape(N_DEV, -1, zone.shape[-1])
        layers = [None] if zone.ndim == 3 else range(zone.shape[1])
        outs = adamw(parts, [own[(keys[b], l)] for l in layers], flat(big[b]), flat(big_m[b]), flat(big_v[b]),
                     name="adamw_" + keys[b], after=after)
        return [o.reshape(big[b].shape) for o in outs]

    res = [None] + [update(b, zones, own, after=token) for b in range(1, 7)]
    zones, own = ex.wait([r[0] for r in res[1:]], name="exchange_last_wait")
    res[0] = update(0, zones, own)
    sw = _pack_small_params(norm_mixer, norm_ffn, attn_q_gain, attn_k_gain, attn_sinks, D)
    sm = _pack_small_params(m_norm_mixer, m_norm_ffn, m_attn_q_gain, m_attn_k_gain, m_attn_sinks, D)
    sv = _pack_small_params(v_norm_mixer, v_norm_ffn, v_attn_q_gain, v_attn_k_gain, v_attn_sinks, D)
    souts = adamw(zones["small"], [own[("small", None)]], sw, sm, sv, name="adamw_small")
    sres = [_unpack_small(o, D, H) for o in souts]
    loss = souts[0][5, 0]

    def ordered(i):
        r, s = [r[i] for r in res], sres[i]
        return [r[0], r[1], r[2], r[3], s[0], s[1], s[2], r[4], s[3], s[4], r[5], r[6]]

    return (loss, gx.reshape(n_seq, seq, D), *ordered(0), *ordered(1), *ordered(2), *ordered(3))
```

```python
import functools
import math

import jax
import jax.numpy as jnp
from jax import lax
from jax.experimental import pallas as pl
from jax.experimental.pallas import tpu as pltpu

F32 = jnp.float32
BF16 = jnp.bfloat16

EPS = 1e-6
HEAD_DIM = 64
N_KV_HEADS = 4
BLOCK = 128
LANES = 128
N_DEV = 8
NEG = -1e30
SCALE = 1.0 / math.sqrt(HEAD_DIM)

ADAM_LR = 0.001
ADAM_B1 = 0.9
ADAM_B2 = 0.999
ADAM_EPS = 1e-08
ADAM_WD = 0.01
ADAM_STEP = 10

V7X_VMEM_BYTES = 64 * 1024 * 1024
VMEM_LIMIT = V7X_VMEM_BYTES - 2 * 1024 * 1024
MESH = pl.DeviceIdType.MESH

_NT = (((1,), (1,)), ((), ()))
_TN = (((0,), (0,)), ((), ()))


def _params(n_grid):
    return pltpu.CompilerParams(dimension_semantics=("arbitrary",) * n_grid, vmem_limit_bytes=VMEM_LIMIT)


def _resident(shape):
    nd = len(shape)
    return pl.BlockSpec(shape, lambda *_: (0,) * nd, pipeline_mode=pl.Buffered(1))


def _rms(x):
    return lax.rsqrt(jnp.mean(x * x, axis=-1, keepdims=True) + EPS)


def _rms_bwd(x, r, gain, dh):
    xn = x * r
    dxn = dh * gain
    dx = r * (dxn - xn * jnp.mean(dxn * xn, axis=-1, keepdims=True))
    return dx, jnp.sum(dh * xn, axis=0, keepdims=True)


def _dot(a, b):
    return jnp.dot(a, b, preferred_element_type=F32)


def _dot_nt(a, b):
    return lax.dot_general(a, b, _NT, preferred_element_type=F32)


def _dot_tn(a, b):
    return lax.dot_general(a, b, _TN, preferred_element_type=F32)


def _place():
    return lax.axis_index("x"), lax.axis_index("y"), lax.axis_index("c")


def _flip(v, bit):
    return 1 - v if bit else v


def _slot(px, py, pc):
    return 4 * px + 2 * py + pc


class _Gather:
    def __init__(self, shards):
        nt = len(shards)
        self.nt = nt
        self.inputs = list(shards)
        self.out_shapes = [jax.ShapeDtypeStruct((N_DEV,) + s.shape, s.dtype) for s in shards]
        self.scratch = [pltpu.SemaphoreType.DMA((nt, 10)), pltpu.SemaphoreType.DMA((nt, 10)), pltpu.SemaphoreType.DMA((nt,))]
        self.aliases = {}
        self.split = []
        for s in shards:
            rows, tile = s.shape[0], 16 if s.dtype == BF16 else 8
            self.split.append(rows // 2 if rows % (2 * tile) == 0 else rows)

    def phases(self, total):
        assert total >= 8
        return [(0, self.start), (total // 2 - 1, self.second), (total - 4, self.forward), (total - 1, self.finish)]

    def _copies(self, ins, outs, sems):
        send_sems, recv_sems, loc_sems = sems
        x, y, c = _place()
        xn, yn, sib = (1 - x, y, c), (x, 1 - y, c), (x, y, 1 - c)
        i_me, i_xn, i_yn, i_dn = _slot(x, y, c), _slot(1 - x, y, c), _slot(x, 1 - y, c), _slot(1 - x, 1 - y, c)
        j_me, j_xn, j_yn, j_dn = _slot(x, y, 1 - c), _slot(1 - x, y, 1 - c), _slot(x, 1 - y, 1 - c), _slot(1 - x, 1 - y, 1 - c)
        local, start, need1, second, need2, forward, need3 = [], [], [], [], [], [], []
        for t in range(self.nt):
            o, rows, h = outs[t], self.inputs[t].shape[0], self.split[t]
            lo = pl.ds(0, h)
            hi = pl.ds(h, rows - h) if h < rows else None

            def rc(k, src, dst, to, t=t):
                return pltpu.make_async_remote_copy(
                    src_ref=src, dst_ref=dst, send_sem=send_sems.at[t, k], recv_sem=recv_sems.at[t, k], device_id=to,
                    device_id_type=MESH)

            def landed(k, slot, part, frm):
                ref = o.at[slot] if part is None else o.at[slot, part]
                return rc(k, ref, ref, frm)

            local.append(pltpu.make_async_copy(ins[t], o.at[i_me], loc_sems.at[t]))
            start += [rc(0, ins[t], o.at[i_me], sib), rc(1, ins[t].at[lo], o.at[i_me, lo], xn),
                      rc(4, ins[t].at[lo], o.at[i_me, lo], yn)]
            need1.append(landed(1, i_xn, lo, xn))
            second.append(rc(3, o.at[i_xn, lo], o.at[i_xn, lo], yn))
            need2 += [landed(4, i_yn, lo, yn), landed(3, i_dn, lo, yn)]
            if hi is not None:
                start += [rc(2, ins[t].at[hi], o.at[i_me, hi], yn), rc(6, ins[t].at[hi], o.at[i_me, hi], xn)]
                need1.append(landed(2, i_yn, hi, yn))
                second.append(rc(5, o.at[i_yn, hi], o.at[i_yn, hi], xn))
                need2 += [landed(6, i_xn, hi, xn), landed(5, i_dn, hi, xn)]
            forward += [rc(7, o.at[i_xn], o.at[i_xn], sib), rc(8, o.at[i_yn], o.at[i_yn], sib), rc(9, o.at[i_dn], o.at[i_dn], sib)]
            need3 += [landed(0, j_me, None, sib), landed(7, j_xn, None, sib), landed(8, j_yn, None, sib), landed(9, j_dn, None, sib)]
        return local, start, need1, second, need2, forward, need3

    def start(self, ins, outs, sems):
        local, start, *_ = self._copies(ins, outs, sems)
        for cp in local + start:
            cp.start()

    def second(self, ins, outs, sems):
        _, _, need1, second, *_ = self._copies(ins, outs, sems)
        for cp in need1:
            cp.wait_recv()
        for cp in second:
            cp.start()

    def forward(self, ins, outs, sems):
        _, _, _, _, need2, forward, _ = self._copies(ins, outs, sems)
        for cp in need2:
            cp.wait_recv()
        for cp in forward:
            cp.start()

    def finish(self, ins, outs, sems):
        local, start, _, second, _, forward, need3 = self._copies(ins, outs, sems)
        for cp in need3:
            cp.wait_recv()
        for cp in start + second + forward:
            cp.wait_send()
        for cp in local:
            cp.wait()


def _any_specs(n):
    return [pl.BlockSpec(memory_space=pl.ANY)] * n


def run_plan(plan, *, name):
    def body(*refs):
        n_in, n_out = len(plan.inputs), len(plan.out_shapes)
        ins, outs, sems = refs[:n_in], refs[n_in : n_in + n_out], refs[n_in + n_out :]
        for _, phase in plan.phases(8):
            phase(ins, outs, sems)

    return pl.pallas_call(
        body,
        name=name,
        in_specs=_any_specs(len(plan.inputs)),
        out_specs=_any_specs(len(plan.out_shapes)),
        out_shape=plan.out_shapes,
        scratch_shapes=plan.scratch,
        input_output_aliases=plan.aliases,
    )(*plan.inputs)


_HBM = pl.BlockSpec(memory_space=pltpu.HBM)
_SEM = pl.BlockSpec(memory_space=pltpu.SEMAPHORE)
_DATAFLOW = pltpu.SideEffectType.DATAFLOW_SIDE_EFFECTING


class Exchange:
    def __init__(self):
        self.zones = {}
        self.pending = []
        self.sources = []

    def start(self, items, *, name):
        nt = len(items)
        keys = list(dict.fromkeys(it[2] for it in items))
        for a, _, key, shape, _ in items:
            if key not in self.zones:
                self.zones[key] = lax.empty(shape, a.dtype)
        nz = len(keys)

        def body(*refs):
            ins, zones, sems, token = refs[:nt], refs[nt : nt + nz], refs[nt + nz : nt + nz + 2 * nt], refs[-1]
            x, y, c = _place()
            me = _slot(x, y, c)
            for k in range(1, N_DEV):
                px, py, pc = _flip(x, (k >> 2) & 1), _flip(y, (k >> 1) & 1), _flip(c, k & 1)
                for t, (_, whole, key, _, layer) in enumerate(items):
                    zone = zones[keys.index(key)]
                    pltpu.make_async_remote_copy(
                        src_ref=ins[t] if whole else ins[t].at[_slot(px, py, pc)],
                        dst_ref=zone.at[me] if layer is None else zone.at[me, layer],
                        send_sem=sems[2 * t], recv_sem=sems[2 * t + 1], device_id=(px, py, pc), device_id_type=MESH).start()
            token[...] = jnp.zeros_like(token)

        bufs = [pltpu.with_memory_space_constraint(b, pltpu.HBM) for b in [it[0] for it in items] + [self.zones[k] for k in keys]]
        outs = pl.pallas_call(
            body,
            name=name,
            in_specs=[_HBM] * (nt + nz),
            out_specs=[_SEM] * (2 * nt) + [_HBM] * (nt + nz) + [pl.BlockSpec(memory_space=pltpu.VMEM)],
            out_shape=[pltpu.SemaphoreType.DMA(())] * (2 * nt) + [pltpu.HBM(b.shape, b.dtype) for b in bufs]
            + [jax.ShapeDtypeStruct((8, LANES), F32)],
            input_output_aliases={i: 2 * nt + i for i in range(nt + nz)},
            compiler_params=pltpu.CompilerParams(has_side_effects=_DATAFLOW),
        )(*bufs)
        for t, (_, _, key, _, layer) in enumerate(items):
            self.pending.append((outs[2 * t], outs[2 * t + 1], key, layer))
        self.sources += [((it[2], it[4]), a) for it, a in zip(items, outs[2 * nt : 3 * nt])]
        for i, key in enumerate(keys):
            self.zones[key] = outs[3 * nt + i]
        return outs[-1]

    def wait(self, after, *, name):
        pending, keys = self.pending, list(self.zones)
        names, sources = [n for n, _ in self.sources], [a for _, a in self.sources]
        ns, nz, npend = len(sources), len(keys), len(pending)
        self.pending, self.sources = [], []

        def body(*refs):
            zones, sems = refs[ns : ns + nz], refs[ns + nz : ns + nz + 2 * npend]
            x, y, c = _place()
            for i, (_, _, key, layer) in enumerate(pending):
                zone = zones[keys.index(key)]
                rows = pl.ds(0, N_DEV - 1)
                seven = zone.at[rows] if layer is None else zone.at[rows, layer]
                pltpu.make_async_remote_copy(
                    src_ref=seven, dst_ref=seven, send_sem=sems[2 * i], recv_sem=sems[2 * i + 1],
                    device_id=(x, y, c), device_id_type=MESH).wait()

        bufs = list(sources) + [self.zones[k] for k in keys]
        flat_sems = [s for p in pending for s in p[:2]]
        outs = pl.pallas_call(
            body,
            name=name,
            in_specs=[_HBM] * (ns + nz) + [_SEM] * (2 * npend) + _any_specs(len(after)),
            out_specs=[_HBM] * (ns + nz),
            out_shape=[pltpu.HBM(b.shape, b.dtype) for b in bufs],
            input_output_aliases={i: i for i in range(ns + nz)},
            compiler_params=pltpu.CompilerParams(has_side_effects=_DATAFLOW),
        )(*bufs, *flat_sems, *after)
        self.zones = {}
        return dict(zip(keys, outs[ns:])), dict(zip(names, outs[:ns]))


def _call(body, *, name, grid, in_specs, out_specs, out_shape, args, scratch=(), plan=None, after=None):
    if after is not None:
        inner, n_real = body, len(in_specs)
        body = lambda *refs: inner(*refs[:n_real], *refs[n_real + 1 :])
        in_specs, args = list(in_specs) + _any_specs(1), list(args) + [after]
    n_in, n_out, n_scr = len(in_specs), len(out_specs), len(scratch)
    if plan is None:
        outs = pl.pallas_call(
            body, name=name, grid=grid, in_specs=in_specs, out_specs=out_specs, out_shape=out_shape,
            scratch_shapes=list(scratch), compiler_params=_params(len(grid)))(*args)
        return outs, None
    c_in, c_out = len(plan.inputs), len(plan.out_shapes)
    phases = plan.phases(math.prod(grid))

    def full(*refs):
        a, refs = refs[:n_in], refs[n_in:]
        ci, refs = refs[:c_in], refs[c_in:]
        o, refs = refs[:n_out], refs[n_out:]
        co, refs = refs[:c_out], refs[c_out:]
        s, cs = refs[:n_scr], refs[n_scr:]
        step = pl.program_id(0)
        for d in range(1, len(grid)):
            step = step * grid[d] + pl.program_id(d)
        for at, phase in phases:
            if at == 0:
                pl.when(step == 0)(functools.partial(phase, ci, co, cs))
        body(*a, *o, *s)
        for at, phase in phases:
            if at > 0:
                pl.when(step == at)(functools.partial(phase, ci, co, cs))

    outs = pl.pallas_call(
        full,
        name=name,
        grid=grid,
        in_specs=list(in_specs) + _any_specs(c_in),
        out_specs=list(out_specs) + _any_specs(c_out),
        out_shape=list(out_shape) + plan.out_shapes,
        scratch_shapes=list(scratch) + plan.scratch,
        input_output_aliases={n_in + i: n_out + t for i, t in plan.aliases.items()},
        compiler_params=_params(len(grid)),
    )(*args, *plan.inputs)
    return outs[:n_out], outs[n_out:]


def _row_tile(R):
    return 256 if R % 256 == 0 else R


def cols_from_shards(a):
    n, R, C = a.shape
    tr = _row_tile(R)

    def body(i_ref, o_ref):
        for s in range(n):
            o_ref[:, s * C : (s + 1) * C] = i_ref[s]

    return pl.pallas_call(
        body,
        name="cols_from_shards",
        grid=(R // tr,),
        in_specs=[pl.BlockSpec((n, tr, C), lambda i: (0, i, 0))],
        out_specs=pl.BlockSpec((tr, n * C), lambda i: (i, 0)),
        out_shape=jax.ShapeDtypeStruct((R, n * C), a.dtype),
        compiler_params=_params(1),
    )(a)


def shards_from_cols(a):
    R, W = a.shape
    C = W // N_DEV
    tr = _row_tile(R)

    def body(i_ref, o_ref):
        for s in range(N_DEV):
            o_ref[s] = i_ref[:, s * C : (s + 1) * C]

    return pl.pallas_call(
        body,
        name="shards_from_cols",
        grid=(R // tr,),
        in_specs=[pl.BlockSpec((tr, W), lambda i: (i, 0))],
        out_specs=pl.BlockSpec((N_DEV, tr, C), lambda i: (0, i, 0)),
        out_shape=jax.ShapeDtypeStruct((N_DEV, R, C), a.dtype),
        compiler_params=_params(1),
    )(a)


def _shift_down(u, prev8, row, n):
    out = pltpu.roll(u, n, 0)
    for k in range(n):
        out = jnp.where(row == k, prev8[8 - n + k : 8 - n + k + 1, :], out)
    return out


def _shift_up(u, next8, row, n, tm):
    out = pltpu.roll(u, tm - n, 0)
    for k in range(n):
        out = jnp.where(row == tm - n + k, next8[k : k + 1, :], out)
    return out


def conv_fwd(x, gain, w_in, cw, w_out, *, seq, tm, plan=None):
    T, D = x.shape
    tps = seq // tm

    def body(x_ref, g_ref, win_ref, cw_ref, wout_ref, x1_ref, bcx_ref, y_ref, z_ref, carry_ref):
        i = pl.program_id(0)

        @pl.when(i % tps == 0)
        def _():
            carry_ref[...] = jnp.zeros_like(carry_ref)

        xt = x_ref[...]
        h = ((xt * _rms(xt)) * g_ref[...]).astype(BF16)
        bcx = _dot(h, win_ref[...])
        bcx_ref[...] = bcx.astype(BF16)
        b, c, xv = bcx[:, :D], bcx[:, D : 2 * D], bcx[:, 2 * D :]
        u = b * xv
        row = lax.broadcasted_iota(jnp.int32, u.shape, 0)
        prev = carry_ref[...]
        u1 = _shift_down(u, prev, row, 1)
        u2 = _shift_down(u, prev, row, 2)
        carry_ref[...] = u[tm - 8 :, :]
        cwv = cw_ref[...]
        y = cwv[0:1, :] * u2 + cwv[1:2, :] * u1 + cwv[2:3, :] * u
        y_ref[...] = y
        z = (c * y).astype(BF16)
        z_ref[...] = z
        x1_ref[...] = xt + _dot(z, wout_ref[...])

    tile = pl.BlockSpec((tm, D), lambda i: (i, 0))
    return _call(
        body,
        plan=plan,
        args=(x, gain, w_in, cw, w_out),
        name="conv_fwd",
        grid=(T // tm,),
        in_specs=[
            pl.BlockSpec((tm, D), lambda i: (i, 0)),
            _resident((1, D)),
            _resident((D, 3 * D)),
            _resident((3, D)),
            _resident((D, D)),
        ],
        out_specs=[tile, pl.BlockSpec((tm, 3 * D), lambda i: (i, 0)), tile, tile],
        out_shape=[jax.ShapeDtypeStruct((T, D), F32), jax.ShapeDtypeStruct((T, 3 * D), BF16),
                   jax.ShapeDtypeStruct((T, D), F32), jax.ShapeDtypeStruct((T, D), BF16)],
        scratch=[pltpu.VMEM((8, D), F32)],
    )


def conv_bwd(dx1, x, gain, bcx, y, cw, w_in, w_out, *, seq, tm, after=None):
    T, D = x.shape
    n = T // tm
    tps = seq // tm

    def body(d_ref, x_ref, g_ref, bcx_ref, y_ref, cw_ref, win_ref, wout_ref,
             gx_ref, dbcx_ref, h_ref, d16_ref, dcw_ref, dg_ref, carry_ref):
        i = pl.program_id(0)
        t = n - 1 - i

        @pl.when(i == 0)
        def _():
            dcw_ref[...] = jnp.zeros_like(dcw_ref)
            dg_ref[...] = jnp.zeros_like(dg_ref)

        @pl.when(t % tps == tps - 1)
        def _():
            carry_ref[...] = jnp.zeros_like(carry_ref)

        d = d_ref[...]
        d16 = d.astype(BF16)
        d16_ref[...] = d16
        dz = _dot_nt(d16, wout_ref[...])
        bcx = bcx_ref[...].astype(F32)
        b, c, xv = bcx[:, :D], bcx[:, D : 2 * D], bcx[:, 2 * D :]
        u = b * xv
        row = lax.broadcasted_iota(jnp.int32, u.shape, 0)
        cwv = cw_ref[...]
        dc = dz * y_ref[...]
        dy = dz * c
        nxt = carry_ref[...]
        dy1 = _shift_up(dy, nxt, row, 1, tm)
        dy2 = _shift_up(dy, nxt, row, 2, tm)
        carry_ref[...] = dy[0:8, :]
        dcw_ref[0:1, :] += jnp.sum(dy2 * u, axis=0, keepdims=True)
        dcw_ref[1:2, :] += jnp.sum(dy1 * u, axis=0, keepdims=True)
        dcw_ref[2:3, :] += jnp.sum(dy * u, axis=0, keepdims=True)
        du = cwv[2:3, :] * dy + cwv[1:2, :] * dy1 + cwv[0:1, :] * dy2
        dbcx_ref[:, :D] = (du * xv).astype(BF16)
        dbcx_ref[:, D : 2 * D] = dc.astype(BF16)
        dbcx_ref[:, 2 * D :] = (du * b).astype(BF16)
        dh = _dot_nt(dbcx_ref[...], win_ref[...])
        xt = x_ref[...]
        r = _rms(xt)
        gn = g_ref[...]
        h_ref[...] = ((xt * r) * gn).astype(BF16)
        dx, dgn = _rms_bwd(xt, r, gn, dh)
        dg_ref[0:1, :] += dgn
        gx_ref[...] = d + dx

    rev = lambda i: (n - 1 - i, 0)
    return _call(
        body,
        after=after,
        args=(dx1, x, gain, bcx, y, cw, w_in, w_out),
        name="conv_bwd",
        grid=(n,),
        in_specs=[
            pl.BlockSpec((tm, D), rev),
            pl.BlockSpec((tm, D), rev),
            _resident((1, D)),
            pl.BlockSpec((tm, 3 * D), rev),
            pl.BlockSpec((tm, D), rev),
            _resident((3, D)),
            _resident((D, 3 * D)),
            _resident((D, D)),
        ],
        out_specs=[
            pl.BlockSpec((tm, D), rev),
            pl.BlockSpec((tm, 3 * D), rev),
            pl.BlockSpec((tm, D), rev),
            pl.BlockSpec((tm, D), rev),
            pl.BlockSpec((8, D), lambda i: (0, 0)),
            pl.BlockSpec((8, D), lambda i: (0, 0)),
        ],
        out_shape=[
            jax.ShapeDtypeStruct((T, D), F32),
            jax.ShapeDtypeStruct((T, 3 * D), BF16),
            jax.ShapeDtypeStruct((T, D), BF16),
            jax.ShapeDtypeStruct((T, D), BF16),
            jax.ShapeDtypeStruct((8, D), F32),
            jax.ShapeDtypeStruct((8, D), F32),
        ],
        scratch=[pltpu.VMEM((8, D), F32)],
    )


MXU_TILE = 256
FFN_CHUNK = 4 * MXU_TILE


def _sigmoid(g):
    return 1.0 / (1.0 + jnp.exp(-g))


def _ffn_chunks(F):
    assert F % MXU_TILE == 0
    return [(s, min(FFN_CHUNK, F - s)) for s in range(0, F, FFN_CHUNK)]


def ffn_fwd(x, gain, w_gu, w_d, *, tm, plan=None, attn=None, target=None):
    T, D = x.shape
    F = w_d.shape[0]
    row = lambda i: (i, 0)
    tile = pl.BlockSpec((tm, D), row)

    def body(*refs):
        refs = list(refs)
        x_ref, g_ref, wgu_ref, wd_ref = refs[:4]
        del refs[:4]
        if attn is not None:
            ao_ref, wo_ref = refs[:2]
            del refs[:2]
        if target is not None:
            t_ref = refs.pop(0)
        if attn is not None:
            xin_ref = refs.pop(0)
        xo_ref, gu_ref = refs[:2]
        xt = x_ref[...]
        if attn is not None:
            xt = xt + _dot(ao_ref[...], wo_ref[...])
            xin_ref[...] = xt
        h = ((xt * _rms(xt)) * g_ref[...]).astype(BF16)
        acc = xt
        for s, n in _ffn_chunks(F):
            g = _dot(h, wgu_ref[:, s : s + n])
            u = _dot(h, wgu_ref[:, F + s : F + s + n])
            gu_ref[:, s : s + n] = g
            gu_ref[:, F + s : F + s + n] = u
            a = ((g * _sigmoid(g)) * u).astype(BF16)
            acc = acc + _dot(a, wd_ref[s : s + n, :])
        if target is None:
            xo_ref[...] = acc
        else:
            s_ref = refs[2]

            @pl.when(pl.program_id(0) == 0)
            def _():
                s_ref[...] = jnp.zeros_like(s_ref)

            e = acc - t_ref[...]
            xo_ref[...] = e * (1.0 / D)
            s_ref[...] += jnp.sum(jnp.sum(e * e, axis=-1, keepdims=True), axis=0, keepdims=True)

    args = [x, gain, w_gu, w_d]
    in_specs = [tile, _resident((1, D)), _resident((D, 2 * F)), _resident((F, D))]
    out_specs = [tile, pl.BlockSpec((tm, 2 * F), row)]
    out_shape = [jax.ShapeDtypeStruct((T, D), F32), jax.ShapeDtypeStruct((T, 2 * F), F32)]
    if attn is not None:
        args += list(attn)
        in_specs += [pl.BlockSpec((tm, attn[0].shape[1]), row), _resident(attn[1].shape)]
        out_specs.insert(0, tile)
        out_shape.insert(0, jax.ShapeDtypeStruct((T, D), F32))
    if target is not None:
        args.append(target)
        in_specs.append(tile)
        out_specs.append(pl.BlockSpec((8, LANES), lambda i: (0, 0)))
        out_shape.append(jax.ShapeDtypeStruct((8, LANES), F32))
    return _call(body, plan=plan, args=args, name="ffn_fwd", grid=(T // tm,), in_specs=in_specs, out_specs=out_specs,
                 out_shape=out_shape)


def ffn_bwd(dxo, x, gain, gu, w_gu, w_d, *, tm, after=None, w_o=None):
    T, D = x.shape
    F = w_d.shape[0]

    def body(d_ref, x_ref, g_ref, gu_ref, wgu_ref, wd_ref, *rest):
        if w_o is not None:
            wo_ref, rest = rest[0], rest[1:]
        dx_ref, a_ref, dgu_ref, h_ref, d16_ref, dg_ref = rest[:6]

        @pl.when(pl.program_id(0) == 0)
        def _():
            dg_ref[...] = jnp.zeros_like(dg_ref)

        d = d_ref[...]
        d16 = d.astype(BF16)
        d16_ref[...] = d16
        dh = jnp.zeros((tm, D), F32)
        for c0, n in _ffn_chunks(F):
            g = gu_ref[:, c0 : c0 + n]
            u = gu_ref[:, F + c0 : F + c0 + n]
            da = _dot_nt(d16, wd_ref[c0 : c0 + n, :])
            s = _sigmoid(g)
            sg = g * s
            a_ref[:, c0 : c0 + n] = (sg * u).astype(BF16)
            dg16 = (da * u * (s + sg * (1.0 - s))).astype(BF16)
            du16 = (da * sg).astype(BF16)
            dgu_ref[:, c0 : c0 + n] = dg16
            dgu_ref[:, F + c0 : F + c0 + n] = du16
            dh = dh + _dot_nt(dg16, wgu_ref[:, c0 : c0 + n]) + _dot_nt(du16, wgu_ref[:, F + c0 : F + c0 + n])
        xt = x_ref[...]
        r = _rms(xt)
        gn = g_ref[...]
        h_ref[...] = ((xt * r) * gn).astype(BF16)
        dx, dgn = _rms_bwd(xt, r, gn, dh)
        dg_ref[0:1, :] += dgn
        dxi = d + dx
        dx_ref[...] = dxi
        if w_o is not None:
            dxi16_ref, dao_ref = rest[6:8]
            dxi16 = dxi.astype(BF16)
            dxi16_ref[...] = dxi16
            dao_ref[...] = _dot_nt(dxi16, wo_ref[...]).astype(BF16)

    tile = pl.BlockSpec((tm, D), lambda i: (i, 0))
    args = [dxo, x, gain, gu, w_gu, w_d]
    wide = lambda n: pl.BlockSpec((tm, n), lambda i: (i, 0))
    in_specs = [tile, tile, _resident((1, D)), wide(2 * F), _resident((D, 2 * F)), _resident((F, D))]
    out_specs = [tile, wide(F), wide(2 * F), tile, tile, pl.BlockSpec((8, D), lambda i: (0, 0))]
    out_shape = [
        jax.ShapeDtypeStruct((T, D), F32),
        jax.ShapeDtypeStruct((T, F), BF16),
        jax.ShapeDtypeStruct((T, 2 * F), BF16),
        jax.ShapeDtypeStruct((T, D), BF16),
        jax.ShapeDtypeStruct((T, D), BF16),
        jax.ShapeDtypeStruct((8, D), F32),
    ]
    if w_o is not None:
        args.append(w_o)
        in_specs.append(_resident(w_o.shape))
        out_specs += [tile, pl.BlockSpec((tm, w_o.shape[0]), lambda i: (i, 0))]
        out_shape += [jax.ShapeDtypeStruct((T, D), BF16), jax.ShapeDtypeStruct((T, w_o.shape[0]), BF16)]
    return _call(body, after=after, args=args, name="ffn_bwd", grid=(T // tm,), in_specs=in_specs, out_specs=out_specs,
                 out_shape=out_shape)


def wgrad(a, b, *, name, a_cols=0, b_cols=0, group=1, flat=False, tk, out_dtype=BF16, after=None):
    T, K = a.shape
    J = 1
    if a_cols:
        K = a_cols
        J = a.shape[1] // K
        a_spec = pl.BlockSpec((tk, K), lambda j, k: (k, j))
    else:
        a_spec = pl.BlockSpec((tk, K), lambda j, k: (k, 0))
    if b_cols:
        N = b_cols * group
        J = b.shape[1] // N
        b_spec = pl.BlockSpec((tk, N), lambda j, k: (k, j))
    else:
        N = b.shape[1]
        b_spec = pl.BlockSpec((tk, N), lambda j, k: (k, 0))
    nk = T // tk
    if flat:
        o_spec, o_shape = pl.BlockSpec((K, N), lambda j, k: (0, j)), (K, J * N)
    elif group > 1:
        o_spec, o_shape = pl.BlockSpec((group, K, b_cols), lambda j, k: (j, 0, 0)), (J * group, K, b_cols)
    else:
        o_spec, o_shape = pl.BlockSpec((None, K, N), lambda j, k: (j, 0, 0)), (J, K, N)

    def body(a_ref, b_ref, o_ref, acc_ref):
        k = pl.program_id(1)

        @pl.when(k == 0)
        def _():
            acc_ref[...] = jnp.zeros_like(acc_ref)

        acc_ref[...] += _dot_tn(a_ref[...], b_ref[...])

        @pl.when(k == nk - 1)
        def _():
            if group > 1 and not flat:
                for i in range(group):
                    o_ref[i] = acc_ref[:, i * b_cols : (i + 1) * b_cols].astype(out_dtype)
            else:
                o_ref[...] = acc_ref[...].astype(out_dtype)

    outs, _ = _call(
        body,
        after=after,
        name=name,
        grid=(J, nk),
        in_specs=[a_spec, b_spec],
        out_specs=[o_spec],
        out_shape=[jax.ShapeDtypeStruct(o_shape, out_dtype)],
        args=(a, b),
        scratch=[pltpu.VMEM((K, N), F32)],
    )
    return outs[0]


def _seg(xs, lo):
    s_lo = [jnp.sum(jnp.where(lo, x, 0.0), axis=-1, keepdims=True) for x in xs]
    s_hi = [jnp.sum(jnp.where(lo, 0.0, x), axis=-1, keepdims=True) for x in xs]
    return [jnp.where(lo, a, b) for a, b in zip(s_lo, s_hi)]


def _head_norm(xs, gains, lo):
    rs = [lax.rsqrt(s * (1.0 / HEAD_DIM) + EPS) for s in _seg([x * x for x in xs], lo)]
    return [(x * r) * g for x, r, g in zip(xs, rs, gains)], rs


def _head_norm_bwd(xs, rs, gains, dys, lo):
    xns = [x * r for x, r in zip(xs, rs)]
    dxns = [dy * g for dy, g in zip(dys, gains)]
    means = [s * (1.0 / HEAD_DIM) for s in _seg([a * b for a, b in zip(dxns, xns)], lo)]
    dxs = [r * (dxn - xn * m) for r, dxn, xn, m in zip(rs, dxns, xns, means)]
    return dxs, [jnp.sum(dy * xn, axis=0, keepdims=True) for dy, xn in zip(dys, xns)]


def _swap_halves(x):
    return pltpu.roll(x, HEAD_DIM, 1)


def qkv_proj(x, gain, w, qg, kg, *, tm):
    T, D = x.shape
    N = w.shape[1]
    kvw = N_KV_HEADS * HEAD_DIM
    nqt, nkt = D // LANES, kvw // LANES

    def body(x_ref, g_ref, w_ref, qg_ref, kg_ref, qkv_ref, q_ref, kd_ref, vd_ref):
        xt = x_ref[...]
        h = ((xt * _rms(xt)) * g_ref[...]).astype(BF16)
        qkv = _dot(h, w_ref[...])
        qkv_ref[...] = qkv
        lo = lax.broadcasted_iota(jnp.int32, (1, LANES), 1) < HEAD_DIM
        tiles = [qkv[:, t * LANES : (t + 1) * LANES] for t in range(nqt + nkt)]
        normed, _ = _head_norm(tiles, [qg_ref[...]] * nqt + [kg_ref[...]] * nkt, lo)
        for t in range(nqt):
            q_ref[:, t * LANES : (t + 1) * LANES] = (normed[t] * SCALE).astype(BF16)
        for t in range(nkt):
            kn = normed[nqt + t]
            v = qkv[:, D + kvw + t * LANES : D + kvw + (t + 1) * LANES]
            for src, dst in ((kn, kd_ref), (v, vd_ref)):
                sw = _swap_halves(src)
                dst[:, 2 * t * LANES : (2 * t + 1) * LANES] = jnp.where(lo, src, sw).astype(BF16)
                dst[:, (2 * t + 1) * LANES : (2 * t + 2) * LANES] = jnp.where(lo, sw, src).astype(BF16)

    row = lambda i: (i, 0)
    return pl.pallas_call(
        body,
        name="qkv_proj",
        grid=(T // tm,),
        in_specs=[pl.BlockSpec((tm, D), row), _resident((1, D)), _resident((D, N)), _resident((1, LANES)), _resident((1, LANES))],
        out_specs=[pl.BlockSpec((tm, N), row), pl.BlockSpec((tm, D), row), pl.BlockSpec((tm, 2 * kvw), row), pl.BlockSpec((tm, 2 * kvw), row)],
        out_shape=[
            jax.ShapeDtypeStruct((T, N), F32),
            jax.ShapeDtypeStruct((T, D), BF16),
            jax.ShapeDtypeStruct((T, 2 * kvw), BF16),
            jax.ShapeDtypeStruct((T, 2 * kvw), BF16),
        ],
        compiler_params=_params(1),
    )(x, gain, w, qg, kg)


def _attn_tables(sinks, n_q_heads):
    P = n_q_heads // N_KV_HEADS // 2
    h = jnp.arange(1, n_q_heads + 1, dtype=F32)
    slopes = jnp.exp2(-8.0 * h / n_q_heads).reshape(N_KV_HEADS, P, 1, 2, 1)
    qi = jnp.arange(BLOCK)[:, None]
    kj = jnp.arange(BLOCK)[None, :]
    dist = jnp.where(kj <= qi, qi - kj, qi + BLOCK - kj).astype(F32)
    shape = (N_KV_HEADS, P, BLOCK, 2, BLOCK)
    bias = jnp.broadcast_to(-slopes * dist[None, None, :, None, :], shape)
    sink = jnp.broadcast_to(sinks.astype(F32).reshape(N_KV_HEADS, P, 1, 2, 1), shape)
    return bias.reshape(N_KV_HEADS, P * BLOCK, 2 * BLOCK), sink.reshape(N_KV_HEADS, P * BLOCK, 2 * BLOCK)


def _attn_specs(D, nb):
    kvw2 = 2 * N_KV_HEADS * HEAD_DIM
    cur = lambda b, i: (b * nb + i, 0)
    prev = lambda b, i: (jnp.maximum(b * nb + i - 1, 0), 0)
    return [
        pl.BlockSpec((BLOCK, D), cur),
        pl.BlockSpec((BLOCK, kvw2), cur),
        pl.BlockSpec((BLOCK, kvw2), prev),
        pl.BlockSpec((BLOCK, kvw2), cur),
        pl.BlockSpec((BLOCK, kvw2), prev),
    ]


def _attn_operands(kh, P, lo, q_ref, kc_ref, kp_ref, vc_ref, vp_ref):
    sl = slice(kh * LANES, (kh + 1) * LANES)

    def cat(prev_ref, cur_ref):
        d = jnp.concatenate([prev_ref[:, sl], cur_ref[:, sl]], axis=0)
        z = jnp.zeros_like(d)
        return jnp.concatenate([jnp.where(lo, d, z), jnp.where(lo, z, d)], axis=0)

    qt = jnp.concatenate([q_ref[:, (kh * P + pr) * LANES : (kh * P + pr + 1) * LANES] for pr in range(P)], axis=0)
    return qt, cat(kp_ref, kc_ref), cat(vp_ref, vc_ref)


def _attn_exp(s_all, bias, sink, tri, first):
    out = []
    for par in range(2):
        c0 = 2 * par * BLOCK
        s = jnp.where(tri, s_all[:, c0 + BLOCK : c0 + 2 * BLOCK], jnp.where(first, NEG, s_all[:, c0 : c0 + BLOCK]))
        s = s + bias[:, par * BLOCK : (par + 1) * BLOCK]
        snk = sink[:, par * BLOCK : (par + 1) * BLOCK]
        m = jnp.maximum(jnp.max(s, axis=-1, keepdims=True), snk)
        out.append((jnp.exp(s - m), jnp.exp(snk - m)))
    return out


def _unfold(x, tri):
    z = jnp.zeros_like(x)
    return jnp.concatenate([jnp.where(tri, z, x), jnp.where(tri, x, z)], axis=1)


def _attn_masks(R):
    lane = lax.broadcasted_iota(jnp.int32, (1, LANES), 1)
    row = lax.broadcasted_iota(jnp.int32, (R, BLOCK), 0) & (BLOCK - 1)
    col = lax.broadcasted_iota(jnp.int32, (R, BLOCK), 1)
    return lane, lane < HEAD_DIM, col <= row


def attn_fwd(q16, kd, vd, bias, sink, *, seq, n_seq):
    T, D = q16.shape
    nb = seq // BLOCK
    P = D // HEAD_DIM // N_KV_HEADS // 2
    R = P * BLOCK
    KV = range(N_KV_HEADS)

    def body(q_ref, kc_ref, kp_ref, vc_ref, vp_ref, bias_ref, sink_ref, o_ref):
        first = pl.program_id(1) == 0
        _, lo, tri = _attn_masks(R)
        ops = [_attn_operands(kh, P, lo, q_ref, kc_ref, kp_ref, vc_ref, vp_ref) for kh in KV]
        s_all = [_dot_nt(ops[kh][0], ops[kh][1]) for kh in KV]
        ex = [_attn_exp(s_all[kh], bias_ref[kh], sink_ref[kh], tri, first) for kh in KV]
        den = [[jnp.sum(e, axis=-1, keepdims=True) + es for e, es in ex[kh]] for kh in KV]
        lhs = [jnp.concatenate([_unfold(e, tri) for e, _ in ex[kh]], axis=1).astype(BF16) for kh in KV]
        o = [_dot(lhs[kh], ops[kh][2]) for kh in KV]
        for kh in KV:
            out = o[kh] / jnp.where(lo, den[kh][0], den[kh][1])
            for pr in range(P):
                t = kh * P + pr
                o_ref[:, t * LANES : (t + 1) * LANES] = out[pr * BLOCK : (pr + 1) * BLOCK, :].astype(BF16)

    return pl.pallas_call(
        body,
        name="attn_fwd",
        grid=(n_seq, nb),
        in_specs=_attn_specs(D, nb) + [_resident((N_KV_HEADS, R, 2 * BLOCK)), _resident((N_KV_HEADS, R, 2 * BLOCK))],
        out_specs=pl.BlockSpec((BLOCK, D), lambda b, i: (b * nb + i, 0)),
        out_shape=jax.ShapeDtypeStruct((T, D), BF16),
        compiler_params=_params(2),
    )(q16, kd, kd, vd, vd, bias, sink)


def attn_bwd(q16, kd, vd, do, bias, sink, *, seq, n_seq):
    T, D = q16.shape
    kvw2 = 2 * N_KV_HEADS * HEAD_DIM
    nb = seq // BLOCK
    G = D // HEAD_DIM // N_KV_HEADS
    P = G // 2
    R = P * BLOCK
    KV = range(N_KV_HEADS)

    def body(q_ref, kc_ref, kp_ref, vc_ref, vp_ref, do_ref, bias_ref, sink_ref,
             dq_ref, dkc_ref, dkp_ref, dvc_ref, dvp_ref, dsink_ref):
        first = pl.program_id(1) == 0

        @pl.when(jnp.logical_and(pl.program_id(0) == 0, first))
        def _():
            dsink_ref[...] = jnp.zeros_like(dsink_ref)

        lane, lo, tri = _attn_masks(R)
        ops = [_attn_operands(kh, P, lo, q_ref, kc_ref, kp_ref, vc_ref, vp_ref) for kh in KV]
        do16 = [jnp.concatenate([do_ref[:, (kh * P + pr) * LANES : (kh * P + pr + 1) * LANES] for pr in range(P)], axis=0)
                for kh in KV]
        s_all = [_dot_nt(ops[kh][0], ops[kh][1]) for kh in KV]
        dp_all = [_dot_nt(do16[kh], ops[kh][2]) for kh in KV]
        ex = [_attn_exp(s_all[kh], bias_ref[kh], sink_ref[kh], tri, first) for kh in KV]
        den = [[jnp.sum(e, axis=-1, keepdims=True) for e, _ in ex[kh]] for kh in KV]
        dsink = jnp.zeros((1, LANES), F32)
        pf, dsf = [], []
        for kh in KV:
            ps_, ds_ = [], []
            for par in range(2):
                e, es = ex[kh][par]
                inv = 1.0 / (den[kh][par] + es)
                p = e * inv
                c0 = 2 * par * BLOCK
                dp = jnp.where(tri, dp_all[kh][:, c0 + BLOCK : c0 + 2 * BLOCK], dp_all[kh][:, c0 : c0 + BLOCK])
                delta = jnp.sum(p * dp, axis=-1, keepdims=True)
                ds_.append(_unfold(p * (dp - delta), tri))
                ps_.append(_unfold(p, tri))
                dsr = -((es * inv) * delta)
                for pr in range(P):
                    hq = kh * G + 2 * pr + par
                    tot = jnp.sum(dsr[pr * BLOCK : (pr + 1) * BLOCK, :], axis=0, keepdims=True)
                    dsink = dsink + jnp.where(lane == hq, tot, 0.0)
            pf.append(jnp.concatenate(ps_, axis=1).astype(BF16))
            dsf.append(jnp.concatenate(ds_, axis=1).astype(BF16))
        dq = [_dot(dsf[kh], ops[kh][1]) for kh in KV]
        dk = [_dot_tn(dsf[kh], ops[kh][0]) for kh in KV]
        dv = [_dot_tn(pf[kh], do16[kh]) for kh in KV]
        dsink_ref[0:1, :] += dsink
        for kh in KV:
            sl = slice(kh * LANES, (kh + 1) * LANES)
            for pr in range(P):
                t = kh * P + pr
                dq_ref[:, t * LANES : (t + 1) * LANES] = dq[kh][pr * BLOCK : (pr + 1) * BLOCK, :]
            for full, prev_ref, cur_ref in ((dk[kh], dkp_ref, dkc_ref), (dv[kh], dvp_ref, dvc_ref)):
                dup = jnp.where(lo, full[: 2 * BLOCK, :], full[2 * BLOCK :, :])
                prev_ref[:, sl] = dup[:BLOCK, :].astype(BF16)
                cur_ref[:, sl] = dup[BLOCK:, :].astype(BF16)

    cur = lambda b, i: (b * nb + i, 0)
    kv_spec = pl.BlockSpec((BLOCK, kvw2), cur)
    kv_shape = jax.ShapeDtypeStruct((T, kvw2), BF16)
    return pl.pallas_call(
        body,
        name="attn_bwd",
        grid=(n_seq, nb),
        in_specs=_attn_specs(D, nb)
        + [pl.BlockSpec((BLOCK, D), cur), _resident((N_KV_HEADS, R, 2 * BLOCK)), _resident((N_KV_HEADS, R, 2 * BLOCK))],
        out_specs=[pl.BlockSpec((BLOCK, D), cur), kv_spec, kv_spec, kv_spec, kv_spec, pl.BlockSpec((8, LANES), lambda b, i: (0, 0))],
        out_shape=[jax.ShapeDtypeStruct((T, D), F32), kv_shape, kv_shape, kv_shape, kv_shape, jax.ShapeDtypeStruct((8, LANES), F32)],
        compiler_params=_params(2),
    )(q16, kd, kd, vd, vd, do, bias, sink)


def qkv_bwd(dq, dkc, dkp, dvc, dvp, qkv, dres, x, gain, w_qkv, qg, kg, *, seq, tm):
    T, D = x.shape
    kvw2 = dkc.shape[1]
    kvw = kvw2 // 2
    nqt, nkt = D // LANES, kvw // LANES
    nb = seq // BLOCK
    nbt = tm // BLOCK
    assert nb % nbt == 0
    n = T // tm

    def body(dq_ref, dkc_ref, dkpa_ref, dkpb_ref, dvc_ref, dvpa_ref, dvpb_ref, qkv_ref, dres_ref, x_ref, g_ref, w_ref,
             qg_ref, kg_ref, dx_ref, dqkv_ref, h_ref, dg_ref, hg_ref):
        i = pl.program_id(0)

        @pl.when(i == 0)
        def _():
            dg_ref[...] = jnp.zeros_like(dg_ref)
            hg_ref[...] = jnp.zeros_like(hg_ref)

        lo = lax.broadcasted_iota(jnp.int32, (1, LANES), 1) < HEAD_DIM
        last = ((i + 1) * nbt) % nb == 0
        up = lambda ref: ref[...].astype(F32)

        def with_next(cur_ref, own_ref, next_ref):
            return up(cur_ref) + jnp.concatenate([up(own_ref)[BLOCK:, :], jnp.where(last, 0.0, up(next_ref))], axis=0)

        dkd = with_next(dkc_ref, dkpa_ref, dkpb_ref)
        dvd = with_next(dvc_ref, dvpa_ref, dvpb_ref)

        def undup(d, t):
            a, b = d[:, 2 * t * LANES : (2 * t + 1) * LANES], d[:, (2 * t + 1) * LANES : (2 * t + 2) * LANES]
            return jnp.where(lo, a + _swap_halves(a), b + _swap_halves(b))

        tiles = [qkv_ref[:, t * LANES : (t + 1) * LANES] for t in range(nqt + nkt)]
        gains = [qg_ref[...]] * nqt + [kg_ref[...]] * nkt
        dys = [dq_ref[:, t * LANES : (t + 1) * LANES] * SCALE for t in range(nqt)] + [undup(dkd, t) for t in range(nkt)]
        _, rs = _head_norm(tiles, gains, lo)
        dxs, dgs = _head_norm_bwd(tiles, rs, gains, dys, lo)
        for t in range(nqt + nkt):
            dqkv_ref[:, t * LANES : (t + 1) * LANES] = dxs[t].astype(BF16)
        for t in range(nkt):
            dqkv_ref[:, D + kvw + t * LANES : D + kvw + (t + 1) * LANES] = undup(dvd, t).astype(BF16)
        hg_ref[0:1, :] += functools.reduce(lambda a, b: a + b, dgs[:nqt])
        hg_ref[1:2, :] += functools.reduce(lambda a, b: a + b, dgs[nqt:])
        dh = _dot_nt(dqkv_ref[...], w_ref[...])
        xt = x_ref[...]
        r = _rms(xt)
        gn = g_ref[...]
        h_ref[...] = ((xt * r) * gn).astype(BF16)
        dx, dgn = _rms_bwd(xt, r, gn, dh)
        dg_ref[0:1, :] += dgn
        dx_ref[...] = dres_ref[...] + dx

    row = lambda i: (i, 0)
    nxt_a = pl.BlockSpec((tm, kvw2), row)
    nxt_b = pl.BlockSpec((BLOCK, kvw2), lambda i: (jnp.minimum((i + 1) * nbt, n * nbt - 1), 0))
    return pl.pallas_call(
        body,
        name="qkv_bwd",
        grid=(n,),
        in_specs=[
            pl.BlockSpec((tm, D), row),
            pl.BlockSpec((tm, kvw2), row),
            nxt_a,
            nxt_b,
            pl.BlockSpec((tm, kvw2), row),
            nxt_a,
            nxt_b,
            pl.BlockSpec((tm, D + kvw2), row),
            pl.BlockSpec((tm, D), row),
            pl.BlockSpec((tm, D), row),
            _resident((1, D)),
            _resident((D, D + kvw2)),
            _resident((1, LANES)),
            _resident((1, LANES)),
        ],
        out_specs=[
            pl.BlockSpec((tm, D), row),
            pl.BlockSpec((tm, D + kvw2), row),
            pl.BlockSpec((tm, D), row),
            pl.BlockSpec((8, D), lambda i: (0, 0)),
            pl.BlockSpec((8, LANES), lambda i: (0, 0)),
        ],
        out_shape=[
            jax.ShapeDtypeStruct((T, D), F32),
            jax.ShapeDtypeStruct((T, D + kvw2), BF16),
            jax.ShapeDtypeStruct((T, D), BF16),
            jax.ShapeDtypeStruct((8, D), F32),
            jax.ShapeDtypeStruct((8, LANES), F32),
        ],
        compiler_params=_params(1),
    )(dq, dkc, dkp, dkp, dvc, dvp, dvp, qkv, dres, x, gain, w_qkv, qg, kg)


def local_step(x, target, gains, w, *, seq, tm=256, tm_ffn=256, tm_conv=512, tk=2048, shards=None, ex=None):
    T, D = x.shape
    n_seq = T // seq
    nm, nf, qgain, kgain, sinks = gains
    H = D // HEAD_DIM
    tk, tk_long = min(tk, T), min(2 * tk, T)
    qg2, kg2 = jnp.tile(qgain, (1, 2)), jnp.tile(kgain, (1, 2))
    bias, sinkcol = _attn_tables(sinks, H)

    dist = shards is not None
    w = dict(w)

    plan = _Gather([shards["w_gu"][0], shards["w_d"][0]]) if dist else None
    (x1, bcx, y_conv, z16), got = conv_fwd(x, nm[0:1], w["w_in"], w["cw"], w["w_out"], seq=seq, tm=tm_conv, plan=plan)
    if dist:
        w["w_gu"], w["w_d"] = [cols_from_shards(got[0]), None], [got[1].reshape(-1, D), None]
    plan = _Gather([shards["w_qkv"], shards["w_o"], shards["w_gu"][1], shards["w_d"][1]]) if dist else None
    (x2, gu0), got = ffn_fwd(x1, nf[0:1], w["w_gu"][0], w["w_d"][0], tm=2 * tm_ffn, plan=plan)
    if dist:
        w["w_qkv"], w["w_o"] = cols_from_shards(got[0]), got[1].reshape(D, D)
        w["w_gu"][1], w["w_d"][1] = cols_from_shards(got[2]), got[3].reshape(-1, D)
    qkv, q16, kd, vd = qkv_proj(x2, nm[1:2], w["w_qkv"], qg2, kg2, tm=tm_conv)
    ao = attn_fwd(q16, kd, vd, bias, sinkcol, seq=seq, n_seq=n_seq)
    (x3, dx4, gu1, sse), _ = ffn_fwd(x2, nf[1:2], w["w_gu"][1], w["w_d"][1], tm=tm_ffn, attn=(ao, w["w_o"]), target=target)

    by_dest = lambda a: a.reshape(N_DEV, -1, a.shape[-1])
    gu_cols = 2 * MXU_TILE

    def send(name, *entries):
        if ex is None:
            return None
        items = [(a, False, key, (N_DEV,) + (() if layers is None else (layers,)) + a.shape[1:], layer)
                 for a, key, layer, layers in entries]
        return ex.start(items, name=name)

    (dx3, a16, dgu, h16, d16, dnf1, dx3_16, dao), _ = ffn_bwd(
        dx4, x3, nf[1:2], gu1, w["w_gu"][1], w["w_d"][1], tm=tm, w_o=w["w_o"])
    g_gu1 = shards_from_cols(wgrad(h16, dgu, name="wgrad_gu1", b_cols=gu_cols, flat=True, tk=tk_long))
    g_d1 = by_dest(wgrad(a16, d16, name="wgrad_d1", a_cols=a16.shape[1] // 2, tk=tk))
    g_o = by_dest(wgrad(ao, dx3_16, name="wgrad_o", tk=tk))
    dq, dkc, dkp, dvc, dvp, dsinks = attn_bwd(q16, kd, vd, dao, bias, sinkcol, seq=seq, n_seq=n_seq)
    dx2, dqkv16, h16, dnm1, dgains = qkv_bwd(dq, dkc, dkp, dvc, dvp, qkv, dx3, x2, nm[1:2], w["w_qkv"], qg2, kg2, seq=seq, tm=tm_conv)
    g_qkv = shards_from_cols(wgrad(h16, dqkv16, name="wgrad_qkv", tk=tk)[0])
    tok = send("exchange_layer1", (g_gu1, "w_gu", 1, 2), (g_d1, "w_d", 1, 2), (g_o, "w_o", None, None),
               (g_qkv, "w_qkv", None, None))
    (dx1, a16, dgu, h16, d16, dnf0), _ = ffn_bwd(dx2, x1, nf[0:1], gu0, w["w_gu"][0], w["w_d"][0], tm=tm, after=tok)
    g_gu0 = shards_from_cols(wgrad(h16, dgu, name="wgrad_gu0", b_cols=gu_cols, flat=True, tk=tk_long))
    tok = send("exchange_gu0", (g_gu0, "w_gu", 0, 2))
    g_d0 = by_dest(wgrad(a16, d16, name="wgrad_d0", a_cols=a16.shape[1] // 2, tk=tk, after=tok))
    tok = send("exchange_d0", (g_d0, "w_d", 0, 2))
    (gx, dbcx, h16, d16, dcw, dnm0), _ = conv_bwd(
        dx1, x, nm[0:1], bcx, y_conv, w["cw"], w["w_in"], w["w_out"], seq=seq, tm=tm_conv, after=tok)
    g_out = by_dest(wgrad(z16, d16, name="wgrad_out", tk=tk))
    g_cw = dcw[0:3].reshape(3, N_DEV, D // N_DEV).transpose(1, 0, 2)
    tok = send("exchange_out", (g_out, "w_out", None, None), (g_cw, "cw", None, None))
    g_in = wgrad(h16, dbcx, name="wgrad_in", b_cols=3 * D // N_DEV, group=2, tk=tk_long, after=tok)
    g = dict(w_in=g_in, cw=g_cw, w_out=g_out, w_o=g_o, w_qkv=g_qkv, w_gu=[g_gu0, g_gu1], w_d=[g_d0, g_d1])
    small = dict(nm0=dnm0, nm1=dnm1, nf0=dnf0, nf1=dnf1, gains=dgains, sinks=dsinks)
    return sse, gx, g, small


def _adamw_math(g, w, m, v):
    m = ADAM_B1 * m + (1.0 - ADAM_B1) * g
    v = ADAM_B2 * v + (1.0 - ADAM_B2) * (g * g)
    m_hat = m / (1.0 - ADAM_B1 ** ADAM_STEP)
    v_hat = v / (1.0 - ADAM_B2 ** ADAM_STEP)
    delta = -ADAM_LR * (m_hat / (jnp.sqrt(v_hat) + ADAM_EPS) + ADAM_WD * w)
    return delta, m, v


def adamw(parts, owns, w, m, v, *, name, after=None):
    n, LR, C = parts.shape
    L = len(owns)
    R = LR // L
    tr = R
    for cand in (256, 128, 88, 64, 32, 16, 8):
        if R > cand and R % cand == 0:
            tr = cand
            break
    per_layer = R // tr
    extra = [] if after is None else [after]

    def body(me_ref, p_ref, *rest):
        own_refs, (w_ref, m_ref, v_ref) = rest[:L], rest[L : L + 3]
        g_ref, d_ref, mo_ref, vo_ref = rest[L + 3 + len(extra) :]
        layer = pl.program_id(0) // per_layer
        mine = own_refs[0][...].astype(F32)
        for j in range(1, L):
            mine = jnp.where(layer == j, own_refs[j][...].astype(F32), mine)
        g = None
        for s in range(n):
            share = jnp.where(me_ref[0] == s, mine, p_ref[s].astype(F32))
            g = share if g is None else g + share
        g_ref[...] = g
        d_ref[...], mo_ref[...], vo_ref[...] = _adamw_math(g, w_ref[...], m_ref[...], v_ref[...])

    blk = pl.BlockSpec((tr, C), lambda i, me: (i, 0))
    own_specs = [pl.BlockSpec((None, tr, C), lambda i, me: (me[0], i % per_layer, 0)) if o.ndim == 3
                 else pl.BlockSpec((tr, C), lambda i, me: (i % per_layer, 0)) for o in owns]
    me = (4 * lax.axis_index("x") + 2 * lax.axis_index("y") + lax.axis_index("c")).astype(jnp.int32).reshape(1)
    return pl.pallas_call(
        body,
        name=name,
        grid_spec=pltpu.PrefetchScalarGridSpec(
            num_scalar_prefetch=1,
            grid=(LR // tr,),
            in_specs=[pl.BlockSpec((n, tr, C), lambda i, me: (0, i, 0))] + own_specs + [blk, blk, blk] + _any_specs(len(extra)),
            out_specs=[blk] * 4,
        ),
        out_shape=[jax.ShapeDtypeStruct((LR, C), F32)] * 4,
        compiler_params=_params(1),
    )(me, parts, *owns, w, m, v, *extra)


def pack_small(small, sse, D):
    W = max(D, 2 * LANES)

    def body(nm0, nm1, nf0, nf1, gains, sinks, sse_ref, o_ref):
        o_ref[...] = jnp.zeros_like(o_ref)
        o_ref[0:1, :D] = nm0[0:1, :]
        o_ref[1:2, :D] = nm1[0:1, :]
        o_ref[2:3, :D] = nf0[0:1, :]
        o_ref[3:4, :D] = nf1[0:1, :]
        gq = gains[0:1, :] + pltpu.roll(gains[0:1, :], HEAD_DIM, 1)
        gk = gains[1:2, :] + pltpu.roll(gains[1:2, :], HEAD_DIM, 1)
        lane = lax.broadcasted_iota(jnp.int32, (1, LANES), 1)
        o_ref[4:5, :LANES] = jnp.where(lane < HEAD_DIM, gq, gk)
        o_ref[4:5, LANES : 2 * LANES] = sinks[0:1, :]
        o_ref[5:6, :LANES] = sse_ref[0:1, :] * (0.5 / D)

    return pl.pallas_call(
        body,
        name="pack_small",
        out_shape=jax.ShapeDtypeStruct((8, W), F32),
    )(small["nm0"], small["nm1"], small["nf0"], small["nf1"], small["gains"], small["sinks"], sse)


def _pack_small_params(nm, nf, qg, kg, sk, D):
    W = max(D, 2 * LANES)
    row4 = jnp.concatenate([qg.reshape(-1), kg.reshape(-1), jnp.zeros((LANES - 2 * HEAD_DIM,), F32), sk.reshape(-1)])
    row4 = jnp.pad(row4, (0, W - row4.shape[0]))
    rows = [jnp.pad(r, (0, W - D)) for r in (nm[0], nm[1], nf[0], nf[1])] + [row4]
    return jnp.concatenate([jnp.stack(rows), jnp.zeros((3, W), F32)], axis=0)


def _unpack_small(a, D, H):
    nm = a[0:2, :D]
    nf = a[2:4, :D]
    qg = a[4:5, 0:HEAD_DIM]
    kg = a[4:5, HEAD_DIM : 2 * HEAD_DIM]
    sk = a[4:5, LANES : LANES + H]
    return qg, kg, sk, nm, nf


def kernel(x, conv_w_in, conv_w, conv_w_out, attn_w_qkv, attn_q_gain, attn_k_gain, attn_sinks, attn_w_o, norm_mixer, norm_ffn, ffn_w_gate_up, ffn_w_down, loss_target, m_conv_w_in, m_conv_w, m_conv_w_out, m_attn_w_qkv, m_attn_q_gain, m_attn_k_gain, m_attn_sinks, m_attn_w_o, m_norm_mixer, m_norm_ffn, m_ffn_w_gate_up, m_ffn_w_down, v_conv_w_in, v_conv_w, v_conv_w_out, v_attn_w_qkv, v_attn_q_gain, v_attn_k_gain, v_attn_sinks, v_attn_w_o, v_norm_mixer, v_norm_ffn, v_ffn_w_gate_up, v_ffn_w_down):
    n_seq, seq, D = x.shape
    T = n_seq * seq
    H = D // HEAD_DIM
    L = ffn_w_gate_up.shape[0]

    full = run_plan(_Gather([conv_w_in[0].astype(BF16), conv_w[0], conv_w_out[0].astype(BF16)]), name="gather_conv_weights")
    w = dict(w_in=cols_from_shards(full[0]), cw=full[1].transpose(1, 0, 2).reshape(3, D),
             w_out=full[2].reshape(D, D))
    shards = dict(w_gu=[ffn_w_gate_up[l].astype(BF16) for l in range(L)], w_d=[ffn_w_down[l].astype(BF16) for l in range(L)],
                  w_qkv=attn_w_qkv[0].astype(BF16), w_o=attn_w_o[0].astype(BF16))
    gains = (norm_mixer, norm_ffn, attn_q_gain, attn_k_gain, attn_sinks)
    ex = Exchange()
    sse, gx, g, small = local_step(x.reshape(T, D), loss_target.reshape(T, D), gains, w, seq=seq, shards=shards, ex=ex)
    zones, own = ex.wait([g["w_in"]], name="exchange_wait")

    packed = pack_small(small, sse, D)
    token = ex.start([(g["w_in"], False, "w_in", g["w_in"].shape, None),
                      (packed, True, "small", (N_DEV,) + packed.shape, None)], name="exchange_last")

    def flat(a):
        return a.reshape(-1, a.shape[-1])

    big = [conv_w_in, conv_w, conv_w_out, attn_w_qkv, attn_w_o, ffn_w_gate_up, ffn_w_down]
    big_m = [m_conv_w_in, m_conv_w, m_conv_w_out, m_attn_w_qkv, m_attn_w_o, m_ffn_w_gate_up, m_ffn_w_down]
    big_v = [v_conv_w_in, v_conv_w, v_conv_w_out, v_attn_w_qkv, v_attn_w_o, v_ffn_w_gate_up, v_ffn_w_down]
    keys = ["w_in", "cw", "w_out", "w_qkv", "w_o", "w_gu", "w_d"]

    def update(b, zones, own, after=None):
        zone = zones[keys[b]]
        parts = zone.reshape(N_DEV, -1, zone.shape[-1])
        layers = [None] if zone.ndim == 3 else range(zone.shape[1])
        outs = adamw(parts, [own[(keys[b], l)] for l in layers], flat(big[b]), flat(big_m[b]), flat(big_v[b]),
                     name="adamw_" + keys[b], after=after)
        return [o.reshape(big[b].shape) for o in outs]

    res = [None] + [update(b, zones, own, after=token) for b in range(1, 7)]
    zones, own = ex.wait([r[0] for r in res[1:]], name="exchange_last_wait")
    res[0] = update(0, zones, own)
    sw = _pack_small_params(norm_mixer, norm_ffn, attn_q_gain, attn_k_gain, attn_sinks, D)
    sm = _pack_small_params(m_norm_mixer, m_norm_ffn, m_attn_q_gain, m_attn_k_gain, m_attn_sinks, D)
    sv = _pack_small_params(v_norm_mixer, v_norm_ffn, v_attn_q_gain, v_attn_k_gain, v_attn_sinks, D)
    souts = adamw(zones["small"], [own[("small", None)]], sw, sm, sv, name="adamw_small")
    sres = [_unpack_small(o, D, H) for o in souts]
    loss = souts[0][5, 0]

    def ordered(i):
        r, s = [r[i] for r in res], sres[i]
        return [r[0], r[1], r[2], r[3], s[0], s[1], s[2], r[4], s[3], s[4], r[5], r[6]]

    return (loss, gx.reshape(n_seq, seq, D), *ordered(0), *ordered(1), *ordered(2), *ordered(3))
```

```python
import functools
import math

import jax
import jax.numpy as jnp
from jax import lax
from jax.experimental import pallas as pl
from jax.experimental.pallas import tpu as pltpu

F32 = jnp.float32
BF16 = jnp.bfloat16

EPS = 1e-6
HEAD_DIM = 64
N_KV_HEADS = 4
BLOCK = 128
LANES = 128
N_DEV = 8
NEG = -1e30
SCALE = 1.0 / math.sqrt(HEAD_DIM)

ADAM_LR = 0.001
ADAM_B1 = 0.9
ADAM_B2 = 0.999
ADAM_EPS = 1e-08
ADAM_WD = 0.01
ADAM_STEP = 10

V7X_VMEM_BYTES = 64 * 1024 * 1024
VMEM_LIMIT = V7X_VMEM_BYTES - 2 * 1024 * 1024
MESH = pl.DeviceIdType.MESH

_NT = (((1,), (1,)), ((), ()))
_TN = (((0,), (0,)), ((), ()))


def _params(n_grid):
    return pltpu.CompilerParams(dimension_semantics=("arbitrary",) * n_grid, vmem_limit_bytes=VMEM_LIMIT)


def _resident(shape):
    nd = len(shape)
    return pl.BlockSpec(shape, lambda *_: (0,) * nd, pipeline_mode=pl.Buffered(1))


def _rms(x):
    return lax.rsqrt(jnp.mean(x * x, axis=-1, keepdims=True) + EPS)


def _rms_bwd(x, r, gain, dh):
    xn = x * r
    dxn = dh * gain
    dx = r * (dxn - xn * jnp.mean(dxn * xn, axis=-1, keepdims=True))
    return dx, jnp.sum(dh * xn, axis=0, keepdims=True)


def _dot(a, b):
    return jnp.dot(a, b, preferred_element_type=F32)


def _dot_nt(a, b):
    return lax.dot_general(a, b, _NT, preferred_element_type=F32)


def _dot_tn(a, b):
    return lax.dot_general(a, b, _TN, preferred_element_type=F32)


def _place():
    return lax.axis_index("x"), lax.axis_index("y"), lax.axis_index("c")


def _flip(v, bit):
    return 1 - v if bit else v


def _slot(px, py, pc):
    return 4 * px + 2 * py + pc


class _Gather:
    def __init__(self, shards):
        nt = len(shards)
        self.nt = nt
        self.inputs = list(shards)
        self.out_shapes = [jax.ShapeDtypeStruct((N_DEV,) + s.shape, s.dtype) for s in shards]
        self.scratch = [pltpu.SemaphoreType.DMA((nt, 10)), pltpu.SemaphoreType.DMA((nt, 10)), pltpu.SemaphoreType.DMA((nt,))]
        self.aliases = {}
        self.split = []
        for s in shards:
            rows, tile = s.shape[0], 16 if s.dtype == BF16 else 8
            self.split.append(rows // 2 if rows % (2 * tile) == 0 else rows)

    def phases(self, total):
        assert total >= 8
        return [(0, self.start), (total // 2, self.second), (total - 3, self.forward), (total - 1, self.finish)]

    def _copies(self, ins, outs, sems):
        send_sems, recv_sems, loc_sems = sems
        x, y, c = _place()
        xn, yn, sib = (1 - x, y, c), (x, 1 - y, c), (x, y, 1 - c)
        i_me, i_xn, i_yn, i_dn = _slot(x, y, c), _slot(1 - x, y, c), _slot(x, 1 - y, c), _slot(1 - x, 1 - y, c)
        j_me, j_xn, j_yn, j_dn = _slot(x, y, 1 - c), _slot(1 - x, y, 1 - c), _slot(x, 1 - y, 1 - c), _slot(1 - x, 1 - y, 1 - c)
        local, start, need1, second, need2, forward, need3 = [], [], [], [], [], [], []
        for t in range(self.nt):
            o, rows, h = outs[t], self.inputs[t].shape[0], self.split[t]
            lo = pl.ds(0, h)
            hi = pl.ds(h, rows - h) if h < rows else None

            def rc(k, src, dst, to, t=t):
                return pltpu.make_async_remote_copy(
                    src_ref=src, dst_ref=dst, send_sem=send_sems.at[t, k], recv_sem=recv_sems.at[t, k], device_id=to,
                    device_id_type=MESH)

            def landed(k, slot, part, frm):
                ref = o.at[slot] if part is None else o.at[slot, part]
                return rc(k, ref, ref, frm)

            local.append(pltpu.make_async_copy(ins[t], o.at[i_me], loc_sems.at[t]))
            start += [rc(0, ins[t], o.at[i_me], sib), rc(1, ins[t].at[lo], o.at[i_me, lo], xn),
                      rc(4, ins[t].at[lo], o.at[i_me, lo], yn)]
            need1.append(landed(1, i_xn, lo, xn))
            second.append(rc(3, o.at[i_xn, lo], o.at[i_xn, lo], yn))
            need2 += [landed(4, i_yn, lo, yn), landed(3, i_dn, lo, yn)]
            if hi is not None:
                start += [rc(2, ins[t].at[hi], o.at[i_me, hi], yn), rc(6, ins[t].at[hi], o.at[i_me, hi], xn)]
                need1.append(landed(2, i_yn, hi, yn))
                second.append(rc(5, o.at[i_yn, hi], o.at[i_yn, hi], xn))
                need2 += [landed(6, i_xn, hi, xn), landed(5, i_dn, hi, xn)]
            forward += [rc(7, o.at[i_xn], o.at[i_xn], sib), rc(8, o.at[i_yn], o.at[i_yn], sib), rc(9, o.at[i_dn], o.at[i_dn], sib)]
            need3 += [landed(0, j_me, None, sib), landed(7, j_xn, None, sib), landed(8, j_yn, None, sib), landed(9, j_dn, None, sib)]
        return local, start, need1, second, need2, forward, need3

    def start(self, ins, outs, sems):
        local, start, *_ = self._copies(ins, outs, sems)
        for cp in local + start:
            cp.start()

    def second(self, ins, outs, sems):
        _, _, need1, second, *_ = self._copies(ins, outs, sems)
        for cp in need1:
            cp.wait_recv()
        for cp in second:
            cp.start()

    def forward(self, ins, outs, sems):
        _, _, _, _, need2, forward, _ = self._copies(ins, outs, sems)
        for cp in need2:
            cp.wait_recv()
        for cp in forward:
            cp.start()

    def finish(self, ins, outs, sems):
        local, start, _, second, _, forward, need3 = self._copies(ins, outs, sems)
        for cp in need3:
            cp.wait_recv()
        for cp in start + second + forward:
            cp.wait_send()
        for cp in local:
            cp.wait()


def _any_specs(n):
    return [pl.BlockSpec(memory_space=pl.ANY)] * n


def run_plan(plan, *, name):
    def body(*refs):
        n_in, n_out = len(plan.inputs), len(plan.out_shapes)
        ins, outs, sems = refs[:n_in], refs[n_in : n_in + n_out], refs[n_in + n_out :]
        for _, phase in plan.phases(8):
            phase(ins, outs, sems)

    return pl.pallas_call(
        body,
        name=name,
        in_specs=_any_specs(len(plan.inputs)),
        out_specs=_any_specs(len(plan.out_shapes)),
        out_shape=plan.out_shapes,
        scratch_shapes=plan.scratch,
        input_output_aliases=plan.aliases,
    )(*plan.inputs)


_HBM = pl.BlockSpec(memory_space=pltpu.HBM)
_SEM = pl.BlockSpec(memory_space=pltpu.SEMAPHORE)
_DATAFLOW = pltpu.SideEffectType.DATAFLOW_SIDE_EFFECTING


class Exchange:
    def __init__(self):
        self.zones = {}
        self.pending = []
        self.sources = []

    def start(self, items, *, name):
        nt = len(items)
        keys = list(dict.fromkeys(it[2] for it in items))
        for a, _, key, shape, _ in items:
            if key not in self.zones:
                self.zones[key] = lax.empty(shape, a.dtype)
        nz = len(keys)

        def body(*refs):
            ins, zones, sems, token = refs[:nt], refs[nt : nt + nz], refs[nt + nz : nt + nz + 2 * nt], refs[-1]
            x, y, c = _place()
            me = _slot(x, y, c)
            for k in range(1, N_DEV):
                px, py, pc = _flip(x, (k >> 2) & 1), _flip(y, (k >> 1) & 1), _flip(c, k & 1)
                for t, (_, whole, key, _, layer) in enumerate(items):
                    zone = zones[keys.index(key)]
                    pltpu.make_async_remote_copy(
                        src_ref=ins[t] if whole else ins[t].at[_slot(px, py, pc)],
                        dst_ref=zone.at[me] if layer is None else zone.at[me, layer],
                        send_sem=sems[2 * t], recv_sem=sems[2 * t + 1], device_id=(px, py, pc), device_id_type=MESH).start()
            token[...] = jnp.zeros_like(token)

        bufs = [pltpu.with_memory_space_constraint(b, pltpu.HBM) for b in [it[0] for it in items] + [self.zones[k] for k in keys]]
        outs = pl.pallas_call(
            body,
            name=name,
            in_specs=[_HBM] * (nt + nz),
            out_specs=[_SEM] * (2 * nt) + [_HBM] * (nt + nz) + [pl.BlockSpec(memory_space=pltpu.VMEM)],
            out_shape=[pltpu.SemaphoreType.DMA(())] * (2 * nt) + [pltpu.HBM(b.shape, b.dtype) for b in bufs]
            + [jax.ShapeDtypeStruct((8, LANES), F32)],
            input_output_aliases={i: 2 * nt + i for i in range(nt + nz)},
            compiler_params=pltpu.CompilerParams(has_side_effects=_DATAFLOW),
        )(*bufs)
        for t, (_, _, key, _, layer) in enumerate(items):
            self.pending.append((outs[2 * t], outs[2 * t + 1], key, layer))
        self.sources += [((it[2], it[4]), a) for it, a in zip(items, outs[2 * nt : 3 * nt])]
        for i, key in enumerate(keys):
            self.zones[key] = outs[3 * nt + i]
        return outs[-1]

    def wait(self, after, *, name):
        pending, keys = self.pending, list(self.zones)
        names, sources = [n for n, _ in self.sources], [a for _, a in self.sources]
        ns, nz, npend = len(sources), len(keys), len(pending)
        self.pending, self.sources = [], []

        def body(*refs):
            zones, sems = refs[ns : ns + nz], refs[ns + nz : ns + nz + 2 * npend]
            x, y, c = _place()
            for i, (_, _, key, layer) in enumerate(pending):
                zone = zones[keys.index(key)]
                rows = pl.ds(0, N_DEV - 1)
                seven = zone.at[rows] if layer is None else zone.at[rows, layer]
                pltpu.make_async_remote_copy(
                    src_ref=seven, dst_ref=seven, send_sem=sems[2 * i], recv_sem=sems[2 * i + 1],
                    device_id=(x, y, c), device_id_type=MESH).wait()

        bufs = list(sources) + [self.zones[k] for k in keys]
        flat_sems = [s for p in pending for s in p[:2]]
        outs = pl.pallas_call(
            body,
            name=name,
            in_specs=[_HBM] * (ns + nz) + [_SEM] * (2 * npend) + _any_specs(len(after)),
            out_specs=[_HBM] * (ns + nz),
            out_shape=[pltpu.HBM(b.shape, b.dtype) for b in bufs],
            input_output_aliases={i: i for i in range(ns + nz)},
            compiler_params=pltpu.CompilerParams(has_side_effects=_DATAFLOW),
        )(*bufs, *flat_sems, *after)
        self.zones = {}
        return dict(zip(keys, outs[ns:])), dict(zip(names, outs[:ns]))


def _call(body, *, name, grid, in_specs, out_specs, out_shape, args, scratch=(), plan=None, after=None):
    if after is not None:
        inner, n_real = body, len(in_specs)
        body = lambda *refs: inner(*refs[:n_real], *refs[n_real + 1 :])
        in_specs, args = list(in_specs) + _any_specs(1), list(args) + [after]
    n_in, n_out, n_scr = len(in_specs), len(out_specs), len(scratch)
    if plan is None:
        outs = pl.pallas_call(
            body, name=name, grid=grid, in_specs=in_specs, out_specs=out_specs, out_shape=out_shape,
            scratch_shapes=list(scratch), compiler_params=_params(len(grid)))(*args)
        return outs, None
    c_in, c_out = len(plan.inputs), len(plan.out_shapes)
    phases = plan.phases(math.prod(grid))

    def full(*refs):
        a, refs = refs[:n_in], refs[n_in:]
        ci, refs = refs[:c_in], refs[c_in:]
        o, refs = refs[:n_out], refs[n_out:]
        co, refs = refs[:c_out], refs[c_out:]
        s, cs = refs[:n_scr], refs[n_scr:]
        step = pl.program_id(0)
        for d in range(1, len(grid)):
            step = step * grid[d] + pl.program_id(d)
        for at, phase in phases:
            if at == 0:
                pl.when(step == 0)(functools.partial(phase, ci, co, cs))
        body(*a, *o, *s)
        for at, phase in phases:
            if at > 0:
                pl.when(step == at)(functools.partial(phase, ci, co, cs))

    outs = pl.pallas_call(
        full,
        name=name,
        grid=grid,
        in_specs=list(in_specs) + _any_specs(c_in),
        out_specs=list(out_specs) + _any_specs(c_out),
        out_shape=list(out_shape) + plan.out_shapes,
        scratch_shapes=list(scratch) + plan.scratch,
        input_output_aliases={n_in + i: n_out + t for i, t in plan.aliases.items()},
        compiler_params=_params(len(grid)),
    )(*args, *plan.inputs)
    return outs[:n_out], outs[n_out:]


def _row_tile(R):
    return 256 if R % 256 == 0 else R


def cols_from_shards(a):
    n, R, C = a.shape
    tr = _row_tile(R)

    def body(i_ref, o_ref):
        for s in range(n):
            o_ref[:, s * C : (s + 1) * C] = i_ref[s]

    return pl.pallas_call(
        body,
        name="cols_from_shards",
        grid=(R // tr,),
        in_specs=[pl.BlockSpec((n, tr, C), lambda i: (0, i, 0))],
        out_specs=pl.BlockSpec((tr, n * C), lambda i: (i, 0)),
        out_shape=jax.ShapeDtypeStruct((R, n * C), a.dtype),
        compiler_params=_params(1),
    )(a)


def shards_from_cols(a):
    R, W = a.shape
    C = W // N_DEV
    tr = _row_tile(R)

    def body(i_ref, o_ref):
        for s in range(N_DEV):
            o_ref[s] = i_ref[:, s * C : (s + 1) * C]

    return pl.pallas_call(
        body,
        name="shards_from_cols",
        grid=(R // tr,),
        in_specs=[pl.BlockSpec((tr, W), lambda i: (i, 0))],
        out_specs=pl.BlockSpec((N_DEV, tr, C), lambda i: (0, i, 0)),
        out_shape=jax.ShapeDtypeStruct((N_DEV, R, C), a.dtype),
        compiler_params=_params(1),
    )(a)


def _shift_down(u, prev8, row, n):
    out = pltpu.roll(u, n, 0)
    for k in range(n):
        out = jnp.where(row == k, prev8[8 - n + k : 8 - n + k + 1, :], out)
    return out


def _shift_up(u, next8, row, n, tm):
    out = pltpu.roll(u, tm - n, 0)
    for k in range(n):
        out = jnp.where(row == tm - n + k, next8[k : k + 1, :], out)
    return out


def conv_fwd(x, gain, w_in, cw, w_out, *, seq, tm, plan=None):
    T, D = x.shape
    tps = seq // tm

    def body(x_ref, g_ref, win_ref, cw_ref, wout_ref, x1_ref, bcx_ref, y_ref, z_ref, carry_ref):
        i = pl.program_id(0)

        @pl.when(i % tps == 0)
        def _():
            carry_ref[...] = jnp.zeros_like(carry_ref)

        xt = x_ref[...]
        h = ((xt * _rms(xt)) * g_ref[...]).astype(BF16)
        bcx = _dot(h, win_ref[...])
        bcx_ref[...] = bcx.astype(BF16)
        b, c, xv = bcx[:, :D], bcx[:, D : 2 * D], bcx[:, 2 * D :]
        u = b * xv
        row = lax.broadcasted_iota(jnp.int32, u.shape, 0)
        prev = carry_ref[...]
        u1 = _shift_down(u, prev, row, 1)
        u2 = _shift_down(u, prev, row, 2)
        carry_ref[...] = u[tm - 8 :, :]
        cwv = cw_ref[...]
        y = cwv[0:1, :] * u2 + cwv[1:2, :] * u1 + cwv[2:3, :] * u
        y_ref[...] = y
        z = (c * y).astype(BF16)
        z_ref[...] = z
        x1_ref[...] = xt + _dot(z, wout_ref[...])

    tile = pl.BlockSpec((tm, D), lambda i: (i, 0))
    return _call(
        body,
        plan=plan,
        args=(x, gain, w_in, cw, w_out),
        name="conv_fwd",
        grid=(T // tm,),
        in_specs=[
            pl.BlockSpec((tm, D), lambda i: (i, 0)),
            _resident((1, D)),
            _resident((D, 3 * D)),
            _resident((3, D)),
            _resident((D, D)),
        ],
        out_specs=[tile, pl.BlockSpec((tm, 3 * D), lambda i: (i, 0)), tile, tile],
        out_shape=[jax.ShapeDtypeStruct((T, D), F32), jax.ShapeDtypeStruct((T, 3 * D), BF16),
                   jax.ShapeDtypeStruct((T, D), F32), jax.ShapeDtypeStruct((T, D), BF16)],
        scratch=[pltpu.VMEM((8, D), F32)],
    )


def conv_bwd(dx1, x, gain, bcx, y, cw, w_in, w_out, *, seq, tm, after=None):
    T, D = x.shape
    n = T // tm
    tps = seq // tm

    def body(d_ref, x_ref, g_ref, bcx_ref, y_ref, cw_ref, win_ref, wout_ref,
             gx_ref, dbcx_ref, h_ref, d16_ref, dcw_ref, dg_ref, carry_ref):
        i = pl.program_id(0)
        t = n - 1 - i

        @pl.when(i == 0)
        def _():
            dcw_ref[...] = jnp.zeros_like(dcw_ref)
            dg_ref[...] = jnp.zeros_like(dg_ref)

        @pl.when(t % tps == tps - 1)
        def _():
            carry_ref[...] = jnp.zeros_like(carry_ref)

        d = d_ref[...]
        d16 = d.astype(BF16)
        d16_ref[...] = d16
        dz = _dot_nt(d16, wout_ref[...])
        bcx = bcx_ref[...].astype(F32)
        b, c, xv = bcx[:, :D], bcx[:, D : 2 * D], bcx[:, 2 * D :]
        u = b * xv
        row = lax.broadcasted_iota(jnp.int32, u.shape, 0)
        cwv = cw_ref[...]
        dc = dz * y_ref[...]
        dy = dz * c
        nxt = carry_ref[...]
        dy1 = _shift_up(dy, nxt, row, 1, tm)
        dy2 = _shift_up(dy, nxt, row, 2, tm)
        carry_ref[...] = dy[0:8, :]
        dcw_ref[0:1, :] += jnp.sum(dy2 * u, axis=0, keepdims=True)
        dcw_ref[1:2, :] += jnp.sum(dy1 * u, axis=0, keepdims=True)
        dcw_ref[2:3, :] += jnp.sum(dy * u, axis=0, keepdims=True)
        du = cwv[2:3, :] * dy + cwv[1:2, :] * dy1 + cwv[0:1, :] * dy2
        dbcx_ref[:, :D] = (du * xv).astype(BF16)
        dbcx_ref[:, D : 2 * D] = dc.astype(BF16)
        dbcx_ref[:, 2 * D :] = (du * b).astype(BF16)
        dh = _dot_nt(dbcx_ref[...], win_ref[...])
        xt = x_ref[...]
        r = _rms(xt)
        gn = g_ref[...]
        h_ref[...] = ((xt * r) * gn).astype(BF16)
        dx, dgn = _rms_bwd(xt, r, gn, dh)
        dg_ref[0:1, :] += dgn
        gx_ref[...] = d + dx

    rev = lambda i: (n - 1 - i, 0)
    return _call(
        body,
        after=after,
        args=(dx1, x, gain, bcx, y, cw, w_in, w_out),
        name="conv_bwd",
        grid=(n,),
        in_specs=[
            pl.BlockSpec((tm, D), rev),
            pl.BlockSpec((tm, D), rev),
            _resident((1, D)),
            pl.BlockSpec((tm, 3 * D), rev),
            pl.BlockSpec((tm, D), rev),
            _resident((3, D)),
            _resident((D, 3 * D)),
            _resident((D, D)),
        ],
        out_specs=[
            pl.BlockSpec((tm, D), rev),
            pl.BlockSpec((tm, 3 * D), rev),
            pl.BlockSpec((tm, D), rev),
            pl.BlockSpec((tm, D), rev),
            pl.BlockSpec((8, D), lambda i: (0, 0)),
            pl.BlockSpec((8, D), lambda i: (0, 0)),
        ],
        out_shape=[
            jax.ShapeDtypeStruct((T, D), F32),
            jax.ShapeDtypeStruct((T, 3 * D), BF16),
            jax.ShapeDtypeStruct((T, D), BF16),
            jax.ShapeDtypeStruct((T, D), BF16),
            jax.ShapeDtypeStruct((8, D), F32),
            jax.ShapeDtypeStruct((8, D), F32),
        ],
        scratch=[pltpu.VMEM((8, D), F32)],
    )


MXU_TILE = 256
FFN_CHUNK = 4 * MXU_TILE


def _sigmoid(g):
    return 1.0 / (1.0 + jnp.exp(-g))


def _ffn_chunks(F):
    assert F % MXU_TILE == 0
    return [(s, min(FFN_CHUNK, F - s)) for s in range(0, F, FFN_CHUNK)]


def ffn_fwd(x, gain, w_gu, w_d, *, tm, plan=None, attn=None, target=None):
    T, D = x.shape
    F = w_d.shape[0]
    row = lambda i: (i, 0)
    tile = pl.BlockSpec((tm, D), row)

    def body(*refs):
        refs = list(refs)
        x_ref, g_ref, wgu_ref, wd_ref = refs[:4]
        del refs[:4]
        if attn is not None:
            ao_ref, wo_ref = refs[:2]
            del refs[:2]
        if target is not None:
            t_ref = refs.pop(0)
        if attn is not None:
            xin_ref = refs.pop(0)
        xo_ref, gu_ref = refs[:2]
        xt = x_ref[...]
        if attn is not None:
            xt = xt + _dot(ao_ref[...], wo_ref[...])
            xin_ref[...] = xt
        h = ((xt * _rms(xt)) * g_ref[...]).astype(BF16)
        acc = xt
        for s, n in _ffn_chunks(F):
            g = _dot(h, wgu_ref[:, s : s + n])
            u = _dot(h, wgu_ref[:, F + s : F + s + n])
            gu_ref[:, s : s + n] = g
            gu_ref[:, F + s : F + s + n] = u
            a = ((g * _sigmoid(g)) * u).astype(BF16)
            acc = acc + _dot(a, wd_ref[s : s + n, :])
        if target is None:
            xo_ref[...] = acc
        else:
            s_ref = refs[2]

            @pl.when(pl.program_id(0) == 0)
            def _():
                s_ref[...] = jnp.zeros_like(s_ref)

            e = acc - t_ref[...]
            xo_ref[...] = e * (1.0 / D)
            s_ref[...] += jnp.sum(jnp.sum(e * e, axis=-1, keepdims=True), axis=0, keepdims=True)

    args = [x, gain, w_gu, w_d]
    in_specs = [tile, _resident((1, D)), _resident((D, 2 * F)), _resident((F, D))]
    out_specs = [tile, pl.BlockSpec((tm, 2 * F), row)]
    out_shape = [jax.ShapeDtypeStruct((T, D), F32), jax.ShapeDtypeStruct((T, 2 * F), F32)]
    if attn is not None:
        args += list(attn)
        in_specs += [pl.BlockSpec((tm, attn[0].shape[1]), row), _resident(attn[1].shape)]
        out_specs.insert(0, tile)
        out_shape.insert(0, jax.ShapeDtypeStruct((T, D), F32))
    if target is not None:
        args.append(target)
        in_specs.append(tile)
        out_specs.append(pl.BlockSpec((8, LANES), lambda i: (0, 0)))
        out_shape.append(jax.ShapeDtypeStruct((8, LANES), F32))
    return _call(body, plan=plan, args=args, name="ffn_fwd", grid=(T // tm,), in_specs=in_specs, out_specs=out_specs,
                 out_shape=out_shape)


def ffn_bwd(dxo, x, gain, gu, w_gu, w_d, *, tm, after=None, w_o=None):
    T, D = x.shape
    F = w_d.shape[0]

    def body(d_ref, x_ref, g_ref, gu_ref, wgu_ref, wd_ref, *rest):
        if w_o is not None:
            wo_ref, rest = rest[0], rest[1:]
        dx_ref, a_ref, dgu_ref, h_ref, d16_ref, dg_ref = rest[:6]

        @pl.when(pl.program_id(0) == 0)
        def _():
            dg_ref[...] = jnp.zeros_like(dg_ref)

        d = d_ref[...]
        d16 = d.astype(BF16)
        d16_ref[...] = d16
        dh = jnp.zeros((tm, D), F32)
        for c0, n in _ffn_chunks(F):
            g = gu_ref[:, c0 : c0 + n]
            u = gu_ref[:, F + c0 : F + c0 + n]
            da = _dot_nt(d16, wd_ref[c0 : c0 + n, :])
            s = _sigmoid(g)
            sg = g * s
            a_ref[:, c0 : c0 + n] = (sg * u).astype(BF16)
            dg16 = (da * u * (s + sg * (1.0 - s))).astype(BF16)
            du16 = (da * sg).astype(BF16)
            dgu_ref[:, c0 : c0 + n] = dg16
            dgu_ref[:, F + c0 : F + c0 + n] = du16
            dh = dh + _dot_nt(dg16, wgu_ref[:, c0 : c0 + n]) + _dot_nt(du16, wgu_ref[:, F + c0 : F + c0 + n])
        xt = x_ref[...]
        r = _rms(xt)
        gn = g_ref[...]
        h_ref[...] = ((xt * r) * gn).astype(BF16)
        dx, dgn = _rms_bwd(xt, r, gn, dh)
        dg_ref[0:1, :] += dgn
        dxi = d + dx
        dx_ref[...] = dxi
        if w_o is not None:
            dxi16_ref, dao_ref = rest[6:8]
            dxi16 = dxi.astype(BF16)
            dxi16_ref[...] = dxi16
            dao_ref[...] = _dot_nt(dxi16, wo_ref[...]).astype(BF16)

    tile = pl.BlockSpec((tm, D), lambda i: (i, 0))
    args = [dxo, x, gain, gu, w_gu, w_d]
    wide = lambda n: pl.BlockSpec((tm, n), lambda i: (i, 0))
    in_specs = [tile, tile, _resident((1, D)), wide(2 * F), _resident((D, 2 * F)), _resident((F, D))]
    out_specs = [tile, wide(F), wide(2 * F), tile, tile, pl.BlockSpec((8, D), lambda i: (0, 0))]
    out_shape = [
        jax.ShapeDtypeStruct((T, D), F32),
        jax.ShapeDtypeStruct((T, F), BF16),
        jax.ShapeDtypeStruct((T, 2 * F), BF16),
        jax.ShapeDtypeStruct((T, D), BF16),
        jax.ShapeDtypeStruct((T, D), BF16),
        jax.ShapeDtypeStruct((8, D), F32),
    ]
    if w_o is not None:
        args.append(w_o)
        in_specs.append(_resident(w_o.shape))
        out_specs += [tile, pl.BlockSpec((tm, w_o.shape[0]), lambda i: (i, 0))]
        out_shape += [jax.ShapeDtypeStruct((T, D), BF16), jax.ShapeDtypeStruct((T, w_o.shape[0]), BF16)]
    return _call(body, after=after, args=args, name="ffn_bwd", grid=(T // tm,), in_specs=in_specs, out_specs=out_specs,
                 out_shape=out_shape)


def wgrad(a, b, *, name, a_cols=0, b_cols=0, group=1, flat=False, tk, out_dtype=BF16, after=None):
    T, K = a.shape
    J = 1
    if a_cols:
        K = a_cols
        J = a.shape[1] // K
        a_spec = pl.BlockSpec((tk, K), lambda j, k: (k, j))
    else:
        a_spec = pl.BlockSpec((tk, K), lambda j, k: (k, 0))
    if b_cols:
        N = b_cols * group
        J = b.shape[1] // N
        b_spec = pl.BlockSpec((tk, N), lambda j, k: (k, j))
    else:
        N = b.shape[1]
        b_spec = pl.BlockSpec((tk, N), lambda j, k: (k, 0))
    nk = T // tk
    if flat:
        o_spec, o_shape = pl.BlockSpec((K, N), lambda j, k: (0, j)), (K, J * N)
    elif group > 1:
        o_spec, o_shape = pl.BlockSpec((group, K, b_cols), lambda j, k: (j, 0, 0)), (J * group, K, b_cols)
    else:
        o_spec, o_shape = pl.BlockSpec((None, K, N), lambda j, k: (j, 0, 0)), (J, K, N)

    def body(a_ref, b_ref, o_ref, acc_ref):
        k = pl.program_id(1)

        @pl.when(k == 0)
        def _():
            acc_ref[...] = jnp.zeros_like(acc_ref)

        acc_ref[...] += _dot_tn(a_ref[...], b_ref[...])

        @pl.when(k == nk - 1)
        def _():
            if group > 1 and not flat:
                for i in range(group):
                    o_ref[i] = acc_ref[:, i * b_cols : (i + 1) * b_cols].astype(out_dtype)
            else:
                o_ref[...] = acc_ref[...].astype(out_dtype)

    outs, _ = _call(
        body,
        after=after,
        name=name,
        grid=(J, nk),
        in_specs=[a_spec, b_spec],
        out_specs=[o_spec],
        out_shape=[jax.ShapeDtypeStruct(o_shape, out_dtype)],
        args=(a, b),
        scratch=[pltpu.VMEM((K, N), F32)],
    )
    return outs[0]


def _seg(xs, lo):
    s_lo = [jnp.sum(jnp.where(lo, x, 0.0), axis=-1, keepdims=True) for x in xs]
    s_hi = [jnp.sum(jnp.where(lo, 0.0, x), axis=-1, keepdims=True) for x in xs]
    return [jnp.where(lo, a, b) for a, b in zip(s_lo, s_hi)]


def _head_norm(xs, gains, lo):
    rs = [lax.rsqrt(s * (1.0 / HEAD_DIM) + EPS) for s in _seg([x * x for x in xs], lo)]
    return [(x * r) * g for x, r, g in zip(xs, rs, gains)], rs


def _head_norm_bwd(xs, rs, gains, dys, lo):
    xns = [x * r for x, r in zip(xs, rs)]
    dxns = [dy * g for dy, g in zip(dys, gains)]
    means = [s * (1.0 / HEAD_DIM) for s in _seg([a * b for a, b in zip(dxns, xns)], lo)]
    dxs = [r * (dxn - xn * m) for r, dxn, xn, m in zip(rs, dxns, xns, means)]
    return dxs, [jnp.sum(dy * xn, axis=0, keepdims=True) for dy, xn in zip(dys, xns)]


def _swap_halves(x):
    return pltpu.roll(x, HEAD_DIM, 1)


def qkv_proj(x, gain, w, qg, kg, *, tm):
    T, D = x.shape
    N = w.shape[1]
    kvw = N_KV_HEADS * HEAD_DIM
    nqt, nkt = D // LANES, kvw // LANES

    def body(x_ref, g_ref, w_ref, qg_ref, kg_ref, qkv_ref, q_ref, kd_ref, vd_ref):
        xt = x_ref[...]
        h = ((xt * _rms(xt)) * g_ref[...]).astype(BF16)
        qkv = _dot(h, w_ref[...])
        qkv_ref[...] = qkv
        lo = lax.broadcasted_iota(jnp.int32, (1, LANES), 1) < HEAD_DIM
        tiles = [qkv[:, t * LANES : (t + 1) * LANES] for t in range(nqt + nkt)]
        normed, _ = _head_norm(tiles, [qg_ref[...]] * nqt + [kg_ref[...]] * nkt, lo)
        for t in range(nqt):
            q_ref[:, t * LANES : (t + 1) * LANES] = (normed[t] * SCALE).astype(BF16)
        for t in range(nkt):
            kn = normed[nqt + t]
            v = qkv[:, D + kvw + t * LANES : D + kvw + (t + 1) * LANES]
            for src, dst in ((kn, kd_ref), (v, vd_ref)):
                sw = _swap_halves(src)
                dst[:, 2 * t * LANES : (2 * t + 1) * LANES] = jnp.where(lo, src, sw).astype(BF16)
                dst[:, (2 * t + 1) * LANES : (2 * t + 2) * LANES] = jnp.where(lo, sw, src).astype(BF16)

    row = lambda i: (i, 0)
    return pl.pallas_call(
        body,
        name="qkv_proj",
        grid=(T // tm,),
        in_specs=[pl.BlockSpec((tm, D), row), _resident((1, D)), _resident((D, N)), _resident((1, LANES)), _resident((1, LANES))],
        out_specs=[pl.BlockSpec((tm, N), row), pl.BlockSpec((tm, D), row), pl.BlockSpec((tm, 2 * kvw), row), pl.BlockSpec((tm, 2 * kvw), row)],
        out_shape=[
            jax.ShapeDtypeStruct((T, N), F32),
            jax.ShapeDtypeStruct((T, D), BF16),
            jax.ShapeDtypeStruct((T, 2 * kvw), BF16),
            jax.ShapeDtypeStruct((T, 2 * kvw), BF16),
        ],
        compiler_params=_params(1),
    )(x, gain, w, qg, kg)


def _attn_tables(sinks, n_q_heads):
    P = n_q_heads // N_KV_HEADS // 2
    h = jnp.arange(1, n_q_heads + 1, dtype=F32)
    slopes = jnp.exp2(-8.0 * h / n_q_heads).reshape(N_KV_HEADS, P, 1, 2, 1)
    qi = jnp.arange(BLOCK)[:, None]
    kj = jnp.arange(BLOCK)[None, :]
    dist = jnp.where(kj <= qi, qi - kj, qi + BLOCK - kj).astype(F32)
    shape = (N_KV_HEADS, P, BLOCK, 2, BLOCK)
    bias = jnp.broadcast_to(-slopes * dist[None, None, :, None, :], shape)
    sink = jnp.broadcast_to(sinks.astype(F32).reshape(N_KV_HEADS, P, 1, 2, 1), shape)
    return bias.reshape(N_KV_HEADS, P * BLOCK, 2 * BLOCK), sink.reshape(N_KV_HEADS, P * BLOCK, 2 * BLOCK)


def _attn_specs(D, nb):
    kvw2 = 2 * N_KV_HEADS * HEAD_DIM
    cur = lambda b, i: (b * nb + i, 0)
    prev = lambda b, i: (jnp.maximum(b * nb + i - 1, 0), 0)
    return [
        pl.BlockSpec((BLOCK, D), cur),
        pl.BlockSpec((BLOCK, kvw2), cur),
        pl.BlockSpec((BLOCK, kvw2), prev),
        pl.BlockSpec((BLOCK, kvw2), cur),
        pl.BlockSpec((BLOCK, kvw2), prev),
    ]


def _attn_operands(kh, P, lo, q_ref, kc_ref, kp_ref, vc_ref, vp_ref):
    sl = slice(kh * LANES, (kh + 1) * LANES)

    def cat(prev_ref, cur_ref):
        d = jnp.concatenate([prev_ref[:, sl], cur_ref[:, sl]], axis=0)
        z = jnp.zeros_like(d)
        return jnp.concatenate([jnp.where(lo, d, z), jnp.where(lo, z, d)], axis=0)

    qt = jnp.concatenate([q_ref[:, (kh * P + pr) * LANES : (kh * P + pr + 1) * LANES] for pr in range(P)], axis=0)
    return qt, cat(kp_ref, kc_ref), cat(vp_ref, vc_ref)


def _attn_exp(s_all, bias, sink, tri, first):
    out = []
    for par in range(2):
        c0 = 2 * par * BLOCK
        s = jnp.where(tri, s_all[:, c0 + BLOCK : c0 + 2 * BLOCK], jnp.where(first, NEG, s_all[:, c0 : c0 + BLOCK]))
        s = s + bias[:, par * BLOCK : (par + 1) * BLOCK]
        snk = sink[:, par * BLOCK : (par + 1) * BLOCK]
        m = jnp.maximum(jnp.max(s, axis=-1, keepdims=True), snk)
        out.append((jnp.exp(s - m), jnp.exp(snk - m)))
    return out


def _unfold(x, tri):
    z = jnp.zeros_like(x)
    return jnp.concatenate([jnp.where(tri, z, x), jnp.where(tri, x, z)], axis=1)


def _attn_masks(R):
    lane = lax.broadcasted_iota(jnp.int32, (1, LANES), 1)
    row = lax.broadcasted_iota(jnp.int32, (R, BLOCK), 0) & (BLOCK - 1)
    col = lax.broadcasted_iota(jnp.int32, (R, BLOCK), 1)
    return lane, lane < HEAD_DIM, col <= row


def attn_fwd(q16, kd, vd, bias, sink, *, seq, n_seq):
    T, D = q16.shape
    nb = seq // BLOCK
    P = D // HEAD_DIM // N_KV_HEADS // 2
    R = P * BLOCK
    KV = range(N_KV_HEADS)

    def body(q_ref, kc_ref, kp_ref, vc_ref, vp_ref, bias_ref, sink_ref, o_ref):
        first = pl.program_id(1) == 0
        _, lo, tri = _attn_masks(R)
        ops = [_attn_operands(kh, P, lo, q_ref, kc_ref, kp_ref, vc_ref, vp_ref) for kh in KV]
        s_all = [_dot_nt(ops[kh][0], ops[kh][1]) for kh in KV]
        ex = [_attn_exp(s_all[kh], bias_ref[kh], sink_ref[kh], tri, first) for kh in KV]
        den = [[jnp.sum(e, axis=-1, keepdims=True) + es for e, es in ex[kh]] for kh in KV]
        lhs = [jnp.concatenate([_unfold(e, tri) for e, _ in ex[kh]], axis=1).astype(BF16) for kh in KV]
        o = [_dot(lhs[kh], ops[kh][2]) for kh in KV]
        for kh in KV:
            out = o[kh] / jnp.where(lo, den[kh][0], den[kh][1])
            for pr in range(P):
                t = kh * P + pr
                o_ref[:, t * LANES : (t + 1) * LANES] = out[pr * BLOCK : (pr + 1) * BLOCK, :].astype(BF16)

    return pl.pallas_call(
        body,
        name="attn_fwd",
        grid=(n_seq, nb),
        in_specs=_attn_specs(D, nb) + [_resident((N_KV_HEADS, R, 2 * BLOCK)), _resident((N_KV_HEADS, R, 2 * BLOCK))],
        out_specs=pl.BlockSpec((BLOCK, D), lambda b, i: (b * nb + i, 0)),
        out_shape=jax.ShapeDtypeStruct((T, D), BF16),
        compiler_params=_params(2),
    )(q16, kd, kd, vd, vd, bias, sink)


def attn_bwd(q16, kd, vd, do, bias, sink, *, seq, n_seq):
    T, D = q16.shape
    kvw2 = 2 * N_KV_HEADS * HEAD_DIM
    nb = seq // BLOCK
    G = D // HEAD_DIM // N_KV_HEADS
    P = G // 2
    R = P * BLOCK
    KV = range(N_KV_HEADS)

    def body(q_ref, kc_ref, kp_ref, vc_ref, vp_ref, do_ref, bias_ref, sink_ref,
             dq_ref, dkc_ref, dkp_ref, dvc_ref, dvp_ref, dsink_ref):
        first = pl.program_id(1) == 0

        @pl.when(jnp.logical_and(pl.program_id(0) == 0, first))
        def _():
            dsink_ref[...] = jnp.zeros_like(dsink_ref)

        lane, lo, tri = _attn_masks(R)
        ops = [_attn_operands(kh, P, lo, q_ref, kc_ref, kp_ref, vc_ref, vp_ref) for kh in KV]
        do16 = [jnp.concatenate([do_ref[:, (kh * P + pr) * LANES : (kh * P + pr + 1) * LANES] for pr in range(P)], axis=0)
                for kh in KV]
        s_all = [_dot_nt(ops[kh][0], ops[kh][1]) for kh in KV]
        dp_all = [_dot_nt(do16[kh], ops[kh][2]) for kh in KV]
        ex = [_attn_exp(s_all[kh], bias_ref[kh], sink_ref[kh], tri, first) for kh in KV]
        den = [[jnp.sum(e, axis=-1, keepdims=True) for e, _ in ex[kh]] for kh in KV]
        dsink = jnp.zeros((1, LANES), F32)
        pf, dsf = [], []
        for kh in KV:
            ps_, ds_ = [], []
            for par in range(2):
                e, es = ex[kh][par]
                inv = 1.0 / (den[kh][par] + es)
                p = e * inv
                c0 = 2 * par * BLOCK
                dp = jnp.where(tri, dp_all[kh][:, c0 + BLOCK : c0 + 2 * BLOCK], dp_all[kh][:, c0 : c0 + BLOCK])
                delta = jnp.sum(p * dp, axis=-1, keepdims=True)
                ds_.append(_unfold(p * (dp - delta), tri))
                ps_.append(_unfold(p, tri))
                dsr = -((es * inv) * delta)
                for pr in range(P):
                    hq = kh * G + 2 * pr + par
                    tot = jnp.sum(dsr[pr * BLOCK : (pr + 1) * BLOCK, :], axis=0, keepdims=True)
                    dsink = dsink + jnp.where(lane == hq, tot, 0.0)
            pf.append(jnp.concatenate(ps_, axis=1).astype(BF16))
            dsf.append(jnp.concatenate(ds_, axis=1).astype(BF16))
        dq = [_dot(dsf[kh], ops[kh][1]) for kh in KV]
        dk = [_dot_tn(dsf[kh], ops[kh][0]) for kh in KV]
        dv = [_dot_tn(pf[kh], do16[kh]) for kh in KV]
        dsink_ref[0:1, :] += dsink
        for kh in KV:
            sl = slice(kh * LANES, (kh + 1) * LANES)
            for pr in range(P):
                t = kh * P + pr
                dq_ref[:, t * LANES : (t + 1) * LANES] = dq[kh][pr * BLOCK : (pr + 1) * BLOCK, :]
            for full, prev_ref, cur_ref in ((dk[kh], dkp_ref, dkc_ref), (dv[kh], dvp_ref, dvc_ref)):
                dup = jnp.where(lo, full[: 2 * BLOCK, :], full[2 * BLOCK :, :])
                prev_ref[:, sl] = dup[:BLOCK, :].astype(BF16)
                cur_ref[:, sl] = dup[BLOCK:, :].astype(BF16)

    cur = lambda b, i: (b * nb + i, 0)
    kv_spec = pl.BlockSpec((BLOCK, kvw2), cur)
    kv_shape = jax.ShapeDtypeStruct((T, kvw2), BF16)
    return pl.pallas_call(
        body,
        name="attn_bwd",
        grid=(n_seq, nb),
        in_specs=_attn_specs(D, nb)
        + [pl.BlockSpec((BLOCK, D), cur), _resident((N_KV_HEADS, R, 2 * BLOCK)), _resident((N_KV_HEADS, R, 2 * BLOCK))],
        out_specs=[pl.BlockSpec((BLOCK, D), cur), kv_spec, kv_spec, kv_spec, kv_spec, pl.BlockSpec((8, LANES), lambda b, i: (0, 0))],
        out_shape=[jax.ShapeDtypeStruct((T, D), F32), kv_shape, kv_shape, kv_shape, kv_shape, jax.ShapeDtypeStruct((8, LANES), F32)],
        compiler_params=_params(2),
    )(q16, kd, kd, vd, vd, do, bias, sink)


def qkv_bwd(dq, dkc, dkp, dvc, dvp, qkv, dres, x, gain, w_qkv, qg, kg, *, seq, tm):
    T, D = x.shape
    kvw2 = dkc.shape[1]
    kvw = kvw2 // 2
    nqt, nkt = D // LANES, kvw // LANES
    nb = seq // BLOCK
    nbt = tm // BLOCK
    assert nb % nbt == 0
    n = T // tm

    def body(dq_ref, dkc_ref, dkpa_ref, dkpb_ref, dvc_ref, dvpa_ref, dvpb_ref, qkv_ref, dres_ref, x_ref, g_ref, w_ref,
             qg_ref, kg_ref, dx_ref, dqkv_ref, h_ref, dg_ref, hg_ref):
        i = pl.program_id(0)

        @pl.when(i == 0)
        def _():
            dg_ref[...] = jnp.zeros_like(dg_ref)
            hg_ref[...] = jnp.zeros_like(hg_ref)

        lo = lax.broadcasted_iota(jnp.int32, (1, LANES), 1) < HEAD_DIM
        last = ((i + 1) * nbt) % nb == 0
        up = lambda ref: ref[...].astype(F32)

        def with_next(cur_ref, own_ref, next_ref):
            return up(cur_ref) + jnp.concatenate([up(own_ref)[BLOCK:, :], jnp.where(last, 0.0, up(next_ref))], axis=0)

        dkd = with_next(dkc_ref, dkpa_ref, dkpb_ref)
        dvd = with_next(dvc_ref, dvpa_ref, dvpb_ref)

        def undup(d, t):
            a, b = d[:, 2 * t * LANES : (2 * t + 1) * LANES], d[:, (2 * t + 1) * LANES : (2 * t + 2) * LANES]
            return jnp.where(lo, a + _swap_halves(a), b + _swap_halves(b))

        tiles = [qkv_ref[:, t * LANES : (t + 1) * LANES] for t in range(nqt + nkt)]
        gains = [qg_ref[...]] * nqt + [kg_ref[...]] * nkt
        dys = [dq_ref[:, t * LANES : (t + 1) * LANES] * SCALE for t in range(nqt)] + [undup(dkd, t) for t in range(nkt)]
        _, rs = _head_norm(tiles, gains, lo)
        dxs, dgs = _head_norm_bwd(tiles, rs, gains, dys, lo)
        for t in range(nqt + nkt):
            dqkv_ref[:, t * LANES : (t + 1) * LANES] = dxs[t].astype(BF16)
        for t in range(nkt):
            dqkv_ref[:, D + kvw + t * LANES : D + kvw + (t + 1) * LANES] = undup(dvd, t).astype(BF16)
        hg_ref[0:1, :] += functools.reduce(lambda a, b: a + b, dgs[:nqt])
        hg_ref[1:2, :] += functools.reduce(lambda a, b: a + b, dgs[nqt:])
        dh = _dot_nt(dqkv_ref[...], w_ref[...])
        xt = x_ref[...]
        r = _rms(xt)
        gn = g_ref[...]
        h_ref[...] = ((xt * r) * gn).astype(BF16)
        dx, dgn = _rms_bwd(xt, r, gn, dh)
        dg_ref[0:1, :] += dgn
        dx_ref[...] = dres_ref[...] + dx

    row = lambda i: (i, 0)
    nxt_a = pl.BlockSpec((tm, kvw2), row)
    nxt_b = pl.BlockSpec((BLOCK, kvw2), lambda i: (jnp.minimum((i + 1) * nbt, n * nbt - 1), 0))
    return pl.pallas_call(
        body,
        name="qkv_bwd",
        grid=(n,),
        in_specs=[
            pl.BlockSpec((tm, D), row),
            pl.BlockSpec((tm, kvw2), row),
            nxt_a,
            nxt_b,
            pl.BlockSpec((tm, kvw2), row),
            nxt_a,
            nxt_b,
            pl.BlockSpec((tm, D + kvw2), row),
            pl.BlockSpec((tm, D), row),
            pl.BlockSpec((tm, D), row),
            _resident((1, D)),
            _resident((D, D + kvw2)),
            _resident((1, LANES)),
            _resident((1, LANES)),
        ],
        out_specs=[
            pl.BlockSpec((tm, D), row),
            pl.BlockSpec((tm, D + kvw2), row),
            pl.BlockSpec((tm, D), row),
            pl.BlockSpec((8, D), lambda i: (0, 0)),
            pl.BlockSpec((8, LANES), lambda i: (0, 0)),
        ],
        out_shape=[
            jax.ShapeDtypeStruct((T, D), F32),
            jax.ShapeDtypeStruct((T, D + kvw2), BF16),
            jax.ShapeDtypeStruct((T, D), BF16),
            jax.ShapeDtypeStruct((8, D), F32),
            jax.ShapeDtypeStruct((8, LANES), F32),
        ],
        compiler_params=_params(1),
    )(dq, dkc, dkp, dkp, dvc, dvp, dvp, qkv, dres, x, gain, w_qkv, qg, kg)


def local_step(x, target, gains, w, *, seq, tm=256, tm_ffn=256, tm_conv=512, tk=2048, shards=None, ex=None):
    T, D = x.shape
    n_seq = T // seq
    nm, nf, qgain, kgain, sinks = gains
    H = D // HEAD_DIM
    tk, tk_long = min(tk, T), min(2 * tk, T)
    qg2, kg2 = jnp.tile(qgain, (1, 2)), jnp.tile(kgain, (1, 2))
    bias, sinkcol = _attn_tables(sinks, H)

    dist = shards is not None
    w = dict(w)

    plan = _Gather([shards["w_gu"][0], shards["w_d"][0]]) if dist else None
    (x1, bcx, y_conv, z16), got = conv_fwd(x, nm[0:1], w["w_in"], w["cw"], w["w_out"], seq=seq, tm=tm_conv, plan=plan)
    if dist:
        w["w_gu"], w["w_d"] = [cols_from_shards(got[0]), None], [got[1].reshape(-1, D), None]
    plan = _Gather([shards["w_qkv"], shards["w_o"], shards["w_gu"][1], shards["w_d"][1]]) if dist else None
    (x2, gu0), got = ffn_fwd(x1, nf[0:1], w["w_gu"][0], w["w_d"][0], tm=2 * tm_ffn, plan=plan)
    if dist:
        w["w_qkv"], w["w_o"] = cols_from_shards(got[0]), got[1].reshape(D, D)
        w["w_gu"][1], w["w_d"][1] = cols_from_shards(got[2]), got[3].reshape(-1, D)
    qkv, q16, kd, vd = qkv_proj(x2, nm[1:2], w["w_qkv"], qg2, kg2, tm=tm_conv)
    ao = attn_fwd(q16, kd, vd, bias, sinkcol, seq=seq, n_seq=n_seq)
    (x3, dx4, gu1, sse), _ = ffn_fwd(x2, nf[1:2], w["w_gu"][1], w["w_d"][1], tm=tm_ffn, attn=(ao, w["w_o"]), target=target)

    by_dest = lambda a: a.reshape(N_DEV, -1, a.shape[-1])
    gu_cols = 2 * MXU_TILE

    def send(name, *entries):
        if ex is None:
            return None
        items = [(a, False, key, (N_DEV,) + (() if layers is None else (layers,)) + a.shape[1:], layer)
                 for a, key, layer, layers in entries]
        return ex.start(items, name=name)

    (dx3, a16, dgu, h16, d16, dnf1, dx3_16, dao), _ = ffn_bwd(
        dx4, x3, nf[1:2], gu1, w["w_gu"][1], w["w_d"][1], tm=tm, w_o=w["w_o"])
    g_gu1 = shards_from_cols(wgrad(h16, dgu, name="wgrad_gu1", b_cols=gu_cols, flat=True, tk=T))
    g_d1 = by_dest(wgrad(a16, d16, name="wgrad_d1", a_cols=a16.shape[1] // 2, tk=tk))
    g_o = by_dest(wgrad(ao, dx3_16, name="wgrad_o", tk=tk))
    dq, dkc, dkp, dvc, dvp, dsinks = attn_bwd(q16, kd, vd, dao, bias, sinkcol, seq=seq, n_seq=n_seq)
    dx2, dqkv16, h16, dnm1, dgains = qkv_bwd(dq, dkc, dkp, dvc, dvp, qkv, dx3, x2, nm[1:2], w["w_qkv"], qg2, kg2, seq=seq, tm=tm_conv)
    g_qkv = shards_from_cols(wgrad(h16, dqkv16, name="wgrad_qkv", tk=tk)[0])
    tok = send("exchange_layer1", (g_gu1, "w_gu", 1, 2), (g_d1, "w_d", 1, 2), (g_o, "w_o", None, None),
               (g_qkv, "w_qkv", None, None))
    (dx1, a16, dgu, h16, d16, dnf0), _ = ffn_bwd(dx2, x1, nf[0:1], gu0, w["w_gu"][0], w["w_d"][0], tm=tm, after=tok)
    g_gu0 = shards_from_cols(wgrad(h16, dgu, name="wgrad_gu0", b_cols=gu_cols, flat=True, tk=T))
    tok = send("exchange_gu0", (g_gu0, "w_gu", 0, 2))
    g_d0 = by_dest(wgrad(a16, d16, name="wgrad_d0", a_cols=a16.shape[1] // 2, tk=tk, after=tok))
    tok = send("exchange_d0", (g_d0, "w_d", 0, 2))
    (gx, dbcx, h16, d16, dcw, dnm0), _ = conv_bwd(
        dx1, x, nm[0:1], bcx, y_conv, w["cw"], w["w_in"], w["w_out"], seq=seq, tm=tm_conv, after=tok)
    g_out = by_dest(wgrad(z16, d16, name="wgrad_out", tk=tk))
    g_cw = dcw[0:3].reshape(3, N_DEV, D // N_DEV).transpose(1, 0, 2)
    tok = send("exchange_out", (g_out, "w_out", None, None), (g_cw, "cw", None, None))
    g_in = wgrad(h16, dbcx, name="wgrad_in", b_cols=3 * D // N_DEV, group=2, tk=tk_long, after=tok)
    g = dict(w_in=g_in, cw=g_cw, w_out=g_out, w_o=g_o, w_qkv=g_qkv, w_gu=[g_gu0, g_gu1], w_d=[g_d0, g_d1])
    small = dict(nm0=dnm0, nm1=dnm1, nf0=dnf0, nf1=dnf1, gains=dgains, sinks=dsinks)
    return sse, gx, g, small


def _adamw_math(g, w, m, v):
    m = ADAM_B1 * m + (1.0 - ADAM_B1) * g
    v = ADAM_B2 * v + (1.0 - ADAM_B2) * (g * g)
    m_hat = m / (1.0 - ADAM_B1 ** ADAM_STEP)
    v_hat = v / (1.0 - ADAM_B2 ** ADAM_STEP)
    delta = -ADAM_LR * (m_hat / (jnp.sqrt(v_hat) + ADAM_EPS) + ADAM_WD * w)
    return delta, m, v


def adamw(parts, owns, w, m, v, *, name, after=None):
    n, LR, C = parts.shape
    L = len(owns)
    R = LR // L
    tr = R
    for cand in (256, 128, 88, 64, 32, 16, 8):
        if R > cand and R % cand == 0:
            tr = cand
            break
    per_layer = R // tr
    extra = [] if after is None else [after]

    def body(me_ref, p_ref, *rest):
        own_refs, (w_ref, m_ref, v_ref) = rest[:L], rest[L : L + 3]
        g_ref, d_ref, mo_ref, vo_ref = rest[L + 3 + len(extra) :]
        layer = pl.program_id(0) // per_layer
        mine = own_refs[0][...].astype(F32)
        for j in range(1, L):
            mine = jnp.where(layer == j, own_refs[j][...].astype(F32), mine)
        g = None
        for s in range(n):
            share = jnp.where(me_ref[0] == s, mine, p_ref[s].astype(F32))
            g = share if g is None else g + share
        g_ref[...] = g
        d_ref[...], mo_ref[...], vo_ref[...] = _adamw_math(g, w_ref[...], m_ref[...], v_ref[...])

    blk = pl.BlockSpec((tr, C), lambda i, me: (i, 0))
    own_specs = [pl.BlockSpec((None, tr, C), lambda i, me: (me[0], i % per_layer, 0)) if o.ndim == 3
                 else pl.BlockSpec((tr, C), lambda i, me: (i % per_layer, 0)) for o in owns]
    me = (4 * lax.axis_index("x") + 2 * lax.axis_index("y") + lax.axis_index("c")).astype(jnp.int32).reshape(1)
    return pl.pallas_call(
        body,
        name=name,
        grid_spec=pltpu.PrefetchScalarGridSpec(
            num_scalar_prefetch=1,
            grid=(LR // tr,),
            in_specs=[pl.BlockSpec((n, tr, C), lambda i, me: (0, i, 0))] + own_specs + [blk, blk, blk] + _any_specs(len(extra)),
            out_specs=[blk] * 4,
        ),
        out_shape=[jax.ShapeDtypeStruct((LR, C), F32)] * 4,
        compiler_params=_params(1),
    )(me, parts, *owns, w, m, v, *extra)


def pack_small(small, sse, D):
    W = max(D, 2 * LANES)

    def body(nm0, nm1, nf0, nf1, gains, sinks, sse_ref, o_ref):
        o_ref[...] = jnp.zeros_like(o_ref)
        o_ref[0:1, :D] = nm0[0:1, :]
        o_ref[1:2, :D] = nm1[0:1, :]
        o_ref[2:3, :D] = nf0[0:1, :]
        o_ref[3:4, :D] = nf1[0:1, :]
        gq = gains[0:1, :] + pltpu.roll(gains[0:1, :], HEAD_DIM, 1)
        gk = gains[1:2, :] + pltpu.roll(gains[1:2, :], HEAD_DIM, 1)
        lane = lax.broadcasted_iota(jnp.int32, (1, LANES), 1)
        o_ref[4:5, :LANES] = jnp.where(lane < HEAD_DIM, gq, gk)
        o_ref[4:5, LANES : 2 * LANES] = sinks[0:1, :]
        o_ref[5:6, :LANES] = sse_ref[0:1, :] * (0.5 / D)

    return pl.pallas_call(
        body,
        name="pack_small",
        out_shape=jax.ShapeDtypeStruct((8, W), F32),
    )(small["nm0"], small["nm1"], small["nf0"], small["nf1"], small["gains"], small["sinks"], sse)


def _pack_small_params(nm, nf, qg, kg, sk, D):
    W = max(D, 2 * LANES)
    row4 = jnp.concatenate([qg.reshape(-1), kg.reshape(-1), jnp.zeros((LANES - 2 * HEAD_DIM,), F32), sk.reshape(-1)])
    row4 = jnp.pad(row4, (0, W - row4.shape[0]))
    rows = [jnp.pad(r, (0, W - D)) for r in (nm[0], nm[1], nf[0], nf[1])] + [row4]
    return jnp.concatenate([jnp.stack(rows), jnp.zeros((3, W), F32)], axis=0)


def _unpack_small(a, D, H):
    nm = a[0:2, :D]
    nf = a[2:4, :D]
    qg = a[4:5, 0:HEAD_DIM]
    kg = a[4:5, HEAD_DIM : 2 * HEAD_DIM]
    sk = a[4:5, LANES : LANES + H]
    return qg, kg, sk, nm, nf


def kernel(x, conv_w_in, conv_w, conv_w_out, attn_w_qkv, attn_q_gain, attn_k_gain, attn_sinks, attn_w_o, norm_mixer, norm_ffn, ffn_w_gate_up, ffn_w_down, loss_target, m_conv_w_in, m_conv_w, m_conv_w_out, m_attn_w_qkv, m_attn_q_gain, m_attn_k_gain, m_attn_sinks, m_attn_w_o, m_norm_mixer, m_norm_ffn, m_ffn_w_gate_up, m_ffn_w_down, v_conv_w_in, v_conv_w, v_conv_w_out, v_attn_w_qkv, v_attn_q_gain, v_attn_k_gain, v_attn_sinks, v_attn_w_o, v_norm_mixer, v_norm_ffn, v_ffn_w_gate_up, v_ffn_w_down):
    n_seq, seq, D = x.shape
    T = n_seq * seq
    H = D // HEAD_DIM
    L = ffn_w_gate_up.shape[0]

    full = run_plan(_Gather([conv_w_in[0].astype(BF16), conv_w[0], conv_w_out[0].astype(BF16)]), name="gather_conv_weights")
    w = dict(w_in=cols_from_shards(full[0]), cw=full[1].transpose(1, 0, 2).reshape(3, D),
             w_out=full[2].reshape(D, D))
    shards = dict(w_gu=[ffn_w_gate_up[l].astype(BF16) for l in range(L)], w_d=[ffn_w_down[l].astype(BF16) for l in range(L)],
                  w_qkv=attn_w_qkv[0].astype(BF16), w_o=attn_w_o[0].astype(BF16))
    gains = (norm_mixer, norm_ffn, attn_q_gain, attn_k_gain, attn_sinks)
    ex = Exchange()
    sse, gx, g, small = local_step(x.reshape(T, D), loss_target.reshape(T, D), gains, w, seq=seq, shards=shards, ex=ex)
    zones, own = ex.wait([g["w_in"]], name="exchange_wait")

    packed = pack_small(small, sse, D)
    token = ex.start([(g["w_in"], False, "w_in", g["w_in"].shape, None),
                      (packed, True, "small", (N_DEV,) + packed.shape, None)], name="exchange_last")

    def flat(a):
        return a.reshape(-1, a.shape[-1])

    big = [conv_w_in, conv_w, conv_w_out, attn_w_qkv, attn_w_o, ffn_w_gate_up, ffn_w_down]
    big_m = [m_conv_w_in, m_conv_w, m_conv_w_out, m_attn_w_qkv, m_attn_w_o, m_ffn_w_gate_up, m_ffn_w_down]
    big_v = [v_conv_w_in, v_conv_w, v_conv_w_out, v_attn_w_qkv, v_attn_w_o, v_ffn_w_gate_up, v_ffn_w_down]
    keys = ["w_in", "cw", "w_out", "w_qkv", "w_o", "w_gu", "w_d"]

    def update(b, zones, own, after=None):
        zone = zones[keys[b]]
        parts = zone.reshape(N_DEV, -1, zone.shape[-1])
        layers = [None] if zone.ndim == 3 else range(zone.shape[1])
        outs = adamw(parts, [own[(keys[b], l)] for l in layers], flat(big[b]), flat(big_m[b]), flat(big_v[b]),
                     name="adamw_" + keys[b], after=after)
        return [o.reshape(big[b].shape) for o in outs]

    res = [None] + [update(b, zones, own, after=token) for b in range(1, 7)]
    zones, own = ex.wait([r[0] for r in res[1:]], name="exchange_last_wait")
    res[0] = update(0, zones, own)
    sw = _pack_small_params(norm_mixer, norm_ffn, attn_q_gain, attn_k_gain, attn_sinks, D)
    sm = _pack_small_params(m_norm_mixer, m_norm_ffn, m_attn_q_gain, m_attn_k_gain, m_attn_sinks, D)
    sv = _pack_small_params(v_norm_mixer, v_norm_ffn, v_attn_q_gain, v_attn_k_gain, v_attn_sinks, D)
    souts = adamw(zones["small"], [own[("small", None)]], sw, sm, sv, name="adamw_small")
    sres = [_unpack_small(o, D, H) for o in souts]
    loss = souts[0][5, 0]

    def ordered(i):
        r, s = [r[i] for r in res], sres[i]
        return [r[0], r[1], r[2], r[3], s[0], s[1], s[2], r[4], s[3], s[4], r[5], r[6]]

    return (loss, gx.reshape(n_seq, seq, D), *ordered(0), *ordered(1), *ordered(2), *ordered(3))
```

```python
import functools
import math

import jax
import jax.numpy as jnp
from jax import lax
from jax.experimental import pallas as pl
from jax.experimental.pallas import tpu as pltpu

F32 = jnp.float32
BF16 = jnp.bfloat16

EPS = 1e-6
HEAD_DIM = 64
N_KV_HEADS = 4
BLOCK = 128
LANES = 128
N_DEV = 8
NEG = -1e30
SCALE = 1.0 / math.sqrt(HEAD_DIM)

ADAM_LR = 0.001
ADAM_B1 = 0.9
ADAM_B2 = 0.999
ADAM_EPS = 1e-08
ADAM_WD = 0.01
ADAM_STEP = 10

V7X_VMEM_BYTES = 64 * 1024 * 1024
VMEM_LIMIT = V7X_VMEM_BYTES - 2 * 1024 * 1024
MESH = pl.DeviceIdType.MESH

_NT = (((1,), (1,)), ((), ()))
_TN = (((0,), (0,)), ((), ()))


def _params(n_grid):
    return pltpu.CompilerParams(dimension_semantics=("arbitrary",) * n_grid, vmem_limit_bytes=VMEM_LIMIT)


def _resident(shape):
    nd = len(shape)
    return pl.BlockSpec(shape, lambda *_: (0,) * nd, pipeline_mode=pl.Buffered(1))


def _rms(x):
    return lax.rsqrt(jnp.mean(x * x, axis=-1, keepdims=True) + EPS)


def _rms_bwd(x, r, gain, dh):
    xn = x * r
    dxn = dh * gain
    dx = r * (dxn - xn * jnp.mean(dxn * xn, axis=-1, keepdims=True))
    return dx, jnp.sum(dh * xn, axis=0, keepdims=True)


def _dot(a, b):
    return jnp.dot(a, b, preferred_element_type=F32)


def _dot_nt(a, b):
    return lax.dot_general(a, b, _NT, preferred_element_type=F32)


def _dot_tn(a, b):
    return lax.dot_general(a, b, _TN, preferred_element_type=F32)


def _place():
    return lax.axis_index("x"), lax.axis_index("y"), lax.axis_index("c")


def _flip(v, bit):
    return 1 - v if bit else v


def _slot(px, py, pc):
    return 4 * px + 2 * py + pc


class _Gather:
    def __init__(self, shards):
        nt = len(shards)
        self.nt = nt
        self.inputs = list(shards)
        self.out_shapes = [jax.ShapeDtypeStruct((N_DEV,) + s.shape, s.dtype) for s in shards]
        self.scratch = [pltpu.SemaphoreType.DMA((nt, 10)), pltpu.SemaphoreType.DMA((nt, 10)), pltpu.SemaphoreType.DMA((nt,))]
        self.aliases = {}
        self.split = []
        for s in shards:
            rows, tile = s.shape[0], 16 if s.dtype == BF16 else 8
            self.split.append(rows // 2 if rows % (2 * tile) == 0 else rows)

    def phases(self, total):
        assert total >= 8
        return [(0, self.start), (total // 2, self.second), (total - 3, self.forward), (total - 1, self.finish)]

    def _copies(self, ins, outs, sems):
        send_sems, recv_sems, loc_sems = sems
        x, y, c = _place()
        xn, yn, sib = (1 - x, y, c), (x, 1 - y, c), (x, y, 1 - c)
        i_me, i_xn, i_yn, i_dn = _slot(x, y, c), _slot(1 - x, y, c), _slot(x, 1 - y, c), _slot(1 - x, 1 - y, c)
        j_me, j_xn, j_yn, j_dn = _slot(x, y, 1 - c), _slot(1 - x, y, 1 - c), _slot(x, 1 - y, 1 - c), _slot(1 - x, 1 - y, 1 - c)
        local, start, need1, second, need2, forward, need3 = [], [], [], [], [], [], []
        for t in range(self.nt):
            o, rows, h = outs[t], self.inputs[t].shape[0], self.split[t]
            lo = pl.ds(0, h)
            hi = pl.ds(h, rows - h) if h < rows else None

            def rc(k, src, dst, to, t=t):
                return pltpu.make_async_remote_copy(
                    src_ref=src, dst_ref=dst, send_sem=send_sems.at[t, k], recv_sem=recv_sems.at[t, k], device_id=to,
                    device_id_type=MESH)

            def landed(k, slot, part, frm):
                ref = o.at[slot] if part is None else o.at[slot, part]
                return rc(k, ref, ref, frm)

            local.append(pltpu.make_async_copy(ins[t], o.at[i_me], loc_sems.at[t]))
            start += [rc(0, ins[t], o.at[i_me], sib), rc(1, ins[t].at[lo], o.at[i_me, lo], xn),
                      rc(4, ins[t].at[lo], o.at[i_me, lo], yn)]
            need1.append(landed(1, i_xn, lo, xn))
            second.append(rc(3, o.at[i_xn, lo], o.at[i_xn, lo], yn))
            need2 += [landed(4, i_yn, lo, yn), landed(3, i_dn, lo, yn)]
            if hi is not None:
                start += [rc(2, ins[t].at[hi], o.at[i_me, hi], yn), rc(6, ins[t].at[hi], o.at[i_me, hi], xn)]
                need1.append(landed(2, i_yn, hi, yn))
                second.append(rc(5, o.at[i_yn, hi], o.at[i_yn, hi], xn))
                need2 += [landed(6, i_xn, hi, xn), landed(5, i_dn, hi, xn)]
            forward += [rc(7, o.at[i_xn], o.at[i_xn], sib), rc(8, o.at[i_yn], o.at[i_yn], sib), rc(9, o.at[i_dn], o.at[i_dn], sib)]
            need3 += [landed(0, j_me, None, sib), landed(7, j_xn, None, sib), landed(8, j_yn, None, sib), landed(9, j_dn, None, sib)]
        return local, start, need1, second, need2, forward, need3

    def start(self, ins, outs, sems):
        local, start, *_ = self._copies(ins, outs, sems)
        for cp in local + start:
            cp.start()

    def second(self, ins, outs, sems):
        _, _, need1, second, *_ = self._copies(ins, outs, sems)
        for cp in need1:
            cp.wait_recv()
        for cp in second:
            cp.start()

    def forward(self, ins, outs, sems):
        _, _, _, _, need2, forward, _ = self._copies(ins, outs, sems)
        for cp in need2:
            cp.wait_recv()
        for cp in forward:
            cp.start()

    def finish(self, ins, outs, sems):
        local, start, _, second, _, forward, need3 = self._copies(ins, outs, sems)
        for cp in need3:
            cp.wait_recv()
        for cp in start + second + forward:
            cp.wait_send()
        for cp in local:
            cp.wait()


def _any_specs(n):
    return [pl.BlockSpec(memory_space=pl.ANY)] * n


def run_plan(plan, *, name):
    def body(*refs):
        n_in, n_out = len(plan.inputs), len(plan.out_shapes)
        ins, outs, sems = refs[:n_in], refs[n_in : n_in + n_out], refs[n_in + n_out :]
        for _, phase in plan.phases(8):
            phase(ins, outs, sems)

    return pl.pallas_call(
        body,
        name=name,
        in_specs=_any_specs(len(plan.inputs)),
        out_specs=_any_specs(len(plan.out_shapes)),
        out_shape=plan.out_shapes,
        scratch_shapes=plan.scratch,
        input_output_aliases=plan.aliases,
    )(*plan.inputs)


_HBM = pl.BlockSpec(memory_space=pltpu.HBM)
_SEM = pl.BlockSpec(memory_space=pltpu.SEMAPHORE)
_DATAFLOW = pltpu.SideEffectType.DATAFLOW_SIDE_EFFECTING


class Exchange:
    def __init__(self):
        self.zones = {}
        self.pending = []
        self.sources = []

    def start(self, items, *, name):
        nt = len(items)
        keys = list(dict.fromkeys(it[2] for it in items))
        for a, _, key, shape, _ in items:
            if key not in self.zones:
                self.zones[key] = lax.empty(shape, a.dtype)
        nz = len(keys)

        def body(*refs):
            ins, zones, sems, token = refs[:nt], refs[nt : nt + nz], refs[nt + nz : nt + nz + 2 * nt], refs[-1]
            x, y, c = _place()
            me = _slot(x, y, c)
            for k in range(1, N_DEV):
                px, py, pc = _flip(x, (k >> 2) & 1), _flip(y, (k >> 1) & 1), _flip(c, k & 1)
                for t, (_, whole, key, _, layer) in enumerate(items):
                    zone = zones[keys.index(key)]
                    pltpu.make_async_remote_copy(
                        src_ref=ins[t] if whole else ins[t].at[_slot(px, py, pc)],
                        dst_ref=zone.at[me] if layer is None else zone.at[me, layer],
                        send_sem=sems[2 * t], recv_sem=sems[2 * t + 1], device_id=(px, py, pc), device_id_type=MESH).start()
            token[...] = jnp.zeros_like(token)

        bufs = [pltpu.with_memory_space_constraint(b, pltpu.HBM) for b in [it[0] for it in items] + [self.zones[k] for k in keys]]
        outs = pl.pallas_call(
            body,
            name=name,
            in_specs=[_HBM] * (nt + nz),
            out_specs=[_SEM] * (2 * nt) + [_HBM] * (nt + nz) + [pl.BlockSpec(memory_space=pltpu.VMEM)],
            out_shape=[pltpu.SemaphoreType.DMA(())] * (2 * nt) + [pltpu.HBM(b.shape, b.dtype) for b in bufs]
            + [jax.ShapeDtypeStruct((8, LANES), F32)],
            input_output_aliases={i: 2 * nt + i for i in range(nt + nz)},
            compiler_params=pltpu.CompilerParams(has_side_effects=_DATAFLOW),
        )(*bufs)
        for t, (_, _, key, _, layer) in enumerate(items):
            self.pending.append((outs[2 * t], outs[2 * t + 1], key, layer))
        self.sources += [((it[2], it[4]), a) for it, a in zip(items, outs[2 * nt : 3 * nt])]
        for i, key in enumerate(keys):
            self.zones[key] = outs[3 * nt + i]
        return outs[-1]

    def wait(self, after, *, name):
        pending, keys = self.pending, list(self.zones)
        names, sources = [n for n, _ in self.sources], [a for _, a in self.sources]
        ns, nz, npend = len(sources), len(keys), len(pending)
        self.pending, self.sources = [], []

        def body(*refs):
            zones, sems = refs[ns : ns + nz], refs[ns + nz : ns + nz + 2 * npend]
            x, y, c = _place()
            for i, (_, _, key, layer) in enumerate(pending):
                zone = zones[keys.index(key)]
                rows = pl.ds(0, N_DEV - 1)
                seven = zone.at[rows] if layer is None else zone.at[rows, layer]
                pltpu.make_async_remote_copy(
                    src_ref=seven, dst_ref=seven, send_sem=sems[2 * i], recv_sem=sems[2 * i + 1],
                    device_id=(x, y, c), device_id_type=MESH).wait()

        bufs = list(sources) + [self.zones[k] for k in keys]
        flat_sems = [s for p in pending for s in p[:2]]
        outs = pl.pallas_call(
            body,
            name=name,
            in_specs=[_HBM] * (ns + nz) + [_SEM] * (2 * npend) + _any_specs(len(after)),
            out_specs=[_HBM] * (ns + nz),
            out_shape=[pltpu.HBM(b.shape, b.dtype) for b in bufs],
            input_output_aliases={i: i for i in range(ns + nz)},
            compiler_params=pltpu.CompilerParams(has_side_effects=_DATAFLOW),
        )(*bufs, *flat_sems, *after)
        self.zones = {}
        return dict(zip(keys, outs[ns:])), dict(zip(names, outs[:ns]))


def _call(body, *, name, grid, in_specs, out_specs, out_shape, args, scratch=(), plan=None, after=None):
    if after is not None:
        inner, n_real = body, len(in_specs)
        body = lambda *refs: inner(*refs[:n_real], *refs[n_real + 1 :])
        in_specs, args = list(in_specs) + _any_specs(1), list(args) + [after]
    n_in, n_out, n_scr = len(in_specs), len(out_specs), len(scratch)
    if plan is None:
        outs = pl.pallas_call(
            body, name=name, grid=grid, in_specs=in_specs, out_specs=out_specs, out_shape=out_shape,
            scratch_shapes=list(scratch), compiler_params=_params(len(grid)))(*args)
        return outs, None
    c_in, c_out = len(plan.inputs), len(plan.out_shapes)
    phases = plan.phases(math.prod(grid))

    def full(*refs):
        a, refs = refs[:n_in], refs[n_in:]
        ci, refs = refs[:c_in], refs[c_in:]
        o, refs = refs[:n_out], refs[n_out:]
        co, refs = refs[:c_out], refs[c_out:]
        s, cs = refs[:n_scr], refs[n_scr:]
        step = pl.program_id(0)
        for d in range(1, len(grid)):
            step = step * grid[d] + pl.program_id(d)
        for at, phase in phases:
            if at == 0:
                pl.when(step == 0)(functools.partial(phase, ci, co, cs))
        body(*a, *o, *s)
        for at, phase in phases:
            if at > 0:
                pl.when(step == at)(functools.partial(phase, ci, co, cs))

    outs = pl.pallas_call(
        full,
        name=name,
        grid=grid,
        in_specs=list(in_specs) + _any_specs(c_in),
        out_specs=list(out_specs) + _any_specs(c_out),
        out_shape=list(out_shape) + plan.out_shapes,
        scratch_shapes=list(scratch) + plan.scratch,
        input_output_aliases={n_in + i: n_out + t for i, t in plan.aliases.items()},
        compiler_params=_params(len(grid)),
    )(*args, *plan.inputs)
    return outs[:n_out], outs[n_out:]


def _row_tile(R):
    return 256 if R % 256 == 0 else R


def cols_from_shards(a):
    n, R, C = a.shape
    tr = _row_tile(R)

    def body(i_ref, o_ref):
        for s in range(n):
            o_ref[:, s * C : (s + 1) * C] = i_ref[s]

    return pl.pallas_call(
        body,
        name="cols_from_shards",
        grid=(R // tr,),
        in_specs=[pl.BlockSpec((n, tr, C), lambda i: (0, i, 0))],
        out_specs=pl.BlockSpec((tr, n * C), lambda i: (i, 0)),
        out_shape=jax.ShapeDtypeStruct((R, n * C), a.dtype),
        compiler_params=_params(1),
    )(a)


def shards_from_cols(a):
    R, W = a.shape
    C = W // N_DEV
    tr = _row_tile(R)

    def body(i_ref, o_ref):
        for s in range(N_DEV):
            o_ref[s] = i_ref[:, s * C : (s + 1) * C]

    return pl.pallas_call(
        body,
        name="shards_from_cols",
        grid=(R // tr,),
        in_specs=[pl.BlockSpec((tr, W), lambda i: (i, 0))],
        out_specs=pl.BlockSpec((N_DEV, tr, C), lambda i: (0, i, 0)),
        out_shape=jax.ShapeDtypeStruct((N_DEV, R, C), a.dtype),
        compiler_params=_params(1),
    )(a)


def _shift_down(u, prev8, row, n):
    out = pltpu.roll(u, n, 0)
    for k in range(n):
        out = jnp.where(row == k, prev8[8 - n + k : 8 - n + k + 1, :], out)
    return out


def _shift_up(u, next8, row, n, tm):
    out = pltpu.roll(u, tm - n, 0)
    for k in range(n):
        out = jnp.where(row == tm - n + k, next8[k : k + 1, :], out)
    return out


def conv_fwd(x, gain, w_in, cw, w_out, *, seq, tm, plan=None):
    T, D = x.shape
    tps = seq // tm

    def body(x_ref, g_ref, win_ref, cw_ref, wout_ref, x1_ref, bcx_ref, y_ref, z_ref, carry_ref):
        i = pl.program_id(0)

        @pl.when(i % tps == 0)
        def _():
            carry_ref[...] = jnp.zeros_like(carry_ref)

        xt = x_ref[...]
        h = ((xt * _rms(xt)) * g_ref[...]).astype(BF16)
        bcx = _dot(h, win_ref[...])
        bcx_ref[...] = bcx.astype(BF16)
        b, c, xv = bcx[:, :D], bcx[:, D : 2 * D], bcx[:, 2 * D :]
        u = b * xv
        row = lax.broadcasted_iota(jnp.int32, u.shape, 0)
        prev = carry_ref[...]
        u1 = _shift_down(u, prev, row, 1)
        u2 = _shift_down(u, prev, row, 2)
        carry_ref[...] = u[tm - 8 :, :]
        cwv = cw_ref[...]
        y = cwv[0:1, :] * u2 + cwv[1:2, :] * u1 + cwv[2:3, :] * u
        y_ref[...] = y
        z = (c * y).astype(BF16)
        z_ref[...] = z
        x1_ref[...] = xt + _dot(z, wout_ref[...])

    tile = pl.BlockSpec((tm, D), lambda i: (i, 0))
    return _call(
        body,
        plan=plan,
        args=(x, gain, w_in, cw, w_out),
        name="conv_fwd",
        grid=(T // tm,),
        in_specs=[
            pl.BlockSpec((tm, D), lambda i: (i, 0)),
            _resident((1, D)),
            _resident((D, 3 * D)),
            _resident((3, D)),
            _resident((D, D)),
        ],
        out_specs=[tile, pl.BlockSpec((tm, 3 * D), lambda i: (i, 0)), tile, tile],
        out_shape=[jax.ShapeDtypeStruct((T, D), F32), jax.ShapeDtypeStruct((T, 3 * D), BF16),
                   jax.ShapeDtypeStruct((T, D), F32), jax.ShapeDtypeStruct((T, D), BF16)],
        scratch=[pltpu.VMEM((8, D), F32)],
    )


def conv_bwd(dx1, x, gain, bcx, y, cw, w_in, w_out, *, seq, tm, after=None):
    T, D = x.shape
    n = T // tm
    tps = seq // tm

    def body(d_ref, x_ref, g_ref, bcx_ref, y_ref, cw_ref, win_ref, wout_ref,
             gx_ref, dbcx_ref, h_ref, d16_ref, dcw_ref, dg_ref, carry_ref):
        i = pl.program_id(0)
        t = n - 1 - i

        @pl.when(i == 0)
        def _():
            dcw_ref[...] = jnp.zeros_like(dcw_ref)
            dg_ref[...] = jnp.zeros_like(dg_ref)

        @pl.when(t % tps == tps - 1)
        def _():
            carry_ref[...] = jnp.zeros_like(carry_ref)

        d = d_ref[...]
        d16 = d.astype(BF16)
        d16_ref[...] = d16
        dz = _dot_nt(d16, wout_ref[...])
        bcx = bcx_ref[...].astype(F32)
        b, c, xv = bcx[:, :D], bcx[:, D : 2 * D], bcx[:, 2 * D :]
        u = b * xv
        row = lax.broadcasted_iota(jnp.int32, u.shape, 0)
        cwv = cw_ref[...]
        dc = dz * y_ref[...]
        dy = dz * c
        nxt = carry_ref[...]
        dy1 = _shift_up(dy, nxt, row, 1, tm)
        dy2 = _shift_up(dy, nxt, row, 2, tm)
        carry_ref[...] = dy[0:8, :]
        dcw_ref[0:1, :] += jnp.sum(dy2 * u, axis=0, keepdims=True)
        dcw_ref[1:2, :] += jnp.sum(dy1 * u, axis=0, keepdims=True)
        dcw_ref[2:3, :] += jnp.sum(dy * u, axis=0, keepdims=True)
        du = cwv[2:3, :] * dy + cwv[1:2, :] * dy1 + cwv[0:1, :] * dy2
        dbcx_ref[:, :D] = (du * xv).astype(BF16)
        dbcx_ref[:, D : 2 * D] = dc.astype(BF16)
        dbcx_ref[:, 2 * D :] = (du * b).astype(BF16)
        dh = _dot_nt(dbcx_ref[...], win_ref[...])
        xt = x_ref[...]
        r = _rms(xt)
        gn = g_ref[...]
        h_ref[...] = ((xt * r) * gn).astype(BF16)
        dx, dgn = _rms_bwd(xt, r, gn, dh)
        dg_ref[0:1, :] += dgn
        gx_ref[...] = d + dx

    rev = lambda i: (n - 1 - i, 0)
    return _call(
        body,
        after=after,
        args=(dx1, x, gain, bcx, y, cw, w_in, w_out),
        name="conv_bwd",
        grid=(n,),
        in_specs=[
            pl.BlockSpec((tm, D), rev),
            pl.BlockSpec((tm, D), rev),
            _resident((1, D)),
            pl.BlockSpec((tm, 3 * D), rev),
            pl.BlockSpec((tm, D), rev),
            _resident((3, D)),
            _resident((D, 3 * D)),
            _resident((D, D)),
        ],
        out_specs=[
            pl.BlockSpec((tm, D), rev),
            pl.BlockSpec((tm, 3 * D), rev),
            pl.BlockSpec((tm, D), rev),
            pl.BlockSpec((tm, D), rev),
            pl.BlockSpec((8, D), lambda i: (0, 0)),
            pl.BlockSpec((8, D), lambda i: (0, 0)),
        ],
        out_shape=[
            jax.ShapeDtypeStruct((T, D), F32),
            jax.ShapeDtypeStruct((T, 3 * D), BF16),
            jax.ShapeDtypeStruct((T, D), BF16),
            jax.ShapeDtypeStruct((T, D), BF16),
            jax.ShapeDtypeStruct((8, D), F32),
            jax.ShapeDtypeStruct((8, D), F32),
        ],
        scratch=[pltpu.VMEM((8, D), F32)],
    )


MXU_TILE = 256
FFN_CHUNK = 4 * MXU_TILE


def _sigmoid(g):
    return 1.0 / (1.0 + jnp.exp(-g))


def _ffn_chunks(F):
    assert F % MXU_TILE == 0
    return [(s, min(FFN_CHUNK, F - s)) for s in range(0, F, FFN_CHUNK)]


def ffn_fwd(x, gain, w_gu, w_d, *, tm, plan=None, attn=None, target=None, staged=False):
    T, D = x.shape
    F = w_d.shape[0]
    row = lambda i: (i, 0)
    tile = pl.BlockSpec((tm, D), row)
    chunks = _ffn_chunks(F)

    def body(*refs):
        refs = list(refs)
        x_ref, g_ref, wgu_ref, wd_ref = refs[:4]
        del refs[:4]
        if attn is not None:
            ao_ref, wo_ref = refs[:2]
            del refs[:2]
        if target is not None:
            t_ref = refs.pop(0)
        if attn is not None:
            xin_ref = refs.pop(0)
        xo_ref, gu_ref = refs[:2]
        buf_ref, sem_ref = refs[-2:] if staged else (None, None)
        i = pl.program_id(0)
        rows = pl.ds(pl.multiple_of(i * tm, tm), tm)

        def to_hbm(slot, col, n):
            return pltpu.make_async_copy(buf_ref.at[slot, :, pl.ds(0, n)], gu_ref.at[rows, pl.ds(col, n)], sem_ref.at[slot])

        if staged:

            @pl.when(i == 0)
            def _():
                buf_ref[...] = jnp.zeros_like(buf_ref)
                for c, (s, n) in enumerate(chunks):
                    to_hbm(2 * c, s, n).start()
                    to_hbm(2 * c + 1, F + s, n).start()

        xt = x_ref[...]
        if attn is not None:
            xt = xt + _dot(ao_ref[...], wo_ref[...])
            xin_ref[...] = xt
        h = ((xt * _rms(xt)) * g_ref[...]).astype(BF16)
        acc = xt
        for c, (s, n) in enumerate(chunks):
            g = _dot(h, wgu_ref[:, s : s + n])
            u = _dot(h, wgu_ref[:, F + s : F + s + n])
            if staged:
                copies = (to_hbm(2 * c, s, n), to_hbm(2 * c + 1, F + s, n))
                for cp in copies:
                    cp.wait()
                buf_ref[2 * c, :, :n] = g
                buf_ref[2 * c + 1, :, :n] = u
                for cp in copies:
                    cp.start()
            else:
                gu_ref[:, s : s + n] = g
                gu_ref[:, F + s : F + s + n] = u
            a = ((g * _sigmoid(g)) * u).astype(BF16)
            acc = acc + _dot(a, wd_ref[s : s + n, :])

        if staged:

            @pl.when(i == T // tm - 1)
            def _():
                for c, (s, n) in enumerate(chunks):
                    to_hbm(2 * c, s, n).wait()
                    to_hbm(2 * c + 1, F + s, n).wait()

        if target is None:
            xo_ref[...] = acc
        else:
            s_ref = refs[2]

            @pl.when(pl.program_id(0) == 0)
            def _():
                s_ref[...] = jnp.zeros_like(s_ref)

            e = acc - t_ref[...]
            xo_ref[...] = e * (1.0 / D)
            s_ref[...] += jnp.sum(jnp.sum(e * e, axis=-1, keepdims=True), axis=0, keepdims=True)

    args = [x, gain, w_gu, w_d]
    in_specs = [tile, _resident((1, D)), _resident((D, 2 * F)), _resident((F, D))]
    out_specs = [tile, pl.BlockSpec(memory_space=pl.ANY) if staged else pl.BlockSpec((tm, 2 * F), row)]
    out_shape = [jax.ShapeDtypeStruct((T, D), F32), jax.ShapeDtypeStruct((T, 2 * F), F32)]
    scratch = []
    if staged:
        scratch = [pltpu.VMEM((2 * len(chunks), tm, FFN_CHUNK), F32), pltpu.SemaphoreType.DMA((2 * len(chunks),))]
    if attn is not None:
        args += list(attn)
        in_specs += [pl.BlockSpec((tm, attn[0].shape[1]), row), _resident(attn[1].shape)]
        out_specs.insert(0, tile)
        out_shape.insert(0, jax.ShapeDtypeStruct((T, D), F32))
    if target is not None:
        args.append(target)
        in_specs.append(tile)
        out_specs.append(pl.BlockSpec((8, LANES), lambda i: (0, 0)))
        out_shape.append(jax.ShapeDtypeStruct((8, LANES), F32))
    return _call(body, plan=plan, args=args, name="ffn_fwd", grid=(T // tm,), in_specs=in_specs, out_specs=out_specs,
                 out_shape=out_shape, scratch=scratch)


def ffn_bwd(dxo, x, gain, gu, w_gu, w_d, *, tm, after=None, w_o=None):
    T, D = x.shape
    F = w_d.shape[0]

    def body(d_ref, x_ref, g_ref, gu_ref, wgu_ref, wd_ref, *rest):
        if w_o is not None:
            wo_ref, rest = rest[0], rest[1:]
        dx_ref, a_ref, dgu_ref, h_ref, d16_ref, dg_ref = rest[:6]

        @pl.when(pl.program_id(0) == 0)
        def _():
            dg_ref[...] = jnp.zeros_like(dg_ref)

        d = d_ref[...]
        d16 = d.astype(BF16)
        d16_ref[...] = d16
        dh = jnp.zeros((tm, D), F32)
        for c0, n in _ffn_chunks(F):
            g = gu_ref[:, c0 : c0 + n]
            u = gu_ref[:, F + c0 : F + c0 + n]
            da = _dot_nt(d16, wd_ref[c0 : c0 + n, :])
            s = _sigmoid(g)
            sg = g * s
            a_ref[:, c0 : c0 + n] = (sg * u).astype(BF16)
            dg16 = (da * u * (s + sg * (1.0 - s))).astype(BF16)
            du16 = (da * sg).astype(BF16)
            dgu_ref[:, c0 : c0 + n] = dg16
            dgu_ref[:, F + c0 : F + c0 + n] = du16
            dh = dh + _dot_nt(dg16, wgu_ref[:, c0 : c0 + n]) + _dot_nt(du16, wgu_ref[:, F + c0 : F + c0 + n])
        xt = x_ref[...]
        r = _rms(xt)
        gn = g_ref[...]
        h_ref[...] = ((xt * r) * gn).astype(BF16)
        dx, dgn = _rms_bwd(xt, r, gn, dh)
        dg_ref[0:1, :] += dgn
        dxi = d + dx
        dx_ref[...] = dxi
        if w_o is not None:
            dxi16_ref, dao_ref = rest[6:8]
            dxi16 = dxi.astype(BF16)
            dxi16_ref[...] = dxi16
            dao_ref[...] = _dot_nt(dxi16, wo_ref[...]).astype(BF16)

    tile = pl.BlockSpec((tm, D), lambda i: (i, 0))
    args = [dxo, x, gain, gu, w_gu, w_d]
    wide = lambda n: pl.BlockSpec((tm, n), lambda i: (i, 0))
    in_specs = [tile, tile, _resident((1, D)), wide(2 * F), _resident((D, 2 * F)), _resident((F, D))]
    out_specs = [tile, wide(F), wide(2 * F), tile, tile, pl.BlockSpec((8, D), lambda i: (0, 0))]
    out_shape = [
        jax.ShapeDtypeStruct((T, D), F32),
        jax.ShapeDtypeStruct((T, F), BF16),
        jax.ShapeDtypeStruct((T, 2 * F), BF16),
        jax.ShapeDtypeStruct((T, D), BF16),
        jax.ShapeDtypeStruct((T, D), BF16),
        jax.ShapeDtypeStruct((8, D), F32),
    ]
    if w_o is not None:
        args.append(w_o)
        in_specs.append(_resident(w_o.shape))
        out_specs += [tile, pl.BlockSpec((tm, w_o.shape[0]), lambda i: (i, 0))]
        out_shape += [jax.ShapeDtypeStruct((T, D), BF16), jax.ShapeDtypeStruct((T, w_o.shape[0]), BF16)]
    return _call(body, after=after, args=args, name="ffn_bwd", grid=(T // tm,), in_specs=in_specs, out_specs=out_specs,
                 out_shape=out_shape)


def wgrad(a, b, *, name, a_cols=0, b_cols=0, group=1, flat=False, tk, out_dtype=BF16, after=None):
    T, K = a.shape
    J = 1
    if a_cols:
        K = a_cols
        J = a.shape[1] // K
        a_spec = pl.BlockSpec((tk, K), lambda j, k: (k, j))
    else:
        a_spec = pl.BlockSpec((tk, K), lambda j, k: (k, 0))
    if b_cols:
        N = b_cols * group
        J = b.shape[1] // N
        b_spec = pl.BlockSpec((tk, N), lambda j, k: (k, j))
    else:
        N = b.shape[1]
        b_spec = pl.BlockSpec((tk, N), lambda j, k: (k, 0))
    nk = T // tk
    if flat:
        o_spec, o_shape = pl.BlockSpec((K, N), lambda j, k: (0, j)), (K, J * N)
    elif group > 1:
        o_spec, o_shape = pl.BlockSpec((group, K, b_cols), lambda j, k: (j, 0, 0)), (J * group, K, b_cols)
    else:
        o_spec, o_shape = pl.BlockSpec((None, K, N), lambda j, k: (j, 0, 0)), (J, K, N)

    def body(a_ref, b_ref, o_ref, acc_ref):
        k = pl.program_id(1)

        @pl.when(k == 0)
        def _():
            acc_ref[...] = jnp.zeros_like(acc_ref)

        acc_ref[...] += _dot_tn(a_ref[...], b_ref[...])

        @pl.when(k == nk - 1)
        def _():
            if group > 1 and not flat:
                for i in range(group):
                    o_ref[i] = acc_ref[:, i * b_cols : (i + 1) * b_cols].astype(out_dtype)
            else:
                o_ref[...] = acc_ref[...].astype(out_dtype)

    outs, _ = _call(
        body,
        after=after,
        name=name,
        grid=(J, nk),
        in_specs=[a_spec, b_spec],
        out_specs=[o_spec],
        out_shape=[jax.ShapeDtypeStruct(o_shape, out_dtype)],
        args=(a, b),
        scratch=[pltpu.VMEM((K, N), F32)],
    )
    return outs[0]


def _seg(xs, lo):
    s_lo = [jnp.sum(jnp.where(lo, x, 0.0), axis=-1, keepdims=True) for x in xs]
    s_hi = [jnp.sum(jnp.where(lo, 0.0, x), axis=-1, keepdims=True) for x in xs]
    return [jnp.where(lo, a, b) for a, b in zip(s_lo, s_hi)]


def _head_norm(xs, gains, lo):
    rs = [lax.rsqrt(s * (1.0 / HEAD_DIM) + EPS) for s in _seg([x * x for x in xs], lo)]
    return [(x * r) * g for x, r, g in zip(xs, rs, gains)], rs


def _head_norm_bwd(xs, rs, gains, dys, lo):
    xns = [x * r for x, r in zip(xs, rs)]
    dxns = [dy * g for dy, g in zip(dys, gains)]
    means = [s * (1.0 / HEAD_DIM) for s in _seg([a * b for a, b in zip(dxns, xns)], lo)]
    dxs = [r * (dxn - xn * m) for r, dxn, xn, m in zip(rs, dxns, xns, means)]
    return dxs, [jnp.sum(dy * xn, axis=0, keepdims=True) for dy, xn in zip(dys, xns)]


def _swap_halves(x):
    return pltpu.roll(x, HEAD_DIM, 1)


def qkv_proj(x, gain, w, qg, kg, *, tm):
    T, D = x.shape
    N = w.shape[1]
    kvw = N_KV_HEADS * HEAD_DIM
    nqt, nkt = D // LANES, kvw // LANES

    def body(x_ref, g_ref, w_ref, qg_ref, kg_ref, qkv_ref, q_ref, kd_ref, vd_ref):
        xt = x_ref[...]
        h = ((xt * _rms(xt)) * g_ref[...]).astype(BF16)
        qkv = _dot(h, w_ref[...])
        qkv_ref[...] = qkv
        lo = lax.broadcasted_iota(jnp.int32, (1, LANES), 1) < HEAD_DIM
        tiles = [qkv[:, t * LANES : (t + 1) * LANES] for t in range(nqt + nkt)]
        normed, _ = _head_norm(tiles, [qg_ref[...]] * nqt + [kg_ref[...]] * nkt, lo)
        for t in range(nqt):
            q_ref[:, t * LANES : (t + 1) * LANES] = (normed[t] * SCALE).astype(BF16)
        for t in range(nkt):
            kn = normed[nqt + t]
            v = qkv[:, D + kvw + t * LANES : D + kvw + (t + 1) * LANES]
            for src, dst in ((kn, kd_ref), (v, vd_ref)):
                sw = _swap_halves(src)
                dst[:, 2 * t * LANES : (2 * t + 1) * LANES] = jnp.where(lo, src, sw).astype(BF16)
                dst[:, (2 * t + 1) * LANES : (2 * t + 2) * LANES] = jnp.where(lo, sw, src).astype(BF16)

    row = lambda i: (i, 0)
    return pl.pallas_call(
        body,
        name="qkv_proj",
        grid=(T // tm,),
        in_specs=[pl.BlockSpec((tm, D), row), _resident((1, D)), _resident((D, N)), _resident((1, LANES)), _resident((1, LANES))],
        out_specs=[pl.BlockSpec((tm, N), row), pl.BlockSpec((tm, D), row), pl.BlockSpec((tm, 2 * kvw), row), pl.BlockSpec((tm, 2 * kvw), row)],
        out_shape=[
            jax.ShapeDtypeStruct((T, N), F32),
            jax.ShapeDtypeStruct((T, D), BF16),
            jax.ShapeDtypeStruct((T, 2 * kvw), BF16),
            jax.ShapeDtypeStruct((T, 2 * kvw), BF16),
        ],
        compiler_params=_params(1),
    )(x, gain, w, qg, kg)


def _attn_tables(sinks, n_q_heads):
    P = n_q_heads // N_KV_HEADS // 2
    h = jnp.arange(1, n_q_heads + 1, dtype=F32)
    slopes = jnp.exp2(-8.0 * h / n_q_heads).reshape(N_KV_HEADS, P, 1, 2, 1)
    qi = jnp.arange(BLOCK)[:, None]
    kj = jnp.arange(BLOCK)[None, :]
    dist = jnp.where(kj <= qi, qi - kj, qi + BLOCK - kj).astype(F32)
    shape = (N_KV_HEADS, P, BLOCK, 2, BLOCK)
    bias = jnp.broadcast_to(-slopes * dist[None, None, :, None, :], shape)
    sink = jnp.broadcast_to(sinks.astype(F32).reshape(N_KV_HEADS, P, 1, 2, 1), shape)
    return bias.reshape(N_KV_HEADS, P * BLOCK, 2 * BLOCK), sink.reshape(N_KV_HEADS, P * BLOCK, 2 * BLOCK)


def _attn_specs(D, nb):
    kvw2 = 2 * N_KV_HEADS * HEAD_DIM
    cur = lambda b, i: (b * nb + i, 0)
    prev = lambda b, i: (jnp.maximum(b * nb + i - 1, 0), 0)
    return [
        pl.BlockSpec((BLOCK, D), cur),
        pl.BlockSpec((BLOCK, kvw2), cur),
        pl.BlockSpec((BLOCK, kvw2), prev),
        pl.BlockSpec((BLOCK, kvw2), cur),
        pl.BlockSpec((BLOCK, kvw2), prev),
    ]


def _attn_operands(kh, P, lo, q_ref, kc_ref, kp_ref, vc_ref, vp_ref):
    sl = slice(kh * LANES, (kh + 1) * LANES)

    def cat(prev_ref, cur_ref):
        d = jnp.concatenate([prev_ref[:, sl], cur_ref[:, sl]], axis=0)
        z = jnp.zeros_like(d)
        return jnp.concatenate([jnp.where(lo, d, z), jnp.where(lo, z, d)], axis=0)

    qt = jnp.concatenate([q_ref[:, (kh * P + pr) * LANES : (kh * P + pr + 1) * LANES] for pr in range(P)], axis=0)
    return qt, cat(kp_ref, kc_ref), cat(vp_ref, vc_ref)


def _attn_exp(s_all, bias, sink, tri, first):
    out = []
    for par in range(2):
        c0 = 2 * par * BLOCK
        s = jnp.where(tri, s_all[:, c0 + BLOCK : c0 + 2 * BLOCK], jnp.where(first, NEG, s_all[:, c0 : c0 + BLOCK]))
        s = s + bias[:, par * BLOCK : (par + 1) * BLOCK]
        snk = sink[:, par * BLOCK : (par + 1) * BLOCK]
        m = jnp.maximum(jnp.max(s, axis=-1, keepdims=True), snk)
        out.append((jnp.exp(s - m), jnp.exp(snk - m)))
    return out


def _unfold(x, tri):
    z = jnp.zeros_like(x)
    return jnp.concatenate([jnp.where(tri, z, x), jnp.where(tri, x, z)], axis=1)


def _attn_masks(R):
    lane = lax.broadcasted_iota(jnp.int32, (1, LANES), 1)
    row = lax.broadcasted_iota(jnp.int32, (R, BLOCK), 0) & (BLOCK - 1)
    col = lax.broadcasted_iota(jnp.int32, (R, BLOCK), 1)
    return lane, lane < HEAD_DIM, col <= row


def attn_fwd(q16, kd, vd, bias, sink, *, seq, n_seq):
    T, D = q16.shape
    nb = seq // BLOCK
    P = D // HEAD_DIM // N_KV_HEADS // 2
    R = P * BLOCK
    KV = range(N_KV_HEADS)

    def body(q_ref, kc_ref, kp_ref, vc_ref, vp_ref, bias_ref, sink_ref, o_ref):
        first = pl.program_id(1) == 0
        _, lo, tri = _attn_masks(R)
        ops = [_attn_operands(kh, P, lo, q_ref, kc_ref, kp_ref, vc_ref, vp_ref) for kh in KV]
        s_all = [_dot_nt(ops[kh][0], ops[kh][1]) for kh in KV]
        ex = [_attn_exp(s_all[kh], bias_ref[kh], sink_ref[kh], tri, first) for kh in KV]
        den = [[jnp.sum(e, axis=-1, keepdims=True) + es for e, es in ex[kh]] for kh in KV]
        lhs = [jnp.concatenate([_unfold(e, tri) for e, _ in ex[kh]], axis=1).astype(BF16) for kh in KV]
        o = [_dot(lhs[kh], ops[kh][2]) for kh in KV]
        for kh in KV:
            out = o[kh] / jnp.where(lo, den[kh][0], den[kh][1])
            for pr in range(P):
                t = kh * P + pr
                o_ref[:, t * LANES : (t + 1) * LANES] = out[pr * BLOCK : (pr + 1) * BLOCK, :].astype(BF16)

    return pl.pallas_call(
        body,
        name="attn_fwd",
        grid=(n_seq, nb),
        in_specs=_attn_specs(D, nb) + [_resident((N_KV_HEADS, R, 2 * BLOCK)), _resident((N_KV_HEADS, R, 2 * BLOCK))],
        out_specs=pl.BlockSpec((BLOCK, D), lambda b, i: (b * nb + i, 0)),
        out_shape=jax.ShapeDtypeStruct((T, D), BF16),
        compiler_params=_params(2),
    )(q16, kd, kd, vd, vd, bias, sink)


def attn_bwd(q16, kd, vd, do, bias, sink, *, seq, n_seq):
    T, D = q16.shape
    kvw2 = 2 * N_KV_HEADS * HEAD_DIM
    nb = seq // BLOCK
    G = D // HEAD_DIM // N_KV_HEADS
    P = G // 2
    R = P * BLOCK
    KV = range(N_KV_HEADS)

    def body(q_ref, kc_ref, kp_ref, vc_ref, vp_ref, do_ref, bias_ref, sink_ref,
             dq_ref, dkc_ref, dkp_ref, dvc_ref, dvp_ref, dsink_ref):
        first = pl.program_id(1) == 0

        @pl.when(jnp.logical_and(pl.program_id(0) == 0, first))
        def _():
            dsink_ref[...] = jnp.zeros_like(dsink_ref)

        lane, lo, tri = _attn_masks(R)
        ops = [_attn_operands(kh, P, lo, q_ref, kc_ref, kp_ref, vc_ref, vp_ref) for kh in KV]
        do16 = [jnp.concatenate([do_ref[:, (kh * P + pr) * LANES : (kh * P + pr + 1) * LANES] for pr in range(P)], axis=0)
                for kh in KV]
        s_all = [_dot_nt(ops[kh][0], ops[kh][1]) for kh in KV]
        dp_all = [_dot_nt(do16[kh], ops[kh][2]) for kh in KV]
        ex = [_attn_exp(s_all[kh], bias_ref[kh], sink_ref[kh], tri, first) for kh in KV]
        den = [[jnp.sum(e, axis=-1, keepdims=True) for e, _ in ex[kh]] for kh in KV]
        dsink = jnp.zeros((1, LANES), F32)
        pf, dsf = [], []
        for kh in KV:
            ps_, ds_ = [], []
            for par in range(2):
                e, es = ex[kh][par]
                inv = 1.0 / (den[kh][par] + es)
                p = e * inv
                c0 = 2 * par * BLOCK
                dp = jnp.where(tri, dp_all[kh][:, c0 + BLOCK : c0 + 2 * BLOCK], dp_all[kh][:, c0 : c0 + BLOCK])
                delta = jnp.sum(p * dp, axis=-1, keepdims=True)
                ds_.append(_unfold(p * (dp - delta), tri))
                ps_.append(_unfold(p, tri))
                dsr = -((es * inv) * delta)
                for pr in range(P):
                    hq = kh * G + 2 * pr + par
                    tot = jnp.sum(dsr[pr * BLOCK : (pr + 1) * BLOCK, :], axis=0, keepdims=True)
                    dsink = dsink + jnp.where(lane == hq, tot, 0.0)
            pf.append(jnp.concatenate(ps_, axis=1).astype(BF16))
            dsf.append(jnp.concatenate(ds_, axis=1).astype(BF16))
        dq = [_dot(dsf[kh], ops[kh][1]) for kh in KV]
        dk = [_dot_tn(dsf[kh], ops[kh][0]) for kh in KV]
        dv = [_dot_tn(pf[kh], do16[kh]) for kh in KV]
        dsink_ref[0:1, :] += dsink
        for kh in KV:
            sl = slice(kh * LANES, (kh + 1) * LANES)
            for pr in range(P):
                t = kh * P + pr
                dq_ref[:, t * LANES : (t + 1) * LANES] = dq[kh][pr * BLOCK : (pr + 1) * BLOCK, :]
            for full, prev_ref, cur_ref in ((dk[kh], dkp_ref, dkc_ref), (dv[kh], dvp_ref, dvc_ref)):
                dup = jnp.where(lo, full[: 2 * BLOCK, :], full[2 * BLOCK :, :])
                prev_ref[:, sl] = dup[:BLOCK, :].astype(BF16)
                cur_ref[:, sl] = dup[BLOCK:, :].astype(BF16)

    cur = lambda b, i: (b * nb + i, 0)
    kv_spec = pl.BlockSpec((BLOCK, kvw2), cur)
    kv_shape = jax.ShapeDtypeStruct((T, kvw2), BF16)
    return pl.pallas_call(
        body,
        name="attn_bwd",
        grid=(n_seq, nb),
        in_specs=_attn_specs(D, nb)
        + [pl.BlockSpec((BLOCK, D), cur), _resident((N_KV_HEADS, R, 2 * BLOCK)), _resident((N_KV_HEADS, R, 2 * BLOCK))],
        out_specs=[pl.BlockSpec((BLOCK, D), cur), kv_spec, kv_spec, kv_spec, kv_spec, pl.BlockSpec((8, LANES), lambda b, i: (0, 0))],
        out_shape=[jax.ShapeDtypeStruct((T, D), F32), kv_shape, kv_shape, kv_shape, kv_shape, jax.ShapeDtypeStruct((8, LANES), F32)],
        compiler_params=_params(2),
    )(q16, kd, kd, vd, vd, do, bias, sink)


def qkv_bwd(dq, dkc, dkp, dvc, dvp, qkv, dres, x, gain, w_qkv, qg, kg, *, seq, tm):
    T, D = x.shape
    kvw2 = dkc.shape[1]
    kvw = kvw2 // 2
    nqt, nkt = D // LANES, kvw // LANES
    nb = seq // BLOCK
    nbt = tm // BLOCK
    assert nb % nbt == 0
    n = T // tm

    def body(dq_ref, dkc_ref, dkpa_ref, dkpb_ref, dvc_ref, dvpa_ref, dvpb_ref, qkv_ref, dres_ref, x_ref, g_ref, w_ref,
             qg_ref, kg_ref, dx_ref, dqkv_ref, h_ref, dg_ref, hg_ref):
        i = pl.program_id(0)

        @pl.when(i == 0)
        def _():
            dg_ref[...] = jnp.zeros_like(dg_ref)
            hg_ref[...] = jnp.zeros_like(hg_ref)

        lo = lax.broadcasted_iota(jnp.int32, (1, LANES), 1) < HEAD_DIM
        last = ((i + 1) * nbt) % nb == 0
        up = lambda ref: ref[...].astype(F32)

        def with_next(cur_ref, own_ref, next_ref):
            return up(cur_ref) + jnp.concatenate([up(own_ref)[BLOCK:, :], jnp.where(last, 0.0, up(next_ref))], axis=0)

        dkd = with_next(dkc_ref, dkpa_ref, dkpb_ref)
        dvd = with_next(dvc_ref, dvpa_ref, dvpb_ref)

        def undup(d, t):
            a, b = d[:, 2 * t * LANES : (2 * t + 1) * LANES], d[:, (2 * t + 1) * LANES : (2 * t + 2) * LANES]
            return jnp.where(lo, a + _swap_halves(a), b + _swap_halves(b))

        tiles = [qkv_ref[:, t * LANES : (t + 1) * LANES] for t in range(nqt + nkt)]
        gains = [qg_ref[...]] * nqt + [kg_ref[...]] * nkt
        dys = [dq_ref[:, t * LANES : (t + 1) * LANES] * SCALE for t in range(nqt)] + [undup(dkd, t) for t in range(nkt)]
        _, rs = _head_norm(tiles, gains, lo)
        dxs, dgs = _head_norm_bwd(tiles, rs, gains, dys, lo)
        for t in range(nqt + nkt):
            dqkv_ref[:, t * LANES : (t + 1) * LANES] = dxs[t].astype(BF16)
        for t in range(nkt):
            dqkv_ref[:, D + kvw + t * LANES : D + kvw + (t + 1) * LANES] = undup(dvd, t).astype(BF16)
        hg_ref[0:1, :] += functools.reduce(lambda a, b: a + b, dgs[:nqt])
        hg_ref[1:2, :] += functools.reduce(lambda a, b: a + b, dgs[nqt:])
        dh = _dot_nt(dqkv_ref[...], w_ref[...])
        xt = x_ref[...]
        r = _rms(xt)
        gn = g_ref[...]
        h_ref[...] = ((xt * r) * gn).astype(BF16)
        dx, dgn = _rms_bwd(xt, r, gn, dh)
        dg_ref[0:1, :] += dgn
        dx_ref[...] = dres_ref[...] + dx

    row = lambda i: (i, 0)
    nxt_a = pl.BlockSpec((tm, kvw2), row)
    nxt_b = pl.BlockSpec((BLOCK, kvw2), lambda i: (jnp.minimum((i + 1) * nbt, n * nbt - 1), 0))
    return pl.pallas_call(
        body,
        name="qkv_bwd",
        grid=(n,),
        in_specs=[
            pl.BlockSpec((tm, D), row),
            pl.BlockSpec((tm, kvw2), row),
            nxt_a,
            nxt_b,
            pl.BlockSpec((tm, kvw2), row),
            nxt_a,
            nxt_b,
            pl.BlockSpec((tm, D + kvw2), row),
            pl.BlockSpec((tm, D), row),
            pl.BlockSpec((tm, D), row),
            _resident((1, D)),
            _resident((D, D + kvw2)),
            _resident((1, LANES)),
            _resident((1, LANES)),
        ],
        out_specs=[
            pl.BlockSpec((tm, D), row),
            pl.BlockSpec((tm, D + kvw2), row),
            pl.BlockSpec((tm, D), row),
            pl.BlockSpec((8, D), lambda i: (0, 0)),
            pl.BlockSpec((8, LANES), lambda i: (0, 0)),
        ],
        out_shape=[
            jax.ShapeDtypeStruct((T, D), F32),
            jax.ShapeDtypeStruct((T, D + kvw2), BF16),
            jax.ShapeDtypeStruct((T, D), BF16),
            jax.ShapeDtypeStruct((8, D), F32),
            jax.ShapeDtypeStruct((8, LANES), F32),
        ],
        compiler_params=_params(1),
    )(dq, dkc, dkp, dkp, dvc, dvp, dvp, qkv, dres, x, gain, w_qkv, qg, kg)


def local_step(x, target, gains, w, *, seq, tm=256, tm_ffn=256, tm_conv=512, tk=2048, shards=None, ex=None):
    T, D = x.shape
    n_seq = T // seq
    nm, nf, qgain, kgain, sinks = gains
    H = D // HEAD_DIM
    tk, tk_long = min(tk, T), min(2 * tk, T)
    qg2, kg2 = jnp.tile(qgain, (1, 2)), jnp.tile(kgain, (1, 2))
    bias, sinkcol = _attn_tables(sinks, H)

    dist = shards is not None
    w = dict(w)

    plan = _Gather([shards["w_gu"][0], shards["w_d"][0]]) if dist else None
    (x1, bcx, y_conv, z16), got = conv_fwd(x, nm[0:1], w["w_in"], w["cw"], w["w_out"], seq=seq, tm=tm_conv, plan=plan)
    if dist:
        w["w_gu"], w["w_d"] = [cols_from_shards(got[0]), None], [got[1].reshape(-1, D), None]
    plan = _Gather([shards["w_qkv"], shards["w_o"], shards["w_gu"][1], shards["w_d"][1]]) if dist else None
    (x2, gu0), got = ffn_fwd(x1, nf[0:1], w["w_gu"][0], w["w_d"][0], tm=2 * tm_ffn, plan=plan)
    if dist:
        w["w_qkv"], w["w_o"] = cols_from_shards(got[0]), got[1].reshape(D, D)
        w["w_gu"][1], w["w_d"][1] = cols_from_shards(got[2]), got[3].reshape(-1, D)
    qkv, q16, kd, vd = qkv_proj(x2, nm[1:2], w["w_qkv"], qg2, kg2, tm=tm_conv)
    ao = attn_fwd(q16, kd, vd, bias, sinkcol, seq=seq, n_seq=n_seq)
    (x3, dx4, gu1, sse), _ = ffn_fwd(x2, nf[1:2], w["w_gu"][1], w["w_d"][1], tm=2 * tm_ffn, attn=(ao, w["w_o"]), target=target, staged=True)

    by_dest = lambda a: a.reshape(N_DEV, -1, a.shape[-1])
    gu_cols = 2 * MXU_TILE

    def send(name, *entries):
        if ex is None:
            return None
        items = [(a, False, key, (N_DEV,) + (() if layers is None else (layers,)) + a.shape[1:], layer)
                 for a, key, layer, layers in entries]
        return ex.start(items, name=name)

    (dx3, a16, dgu, h16, d16, dnf1, dx3_16, dao), _ = ffn_bwd(
        dx4, x3, nf[1:2], gu1, w["w_gu"][1], w["w_d"][1], tm=tm, w_o=w["w_o"])
    g_gu1 = shards_from_cols(wgrad(h16, dgu, name="wgrad_gu1", b_cols=gu_cols, flat=True, tk=T))
    g_d1 = by_dest(wgrad(a16, d16, name="wgrad_d1", a_cols=a16.shape[1] // 2, tk=tk))
    g_o = by_dest(wgrad(ao, dx3_16, name="wgrad_o", tk=tk))
    dq, dkc, dkp, dvc, dvp, dsinks = attn_bwd(q16, kd, vd, dao, bias, sinkcol, seq=seq, n_seq=n_seq)
    dx2, dqkv16, h16, dnm1, dgains = qkv_bwd(dq, dkc, dkp, dvc, dvp, qkv, dx3, x2, nm[1:2], w["w_qkv"], qg2, kg2, seq=seq, tm=tm_conv)
    g_qkv = shards_from_cols(wgrad(h16, dqkv16, name="wgrad_qkv", tk=tk)[0])
    tok = send("exchange_layer1", (g_gu1, "w_gu", 1, 2), (g_d1, "w_d", 1, 2), (g_o, "w_o", None, None),
               (g_qkv, "w_qkv", None, None))
    (dx1, a16, dgu, h16, d16, dnf0), _ = ffn_bwd(dx2, x1, nf[0:1], gu0, w["w_gu"][0], w["w_d"][0], tm=tm, after=tok)
    g_gu0 = shards_from_cols(wgrad(h16, dgu, name="wgrad_gu0", b_cols=gu_cols, flat=True, tk=T))
    tok = send("exchange_gu0", (g_gu0, "w_gu", 0, 2))
    g_d0 = by_dest(wgrad(a16, d16, name="wgrad_d0", a_cols=a16.shape[1] // 2, tk=tk, after=tok))
    tok = send("exchange_d0", (g_d0, "w_d", 0, 2))
    (gx, dbcx, h16, d16, dcw, dnm0), _ = conv_bwd(
        dx1, x, nm[0:1], bcx, y_conv, w["cw"], w["w_in"], w["w_out"], seq=seq, tm=tm_conv, after=tok)
    g_out = by_dest(wgrad(z16, d16, name="wgrad_out", tk=tk))
    g_cw = dcw[0:3].reshape(3, N_DEV, D // N_DEV).transpose(1, 0, 2)
    tok = send("exchange_out", (g_out, "w_out", None, None), (g_cw, "cw", None, None))
    g_in = wgrad(h16, dbcx, name="wgrad_in", b_cols=3 * D // N_DEV, group=2, tk=tk_long, after=tok)
    g = dict(w_in=g_in, cw=g_cw, w_out=g_out, w_o=g_o, w_qkv=g_qkv, w_gu=[g_gu0, g_gu1], w_d=[g_d0, g_d1])
    small = dict(nm0=dnm0, nm1=dnm1, nf0=dnf0, nf1=dnf1, gains=dgains, sinks=dsinks)
    return sse, gx, g, small


def _adamw_math(g, w, m, v):
    m = ADAM_B1 * m + (1.0 - ADAM_B1) * g
    v = ADAM_B2 * v + (1.0 - ADAM_B2) * (g * g)
    m_hat = m / (1.0 - ADAM_B1 ** ADAM_STEP)
    v_hat = v / (1.0 - ADAM_B2 ** ADAM_STEP)
    delta = -ADAM_LR * (m_hat / (jnp.sqrt(v_hat) + ADAM_EPS) + ADAM_WD * w)
    return delta, m, v


def adamw(parts, owns, w, m, v, *, name, after=None):
    n, LR, C = parts.shape
    L = len(owns)
    R = LR // L
    tr = R
    for cand in (256, 128, 88, 64, 32, 16, 8):
        if R > cand and R % cand == 0:
            tr = cand
            break
    per_layer = R // tr
    extra = [] if after is None else [after]

    def body(me_ref, p_ref, *rest):
        own_refs, (w_ref, m_ref, v_ref) = rest[:L], rest[L : L + 3]
        g_ref, d_ref, mo_ref, vo_ref = rest[L + 3 + len(extra) :]
        layer = pl.program_id(0) // per_layer
        mine = own_refs[0][...].astype(F32)
        for j in range(1, L):
            mine = jnp.where(layer == j, own_refs[j][...].astype(F32), mine)
        g = None
        for s in range(n):
            share = jnp.where(me_ref[0] == s, mine, p_ref[s].astype(F32))
            g = share if g is None else g + share
        g_ref[...] = g
        d_ref[...], mo_ref[...], vo_ref[...] = _adamw_math(g, w_ref[...], m_ref[...], v_ref[...])

    blk = pl.BlockSpec((tr, C), lambda i, me: (i, 0))
    own_specs = [pl.BlockSpec((None, tr, C), lambda i, me: (me[0], i % per_layer, 0)) if o.ndim == 3
                 else pl.BlockSpec((tr, C), lambda i, me: (i % per_layer, 0)) for o in owns]
    me = (4 * lax.axis_index("x") + 2 * lax.axis_index("y") + lax.axis_index("c")).astype(jnp.int32).reshape(1)
    return pl.pallas_call(
        body,
        name=name,
        grid_spec=pltpu.PrefetchScalarGridSpec(
            num_scalar_prefetch=1,
            grid=(LR // tr,),
            in_specs=[pl.BlockSpec((n, tr, C), lambda i, me: (0, i, 0))] + own_specs + [blk, blk, blk] + _any_specs(len(extra)),
            out_specs=[blk] * 4,
        ),
        out_shape=[jax.ShapeDtypeStruct((LR, C), F32)] * 4,
        compiler_params=_params(1),
    )(me, parts, *owns, w, m, v, *extra)


def pack_small(small, sse, D):
    W = max(D, 2 * LANES)

    def body(nm0, nm1, nf0, nf1, gains, sinks, sse_ref, o_ref):
        o_ref[...] = jnp.zeros_like(o_ref)
        o_ref[0:1, :D] = nm0[0:1, :]
        o_ref[1:2, :D] = nm1[0:1, :]
        o_ref[2:3, :D] = nf0[0:1, :]
        o_ref[3:4, :D] = nf1[0:1, :]
        gq = gains[0:1, :] + pltpu.roll(gains[0:1, :], HEAD_DIM, 1)
        gk = gains[1:2, :] + pltpu.roll(gains[1:2, :], HEAD_DIM, 1)
        lane = lax.broadcasted_iota(jnp.int32, (1, LANES), 1)
        o_ref[4:5, :LANES] = jnp.where(lane < HEAD_DIM, gq, gk)
        o_ref[4:5, LANES : 2 * LANES] = sinks[0:1, :]
        o_ref[5:6, :LANES] = sse_ref[0:1, :] * (0.5 / D)

    return pl.pallas_call(
        body,
        name="pack_small",
        out_shape=jax.ShapeDtypeStruct((8, W), F32),
    )(small["nm0"], small["nm1"], small["nf0"], small["nf1"], small["gains"], small["sinks"], sse)


def _pack_small_params(nm, nf, qg, kg, sk, D):
    W = max(D, 2 * LANES)
    row4 = jnp.concatenate([qg.reshape(-1), kg.reshape(-1), jnp.zeros((LANES - 2 * HEAD_DIM,), F32), sk.reshape(-1)])
    row4 = jnp.pad(row4, (0, W - row4.shape[0]))
    rows = [jnp.pad(r, (0, W - D)) for r in (nm[0], nm[1], nf[0], nf[1])] + [row4]
    return jnp.concatenate([jnp.stack(rows), jnp.zeros((3, W), F32)], axis=0)


def _unpack_small(a, D, H):
    nm = a[0:2, :D]
    nf = a[2:4, :D]
    qg = a[4:5, 0:HEAD_DIM]
    kg = a[4:5, HEAD_DIM : 2 * HEAD_DIM]
    sk = a[4:5, LANES : LANES + H]
    return qg, kg, sk, nm, nf


def kernel(x, conv_w_in, conv_w, conv_w_out, attn_w_qkv, attn_q_gain, attn_k_gain, attn_sinks, attn_w_o, norm_mixer, norm_ffn, ffn_w_gate_up, ffn_w_down, loss_target, m_conv_w_in, m_conv_w, m_conv_w_out, m_attn_w_qkv, m_attn_q_gain, m_attn_k_gain, m_attn_sinks, m_attn_w_o, m_norm_mixer, m_norm_ffn, m_ffn_w_gate_up, m_ffn_w_down, v_conv_w_in, v_conv_w, v_conv_w_out, v_attn_w_qkv, v_attn_q_gain, v_attn_k_gain, v_attn_sinks, v_attn_w_o, v_norm_mixer, v_norm_ffn, v_ffn_w_gate_up, v_ffn_w_down):
    n_seq, seq, D = x.shape
    T = n_seq * seq
    H = D // HEAD_DIM
    L = ffn_w_gate_up.shape[0]

    full = run_plan(_Gather([conv_w_in[0].astype(BF16), conv_w[0], conv_w_out[0].astype(BF16)]), name="gather_conv_weights")
    w = dict(w_in=cols_from_shards(full[0]), cw=full[1].transpose(1, 0, 2).reshape(3, D),
             w_out=full[2].reshape(D, D))
    shards = dict(w_gu=[ffn_w_gate_up[l].astype(BF16) for l in range(L)], w_d=[ffn_w_down[l].astype(BF16) for l in range(L)],
                  w_qkv=attn_w_qkv[0].astype(BF16), w_o=attn_w_o[0].astype(BF16))
    gains = (norm_mixer, norm_ffn, attn_q_gain, attn_k_gain, attn_sinks)
    ex = Exchange()
    sse, gx, g, small = local_step(x.reshape(T, D), loss_target.reshape(T, D), gains, w, seq=seq, shards=shards, ex=ex)
    zones, own = ex.wait([g["w_in"]], name="exchange_wait")

    packed = pack_small(small, sse, D)
    token = ex.start([(g["w_in"], False, "w_in", g["w_in"].shape, None),
                      (packed, True, "small", (N_DEV,) + packed.shape, None)], name="exchange_last")

    def flat(a):
        return a.reshape(-1, a.shape[-1])

    big = [conv_w_in, conv_w, conv_w_out, attn_w_qkv, attn_w_o, ffn_w_gate_up, ffn_w_down]
    big_m = [m_conv_w_in, m_conv_w, m_conv_w_out, m_attn_w_qkv, m_attn_w_o, m_ffn_w_gate_up, m_ffn_w_down]
    big_v = [v_conv_w_in, v_conv_w, v_conv_w_out, v_attn_w_qkv, v_attn_w_o, v_ffn_w_gate_up, v_ffn_w_down]
    keys = ["w_in", "cw", "w_out", "w_qkv", "w_o", "w_gu", "w_d"]

    def update(b, zones, own, after=None):
        zone = zones[keys[b]]
        parts = zone.reshape(N_DEV, -1, zone.shape[-1])
        layers = [None] if zone.ndim == 3 else range(zone.shape[1])
        outs = adamw(parts, [own[(keys[b], l)] for l in layers], flat(big[b]), flat(big_m[b]), flat(big_v[b]),
                     name="adamw_" + keys[b], after=after)
        return [o.reshape(big[b].shape) for o in outs]

    res = [None] + [update(b, zones, own, after=token) for b in range(1, 7)]
    zones, own = ex.wait([r[0] for r in res[1:]], name="exchange_last_wait")
    res[0] = update(0, zones, own)
    sw = _pack_small_params(norm_mixer, norm_ffn, attn_q_gain, attn_k_gain, attn_sinks, D)
    sm = _pack_small_params(m_norm_mixer, m_norm_ffn, m_attn_q_gain, m_attn_k_gain, m_attn_sinks, D)
    sv = _pack_small_params(v_norm_mixer, v_norm_ffn, v_attn_q_gain, v_attn_k_gain, v_attn_sinks, D)
    souts = adamw(zones["small"], [own[("small", None)]], sw, sm, sv, name="adamw_small")
    sres = [_unpack_small(o, D, H) for o in souts]
    loss = souts[0][5, 0]

    def ordered(i):
        r, s = [r[i] for r in res], sres[i]
        return [r[0], r[1], r[2], r[3], s[0], s[1], s[2], r[4], s[3], s[4], r[5], r[6]]

    return (loss, gx.reshape(n_seq, seq, D), *ordered(0), *ordered(1), *ordered(2), *ordered(3))
```

```python
import functools
import math

import jax
import jax.numpy as jnp
from jax import lax
from jax.experimental import pallas as pl
from jax.experimental.pallas import tpu as pltpu

F32 = jnp.float32
BF16 = jnp.bfloat16

EPS = 1e-6
HEAD_DIM = 64
N_KV_HEADS = 4
BLOCK = 128
LANES = 128
N_DEV = 8
NEG = -1e30
SCALE = 1.0 / math.sqrt(HEAD_DIM)

ADAM_LR = 0.001
ADAM_B1 = 0.9
ADAM_B2 = 0.999
ADAM_EPS = 1e-08
ADAM_WD = 0.01
ADAM_STEP = 10

V7X_VMEM_BYTES = 64 * 1024 * 1024
VMEM_LIMIT = V7X_VMEM_BYTES - 2 * 1024 * 1024
MESH = pl.DeviceIdType.MESH

_NT = (((1,), (1,)), ((), ()))
_TN = (((0,), (0,)), ((), ()))


def _params(n_grid):
    return pltpu.CompilerParams(dimension_semantics=("arbitrary",) * n_grid, vmem_limit_bytes=VMEM_LIMIT)


def _resident(shape):
    nd = len(shape)
    return pl.BlockSpec(shape, lambda *_: (0,) * nd, pipeline_mode=pl.Buffered(1))


def _rms(x):
    return lax.rsqrt(jnp.mean(x * x, axis=-1, keepdims=True) + EPS)


def _rms_bwd(x, r, gain, dh):
    xn = x * r
    dxn = dh * gain
    dx = r * (dxn - xn * jnp.mean(dxn * xn, axis=-1, keepdims=True))
    return dx, jnp.sum(dh * xn, axis=0, keepdims=True)


def _dot(a, b):
    return jnp.dot(a, b, preferred_element_type=F32)


def _dot_nt(a, b):
    return lax.dot_general(a, b, _NT, preferred_element_type=F32)


def _dot_tn(a, b):
    return lax.dot_general(a, b, _TN, preferred_element_type=F32)


def _place():
    return lax.axis_index("x"), lax.axis_index("y"), lax.axis_index("c")


def _flip(v, bit):
    return 1 - v if bit else v


def _slot(px, py, pc):
    return 4 * px + 2 * py + pc


class _Gather:
    def __init__(self, shards):
        nt = len(shards)
        self.nt = nt
        self.inputs = list(shards)
        self.out_shapes = [jax.ShapeDtypeStruct((N_DEV,) + s.shape, s.dtype) for s in shards]
        self.scratch = [pltpu.SemaphoreType.DMA((nt, 10)), pltpu.SemaphoreType.DMA((nt, 10)), pltpu.SemaphoreType.DMA((nt,))]
        self.aliases = {}
        self.split = []
        for s in shards:
            rows, tile = s.shape[0], 16 if s.dtype == BF16 else 8
            self.split.append(rows // 2 if rows % (2 * tile) == 0 else rows)

    def phases(self, total):
        assert total >= 8
        return [(0, self.start), (total // 2, self.second), (total - 3, self.forward), (total - 1, self.finish)]

    def _copies(self, ins, outs, sems):
        send_sems, recv_sems, loc_sems = sems
        x, y, c = _place()
        xn, yn, sib = (1 - x, y, c), (x, 1 - y, c), (x, y, 1 - c)
        i_me, i_xn, i_yn, i_dn = _slot(x, y, c), _slot(1 - x, y, c), _slot(x, 1 - y, c), _slot(1 - x, 1 - y, c)
        j_me, j_xn, j_yn, j_dn = _slot(x, y, 1 - c), _slot(1 - x, y, 1 - c), _slot(x, 1 - y, 1 - c), _slot(1 - x, 1 - y, 1 - c)
        local, start, need1, second, need2, forward, need3 = [], [], [], [], [], [], []
        for t in range(self.nt):
            o, rows, h = outs[t], self.inputs[t].shape[0], self.split[t]
            lo = pl.ds(0, h)
            hi = pl.ds(h, rows - h) if h < rows else None

            def rc(k, src, dst, to, t=t):
                return pltpu.make_async_remote_copy(
                    src_ref=src, dst_ref=dst, send_sem=send_sems.at[t, k], recv_sem=recv_sems.at[t, k], device_id=to,
                    device_id_type=MESH)

            def landed(k, slot, part, frm):
                ref = o.at[slot] if part is None else o.at[slot, part]
                return rc(k, ref, ref, frm)

            local.append(pltpu.make_async_copy(ins[t], o.at[i_me], loc_sems.at[t]))
            start += [rc(0, ins[t], o.at[i_me], sib), rc(1, ins[t].at[lo], o.at[i_me, lo], xn),
                      rc(4, ins[t].at[lo], o.at[i_me, lo], yn)]
            need1.append(landed(1, i_xn, lo, xn))
            second.append(rc(3, o.at[i_xn, lo], o.at[i_xn, lo], yn))
            need2 += [landed(4, i_yn, lo, yn), landed(3, i_dn, lo, yn)]
            if hi is not None:
                start += [rc(2, ins[t].at[hi], o.at[i_me, hi], yn), rc(6, ins[t].at[hi], o.at[i_me, hi], xn)]
                need1.append(landed(2, i_yn, hi, yn))
                second.append(rc(5, o.at[i_yn, hi], o.at[i_yn, hi], xn))
                need2 += [landed(6, i_xn, hi, xn), landed(5, i_dn, hi, xn)]
            forward += [rc(7, o.at[i_xn], o.at[i_xn], sib), rc(8, o.at[i_yn], o.at[i_yn], sib), rc(9, o.at[i_dn], o.at[i_dn], sib)]
            need3 += [landed(0, j_me, None, sib), landed(7, j_xn, None, sib), landed(8, j_yn, None, sib), landed(9, j_dn, None, sib)]
        return local, start, need1, second, need2, forward, need3

    def start(self, ins, outs, sems):
        local, start, *_ = self._copies(ins, outs, sems)
        for cp in local + start:
            cp.start()

    def second(self, ins, outs, sems):
        _, _, need1, second, *_ = self._copies(ins, outs, sems)
        for cp in need1:
            cp.wait_recv()
        for cp in second:
            cp.start()

    def forward(self, ins, outs, sems):
        _, _, _, _, need2, forward, _ = self._copies(ins, outs, sems)
        for cp in need2:
            cp.wait_recv()
        for cp in forward:
            cp.start()

    def finish(self, ins, outs, sems):
        local, start, _, second, _, forward, need3 = self._copies(ins, outs, sems)
        for cp in need3:
            cp.wait_recv()
        for cp in start + second + forward:
            cp.wait_send()
        for cp in local:
            cp.wait()


def _any_specs(n):
    return [pl.BlockSpec(memory_space=pl.ANY)] * n


def run_plan(plan, *, name):
    def body(*refs):
        n_in, n_out = len(plan.inputs), len(plan.out_shapes)
        ins, outs, sems = refs[:n_in], refs[n_in : n_in + n_out], refs[n_in + n_out :]
        for _, phase in plan.phases(8):
            phase(ins, outs, sems)

    return pl.pallas_call(
        body,
        name=name,
        in_specs=_any_specs(len(plan.inputs)),
        out_specs=_any_specs(len(plan.out_shapes)),
        out_shape=plan.out_shapes,
        scratch_shapes=plan.scratch,
        input_output_aliases=plan.aliases,
    )(*plan.inputs)


_HBM = pl.BlockSpec(memory_space=pltpu.HBM)
_SEM = pl.BlockSpec(memory_space=pltpu.SEMAPHORE)
_DATAFLOW = pltpu.SideEffectType.DATAFLOW_SIDE_EFFECTING


class Exchange:
    def __init__(self):
        self.zones = {}
        self.pending = []
        self.sources = []

    def start(self, items, *, name):
        nt = len(items)
        keys = list(dict.fromkeys(it[2] for it in items))
        for a, _, key, shape, _ in items:
            if key not in self.zones:
                self.zones[key] = lax.empty(shape, a.dtype)
        nz = len(keys)

        def body(*refs):
            ins, zones, sems, token = refs[:nt], refs[nt : nt + nz], refs[nt + nz : nt + nz + 2 * nt], refs[-1]
            x, y, c = _place()
            me = _slot(x, y, c)
            for k in range(1, N_DEV):
                px, py, pc = _flip(x, (k >> 2) & 1), _flip(y, (k >> 1) & 1), _flip(c, k & 1)
                for t, (_, whole, key, _, layer) in enumerate(items):
                    zone = zones[keys.index(key)]
                    pltpu.make_async_remote_copy(
                        src_ref=ins[t] if whole else ins[t].at[_slot(px, py, pc)],
                        dst_ref=zone.at[me] if layer is None else zone.at[me, layer],
                        send_sem=sems[2 * t], recv_sem=sems[2 * t + 1], device_id=(px, py, pc), device_id_type=MESH).start()
            token[...] = jnp.zeros_like(token)

        bufs = [pltpu.with_memory_space_constraint(b, pltpu.HBM) for b in [it[0] for it in items] + [self.zones[k] for k in keys]]
        outs = pl.pallas_call(
            body,
            name=name,
            in_specs=[_HBM] * (nt + nz),
            out_specs=[_SEM] * (2 * nt) + [_HBM] * (nt + nz) + [pl.BlockSpec(memory_space=pltpu.VMEM)],
            out_shape=[pltpu.SemaphoreType.DMA(())] * (2 * nt) + [pltpu.HBM(b.shape, b.dtype) for b in bufs]
            + [jax.ShapeDtypeStruct((8, LANES), F32)],
            input_output_aliases={i: 2 * nt + i for i in range(nt + nz)},
            compiler_params=pltpu.CompilerParams(has_side_effects=_DATAFLOW),
        )(*bufs)
        for t, (_, _, key, _, layer) in enumerate(items):
            self.pending.append((outs[2 * t], outs[2 * t + 1], key, layer))
        self.sources += [((it[2], it[4]), a) for it, a in zip(items, outs[2 * nt : 3 * nt])]
        for i, key in enumerate(keys):
            self.zones[key] = outs[3 * nt + i]
        return outs[-1]

    def wait(self, after, *, name):
        pending, keys = self.pending, list(self.zones)
        names, sources = [n for n, _ in self.sources], [a for _, a in self.sources]
        ns, nz, npend = len(sources), len(keys), len(pending)
        self.pending, self.sources = [], []

        def body(*refs):
            zones, sems = refs[ns : ns + nz], refs[ns + nz : ns + nz + 2 * npend]
            x, y, c = _place()
            for i, (_, _, key, layer) in enumerate(pending):
                zone = zones[keys.index(key)]
                rows = pl.ds(0, N_DEV - 1)
                seven = zone.at[rows] if layer is None else zone.at[rows, layer]
                pltpu.make_async_remote_copy(
                    src_ref=seven, dst_ref=seven, send_sem=sems[2 * i], recv_sem=sems[2 * i + 1],
                    device_id=(x, y, c), device_id_type=MESH).wait()

        bufs = list(sources) + [self.zones[k] for k in keys]
        flat_sems = [s for p in pending for s in p[:2]]
        outs = pl.pallas_call(
            body,
            name=name,
            in_specs=[_HBM] * (ns + nz) + [_SEM] * (2 * npend) + _any_specs(len(after)),
            out_specs=[_HBM] * (ns + nz),
            out_shape=[pltpu.HBM(b.shape, b.dtype) for b in bufs],
            input_output_aliases={i: i for i in range(ns + nz)},
            compiler_params=pltpu.CompilerParams(has_side_effects=_DATAFLOW),
        )(*bufs, *flat_sems, *after)
        self.zones = {}
        return dict(zip(keys, outs[ns:])), dict(zip(names, outs[:ns]))


def _call(body, *, name, grid, in_specs, out_specs, out_shape, args, scratch=(), plan=None, after=None):
    if after is not None:
        inner, n_real = body, len(in_specs)
        body = lambda *refs: inner(*refs[:n_real], *refs[n_real + 1 :])
        in_specs, args = list(in_specs) + _any_specs(1), list(args) + [after]
    n_in, n_out, n_scr = len(in_specs), len(out_specs), len(scratch)
    if plan is None:
        outs = pl.pallas_call(
            body, name=name, grid=grid, in_specs=in_specs, out_specs=out_specs, out_shape=out_shape,
            scratch_shapes=list(scratch), compiler_params=_params(len(grid)))(*args)
        return outs, None
    c_in, c_out = len(plan.inputs), len(plan.out_shapes)
    phases = plan.phases(math.prod(grid))

    def full(*refs):
        a, refs = refs[:n_in], refs[n_in:]
        ci, refs = refs[:c_in], refs[c_in:]
        o, refs = refs[:n_out], refs[n_out:]
        co, refs = refs[:c_out], refs[c_out:]
        s, cs = refs[:n_scr], refs[n_scr:]
        step = pl.program_id(0)
        for d in range(1, len(grid)):
            step = step * grid[d] + pl.program_id(d)
        for at, phase in phases:
            if at == 0:
                pl.when(step == 0)(functools.partial(phase, ci, co, cs))
        body(*a, *o, *s)
        for at, phase in phases:
            if at > 0:
                pl.when(step == at)(functools.partial(phase, ci, co, cs))

    outs = pl.pallas_call(
        full,
        name=name,
        grid=grid,
        in_specs=list(in_specs) + _any_specs(c_in),
        out_specs=list(out_specs) + _any_specs(c_out),
        out_shape=list(out_shape) + plan.out_shapes,
        scratch_shapes=list(scratch) + plan.scratch,
        input_output_aliases={n_in + i: n_out + t for i, t in plan.aliases.items()},
        compiler_params=_params(len(grid)),
    )(*args, *plan.inputs)
    return outs[:n_out], outs[n_out:]


def _row_tile(R):
    return 256 if R % 256 == 0 else R


def cols_from_shards(a):
    n, R, C = a.shape
    tr = _row_tile(R)

    def body(i_ref, o_ref):
        for s in range(n):
            o_ref[:, s * C : (s + 1) * C] = i_ref[s]

    return pl.pallas_call(
        body,
        name="cols_from_shards",
        grid=(R // tr,),
        in_specs=[pl.BlockSpec((n, tr, C), lambda i: (0, i, 0))],
        out_specs=pl.BlockSpec((tr, n * C), lambda i: (i, 0)),
        out_shape=jax.ShapeDtypeStruct((R, n * C), a.dtype),
        compiler_params=_params(1),
    )(a)


def shards_from_cols(a):
    R, W = a.shape
    C = W // N_DEV
    tr = _row_tile(R)

    def body(i_ref, o_ref):
        for s in range(N_DEV):
            o_ref[s] = i_ref[:, s * C : (s + 1) * C]

    return pl.pallas_call(
        body,
        name="shards_from_cols",
        grid=(R // tr,),
        in_specs=[pl.BlockSpec((tr, W), lambda i: (i, 0))],
        out_specs=pl.BlockSpec((N_DEV, tr, C), lambda i: (0, i, 0)),
        out_shape=jax.ShapeDtypeStruct((N_DEV, R, C), a.dtype),
        compiler_params=_params(1),
    )(a)


def _shift_down(u, prev8, row, n):
    out = pltpu.roll(u, n, 0)
    for k in range(n):
        out = jnp.where(row == k, prev8[8 - n + k : 8 - n + k + 1, :], out)
    return out


def _shift_up(u, next8, row, n, tm):
    out = pltpu.roll(u, tm - n, 0)
    for k in range(n):
        out = jnp.where(row == tm - n + k, next8[k : k + 1, :], out)
    return out


def conv_fwd(x, gain, w_in, cw, w_out, *, seq, tm, plan=None):
    T, D = x.shape
    tps = seq // tm

    def body(x_ref, g_ref, win_ref, cw_ref, wout_ref, x1_ref, bcx_ref, y_ref, z_ref, carry_ref):
        i = pl.program_id(0)

        @pl.when(i % tps == 0)
        def _():
            carry_ref[...] = jnp.zeros_like(carry_ref)

        xt = x_ref[...]
        h = ((xt * _rms(xt)) * g_ref[...]).astype(BF16)
        bcx = _dot(h, win_ref[...])
        bcx_ref[...] = bcx.astype(BF16)
        b, c, xv = bcx[:, :D], bcx[:, D : 2 * D], bcx[:, 2 * D :]
        u = b * xv
        row = lax.broadcasted_iota(jnp.int32, u.shape, 0)
        prev = carry_ref[...]
        u1 = _shift_down(u, prev, row, 1)
        u2 = _shift_down(u, prev, row, 2)
        carry_ref[...] = u[tm - 8 :, :]
        cwv = cw_ref[...]
        y = cwv[0:1, :] * u2 + cwv[1:2, :] * u1 + cwv[2:3, :] * u
        y_ref[...] = y
        z = (c * y).astype(BF16)
        z_ref[...] = z
        x1_ref[...] = xt + _dot(z, wout_ref[...])

    tile = pl.BlockSpec((tm, D), lambda i: (i, 0))
    return _call(
        body,
        plan=plan,
        args=(x, gain, w_in, cw, w_out),
        name="conv_fwd",
        grid=(T // tm,),
        in_specs=[
            pl.BlockSpec((tm, D), lambda i: (i, 0)),
            _resident((1, D)),
            _resident((D, 3 * D)),
            _resident((3, D)),
            _resident((D, D)),
        ],
        out_specs=[tile, pl.BlockSpec((tm, 3 * D), lambda i: (i, 0)), tile, tile],
        out_shape=[jax.ShapeDtypeStruct((T, D), F32), jax.ShapeDtypeStruct((T, 3 * D), BF16),
                   jax.ShapeDtypeStruct((T, D), F32), jax.ShapeDtypeStruct((T, D), BF16)],
        scratch=[pltpu.VMEM((8, D), F32)],
    )


def conv_bwd(dx1, x, gain, bcx, y, cw, w_in, w_out, *, seq, tm, after=None):
    T, D = x.shape
    n = T // tm
    tps = seq // tm

    def body(d_ref, x_ref, g_ref, bcx_ref, y_ref, cw_ref, win_ref, wout_ref,
             gx_ref, dbcx_ref, h_ref, d16_ref, dcw_ref, dg_ref, carry_ref):
        i = pl.program_id(0)
        t = n - 1 - i

        @pl.when(i == 0)
        def _():
            dcw_ref[...] = jnp.zeros_like(dcw_ref)
            dg_ref[...] = jnp.zeros_like(dg_ref)

        @pl.when(t % tps == tps - 1)
        def _():
            carry_ref[...] = jnp.zeros_like(carry_ref)

        d = d_ref[...]
        d16 = d.astype(BF16)
        d16_ref[...] = d16
        dz = _dot_nt(d16, wout_ref[...])
        bcx = bcx_ref[...].astype(F32)
        b, c, xv = bcx[:, :D], bcx[:, D : 2 * D], bcx[:, 2 * D :]
        u = b * xv
        row = lax.broadcasted_iota(jnp.int32, u.shape, 0)
        cwv = cw_ref[...]
        dc = dz * y_ref[...]
        dy = dz * c
        nxt = carry_ref[...]
        dy1 = _shift_up(dy, nxt, row, 1, tm)
        dy2 = _shift_up(dy, nxt, row, 2, tm)
        carry_ref[...] = dy[0:8, :]
        dcw_ref[0:1, :] += jnp.sum(dy2 * u, axis=0, keepdims=True)
        dcw_ref[1:2, :] += jnp.sum(dy1 * u, axis=0, keepdims=True)
        dcw_ref[2:3, :] += jnp.sum(dy * u, axis=0, keepdims=True)
        du = cwv[2:3, :] * dy + cwv[1:2, :] * dy1 + cwv[0:1, :] * dy2
        dbcx_ref[:, :D] = (du * xv).astype(BF16)
        dbcx_ref[:, D : 2 * D] = dc.astype(BF16)
        dbcx_ref[:, 2 * D :] = (du * b).astype(BF16)
        dh = _dot_nt(dbcx_ref[...], win_ref[...])
        xt = x_ref[...]
        r = _rms(xt)
        gn = g_ref[...]
        h_ref[...] = ((xt * r) * gn).astype(BF16)
        dx, dgn = _rms_bwd(xt, r, gn, dh)
        dg_ref[0:1, :] += dgn
        gx_ref[...] = d + dx

    rev = lambda i: (n - 1 - i, 0)
    return _call(
        body,
        after=after,
        args=(dx1, x, gain, bcx, y, cw, w_in, w_out),
        name="conv_bwd",
        grid=(n,),
        in_specs=[
            pl.BlockSpec((tm, D), rev),
            pl.BlockSpec((tm, D), rev),
            _resident((1, D)),
            pl.BlockSpec((tm, 3 * D), rev),
            pl.BlockSpec((tm, D), rev),
            _resident((3, D)),
            _resident((D, 3 * D)),
            _resident((D, D)),
        ],
        out_specs=[
            pl.BlockSpec((tm, D), rev),
            pl.BlockSpec((tm, 3 * D), rev),
            pl.BlockSpec((tm, D), rev),
            pl.BlockSpec((tm, D), rev),
            pl.BlockSpec((8, D), lambda i: (0, 0)),
            pl.BlockSpec((8, D), lambda i: (0, 0)),
        ],
        out_shape=[
            jax.ShapeDtypeStruct((T, D), F32),
            jax.ShapeDtypeStruct((T, 3 * D), BF16),
            jax.ShapeDtypeStruct((T, D), BF16),
            jax.ShapeDtypeStruct((T, D), BF16),
            jax.ShapeDtypeStruct((8, D), F32),
            jax.ShapeDtypeStruct((8, D), F32),
        ],
        scratch=[pltpu.VMEM((8, D), F32)],
    )


MXU_TILE = 256
FFN_CHUNK = 4 * MXU_TILE


def _sigmoid(g):
    return 1.0 / (1.0 + jnp.exp(-g))


def _ffn_chunks(F):
    assert F % MXU_TILE == 0
    return [(s, min(FFN_CHUNK, F - s)) for s in range(0, F, FFN_CHUNK)]


def ffn_fwd(x, gain, w_gu, w_d, *, tm, plan=None, attn=None, target=None):
    T, D = x.shape
    F = w_d.shape[0]
    row = lambda i: (i, 0)
    tile = pl.BlockSpec((tm, D), row)

    def body(*refs):
        refs = list(refs)
        x_ref, g_ref, wgu_ref, wd_ref = refs[:4]
        del refs[:4]
        if attn is not None:
            ao_ref, wo_ref = refs[:2]
            del refs[:2]
        if target is not None:
            t_ref = refs.pop(0)
        if attn is not None:
            xin_ref = refs.pop(0)
        xo_ref, gu_ref = refs[:2]
        xt = x_ref[...]
        if attn is not None:
            xt = xt + _dot(ao_ref[...], wo_ref[...])
            xin_ref[...] = xt
        h = ((xt * _rms(xt)) * g_ref[...]).astype(BF16)
        acc = xt
        for s, n in _ffn_chunks(F):
            g = _dot(h, wgu_ref[:, s : s + n])
            u = _dot(h, wgu_ref[:, F + s : F + s + n])
            gu_ref[:, s : s + n] = g
            gu_ref[:, F + s : F + s + n] = u
            a = ((g * _sigmoid(g)) * u).astype(BF16)
            acc = acc + _dot(a, wd_ref[s : s + n, :])
        if target is None:
            xo_ref[...] = acc
        else:
            s_ref = refs[2]

            @pl.when(pl.program_id(0) == 0)
            def _():
                s_ref[...] = jnp.zeros_like(s_ref)

            e = acc - t_ref[...]
            xo_ref[...] = e * (1.0 / D)
            s_ref[...] += jnp.sum(jnp.sum(e * e, axis=-1, keepdims=True), axis=0, keepdims=True)

    args = [x, gain, w_gu, w_d]
    in_specs = [tile, _resident((1, D)), _resident((D, 2 * F)), _resident((F, D))]
    out_specs = [tile, pl.BlockSpec((tm, 2 * F), row)]
    out_shape = [jax.ShapeDtypeStruct((T, D), F32), jax.ShapeDtypeStruct((T, 2 * F), F32)]
    if attn is not None:
        args += list(attn)
        in_specs += [pl.BlockSpec((tm, attn[0].shape[1]), row), _resident(attn[1].shape)]
        out_specs.insert(0, tile)
        out_shape.insert(0, jax.ShapeDtypeStruct((T, D), F32))
    if target is not None:
        args.append(target)
        in_specs.append(tile)
        out_specs.append(pl.BlockSpec((8, LANES), lambda i: (0, 0)))
        out_shape.append(jax.ShapeDtypeStruct((8, LANES), F32))
    return _call(body, plan=plan, args=args, name="ffn_fwd", grid=(T // tm,), in_specs=in_specs, out_specs=out_specs,
                 out_shape=out_shape)


def ffn_bwd(dxo, x, gain, gu, w_gu, w_d, *, tm, after=None, w_o=None):
    T, D = x.shape
    F = w_d.shape[0]

    def body(d_ref, x_ref, g_ref, gu_ref, wgu_ref, wd_ref, *rest):
        if w_o is not None:
            wo_ref, rest = rest[0], rest[1:]
        dx_ref, a_ref, dgu_ref, h_ref, d16_ref, dg_ref = rest[:6]

        @pl.when(pl.program_id(0) == 0)
        def _():
            dg_ref[...] = jnp.zeros_like(dg_ref)

        d = d_ref[...]
        d16 = d.astype(BF16)
        d16_ref[...] = d16
        dh = jnp.zeros((tm, D), F32)
        for c0, n in _ffn_chunks(F):
            g = gu_ref[:, c0 : c0 + n]
            u = gu_ref[:, F + c0 : F + c0 + n]
            da = _dot_nt(d16, wd_ref[c0 : c0 + n, :])
            s = _sigmoid(g)
            sg = g * s
            a_ref[:, c0 : c0 + n] = (sg * u).astype(BF16)
            dg16 = (da * u * (s + sg * (1.0 - s))).astype(BF16)
            du16 = (da * sg).astype(BF16)
            dgu_ref[:, c0 : c0 + n] = dg16
            dgu_ref[:, F + c0 : F + c0 + n] = du16
            dh = dh + _dot_nt(dg16, wgu_ref[:, c0 : c0 + n]) + _dot_nt(du16, wgu_ref[:, F + c0 : F + c0 + n])
        xt = x_ref[...]
        r = _rms(xt)
        gn = g_ref[...]
        h_ref[...] = ((xt * r) * gn).astype(BF16)
        dx, dgn = _rms_bwd(xt, r, gn, dh)
        dg_ref[0:1, :] += dgn
        dxi = d + dx
        dx_ref[...] = dxi
        if w_o is not None:
            dxi16_ref, dao_ref = rest[6:8]
            dxi16 = dxi.astype(BF16)
            dxi16_ref[...] = dxi16
            dao_ref[...] = _dot_nt(dxi16, wo_ref[...]).astype(BF16)

    tile = pl.BlockSpec((tm, D), lambda i: (i, 0))
    args = [dxo, x, gain, gu, w_gu, w_d]
    wide = lambda n: pl.BlockSpec((tm, n), lambda i: (i, 0))
    in_specs = [tile, tile, _resident((1, D)), wide(2 * F), _resident((D, 2 * F)), _resident((F, D))]
    out_specs = [tile, wide(F), wide(2 * F), tile, tile, pl.BlockSpec((8, D), lambda i: (0, 0))]
    out_shape = [
        jax.ShapeDtypeStruct((T, D), F32),
        jax.ShapeDtypeStruct((T, F), BF16),
        jax.ShapeDtypeStruct((T, 2 * F), BF16),
        jax.ShapeDtypeStruct((T, D), BF16),
        jax.ShapeDtypeStruct((T, D), BF16),
        jax.ShapeDtypeStruct((8, D), F32),
    ]
    if w_o is not None:
        args.append(w_o)
        in_specs.append(_resident(w_o.shape))
        out_specs += [tile, pl.BlockSpec((tm, w_o.shape[0]), lambda i: (i, 0))]
        out_shape += [jax.ShapeDtypeStruct((T, D), BF16), jax.ShapeDtypeStruct((T, w_o.shape[0]), BF16)]
    return _call(body, after=after, args=args, name="ffn_bwd", grid=(T // tm,), in_specs=in_specs, out_specs=out_specs,
                 out_shape=out_shape)


def wgrad(a, b, *, name, a_cols=0, b_cols=0, group=1, flat=False, a_resident=False, tk, out_dtype=BF16, after=None):
    T, K = a.shape
    J = 1
    if a_cols:
        K = a_cols
        J = a.shape[1] // K
        a_spec = pl.BlockSpec((tk, K), lambda j, k: (k, j))
    elif a_resident:
        assert tk == T
        a_spec = _resident((T, K))
    else:
        a_spec = pl.BlockSpec((tk, K), lambda j, k: (k, 0))
    if b_cols:
        N = b_cols * group
        J = b.shape[1] // N
        b_spec = pl.BlockSpec((tk, N), lambda j, k: (k, j))
    else:
        N = b.shape[1]
        b_spec = pl.BlockSpec((tk, N), lambda j, k: (k, 0))
    nk = T // tk
    if flat:
        o_spec, o_shape = pl.BlockSpec((K, N), lambda j, k: (0, j)), (K, J * N)
    elif group > 1:
        o_spec, o_shape = pl.BlockSpec((group, K, b_cols), lambda j, k: (j, 0, 0)), (J * group, K, b_cols)
    else:
        o_spec, o_shape = pl.BlockSpec((None, K, N), lambda j, k: (j, 0, 0)), (J, K, N)

    def body(a_ref, b_ref, o_ref, acc_ref):
        k = pl.program_id(1)

        @pl.when(k == 0)
        def _():
            acc_ref[...] = jnp.zeros_like(acc_ref)

        acc_ref[...] += _dot_tn(a_ref[...], b_ref[...])

        @pl.when(k == nk - 1)
        def _():
            if group > 1 and not flat:
                for i in range(group):
                    o_ref[i] = acc_ref[:, i * b_cols : (i + 1) * b_cols].astype(out_dtype)
            else:
                o_ref[...] = acc_ref[...].astype(out_dtype)

    outs, _ = _call(
        body,
        after=after,
        name=name,
        grid=(J, nk),
        in_specs=[a_spec, b_spec],
        out_specs=[o_spec],
        out_shape=[jax.ShapeDtypeStruct(o_shape, out_dtype)],
        args=(a, b),
        scratch=[pltpu.VMEM((K, N), F32)],
    )
    return outs[0]


def _seg(xs, lo):
    s_lo = [jnp.sum(jnp.where(lo, x, 0.0), axis=-1, keepdims=True) for x in xs]
    s_hi = [jnp.sum(jnp.where(lo, 0.0, x), axis=-1, keepdims=True) for x in xs]
    return [jnp.where(lo, a, b) for a, b in zip(s_lo, s_hi)]


def _head_norm(xs, gains, lo):
    rs = [lax.rsqrt(s * (1.0 / HEAD_DIM) + EPS) for s in _seg([x * x for x in xs], lo)]
    return [(x * r) * g for x, r, g in zip(xs, rs, gains)], rs


def _head_norm_bwd(xs, rs, gains, dys, lo):
    xns = [x * r for x, r in zip(xs, rs)]
    dxns = [dy * g for dy, g in zip(dys, gains)]
    means = [s * (1.0 / HEAD_DIM) for s in _seg([a * b for a, b in zip(dxns, xns)], lo)]
    dxs = [r * (dxn - xn * m) for r, dxn, xn, m in zip(rs, dxns, xns, means)]
    return dxs, [jnp.sum(dy * xn, axis=0, keepdims=True) for dy, xn in zip(dys, xns)]


def _swap_halves(x):
    return pltpu.roll(x, HEAD_DIM, 1)


def qkv_proj(x, gain, w, qg, kg, *, tm):
    T, D = x.shape
    N = w.shape[1]
    kvw = N_KV_HEADS * HEAD_DIM
    nqt, nkt = D // LANES, kvw // LANES

    def body(x_ref, g_ref, w_ref, qg_ref, kg_ref, qkv_ref, q_ref, kd_ref, vd_ref):
        xt = x_ref[...]
        h = ((xt * _rms(xt)) * g_ref[...]).astype(BF16)
        qkv = _dot(h, w_ref[...])
        qkv_ref[...] = qkv
        lo = lax.broadcasted_iota(jnp.int32, (1, LANES), 1) < HEAD_DIM
        tiles = [qkv[:, t * LANES : (t + 1) * LANES] for t in range(nqt + nkt)]
        normed, _ = _head_norm(tiles, [qg_ref[...]] * nqt + [kg_ref[...]] * nkt, lo)
        for t in range(nqt):
            q_ref[:, t * LANES : (t + 1) * LANES] = (normed[t] * SCALE).astype(BF16)
        for t in range(nkt):
            kn = normed[nqt + t]
            v = qkv[:, D + kvw + t * LANES : D + kvw + (t + 1) * LANES]
            for src, dst in ((kn, kd_ref), (v, vd_ref)):
                sw = _swap_halves(src)
                dst[:, 2 * t * LANES : (2 * t + 1) * LANES] = jnp.where(lo, src, sw).astype(BF16)
                dst[:, (2 * t + 1) * LANES : (2 * t + 2) * LANES] = jnp.where(lo, sw, src).astype(BF16)

    row = lambda i: (i, 0)
    return pl.pallas_call(
        body,
        name="qkv_proj",
        grid=(T // tm,),
        in_specs=[pl.BlockSpec((tm, D), row), _resident((1, D)), _resident((D, N)), _resident((1, LANES)), _resident((1, LANES))],
        out_specs=[pl.BlockSpec((tm, N), row), pl.BlockSpec((tm, D), row), pl.BlockSpec((tm, 2 * kvw), row), pl.BlockSpec((tm, 2 * kvw), row)],
        out_shape=[
            jax.ShapeDtypeStruct((T, N), F32),
            jax.ShapeDtypeStruct((T, D), BF16),
            jax.ShapeDtypeStruct((T, 2 * kvw), BF16),
            jax.ShapeDtypeStruct((T, 2 * kvw), BF16),
        ],
        compiler_params=_params(1),
    )(x, gain, w, qg, kg)


def _attn_tables(sinks, n_q_heads):
    P = n_q_heads // N_KV_HEADS // 2
    h = jnp.arange(1, n_q_heads + 1, dtype=F32)
    slopes = jnp.exp2(-8.0 * h / n_q_heads).reshape(N_KV_HEADS, P, 1, 2, 1)
    qi = jnp.arange(BLOCK)[:, None]
    kj = jnp.arange(BLOCK)[None, :]
    dist = jnp.where(kj <= qi, qi - kj, qi + BLOCK - kj).astype(F32)
    shape = (N_KV_HEADS, P, BLOCK, 2, BLOCK)
    bias = jnp.broadcast_to(-slopes * dist[None, None, :, None, :], shape)
    sink = jnp.broadcast_to(sinks.astype(F32).reshape(N_KV_HEADS, P, 1, 2, 1), shape)
    return bias.reshape(N_KV_HEADS, P * BLOCK, 2 * BLOCK), sink.reshape(N_KV_HEADS, P * BLOCK, 2 * BLOCK)


def _attn_specs(D, nb):
    kvw2 = 2 * N_KV_HEADS * HEAD_DIM
    cur = lambda b, i: (b * nb + i, 0)
    prev = lambda b, i: (jnp.maximum(b * nb + i - 1, 0), 0)
    return [
        pl.BlockSpec((BLOCK, D), cur),
        pl.BlockSpec((BLOCK, kvw2), cur),
        pl.BlockSpec((BLOCK, kvw2), prev),
        pl.BlockSpec((BLOCK, kvw2), cur),
        pl.BlockSpec((BLOCK, kvw2), prev),
    ]


def _attn_operands(kh, P, lo, q_ref, kc_ref, kp_ref, vc_ref, vp_ref):
    sl = slice(kh * LANES, (kh + 1) * LANES)

    def cat(prev_ref, cur_ref):
        d = jnp.concatenate([prev_ref[:, sl], cur_ref[:, sl]], axis=0)
        z = jnp.zeros_like(d)
        return jnp.concatenate([jnp.where(lo, d, z), jnp.where(lo, z, d)], axis=0)

    qt = jnp.concatenate([q_ref[:, (kh * P + pr) * LANES : (kh * P + pr + 1) * LANES] for pr in range(P)], axis=0)
    return qt, cat(kp_ref, kc_ref), cat(vp_ref, vc_ref)


def _attn_exp(s_all, bias, sink, tri, first):
    out = []
    for par in range(2):
        c0 = 2 * par * BLOCK
        s = jnp.where(tri, s_all[:, c0 + BLOCK : c0 + 2 * BLOCK], jnp.where(first, NEG, s_all[:, c0 : c0 + BLOCK]))
        s = s + bias[:, par * BLOCK : (par + 1) * BLOCK]
        snk = sink[:, par * BLOCK : (par + 1) * BLOCK]
        m = jnp.maximum(jnp.max(s, axis=-1, keepdims=True), snk)
        out.append((jnp.exp(s - m), jnp.exp(snk - m)))
    return out


def _unfold(x, tri):
    z = jnp.zeros_like(x)
    return jnp.concatenate([jnp.where(tri, z, x), jnp.where(tri, x, z)], axis=1)


def _attn_masks(R):
    lane = lax.broadcasted_iota(jnp.int32, (1, LANES), 1)
    row = lax.broadcasted_iota(jnp.int32, (R, BLOCK), 0) & (BLOCK - 1)
    col = lax.broadcasted_iota(jnp.int32, (R, BLOCK), 1)
    return lane, lane < HEAD_DIM, col <= row


def attn_fwd(q16, kd, vd, bias, sink, *, seq, n_seq):
    T, D = q16.shape
    nb = seq // BLOCK
    P = D // HEAD_DIM // N_KV_HEADS // 2
    R = P * BLOCK
    KV = range(N_KV_HEADS)

    def body(q_ref, kc_ref, kp_ref, vc_ref, vp_ref, bias_ref, sink_ref, o_ref):
        first = pl.program_id(1) == 0
        _, lo, tri = _attn_masks(R)
        ops = [_attn_operands(kh, P, lo, q_ref, kc_ref, kp_ref, vc_ref, vp_ref) for kh in KV]
        s_all = [_dot_nt(ops[kh][0], ops[kh][1]) for kh in KV]
        ex = [_attn_exp(s_all[kh], bias_ref[kh], sink_ref[kh], tri, first) for kh in KV]
        den = [[jnp.sum(e, axis=-1, keepdims=True) + es for e, es in ex[kh]] for kh in KV]
        lhs = [jnp.concatenate([_unfold(e, tri) for e, _ in ex[kh]], axis=1).astype(BF16) for kh in KV]
        o = [_dot(lhs[kh], ops[kh][2]) for kh in KV]
        for kh in KV:
            out = o[kh] / jnp.where(lo, den[kh][0], den[kh][1])
            for pr in range(P):
                t = kh * P + pr
                o_ref[:, t * LANES : (t + 1) * LANES] = out[pr * BLOCK : (pr + 1) * BLOCK, :].astype(BF16)

    return pl.pallas_call(
        body,
        name="attn_fwd",
        grid=(n_seq, nb),
        in_specs=_attn_specs(D, nb) + [_resident((N_KV_HEADS, R, 2 * BLOCK)), _resident((N_KV_HEADS, R, 2 * BLOCK))],
        out_specs=pl.BlockSpec((BLOCK, D), lambda b, i: (b * nb + i, 0)),
        out_shape=jax.ShapeDtypeStruct((T, D), BF16),
        compiler_params=_params(2),
    )(q16, kd, kd, vd, vd, bias, sink)


def attn_bwd(q16, kd, vd, do, bias, sink, *, seq, n_seq):
    T, D = q16.shape
    kvw2 = 2 * N_KV_HEADS * HEAD_DIM
    nb = seq // BLOCK
    G = D // HEAD_DIM // N_KV_HEADS
    P = G // 2
    R = P * BLOCK
    KV = range(N_KV_HEADS)

    def body(q_ref, kc_ref, kp_ref, vc_ref, vp_ref, do_ref, bias_ref, sink_ref,
             dq_ref, dkc_ref, dkp_ref, dvc_ref, dvp_ref, dsink_ref):
        first = pl.program_id(1) == 0

        @pl.when(jnp.logical_and(pl.program_id(0) == 0, first))
        def _():
            dsink_ref[...] = jnp.zeros_like(dsink_ref)

        lane, lo, tri = _attn_masks(R)
        ops = [_attn_operands(kh, P, lo, q_ref, kc_ref, kp_ref, vc_ref, vp_ref) for kh in KV]
        do16 = [jnp.concatenate([do_ref[:, (kh * P + pr) * LANES : (kh * P + pr + 1) * LANES] for pr in range(P)], axis=0)
                for kh in KV]
        s_all = [_dot_nt(ops[kh][0], ops[kh][1]) for kh in KV]
        dp_all = [_dot_nt(do16[kh], ops[kh][2]) for kh in KV]
        ex = [_attn_exp(s_all[kh], bias_ref[kh], sink_ref[kh], tri, first) for kh in KV]
        den = [[jnp.sum(e, axis=-1, keepdims=True) for e, _ in ex[kh]] for kh in KV]
        dsink = jnp.zeros((1, LANES), F32)
        pf, dsf = [], []
        for kh in KV:
            ps_, ds_ = [], []
            for par in range(2):
                e, es = ex[kh][par]
                inv = 1.0 / (den[kh][par] + es)
                p = e * inv
                c0 = 2 * par * BLOCK
                dp = jnp.where(tri, dp_all[kh][:, c0 + BLOCK : c0 + 2 * BLOCK], dp_all[kh][:, c0 : c0 + BLOCK])
                delta = jnp.sum(p * dp, axis=-1, keepdims=True)
                ds_.append(_unfold(p * (dp - delta), tri))
                ps_.append(_unfold(p, tri))
                dsr = -((es * inv) * delta)
                for pr in range(P):
                    hq = kh * G + 2 * pr + par
                    tot = jnp.sum(dsr[pr * BLOCK : (pr + 1) * BLOCK, :], axis=0, keepdims=True)
                    dsink = dsink + jnp.where(lane == hq, tot, 0.0)
            pf.append(jnp.concatenate(ps_, axis=1).astype(BF16))
            dsf.append(jnp.concatenate(ds_, axis=1).astype(BF16))
        dq = [_dot(dsf[kh], ops[kh][1]) for kh in KV]
        dk = [_dot_tn(dsf[kh], ops[kh][0]) for kh in KV]
        dv = [_dot_tn(pf[kh], do16[kh]) for kh in KV]
        dsink_ref[0:1, :] += dsink
        for kh in KV:
            sl = slice(kh * LANES, (kh + 1) * LANES)
            for pr in range(P):
                t = kh * P + pr
                dq_ref[:, t * LANES : (t + 1) * LANES] = dq[kh][pr * BLOCK : (pr + 1) * BLOCK, :]
            for full, prev_ref, cur_ref in ((dk[kh], dkp_ref, dkc_ref), (dv[kh], dvp_ref, dvc_ref)):
                dup = jnp.where(lo, full[: 2 * BLOCK, :], full[2 * BLOCK :, :])
                prev_ref[:, sl] = dup[:BLOCK, :].astype(BF16)
                cur_ref[:, sl] = dup[BLOCK:, :].astype(BF16)

    cur = lambda b, i: (b * nb + i, 0)
    kv_spec = pl.BlockSpec((BLOCK, kvw2), cur)
    kv_shape = jax.ShapeDtypeStruct((T, kvw2), BF16)
    return pl.pallas_call(
        body,
        name="attn_bwd",
        grid=(n_seq, nb),
        in_specs=_attn_specs(D, nb)
        + [pl.BlockSpec((BLOCK, D), cur), _resident((N_KV_HEADS, R, 2 * BLOCK)), _resident((N_KV_HEADS, R, 2 * BLOCK))],
        out_specs=[pl.BlockSpec((BLOCK, D), cur), kv_spec, kv_spec, kv_spec, kv_spec, pl.BlockSpec((8, LANES), lambda b, i: (0, 0))],
        out_shape=[jax.ShapeDtypeStruct((T, D), F32), kv_shape, kv_shape, kv_shape, kv_shape, jax.ShapeDtypeStruct((8, LANES), F32)],
        compiler_params=_params(2),
    )(q16, kd, kd, vd, vd, do, bias, sink)


def qkv_bwd(dq, dkc, dkp, dvc, dvp, qkv, dres, x, gain, w_qkv, qg, kg, *, seq, tm):
    T, D = x.shape
    kvw2 = dkc.shape[1]
    kvw = kvw2 // 2
    nqt, nkt = D // LANES, kvw // LANES
    nb = seq // BLOCK
    nbt = tm // BLOCK
    assert nb % nbt == 0
    n = T // tm

    def body(dq_ref, dkc_ref, dkpa_ref, dkpb_ref, dvc_ref, dvpa_ref, dvpb_ref, qkv_ref, dres_ref, x_ref, g_ref, w_ref,
             qg_ref, kg_ref, dx_ref, dqkv_ref, h_ref, dg_ref, hg_ref):
        i = pl.program_id(0)

        @pl.when(i == 0)
        def _():
            dg_ref[...] = jnp.zeros_like(dg_ref)
            hg_ref[...] = jnp.zeros_like(hg_ref)

        lo = lax.broadcasted_iota(jnp.int32, (1, LANES), 1) < HEAD_DIM
        last = ((i + 1) * nbt) % nb == 0
        up = lambda ref: ref[...].astype(F32)

        def with_next(cur_ref, own_ref, next_ref):
            return up(cur_ref) + jnp.concatenate([up(own_ref)[BLOCK:, :], jnp.where(last, 0.0, up(next_ref))], axis=0)

        dkd = with_next(dkc_ref, dkpa_ref, dkpb_ref)
        dvd = with_next(dvc_ref, dvpa_ref, dvpb_ref)

        def undup(d, t):
            a, b = d[:, 2 * t * LANES : (2 * t + 1) * LANES], d[:, (2 * t + 1) * LANES : (2 * t + 2) * LANES]
            return jnp.where(lo, a + _swap_halves(a), b + _swap_halves(b))

        tiles = [qkv_ref[:, t * LANES : (t + 1) * LANES] for t in range(nqt + nkt)]
        gains = [qg_ref[...]] * nqt + [kg_ref[...]] * nkt
        dys = [dq_ref[:, t * LANES : (t + 1) * LANES] * SCALE for t in range(nqt)] + [undup(dkd, t) for t in range(nkt)]
        _, rs = _head_norm(tiles, gains, lo)
        dxs, dgs = _head_norm_bwd(tiles, rs, gains, dys, lo)
        for t in range(nqt + nkt):
            dqkv_ref[:, t * LANES : (t + 1) * LANES] = dxs[t].astype(BF16)
        for t in range(nkt):
            dqkv_ref[:, D + kvw + t * LANES : D + kvw + (t + 1) * LANES] = undup(dvd, t).astype(BF16)
        hg_ref[0:1, :] += functools.reduce(lambda a, b: a + b, dgs[:nqt])
        hg_ref[1:2, :] += functools.reduce(lambda a, b: a + b, dgs[nqt:])
        dh = _dot_nt(dqkv_ref[...], w_ref[...])
        xt = x_ref[...]
        r = _rms(xt)
        gn = g_ref[...]
        h_ref[...] = ((xt * r) * gn).astype(BF16)
        dx, dgn = _rms_bwd(xt, r, gn, dh)
        dg_ref[0:1, :] += dgn
        dx_ref[...] = dres_ref[...] + dx

    row = lambda i: (i, 0)
    nxt_a = pl.BlockSpec((tm, kvw2), row)
    nxt_b = pl.BlockSpec((BLOCK, kvw2), lambda i: (jnp.minimum((i + 1) * nbt, n * nbt - 1), 0))
    return pl.pallas_call(
        body,
        name="qkv_bwd",
        grid=(n,),
        in_specs=[
            pl.BlockSpec((tm, D), row),
            pl.BlockSpec((tm, kvw2), row),
            nxt_a,
            nxt_b,
            pl.BlockSpec((tm, kvw2), row),
            nxt_a,
            nxt_b,
            pl.BlockSpec((tm, D + kvw2), row),
            pl.BlockSpec((tm, D), row),
            pl.BlockSpec((tm, D), row),
            _resident((1, D)),
            _resident((D, D + kvw2)),
            _resident((1, LANES)),
            _resident((1, LANES)),
        ],
        out_specs=[
            pl.BlockSpec((tm, D), row),
            pl.BlockSpec((tm, D + kvw2), row),
            pl.BlockSpec((tm, D), row),
            pl.BlockSpec((8, D), lambda i: (0, 0)),
            pl.BlockSpec((8, LANES), lambda i: (0, 0)),
        ],
        out_shape=[
            jax.ShapeDtypeStruct((T, D), F32),
            jax.ShapeDtypeStruct((T, D + kvw2), BF16),
            jax.ShapeDtypeStruct((T, D), BF16),
            jax.ShapeDtypeStruct((8, D), F32),
            jax.ShapeDtypeStruct((8, LANES), F32),
        ],
        compiler_params=_params(1),
    )(dq, dkc, dkp, dkp, dvc, dvp, dvp, qkv, dres, x, gain, w_qkv, qg, kg)


def local_step(x, target, gains, w, *, seq, tm=256, tm_ffn=256, tm_conv=512, tk=2048, shards=None, ex=None):
    T, D = x.shape
    n_seq = T // seq
    nm, nf, qgain, kgain, sinks = gains
    H = D // HEAD_DIM
    tk, tk_long = min(tk, T), min(2 * tk, T)
    qg2, kg2 = jnp.tile(qgain, (1, 2)), jnp.tile(kgain, (1, 2))
    bias, sinkcol = _attn_tables(sinks, H)

    dist = shards is not None
    w = dict(w)

    plan = _Gather([shards["w_gu"][0], shards["w_d"][0]]) if dist else None
    (x1, bcx, y_conv, z16), got = conv_fwd(x, nm[0:1], w["w_in"], w["cw"], w["w_out"], seq=seq, tm=tm_conv, plan=plan)
    if dist:
        w["w_gu"], w["w_d"] = [cols_from_shards(got[0]), None], [got[1].reshape(-1, D), None]
    plan = _Gather([shards["w_qkv"], shards["w_o"], shards["w_gu"][1], shards["w_d"][1]]) if dist else None
    (x2, gu0), got = ffn_fwd(x1, nf[0:1], w["w_gu"][0], w["w_d"][0], tm=2 * tm_ffn, plan=plan)
    if dist:
        w["w_qkv"], w["w_o"] = cols_from_shards(got[0]), got[1].reshape(D, D)
        w["w_gu"][1], w["w_d"][1] = cols_from_shards(got[2]), got[3].reshape(-1, D)
    qkv, q16, kd, vd = qkv_proj(x2, nm[1:2], w["w_qkv"], qg2, kg2, tm=tm_conv)
    ao = attn_fwd(q16, kd, vd, bias, sinkcol, seq=seq, n_seq=n_seq)
    (x3, dx4, gu1, sse), _ = ffn_fwd(x2, nf[1:2], w["w_gu"][1], w["w_d"][1], tm=tm_ffn, attn=(ao, w["w_o"]), target=target)

    by_dest = lambda a: a.reshape(N_DEV, -1, a.shape[-1])
    gu_cols = 2 * MXU_TILE

    def send(name, *entries):
        if ex is None:
            return None
        items = [(a, False, key, (N_DEV,) + (() if layers is None else (layers,)) + a.shape[1:], layer)
                 for a, key, layer, layers in entries]
        return ex.start(items, name=name)

    (dx3, a16, dgu, h16, d16, dnf1, dx3_16, dao), _ = ffn_bwd(
        dx4, x3, nf[1:2], gu1, w["w_gu"][1], w["w_d"][1], tm=tm, w_o=w["w_o"])
    g_gu1 = shards_from_cols(wgrad(h16, dgu, name="wgrad_gu1", b_cols=gu_cols, flat=True, tk=T))
    g_d1 = by_dest(wgrad(a16, d16, name="wgrad_d1", a_cols=a16.shape[1] // 2, tk=tk))
    g_o = by_dest(wgrad(ao, dx3_16, name="wgrad_o", tk=tk))
    dq, dkc, dkp, dvc, dvp, dsinks = attn_bwd(q16, kd, vd, dao, bias, sinkcol, seq=seq, n_seq=n_seq)
    dx2, dqkv16, h16, dnm1, dgains = qkv_bwd(dq, dkc, dkp, dvc, dvp, qkv, dx3, x2, nm[1:2], w["w_qkv"], qg2, kg2, seq=seq, tm=tm_conv)
    g_qkv = shards_from_cols(wgrad(h16, dqkv16, name="wgrad_qkv", tk=tk)[0])
    tok = send("exchange_layer1", (g_gu1, "w_gu", 1, 2), (g_d1, "w_d", 1, 2), (g_o, "w_o", None, None),
               (g_qkv, "w_qkv", None, None))
    (dx1, a16, dgu, h16, d16, dnf0), _ = ffn_bwd(dx2, x1, nf[0:1], gu0, w["w_gu"][0], w["w_d"][0], tm=tm, after=tok)
    g_gu0 = shards_from_cols(wgrad(h16, dgu, name="wgrad_gu0", b_cols=gu_cols, flat=True, tk=T))
    tok = send("exchange_gu0", (g_gu0, "w_gu", 0, 2))
    g_d0 = by_dest(wgrad(a16, d16, name="wgrad_d0", a_cols=a16.shape[1] // 2, tk=tk, after=tok))
    tok = send("exchange_d0", (g_d0, "w_d", 0, 2))
    (gx, dbcx, h16, d16, dcw, dnm0), _ = conv_bwd(
        dx1, x, nm[0:1], bcx, y_conv, w["cw"], w["w_in"], w["w_out"], seq=seq, tm=tm_conv, after=tok)
    g_out = by_dest(wgrad(z16, d16, name="wgrad_out", tk=tk))
    g_cw = dcw[0:3].reshape(3, N_DEV, D // N_DEV).transpose(1, 0, 2)
    tok = send("exchange_out", (g_out, "w_out", None, None), (g_cw, "cw", None, None))
    g_in = wgrad(h16, dbcx, name="wgrad_in", b_cols=3 * D // N_DEV, group=2, a_resident=True, tk=T, after=tok)
    g = dict(w_in=g_in, cw=g_cw, w_out=g_out, w_o=g_o, w_qkv=g_qkv, w_gu=[g_gu0, g_gu1], w_d=[g_d0, g_d1])
    small = dict(nm0=dnm0, nm1=dnm1, nf0=dnf0, nf1=dnf1, gains=dgains, sinks=dsinks)
    return sse, gx, g, small


def _adamw_math(g, w, m, v):
    m = ADAM_B1 * m + (1.0 - ADAM_B1) * g
    v = ADAM_B2 * v + (1.0 - ADAM_B2) * (g * g)
    m_hat = m / (1.0 - ADAM_B1 ** ADAM_STEP)
    v_hat = v / (1.0 - ADAM_B2 ** ADAM_STEP)
    delta = -ADAM_LR * (m_hat / (jnp.sqrt(v_hat) + ADAM_EPS) + ADAM_WD * w)
    return delta, m, v


def adamw(parts, owns, w, m, v, *, name, after=None):
    n, LR, C = parts.shape
    L = len(owns)
    R = LR // L
    tr = R
    for cand in (256, 128, 88, 64, 32, 16, 8):
        if R > cand and R % cand == 0:
            tr = cand
            break
    per_layer = R // tr
    extra = [] if after is None else [after]

    def body(me_ref, p_ref, *rest):
        own_refs, (w_ref, m_ref, v_ref) = rest[:L], rest[L : L + 3]
        g_ref, d_ref, mo_ref, vo_ref = rest[L + 3 + len(extra) :]
        layer = pl.program_id(0) // per_layer
        mine = own_refs[0][...].astype(F32)
        for j in range(1, L):
            mine = jnp.where(layer == j, own_refs[j][...].astype(F32), mine)
        g = None
        for s in range(n):
            share = jnp.where(me_ref[0] == s, mine, p_ref[s].astype(F32))
            g = share if g is None else g + share
        g_ref[...] = g
        d_ref[...], mo_ref[...], vo_ref[...] = _adamw_math(g, w_ref[...], m_ref[...], v_ref[...])

    blk = pl.BlockSpec((tr, C), lambda i, me: (i, 0))
    own_specs = [pl.BlockSpec((None, tr, C), lambda i, me: (me[0], i % per_layer, 0)) if o.ndim == 3
                 else pl.BlockSpec((tr, C), lambda i, me: (i % per_layer, 0)) for o in owns]
    me = (4 * lax.axis_index("x") + 2 * lax.axis_index("y") + lax.axis_index("c")).astype(jnp.int32).reshape(1)
    return pl.pallas_call(
        body,
        name=name,
        grid_spec=pltpu.PrefetchScalarGridSpec(
            num_scalar_prefetch=1,
            grid=(LR // tr,),
            in_specs=[pl.BlockSpec((n, tr, C), lambda i, me: (0, i, 0))] + own_specs + [blk, blk, blk] + _any_specs(len(extra)),
            out_specs=[blk] * 4,
        ),
        out_shape=[jax.ShapeDtypeStruct((LR, C), F32)] * 4,
        compiler_params=_params(1),
    )(me, parts, *owns, w, m, v, *extra)


def pack_small(small, sse, D):
    W = max(D, 2 * LANES)

    def body(nm0, nm1, nf0, nf1, gains, sinks, sse_ref, o_ref):
        o_ref[...] = jnp.zeros_like(o_ref)
        o_ref[0:1, :D] = nm0[0:1, :]
        o_ref[1:2, :D] = nm1[0:1, :]
        o_ref[2:3, :D] = nf0[0:1, :]
        o_ref[3:4, :D] = nf1[0:1, :]
        gq = gains[0:1, :] + pltpu.roll(gains[0:1, :], HEAD_DIM, 1)
        gk = gains[1:2, :] + pltpu.roll(gains[1:2, :], HEAD_DIM, 1)
        lane = lax.broadcasted_iota(jnp.int32, (1, LANES), 1)
        o_ref[4:5, :LANES] = jnp.where(lane < HEAD_DIM, gq, gk)
        o_ref[4:5, LANES : 2 * LANES] = sinks[0:1, :]
        o_ref[5:6, :LANES] = sse_ref[0:1, :] * (0.5 / D)

    return pl.pallas_call(
        body,
        name="pack_small",
        out_shape=jax.ShapeDtypeStruct((8, W), F32),
    )(small["nm0"], small["nm1"], small["nf0"], small["nf1"], small["gains"], small["sinks"], sse)


def _pack_small_params(nm, nf, qg, kg, sk, D):
    W = max(D, 2 * LANES)
    row4 = jnp.concatenate([qg.reshape(-1), kg.reshape(-1), jnp.zeros((LANES - 2 * HEAD_DIM,), F32), sk.reshape(-1)])
    row4 = jnp.pad(row4, (0, W - row4.shape[0]))
    rows = [jnp.pad(r, (0, W - D)) for r in (nm[0], nm[1], nf[0], nf[1])] + [row4]
    return jnp.concatenate([jnp.stack(rows), jnp.zeros((3, W), F32)], axis=0)


def _unpack_small(a, D, H):
    nm = a[0:2, :D]
    nf = a[2:4, :D]
    qg = a[4:5, 0:HEAD_DIM]
    kg = a[4:5, HEAD_DIM : 2 * HEAD_DIM]
    sk = a[4:5, LANES : LANES + H]
    return qg, kg, sk, nm, nf


def kernel(x, conv_w_in, conv_w, conv_w_out, attn_w_qkv, attn_q_gain, attn_k_gain, attn_sinks, attn_w_o, norm_mixer, norm_ffn, ffn_w_gate_up, ffn_w_down, loss_target, m_conv_w_in, m_conv_w, m_conv_w_out, m_attn_w_qkv, m_attn_q_gain, m_attn_k_gain, m_attn_sinks, m_attn_w_o, m_norm_mixer, m_norm_ffn, m_ffn_w_gate_up, m_ffn_w_down, v_conv_w_in, v_conv_w, v_conv_w_out, v_attn_w_qkv, v_attn_q_gain, v_attn_k_gain, v_attn_sinks, v_attn_w_o, v_norm_mixer, v_norm_ffn, v_ffn_w_gate_up, v_ffn_w_down):
    n_seq, seq, D = x.shape
    T = n_seq * seq
    H = D // HEAD_DIM
    L = ffn_w_gate_up.shape[0]

    full = run_plan(_Gather([conv_w_in[0].astype(BF16), conv_w[0], conv_w_out[0].astype(BF16)]), name="gather_conv_weights")
    w = dict(w_in=cols_from_shards(full[0]), cw=full[1].transpose(1, 0, 2).reshape(3, D),
             w_out=full[2].reshape(D, D))
    shards = dict(w_gu=[ffn_w_gate_up[l].astype(BF16) for l in range(L)], w_d=[ffn_w_down[l].astype(BF16) for l in range(L)],
                  w_qkv=attn_w_qkv[0].astype(BF16), w_o=attn_w_o[0].astype(BF16))
    gains = (norm_mixer, norm_ffn, attn_q_gain, attn_k_gain, attn_sinks)
    ex = Exchange()
    sse, gx, g, small = local_step(x.reshape(T, D), loss_target.reshape(T, D), gains, w, seq=seq, shards=shards, ex=ex)
    zones, own = ex.wait([g["w_in"]], name="exchange_wait")

    packed = pack_small(small, sse, D)
    token = ex.start([(g["w_in"], False, "w_in", g["w_in"].shape, None),
                      (packed, True, "small", (N_DEV,) + packed.shape, None)], name="exchange_last")

    def flat(a):
        return a.reshape(-1, a.shape[-1])

    big = [conv_w_in, conv_w, conv_w_out, attn_w_qkv, attn_w_o, ffn_w_gate_up, ffn_w_down]
    big_m = [m_conv_w_in, m_conv_w, m_conv_w_out, m_attn_w_qkv, m_attn_w_o, m_ffn_w_gate_up, m_ffn_w_down]
    big_v = [v_conv_w_in, v_conv_w, v_conv_w_out, v_attn_w_qkv, v_attn_w_o, v_ffn_w_gate_up, v_ffn_w_down]
    keys = ["w_in", "cw", "w_out", "w_qkv", "w_o", "w_gu", "w_d"]

    def update(b, zones, own, after=None):
        zone = zones[keys[b]]
        parts = zone.reshape(N_DEV, -1, zone.shape[-1])
        layers = [None] if zone.ndim == 3 else range(zone.shape[1])
        outs = adamw(parts, [own[(keys[b], l)] for l in layers], flat(big[b]), flat(big_m[b]), flat(big_v[b]),
                     name="adamw_" + keys[b], after=after)
        return [o.reshape(big[b].shape) for o in outs]

    res = [None] + [update(b, zones, own, after=token) for b in range(1, 7)]
    zones, own = ex.wait([r[0] for r in res[1:]], name="exchange_last_wait")
    res[0] = update(0, zones, own)
    sw = _pack_small_params(norm_mixer, norm_ffn, attn_q_gain, attn_k_gain, attn_sinks, D)
    sm = _pack_small_params(m_norm_mixer, m_norm_ffn, m_attn_q_gain, m_attn_k_gain, m_attn_sinks, D)
    sv = _pack_small_params(v_norm_mixer, v_norm_ffn, v_attn_q_gain, v_attn_k_gain, v_attn_sinks, D)
    souts = adamw(zones["small"], [own[("small", None)]], sw, sm, sv, name="adamw_small")
    sres = [_unpack_small(o, D, H) for o in souts]
    loss = souts[0][5, 0]

    def ordered(i):
        r, s = [r[i] for r in res], sres[i]
        return [r[0], r[1], r[2], r[3], s[0], s[1], s[2], r[4], s[3], s[4], r[5], r[6]]

    return (loss, gx.reshape(n_seq, seq, D), *ordered(0), *ordered(1), *ordered(2), *ordered(3))
```

```python
import functools
import math

import jax
import jax.numpy as jnp
from jax import lax
from jax.experimental import pallas as pl
from jax.experimental.pallas import tpu as pltpu

F32 = jnp.float32
BF16 = jnp.bfloat16

EPS = 1e-6
HEAD_DIM = 64
N_KV_HEADS = 4
BLOCK = 128
LANES = 128
N_DEV = 8
NEG = -1e30
SCALE = 1.0 / math.sqrt(HEAD_DIM)

ADAM_LR = 0.001
ADAM_B1 = 0.9
ADAM_B2 = 0.999
ADAM_EPS = 1e-08
ADAM_WD = 0.01
ADAM_STEP = 10

V7X_VMEM_BYTES = 64 * 1024 * 1024
VMEM_LIMIT = V7X_VMEM_BYTES - 2 * 1024 * 1024
MESH = pl.DeviceIdType.MESH

_NT = (((1,), (1,)), ((), ()))
_TN = (((0,), (0,)), ((), ()))


def _params(n_grid):
    return pltpu.CompilerParams(dimension_semantics=("arbitrary",) * n_grid, vmem_limit_bytes=VMEM_LIMIT)


def _resident(shape):
    nd = len(shape)
    return pl.BlockSpec(shape, lambda *_: (0,) * nd, pipeline_mode=pl.Buffered(1))


def _rms(x):
    return lax.rsqrt(jnp.mean(x * x, axis=-1, keepdims=True) + EPS)


def _rms_bwd(x, r, gain, dh):
    xn = x * r
    dxn = dh * gain
    dx = r * (dxn - xn * jnp.mean(dxn * xn, axis=-1, keepdims=True))
    return dx, jnp.sum(dh * xn, axis=0, keepdims=True)


def _dot(a, b):
    return jnp.dot(a, b, preferred_element_type=F32)


def _dot_nt(a, b):
    return lax.dot_general(a, b, _NT, preferred_element_type=F32)


def _dot_tn(a, b):
    return lax.dot_general(a, b, _TN, preferred_element_type=F32)


def _place():
    return lax.axis_index("x"), lax.axis_index("y"), lax.axis_index("c")


def _flip(v, bit):
    return 1 - v if bit else v


def _slot(px, py, pc):
    return 4 * px + 2 * py + pc


class _Gather:
    def __init__(self, shards):
        nt = len(shards)
        self.nt = nt
        self.inputs = list(shards)
        self.out_shapes = [jax.ShapeDtypeStruct((N_DEV,) + s.shape, s.dtype) for s in shards]
        self.scratch = [pltpu.SemaphoreType.DMA((nt, 10)), pltpu.SemaphoreType.DMA((nt, 10)), pltpu.SemaphoreType.DMA((nt,))]
        self.aliases = {}
        self.split = []
        for s in shards:
            rows, tile = s.shape[0], 16 if s.dtype == BF16 else 8
            self.split.append(rows // 2 if rows % (2 * tile) == 0 else rows)

    def phases(self, total):
        assert total >= 8
        return [(0, self.start), (total // 2, self.second), (total - 3, self.forward), (total - 1, self.finish)]

    def _copies(self, ins, outs, sems):
        send_sems, recv_sems, loc_sems = sems
        x, y, c = _place()
        xn, yn, sib = (1 - x, y, c), (x, 1 - y, c), (x, y, 1 - c)
        i_me, i_xn, i_yn, i_dn = _slot(x, y, c), _slot(1 - x, y, c), _slot(x, 1 - y, c), _slot(1 - x, 1 - y, c)
        j_me, j_xn, j_yn, j_dn = _slot(x, y, 1 - c), _slot(1 - x, y, 1 - c), _slot(x, 1 - y, 1 - c), _slot(1 - x, 1 - y, 1 - c)
        local, start, need1, second, need2, forward, need3 = [], [], [], [], [], [], []
        for t in range(self.nt):
            o, rows, h = outs[t], self.inputs[t].shape[0], self.split[t]
            lo = pl.ds(0, h)
            hi = pl.ds(h, rows - h) if h < rows else None

            def rc(k, src, dst, to, t=t):
                return pltpu.make_async_remote_copy(
                    src_ref=src, dst_ref=dst, send_sem=send_sems.at[t, k], recv_sem=recv_sems.at[t, k], device_id=to,
                    device_id_type=MESH)

            def landed(k, slot, part, frm):
                ref = o.at[slot] if part is None else o.at[slot, part]
                return rc(k, ref, ref, frm)

            local.append(pltpu.make_async_copy(ins[t], o.at[i_me], loc_sems.at[t]))
            start += [rc(0, ins[t], o.at[i_me], sib), rc(1, ins[t].at[lo], o.at[i_me, lo], xn),
                      rc(4, ins[t].at[lo], o.at[i_me, lo], yn)]
            need1.append(landed(1, i_xn, lo, xn))
            second.append(rc(3, o.at[i_xn, lo], o.at[i_xn, lo], yn))
            need2 += [landed(4, i_yn, lo, yn), landed(3, i_dn, lo, yn)]
            if hi is not None:
                start += [rc(2, ins[t].at[hi], o.at[i_me, hi], yn), rc(6, ins[t].at[hi], o.at[i_me, hi], xn)]
                need1.append(landed(2, i_yn, hi, yn))
                second.append(rc(5, o.at[i_yn, hi], o.at[i_yn, hi], xn))
                need2 += [landed(6, i_xn, hi, xn), landed(5, i_dn, hi, xn)]
            forward += [rc(7, o.at[i_xn], o.at[i_xn], sib), rc(8, o.at[i_yn], o.at[i_yn], sib), rc(9, o.at[i_dn], o.at[i_dn], sib)]
            need3 += [landed(0, j_me, None, sib), landed(7, j_xn, None, sib), landed(8, j_yn, None, sib), landed(9, j_dn, None, sib)]
        return local, start, need1, second, need2, forward, need3

    def start(self, ins, outs, sems):
        local, start, *_ = self._copies(ins, outs, sems)
        for cp in local + start:
            cp.start()

    def second(self, ins, outs, sems):
        _, _, need1, second, *_ = self._copies(ins, outs, sems)
        for cp in need1:
            cp.wait_recv()
        for cp in second:
            cp.start()

    def forward(self, ins, outs, sems):
        _, _, _, _, need2, forward, _ = self._copies(ins, outs, sems)
        for cp in need2:
            cp.wait_recv()
        for cp in forward:
            cp.start()

    def finish(self, ins, outs, sems):
        local, start, _, second, _, forward, need3 = self._copies(ins, outs, sems)
        for cp in need3:
            cp.wait_recv()
        for cp in start + second + forward:
            cp.wait_send()
        for cp in local:
            cp.wait()


def _any_specs(n):
    return [pl.BlockSpec(memory_space=pl.ANY)] * n


def run_plan(plan, *, name):
    def body(*refs):
        n_in, n_out = len(plan.inputs), len(plan.out_shapes)
        ins, outs, sems = refs[:n_in], refs[n_in : n_in + n_out], refs[n_in + n_out :]
        for _, phase in plan.phases(8):
            phase(ins, outs, sems)

    return pl.pallas_call(
        body,
        name=name,
        in_specs=_any_specs(len(plan.inputs)),
        out_specs=_any_specs(len(plan.out_shapes)),
        out_shape=plan.out_shapes,
        scratch_shapes=plan.scratch,
        input_output_aliases=plan.aliases,
    )(*plan.inputs)


_HBM = pl.BlockSpec(memory_space=pltpu.HBM)
_SEM = pl.BlockSpec(memory_space=pltpu.SEMAPHORE)
_DATAFLOW = pltpu.SideEffectType.DATAFLOW_SIDE_EFFECTING


class Exchange:
    def __init__(self):
        self.zones = {}
        self.pending = []
        self.sources = []

    def start(self, items, *, name):
        nt = len(items)
        keys = list(dict.fromkeys(it[2] for it in items))
        for a, _, key, shape, _ in items:
            if key not in self.zones:
                self.zones[key] = lax.empty(shape, a.dtype)
        nz = len(keys)

        def body(*refs):
            ins, zones, sems, token = refs[:nt], refs[nt : nt + nz], refs[nt + nz : nt + nz + 2 * nt], refs[-1]
            x, y, c = _place()
            me = _slot(x, y, c)
            for k in range(1, N_DEV):
                px, py, pc = _flip(x, (k >> 2) & 1), _flip(y, (k >> 1) & 1), _flip(c, k & 1)
                for t, (_, whole, key, _, layer) in enumerate(items):
                    zone = zones[keys.index(key)]
                    pltpu.make_async_remote_copy(
                        src_ref=ins[t] if whole else ins[t].at[_slot(px, py, pc)],
                        dst_ref=zone.at[me] if layer is None else zone.at[me, layer],
                        send_sem=sems[2 * t], recv_sem=sems[2 * t + 1], device_id=(px, py, pc), device_id_type=MESH).start()
            token[...] = jnp.zeros_like(token)

        bufs = [pltpu.with_memory_space_constraint(b, pltpu.HBM) for b in [it[0] for it in items] + [self.zones[k] for k in keys]]
        outs = pl.pallas_call(
            body,
            name=name,
            in_specs=[_HBM] * (nt + nz),
            out_specs=[_SEM] * (2 * nt) + [_HBM] * (nt + nz) + [pl.BlockSpec(memory_space=pltpu.VMEM)],
            out_shape=[pltpu.SemaphoreType.DMA(())] * (2 * nt) + [pltpu.HBM(b.shape, b.dtype) for b in bufs]
            + [jax.ShapeDtypeStruct((8, LANES), F32)],
            input_output_aliases={i: 2 * nt + i for i in range(nt + nz)},
            compiler_params=pltpu.CompilerParams(has_side_effects=_DATAFLOW),
        )(*bufs)
        for t, (_, _, key, _, layer) in enumerate(items):
            self.pending.append((outs[2 * t], outs[2 * t + 1], key, layer))
        self.sources += [((it[2], it[4]), a) for it, a in zip(items, outs[2 * nt : 3 * nt])]
        for i, key in enumerate(keys):
            self.zones[key] = outs[3 * nt + i]
        return outs[-1]

    def wait(self, after, *, name):
        pending, keys = self.pending, list(self.zones)
        names, sources = [n for n, _ in self.sources], [a for _, a in self.sources]
        ns, nz, npend = len(sources), len(keys), len(pending)
        self.pending, self.sources = [], []

        def body(*refs):
            zones, sems = refs[ns : ns + nz], refs[ns + nz : ns + nz + 2 * npend]
            x, y, c = _place()
            for i, (_, _, key, layer) in enumerate(pending):
                zone = zones[keys.index(key)]
                rows = pl.ds(0, N_DEV - 1)
                seven = zone.at[rows] if layer is None else zone.at[rows, layer]
                pltpu.make_async_remote_copy(
                    src_ref=seven, dst_ref=seven, send_sem=sems[2 * i], recv_sem=sems[2 * i + 1],
                    device_id=(x, y, c), device_id_type=MESH).wait()

        bufs = list(sources) + [self.zones[k] for k in keys]
        flat_sems = [s for p in pending for s in p[:2]]
        outs = pl.pallas_call(
            body,
            name=name,
            in_specs=[_HBM] * (ns + nz) + [_SEM] * (2 * npend) + _any_specs(len(after)),
            out_specs=[_HBM] * (ns + nz),
            out_shape=[pltpu.HBM(b.shape, b.dtype) for b in bufs],
            input_output_aliases={i: i for i in range(ns + nz)},
            compiler_params=pltpu.CompilerParams(has_side_effects=_DATAFLOW),
        )(*bufs, *flat_sems, *after)
        self.zones = {}
        return dict(zip(keys, outs[ns:])), dict(zip(names, outs[:ns]))


def _call(body, *, name, grid, in_specs, out_specs, out_shape, args, scratch=(), plan=None, after=None):
    if after is not None:
        inner, n_real = body, len(in_specs)
        body = lambda *refs: inner(*refs[:n_real], *refs[n_real + 1 :])
        in_specs, args = list(in_specs) + _any_specs(1), list(args) + [after]
    n_in, n_out, n_scr = len(in_specs), len(out_specs), len(scratch)
    if plan is None:
        outs = pl.pallas_call(
            body, name=name, grid=grid, in_specs=in_specs, out_specs=out_specs, out_shape=out_shape,
            scratch_shapes=list(scratch), compiler_params=_params(len(grid)))(*args)
        return outs, None
    c_in, c_out = len(plan.inputs), len(plan.out_shapes)
    phases = plan.phases(math.prod(grid))

    def full(*refs):
        a, refs = refs[:n_in], refs[n_in:]
        ci, refs = refs[:c_in], refs[c_in:]
        o, refs = refs[:n_out], refs[n_out:]
        co, refs = refs[:c_out], refs[c_out:]
        s, cs = refs[:n_scr], refs[n_scr:]
        step = pl.program_id(0)
        for d in range(1, len(grid)):
            step = step * grid[d] + pl.program_id(d)
        for at, phase in phases:
            if at == 0:
                pl.when(step == 0)(functools.partial(phase, ci, co, cs))
        body(*a, *o, *s)
        for at, phase in phases:
            if at > 0:
                pl.when(step == at)(functools.partial(phase, ci, co, cs))

    outs = pl.pallas_call(
        full,
        name=name,
        grid=grid,
        in_specs=list(in_specs) + _any_specs(c_in),
        out_specs=list(out_specs) + _any_specs(c_out),
        out_shape=list(out_shape) + plan.out_shapes,
        scratch_shapes=list(scratch) + plan.scratch,
        input_output_aliases={n_in + i: n_out + t for i, t in plan.aliases.items()},
        compiler_params=_params(len(grid)),
    )(*args, *plan.inputs)
    return outs[:n_out], outs[n_out:]


def _row_tile(R):
    return 256 if R % 256 == 0 else R


def cols_from_shards(a):
    n, R, C = a.shape
    tr = _row_tile(R)

    def body(i_ref, o_ref):
        for s in range(n):
            o_ref[:, s * C : (s + 1) * C] = i_ref[s]

    return pl.pallas_call(
        body,
        name="cols_from_shards",
        grid=(R // tr,),
        in_specs=[pl.BlockSpec((n, tr, C), lambda i: (0, i, 0))],
        out_specs=pl.BlockSpec((tr, n * C), lambda i: (i, 0)),
        out_shape=jax.ShapeDtypeStruct((R, n * C), a.dtype),
        compiler_params=_params(1),
    )(a)


def shards_from_cols(a):
    R, W = a.shape
    C = W // N_DEV
    tr = _row_tile(R)

    def body(i_ref, o_ref):
        for s in range(N_DEV):
            o_ref[s] = i_ref[:, s * C : (s + 1) * C]

    return pl.pallas_call(
        body,
        name="shards_from_cols",
        grid=(R // tr,),
        in_specs=[pl.BlockSpec((tr, W), lambda i: (i, 0))],
        out_specs=pl.BlockSpec((N_DEV, tr, C), lambda i: (0, i, 0)),
        out_shape=jax.ShapeDtypeStruct((N_DEV, R, C), a.dtype),
        compiler_params=_params(1),
    )(a)


def _shift_down(u, prev8, row, n):
    out = pltpu.roll(u, n, 0)
    for k in range(n):
        out = jnp.where(row == k, prev8[8 - n + k : 8 - n + k + 1, :], out)
    return out


def _shift_up(u, next8, row, n, tm):
    out = pltpu.roll(u, tm - n, 0)
    for k in range(n):
        out = jnp.where(row == tm - n + k, next8[k : k + 1, :], out)
    return out


def conv_fwd(x, gain, w_in, cw, w_out, *, seq, tm, plan=None):
    T, D = x.shape
    tps = seq // tm

    def body(x_ref, g_ref, win_ref, cw_ref, wout_ref, x1_ref, bcx_ref, y_ref, z_ref, carry_ref):
        i = pl.program_id(0)

        @pl.when(i % tps == 0)
        def _():
            carry_ref[...] = jnp.zeros_like(carry_ref)

        xt = x_ref[...]
        h = ((xt * _rms(xt)) * g_ref[...]).astype(BF16)
        bcx = _dot(h, win_ref[...])
        bcx_ref[...] = bcx.astype(BF16)
        b, c, xv = bcx[:, :D], bcx[:, D : 2 * D], bcx[:, 2 * D :]
        u = b * xv
        row = lax.broadcasted_iota(jnp.int32, u.shape, 0)
        prev = carry_ref[...]
        u1 = _shift_down(u, prev, row, 1)
        u2 = _shift_down(u, prev, row, 2)
        carry_ref[...] = u[tm - 8 :, :]
        cwv = cw_ref[...]
        y = cwv[0:1, :] * u2 + cwv[1:2, :] * u1 + cwv[2:3, :] * u
        y_ref[...] = y
        z = (c * y).astype(BF16)
        z_ref[...] = z
        x1_ref[...] = xt + _dot(z, wout_ref[...])

    tile = pl.BlockSpec((tm, D), lambda i: (i, 0))
    return _call(
        body,
        plan=plan,
        args=(x, gain, w_in, cw, w_out),
        name="conv_fwd",
        grid=(T // tm,),
        in_specs=[
            pl.BlockSpec((tm, D), lambda i: (i, 0)),
            _resident((1, D)),
            _resident((D, 3 * D)),
            _resident((3, D)),
            _resident((D, D)),
        ],
        out_specs=[tile, pl.BlockSpec((tm, 3 * D), lambda i: (i, 0)), tile, tile],
        out_shape=[jax.ShapeDtypeStruct((T, D), F32), jax.ShapeDtypeStruct((T, 3 * D), BF16),
                   jax.ShapeDtypeStruct((T, D), F32), jax.ShapeDtypeStruct((T, D), BF16)],
        scratch=[pltpu.VMEM((8, D), F32)],
    )


def conv_bwd(dx1, x, gain, bcx, y, cw, w_in, w_out, *, seq, tm, after=None):
    T, D = x.shape
    n = T // tm
    tps = seq // tm

    def body(d_ref, x_ref, g_ref, bcx_ref, y_ref, cw_ref, win_ref, wout_ref,
             gx_ref, dbcx_ref, h_ref, d16_ref, dcw_ref, dg_ref, carry_ref):
        i = pl.program_id(0)
        t = n - 1 - i

        @pl.when(i == 0)
        def _():
            dcw_ref[...] = jnp.zeros_like(dcw_ref)
            dg_ref[...] = jnp.zeros_like(dg_ref)

        @pl.when(t % tps == tps - 1)
        def _():
            carry_ref[...] = jnp.zeros_like(carry_ref)

        d = d_ref[...]
        d16 = d.astype(BF16)
        d16_ref[...] = d16
        dz = _dot_nt(d16, wout_ref[...])
        bcx = bcx_ref[...].astype(F32)
        b, c, xv = bcx[:, :D], bcx[:, D : 2 * D], bcx[:, 2 * D :]
        u = b * xv
        row = lax.broadcasted_iota(jnp.int32, u.shape, 0)
        cwv = cw_ref[...]
        dc = dz * y_ref[...]
        dy = dz * c
        nxt = carry_ref[...]
        dy1 = _shift_up(dy, nxt, row, 1, tm)
        dy2 = _shift_up(dy, nxt, row, 2, tm)
        carry_ref[...] = dy[0:8, :]
        dcw_ref[0:1, :] += jnp.sum(dy2 * u, axis=0, keepdims=True)
        dcw_ref[1:2, :] += jnp.sum(dy1 * u, axis=0, keepdims=True)
        dcw_ref[2:3, :] += jnp.sum(dy * u, axis=0, keepdims=True)
        du = cwv[2:3, :] * dy + cwv[1:2, :] * dy1 + cwv[0:1, :] * dy2
        dbcx_ref[:, :D] = (du * xv).astype(BF16)
        dbcx_ref[:, D : 2 * D] = dc.astype(BF16)
        dbcx_ref[:, 2 * D :] = (du * b).astype(BF16)
        dh = _dot_nt(dbcx_ref[...], win_ref[...])
        xt = x_ref[...]
        r = _rms(xt)
        gn = g_ref[...]
        h_ref[...] = ((xt * r) * gn).astype(BF16)
        dx, dgn = _rms_bwd(xt, r, gn, dh)
        dg_ref[0:1, :] += dgn
        gx_ref[...] = d + dx

    rev = lambda i: (n - 1 - i, 0)
    return _call(
        body,
        after=after,
        args=(dx1, x, gain, bcx, y, cw, w_in, w_out),
        name="conv_bwd",
        grid=(n,),
        in_specs=[
            pl.BlockSpec((tm, D), rev),
            pl.BlockSpec((tm, D), rev),
            _resident((1, D)),
            pl.BlockSpec((tm, 3 * D), rev),
            pl.BlockSpec((tm, D), rev),
            _resident((3, D)),
            _resident((D, 3 * D)),
            _resident((D, D)),
        ],
        out_specs=[
            pl.BlockSpec((tm, D), rev),
            pl.BlockSpec((tm, 3 * D), rev),
            pl.BlockSpec((tm, D), rev),
            pl.BlockSpec((tm, D), rev),
            pl.BlockSpec((8, D), lambda i: (0, 0)),
            pl.BlockSpec((8, D), lambda i: (0, 0)),
        ],
        out_shape=[
            jax.ShapeDtypeStruct((T, D), F32),
            jax.ShapeDtypeStruct((T, 3 * D), BF16),
            jax.ShapeDtypeStruct((T, D), BF16),
            jax.ShapeDtypeStruct((T, D), BF16),
            jax.ShapeDtypeStruct((8, D), F32),
            jax.ShapeDtypeStruct((8, D), F32),
        ],
        scratch=[pltpu.VMEM((8, D), F32)],
    )


MXU_TILE = 256
FFN_CHUNK = 4 * MXU_TILE


def _sigmoid(g):
    return 1.0 / (1.0 + jnp.exp(-g))


def _ffn_chunks(F):
    assert F % MXU_TILE == 0
    return [(s, min(FFN_CHUNK, F - s)) for s in range(0, F, FFN_CHUNK)]


def ffn_fwd(x, gain, w_gu, w_d, *, tm, plan=None, attn=None, target=None):
    T, D = x.shape
    F = w_d.shape[0]
    row = lambda i: (i, 0)
    tile = pl.BlockSpec((tm, D), row)

    def body(*refs):
        refs = list(refs)
        x_ref, g_ref, wgu_ref, wd_ref = refs[:4]
        del refs[:4]
        if attn is not None:
            ao_ref, wo_ref = refs[:2]
            del refs[:2]
        if target is not None:
            t_ref = refs.pop(0)
        if attn is not None:
            xin_ref = refs.pop(0)
        xo_ref, gu_ref = refs[:2]
        xt = x_ref[...]
        if attn is not None:
            xt = xt + _dot(ao_ref[...], wo_ref[...])
            xin_ref[...] = xt
        h = ((xt * _rms(xt)) * g_ref[...]).astype(BF16)
        acc = xt
        for s, n in _ffn_chunks(F):
            g = _dot(h, wgu_ref[:, s : s + n])
            u = _dot(h, wgu_ref[:, F + s : F + s + n])
            gu_ref[:, s : s + n] = g
            gu_ref[:, F + s : F + s + n] = u
            a = ((g * _sigmoid(g)) * u).astype(BF16)
            acc = acc + _dot(a, wd_ref[s : s + n, :])
        if target is None:
            xo_ref[...] = acc
        else:
            s_ref = refs[2]

            @pl.when(pl.program_id(0) == 0)
            def _():
                s_ref[...] = jnp.zeros_like(s_ref)

            e = acc - t_ref[...]
            xo_ref[...] = e * (1.0 / D)
            s_ref[...] += jnp.sum(jnp.sum(e * e, axis=-1, keepdims=True), axis=0, keepdims=True)

    args = [x, gain, w_gu, w_d]
    in_specs = [tile, _resident((1, D)), _resident((D, 2 * F)), _resident((F, D))]
    out_specs = [tile, pl.BlockSpec((tm, 2 * F), row)]
    out_shape = [jax.ShapeDtypeStruct((T, D), F32), jax.ShapeDtypeStruct((T, 2 * F), F32)]
    if attn is not None:
        args += list(attn)
        in_specs += [pl.BlockSpec((tm, attn[0].shape[1]), row), _resident(attn[1].shape)]
        out_specs.insert(0, tile)
        out_shape.insert(0, jax.ShapeDtypeStruct((T, D), F32))
    if target is not None:
        args.append(target)
        in_specs.append(tile)
        out_specs.append(pl.BlockSpec((8, LANES), lambda i: (0, 0)))
        out_shape.append(jax.ShapeDtypeStruct((8, LANES), F32))
    return _call(body, plan=plan, args=args, name="ffn_fwd", grid=(T // tm,), in_specs=in_specs, out_specs=out_specs,
                 out_shape=out_shape)


def ffn_bwd(dxo, x, gain, gu, w_gu, w_d, *, tm, after=None, w_o=None):
    T, D = x.shape
    F = w_d.shape[0]

    def body(d_ref, x_ref, g_ref, gu_ref, wgu_ref, wd_ref, *rest):
        if w_o is not None:
            wo_ref, rest = rest[0], rest[1:]
        dx_ref, a_ref, dgu_ref, h_ref, d16_ref, dg_ref = rest[:6]

        @pl.when(pl.program_id(0) == 0)
        def _():
            dg_ref[...] = jnp.zeros_like(dg_ref)

        d = d_ref[...]
        d16 = d.astype(BF16)
        d16_ref[...] = d16
        dh = jnp.zeros((tm, D), F32)
        for c0, n in _ffn_chunks(F):
            g = gu_ref[:, c0 : c0 + n]
            u = gu_ref[:, F + c0 : F + c0 + n]
            da = _dot_nt(d16, wd_ref[c0 : c0 + n, :])
            s = _sigmoid(g)
            sg = g * s
            a_ref[:, c0 : c0 + n] = (sg * u).astype(BF16)
            dg16 = (da * u * (s + sg * (1.0 - s))).astype(BF16)
            du16 = (da * sg).astype(BF16)
            dgu_ref[:, c0 : c0 + n] = dg16
            dgu_ref[:, F + c0 : F + c0 + n] = du16
            dh = dh + _dot_nt(dg16, wgu_ref[:, c0 : c0 + n]) + _dot_nt(du16, wgu_ref[:, F + c0 : F + c0 + n])
        xt = x_ref[...]
        r = _rms(xt)
        gn = g_ref[...]
        h_ref[...] = ((xt * r) * gn).astype(BF16)
        dx, dgn = _rms_bwd(xt, r, gn, dh)
        dg_ref[0:1, :] += dgn
        dxi = d + dx
        dx_ref[...] = dxi
        if w_o is not None:
            dxi16_ref, dao_ref = rest[6:8]
            dxi16 = dxi.astype(BF16)
            dxi16_ref[...] = dxi16
            dao_ref[...] = _dot_nt(dxi16, wo_ref[...]).astype(BF16)

    tile = pl.BlockSpec((tm, D), lambda i: (i, 0))
    args = [dxo, x, gain, gu, w_gu, w_d]
    wide = lambda n: pl.BlockSpec((tm, n), lambda i: (i, 0))
    in_specs = [tile, tile, _resident((1, D)), wide(2 * F), _resident((D, 2 * F)), _resident((F, D))]
    out_specs = [tile, wide(F), wide(2 * F), tile, tile, pl.BlockSpec((8, D), lambda i: (0, 0))]
    out_shape = [
        jax.ShapeDtypeStruct((T, D), F32),
        jax.ShapeDtypeStruct((T, F), BF16),
        jax.ShapeDtypeStruct((T, 2 * F), BF16),
        jax.ShapeDtypeStruct((T, D), BF16),
        jax.ShapeDtypeStruct((T, D), BF16),
        jax.ShapeDtypeStruct((8, D), F32),
    ]
    if w_o is not None:
        args.append(w_o)
        in_specs.append(_resident(w_o.shape))
        out_specs += [tile, pl.BlockSpec((tm, w_o.shape[0]), lambda i: (i, 0))]
        out_shape += [jax.ShapeDtypeStruct((T, D), BF16), jax.ShapeDtypeStruct((T, w_o.shape[0]), BF16)]
    return _call(body, after=after, args=args, name="ffn_bwd", grid=(T // tm,), in_specs=in_specs, out_specs=out_specs,
                 out_shape=out_shape)


def wgrad(a, b, *, name, a_cols=0, b_cols=0, group=1, flat=False, tk, out_dtype=BF16, after=None):
    T, K = a.shape
    J = 1
    if a_cols:
        K = a_cols
        J = a.shape[1] // K
        a_spec = pl.BlockSpec((tk, K), lambda j, k: (k, j))
    else:
        a_spec = pl.BlockSpec((tk, K), lambda j, k: (k, 0))
    if b_cols:
        N = b_cols * group
        J = b.shape[1] // N
        b_spec = pl.BlockSpec((tk, N), lambda j, k: (k, j))
    else:
        N = b.shape[1]
        b_spec = pl.BlockSpec((tk, N), lambda j, k: (k, 0))
    nk = T // tk
    if flat:
        o_spec, o_shape = pl.BlockSpec((K, N), lambda j, k: (0, j)), (K, J * N)
    elif group > 1:
        o_spec, o_shape = pl.BlockSpec((group, K, b_cols), lambda j, k: (j, 0, 0)), (J * group, K, b_cols)
    else:
        o_spec, o_shape = pl.BlockSpec((None, K, N), lambda j, k: (j, 0, 0)), (J, K, N)

    def body(a_ref, b_ref, o_ref, acc_ref):
        k = pl.program_id(1)

        @pl.when(k == 0)
        def _():
            acc_ref[...] = jnp.zeros_like(acc_ref)

        acc_ref[...] += _dot_tn(a_ref[...], b_ref[...])

        @pl.when(k == nk - 1)
        def _():
            if group > 1 and not flat:
                for i in range(group):
                    o_ref[i] = acc_ref[:, i * b_cols : (i + 1) * b_cols].astype(out_dtype)
            else:
                o_ref[...] = acc_ref[...].astype(out_dtype)

    outs, _ = _call(
        body,
        after=after,
        name=name,
        grid=(J, nk),
        in_specs=[a_spec, b_spec],
        out_specs=[o_spec],
        out_shape=[jax.ShapeDtypeStruct(o_shape, out_dtype)],
        args=(a, b),
        scratch=[pltpu.VMEM((K, N), F32)],
    )
    return outs[0]


def _seg(xs, lo):
    s_lo = [jnp.sum(jnp.where(lo, x, 0.0), axis=-1, keepdims=True) for x in xs]
    s_hi = [jnp.sum(jnp.where(lo, 0.0, x), axis=-1, keepdims=True) for x in xs]
    return [jnp.where(lo, a, b) for a, b in zip(s_lo, s_hi)]


def _head_norm(xs, gains, lo):
    rs = [lax.rsqrt(s * (1.0 / HEAD_DIM) + EPS) for s in _seg([x * x for x in xs], lo)]
    return [(x * r) * g for x, r, g in zip(xs, rs, gains)], rs


def _head_norm_bwd(xs, rs, gains, dys, lo):
    xns = [x * r for x, r in zip(xs, rs)]
    dxns = [dy * g for dy, g in zip(dys, gains)]
    means = [s * (1.0 / HEAD_DIM) for s in _seg([a * b for a, b in zip(dxns, xns)], lo)]
    dxs = [r * (dxn - xn * m) for r, dxn, xn, m in zip(rs, dxns, xns, means)]
    return dxs, [jnp.sum(dy * xn, axis=0, keepdims=True) for dy, xn in zip(dys, xns)]


def _swap_halves(x):
    return pltpu.roll(x, HEAD_DIM, 1)


def qkv_proj(x, gain, w, qg, kg, *, tm):
    T, D = x.shape
    N = w.shape[1]
    kvw = N_KV_HEADS * HEAD_DIM
    nqt, nkt = D // LANES, kvw // LANES

    def body(x_ref, g_ref, w_ref, qg_ref, kg_ref, qkv_ref, q_ref, kd_ref, vd_ref):
        xt = x_ref[...]
        h = ((xt * _rms(xt)) * g_ref[...]).astype(BF16)
        qkv = _dot(h, w_ref[...])
        qkv_ref[...] = qkv
        lo = lax.broadcasted_iota(jnp.int32, (1, LANES), 1) < HEAD_DIM
        tiles = [qkv[:, t * LANES : (t + 1) * LANES] for t in range(nqt + nkt)]
        normed, _ = _head_norm(tiles, [qg_ref[...]] * nqt + [kg_ref[...]] * nkt, lo)
        for t in range(nqt):
            q_ref[:, t * LANES : (t + 1) * LANES] = (normed[t] * SCALE).astype(BF16)
        for t in range(nkt):
            kn = normed[nqt + t]
            v = qkv[:, D + kvw + t * LANES : D + kvw + (t + 1) * LANES]
            for src, dst in ((kn, kd_ref), (v, vd_ref)):
                sw = _swap_halves(src)
                dst[:, 2 * t * LANES : (2 * t + 1) * LANES] = jnp.where(lo, src, sw).astype(BF16)
                dst[:, (2 * t + 1) * LANES : (2 * t + 2) * LANES] = jnp.where(lo, sw, src).astype(BF16)

    row = lambda i: (i, 0)
    return pl.pallas_call(
        body,
        name="qkv_proj",
        grid=(T // tm,),
        in_specs=[pl.BlockSpec((tm, D), row), _resident((1, D)), _resident((D, N)), _resident((1, LANES)), _resident((1, LANES))],
        out_specs=[pl.BlockSpec((tm, N), row), pl.BlockSpec((tm, D), row), pl.BlockSpec((tm, 2 * kvw), row), pl.BlockSpec((tm, 2 * kvw), row)],
        out_shape=[
            jax.ShapeDtypeStruct((T, N), F32),
            jax.ShapeDtypeStruct((T, D), BF16),
            jax.ShapeDtypeStruct((T, 2 * kvw), BF16),
            jax.ShapeDtypeStruct((T, 2 * kvw), BF16),
        ],
        compiler_params=_params(1),
    )(x, gain, w, qg, kg)


def _attn_tables(sinks, n_q_heads):
    P = n_q_heads // N_KV_HEADS // 2
    h = jnp.arange(1, n_q_heads + 1, dtype=F32)
    slopes = jnp.exp2(-8.0 * h / n_q_heads).reshape(N_KV_HEADS, P, 1, 2, 1)
    qi = jnp.arange(BLOCK)[:, None]
    kj = jnp.arange(BLOCK)[None, :]
    dist = jnp.where(kj <= qi, qi - kj, qi + BLOCK - kj).astype(F32)
    shape = (N_KV_HEADS, P, BLOCK, 2, BLOCK)
    bias = jnp.broadcast_to(-slopes * dist[None, None, :, None, :], shape)
    sink = jnp.broadcast_to(sinks.astype(F32).reshape(N_KV_HEADS, P, 1, 2, 1), shape)
    return bias.reshape(N_KV_HEADS, P * BLOCK, 2 * BLOCK), sink.reshape(N_KV_HEADS, P * BLOCK, 2 * BLOCK)


def _attn_specs(D, nb):
    kvw2 = 2 * N_KV_HEADS * HEAD_DIM
    cur = lambda b, i: (b * nb + i, 0)
    prev = lambda b, i: (jnp.maximum(b * nb + i - 1, 0), 0)
    return [
        pl.BlockSpec((BLOCK, D), cur),
        pl.BlockSpec((BLOCK, kvw2), cur),
        pl.BlockSpec((BLOCK, kvw2), prev),
        pl.BlockSpec((BLOCK, kvw2), cur),
        pl.BlockSpec((BLOCK, kvw2), prev),
    ]


def _attn_operands(kh, P, lo, q_ref, kc_ref, kp_ref, vc_ref, vp_ref):
    sl = slice(kh * LANES, (kh + 1) * LANES)

    def cat(prev_ref, cur_ref):
        d = jnp.concatenate([prev_ref[:, sl], cur_ref[:, sl]], axis=0)
        z = jnp.zeros_like(d)
        return jnp.concatenate([jnp.where(lo, d, z), jnp.where(lo, z, d)], axis=0)

    qt = jnp.concatenate([q_ref[:, (kh * P + pr) * LANES : (kh * P + pr + 1) * LANES] for pr in range(P)], axis=0)
    return qt, cat(kp_ref, kc_ref), cat(vp_ref, vc_ref)


def _attn_exp(s_all, bias, sink, tri, first):
    out = []
    for par in range(2):
        c0 = 2 * par * BLOCK
        s = jnp.where(tri, s_all[:, c0 + BLOCK : c0 + 2 * BLOCK], jnp.where(first, NEG, s_all[:, c0 : c0 + BLOCK]))
        s = s + bias[:, par * BLOCK : (par + 1) * BLOCK]
        snk = sink[:, par * BLOCK : (par + 1) * BLOCK]
        m = jnp.maximum(jnp.max(s, axis=-1, keepdims=True), snk)
        out.append((jnp.exp(s - m), jnp.exp(snk - m)))
    return out


def _unfold(x, tri):
    z = jnp.zeros_like(x)
    return jnp.concatenate([jnp.where(tri, z, x), jnp.where(tri, x, z)], axis=1)


def _attn_masks(R):
    lane = lax.broadcasted_iota(jnp.int32, (1, LANES), 1)
    row = lax.broadcasted_iota(jnp.int32, (R, BLOCK), 0) & (BLOCK - 1)
    col = lax.broadcasted_iota(jnp.int32, (R, BLOCK), 1)
    return lane, lane < HEAD_DIM, col <= row


def attn_fwd(q16, kd, vd, bias, sink, *, seq, n_seq):
    T, D = q16.shape
    nb = seq // BLOCK
    P = D // HEAD_DIM // N_KV_HEADS // 2
    R = P * BLOCK
    KV = range(N_KV_HEADS)

    def body(q_ref, kc_ref, kp_ref, vc_ref, vp_ref, bias_ref, sink_ref, o_ref):
        first = pl.program_id(1) == 0
        _, lo, tri = _attn_masks(R)
        ops = [_attn_operands(kh, P, lo, q_ref, kc_ref, kp_ref, vc_ref, vp_ref) for kh in KV]
        s_all = [_dot_nt(ops[kh][0], ops[kh][1]) for kh in KV]
        ex = [_attn_exp(s_all[kh], bias_ref[kh], sink_ref[kh], tri, first) for kh in KV]
        den = [[jnp.sum(e, axis=-1, keepdims=True) + es for e, es in ex[kh]] for kh in KV]
        lhs = [jnp.concatenate([_unfold(e, tri) for e, _ in ex[kh]], axis=1).astype(BF16) for kh in KV]
        o = [_dot(lhs[kh], ops[kh][2]) for kh in KV]
        for kh in KV:
            out = o[kh] / jnp.where(lo, den[kh][0], den[kh][1])
            for pr in range(P):
                t = kh * P + pr
                o_ref[:, t * LANES : (t + 1) * LANES] = out[pr * BLOCK : (pr + 1) * BLOCK, :].astype(BF16)

    return pl.pallas_call(
        body,
        name="attn_fwd",
        grid=(n_seq, nb),
        in_specs=_attn_specs(D, nb) + [_resident((N_KV_HEADS, R, 2 * BLOCK)), _resident((N_KV_HEADS, R, 2 * BLOCK))],
        out_specs=pl.BlockSpec((BLOCK, D), lambda b, i: (b * nb + i, 0)),
        out_shape=jax.ShapeDtypeStruct((T, D), BF16),
        compiler_params=_params(2),
    )(q16, kd, kd, vd, vd, bias, sink)


def attn_bwd(q16, kd, vd, do, bias, sink, *, seq, n_seq):
    T, D = q16.shape
    kvw2 = 2 * N_KV_HEADS * HEAD_DIM
    nb = seq // BLOCK
    G = D // HEAD_DIM // N_KV_HEADS
    P = G // 2
    R = P * BLOCK
    KV = range(N_KV_HEADS)

    def body(q_ref, kc_ref, kp_ref, vc_ref, vp_ref, do_ref, bias_ref, sink_ref,
             dq_ref, dkc_ref, dkp_ref, dvc_ref, dvp_ref, dsink_ref):
        first = pl.program_id(1) == 0

        @pl.when(jnp.logical_and(pl.program_id(0) == 0, first))
        def _():
            dsink_ref[...] = jnp.zeros_like(dsink_ref)

        lane, lo, tri = _attn_masks(R)
        ops = [_attn_operands(kh, P, lo, q_ref, kc_ref, kp_ref, vc_ref, vp_ref) for kh in KV]
        do16 = [jnp.concatenate([do_ref[:, (kh * P + pr) * LANES : (kh * P + pr + 1) * LANES] for pr in range(P)], axis=0)
                for kh in KV]
        s_all = [_dot_nt(ops[kh][0], ops[kh][1]) for kh in KV]
        dp_all = [_dot_nt(do16[kh], ops[kh][2]) for kh in KV]
        ex = [_attn_exp(s_all[kh], bias_ref[kh], sink_ref[kh], tri, first) for kh in KV]
        den = [[jnp.sum(e, axis=-1, keepdims=True) for e, _ in ex[kh]] for kh in KV]
        dsink = jnp.zeros((1, LANES), F32)
        pf, dsf = [], []
        for kh in KV:
            ps_, ds_ = [], []
            for par in range(2):
                e, es = ex[kh][par]
                inv = 1.0 / (den[kh][par] + es)
                p = e * inv
                c0 = 2 * par * BLOCK
                dp = jnp.where(tri, dp_all[kh][:, c0 + BLOCK : c0 + 2 * BLOCK], dp_all[kh][:, c0 : c0 + BLOCK])
                delta = jnp.sum(p * dp, axis=-1, keepdims=True)
                ds_.append(_unfold(p * (dp - delta), tri))
                ps_.append(_unfold(p, tri))
                dsr = -((es * inv) * delta)
                for pr in range(P):
                    hq = kh * G + 2 * pr + par
                    tot = jnp.sum(dsr[pr * BLOCK : (pr + 1) * BLOCK, :], axis=0, keepdims=True)
                    dsink = dsink + jnp.where(lane == hq, tot, 0.0)
            pf.append(jnp.concatenate(ps_, axis=1).astype(BF16))
            dsf.append(jnp.concatenate(ds_, axis=1).astype(BF16))
        dq = [_dot(dsf[kh], ops[kh][1]) for kh in KV]
        dk = [_dot_tn(dsf[kh], ops[kh][0]) for kh in KV]
        dv = [_dot_tn(pf[kh], do16[kh]) for kh in KV]
        dsink_ref[0:1, :] += dsink
        for kh in KV:
            sl = slice(kh * LANES, (kh + 1) * LANES)
            for pr in range(P):
                t = kh * P + pr
                dq_ref[:, t * LANES : (t + 1) * LANES] = dq[kh][pr * BLOCK : (pr + 1) * BLOCK, :]
            for full, prev_ref, cur_ref in ((dk[kh], dkp_ref, dkc_ref), (dv[kh], dvp_ref, dvc_ref)):
                dup = jnp.where(lo, full[: 2 * BLOCK, :], full[2 * BLOCK :, :])
                prev_ref[:, sl] = dup[:BLOCK, :].astype(BF16)
                cur_ref[:, sl] = dup[BLOCK:, :].astype(BF16)

    cur = lambda b, i: (b * nb + i, 0)
    kv_spec = pl.BlockSpec((BLOCK, kvw2), cur)
    kv_shape = jax.ShapeDtypeStruct((T, kvw2), BF16)
    return pl.pallas_call(
        body,
        name="attn_bwd",
        grid=(n_seq, nb),
        in_specs=_attn_specs(D, nb)
        + [pl.BlockSpec((BLOCK, D), cur), _resident((N_KV_HEADS, R, 2 * BLOCK)), _resident((N_KV_HEADS, R, 2 * BLOCK))],
        out_specs=[pl.BlockSpec((BLOCK, D), cur), kv_spec, kv_spec, kv_spec, kv_spec, pl.BlockSpec((8, LANES), lambda b, i: (0, 0))],
        out_shape=[jax.ShapeDtypeStruct((T, D), F32), kv_shape, kv_shape, kv_shape, kv_shape, jax.ShapeDtypeStruct((8, LANES), F32)],
        compiler_params=_params(2),
    )(q16, kd, kd, vd, vd, do, bias, sink)


def qkv_bwd(dq, dkc, dkp, dvc, dvp, qkv, dres, x, gain, w_qkv, qg, kg, *, seq, tm):
    T, D = x.shape
    kvw2 = dkc.shape[1]
    kvw = kvw2 // 2
    nqt, nkt = D // LANES, kvw // LANES
    nb = seq // BLOCK
    nbt = tm // BLOCK
    assert nb % nbt == 0
    n = T // tm

    def body(dq_ref, dkc_ref, dkpa_ref, dkpb_ref, dvc_ref, dvpa_ref, dvpb_ref, qkv_ref, dres_ref, x_ref, g_ref, w_ref,
             qg_ref, kg_ref, dx_ref, dqkv_ref, h_ref, dg_ref, hg_ref):
        i = pl.program_id(0)

        @pl.when(i == 0)
        def _():
            dg_ref[...] = jnp.zeros_like(dg_ref)
            hg_ref[...] = jnp.zeros_like(hg_ref)

        lo = lax.broadcasted_iota(jnp.int32, (1, LANES), 1) < HEAD_DIM
        last = ((i + 1) * nbt) % nb == 0
        up = lambda ref: ref[...].astype(F32)

        def with_next(cur_ref, own_ref, next_ref):
            return up(cur_ref) + jnp.concatenate([up(own_ref)[BLOCK:, :], jnp.where(last, 0.0, up(next_ref))], axis=0)

        dkd = with_next(dkc_ref, dkpa_ref, dkpb_ref)
        dvd = with_next(dvc_ref, dvpa_ref, dvpb_ref)

        def undup(d, t):
            a, b = d[:, 2 * t * LANES : (2 * t + 1) * LANES], d[:, (2 * t + 1) * LANES : (2 * t + 2) * LANES]
            return jnp.where(lo, a + _swap_halves(a), b + _swap_halves(b))

        tiles = [qkv_ref[:, t * LANES : (t + 1) * LANES] for t in range(nqt + nkt)]
        gains = [qg_ref[...]] * nqt + [kg_ref[...]] * nkt
        dys = [dq_ref[:, t * LANES : (t + 1) * LANES] * SCALE for t in range(nqt)] + [undup(dkd, t) for t in range(nkt)]
        _, rs = _head_norm(tiles, gains, lo)
        dxs, dgs = _head_norm_bwd(tiles, rs, gains, dys, lo)
        for t in range(nqt + nkt):
            dqkv_ref[:, t * LANES : (t + 1) * LANES] = dxs[t].astype(BF16)
        for t in range(nkt):
            dqkv_ref[:, D + kvw + t * LANES : D + kvw + (t + 1) * LANES] = undup(dvd, t).astype(BF16)
        hg_ref[0:1, :] += functools.reduce(lambda a, b: a + b, dgs[:nqt])
        hg_ref[1:2, :] += functools.reduce(lambda a, b: a + b, dgs[nqt:])
        dh = _dot_nt(dqkv_ref[...], w_ref[...])
        xt = x_ref[...]
        r = _rms(xt)
        gn = g_ref[...]
        h_ref[...] = ((xt * r) * gn).astype(BF16)
        dx, dgn = _rms_bwd(xt, r, gn, dh)
        dg_ref[0:1, :] += dgn
        dx_ref[...] = dres_ref[...] + dx

    row = lambda i: (i, 0)
    nxt_a = pl.BlockSpec((tm, kvw2), row)
    nxt_b = pl.BlockSpec((BLOCK, kvw2), lambda i: (jnp.minimum((i + 1) * nbt, n * nbt - 1), 0))
    return pl.pallas_call(
        body,
        name="qkv_bwd",
        grid=(n,),
        in_specs=[
            pl.BlockSpec((tm, D), row),
            pl.BlockSpec((tm, kvw2), row),
            nxt_a,
            nxt_b,
            pl.BlockSpec((tm, kvw2), row),
            nxt_a,
            nxt_b,
            pl.BlockSpec((tm, D + kvw2), row),
            pl.BlockSpec((tm, D), row),
            pl.BlockSpec((tm, D), row),
            _resident((1, D)),
            _resident((D, D + kvw2)),
            _resident((1, LANES)),
            _resident((1, LANES)),
        ],
        out_specs=[
            pl.BlockSpec((tm, D), row),
            pl.BlockSpec((tm, D + kvw2), row),
            pl.BlockSpec((tm, D), row),
            pl.BlockSpec((8, D), lambda i: (0, 0)),
            pl.BlockSpec((8, LANES), lambda i: (0, 0)),
        ],
        out_shape=[
            jax.ShapeDtypeStruct((T, D), F32),
            jax.ShapeDtypeStruct((T, D + kvw2), BF16),
            jax.ShapeDtypeStruct((T, D), BF16),
            jax.ShapeDtypeStruct((8, D), F32),
            jax.ShapeDtypeStruct((8, LANES), F32),
        ],
        compiler_params=_params(1),
    )(dq, dkc, dkp, dkp, dvc, dvp, dvp, qkv, dres, x, gain, w_qkv, qg, kg)


def local_step(x, target, gains, w, *, seq, tm=256, tm_ffn=256, tm_conv=512, tk=2048, shards=None, ex=None):
    T, D = x.shape
    n_seq = T // seq
    nm, nf, qgain, kgain, sinks = gains
    H = D // HEAD_DIM
    tk, tk_long = min(tk, T), min(2 * tk, T)
    qg2, kg2 = jnp.tile(qgain, (1, 2)), jnp.tile(kgain, (1, 2))
    bias, sinkcol = _attn_tables(sinks, H)

    dist = shards is not None
    w = dict(w)

    plan = _Gather([shards["w_gu"][0], shards["w_d"][0]]) if dist else None
    (x1, bcx, y_conv, z16), got = conv_fwd(x, nm[0:1], w["w_in"], w["cw"], w["w_out"], seq=seq, tm=tm_conv, plan=plan)
    if dist:
        w["w_gu"], w["w_d"] = [cols_from_shards(got[0]), None], [got[1].reshape(-1, D), None]
    plan = _Gather([shards["w_qkv"], shards["w_o"], shards["w_gu"][1], shards["w_d"][1]]) if dist else None
    (x2, gu0), got = ffn_fwd(x1, nf[0:1], w["w_gu"][0], w["w_d"][0], tm=2 * tm_ffn, plan=plan)
    if dist:
        w["w_qkv"], w["w_o"] = cols_from_shards(got[0]), got[1].reshape(D, D)
        w["w_gu"][1], w["w_d"][1] = cols_from_shards(got[2]), got[3].reshape(-1, D)
    qkv, q16, kd, vd = qkv_proj(x2, nm[1:2], w["w_qkv"], qg2, kg2, tm=tm_conv)
    ao = attn_fwd(q16, kd, vd, bias, sinkcol, seq=seq, n_seq=n_seq)
    (x3, dx4, gu1, sse), _ = ffn_fwd(x2, nf[1:2], w["w_gu"][1], w["w_d"][1], tm=tm_ffn, attn=(ao, w["w_o"]), target=target)

    by_dest = lambda a: a.reshape(N_DEV, -1, a.shape[-1])
    gu_cols = 2 * MXU_TILE

    def send(name, *entries):
        if ex is None:
            return None
        items = [(a, False, key, (N_DEV,) + (() if layers is None else (layers,)) + a.shape[1:], layer)
                 for a, key, layer, layers in entries]
        return ex.start(items, name=name)

    (dx3, a16, dgu, h16, d16, dnf1, dx3_16, dao), _ = ffn_bwd(
        dx4, x3, nf[1:2], gu1, w["w_gu"][1], w["w_d"][1], tm=tm, w_o=w["w_o"])
    g_gu1 = shards_from_cols(wgrad(h16, dgu, name="wgrad_gu1", b_cols=gu_cols, flat=True, tk=T))
    g_d1 = by_dest(wgrad(a16, d16, name="wgrad_d1", a_cols=a16.shape[1] // 2, tk=tk))
    g_o = by_dest(wgrad(ao, dx3_16, name="wgrad_o", tk=tk_long))
    dq, dkc, dkp, dvc, dvp, dsinks = attn_bwd(q16, kd, vd, dao, bias, sinkcol, seq=seq, n_seq=n_seq)
    dx2, dqkv16, h16, dnm1, dgains = qkv_bwd(dq, dkc, dkp, dvc, dvp, qkv, dx3, x2, nm[1:2], w["w_qkv"], qg2, kg2, seq=seq, tm=tm_conv)
    g_qkv = shards_from_cols(wgrad(h16, dqkv16, name="wgrad_qkv", tk=tk)[0])
    tok = send("exchange_layer1", (g_gu1, "w_gu", 1, 2), (g_d1, "w_d", 1, 2), (g_o, "w_o", None, None),
               (g_qkv, "w_qkv", None, None))
    (dx1, a16, dgu, h16, d16, dnf0), _ = ffn_bwd(dx2, x1, nf[0:1], gu0, w["w_gu"][0], w["w_d"][0], tm=tm, after=tok)
    g_gu0 = shards_from_cols(wgrad(h16, dgu, name="wgrad_gu0", b_cols=gu_cols, flat=True, tk=T))
    tok = send("exchange_gu0", (g_gu0, "w_gu", 0, 2))
    g_d0 = by_dest(wgrad(a16, d16, name="wgrad_d0", a_cols=a16.shape[1] // 2, tk=tk, after=tok))
    tok = send("exchange_d0", (g_d0, "w_d", 0, 2))
    (gx, dbcx, h16, d16, dcw, dnm0), _ = conv_bwd(
        dx1, x, nm[0:1], bcx, y_conv, w["cw"], w["w_in"], w["w_out"], seq=seq, tm=tm_conv, after=tok)
    g_out = by_dest(wgrad(z16, d16, name="wgrad_out", tk=tk_long))
    g_cw = dcw[0:3].reshape(3, N_DEV, D // N_DEV).transpose(1, 0, 2)
    tok = send("exchange_out", (g_out, "w_out", None, None), (g_cw, "cw", None, None))
    g_in = wgrad(h16, dbcx, name="wgrad_in", b_cols=3 * D // N_DEV, group=2, tk=tk_long, after=tok)
    g = dict(w_in=g_in, cw=g_cw, w_out=g_out, w_o=g_o, w_qkv=g_qkv, w_gu=[g_gu0, g_gu1], w_d=[g_d0, g_d1])
    small = dict(nm0=dnm0, nm1=dnm1, nf0=dnf0, nf1=dnf1, gains=dgains, sinks=dsinks)
    return sse, gx, g, small


def _adamw_math(g, w, m, v):
    m = ADAM_B1 * m + (1.0 - ADAM_B1) * g
    v = ADAM_B2 * v + (1.0 - ADAM_B2) * (g * g)
    m_hat = m / (1.0 - ADAM_B1 ** ADAM_STEP)
    v_hat = v / (1.0 - ADAM_B2 ** ADAM_STEP)
    delta = -ADAM_LR * (m_hat / (jnp.sqrt(v_hat) + ADAM_EPS) + ADAM_WD * w)
    return delta, m, v


def adamw(parts, owns, w, m, v, *, name, after=None):
    n, LR, C = parts.shape
    L = len(owns)
    R = LR // L
    tr = R
    for cand in (256, 128, 88, 64, 32, 16, 8):
        if R > cand and R % cand == 0:
            tr = cand
            break
    per_layer = R // tr
    extra = [] if after is None else [after]

    def body(me_ref, p_ref, *rest):
        own_refs, (w_ref, m_ref, v_ref) = rest[:L], rest[L : L + 3]
        g_ref, d_ref, mo_ref, vo_ref = rest[L + 3 + len(extra) :]
        layer = pl.program_id(0) // per_layer
        mine = own_refs[0][...].astype(F32)
        for j in range(1, L):
            mine = jnp.where(layer == j, own_refs[j][...].astype(F32), mine)
        g = None
        for s in range(n):
            share = jnp.where(me_ref[0] == s, mine, p_ref[s].astype(F32))
            g = share if g is None else g + share
        g_ref[...] = g
        d_ref[...], mo_ref[...], vo_ref[...] = _adamw_math(g, w_ref[...], m_ref[...], v_ref[...])

    blk = pl.BlockSpec((tr, C), lambda i, me: (i, 0))
    own_specs = [pl.BlockSpec((None, tr, C), lambda i, me: (me[0], i % per_layer, 0)) if o.ndim == 3
                 else pl.BlockSpec((tr, C), lambda i, me: (i % per_layer, 0)) for o in owns]
    me = (4 * lax.axis_index("x") + 2 * lax.axis_index("y") + lax.axis_index("c")).astype(jnp.int32).reshape(1)
    return pl.pallas_call(
        body,
        name=name,
        grid_spec=pltpu.PrefetchScalarGridSpec(
            num_scalar_prefetch=1,
            grid=(LR // tr,),
            in_specs=[pl.BlockSpec((n, tr, C), lambda i, me: (0, i, 0))] + own_specs + [blk, blk, blk] + _any_specs(len(extra)),
            out_specs=[blk] * 4,
        ),
        out_shape=[jax.ShapeDtypeStruct((LR, C), F32)] * 4,
        compiler_params=_params(1),
    )(me, parts, *owns, w, m, v, *extra)


def pack_small(small, sse, D):
    W = max(D, 2 * LANES)

    def body(nm0, nm1, nf0, nf1, gains, sinks, sse_ref, o_ref):
        o_ref[...] = jnp.zeros_like(o_ref)
        o_ref[0:1, :D] = nm0[0:1, :]
        o_ref[1:2, :D] = nm1[0:1, :]
        o_ref[2:3, :D] = nf0[0:1, :]
        o_ref[3:4, :D] = nf1[0:1, :]
        gq = gains[0:1, :] + pltpu.roll(gains[0:1, :], HEAD_DIM, 1)
        gk = gains[1:2, :] + pltpu.roll(gains[1:2, :], HEAD_DIM, 1)
        lane = lax.broadcasted_iota(jnp.int32, (1, LANES), 1)
        o_ref[4:5, :LANES] = jnp.where(lane < HEAD_DIM, gq, gk)
        o_ref[4:5, LANES : 2 * LANES] = sinks[0:1, :]
        o_ref[5:6, :LANES] = sse_ref[0:1, :] * (0.5 / D)

    return pl.pallas_call(
        body,
        name="pack_small",
        out_shape=jax.ShapeDtypeStruct((8, W), F32),
    )(small["nm0"], small["nm1"], small["nf0"], small["nf1"], small["gains"], small["sinks"], sse)


def _pack_small_params(nm, nf, qg, kg, sk, D):
    W = max(D, 2 * LANES)
    row4 = jnp.concatenate([qg.reshape(-1), kg.reshape(-1), jnp.zeros((LANES - 2 * HEAD_DIM,), F32), sk.reshape(-1)])
    row4 = jnp.pad(row4, (0, W - row4.shape[0]))
    rows = [jnp.pad(r, (0, W - D)) for r in (nm[0], nm[1], nf[0], nf[1])] + [row4]
    return jnp.concatenate([jnp.stack(rows), jnp.zeros((3, W), F32)], axis=0)


def _unpack_small(a, D, H):
    nm = a[0:2, :D]
    nf = a[2:4, :D]
    qg = a[4:5, 0:HEAD_DIM]
    kg = a[4:5, HEAD_DIM : 2 * HEAD_DIM]
    sk = a[4:5, LANES : LANES + H]
    return qg, kg, sk, nm, nf


def kernel(x, conv_w_in, conv_w, conv_w_out, attn_w_qkv, attn_q_gain, attn_k_gain, attn_sinks, attn_w_o, norm_mixer, norm_ffn, ffn_w_gate_up, ffn_w_down, loss_target, m_conv_w_in, m_conv_w, m_conv_w_out, m_attn_w_qkv, m_attn_q_gain, m_attn_k_gain, m_attn_sinks, m_attn_w_o, m_norm_mixer, m_norm_ffn, m_ffn_w_gate_up, m_ffn_w_down, v_conv_w_in, v_conv_w, v_conv_w_out, v_attn_w_qkv, v_attn_q_gain, v_attn_k_gain, v_attn_sinks, v_attn_w_o, v_norm_mixer, v_norm_ffn, v_ffn_w_gate_up, v_ffn_w_down):
    n_seq, seq, D = x.shape
    T = n_seq * seq
    H = D // HEAD_DIM
    L = ffn_w_gate_up.shape[0]

    full = run_plan(_Gather([conv_w_in[0].astype(BF16), conv_w[0], conv_w_out[0].astype(BF16)]), name="gather_conv_weights")
    w = dict(w_in=cols_from_shards(full[0]), cw=full[1].transpose(1, 0, 2).reshape(3, D),
             w_out=full[2].reshape(D, D))
    shards = dict(w_gu=[ffn_w_gate_up[l].astype(BF16) for l in range(L)], w_d=[ffn_w_down[l].astype(BF16) for l in range(L)],
                  w_qkv=attn_w_qkv[0].astype(BF16), w_o=attn_w_o[0].astype(BF16))
    gains = (norm_mixer, norm_ffn, attn_q_gain, attn_k_gain, attn_sinks)
    ex = Exchange()
    sse, gx, g, small = local_step(x.reshape(T, D), loss_target.reshape(T, D), gains, w, seq=seq, shards=shards, ex=ex)
    zones, own = ex.wait([g["w_in"]], name="exchange_wait")

    packed = pack_small(small, sse, D)
    token = ex.start([(g["w_in"], False, "w_in", g["w_in"].shape, None),
                      (packed, True, "small", (N_DEV,) + packed.shape, None)], name="exchange_last")

    def flat(a):
        return a.reshape(-1, a.shape[-1])

    big = [conv_w_in, conv_w, conv_w_out, attn_w_qkv, attn_w_o, ffn_w_gate_up, ffn_w_down]
    big_m = [m_conv_w_in, m_conv_w, m_conv_w_out, m_attn_w_qkv, m_attn_w_o, m_ffn_w_gate_up, m_ffn_w_down]
    big_v = [v_conv_w_in, v_conv_w, v_conv_w_out, v_attn_w_qkv, v_attn_w_o, v_ffn_w_gate_up, v_ffn_w_down]
    keys = ["w_in", "cw", "w_out", "w_qkv", "w_o", "w_gu", "w_d"]

    def update(b, zones, own, after=None):
        zone = zones[keys[b]]
        parts = zone.reshape(N_DEV, -1, zone.shape[-1])
        layers = [None] if zone.ndim == 3 else range(zone.shape[1])
        outs = adamw(parts, [own[(keys[b], l)] for l in layers], flat(big[b]), flat(big_m[b]), flat(big_v[b]),
                     name="adamw_" + keys[b], after=after)
        return [o.reshape(big[b].shape) for o in outs]

    res = [None] + [update(b, zones, own, after=token) for b in range(1, 7)]
    zones, own = ex.wait([r[0] for r in res[1:]], name="exchange_last_wait")
    res[0] = update(0, zones, own)
    sw = _pack_small_params(norm_mixer, norm_ffn, attn_q_gain, attn_k_gain, attn_sinks, D)
    sm = _pack_small_params(m_norm_mixer, m_norm_ffn, m_attn_q_gain, m_attn_k_gain, m_attn_sinks, D)
    sv = _pack_small_params(v_norm_mixer, v_norm_ffn, v_attn_q_gain, v_attn_k_gain, v_attn_sinks, D)
    souts = adamw(zones["small"], [own[("small", None)]], sw, sm, sv, name="adamw_small")
    sres = [_unpack_small(o, D, H) for o in souts]
    loss = souts[0][5, 0]

    def ordered(i):
        r, s = [r[i] for r in res], sres[i]
        return [r[0], r[1], r[2], r[3], s[0], s[1], s[2], r[4], s[3], s[4], r[5], r[6]]

    return (loss, gx.reshape(n_seq, seq, D), *ordered(0), *ordered(1), *ordered(2), *ordered(3))
```
